```python
import math
import jax, jax.numpy as jnp
from jax import lax
import numpy as np

D_MODEL = 1024
BATCH = 4
SEQ = 4096
DEPTH = 2

PLE_DIM = 256
MOBA_HEADS = 8
MOBA_HEAD_DIM = 64
MOBA_WIDTH = MOBA_HEADS * MOBA_HEAD_DIM
MOBA_BLOCK = 256
MOBA_TOPK = 3
MOBA_QCHUNK = 32
MLSTM_HEADS = 4
MLSTM_QK_DIM = 128
MLSTM_V_DIM = 128
MLSTM_QK_WIDTH = MLSTM_HEADS * MLSTM_QK_DIM
MLSTM_V_WIDTH = MLSTM_HEADS * MLSTM_V_DIM
MLSTM_CHUNK = 64
CONV_WIDTH = 4
D_FF = 2816
N_EXPERTS = 8
TOP_K = 2
D_FF_EXPERT = 3584
MOE_BLOCK = 512
N_DENSE = (DEPTH + 1) // 2
N_MOE = DEPTH // 2
RMS_EPS = 1e-6

IN_SPLITS = (MOBA_WIDTH, MOBA_WIDTH, MOBA_WIDTH, 2 * MLSTM_QK_WIDTH, MLSTM_V_WIDTH,
             MLSTM_V_WIDTH, MLSTM_HEADS, MLSTM_HEADS, D_MODEL, D_MODEL)
N_IN = sum(IN_SPLITS)

kernel_name = "hybrid_moba_mlstm_moe_block"


def rms_norm(x, g):
    xf = x.astype(jnp.float32)
    y = xf * lax.rsqrt(jnp.mean(xf * xf, axis=-1, keepdims=True) + RMS_EPS)
    return (y * g.astype(jnp.float32)).astype(x.dtype)


def split_heads(t, n_heads):
    b, s, _ = t.shape
    return t.reshape(b, s, n_heads, -1).transpose(0, 2, 1, 3)


def merge_heads(t):
    b, h, s, d = t.shape
    return t.transpose(0, 2, 1, 3).reshape(b, s, h * d)


def causal_depthwise_conv(x, w):
    c = x.shape[-1]
    return lax.conv_general_dilated(
        x, w[:, None, :].astype(x.dtype), window_strides=(1,),
        padding=[(CONV_WIDTH - 1, 0)], dimension_numbers=("NWC", "WIO", "NWC"),
        feature_group_count=c)


def moba_attention(q, k, v):
    b, h, s, dh = q.shape
    nb = -(-s // MOBA_BLOCK)
    s_pad = nb * MOBA_BLOCK
    padw = ((0, 0), (0, 0), (0, s_pad - s), (0, 0))
    q, k, v = jnp.pad(q, padw), jnp.pad(k, padw), jnp.pad(v, padw)
    k_blocks = k.reshape(b, h, nb, MOBA_BLOCK, dh)
    v_blocks = v.reshape(b, h, nb, MOBA_BLOCK, dh)
    k_mean = jnp.mean(k_blocks.astype(jnp.float32), axis=3).astype(k.dtype)
    n_sel = min(MOBA_TOPK, nb)
    scale = dh ** -0.5
    bi = jnp.arange(b)[:, None, None, None]
    hi = jnp.arange(h)[None, :, None, None]

    def chunk_fn(c):
        q0 = c * MOBA_QCHUNK
        blk = q0 // MOBA_BLOCK
        qc = lax.dynamic_slice_in_dim(q, q0, MOBA_QCHUNK, axis=2)
        gate = jnp.einsum('bhqd,bhnd->bhqn', qc, k_mean).astype(jnp.float32)
        gate = jnp.where(jnp.arange(nb) < blk, gate, -jnp.inf)
        _, sel = lax.top_k(gate, n_sel)
        sel_ok = jnp.arange(n_sel) < blk
        kg = k_blocks[bi, hi, sel]
        vg = v_blocks[bi, hi, sel]
        s_sel = jnp.einsum('bhqd,bhqnkd->bhqnk', qc, kg).astype(jnp.float32) * scale
        s_sel = jnp.where(sel_ok[:, None], s_sel, -jnp.inf)
        k_own = lax.dynamic_slice_in_dim(k, blk * MOBA_BLOCK, MOBA_BLOCK, axis=2)
        v_own = lax.dynamic_slice_in_dim(v, blk * MOBA_BLOCK, MOBA_BLOCK, axis=2)
        s_own = jnp.einsum('bhqd,bhkd->bhqk', qc, k_own).astype(jnp.float32) * scale
        causal = (blk * MOBA_BLOCK + jnp.arange(MOBA_BLOCK))[None, :] <= (q0 + jnp.arange(MOBA_QCHUNK))[:, None]
        s_own = jnp.where(causal, s_own, -jnp.inf)
        n_g = n_sel * MOBA_BLOCK
        s_all = jnp.concatenate([s_sel.reshape(b, h, MOBA_QCHUNK, n_g), s_own], axis=-1)
        pr = jax.nn.softmax(s_all, axis=-1).astype(v.dtype)
        p_sel = pr[..., :n_g].reshape(b, h, MOBA_QCHUNK, n_sel, MOBA_BLOCK)
        return (jnp.einsum('bhqnk,bhqnkd->bhqd', p_sel, vg)
                + jnp.einsum('bhqk,bhkd->bhqd', pr[..., n_g:], v_own))

    outs = lax.map(chunk_fn, jnp.arange(s_pad // MOBA_QCHUNK))
    out = outs.transpose(1, 2, 0, 3, 4).reshape(b, h, s_pad, dh)
    return out[:, :, :s]


def mlstm_chunkwise(q, k, v, i_pre, f_pre):
    b, h, s, dk = q.shape
    dv = v.shape[-1]
    L = MLSTM_CHUNK
    nc = s // L
    f32 = jnp.float32
    q = q.astype(f32)
    k = k.astype(f32) * (dk ** -0.5)
    v = v.astype(f32)
    logf = jax.nn.log_sigmoid(f_pre.astype(f32))
    ig = i_pre.astype(f32)

    def to_chunks(t):
        return jnp.moveaxis(t.reshape(b, h, nc, L, *t.shape[3:]), 2, 0)

    xs = (to_chunks(q), to_chunks(k), to_chunks(v), to_chunks(ig), to_chunks(logf))
    tril = jnp.tril(jnp.ones((L, L), dtype=bool))

    def step(carry, inp):
        C, n, m = carry
        qc, kc, vc, ic, lfc = inp
        bcum = jnp.cumsum(lfc, axis=-1)
        D = bcum[..., :, None] - bcum[..., None, :] + ic[..., None, :]
        D = jnp.where(tril, D, -jnp.inf)
        a = bcum + m[..., None]
        m_t = jnp.maximum(a, jnp.max(D, axis=-1))
        Dw = jnp.exp(D - m_t[..., None])
        aw = jnp.exp(a - m_t)
        sqk = jnp.einsum('bhtd,bhsd->bhts', qc, kc) * Dw
        num = aw[..., None] * jnp.einsum('bhtd,bhde->bhte', qc, C) + jnp.einsum('bhts,bhse->bhte', sqk, vc)
        den = aw * jnp.einsum('bhtd,bhd->bht', qc, n) + jnp.sum(sqk, axis=-1)
        hc = num / jnp.maximum(jnp.abs(den), jnp.exp(-m_t))[..., None]
        bL = bcum[..., -1]
        g = bL[..., None] - bcum + ic
        m_new = jnp.maximum(bL + m, jnp.max(g, axis=-1))
        wC = jnp.exp(bL + m - m_new)
        ws = jnp.exp(g - m_new[..., None])
        C_new = wC[..., None, None] * C + jnp.einsum('bhs,bhsd,bhse->bhde', ws, kc, vc)
        n_new = wC[..., None] * n + jnp.einsum('bhs,bhsd->bhd', ws, kc)
        return (C_new, n_new, m_new), hc

    init = (jnp.zeros((b, h, dk, dv), f32), jnp.zeros((b, h, dk), f32), jnp.zeros((b, h), f32))
    _, hs = lax.scan(step, init, xs)
    return jnp.moveaxis(hs, 0, 2).reshape(b, h, s, dv)


def swiglu(x, w1, w3, w2):
    return (jax.nn.silu(x @ w1) * (x @ w3)) @ w2


def moe_swiglu(xt, w_router, w1, w3, w2):
    t = xt.shape[0]
    logits = (xt @ w_router).astype(jnp.float32)
    top_val, top_idx = lax.top_k(logits, TOP_K)
    gates = jax.nn.softmax(top_val, axis=-1)
    a = t * TOP_K
    e_flat = top_idx.reshape(a).astype(jnp.int32)
    g_flat = gates.reshape(a).astype(xt.dtype)
    t_flat = jnp.repeat(jnp.arange(t, dtype=jnp.int32), TOP_K)
    order = jnp.argsort(e_flat)
    e_s, t_s, g_s = e_flat[order], t_flat[order], g_flat[order]
    counts = jnp.bincount(e_flat, length=N_EXPERTS)
    start = jnp.cumsum(counts) - counts
    padded = ((counts + MOE_BLOCK - 1) // MOE_BLOCK) * MOE_BLOCK
    pad_end = jnp.cumsum(padded)
    pad_start = pad_end - padded
    dest = pad_start[e_s] + (jnp.arange(a, dtype=jnp.int32) - start[e_s])
    n_rows = (-(-a // MOE_BLOCK) + N_EXPERTS) * MOE_BLOCK
    n_blk = n_rows // MOE_BLOCK
    row_tok = jnp.zeros((n_rows,), jnp.int32).at[dest].set(t_s)
    row_w = jnp.zeros((n_rows,), xt.dtype).at[dest].set(g_s)
    blk_start = jnp.arange(n_blk) * MOE_BLOCK
    blk_exp = jnp.minimum(jnp.sum(blk_start[:, None] >= pad_end[None, :], axis=1), N_EXPERTS - 1)

    def blk_fn(args):
        tok, w, e = args
        xb = xt[tok]
        return swiglu(xb, w1[e], w3[e], w2[e]) * w[:, None]

    y_rows = lax.map(blk_fn, (row_tok.reshape(n_blk, MOE_BLOCK), row_w.reshape(n_blk, MOE_BLOCK), blk_exp))
    return jax.ops.segment_sum(y_rows.reshape(n_rows, -1), row_tok, num_segments=t)


def setup_inputs(seed: int = 0) -> dict:
    key = jax.random.key(seed)
    ks = jax.random.split(key, 26)
    f32 = jnp.float32

    def nrm(k, shape, fan_in):
        return jax.random.normal(k, shape, f32) * (fan_in ** -0.5)

    def gain(k, shape):
        return 1.0 + 0.05 * jax.random.normal(k, shape, f32)

    return {
        "x": jax.random.normal(ks[0], (BATCH, SEQ, D_MODEL), f32),
        "p": jax.random.normal(ks[1], (DEPTH, BATCH, SEQ, PLE_DIM), f32),
        "g_mix": gain(ks[2], (DEPTH, D_MODEL)),
        "w_in": nrm(ks[3], (DEPTH, D_MODEL, N_IN), D_MODEL),
        "g_q": gain(ks[4], (DEPTH, MOBA_HEAD_DIM)),
        "g_k": gain(ks[5], (DEPTH, MOBA_HEAD_DIM)),
        "conv_w": nrm(ks[6], (DEPTH, CONV_WIDTH, 2 * MLSTM_QK_WIDTH), CONV_WIDTH),
        "conv_b": 0.02 * jax.random.normal(ks[7], (DEPTH, 2 * MLSTM_QK_WIDTH), f32),
        "b_i": 0.1 * jax.random.normal(ks[8], (DEPTH, MLSTM_HEADS), f32),
        "b_f": 3.0 + 0.5 * jax.random.normal(ks[9], (DEPTH, MLSTM_HEADS), f32),
        "g_h": gain(ks[10], (DEPTH, MLSTM_V_DIM)),
        "w_oa": nrm(ks[11], (DEPTH, MOBA_WIDTH, D_MODEL), MOBA_WIDTH),
        "w_ob": nrm(ks[12], (DEPTH, MLSTM_V_WIDTH, D_MODEL), MLSTM_V_WIDTH),
        "w_out": nrm(ks[13], (DEPTH, D_MODEL, D_MODEL), D_MODEL),
        "g_ffn": gain(ks[14], (DEPTH, D_MODEL)),
        "w_d1": nrm(ks[15], (N_DENSE, D_MODEL, D_FF), D_MODEL),
        "w_d3": nrm(ks[16], (N_DENSE, D_MODEL, D_FF), D_MODEL),
        "w_d2": nrm(ks[17], (N_DENSE, D_FF, D_MODEL), D_FF),
        "w_router": nrm(ks[18], (N_MOE, D_MODEL, N_EXPERTS), D_MODEL),
        "w_e1": nrm(ks[19], (N_MOE, N_EXPERTS, D_MODEL, D_FF_EXPERT), D_MODEL),
        "w_e3": nrm(ks[20], (N_MOE, N_EXPERTS, D_MODEL, D_FF_EXPERT), D_MODEL),
        "w_e2": nrm(ks[21], (N_MOE, N_EXPERTS, D_FF_EXPERT, D_MODEL), D_FF_EXPERT),
        "g_ple": gain(ks[22], (DEPTH, D_MODEL)),
        "w_ple_gate": nrm(ks[23], (DEPTH, D_MODEL, D_MODEL), D_MODEL),
        "w_ple_proj": nrm(ks[24], (DEPTH, PLE_DIM, D_MODEL), PLE_DIM),
    }


def reference(x, p, g_mix, w_in, g_q, g_k, conv_w, conv_b, b_i, b_f, g_h, w_oa, w_ob, w_out,
              g_ffn, w_d1, w_d3, w_d2, w_router, w_e1, w_e3, w_e2, g_ple, w_ple_gate, w_ple_proj):
    b, s, d = x.shape
    cuts = np.cumsum(IN_SPLITS)[:-1].tolist()
    for l in range(DEPTH):
        h = rms_norm(x, g_mix[l])
        proj = h @ w_in[l]
        qa, ka, va, qk_m, v_m, o_m, i_m, f_m, ga, gb = jnp.split(proj, cuts, axis=-1)
        qa = rms_norm(split_heads(qa, MOBA_HEADS), g_q[l])
        ka = rms_norm(split_heads(ka, MOBA_HEADS), g_k[l])
        ya = merge_heads(moba_attention(qa, ka, split_heads(va, MOBA_HEADS)))
        qk_m = jax.nn.silu(causal_depthwise_conv(qk_m, conv_w[l]) + conv_b[l])
        q_m, k_m = jnp.split(qk_m, 2, axis=-1)
        i_pre = (i_m + b_i[l]).transpose(0, 2, 1)
        f_pre = (f_m + b_f[l]).transpose(0, 2, 1)
        hm = mlstm_chunkwise(split_heads(q_m, MLSTM_HEADS), split_heads(k_m, MLSTM_HEADS),
                             split_heads(v_m, MLSTM_HEADS), i_pre, f_pre).astype(x.dtype)
        hm = rms_norm(hm, g_h[l]) * jax.nn.sigmoid(split_heads(o_m, MLSTM_HEADS))
        yb = merge_heads(hm)
        mixed = jax.nn.sigmoid(ga) * (ya @ w_oa[l]) + jax.nn.sigmoid(gb) * (yb @ w_ob[l])
        x = x + mixed @ w_out[l]
        hf = rms_norm(x, g_ffn[l])
        j = l // 2
        if l % 2 == 0:
            ffn = swiglu(hf, w_d1[j], w_d3[j], w_d2[j])
        else:
            ffn = moe_swiglu(hf.reshape(b * s, d), w_router[j], w_e1[j], w_e3[j], w_e2[j]).reshape(b, s, d)
        x = x + ffn
        gate = jax.nn.sigmoid(rms_norm(x, g_ple[l]) @ w_ple_gate[l])
        x = x + gate * (p[l] @ w_ple_proj[l])
    return x
```

```python
import functools

import jax
import jax.numpy as jnp
from jax import lax
from jax.experimental import pallas as pl
from jax.experimental.pallas import tpu as pltpu

F32 = jnp.float32
BF16 = jnp.bfloat16
HIGHEST = lax.Precision.HIGHEST

RMS_EPS = 1e-6
LANES = 128
SUBLANES = 8

MOBA_HEADS = 8
MOBA_HEAD_DIM = 64
MOBA_BLOCK = 256
MOBA_TOPK = 3
MLSTM_HEADS = 4
MLSTM_DIM = 128
CONV_WIDTH = 4
N_EXPERTS = 8

COL_GA, COL_GB = 0, 8
COL_QA, COL_KA, COL_VA = 16, 20, 24
COL_QM, COL_KM, COL_VM, COL_OM = 28, 32, 36, 40
N_PROJ = 44 * LANES

VMEM_LIMIT = 48 * 1024 * 1024


def _params(*sem):
    return pltpu.CompilerParams(dimension_semantics=sem, vmem_limit_bytes=VMEM_LIMIT)


def _sigmoid(x):
    return 1.0 / (1.0 + jnp.exp(-x))


def _rms(x, g):
    return x * lax.rsqrt(jnp.mean(x * x, axis=-1, keepdims=True) + RMS_EPS) * g


def _nt_dot(a, b, **kw):
    return lax.dot_general(a, b, (((1,), (1,)), ((), ())), preferred_element_type=F32, **kw)


def _in_proj_body(x_ref, g_ref, w_ref, wif_ref, o_ref, oif_ref, h_ref):
    @pl.when(pl.program_id(1) == 0)
    def _():
        hf = _rms(x_ref[...], g_ref[...])
        h_ref[...] = hf.astype(BF16)
        oif_ref[...] = jnp.dot(hf, wif_ref[...], precision=HIGHEST, preferred_element_type=F32)

    o_ref[...] = jnp.dot(h_ref[...], w_ref[...], preferred_element_type=F32).astype(o_ref.dtype)


def in_proj(x2, g, w, wif, *, tm, tn):
    t, d = x2.shape
    n = w.shape[1]
    return pl.pallas_call(
        _in_proj_body,
        grid=(t // tm, n // tn),
        in_specs=[
            pl.BlockSpec((tm, d), lambda m, j: (m, 0)),
            pl.BlockSpec((1, d), lambda m, j: (0, 0)),
            pl.BlockSpec((d, tn), lambda m, j: (0, j)),
            pl.BlockSpec((d, LANES), lambda m, j: (0, 0)),
        ],
        out_specs=[
            pl.BlockSpec((tm, tn), lambda m, j: (m, j)),
            pl.BlockSpec((tm, LANES), lambda m, j: (m, 0)),
        ],
        out_shape=[jax.ShapeDtypeStruct((t, n), BF16), jax.ShapeDtypeStruct((t, LANES), F32)],
        scratch_shapes=[pltpu.VMEM((tm, d), BF16)],
        compiler_params=_params("arbitrary", "arbitrary"),
        name="in_proj",
    )(x2, g, w, wif)


def _moba_body(q_ref, k_ref, v_ref, gq_ref, gk_ref, o_ref, kn_ref, vt_ref, kmean_ref, sel_ref,
               *, nb, blk, dh, topk):
    i = pl.program_id(2)
    lane = lax.broadcasted_iota(jnp.int32, (1, 2 * dh), 1)
    head0 = lane < dh

    def head_rms(x, g):
        x2 = x * x
        s0 = jnp.sum(jnp.where(head0, x2, 0.0), axis=-1, keepdims=True)
        s1 = jnp.sum(jnp.where(head0, 0.0, x2), axis=-1, keepdims=True)
        ms = jnp.where(head0, s0, s1) * (1.0 / dh)
        return x * lax.rsqrt(ms + RMS_EPS) * g

    @pl.when(i == 0)
    def _():
        def prep(j, c):
            r0 = pl.multiple_of(j * blk, blk)
            kn = head_rms(k_ref[pl.ds(r0, blk), :].astype(F32), gk_ref[...])
            kn_ref[pl.ds(r0, blk), :] = kn.astype(BF16)
            kmean_ref[pl.ds(j, 1), :] = jnp.mean(kn, axis=0, keepdims=True)
            vt_ref[:, pl.ds(r0, blk)] = v_ref[pl.ds(r0, blk), :].astype(F32).T.astype(BF16)
            return c

        lax.fori_loop(0, nb, prep, 0)

    qn = head_rms(q_ref[...].astype(F32), gq_ref[...])
    jidx = lax.broadcasted_iota(jnp.int32, (nb, blk), 0)
    scale = dh ** -0.5
    qs = []
    for h in range(2):
        qh = jnp.where(head0 if h == 0 else jnp.logical_not(head0), qn, 0.0)
        gate = _nt_dot(kmean_ref[...], qh, precision=HIGHEST)
        rank = jnp.zeros((nb, blk), F32)
        for jp in range(nb):
            row = gate[jp:jp + 1, :]
            beats = (row > gate) | ((row == gate) & (jidx > jp))
            rank = rank + jnp.where(beats, jnp.where(jp < i, 1.0, 0.0), 0.0)
        sel = (rank < topk) & (jidx < i)
        sel_ref[h] = jnp.where(sel, 1.0, 0.0)
        qs.append((qh * scale).astype(BF16))

    r_own = pl.multiple_of(i * blk, blk)
    k_own = kn_ref[pl.ds(r_own, blk), :]
    vt_own = vt_ref[:, pl.ds(r_own, blk)]
    key_i = lax.broadcasted_iota(jnp.int32, (blk, blk), 0)
    qry_i = lax.broadcasted_iota(jnp.int32, (blk, blk), 1)
    causal = key_i <= qry_i
    init = []
    for h in range(2):
        st = jnp.where(causal, _nt_dot(k_own, qs[h]), -jnp.inf)
        m = jnp.max(st, axis=0, keepdims=True)
        p = jnp.exp(st - m)
        l = jnp.sum(p, axis=0, keepdims=True)
        acc = jnp.dot(vt_own[h * dh:(h + 1) * dh, :], p.astype(BF16), preferred_element_type=F32)
        init += [m, l, acc]

    def body(j, carry):
        r0 = pl.multiple_of(j * blk, blk)
        kj = kn_ref[pl.ds(r0, blk), :]
        vtj = vt_ref[:, pl.ds(r0, blk)]
        out = []
        for h in range(2):
            m, l, acc = carry[3 * h:3 * h + 3]
            selrow = sel_ref[h, pl.ds(j, 1), :]
            st = jnp.where(selrow > 0.5, _nt_dot(kj, qs[h]), -jnp.inf)
            m_new = jnp.maximum(m, jnp.max(st, axis=0, keepdims=True))
            alpha = jnp.exp(m - m_new)
            p = jnp.exp(st - m_new)
            l = alpha * l + jnp.sum(p, axis=0, keepdims=True)
            acc = alpha * acc + jnp.dot(vtj[h * dh:(h + 1) * dh, :], p.astype(BF16),
                                        preferred_element_type=F32)
            out += [m_new, l, acc]
        return tuple(out)

    fin = lax.fori_loop(0, i, body, tuple(init))
    ot = jnp.concatenate([fin[2] / fin[1], fin[5] / fin[4]], axis=0)
    o_ref[...] = ot.T.astype(o_ref.dtype)


def moba(proj, gq2, gk2, *, batch, seq):
    nb = seq // MOBA_BLOCK
    blk = MOBA_BLOCK
    dh = MOBA_HEAD_DIM
    hp = MOBA_HEADS // 2
    body = functools.partial(_moba_body, nb=nb, blk=blk, dh=dh, topk=MOBA_TOPK)
    return pl.pallas_call(
        body,
        grid=(batch, hp, nb),
        in_specs=[
            pl.BlockSpec((blk, LANES), lambda b, p, i: (b * nb + i, COL_QA + p)),
            pl.BlockSpec((seq, LANES), lambda b, p, i: (b, COL_KA + p)),
            pl.BlockSpec((seq, LANES), lambda b, p, i: (b, COL_VA + p)),
            pl.BlockSpec((1, LANES), lambda b, p, i: (0, 0)),
            pl.BlockSpec((1, LANES), lambda b, p, i: (0, 0)),
        ],
        out_specs=pl.BlockSpec((blk, LANES), lambda b, p, i: (b * nb + i, p)),
        out_shape=jax.ShapeDtypeStruct((batch * seq, hp * LANES), BF16),
        scratch_shapes=[
            pltpu.VMEM((seq, LANES), BF16),
            pltpu.VMEM((LANES, seq), BF16),
            pltpu.VMEM((nb, LANES), F32),
            pltpu.VMEM((2, nb, blk), F32),
        ],
        compiler_params=_params("arbitrary", "arbitrary", "arbitrary"),
        name="moba",
    )(proj, proj, proj, gq2, gk2)


def _log_sigmoid(x):
    return jnp.minimum(x, 0.0) - jnp.log(1.0 + jnp.exp(-jnp.abs(x)))


def _mlstm_body(qr_ref, kr_ref, v_ref, og_ref, gcol_ref, grow_ref, brow_ref, bcol_ref,
                cwq_ref, cwk_ref, cbq_ref, cbk_ref, gh_ref, o_ref,
                qx_ref, kx_ref, c_ref, m_ref, *, chunk, dk, nh):
    L = chunk
    h = pl.program_id(1)
    c = pl.program_id(2)

    @pl.when(c == 0)
    def _():
        qx_ref[0:SUBLANES, :] = jnp.zeros((SUBLANES, dk), F32)
        kx_ref[0:SUBLANES, :] = jnp.zeros((SUBLANES, dk), F32)
        c_ref[...] = jnp.zeros_like(c_ref)
        m_ref[...] = jnp.zeros_like(m_ref)

    qx_ref[SUBLANES:SUBLANES + L, :] = qr_ref[...].astype(F32)
    kx_ref[SUBLANES:SUBLANES + L, :] = kr_ref[...].astype(F32)

    def conv_silu(x_ref, w_ref, b_ref):
        acc = b_ref[...] + w_ref[0:1, :] * x_ref[pl.ds(SUBLANES - CONV_WIDTH + 1, L), :]
        for j in range(1, CONV_WIDTH):
            acc = acc + w_ref[j:j + 1, :] * x_ref[pl.ds(SUBLANES - CONV_WIDTH + 1 + j, L), :]
        return acc * _sigmoid(acc)

    q = conv_silu(qx_ref, cwq_ref, cbq_ref)
    k = conv_silu(kx_ref, cwk_ref, cbk_ref) * (dk ** -0.5)
    qx_ref[0:SUBLANES, :] = qx_ref[L:L + SUBLANES, :]
    kx_ref[0:SUBLANES, :] = kx_ref[L:L + SUBLANES, :]

    lane = lax.broadcasted_iota(jnp.int32, (1, LANES), 1)
    pre_col = gcol_ref[...] + brow_ref[...]
    i_col = jnp.sum(jnp.where(lane == h, pre_col, 0.0), axis=-1, keepdims=True)
    f_col = jnp.sum(jnp.where(lane == h + nh, pre_col, 0.0), axis=-1, keepdims=True)
    sub = lax.broadcasted_iota(jnp.int32, (SUBLANES, 1), 0)
    pre_row = grow_ref[0] + bcol_ref[...]
    i_row = jnp.sum(jnp.where(sub == h, pre_row, 0.0), axis=0, keepdims=True)
    f_row = jnp.sum(jnp.where(sub == h + nh, pre_row, 0.0), axis=0, keepdims=True)
    logf_col = _log_sigmoid(f_col)
    logf_row = _log_sigmoid(f_row)

    t_i = lax.broadcasted_iota(jnp.int32, (L, L), 0)
    s_i = lax.broadcasted_iota(jnp.int32, (L, L), 1)
    tril = s_i <= t_i
    bcum_col = jnp.dot(jnp.where(tril, 1.0, 0.0), jnp.broadcast_to(logf_col, (L, LANES)),
                       precision=HIGHEST, preferred_element_type=F32)[:, 0:1]
    bcum_row = jnp.dot(jnp.broadcast_to(logf_row, (SUBLANES, L)), jnp.where(t_i <= s_i, 1.0, 0.0),
                       precision=HIGHEST, preferred_element_type=F32)[0:1, :]

    m_prev = m_ref[0:1, 0:1]
    a_col = bcum_col + m_prev
    dmat = jnp.where(tril, bcum_col - bcum_row + i_row, -jnp.inf)
    m_t = jnp.maximum(a_col, jnp.max(dmat, axis=-1, keepdims=True))
    dw = jnp.exp(dmat - m_t)
    aw = jnp.exp(a_col - m_t)

    qb = q.astype(BF16)
    kb = k.astype(BF16)
    v_aug = jnp.concatenate([v_ref[...], jnp.ones((L, dk), BF16)], axis=-1)
    sqk = _nt_dot(qb, kb) * dw
    num_aug = (aw * jnp.dot(qb, c_ref[...].astype(BF16), preferred_element_type=F32)
               + jnp.dot(sqk.astype(BF16), v_aug, preferred_element_type=F32))
    den = num_aug[:, dk:dk + 1]
    hc = num_aug[:, 0:dk] / jnp.maximum(jnp.abs(den), jnp.exp(-m_t))

    b_last = bcum_col[L - 1:L, :]
    g_col = b_last - bcum_col + i_col
    m_new = jnp.maximum(b_last + m_prev, jnp.max(g_col, axis=0, keepdims=True))
    w_c = jnp.exp(b_last + m_prev - m_new)
    kw_t = (k * jnp.exp(g_col - m_new)).T.astype(BF16)
    c_ref[...] = w_c * c_ref[...] + jnp.dot(kw_t, v_aug, preferred_element_type=F32)
    m_ref[...] = jnp.broadcast_to(m_new, m_ref.shape)

    o_ref[...] = (_rms(hc, gh_ref[...]) * _sigmoid(og_ref[...].astype(F32))).astype(o_ref.dtype)


def mlstm(proj, gates_col, gates_row, bias_row, bias_col, conv_w, conv_b, gh, *, batch, seq, chunk):
    nh = MLSTM_HEADS
    dk = MLSTM_DIM
    nc = seq // chunk
    body = functools.partial(_mlstm_body, chunk=chunk, dk=dk, nh=nh)

    def rows(col0):
        return pl.BlockSpec((chunk, LANES), lambda b, h, c: (b * nc + c, col0 + h))

    return pl.pallas_call(
        body,
        grid=(batch, nh, nc),
        in_specs=[
            rows(COL_QM), rows(COL_KM), rows(COL_VM), rows(COL_OM),
            pl.BlockSpec((chunk, LANES), lambda b, h, c: (b * nc + c, 0)),
            pl.BlockSpec((1, SUBLANES, chunk), lambda b, h, c: (b, 0, c)),
            pl.BlockSpec((1, LANES), lambda b, h, c: (0, 0)),
            pl.BlockSpec((SUBLANES, 1), lambda b, h, c: (0, 0)),
            pl.BlockSpec((CONV_WIDTH, LANES), lambda b, h, c: (0, h)),
            pl.BlockSpec((CONV_WIDTH, LANES), lambda b, h, c: (0, nh + h)),
            pl.BlockSpec((1, LANES), lambda b, h, c: (0, h)),
            pl.BlockSpec((1, LANES), lambda b, h, c: (0, nh + h)),
            pl.BlockSpec((1, LANES), lambda b, h, c: (0, 0)),
        ],
        out_specs=pl.BlockSpec((chunk, LANES), lambda b, h, c: (b * nc + c, h)),
        out_shape=jax.ShapeDtypeStruct((batch * seq, nh * dk), BF16),
        scratch_shapes=[
            pltpu.VMEM((chunk + 2 * SUBLANES, dk), F32),
            pltpu.VMEM((chunk + 2 * SUBLANES, dk), F32),
            pltpu.VMEM((dk, 2 * dk), F32),
            pltpu.VMEM((1, LANES), F32),
        ],
        compiler_params=_params("arbitrary", "arbitrary", "arbitrary"),
        name="mlstm",
    )(proj, proj, proj, proj, gates_col, gates_row, bias_row, bias_col,
      conv_w, conv_w, conv_b, conv_b, gh)


def _merge_body(ya_ref, yb_ref, ga_ref, gb_ref, x_ref, woa_ref, wob_ref, wout_ref, gffn_ref, *rest, moe):
    a = jnp.dot(ya_ref[...], woa_ref[...], preferred_element_type=F32)
    b = jnp.dot(yb_ref[...], wob_ref[...], preferred_element_type=F32)
    mixed = _sigmoid(ga_ref[...].astype(F32)) * a + _sigmoid(gb_ref[...].astype(F32)) * b
    x1 = x_ref[...] + jnp.dot(mixed.astype(BF16), wout_ref[...], preferred_element_type=F32)
    hf = _rms(x1, gffn_ref[...])
    if moe:
        wr_ref, x1_ref, hf_ref, lg_ref = rest
        hf_ref[...] = hf
        lg_ref[...] = jnp.dot(hf, wr_ref[...], precision=HIGHEST, preferred_element_type=F32)
    else:
        x1_ref, hf_ref = rest
        hf_ref[...] = hf.astype(BF16)
    x1_ref[...] = x1


def merge(ya, yb, proj, x2, woa, wob, wout, gffn, wr, *, tm):
    t, d = x2.shape
    moe = wr is not None
    full = lambda m: (0, 0)
    in_specs = [
        pl.BlockSpec((tm, ya.shape[1]), lambda m: (m, 0)),
        pl.BlockSpec((tm, yb.shape[1]), lambda m: (m, 0)),
        pl.BlockSpec((tm, d), lambda m: (m, COL_GA * LANES // d)),
        pl.BlockSpec((tm, d), lambda m: (m, COL_GB * LANES // d)),
        pl.BlockSpec((tm, d), lambda m: (m, 0)),
        pl.BlockSpec(woa.shape, full), pl.BlockSpec(wob.shape, full), pl.BlockSpec(wout.shape, full),
        pl.BlockSpec((1, d), full),
    ]
    args = [ya, yb, proj, proj, x2, woa, wob, wout, gffn]
    out_specs = [pl.BlockSpec((tm, d), lambda m: (m, 0)), pl.BlockSpec((tm, d), lambda m: (m, 0))]
    out_shape = [jax.ShapeDtypeStruct((t, d), F32), jax.ShapeDtypeStruct((t, d), F32 if moe else BF16)]
    if moe:
        in_specs.append(pl.BlockSpec(wr.shape, full))
        args.append(wr)
        out_specs.append(pl.BlockSpec((tm, LANES), lambda m: (m, 0)))
        out_shape.append(jax.ShapeDtypeStruct((t, LANES), F32))
    return pl.pallas_call(
        functools.partial(_merge_body, moe=moe),
        grid=(t // tm,),
        in_specs=in_specs, out_specs=out_specs, out_shape=out_shape,
        compiler_params=_params("arbitrary"),
        name="merge_moe" if moe else "merge",
    )(*args)


def _ffn_body(hf_ref, x1_ref, w1_ref, w3_ref, w2_ref, o_ref):
    f = pl.program_id(1)
    hf = hf_ref[...]
    a = jnp.dot(hf, w1_ref[...], preferred_element_type=F32)
    b = jnp.dot(hf, w3_ref[...], preferred_element_type=F32)
    g = (a * _sigmoid(a) * b).astype(BF16)
    y = jnp.dot(g, w2_ref[...], preferred_element_type=F32)

    @pl.when(f == 0)
    def _():
        o_ref[...] = x1_ref[...] + y

    @pl.when(f > 0)
    def _():
        o_ref[...] += y


def dense_ffn(hf, x1, w1, w3, w2, *, tm, tf):
    t, d = x1.shape
    dff = w1.shape[1]
    return pl.pallas_call(
        _ffn_body,
        grid=(t // tm, dff // tf),
        in_specs=[
            pl.BlockSpec((tm, d), lambda m, f: (m, 0)),
            pl.BlockSpec((tm, d), lambda m, f: (m, 0)),
            pl.BlockSpec((d, tf), lambda m, f: (0, f)),
            pl.BlockSpec((d, tf), lambda m, f: (0, f)),
            pl.BlockSpec((tf, d), lambda m, f: (f, 0)),
        ],
        out_specs=pl.BlockSpec((tm, d), lambda m, f: (m, 0)),
        out_shape=jax.ShapeDtypeStruct((t, d), F32),
        compiler_params=_params("arbitrary", "arbitrary"),
        name="dense_ffn",
    )(hf, x1, w1, w3, w2)


def _ple_body(x_ref, p_ref, g_ref, wg_ref, wp_ref, o_ref):
    x = x_ref[...]
    gate = _sigmoid(jnp.dot(_rms(x, g_ref[...]).astype(BF16), wg_ref[...], preferred_element_type=F32))
    emb = jnp.dot(p_ref[...].astype(BF16), wp_ref[...], preferred_element_type=F32)
    o_ref[...] = x + gate * emb


def ple(x2, p2, g, wg, wp, *, tm):
    t, d = x2.shape
    full = lambda m: (0, 0)
    return pl.pallas_call(
        _ple_body,
        grid=(t // tm,),
        in_specs=[
            pl.BlockSpec((tm, d), lambda m: (m, 0)),
            pl.BlockSpec((tm, p2.shape[1]), lambda m: (m, 0)),
            pl.BlockSpec((1, d), full), pl.BlockSpec(wg.shape, full), pl.BlockSpec(wp.shape, full),
        ],
        out_specs=pl.BlockSpec((tm, d), lambda m: (m, 0)),
        out_shape=jax.ShapeDtypeStruct((t, d), F32),
        compiler_params=_params("arbitrary"),
        name="ple",
    )(x2, p2, g, wg, wp)


META_E0, META_E1, META_G0, META_G1, META_R0, META_R1 = 0, 1, 2, 3, 4, 5


def _route_body(lg_ref, meta_ref, cnt_ref, carry_ref, *, tm, ne):
    @pl.when(pl.program_id(0) == 0)
    def _():
        carry_ref[...] = jnp.zeros_like(carry_ref)

    lane = lax.broadcasted_iota(jnp.int32, (tm, LANES), 1)
    lanef = lane.astype(F32)
    lg = jnp.where(lane < ne, lg_ref[...], -jnp.inf)
    m1 = jnp.max(lg, axis=-1, keepdims=True)
    e1 = jnp.min(jnp.where(lg == m1, lanef, float(LANES)), axis=-1, keepdims=True)
    lg2 = jnp.where(lanef == e1, -jnp.inf, lg)
    m2 = jnp.max(lg2, axis=-1, keepdims=True)
    e2 = jnp.min(jnp.where(lg2 == m2, lanef, float(LANES)), axis=-1, keepdims=True)
    ex = jnp.exp(m2 - m1)
    g1 = 1.0 / (1.0 + ex)
    g2 = ex / (1.0 + ex)
    onehot = jnp.where((lanef == e1) | (lanef == e2), 1.0, 0.0)
    row = lax.broadcasted_iota(jnp.int32, (tm, tm), 0)
    col = lax.broadcasted_iota(jnp.int32, (tm, tm), 1)
    before = jnp.dot(jnp.where(col < row, 1.0, 0.0).astype(BF16), onehot.astype(BF16),
                     preferred_element_type=F32) + carry_ref[...]
    r1 = jnp.sum(jnp.where(lanef == e1, before, 0.0), axis=-1, keepdims=True)
    r2 = jnp.sum(jnp.where(lanef == e2, before, 0.0), axis=-1, keepdims=True)
    carry_ref[...] += jnp.sum(onehot, axis=0, keepdims=True)
    meta = jnp.zeros((tm, LANES), F32)
    for pos, val in ((META_E0, e1), (META_E1, e2), (META_G0, g1), (META_G1, g2), (META_R0, r1), (META_R1, r2)):
        meta = jnp.where(lane == pos, val, meta)
    meta_ref[...] = meta
    cnt_ref[...] = jnp.broadcast_to(carry_ref[...], cnt_ref.shape)


def route(logits, *, tm):
    t = logits.shape[0]
    return pl.pallas_call(
        functools.partial(_route_body, tm=tm, ne=N_EXPERTS),
        grid=(t // tm,),
        in_specs=[pl.BlockSpec((tm, LANES), lambda m: (m, 0))],
        out_specs=[pl.BlockSpec((tm, LANES), lambda m: (m, 0)),
                   pl.BlockSpec((SUBLANES, LANES), lambda m: (0, 0))],
        out_shape=[jax.ShapeDtypeStruct((t, LANES), F32), jax.ShapeDtypeStruct((SUBLANES, LANES), F32)],
        scratch_shapes=[pltpu.VMEM((1, LANES), F32)],
        compiler_params=_params("arbitrary"),
        name="route",
    )(logits)


def _dispatch_body(dest_ref, hf_ref, xs_in_ref, xs_ref, sem, *, tm, topk):
    del xs_in_ref
    base = pl.program_id(0) * tm * topk

    def row_copy(r, k):
        d = dest_ref[base + topk * r + k]
        return pltpu.make_async_copy(hf_ref.at[pl.ds(r, 1)], xs_ref.at[pl.ds(d, 1)], sem)

    def issue(r, c):
        for k in range(topk):
            row_copy(r, k).start()
        return c

    def drain(r, c):
        for k in range(topk):
            row_copy(r, k).wait()
        return c

    lax.fori_loop(0, tm, issue, 0)
    lax.fori_loop(0, tm, drain, 0)


def dispatch(dest, hf, xs_init, *, tm, topk):
    t, d = hf.shape
    return pl.pallas_call(
        functools.partial(_dispatch_body, tm=tm, topk=topk),
        grid_spec=pltpu.PrefetchScalarGridSpec(
            num_scalar_prefetch=1,
            grid=(t // tm,),
            in_specs=[pl.BlockSpec((tm, d), lambda m, dest: (m, 0)),
                      pl.BlockSpec(memory_space=pl.ANY)],
            out_specs=pl.BlockSpec(memory_space=pl.ANY),
            scratch_shapes=[pltpu.SemaphoreType.DMA],
        ),
        out_shape=jax.ShapeDtypeStruct(xs_init.shape, xs_init.dtype),
        input_output_aliases={2: 0},
        compiler_params=_params("arbitrary"),
        name="moe_dispatch",
    )(dest, hf, xs_init)


def _experts_body(te_ref, xs_ref, w1_ref, w3_ref, w2_ref, y_ref, xb_ref):
    del te_ref
    f = pl.program_id(1)

    @pl.when(f == 0)
    def _():
        xb_ref[...] = xs_ref[...].astype(BF16)

    xb = xb_ref[...]
    a = jnp.dot(xb, w1_ref[0], preferred_element_type=F32)
    b = jnp.dot(xb, w3_ref[0], preferred_element_type=F32)
    g = (a * _sigmoid(a) * b).astype(BF16)
    y = jnp.dot(g, w2_ref[0], preferred_element_type=F32)

    @pl.when(f == 0)
    def _():
        y_ref[...] = y

    @pl.when(f > 0)
    def _():
        y_ref[...] += y


def experts(tile_expert, xs, w1, w3, w2, *, tm, tf):
    n_rows, d = xs.shape
    dff = w1.shape[2]
    return pl.pallas_call(
        _experts_body,
        grid_spec=pltpu.PrefetchScalarGridSpec(
            num_scalar_prefetch=1,
            grid=(n_rows // tm, dff // tf),
            in_specs=[
                pl.BlockSpec((tm, d), lambda i, f, te: (i, 0)),
                pl.BlockSpec((1, d, tf), lambda i, f, te: (te[i], 0, f)),
                pl.BlockSpec((1, d, tf), lambda i, f, te: (te[i], 0, f)),
                pl.BlockSpec((1, tf, d), lambda i, f, te: (te[i], f, 0)),
            ],
            out_specs=pl.BlockSpec((tm, d), lambda i, f, te: (i, 0)),
            scratch_shapes=[pltpu.VMEM((tm, d), BF16)],
        ),
        out_shape=jax.ShapeDtypeStruct((n_rows, d), F32),
        compiler_params=_params("arbitrary", "arbitrary"),
        name="moe_experts",
    )(tile_expert, xs, w1, w3, w2)


def _combine_body(dest_ref, x1_ref, meta_ref, y_ref, o_ref, buf_ref, sem, *, tm, topk):
    base = pl.program_id(0) * tm * topk

    def row_copy(r, k):
        d = dest_ref[base + topk * r + k]
        return pltpu.make_async_copy(y_ref.at[pl.ds(d, 1)], buf_ref.at[k, pl.ds(r, 1)], sem)

    def issue(r, c):
        for k in range(topk):
            row_copy(r, k).start()
        return c

    def drain(r, c):
        for k in range(topk):
            row_copy(r, k).wait()
        return c

    lax.fori_loop(0, tm, issue, 0)
    lax.fori_loop(0, tm, drain, 0)
    meta = meta_ref[...]
    g0 = meta[:, META_G0:META_G0 + 1]
    g1 = meta[:, META_G1:META_G1 + 1]
    o_ref[...] = x1_ref[...] + (g0 * buf_ref[0] + g1 * buf_ref[1])


def combine(dest, x1, meta, y, *, tm, topk):
    t, d = x1.shape
    return pl.pallas_call(
        functools.partial(_combine_body, tm=tm, topk=topk),
        grid_spec=pltpu.PrefetchScalarGridSpec(
            num_scalar_prefetch=1,
            grid=(t // tm,),
            in_specs=[pl.BlockSpec((tm, d), lambda m, dest: (m, 0)),
                      pl.BlockSpec((tm, LANES), lambda m, dest: (m, 0)),
                      pl.BlockSpec(memory_space=pl.ANY)],
            out_specs=pl.BlockSpec((tm, d), lambda m, dest: (m, 0)),
            scratch_shapes=[pltpu.VMEM((topk, tm, d), F32), pltpu.SemaphoreType.DMA],
        ),
        out_shape=jax.ShapeDtypeStruct((t, d), F32),
        compiler_params=_params("arbitrary"),
        name="moe_combine",
    )(dest, x1, meta, y)


def moe_ffn(hf, x1, logits, w1, w3, w2, *, tm_route, tm_rows, tf, tm_move):
    t, d = x1.shape
    topk = 2
    meta, cnt = route(logits, tm=tm_route)
    counts = cnt[0, :N_EXPERTS].astype(jnp.int32)
    padded = ((counts + tm_rows - 1) // tm_rows) * tm_rows
    pad_end = jnp.cumsum(padded)
    pad_start = pad_end - padded
    eidx = meta[:, META_E0:META_E1 + 1].astype(jnp.int32)
    rank = meta[:, META_R0:META_R1 + 1].astype(jnp.int32)
    dest = (pad_start[eidx] + rank).reshape(t * topk)
    n_tiles = -(-(t * topk) // tm_rows) + N_EXPERTS
    tile_start = jnp.arange(n_tiles, dtype=jnp.int32) * tm_rows
    tile_expert = jnp.minimum(jnp.sum(tile_start[:, None] >= pad_end[None, :], axis=1),
                              N_EXPERTS - 1).astype(jnp.int32)
    xs = dispatch(dest, hf, jnp.zeros((n_tiles * tm_rows, d), F32), tm=tm_move, topk=topk)
    y = experts(tile_expert, xs, w1, w3, w2, tm=tm_rows, tf=tf)
    return combine(dest, x1, meta, y, tm=tm_move, topk=topk)


def _tile2(g):
    return jnp.concatenate([g, g]).reshape(1, 2 * g.shape[0])


def kernel(x, p, g_mix, w_in, g_q, g_k, conv_w, conv_b, b_i, b_f, g_h, w_oa, w_ob, w_out, g_ffn, w_d1, w_d3,
           w_d2, w_router, w_e1, w_e3, w_e2, g_ple, w_ple_gate, w_ple_proj):
    batch, seq, d = x.shape
    depth = w_in.shape[0]
    t = batch * seq
    nh = MLSTM_HEADS
    x2 = x.reshape(t, d)
    c_q, c_k, c_v = 0, 512, 1024
    c_qk, c_vm, c_om, c_i, c_f, c_ga, c_gb, c_end = 1536, 2560, 3072, 3584, 3588, 3592, 4616, 5640

    for l in range(depth):
        w = w_in[l]
        w_main = jnp.concatenate([w[:, c_ga:c_gb], w[:, c_gb:c_end], w[:, c_q:c_i]], axis=1).astype(BF16)
        w_if = jnp.pad(w[:, c_i:c_ga], ((0, 0), (0, LANES - 2 * nh)))
        proj, gif = in_proj(x2, g_mix[l].reshape(1, d), w_main, w_if, tm=1024, tn=512)

        ya = moba(proj, _tile2(g_q[l]), _tile2(g_k[l]), batch=batch, seq=seq)

        bias = jnp.concatenate([b_i[l], b_f[l]])
        bias_row = jnp.pad(bias, (0, LANES - 2 * nh)).reshape(1, LANES)
        bias_col = bias.reshape(2 * nh, 1)
        gates_row = gif[:, :2 * nh].reshape(batch, seq, 2 * nh).transpose(0, 2, 1)
        yb = mlstm(proj, gif, gates_row, bias_row, bias_col, conv_w[l], conv_b[l].reshape(1, -1),
                   g_h[l].reshape(1, -1), batch=batch, seq=seq, chunk=256)

        j = l // 2
        moe = l % 2 == 1
        wr = jnp.pad(w_router[j], ((0, 0), (0, LANES - N_EXPERTS))) if moe else None
        outs = merge(ya, yb, proj, x2, w_oa[l].astype(BF16), w_ob[l].astype(BF16), w_out[l].astype(BF16),
                     g_ffn[l].reshape(1, d), wr, tm=512)
        if moe:
            x1, hf, logits = outs
            x2 = moe_ffn(hf, x1, logits, w_e1[j].astype(BF16), w_e3[j].astype(BF16), w_e2[j].astype(BF16),
                         tm_route=512, tm_rows=512, tf=512, tm_move=256)
        else:
            x1, hf = outs
            x2 = dense_ffn(hf, x1, w_d1[j].astype(BF16), w_d3[j].astype(BF16), w_d2[j].astype(BF16),
                           tm=1024, tf=256)

        x2 = ple(x2, p[l].reshape(t, -1), g_ple[l].reshape(1, d), w_ple_gate[l].astype(BF16),
                 w_ple_proj[l].astype(BF16), tm=1024)
    return x2.reshape(batch, seq, d)
```

```python
import functools

import jax
import jax.numpy as jnp
from jax import lax
from jax.experimental import pallas as pl
from jax.experimental.pallas import tpu as pltpu

F32 = jnp.float32
BF16 = jnp.bfloat16
HIGHEST = lax.Precision.HIGHEST

RMS_EPS = 1e-6
LANES = 128
SUBLANES = 8

MOBA_HEADS = 8
MOBA_HEAD_DIM = 64
MOBA_BLOCK = 256
MOBA_TOPK = 3
MLSTM_HEADS = 4
MLSTM_DIM = 128
CONV_WIDTH = 4
N_EXPERTS = 8

COL_GA, COL_GB = 0, 8
COL_QA, COL_KA, COL_VA = 16, 20, 24
COL_QM, COL_KM, COL_VM, COL_OM = 28, 32, 36, 40
N_PROJ = 44 * LANES

VMEM_LIMIT = 48 * 1024 * 1024


def _params(*sem):
    return pltpu.CompilerParams(dimension_semantics=sem, vmem_limit_bytes=VMEM_LIMIT)


def _sigmoid(x):
    return 1.0 / (1.0 + jnp.exp(-x))


def _rms(x, g):
    return x * lax.rsqrt(jnp.mean(x * x, axis=-1, keepdims=True) + RMS_EPS) * g


def _nt_dot(a, b, **kw):
    return lax.dot_general(a, b, (((1,), (1,)), ((), ())), preferred_element_type=F32, **kw)


def _in_proj_body(x_ref, g_ref, w_ref, wif_ref, o_ref, oif_ref, h_ref):
    @pl.when(pl.program_id(1) == 0)
    def _():
        hf = _rms(x_ref[...], g_ref[...])
        h_ref[...] = hf.astype(BF16)
        oif_ref[...] = jnp.dot(hf, wif_ref[...], precision=HIGHEST, preferred_element_type=F32)

    o_ref[...] = jnp.dot(h_ref[...], w_ref[...], preferred_element_type=F32).astype(o_ref.dtype)


def in_proj(x2, g, w, wif, *, tm, tn):
    t, d = x2.shape
    n = w.shape[1]
    return pl.pallas_call(
        _in_proj_body,
        grid=(t // tm, n // tn),
        in_specs=[
            pl.BlockSpec((tm, d), lambda m, j: (m, 0)),
            pl.BlockSpec((1, d), lambda m, j: (0, 0)),
            pl.BlockSpec((d, tn), lambda m, j: (0, j)),
            pl.BlockSpec((d, LANES), lambda m, j: (0, 0)),
        ],
        out_specs=[
            pl.BlockSpec((tm, tn), lambda m, j: (m, j)),
            pl.BlockSpec((tm, LANES), lambda m, j: (m, 0)),
        ],
        out_shape=[jax.ShapeDtypeStruct((t, n), BF16), jax.ShapeDtypeStruct((t, LANES), F32)],
        scratch_shapes=[pltpu.VMEM((tm, d), BF16)],
        compiler_params=_params("arbitrary", "arbitrary"),
        name="in_proj",
    )(x2, g, w, wif)


MASK_BIAS = -1e30
LOG2_E = 1.4426950408889634


def _moba_body(q_ref, k_ref, v_ref, gq_ref, gk_ref, o_ref,
               kn_ref, vt_ref, kmean_ref, qaug_ref, s_ref, m_ref, alpha_ref, l_ref, acc_ref,
               *, nb, blk, dh, topk, nheads):
    i = pl.program_id(1)
    lane = lax.broadcasted_iota(jnp.int32, (1, LANES), 1)
    head0 = lane < dh

    def head_rms(x, g):
        x2 = x * x
        s0 = jnp.sum(jnp.where(head0, x2, 0.0), axis=-1, keepdims=True)
        s1 = jnp.sum(jnp.where(head0, 0.0, x2), axis=-1, keepdims=True)
        ms = jnp.where(head0, s0, s1) * (1.0 / dh)
        return x * lax.rsqrt(ms + RMS_EPS) * g

    @pl.when(i == 0)
    def _():
        def prep(j, c):
            r0 = pl.multiple_of(j * blk, blk)
            onehot = jnp.where(lane == dh + j, 1.0, 0.0)
            for p in range(nheads // 2):
                cols = slice(p * LANES, (p + 1) * LANES)
                kn = head_rms(k_ref[pl.ds(r0, blk), cols].astype(F32), gk_ref[...])
                for hh, kh in ((0, kn), (1, pltpu.roll(kn, dh, axis=1))):
                    h = 2 * p + hh
                    kmean_ref[h, pl.ds(j, 1), :] = jnp.mean(jnp.where(head0, kh, 0.0), axis=0, keepdims=True)
                    kn_ref[h, pl.ds(r0, blk), :] = jnp.where(head0, kh, onehot).astype(BF16)
                vt_ref[cols, pl.ds(r0, blk)] = v_ref[pl.ds(r0, blk), cols].astype(F32).T.astype(BF16)
            return c

        lax.fori_loop(0, nb, prep, 0)

    jidx = lax.broadcasted_iota(jnp.int32, (nb, blk), 0)
    key_i = lax.broadcasted_iota(jnp.int32, (blk, blk), 0)
    qry_i = lax.broadcasted_iota(jnp.int32, (blk, blk), 1)
    causal = key_i <= qry_i
    r_own = pl.multiple_of(i * blk, blk)
    qk_scale = dh ** -0.5 * LOG2_E
    for p in range(nheads // 2):
        cols = slice(p * LANES, (p + 1) * LANES)
        qn_t = head_rms(q_ref[:, cols].astype(F32), gq_ref[...]).T
        for hh in range(2):
            h = 2 * p + hh
            q_t = qn_t[hh * dh:(hh + 1) * dh, :]
            gate = jnp.dot(kmean_ref[h], jnp.concatenate([q_t, jnp.zeros((LANES - dh, blk), F32)], axis=0),
                           precision=HIGHEST, preferred_element_type=F32)
            rank = jnp.zeros((nb, blk), F32)
            for jp in range(nb):
                row = gate[jp:jp + 1, :]
                beats = (row > gate) | ((row == gate) & (jidx > jp))
                rank = rank + jnp.where(beats, jnp.where(jp < i, 1.0, 0.0), 0.0)
            sel = ((rank < topk) & (jidx < i)) | (jidx == i)
            bias = jnp.where(sel, 0.0, MASK_BIAS)
            qaug = jnp.concatenate([q_t * qk_scale, bias, jnp.zeros((LANES - dh - nb, blk), F32)],
                                   axis=0).astype(BF16)
            qaug_ref[h] = qaug
            st = jnp.dot(kn_ref[h, pl.ds(r_own, blk), :], qaug, preferred_element_type=F32)
            st = jnp.where(causal, st, -jnp.inf)
            s_ref[h] = st
            m_ref[h] = jnp.max(st, axis=0, keepdims=True)
            alpha_ref[h] = jnp.zeros((1, blk), F32)
            l_ref[h] = jnp.zeros((1, blk), F32)
            acc_ref[h] = jnp.zeros((dh, blk), F32)

    def finish_block(h, r_blk):
        pr = jnp.exp2(s_ref[h] - m_ref[h])
        alpha = alpha_ref[h]
        l_ref[h] = alpha * l_ref[h] + jnp.sum(pr, axis=0, keepdims=True)
        acc_ref[h] = alpha * acc_ref[h] + jnp.dot(vt_ref[h * dh:(h + 1) * dh, pl.ds(r_blk, blk)],
                                                  pr.astype(BF16), preferred_element_type=F32)

    def body(j, r_prev):
        r0 = pl.multiple_of(j * blk, blk)
        r_prev = pl.multiple_of(r_prev, blk)
        for h in range(nheads):
            finish_block(h, r_prev)
            st = jnp.dot(kn_ref[h, pl.ds(r0, blk), :], qaug_ref[h], preferred_element_type=F32)
            m_old = m_ref[h]
            m_new = jnp.maximum(m_old, jnp.max(st, axis=0, keepdims=True))
            s_ref[h] = st
            alpha_ref[h] = jnp.exp2(m_old - m_new)
            m_ref[h] = m_new
        return r0

    r_last = lax.fori_loop(0, i, body, r_own)
    r_last = pl.multiple_of(r_last, blk)
    for h in range(nheads):
        finish_block(h, r_last)
    for p in range(nheads // 2):
        ot = jnp.concatenate([acc_ref[2 * p] / l_ref[2 * p], acc_ref[2 * p + 1] / l_ref[2 * p + 1]], axis=0)
        o_ref[:, p * LANES:(p + 1) * LANES] = ot.T.astype(o_ref.dtype)


def moba(proj, gq2, gk2, *, batch, seq):
    nb = seq // MOBA_BLOCK
    blk = MOBA_BLOCK
    dh = MOBA_HEAD_DIM
    nheads = MOBA_HEADS
    width = nheads * dh
    wb = width // LANES
    assert dh + nb <= LANES and 2 * dh == LANES
    body = functools.partial(_moba_body, nb=nb, blk=blk, dh=dh, topk=MOBA_TOPK, nheads=nheads)
    return pl.pallas_call(
        body,
        grid=(batch, nb),
        in_specs=[
            pl.BlockSpec((blk, width), lambda b, i: (b * nb + i, COL_QA // wb)),
            pl.BlockSpec((seq, width), lambda b, i: (b, COL_KA // wb)),
            pl.BlockSpec((seq, width), lambda b, i: (b, COL_VA // wb)),
            pl.BlockSpec((1, LANES), lambda b, i: (0, 0)),
            pl.BlockSpec((1, LANES), lambda b, i: (0, 0)),
        ],
        out_specs=pl.BlockSpec((blk, width), lambda b, i: (b * nb + i, 0)),
        out_shape=jax.ShapeDtypeStruct((batch * seq, width), BF16),
        scratch_shapes=[
            pltpu.VMEM((nheads, seq, LANES), BF16),
            pltpu.VMEM((width, seq), BF16),
            pltpu.VMEM((nheads, nb, LANES), F32),
            pltpu.VMEM((nheads, LANES, blk), BF16),
            pltpu.VMEM((nheads, blk, blk), F32),
            pltpu.VMEM((nheads, 1, blk), F32),
            pltpu.VMEM((nheads, 1, blk), F32),
            pltpu.VMEM((nheads, 1, blk), F32),
            pltpu.VMEM((nheads, dh, blk), F32),
        ],
        compiler_params=_params("arbitrary", "arbitrary"),
        name="moba",
    )(proj, proj, proj, gq2, gk2)


def _log_sigmoid(x):
    return jnp.minimum(x, 0.0) - jnp.log(1.0 + jnp.exp(-jnp.abs(x)))


def _mlstm_body(qr_ref, kr_ref, v_ref, og_ref, gcol_ref, grow_ref, brow_ref, bcol_ref,
                cwq_ref, cwk_ref, cbq_ref, cbk_ref, gh_ref, o_ref,
                qx_ref, kx_ref, c_ref, m_ref, *, chunk, dk, nh):
    L = chunk
    h = pl.program_id(1)
    c = pl.program_id(2)

    @pl.when(c == 0)
    def _():
        qx_ref[0:SUBLANES, :] = jnp.zeros((SUBLANES, dk), F32)
        kx_ref[0:SUBLANES, :] = jnp.zeros((SUBLANES, dk), F32)
        c_ref[...] = jnp.zeros_like(c_ref)
        m_ref[...] = jnp.zeros_like(m_ref)

    qx_ref[SUBLANES:SUBLANES + L, :] = qr_ref[...].astype(F32)
    kx_ref[SUBLANES:SUBLANES + L, :] = kr_ref[...].astype(F32)

    def conv_silu(x_ref, w_ref, b_ref):
        acc = b_ref[...] + w_ref[0:1, :] * x_ref[pl.ds(SUBLANES - CONV_WIDTH + 1, L), :]
        for j in range(1, CONV_WIDTH):
            acc = acc + w_ref[j:j + 1, :] * x_ref[pl.ds(SUBLANES - CONV_WIDTH + 1 + j, L), :]
        return acc * _sigmoid(acc)

    q = conv_silu(qx_ref, cwq_ref, cbq_ref)
    k = conv_silu(kx_ref, cwk_ref, cbk_ref) * (dk ** -0.5)
    qx_ref[0:SUBLANES, :] = qx_ref[L:L + SUBLANES, :]
    kx_ref[0:SUBLANES, :] = kx_ref[L:L + SUBLANES, :]

    lane = lax.broadcasted_iota(jnp.int32, (1, LANES), 1)
    pre_col = gcol_ref[...] + brow_ref[...]
    i_col = jnp.sum(jnp.where(lane == h, pre_col, 0.0), axis=-1, keepdims=True)
    f_col = jnp.sum(jnp.where(lane == h + nh, pre_col, 0.0), axis=-1, keepdims=True)
    sub = lax.broadcasted_iota(jnp.int32, (SUBLANES, 1), 0)
    pre_row = grow_ref[0] + bcol_ref[...]
    i_row = jnp.sum(jnp.where(sub == h, pre_row, 0.0), axis=0, keepdims=True)
    f_row = jnp.sum(jnp.where(sub == h + nh, pre_row, 0.0), axis=0, keepdims=True)
    logf_col = _log_sigmoid(f_col)
    logf_row = _log_sigmoid(f_row)

    t_i = lax.broadcasted_iota(jnp.int32, (L, L), 0)
    s_i = lax.broadcasted_iota(jnp.int32, (L, L), 1)
    tril = s_i <= t_i
    bcum_col = jnp.dot(jnp.where(tril, 1.0, 0.0), jnp.broadcast_to(logf_col, (L, LANES)),
                       precision=HIGHEST, preferred_element_type=F32)[:, 0:1]
    bcum_row = jnp.dot(jnp.broadcast_to(logf_row, (SUBLANES, L)), jnp.where(t_i <= s_i, 1.0, 0.0),
                       precision=HIGHEST, preferred_element_type=F32)[0:1, :]

    m_prev = m_ref[0:1, 0:1]
    a_col = bcum_col + m_prev
    dmat = jnp.where(tril, bcum_col - bcum_row + i_row, -jnp.inf)
    m_t = jnp.maximum(a_col, jnp.max(dmat, axis=-1, keepdims=True))
    dw = jnp.exp(dmat - m_t)
    aw = jnp.exp(a_col - m_t)

    qb = q.astype(BF16)
    kb = k.astype(BF16)
    v_aug = jnp.concatenate([v_ref[...], jnp.ones((L, dk), BF16)], axis=-1)
    sqk = _nt_dot(qb, kb) * dw
    num_aug = (aw * jnp.dot(qb, c_ref[...].astype(BF16), preferred_element_type=F32)
               + jnp.dot(sqk.astype(BF16), v_aug, preferred_element_type=F32))
    den = num_aug[:, dk:dk + 1]
    hc = num_aug[:, 0:dk] / jnp.maximum(jnp.abs(den), jnp.exp(-m_t))

    b_last = bcum_col[L - 1:L, :]
    g_col = b_last - bcum_col + i_col
    m_new = jnp.maximum(b_last + m_prev, jnp.max(g_col, axis=0, keepdims=True))
    w_c = jnp.exp(b_last + m_prev - m_new)
    kw_t = (k * jnp.exp(g_col - m_new)).T.astype(BF16)
    c_ref[...] = w_c * c_ref[...] + jnp.dot(kw_t, v_aug, preferred_element_type=F32)
    m_ref[...] = jnp.broadcast_to(m_new, m_ref.shape)

    o_ref[...] = (_rms(hc, gh_ref[...]) * _sigmoid(og_ref[...].astype(F32))).astype(o_ref.dtype)


def mlstm(proj, gates_col, gates_row, bias_row, bias_col, conv_w, conv_b, gh, *, batch, seq, chunk):
    nh = MLSTM_HEADS
    dk = MLSTM_DIM
    nc = seq // chunk
    body = functools.partial(_mlstm_body, chunk=chunk, dk=dk, nh=nh)

    def rows(col0):
        return pl.BlockSpec((chunk, LANES), lambda b, h, c: (b * nc + c, col0 + h))

    return pl.pallas_call(
        body,
        grid=(batch, nh, nc),
        in_specs=[
            rows(COL_QM), rows(COL_KM), rows(COL_VM), rows(COL_OM),
            pl.BlockSpec((chunk, LANES), lambda b, h, c: (b * nc + c, 0)),
            pl.BlockSpec((1, SUBLANES, chunk), lambda b, h, c: (b, 0, c)),
            pl.BlockSpec((1, LANES), lambda b, h, c: (0, 0)),
            pl.BlockSpec((SUBLANES, 1), lambda b, h, c: (0, 0)),
            pl.BlockSpec((CONV_WIDTH, LANES), lambda b, h, c: (0, h)),
            pl.BlockSpec((CONV_WIDTH, LANES), lambda b, h, c: (0, nh + h)),
            pl.BlockSpec((1, LANES), lambda b, h, c: (0, h)),
            pl.BlockSpec((1, LANES), lambda b, h, c: (0, nh + h)),
            pl.BlockSpec((1, LANES), lambda b, h, c: (0, 0)),
        ],
        out_specs=pl.BlockSpec((chunk, LANES), lambda b, h, c: (b * nc + c, h)),
        out_shape=jax.ShapeDtypeStruct((batch * seq, nh * dk), BF16),
        scratch_shapes=[
            pltpu.VMEM((chunk + 2 * SUBLANES, dk), F32),
            pltpu.VMEM((chunk + 2 * SUBLANES, dk), F32),
            pltpu.VMEM((dk, 2 * dk), F32),
            pltpu.VMEM((1, LANES), F32),
        ],
        compiler_params=_params("arbitrary", "arbitrary", "arbitrary"),
        name="mlstm",
    )(proj, proj, proj, proj, gates_col, gates_row, bias_row, bias_col,
      conv_w, conv_w, conv_b, conv_b, gh)


def _merge_body(ya_ref, yb_ref, ga_ref, gb_ref, x_ref, woa_ref, wob_ref, wout_ref, gffn_ref, *rest, moe):
    a = jnp.dot(ya_ref[...], woa_ref[...], preferred_element_type=F32)
    b = jnp.dot(yb_ref[...], wob_ref[...], preferred_element_type=F32)
    mixed = _sigmoid(ga_ref[...].astype(F32)) * a + _sigmoid(gb_ref[...].astype(F32)) * b
    x1 = x_ref[...] + jnp.dot(mixed.astype(BF16), wout_ref[...], preferred_element_type=F32)
    hf = _rms(x1, gffn_ref[...])
    if moe:
        wr_ref, x1_ref, hf_ref, lg_ref = rest
        hf_ref[...] = hf
        lg_ref[...] = jnp.dot(hf, wr_ref[...], precision=HIGHEST, preferred_element_type=F32)
    else:
        x1_ref, hf_ref = rest
        hf_ref[...] = hf.astype(BF16)
    x1_ref[...] = x1


def merge(ya, yb, proj, x2, woa, wob, wout, gffn, wr, *, tm):
    t, d = x2.shape
    moe = wr is not None
    full = lambda m: (0, 0)
    in_specs = [
        pl.BlockSpec((tm, ya.shape[1]), lambda m: (m, 0)),
        pl.BlockSpec((tm, yb.shape[1]), lambda m: (m, 0)),
        pl.BlockSpec((tm, d), lambda m: (m, COL_GA * LANES // d)),
        pl.BlockSpec((tm, d), lambda m: (m, COL_GB * LANES // d)),
        pl.BlockSpec((tm, d), lambda m: (m, 0)),
        pl.BlockSpec(woa.shape, full), pl.BlockSpec(wob.shape, full), pl.BlockSpec(wout.shape, full),
        pl.BlockSpec((1, d), full),
    ]
    args = [ya, yb, proj, proj, x2, woa, wob, wout, gffn]
    out_specs = [pl.BlockSpec((tm, d), lambda m: (m, 0)), pl.BlockSpec((tm, d), lambda m: (m, 0))]
    out_shape = [jax.ShapeDtypeStruct((t, d), F32), jax.ShapeDtypeStruct((t, d), F32 if moe else BF16)]
    if moe:
        in_specs.append(pl.BlockSpec(wr.shape, full))
        args.append(wr)
        out_specs.append(pl.BlockSpec((tm, LANES), lambda m: (m, 0)))
        out_shape.append(jax.ShapeDtypeStruct((t, LANES), F32))
    return pl.pallas_call(
        functools.partial(_merge_body, moe=moe),
        grid=(t // tm,),
        in_specs=in_specs, out_specs=out_specs, out_shape=out_shape,
        compiler_params=_params("arbitrary"),
        name="merge_moe" if moe else "merge",
    )(*args)


def _ffn_body(hf_ref, x1_ref, w1_ref, w3_ref, w2_ref, o_ref):
    f = pl.program_id(1)
    hf = hf_ref[...]
    a = jnp.dot(hf, w1_ref[...], preferred_element_type=F32)
    b = jnp.dot(hf, w3_ref[...], preferred_element_type=F32)
    g = (a * _sigmoid(a) * b).astype(BF16)
    y = jnp.dot(g, w2_ref[...], preferred_element_type=F32)

    @pl.when(f == 0)
    def _():
        o_ref[...] = x1_ref[...] + y

    @pl.when(f > 0)
    def _():
        o_ref[...] += y


def dense_ffn(hf, x1, w1, w3, w2, *, tm, tf):
    t, d = x1.shape
    dff = w1.shape[1]
    return pl.pallas_call(
        _ffn_body,
        grid=(t // tm, dff // tf),
        in_specs=[
            pl.BlockSpec((tm, d), lambda m, f: (m, 0)),
            pl.BlockSpec((tm, d), lambda m, f: (m, 0)),
            pl.BlockSpec((d, tf), lambda m, f: (0, f)),
            pl.BlockSpec((d, tf), lambda m, f: (0, f)),
            pl.BlockSpec((tf, d), lambda m, f: (f, 0)),
        ],
        out_specs=pl.BlockSpec((tm, d), lambda m, f: (m, 0)),
        out_shape=jax.ShapeDtypeStruct((t, d), F32),
        compiler_params=_params("arbitrary", "arbitrary"),
        name="dense_ffn",
    )(hf, x1, w1, w3, w2)


def _ple_body(x_ref, p_ref, g_ref, wg_ref, wp_ref, o_ref):
    x = x_ref[...]
    gate = _sigmoid(jnp.dot(_rms(x, g_ref[...]).astype(BF16), wg_ref[...], preferred_element_type=F32))
    emb = jnp.dot(p_ref[...].astype(BF16), wp_ref[...], preferred_element_type=F32)
    o_ref[...] = x + gate * emb


def ple(x2, p2, g, wg, wp, *, tm):
    t, d = x2.shape
    full = lambda m: (0, 0)
    return pl.pallas_call(
        _ple_body,
        grid=(t // tm,),
        in_specs=[
            pl.BlockSpec((tm, d), lambda m: (m, 0)),
            pl.BlockSpec((tm, p2.shape[1]), lambda m: (m, 0)),
            pl.BlockSpec((1, d), full), pl.BlockSpec(wg.shape, full), pl.BlockSpec(wp.shape, full),
        ],
        out_specs=pl.BlockSpec((tm, d), lambda m: (m, 0)),
        out_shape=jax.ShapeDtypeStruct((t, d), F32),
        compiler_params=_params("arbitrary"),
        name="ple",
    )(x2, p2, g, wg, wp)


META_E0, META_E1, META_G0, META_G1, META_R0, META_R1 = 0, 1, 2, 3, 4, 5


def _route_body(lg_ref, meta_ref, cnt_ref, carry_ref, *, tm, ne):
    @pl.when(pl.program_id(0) == 0)
    def _():
        carry_ref[...] = jnp.zeros_like(carry_ref)

    lane = lax.broadcasted_iota(jnp.int32, (tm, LANES), 1)
    lanef = lane.astype(F32)
    lg = jnp.where(lane < ne, lg_ref[...], -jnp.inf)
    m1 = jnp.max(lg, axis=-1, keepdims=True)
    e1 = jnp.min(jnp.where(lg == m1, lanef, float(LANES)), axis=-1, keepdims=True)
    lg2 = jnp.where(lanef == e1, -jnp.inf, lg)
    m2 = jnp.max(lg2, axis=-1, keepdims=True)
    e2 = jnp.min(jnp.where(lg2 == m2, lanef, float(LANES)), axis=-1, keepdims=True)
    ex = jnp.exp(m2 - m1)
    g1 = 1.0 / (1.0 + ex)
    g2 = ex / (1.0 + ex)
    onehot = jnp.where((lanef == e1) | (lanef == e2), 1.0, 0.0)
    row = lax.broadcasted_iota(jnp.int32, (tm, tm), 0)
    col = lax.broadcasted_iota(jnp.int32, (tm, tm), 1)
    before = jnp.dot(jnp.where(col < row, 1.0, 0.0).astype(BF16), onehot.astype(BF16),
                     preferred_element_type=F32) + carry_ref[...]
    r1 = jnp.sum(jnp.where(lanef == e1, before, 0.0), axis=-1, keepdims=True)
    r2 = jnp.sum(jnp.where(lanef == e2, before, 0.0), axis=-1, keepdims=True)
    carry_ref[...] += jnp.sum(onehot, axis=0, keepdims=True)
    meta = jnp.zeros((tm, LANES), F32)
    for pos, val in ((META_E0, e1), (META_E1, e2), (META_G0, g1), (META_G1, g2), (META_R0, r1), (META_R1, r2)):
        meta = jnp.where(lane == pos, val, meta)
    meta_ref[...] = meta
    cnt_ref[...] = jnp.broadcast_to(carry_ref[...], cnt_ref.shape)


def route(logits, *, tm):
    t = logits.shape[0]
    return pl.pallas_call(
        functools.partial(_route_body, tm=tm, ne=N_EXPERTS),
        grid=(t // tm,),
        in_specs=[pl.BlockSpec((tm, LANES), lambda m: (m, 0))],
        out_specs=[pl.BlockSpec((tm, LANES), lambda m: (m, 0)),
                   pl.BlockSpec((SUBLANES, LANES), lambda m: (0, 0))],
        out_shape=[jax.ShapeDtypeStruct((t, LANES), F32), jax.ShapeDtypeStruct((SUBLANES, LANES), F32)],
        scratch_shapes=[pltpu.VMEM((1, LANES), F32)],
        compiler_params=_params("arbitrary"),
        name="route",
    )(logits)


def _dispatch_body(dest_ref, hf_ref, xs_in_ref, xs_ref, sem, *, tm, topk):
    del xs_in_ref
    base = pl.program_id(0) * tm * topk

    def row_copy(r, k):
        d = dest_ref[base + topk * r + k]
        return pltpu.make_async_copy(hf_ref.at[pl.ds(r, 1)], xs_ref.at[pl.ds(d, 1)], sem)

    def issue(r, c):
        for k in range(topk):
            row_copy(r, k).start()
        return c

    def drain(r, c):
        for k in range(topk):
            row_copy(r, k).wait()
        return c

    lax.fori_loop(0, tm, issue, 0)
    lax.fori_loop(0, tm, drain, 0)


def dispatch(dest, hf, xs_init, *, tm, topk):
    t, d = hf.shape
    return pl.pallas_call(
        functools.partial(_dispatch_body, tm=tm, topk=topk),
        grid_spec=pltpu.PrefetchScalarGridSpec(
            num_scalar_prefetch=1,
            grid=(t // tm,),
            in_specs=[pl.BlockSpec((tm, d), lambda m, dest: (m, 0)),
                      pl.BlockSpec(memory_space=pl.ANY)],
            out_specs=pl.BlockSpec(memory_space=pl.ANY),
            scratch_shapes=[pltpu.SemaphoreType.DMA],
        ),
        out_shape=jax.ShapeDtypeStruct(xs_init.shape, xs_init.dtype),
        input_output_aliases={2: 0},
        compiler_params=_params("arbitrary"),
        name="moe_dispatch",
    )(dest, hf, xs_init)


def _experts_body(te_ref, xs_ref, w1_ref, w3_ref, w2_ref, y_ref, xb_ref):
    del te_ref
    f = pl.program_id(1)

    @pl.when(f == 0)
    def _():
        xb_ref[...] = xs_ref[...].astype(BF16)

    xb = xb_ref[...]
    a = jnp.dot(xb, w1_ref[0], preferred_element_type=F32)
    b = jnp.dot(xb, w3_ref[0], preferred_element_type=F32)
    g = (a * _sigmoid(a) * b).astype(BF16)
    y = jnp.dot(g, w2_ref[0], preferred_element_type=F32)

    @pl.when(f == 0)
    def _():
        y_ref[...] = y

    @pl.when(f > 0)
    def _():
        y_ref[...] += y


def experts(tile_expert, xs, w1, w3, w2, *, tm, tf):
    n_rows, d = xs.shape
    dff = w1.shape[2]
    return pl.pallas_call(
        _experts_body,
        grid_spec=pltpu.PrefetchScalarGridSpec(
            num_scalar_prefetch=1,
            grid=(n_rows // tm, dff // tf),
            in_specs=[
                pl.BlockSpec((tm, d), lambda i, f, te: (i, 0)),
                pl.BlockSpec((1, d, tf), lambda i, f, te: (te[i], 0, f)),
                pl.BlockSpec((1, d, tf), lambda i, f, te: (te[i], 0, f)),
                pl.BlockSpec((1, tf, d), lambda i, f, te: (te[i], f, 0)),
            ],
            out_specs=pl.BlockSpec((tm, d), lambda i, f, te: (i, 0)),
            scratch_shapes=[pltpu.VMEM((tm, d), BF16)],
        ),
        out_shape=jax.ShapeDtypeStruct((n_rows, d), F32),
        compiler_params=_params("arbitrary", "arbitrary"),
        name="moe_experts",
    )(tile_expert, xs, w1, w3, w2)


def _combine_body(dest_ref, x1_ref, meta_ref, y_ref, o_ref, buf_ref, sem, *, tm, topk):
    base = pl.program_id(0) * tm * topk

    def row_copy(r, k):
        d = dest_ref[base + topk * r + k]
        return pltpu.make_async_copy(y_ref.at[pl.ds(d, 1)], buf_ref.at[k, pl.ds(r, 1)], sem)

    def issue(r, c):
        for k in range(topk):
            row_copy(r, k).start()
        return c

    def drain(r, c):
        for k in range(topk):
            row_copy(r, k).wait()
        return c

    lax.fori_loop(0, tm, issue, 0)
    lax.fori_loop(0, tm, drain, 0)
    meta = meta_ref[...]
    g0 = meta[:, META_G0:META_G0 + 1]
    g1 = meta[:, META_G1:META_G1 + 1]
    o_ref[...] = x1_ref[...] + (g0 * buf_ref[0] + g1 * buf_ref[1])


def combine(dest, x1, meta, y, *, tm, topk):
    t, d = x1.shape
    return pl.pallas_call(
        functools.partial(_combine_body, tm=tm, topk=topk),
        grid_spec=pltpu.PrefetchScalarGridSpec(
            num_scalar_prefetch=1,
            grid=(t // tm,),
            in_specs=[pl.BlockSpec((tm, d), lambda m, dest: (m, 0)),
                      pl.BlockSpec((tm, LANES), lambda m, dest: (m, 0)),
                      pl.BlockSpec(memory_space=pl.ANY)],
            out_specs=pl.BlockSpec((tm, d), lambda m, dest: (m, 0)),
            scratch_shapes=[pltpu.VMEM((topk, tm, d), F32), pltpu.SemaphoreType.DMA],
        ),
        out_shape=jax.ShapeDtypeStruct((t, d), F32),
        compiler_params=_params("arbitrary"),
        name="moe_combine",
    )(dest, x1, meta, y)


def moe_ffn(hf, x1, logits, w1, w3, w2, *, tm_route, tm_rows, tf, tm_move):
    t, d = x1.shape
    topk = 2
    meta, cnt = route(logits, tm=tm_route)
    counts = cnt[0, :N_EXPERTS].astype(jnp.int32)
    padded = ((counts + tm_rows - 1) // tm_rows) * tm_rows
    pad_end = jnp.cumsum(padded)
    pad_start = pad_end - padded
    eidx = meta[:, META_E0:META_E1 + 1].astype(jnp.int32)
    rank = meta[:, META_R0:META_R1 + 1].astype(jnp.int32)
    dest = (pad_start[eidx] + rank).reshape(t * topk)
    n_tiles = -(-(t * topk) // tm_rows) + N_EXPERTS
    tile_start = jnp.arange(n_tiles, dtype=jnp.int32) * tm_rows
    tile_expert = jnp.minimum(jnp.sum(tile_start[:, None] >= pad_end[None, :], axis=1),
                              N_EXPERTS - 1).astype(jnp.int32)
    xs = dispatch(dest, hf, jnp.zeros((n_tiles * tm_rows, d), F32), tm=tm_move, topk=topk)
    y = experts(tile_expert, xs, w1, w3, w2, tm=tm_rows, tf=tf)
    return combine(dest, x1, meta, y, tm=tm_move, topk=topk)


def _tile2(g):
    return jnp.concatenate([g, g]).reshape(1, 2 * g.shape[0])


def kernel(x, p, g_mix, w_in, g_q, g_k, conv_w, conv_b, b_i, b_f, g_h, w_oa, w_ob, w_out, g_ffn, w_d1, w_d3,
           w_d2, w_router, w_e1, w_e3, w_e2, g_ple, w_ple_gate, w_ple_proj):
    batch, seq, d = x.shape
    depth = w_in.shape[0]
    t = batch * seq
    nh = MLSTM_HEADS
    x2 = x.reshape(t, d)
    c_q, c_k, c_v = 0, 512, 1024
    c_qk, c_vm, c_om, c_i, c_f, c_ga, c_gb, c_end = 1536, 2560, 3072, 3584, 3588, 3592, 4616, 5640

    for l in range(depth):
        w = w_in[l]
        w_main = jnp.concatenate([w[:, c_ga:c_gb], w[:, c_gb:c_end], w[:, c_q:c_i]], axis=1).astype(BF16)
        w_if = jnp.pad(w[:, c_i:c_ga], ((0, 0), (0, LANES - 2 * nh)))
        proj, gif = in_proj(x2, g_mix[l].reshape(1, d), w_main, w_if, tm=1024, tn=512)

        ya = moba(proj, _tile2(g_q[l]), _tile2(g_k[l]), batch=batch, seq=seq)

        bias = jnp.concatenate([b_i[l], b_f[l]])
        bias_row = jnp.pad(bias, (0, LANES - 2 * nh)).reshape(1, LANES)
        bias_col = bias.reshape(2 * nh, 1)
        gates_row = gif[:, :2 * nh].reshape(batch, seq, 2 * nh).transpose(0, 2, 1)
        yb = mlstm(proj, gif, gates_row, bias_row, bias_col, conv_w[l], conv_b[l].reshape(1, -1),
                   g_h[l].reshape(1, -1), batch=batch, seq=seq, chunk=256)

        j = l // 2
        moe = l % 2 == 1
        wr = jnp.pad(w_router[j], ((0, 0), (0, LANES - N_EXPERTS))) if moe else None
        outs = merge(ya, yb, proj, x2, w_oa[l].astype(BF16), w_ob[l].astype(BF16), w_out[l].astype(BF16),
                     g_ffn[l].reshape(1, d), wr, tm=512)
        if moe:
            x1, hf, logits = outs
            x2 = moe_ffn(hf, x1, logits, w_e1[j].astype(BF16), w_e3[j].astype(BF16), w_e2[j].astype(BF16),
                         tm_route=512, tm_rows=512, tf=512, tm_move=256)
        else:
            x1, hf = outs
            x2 = dense_ffn(hf, x1, w_d1[j].astype(BF16), w_d3[j].astype(BF16), w_d2[j].astype(BF16),
                           tm=1024, tf=256)

        x2 = ple(x2, p[l].reshape(t, -1), g_ple[l].reshape(1, d), w_ple_gate[l].astype(BF16),
                 w_ple_proj[l].astype(BF16), tm=1024)
    return x2.reshape(batch, seq, d)
```

```python
import functools

import jax
import jax.numpy as jnp
from jax import lax
from jax.experimental import pallas as pl
from jax.experimental.pallas import tpu as pltpu

F32 = jnp.float32
BF16 = jnp.bfloat16
HIGHEST = lax.Precision.HIGHEST

RMS_EPS = 1e-6
LANES = 128
SUBLANES = 8

MOBA_HEADS = 8
MOBA_HEAD_DIM = 64
MOBA_BLOCK = 256
MOBA_TOPK = 3
MLSTM_HEADS = 4
MLSTM_DIM = 128
CONV_WIDTH = 4
N_EXPERTS = 8

COL_GA, COL_GB = 0, 8
COL_QA, COL_KA, COL_VA = 16, 20, 24
COL_QM, COL_KM, COL_VM, COL_OM = 28, 32, 36, 40
N_PROJ = 44 * LANES

VMEM_LIMIT = 56 * 1024 * 1024


def _params(*sem):
    return pltpu.CompilerParams(dimension_semantics=sem, vmem_limit_bytes=VMEM_LIMIT)


def _sigmoid(x):
    return 1.0 / (1.0 + jnp.exp(-x))


def _rms(x, g):
    return x * lax.rsqrt(jnp.mean(x * x, axis=-1, keepdims=True) + RMS_EPS) * g


def _split_bf16(x):
    hi = x.astype(BF16)
    return hi, (x - hi.astype(F32)).astype(BF16)


def _dot_split(a, b):
    ah, al = _split_bf16(a)
    bh, bl = _split_bf16(b)
    return (jnp.dot(ah, bh, preferred_element_type=F32) + jnp.dot(ah, bl, preferred_element_type=F32)
            + jnp.dot(al, bh, preferred_element_type=F32))


def _nt_dot(a, b, **kw):
    return lax.dot_general(a, b, (((1,), (1,)), ((), ())), preferred_element_type=F32, **kw)


def _in_proj_body(x_ref, g_ref, w_ref, wif_ref, o_ref, oif_ref, h_ref):
    @pl.when(pl.program_id(1) == 0)
    def _():
        h = _rms(x_ref[...], g_ref[...]).astype(BF16)
        h_ref[...] = h
        oif_ref[...] = jnp.dot(h, wif_ref[...], preferred_element_type=F32)

    o_ref[...] = jnp.dot(h_ref[...], w_ref[...], preferred_element_type=F32).astype(o_ref.dtype)


def in_proj(x2, g, w, wif, *, tm, tn):
    t, d = x2.shape
    n = w.shape[1]
    return pl.pallas_call(
        _in_proj_body,
        grid=(t // tm, n // tn),
        in_specs=[
            pl.BlockSpec((tm, d), lambda m, j: (m, 0)),
            pl.BlockSpec((1, d), lambda m, j: (0, 0)),
            pl.BlockSpec((d, tn), lambda m, j: (0, j)),
            pl.BlockSpec((d, LANES), lambda m, j: (0, 0)),
        ],
        out_specs=[
            pl.BlockSpec((tm, tn), lambda m, j: (m, j)),
            pl.BlockSpec((tm, LANES), lambda m, j: (m, 0)),
        ],
        out_shape=[jax.ShapeDtypeStruct((t, n), BF16), jax.ShapeDtypeStruct((t, LANES), F32)],
        scratch_shapes=[pltpu.VMEM((tm, d), BF16)],
        compiler_params=_params("arbitrary", "arbitrary"),
        name="in_proj",
    )(x2, g, w, wif)


MASK_BIAS = -1e30
LOG2_E = 1.4426950408889634


def _moba_body(q_ref, k_ref, v_ref, gq_ref, gk_ref, o_ref,
               kn_ref, vt_ref, kmean_ref, qaug_ref, s_ref, m_ref, alpha_ref, l_ref, acc_ref,
               *, nb, blk, dh, topk, nheads):
    i = pl.program_id(1)
    lane = lax.broadcasted_iota(jnp.int32, (1, LANES), 1)
    head0 = lane < dh

    def head_rms(x, g):
        x2 = x * x
        s0 = jnp.sum(jnp.where(head0, x2, 0.0), axis=-1, keepdims=True)
        s1 = jnp.sum(jnp.where(head0, 0.0, x2), axis=-1, keepdims=True)
        ms = jnp.where(head0, s0, s1) * (1.0 / dh)
        return x * lax.rsqrt(ms + RMS_EPS) * g

    @pl.when(i == 0)
    def _():
        def prep(j, c):
            r0 = pl.multiple_of(j * blk, blk)
            onehot = jnp.where(lane == dh + j, 1.0, 0.0)
            for p in range(nheads // 2):
                cols = slice(p * LANES, (p + 1) * LANES)
                kn = head_rms(k_ref[pl.ds(r0, blk), cols].astype(F32), gk_ref[...])
                for hh, kh in ((0, kn), (1, pltpu.roll(kn, dh, axis=1))):
                    h = 2 * p + hh
                    kmean_ref[h, pl.ds(j, 1), :] = jnp.mean(jnp.where(head0, kh, 0.0), axis=0, keepdims=True)
                    kn_ref[h, pl.ds(r0, blk), :] = jnp.where(head0, kh, onehot).astype(BF16)
                vt_ref[cols, pl.ds(r0, blk)] = v_ref[pl.ds(r0, blk), cols].astype(F32).T.astype(BF16)
            return c

        lax.fori_loop(0, nb, prep, 0)

    jidx = lax.broadcasted_iota(jnp.int32, (nb, blk), 0)
    key_i = lax.broadcasted_iota(jnp.int32, (blk, blk), 0)
    qry_i = lax.broadcasted_iota(jnp.int32, (blk, blk), 1)
    causal = key_i <= qry_i
    r_own = pl.multiple_of(i * blk, blk)
    qk_scale = dh ** -0.5 * LOG2_E
    for p in range(nheads // 2):
        cols = slice(p * LANES, (p + 1) * LANES)
        qn_t = head_rms(q_ref[:, cols].astype(F32), gq_ref[...]).T
        for hh in range(2):
            h = 2 * p + hh
            q_t = qn_t[hh * dh:(hh + 1) * dh, :]
            gate = jnp.dot(kmean_ref[h], jnp.concatenate([q_t, jnp.zeros((LANES - dh, blk), F32)], axis=0),
                           precision=HIGHEST, preferred_element_type=F32)
            rank = jnp.zeros((nb, blk), F32)
            for jp in range(nb):
                row = gate[jp:jp + 1, :]
                beats = (row > gate) | ((row == gate) & (jidx > jp))
                rank = rank + jnp.where(beats, jnp.where(jp < i, 1.0, 0.0), 0.0)
            sel = ((rank < topk) & (jidx < i)) | (jidx == i)
            bias = jnp.where(sel, 0.0, MASK_BIAS)
            qaug = jnp.concatenate([q_t * qk_scale, bias, jnp.zeros((LANES - dh - nb, blk), F32)],
                                   axis=0).astype(BF16)
            qaug_ref[h] = qaug
            st = jnp.dot(kn_ref[h, pl.ds(r_own, blk), :], qaug, preferred_element_type=F32)
            st = jnp.where(causal, st, -jnp.inf)
            s_ref[h] = st
            m_ref[h] = jnp.max(st, axis=0, keepdims=True)
            alpha_ref[h] = jnp.zeros((1, blk), F32)
            l_ref[h] = jnp.zeros((1, blk), F32)
            acc_ref[h] = jnp.zeros((dh, blk), F32)

    def finish_block(h, r_blk):
        pr = jnp.exp2(s_ref[h] - m_ref[h])
        alpha = alpha_ref[h]
        l_ref[h] = alpha * l_ref[h] + jnp.sum(pr, axis=0, keepdims=True)
        acc_ref[h] = alpha * acc_ref[h] + jnp.dot(vt_ref[h * dh:(h + 1) * dh, pl.ds(r_blk, blk)],
                                                  pr.astype(BF16), preferred_element_type=F32)

    def body(j, r_prev):
        r0 = pl.multiple_of(j * blk, blk)
        r_prev = pl.multiple_of(r_prev, blk)
        for h in range(nheads):
            finish_block(h, r_prev)
            st = jnp.dot(kn_ref[h, pl.ds(r0, blk), :], qaug_ref[h], preferred_element_type=F32)
            m_old = m_ref[h]
            m_new = jnp.maximum(m_old, jnp.max(st, axis=0, keepdims=True))
            s_ref[h] = st
            alpha_ref[h] = jnp.exp2(m_old - m_new)
            m_ref[h] = m_new
        return r0

    r_last = lax.fori_loop(0, i, body, r_own)
    r_last = pl.multiple_of(r_last, blk)
    for h in range(nheads):
        finish_block(h, r_last)
    for p in range(nheads // 2):
        ot = jnp.concatenate([acc_ref[2 * p] / l_ref[2 * p], acc_ref[2 * p + 1] / l_ref[2 * p + 1]], axis=0)
        o_ref[:, p * LANES:(p + 1) * LANES] = ot.T.astype(o_ref.dtype)


def moba(proj, gq2, gk2, *, batch, seq):
    nb = seq // MOBA_BLOCK
    blk = MOBA_BLOCK
    dh = MOBA_HEAD_DIM
    nheads = MOBA_HEADS
    width = nheads * dh
    wb = width // LANES
    assert dh + nb <= LANES and 2 * dh == LANES
    body = functools.partial(_moba_body, nb=nb, blk=blk, dh=dh, topk=MOBA_TOPK, nheads=nheads)
    return pl.pallas_call(
        body,
        grid=(batch, nb),
        in_specs=[
            pl.BlockSpec((blk, width), lambda b, i: (b * nb + i, COL_QA // wb)),
            pl.BlockSpec((seq, width), lambda b, i: (b, COL_KA // wb)),
            pl.BlockSpec((seq, width), lambda b, i: (b, COL_VA // wb)),
            pl.BlockSpec((1, LANES), lambda b, i: (0, 0)),
            pl.BlockSpec((1, LANES), lambda b, i: (0, 0)),
        ],
        out_specs=pl.BlockSpec((blk, width), lambda b, i: (b * nb + i, 0)),
        out_shape=jax.ShapeDtypeStruct((batch * seq, width), BF16),
        scratch_shapes=[
            pltpu.VMEM((nheads, seq, LANES), BF16),
            pltpu.VMEM((width, seq), BF16),
            pltpu.VMEM((nheads, nb, LANES), F32),
            pltpu.VMEM((nheads, LANES, blk), BF16),
            pltpu.VMEM((nheads, blk, blk), F32),
            pltpu.VMEM((nheads, 1, blk), F32),
            pltpu.VMEM((nheads, 1, blk), F32),
            pltpu.VMEM((nheads, 1, blk), F32),
            pltpu.VMEM((nheads, dh, blk), F32),
        ],
        compiler_params=_params("arbitrary", "arbitrary"),
        name="moba",
    )(proj, proj, proj, gq2, gk2)


def _log_sigmoid(x):
    return jnp.minimum(x, 0.0) - jnp.log(1.0 + jnp.exp(-jnp.abs(x)))


def _dot_tri(tri, x, tri_left):
    out = None
    for _ in range(3):
        piece = x.astype(BF16)
        x = x - piece.astype(F32)
        term = (jnp.dot(tri, piece, preferred_element_type=F32) if tri_left
                else jnp.dot(piece, tri, preferred_element_type=F32))
        out = term if out is None else out + term
    return out


def _mlstm_body(qr_ref, kr_ref, v_ref, og_ref, gcol_ref, grow_ref, brow_ref, bcol_ref,
                cwq_ref, cwk_ref, cbq_ref, cbk_ref, gh_ref, o_ref,
                qx_ref, kx_ref, c_ref, m_ref, *, chunk, dk, nh):
    L = chunk
    width = nh * dk

    @pl.when(pl.program_id(1) == 0)
    def _():
        qx_ref[0:SUBLANES, :] = jnp.zeros((SUBLANES, width), F32)
        kx_ref[0:SUBLANES, :] = jnp.zeros((SUBLANES, width), F32)
        c_ref[...] = jnp.zeros_like(c_ref)
        m_ref[...] = jnp.zeros_like(m_ref)

    qx_ref[SUBLANES:SUBLANES + L, :] = qr_ref[...].astype(F32)
    kx_ref[SUBLANES:SUBLANES + L, :] = kr_ref[...].astype(F32)

    def conv_silu(x_ref, w_ref, b_ref):
        acc = b_ref[...] + w_ref[0:1, :] * x_ref[pl.ds(SUBLANES - CONV_WIDTH + 1, L), :]
        for j in range(1, CONV_WIDTH):
            acc = acc + w_ref[j:j + 1, :] * x_ref[pl.ds(SUBLANES - CONV_WIDTH + 1 + j, L), :]
        return acc * _sigmoid(acc)

    q_all = conv_silu(qx_ref, cwq_ref, cbq_ref)
    k_all = conv_silu(kx_ref, cwk_ref, cbk_ref) * (dk ** -0.5)
    qx_ref[0:SUBLANES, :] = qx_ref[L:L + SUBLANES, :]
    kx_ref[0:SUBLANES, :] = kx_ref[L:L + SUBLANES, :]

    pre_col = gcol_ref[...] + brow_ref[...]
    pre_row = grow_ref[0] + bcol_ref[...]
    t_i = lax.broadcasted_iota(jnp.int32, (L, L), 0)
    s_i = lax.broadcasted_iota(jnp.int32, (L, L), 1)
    tril = s_i <= t_i
    bcum_cols = _dot_tri(jnp.where(tril, 1.0, 0.0).astype(BF16), _log_sigmoid(pre_col), True)
    bcum_rows = _dot_tri(jnp.where(t_i <= s_i, 1.0, 0.0).astype(BF16), _log_sigmoid(pre_row), False)
    ones = jnp.ones((L, dk), BF16)

    for h in range(nh):
        cols = slice(h * dk, (h + 1) * dk)
        q = q_all[:, cols]
        k = k_all[:, cols]
        i_col = pre_col[:, h:h + 1]
        i_row = pre_row[h:h + 1, :]
        bcum_col = bcum_cols[:, nh + h:nh + h + 1]
        bcum_row = bcum_rows[nh + h:nh + h + 1, :]

        m_prev = m_ref[h, 0:1, 0:1]
        a_col = bcum_col + m_prev
        dmat = jnp.where(tril, bcum_col - bcum_row + i_row, -jnp.inf)
        m_t = jnp.maximum(a_col, jnp.max(dmat, axis=-1, keepdims=True))
        dw = jnp.exp(dmat - m_t)
        aw = jnp.exp(a_col - m_t)

        qb = q.astype(BF16)
        kb = k.astype(BF16)
        v_aug = jnp.concatenate([v_ref[:, cols], ones], axis=-1)
        sqk = _nt_dot(qb, kb) * dw
        num_aug = (aw * jnp.dot(qb, c_ref[h].astype(BF16), preferred_element_type=F32)
                   + jnp.dot(sqk.astype(BF16), v_aug, preferred_element_type=F32))
        den = num_aug[:, dk:dk + 1]
        hc = num_aug[:, 0:dk] / jnp.maximum(jnp.abs(den), jnp.exp(-m_t))

        b_last = bcum_col[L - 1:L, :]
        g_col = b_last - bcum_col + i_col
        m_new = jnp.maximum(b_last + m_prev, jnp.max(g_col, axis=0, keepdims=True))
        w_c = jnp.exp(b_last + m_prev - m_new)
        kw_t = (k * jnp.exp(g_col - m_new)).T.astype(BF16)
        c_ref[h] = w_c * c_ref[h] + jnp.dot(kw_t, v_aug, preferred_element_type=F32)
        m_ref[h] = jnp.broadcast_to(m_new, (1, LANES))

        o_ref[:, cols] = (_rms(hc, gh_ref[...]) * _sigmoid(og_ref[:, cols].astype(F32))).astype(o_ref.dtype)


def mlstm(proj, gates_col, gates_row, bias_row, bias_col, conv_w, conv_b, gh, *, batch, seq, chunk):
    nh = MLSTM_HEADS
    dk = MLSTM_DIM
    width = nh * dk
    wb = width // LANES
    nc = seq // chunk
    body = functools.partial(_mlstm_body, chunk=chunk, dk=dk, nh=nh)

    def rows(col0):
        return pl.BlockSpec((chunk, width), lambda b, c: (b * nc + c, col0 // wb))

    return pl.pallas_call(
        body,
        grid=(batch, nc),
        in_specs=[
            rows(COL_QM), rows(COL_KM), rows(COL_VM), rows(COL_OM),
            pl.BlockSpec((chunk, LANES), lambda b, c: (b * nc + c, 0)),
            pl.BlockSpec((1, SUBLANES, chunk), lambda b, c: (b, 0, c)),
            pl.BlockSpec((1, LANES), lambda b, c: (0, 0)),
            pl.BlockSpec((SUBLANES, 1), lambda b, c: (0, 0)),
            pl.BlockSpec((CONV_WIDTH, width), lambda b, c: (0, 0)),
            pl.BlockSpec((CONV_WIDTH, width), lambda b, c: (0, 1)),
            pl.BlockSpec((1, width), lambda b, c: (0, 0)),
            pl.BlockSpec((1, width), lambda b, c: (0, 1)),
            pl.BlockSpec((1, LANES), lambda b, c: (0, 0)),
        ],
        out_specs=pl.BlockSpec((chunk, width), lambda b, c: (b * nc + c, 0)),
        out_shape=jax.ShapeDtypeStruct((batch * seq, width), BF16),
        scratch_shapes=[
            pltpu.VMEM((chunk + 2 * SUBLANES, width), F32),
            pltpu.VMEM((chunk + 2 * SUBLANES, width), F32),
            pltpu.VMEM((nh, dk, 2 * dk), F32),
            pltpu.VMEM((nh, 1, LANES), F32),
        ],
        compiler_params=_params("arbitrary", "arbitrary"),
        name="mlstm",
    )(proj, proj, proj, proj, gates_col, gates_row, bias_row, bias_col,
      conv_w, conv_w, conv_b, conv_b, gh)


def _merge_body(ya_ref, yb_ref, ga_ref, gb_ref, x_ref, woa_ref, wob_ref, wout_ref, gffn_ref, *rest, moe):
    a = jnp.dot(ya_ref[...], woa_ref[...], preferred_element_type=F32)
    b = jnp.dot(yb_ref[...], wob_ref[...], preferred_element_type=F32)
    mixed = _sigmoid(ga_ref[...].astype(F32)) * a + _sigmoid(gb_ref[...].astype(F32)) * b
    x1 = x_ref[...] + jnp.dot(mixed.astype(BF16), wout_ref[...], preferred_element_type=F32)
    hf = _rms(x1, gffn_ref[...])
    if moe:
        wr_ref, x1_ref, hf_ref, lg_ref = rest
        hf_ref[...] = hf
        lg_ref[...] = _dot_split(hf, wr_ref[...])
    else:
        x1_ref, hf_ref = rest
        hf_ref[...] = hf.astype(BF16)
    x1_ref[...] = x1


def merge(ya, yb, proj, x2, woa, wob, wout, gffn, wr, *, tm):
    t, d = x2.shape
    moe = wr is not None
    full = lambda m: (0, 0)
    in_specs = [
        pl.BlockSpec((tm, ya.shape[1]), lambda m: (m, 0)),
        pl.BlockSpec((tm, yb.shape[1]), lambda m: (m, 0)),
        pl.BlockSpec((tm, d), lambda m: (m, COL_GA * LANES // d)),
        pl.BlockSpec((tm, d), lambda m: (m, COL_GB * LANES // d)),
        pl.BlockSpec((tm, d), lambda m: (m, 0)),
        pl.BlockSpec(woa.shape, full), pl.BlockSpec(wob.shape, full), pl.BlockSpec(wout.shape, full),
        pl.BlockSpec((1, d), full),
    ]
    args = [ya, yb, proj, proj, x2, woa, wob, wout, gffn]
    out_specs = [pl.BlockSpec((tm, d), lambda m: (m, 0)), pl.BlockSpec((tm, d), lambda m: (m, 0))]
    out_shape = [jax.ShapeDtypeStruct((t, d), F32), jax.ShapeDtypeStruct((t, d), F32 if moe else BF16)]
    if moe:
        in_specs.append(pl.BlockSpec(wr.shape, full))
        args.append(wr)
        out_specs.append(pl.BlockSpec((tm, LANES), lambda m: (m, 0)))
        out_shape.append(jax.ShapeDtypeStruct((t, LANES), F32))
    return pl.pallas_call(
        functools.partial(_merge_body, moe=moe),
        grid=(t // tm,),
        in_specs=in_specs, out_specs=out_specs, out_shape=out_shape,
        compiler_params=_params("arbitrary"),
        name="merge_moe" if moe else "merge",
    )(*args)


def _swiglu(x, w1_ref, w3_ref, w2_ref, g_ref, fc):
    dff = g_ref.shape[1]
    for f0 in range(0, dff, fc):
        a = jnp.dot(x, w1_ref[:, f0:f0 + fc], preferred_element_type=F32)
        b = jnp.dot(x, w3_ref[:, f0:f0 + fc], preferred_element_type=F32)
        g_ref[:, f0:f0 + fc] = (a * _sigmoid(a) * b).astype(BF16)
    return jnp.dot(g_ref[...], w2_ref[...], preferred_element_type=F32)


def _ffn_body(hf_ref, x1_ref, w1_ref, w3_ref, w2_ref, o_ref, g_ref, *, fc):
    o_ref[...] = x1_ref[...] + _swiglu(hf_ref[...], w1_ref, w3_ref, w2_ref, g_ref, fc)


def dense_ffn(hf, x1, w1, w3, w2, *, tm, fc):
    t, d = x1.shape
    dff = w1.shape[1]
    resident = dict(pipeline_mode=pl.Buffered(1))
    return pl.pallas_call(
        functools.partial(_ffn_body, fc=fc),
        grid=(t // tm,),
        in_specs=[
            pl.BlockSpec((tm, d), lambda m: (m, 0)),
            pl.BlockSpec((tm, d), lambda m: (m, 0)),
            pl.BlockSpec((d, dff), lambda m: (0, 0), **resident),
            pl.BlockSpec((d, dff), lambda m: (0, 0), **resident),
            pl.BlockSpec((dff, d), lambda m: (0, 0), **resident),
        ],
        out_specs=pl.BlockSpec((tm, d), lambda m: (m, 0)),
        out_shape=jax.ShapeDtypeStruct((t, d), F32),
        scratch_shapes=[pltpu.VMEM((tm, dff), BF16)],
        compiler_params=_params("arbitrary"),
        name="dense_ffn",
    )(hf, x1, w1, w3, w2)


def _ple_body(x_ref, p_ref, g_ref, wg_ref, wp_ref, o_ref):
    x = x_ref[...]
    gate = _sigmoid(jnp.dot(_rms(x, g_ref[...]).astype(BF16), wg_ref[...], preferred_element_type=F32))
    emb = jnp.dot(p_ref[...].astype(BF16), wp_ref[...], preferred_element_type=F32)
    o_ref[...] = x + gate * emb


def ple(x2, p2, g, wg, wp, *, tm):
    t, d = x2.shape
    full = lambda m: (0, 0)
    return pl.pallas_call(
        _ple_body,
        grid=(t // tm,),
        in_specs=[
            pl.BlockSpec((tm, d), lambda m: (m, 0)),
            pl.BlockSpec((tm, p2.shape[1]), lambda m: (m, 0)),
            pl.BlockSpec((1, d), full), pl.BlockSpec(wg.shape, full), pl.BlockSpec(wp.shape, full),
        ],
        out_specs=pl.BlockSpec((tm, d), lambda m: (m, 0)),
        out_shape=jax.ShapeDtypeStruct((t, d), F32),
        compiler_params=_params("arbitrary"),
        name="ple",
    )(x2, p2, g, wg, wp)


META_E0, META_E1, META_G0, META_G1, META_R0, META_R1 = 0, 1, 2, 3, 4, 5


def _route_body(lg_ref, meta_ref, cnt_ref, carry_ref, *, tm, ne):
    @pl.when(pl.program_id(0) == 0)
    def _():
        carry_ref[...] = jnp.zeros_like(carry_ref)

    lane = lax.broadcasted_iota(jnp.int32, (tm, LANES), 1)
    lanef = lane.astype(F32)
    lg = jnp.where(lane < ne, lg_ref[...], -jnp.inf)
    m1 = jnp.max(lg, axis=-1, keepdims=True)
    e1 = jnp.min(jnp.where(lg == m1, lanef, float(LANES)), axis=-1, keepdims=True)
    lg2 = jnp.where(lanef == e1, -jnp.inf, lg)
    m2 = jnp.max(lg2, axis=-1, keepdims=True)
    e2 = jnp.min(jnp.where(lg2 == m2, lanef, float(LANES)), axis=-1, keepdims=True)
    ex = jnp.exp(m2 - m1)
    g1 = 1.0 / (1.0 + ex)
    g2 = ex / (1.0 + ex)
    onehot = jnp.where((lanef == e1) | (lanef == e2), 1.0, 0.0)
    row = lax.broadcasted_iota(jnp.int32, (tm, tm), 0)
    col = lax.broadcasted_iota(jnp.int32, (tm, tm), 1)
    before = jnp.dot(jnp.where(col < row, 1.0, 0.0).astype(BF16), onehot.astype(BF16),
                     preferred_element_type=F32) + carry_ref[...]
    r1 = jnp.sum(jnp.where(lanef == e1, before, 0.0), axis=-1, keepdims=True)
    r2 = jnp.sum(jnp.where(lanef == e2, before, 0.0), axis=-1, keepdims=True)
    carry_ref[...] += jnp.sum(onehot, axis=0, keepdims=True)
    meta = jnp.zeros((tm, LANES), F32)
    for pos, val in ((META_E0, e1), (META_E1, e2), (META_G0, g1), (META_G1, g2), (META_R0, r1), (META_R1, r2)):
        meta = jnp.where(lane == pos, val, meta)
    meta_ref[...] = meta
    cnt_ref[...] = jnp.broadcast_to(carry_ref[...], cnt_ref.shape)


def route(logits, *, tm):
    t = logits.shape[0]
    return pl.pallas_call(
        functools.partial(_route_body, tm=tm, ne=N_EXPERTS),
        grid=(t // tm,),
        in_specs=[pl.BlockSpec((tm, LANES), lambda m: (m, 0))],
        out_specs=[pl.BlockSpec((tm, LANES), lambda m: (m, 0)),
                   pl.BlockSpec((SUBLANES, LANES), lambda m: (0, 0))],
        out_shape=[jax.ShapeDtypeStruct((t, LANES), F32), jax.ShapeDtypeStruct((SUBLANES, LANES), F32)],
        scratch_shapes=[pltpu.VMEM((1, LANES), F32)],
        compiler_params=_params("arbitrary"),
        name="route",
    )(logits)


def _dispatch_body(dest_ref, hf_ref, xs_in_ref, xs_ref, sem, *, tm, topk):
    del xs_in_ref
    base = pl.program_id(0) * tm * topk

    def row_copy(r, k):
        d = dest_ref[base + topk * r + k]
        return pltpu.make_async_copy(hf_ref.at[pl.ds(r, 1)], xs_ref.at[pl.ds(d, 1)], sem)

    def issue(r, c):
        for k in range(topk):
            row_copy(r, k).start()
        return c

    def drain(r, c):
        for k in range(topk):
            row_copy(r, k).wait()
        return c

    lax.fori_loop(0, tm, issue, 0)
    lax.fori_loop(0, tm, drain, 0)


def dispatch(dest, hf, xs_init, *, tm, topk):
    t, d = hf.shape
    return pl.pallas_call(
        functools.partial(_dispatch_body, tm=tm, topk=topk),
        grid_spec=pltpu.PrefetchScalarGridSpec(
            num_scalar_prefetch=1,
            grid=(t // tm,),
            in_specs=[pl.BlockSpec((tm, d), lambda m, dest: (m, 0)),
                      pl.BlockSpec(memory_space=pl.ANY)],
            out_specs=pl.BlockSpec(memory_space=pl.ANY),
            scratch_shapes=[pltpu.SemaphoreType.DMA],
        ),
        out_shape=jax.ShapeDtypeStruct(xs_init.shape, xs_init.dtype),
        input_output_aliases={2: 0},
        compiler_params=_params("arbitrary"),
        name="moe_dispatch",
    )(dest, hf, xs_init)


def _experts_body(te_ref, na_ref, xs_ref, w1_ref, w3_ref, w2_ref, y_ref, xb_ref, g_ref, *, fc):
    del te_ref
    f = pl.program_id(1)

    @pl.when(pl.program_id(0) >= na_ref[0])
    def _():
        y_ref[...] = jnp.zeros_like(y_ref)

    @pl.when(pl.program_id(0) < na_ref[0])
    def _():
        @pl.when(f == 0)
        def _():
            xb_ref[...] = xs_ref[...].astype(BF16)

        y = _swiglu(xb_ref[...], w1_ref.at[0], w3_ref.at[0], w2_ref.at[0], g_ref, fc)

        @pl.when(f == 0)
        def _():
            y_ref[...] = y

        @pl.when(f > 0)
        def _():
            y_ref[...] += y


def experts(tile_expert, n_active, xs, w1, w3, w2, *, tm, tf, fc):
    n_rows, d = xs.shape
    dff = w1.shape[2]
    row_tile = lambda i, f, te, na: (jnp.minimum(i, na[0] - 1), 0)
    ftile = lambda i, f, na: jnp.where(i < na[0], f, dff // tf - 1)
    return pl.pallas_call(
        functools.partial(_experts_body, fc=fc),
        grid_spec=pltpu.PrefetchScalarGridSpec(
            num_scalar_prefetch=2,
            grid=(n_rows // tm, dff // tf),
            in_specs=[
                pl.BlockSpec((tm, d), row_tile),
                pl.BlockSpec((1, d, tf), lambda i, f, te, na: (te[i], 0, ftile(i, f, na))),
                pl.BlockSpec((1, d, tf), lambda i, f, te, na: (te[i], 0, ftile(i, f, na))),
                pl.BlockSpec((1, tf, d), lambda i, f, te, na: (te[i], ftile(i, f, na), 0)),
            ],
            out_specs=pl.BlockSpec((tm, d), lambda i, f, te, na: (i, 0)),
            scratch_shapes=[pltpu.VMEM((tm, d), BF16), pltpu.VMEM((tm, tf), BF16)],
        ),
        out_shape=jax.ShapeDtypeStruct((n_rows, d), F32),
        compiler_params=_params("arbitrary", "arbitrary"),
        name="moe_experts",
    )(tile_expert, n_active, xs, w1, w3, w2)


def _combine_body(dest_ref, x1_ref, meta_ref, y_ref, o_ref, buf_ref, sem, *, tm, topk):
    base = pl.program_id(0) * tm * topk

    def row_copy(r, k):
        d = dest_ref[base + topk * r + k]
        return pltpu.make_async_copy(y_ref.at[pl.ds(d, 1)], buf_ref.at[k, pl.ds(r, 1)], sem)

    def issue(r, c):
        for k in range(topk):
            row_copy(r, k).start()
        return c

    def drain(r, c):
        for k in range(topk):
            row_copy(r, k).wait()
        return c

    lax.fori_loop(0, tm, issue, 0)
    lax.fori_loop(0, tm, drain, 0)
    meta = meta_ref[...]
    g0 = meta[:, META_G0:META_G0 + 1]
    g1 = meta[:, META_G1:META_G1 + 1]
    o_ref[...] = x1_ref[...] + (g0 * buf_ref[0] + g1 * buf_ref[1])


def combine(dest, x1, meta, y, *, tm, topk):
    t, d = x1.shape
    return pl.pallas_call(
        functools.partial(_combine_body, tm=tm, topk=topk),
        grid_spec=pltpu.PrefetchScalarGridSpec(
            num_scalar_prefetch=1,
            grid=(t // tm,),
            in_specs=[pl.BlockSpec((tm, d), lambda m, dest: (m, 0)),
                      pl.BlockSpec((tm, LANES), lambda m, dest: (m, 0)),
                      pl.BlockSpec(memory_space=pl.ANY)],
            out_specs=pl.BlockSpec((tm, d), lambda m, dest: (m, 0)),
            scratch_shapes=[pltpu.VMEM((topk, tm, d), F32), pltpu.SemaphoreType.DMA],
        ),
        out_shape=jax.ShapeDtypeStruct((t, d), F32),
        compiler_params=_params("arbitrary"),
        name="moe_combine",
    )(dest, x1, meta, y)


def moe_ffn(hf, x1, logits, w1, w3, w2, *, tm_route, tm_rows, tf, tm_move):
    t, d = x1.shape
    topk = 2
    meta, cnt = route(logits, tm=tm_route)
    counts = cnt[0, :N_EXPERTS].astype(jnp.int32)
    padded = ((counts + tm_rows - 1) // tm_rows) * tm_rows
    pad_end = jnp.cumsum(padded)
    pad_start = pad_end - padded
    eidx = meta[:, META_E0:META_E1 + 1].astype(jnp.int32)
    rank = meta[:, META_R0:META_R1 + 1].astype(jnp.int32)
    dest = (pad_start[eidx] + rank).reshape(t * topk)
    n_tiles = -(-(t * topk) // tm_rows) + N_EXPERTS
    tile_start = jnp.arange(n_tiles, dtype=jnp.int32) * tm_rows
    tile_expert = jnp.minimum(jnp.sum(tile_start[:, None] >= pad_end[None, :], axis=1),
                              N_EXPERTS - 1).astype(jnp.int32)
    xs = dispatch(dest, hf, jnp.zeros((n_tiles * tm_rows, d), F32), tm=tm_move, topk=topk)
    n_active = (pad_end[N_EXPERTS - 1:] // tm_rows).astype(jnp.int32)
    y = experts(tile_expert, n_active, xs, w1, w3, w2, tm=tm_rows, tf=tf, fc=256)
    return combine(dest, x1, meta, y, tm=tm_move, topk=topk)


def _tile2(g):
    return jnp.concatenate([g, g]).reshape(1, 2 * g.shape[0])


def kernel(x, p, g_mix, w_in, g_q, g_k, conv_w, conv_b, b_i, b_f, g_h, w_oa, w_ob, w_out, g_ffn, w_d1, w_d3,
           w_d2, w_router, w_e1, w_e3, w_e2, g_ple, w_ple_gate, w_ple_proj):
    batch, seq, d = x.shape
    depth = w_in.shape[0]
    t = batch * seq
    nh = MLSTM_HEADS
    x2 = x.reshape(t, d)
    c_q, c_k, c_v = 0, 512, 1024
    c_qk, c_vm, c_om, c_i, c_f, c_ga, c_gb, c_end = 1536, 2560, 3072, 3584, 3588, 3592, 4616, 5640

    for l in range(depth):
        w = w_in[l]
        w_main = jnp.concatenate([w[:, c_ga:c_gb], w[:, c_gb:c_end], w[:, c_q:c_i]], axis=1).astype(BF16)
        w_if = jnp.pad(w[:, c_i:c_ga], ((0, 0), (0, LANES - 2 * nh))).astype(BF16)
        proj, gif = in_proj(x2, g_mix[l].reshape(1, d), w_main, w_if, tm=2048, tn=512)

        ya = moba(proj, _tile2(g_q[l]), _tile2(g_k[l]), batch=batch, seq=seq)

        bias = jnp.concatenate([b_i[l], b_f[l]])
        bias_row = jnp.pad(bias, (0, LANES - 2 * nh)).reshape(1, LANES)
        bias_col = bias.reshape(2 * nh, 1)
        gates_row = gif[:, :2 * nh].reshape(batch, seq, 2 * nh).transpose(0, 2, 1)
        yb = mlstm(proj, gif, gates_row, bias_row, bias_col, conv_w[l], conv_b[l].reshape(1, -1),
                   g_h[l].reshape(1, -1), batch=batch, seq=seq, chunk=256)

        j = l // 2
        moe = l % 2 == 1
        wr = jnp.pad(w_router[j], ((0, 0), (0, LANES - N_EXPERTS))) if moe else None
        outs = merge(ya, yb, proj, x2, w_oa[l].astype(BF16), w_ob[l].astype(BF16), w_out[l].astype(BF16),
                     g_ffn[l].reshape(1, d), wr, tm=512)
        if moe:
            x1, hf, logits = outs
            x2 = moe_ffn(hf, x1, logits, w_e1[j].astype(BF16), w_e3[j].astype(BF16), w_e2[j].astype(BF16),
                         tm_route=512, tm_rows=512, tf=1792, tm_move=256)
        else:
            x1, hf = outs
            x2 = dense_ffn(hf, x1, w_d1[j].astype(BF16), w_d3[j].astype(BF16), w_d2[j].astype(BF16),
                           tm=1024, fc=256)

        x2 = ple(x2, p[l].reshape(t, -1), g_ple[l].reshape(1, d), w_ple_gate[l].astype(BF16),
                 w_ple_proj[l].astype(BF16), tm=1024)
    return x2.reshape(batch, seq, d)
```

```python
import functools

import jax
import jax.numpy as jnp
from jax import lax
from jax.experimental import pallas as pl
from jax.experimental.pallas import tpu as pltpu

F32 = jnp.float32
BF16 = jnp.bfloat16
HIGHEST = lax.Precision.HIGHEST

RMS_EPS = 1e-6
LANES = 128
SUBLANES = 8

MOBA_HEADS = 8
MOBA_HEAD_DIM = 64
MOBA_BLOCK = 256
MOBA_TOPK = 3
MLSTM_HEADS = 4
MLSTM_DIM = 128
CONV_WIDTH = 4
N_EXPERTS = 8

COL_GA, COL_GB = 0, 8
COL_QA, COL_KA, COL_VA = 16, 20, 24
COL_QM, COL_KM, COL_VM, COL_OM = 28, 32, 36, 40
N_PROJ = 44 * LANES

VMEM_LIMIT = 56 * 1024 * 1024
ROW_DMA_UNROLL = 8


def _params(*sem):
    return pltpu.CompilerParams(dimension_semantics=sem, vmem_limit_bytes=VMEM_LIMIT)


def _sigmoid(x):
    return 1.0 / (1.0 + jnp.exp(-x))


def _rms(x, g):
    return x * lax.rsqrt(jnp.mean(x * x, axis=-1, keepdims=True) + RMS_EPS) * g


def _split_bf16(x):
    hi = x.astype(BF16)
    return hi, (x - hi.astype(F32)).astype(BF16)


def _dot_split(a, b):
    ah, al = _split_bf16(a)
    bh, bl = _split_bf16(b)
    return (jnp.dot(ah, bh, preferred_element_type=F32) + jnp.dot(ah, bl, preferred_element_type=F32)
            + jnp.dot(al, bh, preferred_element_type=F32))


def _nt_dot(a, b, **kw):
    return lax.dot_general(a, b, (((1,), (1,)), ((), ())), preferred_element_type=F32, **kw)


def _in_proj_body(x_ref, g_ref, w_ref, wif_ref, o_ref, oif_ref, h_ref):
    @pl.when(pl.program_id(1) == 0)
    def _():
        h = _rms(x_ref[...], g_ref[...]).astype(BF16)
        h_ref[...] = h
        oif_ref[...] = jnp.dot(h, wif_ref[...], preferred_element_type=F32)

    o_ref[...] = jnp.dot(h_ref[...], w_ref[...], preferred_element_type=F32).astype(o_ref.dtype)


def in_proj(x2, g, w, wif, *, tm, tn):
    t, d = x2.shape
    n = w.shape[1]
    return pl.pallas_call(
        _in_proj_body,
        grid=(t // tm, n // tn),
        in_specs=[
            pl.BlockSpec((tm, d), lambda m, j: (m, 0)),
            pl.BlockSpec((1, d), lambda m, j: (0, 0)),
            pl.BlockSpec((d, tn), lambda m, j: (0, j)),
            pl.BlockSpec((d, LANES), lambda m, j: (0, 0)),
        ],
        out_specs=[
            pl.BlockSpec((tm, tn), lambda m, j: (m, j)),
            pl.BlockSpec((tm, LANES), lambda m, j: (m, 0)),
        ],
        out_shape=[jax.ShapeDtypeStruct((t, n), BF16), jax.ShapeDtypeStruct((t, LANES), F32)],
        scratch_shapes=[pltpu.VMEM((tm, d), BF16)],
        compiler_params=_params("arbitrary", "arbitrary"),
        name="in_proj",
    )(x2, g, w, wif)


MASK_BIAS = -1e30
LOG2_E = 1.4426950408889634


def _moba_body(q_ref, k_ref, v_ref, gq_ref, gk_ref, o_ref,
               kn_ref, vt_ref, kmean_ref, qaug_ref, s_ref, m_ref, alpha_ref, l_ref, acc_ref,
               *, nb, blk, dh, topk, nheads):
    i = pl.program_id(1)
    lane = lax.broadcasted_iota(jnp.int32, (1, LANES), 1)
    head0 = lane < dh

    def head_rms(x, g):
        x2 = x * x
        s0 = jnp.sum(jnp.where(head0, x2, 0.0), axis=-1, keepdims=True)
        s1 = jnp.sum(jnp.where(head0, 0.0, x2), axis=-1, keepdims=True)
        ms = jnp.where(head0, s0, s1) * (1.0 / dh)
        return x * lax.rsqrt(ms + RMS_EPS) * g

    @pl.when(i == 0)
    def _():
        def prep(j, c):
            r0 = pl.multiple_of(j * blk, blk)
            onehot = jnp.where(lane == dh + j, 1.0, 0.0)
            for p in range(nheads // 2):
                cols = slice(p * LANES, (p + 1) * LANES)
                kn = head_rms(k_ref[pl.ds(r0, blk), cols].astype(F32), gk_ref[...])
                for hh, kh in ((0, kn), (1, pltpu.roll(kn, dh, axis=1))):
                    h = 2 * p + hh
                    kmean_ref[h, pl.ds(j, 1), :] = jnp.mean(jnp.where(head0, kh, 0.0), axis=0, keepdims=True)
                    kn_ref[h, pl.ds(r0, blk), :] = jnp.where(head0, kh, onehot).astype(BF16)
                vt_ref[cols, pl.ds(r0, blk)] = v_ref[pl.ds(r0, blk), cols].astype(F32).T.astype(BF16)
            return c

        lax.fori_loop(0, nb, prep, 0)

    jidx = lax.broadcasted_iota(jnp.int32, (nb, blk), 0)
    key_i = lax.broadcasted_iota(jnp.int32, (blk, blk), 0)
    qry_i = lax.broadcasted_iota(jnp.int32, (blk, blk), 1)
    causal = key_i <= qry_i
    r_own = pl.multiple_of(i * blk, blk)
    qk_scale = dh ** -0.5 * LOG2_E
    for p in range(nheads // 2):
        cols = slice(p * LANES, (p + 1) * LANES)
        qn_t = head_rms(q_ref[:, cols].astype(F32), gq_ref[...]).T
        for hh in range(2):
            h = 2 * p + hh
            q_t = qn_t[hh * dh:(hh + 1) * dh, :]
            gate = jnp.dot(kmean_ref[h], jnp.concatenate([q_t, jnp.zeros((LANES - dh, blk), F32)], axis=0),
                           precision=HIGHEST, preferred_element_type=F32)
            rank = jnp.zeros((nb, blk), F32)
            for jp in range(nb):
                row = gate[jp:jp + 1, :]
                beats = (row > gate) | ((row == gate) & (jidx > jp))
                rank = rank + jnp.where(beats, jnp.where(jp < i, 1.0, 0.0), 0.0)
            sel = ((rank < topk) & (jidx < i)) | (jidx == i)
            bias = jnp.where(sel, 0.0, MASK_BIAS)
            qaug = jnp.concatenate([q_t * qk_scale, bias, jnp.zeros((LANES - dh - nb, blk), F32)],
                                   axis=0).astype(BF16)
            qaug_ref[h] = qaug
            st = jnp.dot(kn_ref[h, pl.ds(r_own, blk), :], qaug, preferred_element_type=F32)
            st = jnp.where(causal, st, -jnp.inf)
            s_ref[h] = st
            m_ref[h] = jnp.max(st, axis=0, keepdims=True)
            alpha_ref[h] = jnp.zeros((1, blk), F32)
            l_ref[h] = jnp.zeros((1, blk), F32)
            acc_ref[h] = jnp.zeros((dh, blk), F32)

    def finish_block(h, r_blk):
        pr = jnp.exp2(s_ref[h] - m_ref[h])
        alpha = alpha_ref[h]
        l_ref[h] = alpha * l_ref[h] + jnp.sum(pr, axis=0, keepdims=True)
        acc_ref[h] = alpha * acc_ref[h] + jnp.dot(vt_ref[h * dh:(h + 1) * dh, pl.ds(r_blk, blk)],
                                                  pr.astype(BF16), preferred_element_type=F32)

    def body(j, r_prev):
        r0 = pl.multiple_of(j * blk, blk)
        r_prev = pl.multiple_of(r_prev, blk)
        for h in range(nheads):
            finish_block(h, r_prev)
            st = jnp.dot(kn_ref[h, pl.ds(r0, blk), :], qaug_ref[h], preferred_element_type=F32)
            m_old = m_ref[h]
            m_new = jnp.maximum(m_old, jnp.max(st, axis=0, keepdims=True))
            s_ref[h] = st
            alpha_ref[h] = jnp.exp2(m_old - m_new)
            m_ref[h] = m_new
        return r0

    r_last = lax.fori_loop(0, i, body, r_own)
    r_last = pl.multiple_of(r_last, blk)
    for h in range(nheads):
        finish_block(h, r_last)
    for p in range(nheads // 2):
        ot = jnp.concatenate([acc_ref[2 * p] / l_ref[2 * p], acc_ref[2 * p + 1] / l_ref[2 * p + 1]], axis=0)
        o_ref[:, p * LANES:(p + 1) * LANES] = ot.T.astype(o_ref.dtype)


def moba(proj, gq2, gk2, *, batch, seq):
    nb = seq // MOBA_BLOCK
    blk = MOBA_BLOCK
    dh = MOBA_HEAD_DIM
    nheads = MOBA_HEADS
    width = nheads * dh
    wb = width // LANES
    assert dh + nb <= LANES and 2 * dh == LANES
    body = functools.partial(_moba_body, nb=nb, blk=blk, dh=dh, topk=MOBA_TOPK, nheads=nheads)
    return pl.pallas_call(
        body,
        grid=(batch, nb),
        in_specs=[
            pl.BlockSpec((blk, width), lambda b, i: (b * nb + i, COL_QA // wb)),
            pl.BlockSpec((seq, width), lambda b, i: (b, COL_KA // wb)),
            pl.BlockSpec((seq, width), lambda b, i: (b, COL_VA // wb)),
            pl.BlockSpec((1, LANES), lambda b, i: (0, 0)),
            pl.BlockSpec((1, LANES), lambda b, i: (0, 0)),
        ],
        out_specs=pl.BlockSpec((blk, width), lambda b, i: (b * nb + i, 0)),
        out_shape=jax.ShapeDtypeStruct((batch * seq, width), BF16),
        scratch_shapes=[
            pltpu.VMEM((nheads, seq, LANES), BF16),
            pltpu.VMEM((width, seq), BF16),
            pltpu.VMEM((nheads, nb, LANES), F32),
            pltpu.VMEM((nheads, LANES, blk), BF16),
            pltpu.VMEM((nheads, blk, blk), F32),
            pltpu.VMEM((nheads, 1, blk), F32),
            pltpu.VMEM((nheads, 1, blk), F32),
            pltpu.VMEM((nheads, 1, blk), F32),
            pltpu.VMEM((nheads, dh, blk), F32),
        ],
        compiler_params=_params("arbitrary", "arbitrary"),
        name="moba",
    )(proj, proj, proj, gq2, gk2)


def _log_sigmoid(x):
    return jnp.minimum(x, 0.0) - jnp.log(1.0 + jnp.exp(-jnp.abs(x)))


def _dot_tri(tri, x, tri_left):
    out = None
    for _ in range(3):
        piece = x.astype(BF16)
        x = x - piece.astype(F32)
        term = (jnp.dot(tri, piece, preferred_element_type=F32) if tri_left
                else jnp.dot(piece, tri, preferred_element_type=F32))
        out = term if out is None else out + term
    return out


def _mlstm_body(qr_ref, kr_ref, v_ref, og_ref, gcol_ref, grow_ref, brow_ref, bcol_ref,
                cwq_ref, cwk_ref, cbq_ref, cbk_ref, gh_ref, o_ref,
                qx_ref, kx_ref, c_ref, m_ref, *, chunk, dk, nh):
    L = chunk
    width = nh * dk

    @pl.when(pl.program_id(1) == 0)
    def _():
        qx_ref[0:SUBLANES, :] = jnp.zeros((SUBLANES, width), F32)
        kx_ref[0:SUBLANES, :] = jnp.zeros((SUBLANES, width), F32)
        c_ref[...] = jnp.zeros_like(c_ref)
        m_ref[...] = jnp.zeros_like(m_ref)

    qx_ref[SUBLANES:SUBLANES + L, :] = qr_ref[...].astype(F32)
    kx_ref[SUBLANES:SUBLANES + L, :] = kr_ref[...].astype(F32)

    def conv_silu(x_ref, w_ref, b_ref):
        acc = b_ref[...] + w_ref[0:1, :] * x_ref[pl.ds(SUBLANES - CONV_WIDTH + 1, L), :]
        for j in range(1, CONV_WIDTH):
            acc = acc + w_ref[j:j + 1, :] * x_ref[pl.ds(SUBLANES - CONV_WIDTH + 1 + j, L), :]
        return acc * _sigmoid(acc)

    q_all = conv_silu(qx_ref, cwq_ref, cbq_ref)
    k_all = conv_silu(kx_ref, cwk_ref, cbk_ref) * (dk ** -0.5)
    qx_ref[0:SUBLANES, :] = qx_ref[L:L + SUBLANES, :]
    kx_ref[0:SUBLANES, :] = kx_ref[L:L + SUBLANES, :]

    pre_col = gcol_ref[...] + brow_ref[...]
    pre_row = grow_ref[0] + bcol_ref[...]
    t_i = lax.broadcasted_iota(jnp.int32, (L, L), 0)
    s_i = lax.broadcasted_iota(jnp.int32, (L, L), 1)
    tril = s_i <= t_i
    bcum_cols = _dot_tri(jnp.where(tril, 1.0, 0.0).astype(BF16), _log_sigmoid(pre_col), True)
    bcum_rows = _dot_tri(jnp.where(t_i <= s_i, 1.0, 0.0).astype(BF16), _log_sigmoid(pre_row), False)
    ones = jnp.ones((L, dk), BF16)

    for h in range(nh):
        cols = slice(h * dk, (h + 1) * dk)
        q = q_all[:, cols]
        k = k_all[:, cols]
        i_col = pre_col[:, h:h + 1]
        i_row = pre_row[h:h + 1, :]
        bcum_col = bcum_cols[:, nh + h:nh + h + 1]
        bcum_row = bcum_rows[nh + h:nh + h + 1, :]

        m_prev = m_ref[h, 0:1, 0:1]
        a_col = bcum_col + m_prev
        dmat = jnp.where(tril, bcum_col - bcum_row + i_row, -jnp.inf)
        m_t = jnp.maximum(a_col, jnp.max(dmat, axis=-1, keepdims=True))
        dw = jnp.exp(dmat - m_t)
        aw = jnp.exp(a_col - m_t)

        qb = q.astype(BF16)
        kb = k.astype(BF16)
        v_aug = jnp.concatenate([v_ref[:, cols], ones], axis=-1)
        sqk = _nt_dot(qb, kb) * dw
        num_aug = (aw * jnp.dot(qb, c_ref[h].astype(BF16), preferred_element_type=F32)
                   + jnp.dot(sqk.astype(BF16), v_aug, preferred_element_type=F32))
        den = num_aug[:, dk:dk + 1]
        hc = num_aug[:, 0:dk] / jnp.maximum(jnp.abs(den), jnp.exp(-m_t))

        b_last = bcum_col[L - 1:L, :]
        g_col = b_last - bcum_col + i_col
        m_new = jnp.maximum(b_last + m_prev, jnp.max(g_col, axis=0, keepdims=True))
        w_c = jnp.exp(b_last + m_prev - m_new)
        kw_t = (k * jnp.exp(g_col - m_new)).T.astype(BF16)
        c_ref[h] = w_c * c_ref[h] + jnp.dot(kw_t, v_aug, preferred_element_type=F32)
        m_ref[h] = jnp.broadcast_to(m_new, (1, LANES))

        o_ref[:, cols] = (_rms(hc, gh_ref[...]) * _sigmoid(og_ref[:, cols].astype(F32))).astype(o_ref.dtype)


def mlstm(proj, gates_col, gates_row, bias_row, bias_col, conv_w, conv_b, gh, *, batch, seq, chunk):
    nh = MLSTM_HEADS
    dk = MLSTM_DIM
    width = nh * dk
    wb = width // LANES
    nc = seq // chunk
    body = functools.partial(_mlstm_body, chunk=chunk, dk=dk, nh=nh)

    def rows(col0):
        return pl.BlockSpec((chunk, width), lambda b, c: (b * nc + c, col0 // wb))

    return pl.pallas_call(
        body,
        grid=(batch, nc),
        in_specs=[
            rows(COL_QM), rows(COL_KM), rows(COL_VM), rows(COL_OM),
            pl.BlockSpec((chunk, LANES), lambda b, c: (b * nc + c, 0)),
            pl.BlockSpec((1, SUBLANES, chunk), lambda b, c: (b, 0, c)),
            pl.BlockSpec((1, LANES), lambda b, c: (0, 0)),
            pl.BlockSpec((SUBLANES, 1), lambda b, c: (0, 0)),
            pl.BlockSpec((CONV_WIDTH, width), lambda b, c: (0, 0)),
            pl.BlockSpec((CONV_WIDTH, width), lambda b, c: (0, 1)),
            pl.BlockSpec((1, width), lambda b, c: (0, 0)),
            pl.BlockSpec((1, width), lambda b, c: (0, 1)),
            pl.BlockSpec((1, LANES), lambda b, c: (0, 0)),
        ],
        out_specs=pl.BlockSpec((chunk, width), lambda b, c: (b * nc + c, 0)),
        out_shape=jax.ShapeDtypeStruct((batch * seq, width), BF16),
        scratch_shapes=[
            pltpu.VMEM((chunk + 2 * SUBLANES, width), F32),
            pltpu.VMEM((chunk + 2 * SUBLANES, width), F32),
            pltpu.VMEM((nh, dk, 2 * dk), F32),
            pltpu.VMEM((nh, 1, LANES), F32),
        ],
        compiler_params=_params("arbitrary", "arbitrary"),
        name="mlstm",
    )(proj, proj, proj, proj, gates_col, gates_row, bias_row, bias_col,
      conv_w, conv_w, conv_b, conv_b, gh)


def _merge_body(ya_ref, yb_ref, ga_ref, gb_ref, x_ref, woa_ref, wob_ref, wout_ref, gffn_ref, *rest, moe):
    a = jnp.dot(ya_ref[...], woa_ref[...], preferred_element_type=F32)
    b = jnp.dot(yb_ref[...], wob_ref[...], preferred_element_type=F32)
    mixed = _sigmoid(ga_ref[...].astype(F32)) * a + _sigmoid(gb_ref[...].astype(F32)) * b
    x1 = x_ref[...] + jnp.dot(mixed.astype(BF16), wout_ref[...], preferred_element_type=F32)
    hf = _rms(x1, gffn_ref[...])
    if moe:
        wr_ref, x1_ref, hf_ref, lg_ref = rest
        hf_ref[...] = hf
        lg_ref[...] = _dot_split(hf, wr_ref[...])
    else:
        x1_ref, hf_ref = rest
        hf_ref[...] = hf.astype(BF16)
    x1_ref[...] = x1


def merge(ya, yb, proj, x2, woa, wob, wout, gffn, wr, *, tm):
    t, d = x2.shape
    moe = wr is not None
    full = lambda m: (0, 0)
    in_specs = [
        pl.BlockSpec((tm, ya.shape[1]), lambda m: (m, 0)),
        pl.BlockSpec((tm, yb.shape[1]), lambda m: (m, 0)),
        pl.BlockSpec((tm, d), lambda m: (m, COL_GA * LANES // d)),
        pl.BlockSpec((tm, d), lambda m: (m, COL_GB * LANES // d)),
        pl.BlockSpec((tm, d), lambda m: (m, 0)),
        pl.BlockSpec(woa.shape, full), pl.BlockSpec(wob.shape, full), pl.BlockSpec(wout.shape, full),
        pl.BlockSpec((1, d), full),
    ]
    args = [ya, yb, proj, proj, x2, woa, wob, wout, gffn]
    out_specs = [pl.BlockSpec((tm, d), lambda m: (m, 0)), pl.BlockSpec((tm, d), lambda m: (m, 0))]
    out_shape = [jax.ShapeDtypeStruct((t, d), F32), jax.ShapeDtypeStruct((t, d), F32 if moe else BF16)]
    if moe:
        in_specs.append(pl.BlockSpec(wr.shape, full))
        args.append(wr)
        out_specs.append(pl.BlockSpec((tm, LANES), lambda m: (m, 0)))
        out_shape.append(jax.ShapeDtypeStruct((t, LANES), F32))
    return pl.pallas_call(
        functools.partial(_merge_body, moe=moe),
        grid=(t // tm,),
        in_specs=in_specs, out_specs=out_specs, out_shape=out_shape,
        compiler_params=_params("arbitrary"),
        name="merge_moe" if moe else "merge",
    )(*args)


def _swiglu(x, w1_ref, w3_ref, w2_ref, g_ref, fc):
    dff = g_ref.shape[1]
    for f0 in range(0, dff, fc):
        a = jnp.dot(x, w1_ref[:, f0:f0 + fc], preferred_element_type=F32)
        b = jnp.dot(x, w3_ref[:, f0:f0 + fc], preferred_element_type=F32)
        g_ref[:, f0:f0 + fc] = (a * _sigmoid(a) * b).astype(BF16)
    return jnp.dot(g_ref[...], w2_ref[...], preferred_element_type=F32)


def _ple(x, p_ref, g_ref, wg_ref, wp_ref):
    gate = _sigmoid(jnp.dot(_rms(x, g_ref[...]).astype(BF16), wg_ref[...], preferred_element_type=F32))
    emb = jnp.dot(p_ref[...].astype(BF16), wp_ref[...], preferred_element_type=F32)
    return x + gate * emb


def _ffn_body(hf_ref, x1_ref, w1_ref, w3_ref, w2_ref, p_ref, g_ref, wg_ref, wp_ref, o_ref, act_ref, *, fc):
    x2 = x1_ref[...] + _swiglu(hf_ref[...], w1_ref, w3_ref, w2_ref, act_ref, fc)
    o_ref[...] = _ple(x2, p_ref, g_ref, wg_ref, wp_ref)


def dense_ffn(hf, x1, w1, w3, w2, p2, g, wg, wp, *, tm, fc):
    t, d = x1.shape
    dff = w1.shape[1]
    resident = dict(pipeline_mode=pl.Buffered(1))
    full = lambda m: (0, 0)
    return pl.pallas_call(
        functools.partial(_ffn_body, fc=fc),
        grid=(t // tm,),
        in_specs=[
            pl.BlockSpec((tm, d), lambda m: (m, 0)),
            pl.BlockSpec((tm, d), lambda m: (m, 0)),
            pl.BlockSpec((d, dff), full, **resident),
            pl.BlockSpec((d, dff), full, **resident),
            pl.BlockSpec((dff, d), full, **resident),
            pl.BlockSpec((tm, p2.shape[1]), lambda m: (m, 0)),
            pl.BlockSpec((1, d), full),
            pl.BlockSpec(wg.shape, full, **resident),
            pl.BlockSpec(wp.shape, full, **resident),
        ],
        out_specs=pl.BlockSpec((tm, d), lambda m: (m, 0)),
        out_shape=jax.ShapeDtypeStruct((t, d), F32),
        scratch_shapes=[pltpu.VMEM((tm, dff), BF16)],
        compiler_params=_params("arbitrary"),
        name="dense_ffn",
    )(hf, x1, w1, w3, w2, p2, g, wg, wp)


META_E0, META_E1, META_G0, META_G1, META_R0, META_R1 = 0, 1, 2, 3, 4, 5


def _route_body(lg_ref, meta_ref, cnt_ref, carry_ref, *, tm, ne):
    @pl.when(pl.program_id(0) == 0)
    def _():
        carry_ref[...] = jnp.zeros_like(carry_ref)

    lane = lax.broadcasted_iota(jnp.int32, (tm, LANES), 1)
    lanef = lane.astype(F32)
    lg = jnp.where(lane < ne, lg_ref[...], -jnp.inf)
    m1 = jnp.max(lg, axis=-1, keepdims=True)
    e1 = jnp.min(jnp.where(lg == m1, lanef, float(LANES)), axis=-1, keepdims=True)
    lg2 = jnp.where(lanef == e1, -jnp.inf, lg)
    m2 = jnp.max(lg2, axis=-1, keepdims=True)
    e2 = jnp.min(jnp.where(lg2 == m2, lanef, float(LANES)), axis=-1, keepdims=True)
    ex = jnp.exp(m2 - m1)
    g1 = 1.0 / (1.0 + ex)
    g2 = ex / (1.0 + ex)
    onehot = jnp.where((lanef == e1) | (lanef == e2), 1.0, 0.0)
    row = lax.broadcasted_iota(jnp.int32, (tm, tm), 0)
    col = lax.broadcasted_iota(jnp.int32, (tm, tm), 1)
    before = jnp.dot(jnp.where(col < row, 1.0, 0.0).astype(BF16), onehot.astype(BF16),
                     preferred_element_type=F32) + carry_ref[...]
    r1 = jnp.sum(jnp.where(lanef == e1, before, 0.0), axis=-1, keepdims=True)
    r2 = jnp.sum(jnp.where(lanef == e2, before, 0.0), axis=-1, keepdims=True)
    carry_ref[...] += jnp.sum(onehot, axis=0, keepdims=True)
    meta = jnp.zeros((tm, LANES), F32)
    for pos, val in ((META_E0, e1), (META_E1, e2), (META_G0, g1), (META_G1, g2), (META_R0, r1), (META_R1, r2)):
        meta = jnp.where(lane == pos, val, meta)
    meta_ref[...] = meta
    cnt_ref[...] = jnp.broadcast_to(carry_ref[...], cnt_ref.shape)


def route(logits, *, tm):
    t = logits.shape[0]
    return pl.pallas_call(
        functools.partial(_route_body, tm=tm, ne=N_EXPERTS),
        grid=(t // tm,),
        in_specs=[pl.BlockSpec((tm, LANES), lambda m: (m, 0))],
        out_specs=[pl.BlockSpec((tm, LANES), lambda m: (m, 0)),
                   pl.BlockSpec((SUBLANES, LANES), lambda m: (0, 0))],
        out_shape=[jax.ShapeDtypeStruct((t, LANES), F32), jax.ShapeDtypeStruct((SUBLANES, LANES), F32)],
        scratch_shapes=[pltpu.VMEM((1, LANES), F32)],
        compiler_params=_params("arbitrary"),
        name="route",
    )(logits)


def _dispatch_body(dest_ref, hf_ref, xs_in_ref, xs_ref, sem, *, tm, topk):
    del xs_in_ref
    base = pl.program_id(0) * tm * topk

    def issue(r, c):
        for k in range(topk):
            d = dest_ref[base + topk * r + k]
            pltpu.make_async_copy(hf_ref.at[pl.ds(r, 1)], xs_ref.at[pl.ds(d, 1)], sem).start(priority=k % 2)
        return c

    lax.fori_loop(0, tm, issue, 0, unroll=ROW_DMA_UNROLL)
    for k in range(topk):
        pltpu.make_async_copy(hf_ref, xs_ref.at[pl.ds(0, tm)], sem).wait()


def dispatch(dest, hf, xs_init, *, tm, topk):
    t, d = hf.shape
    return pl.pallas_call(
        functools.partial(_dispatch_body, tm=tm, topk=topk),
        grid_spec=pltpu.PrefetchScalarGridSpec(
            num_scalar_prefetch=1,
            grid=(t // tm,),
            in_specs=[pl.BlockSpec((tm, d), lambda m, dest: (m, 0)),
                      pl.BlockSpec(memory_space=pl.ANY)],
            out_specs=pl.BlockSpec(memory_space=pl.ANY),
            scratch_shapes=[pltpu.SemaphoreType.DMA],
        ),
        out_shape=jax.ShapeDtypeStruct(xs_init.shape, xs_init.dtype),
        input_output_aliases={2: 0},
        compiler_params=_params("arbitrary"),
        name="moe_dispatch",
    )(dest, hf, xs_init)


def _experts_body(te_ref, na_ref, xs_ref, w1_ref, w3_ref, w2_ref, y_ref, xb_ref, g_ref, *, fc):
    del te_ref
    f = pl.program_id(1)

    @pl.when(pl.program_id(0) >= na_ref[0])
    def _():
        y_ref[...] = jnp.zeros_like(y_ref)

    @pl.when(pl.program_id(0) < na_ref[0])
    def _():
        @pl.when(f == 0)
        def _():
            xb_ref[...] = xs_ref[...].astype(BF16)

        y = _swiglu(xb_ref[...], w1_ref.at[0], w3_ref.at[0], w2_ref.at[0], g_ref, fc)

        @pl.when(f == 0)
        def _():
            y_ref[...] = y

        @pl.when(f > 0)
        def _():
            y_ref[...] += y


def experts(tile_expert, n_active, xs, w1, w3, w2, *, tm, tf, fc):
    n_rows, d = xs.shape
    dff = w1.shape[2]
    row_tile = lambda i, f, te, na: (jnp.minimum(i, na[0] - 1), 0)
    ftile = lambda i, f, na: jnp.where(i < na[0], f, dff // tf - 1)
    return pl.pallas_call(
        functools.partial(_experts_body, fc=fc),
        grid_spec=pltpu.PrefetchScalarGridSpec(
            num_scalar_prefetch=2,
            grid=(n_rows // tm, dff // tf),
            in_specs=[
                pl.BlockSpec((tm, d), row_tile),
                pl.BlockSpec((1, d, tf), lambda i, f, te, na: (te[i], 0, ftile(i, f, na))),
                pl.BlockSpec((1, d, tf), lambda i, f, te, na: (te[i], 0, ftile(i, f, na))),
                pl.BlockSpec((1, tf, d), lambda i, f, te, na: (te[i], ftile(i, f, na), 0)),
            ],
            out_specs=pl.BlockSpec((tm, d), lambda i, f, te, na: (i, 0)),
            scratch_shapes=[pltpu.VMEM((tm, d), BF16), pltpu.VMEM((tm, tf), BF16)],
        ),
        out_shape=jax.ShapeDtypeStruct((n_rows, d), F32),
        compiler_params=_params("arbitrary", "arbitrary"),
        name="moe_experts",
    )(tile_expert, n_active, xs, w1, w3, w2)


def _combine_body(dest_ref, x1_ref, meta_ref, p_ref, g_ref, wg_ref, wp_ref, y_ref, o_ref, buf_ref, sem,
                  *, tm, topk):
    base = pl.program_id(0) * tm * topk

    def issue(r, c):
        for k in range(topk):
            d = dest_ref[base + topk * r + k]
            pltpu.make_async_copy(y_ref.at[pl.ds(d, 1)], buf_ref.at[k, pl.ds(r, 1)], sem).start(priority=k % 2)
        return c

    lax.fori_loop(0, tm, issue, 0, unroll=ROW_DMA_UNROLL)
    for k in range(topk):
        pltpu.make_async_copy(y_ref.at[pl.ds(0, tm)], buf_ref.at[k], sem).wait()
    meta = meta_ref[...]
    g0 = meta[:, META_G0:META_G0 + 1]
    g1 = meta[:, META_G1:META_G1 + 1]
    x2 = x1_ref[...] + (g0 * buf_ref[0] + g1 * buf_ref[1])
    o_ref[...] = _ple(x2, p_ref, g_ref, wg_ref, wp_ref)


def combine(dest, x1, meta, y, p2, g, wg, wp, *, tm, topk):
    t, d = x1.shape
    full = lambda m, dest: (0, 0)
    return pl.pallas_call(
        functools.partial(_combine_body, tm=tm, topk=topk),
        grid_spec=pltpu.PrefetchScalarGridSpec(
            num_scalar_prefetch=1,
            grid=(t // tm,),
            in_specs=[pl.BlockSpec((tm, d), lambda m, dest: (m, 0)),
                      pl.BlockSpec((tm, LANES), lambda m, dest: (m, 0)),
                      pl.BlockSpec((tm, p2.shape[1]), lambda m, dest: (m, 0)),
                      pl.BlockSpec((1, d), full), pl.BlockSpec(wg.shape, full), pl.BlockSpec(wp.shape, full),
                      pl.BlockSpec(memory_space=pl.ANY)],
            out_specs=pl.BlockSpec((tm, d), lambda m, dest: (m, 0)),
            scratch_shapes=[pltpu.VMEM((topk, tm, d), F32), pltpu.SemaphoreType.DMA],
        ),
        out_shape=jax.ShapeDtypeStruct((t, d), F32),
        compiler_params=_params("arbitrary"),
        name="moe_combine",
    )(dest, x1, meta, p2, g, wg, wp, y)


def moe_ffn(hf, x1, logits, w1, w3, w2, ple_args, *, tm_route, tm_rows, tf, tm_move):
    t, d = x1.shape
    topk = 2
    meta, cnt = route(logits, tm=tm_route)
    counts = cnt[0, :N_EXPERTS].astype(jnp.int32)
    padded = ((counts + tm_rows - 1) // tm_rows) * tm_rows
    pad_end = jnp.cumsum(padded)
    pad_start = pad_end - padded
    eidx = meta[:, META_E0:META_E1 + 1].astype(jnp.int32)
    rank = meta[:, META_R0:META_R1 + 1].astype(jnp.int32)
    dest = (pad_start[eidx] + rank).reshape(t * topk)
    n_tiles = -(-(t * topk) // tm_rows) + N_EXPERTS
    tile_start = jnp.arange(n_tiles, dtype=jnp.int32) * tm_rows
    tile_expert = jnp.minimum(jnp.sum(tile_start[:, None] >= pad_end[None, :], axis=1),
                              N_EXPERTS - 1).astype(jnp.int32)
    xs = dispatch(dest, hf, jnp.zeros((n_tiles * tm_rows, d), F32), tm=tm_move, topk=topk)
    n_active = (pad_end[N_EXPERTS - 1:] // tm_rows).astype(jnp.int32)
    y = experts(tile_expert, n_active, xs, w1, w3, w2, tm=tm_rows, tf=tf, fc=256)
    return combine(dest, x1, meta, y, *ple_args, tm=tm_move, topk=topk)


def _tile2(g):
    return jnp.concatenate([g, g]).reshape(1, 2 * g.shape[0])


def kernel(x, p, g_mix, w_in, g_q, g_k, conv_w, conv_b, b_i, b_f, g_h, w_oa, w_ob, w_out, g_ffn, w_d1, w_d3,
           w_d2, w_router, w_e1, w_e3, w_e2, g_ple, w_ple_gate, w_ple_proj):
    batch, seq, d = x.shape
    depth = w_in.shape[0]
    t = batch * seq
    nh = MLSTM_HEADS
    x2 = x.reshape(t, d)
    c_q, c_k, c_v = 0, 512, 1024
    c_qk, c_vm, c_om, c_i, c_f, c_ga, c_gb, c_end = 1536, 2560, 3072, 3584, 3588, 3592, 4616, 5640

    for l in range(depth):
        w = w_in[l]
        w_main = jnp.concatenate([w[:, c_ga:c_gb], w[:, c_gb:c_end], w[:, c_q:c_i]], axis=1).astype(BF16)
        w_if = jnp.pad(w[:, c_i:c_ga], ((0, 0), (0, LANES - 2 * nh))).astype(BF16)
        proj, gif = in_proj(x2, g_mix[l].reshape(1, d), w_main, w_if, tm=2048, tn=512)

        ya = moba(proj, _tile2(g_q[l]), _tile2(g_k[l]), batch=batch, seq=seq)

        bias = jnp.concatenate([b_i[l], b_f[l]])
        bias_row = jnp.pad(bias, (0, LANES - 2 * nh)).reshape(1, LANES)
        bias_col = bias.reshape(2 * nh, 1)
        gates_row = gif[:, :2 * nh].reshape(batch, seq, 2 * nh).transpose(0, 2, 1)
        yb = mlstm(proj, gif, gates_row, bias_row, bias_col, conv_w[l], conv_b[l].reshape(1, -1),
                   g_h[l].reshape(1, -1), batch=batch, seq=seq, chunk=256)

        j = l // 2
        moe = l % 2 == 1
        wr = jnp.pad(w_router[j], ((0, 0), (0, LANES - N_EXPERTS))) if moe else None
        outs = merge(ya, yb, proj, x2, w_oa[l].astype(BF16), w_ob[l].astype(BF16), w_out[l].astype(BF16),
                     g_ffn[l].reshape(1, d), wr, tm=512)
        ple_args = (p[l].reshape(t, -1), g_ple[l].reshape(1, d), w_ple_gate[l].astype(BF16),
                    w_ple_proj[l].astype(BF16))
        if moe:
            x1, hf, logits = outs
            x2 = moe_ffn(hf, x1, logits, w_e1[j].astype(BF16), w_e3[j].astype(BF16), w_e2[j].astype(BF16),
                         ple_args, tm_route=512, tm_rows=512, tf=1792, tm_move=512)
        else:
            x1, hf = outs
            x2 = dense_ffn(hf, x1, w_d1[j].astype(BF16), w_d3[j].astype(BF16), w_d2[j].astype(BF16),
                           *ple_args, tm=1024, fc=256)
    return x2.reshape(batch, seq, d)
```

```python
import functools

import jax
import jax.numpy as jnp
from jax import lax
from jax.experimental import pallas as pl
from jax.experimental.pallas import tpu as pltpu

F32 = jnp.float32
BF16 = jnp.bfloat16
HIGHEST = lax.Precision.HIGHEST

RMS_EPS = 1e-6
LANES = 128
SUBLANES = 8

MOBA_HEADS = 8
MOBA_HEAD_DIM = 64
MOBA_BLOCK = 256
MOBA_TOPK = 3
MLSTM_HEADS = 4
MLSTM_DIM = 128
CONV_WIDTH = 4
N_EXPERTS = 8

COL_GA, COL_GB = 0, 8
COL_QA, COL_KA, COL_VA = 16, 20, 24
COL_QM, COL_KM, COL_VM, COL_OM = 28, 32, 36, 40
N_PROJ = 44 * LANES

VMEM_LIMIT = 56 * 1024 * 1024
ROW_DMA_UNROLL = 8


def _params(*sem):
    return pltpu.CompilerParams(dimension_semantics=sem, vmem_limit_bytes=VMEM_LIMIT)


def _sigmoid(x):
    return 1.0 / (1.0 + jnp.exp(-x))


def _rms(x, g):
    return x * lax.rsqrt(jnp.mean(x * x, axis=-1, keepdims=True) + RMS_EPS) * g


def _split_bf16(x):
    hi = x.astype(BF16)
    return hi, (x - hi.astype(F32)).astype(BF16)


def _dot_split(a, b):
    ah, al = _split_bf16(a)
    bh, bl = _split_bf16(b)
    return (jnp.dot(ah, bh, preferred_element_type=F32) + jnp.dot(ah, bl, preferred_element_type=F32)
            + jnp.dot(al, bh, preferred_element_type=F32))


def _nt_dot(a, b, **kw):
    return lax.dot_general(a, b, (((1,), (1,)), ((), ())), preferred_element_type=F32, **kw)


def _in_proj_body(x_ref, g_ref, w_ref, wif_ref, o_ref, oif_ref, h_ref):
    @pl.when(pl.program_id(1) == 0)
    def _():
        h = _rms(x_ref[...], g_ref[...]).astype(BF16)
        h_ref[...] = h
        oif_ref[...] = jnp.dot(h, wif_ref[...], preferred_element_type=F32)

    o_ref[...] = jnp.dot(h_ref[...], w_ref[...], preferred_element_type=F32).astype(o_ref.dtype)


def in_proj(x2, g, w, wif, *, tm, tn):
    t, d = x2.shape
    n = w.shape[1]
    return pl.pallas_call(
        _in_proj_body,
        grid=(t // tm, n // tn),
        in_specs=[
            pl.BlockSpec((tm, d), lambda m, j: (m, 0)),
            pl.BlockSpec((1, d), lambda m, j: (0, 0)),
            pl.BlockSpec((d, tn), lambda m, j: (0, j)),
            pl.BlockSpec((d, LANES), lambda m, j: (0, 0)),
        ],
        out_specs=[
            pl.BlockSpec((tm, tn), lambda m, j: (m, j)),
            pl.BlockSpec((tm, LANES), lambda m, j: (m, 0)),
        ],
        out_shape=[jax.ShapeDtypeStruct((t, n), BF16), jax.ShapeDtypeStruct((t, LANES), F32)],
        scratch_shapes=[pltpu.VMEM((tm, d), BF16)],
        compiler_params=_params("arbitrary", "arbitrary"),
        name="in_proj",
    )(x2, g, w, wif)


MASK_BIAS = -1e30
LOG2_E = 1.4426950408889634


def _moba_body(q_ref, k_ref, v_ref, gq_ref, gk_ref, o_ref,
               kn_ref, vt_ref, kmean_ref, qaug_ref, s_ref, m_ref, alpha_ref, acc_ref,
               *, nb, blk, dh, topk, nheads):
    i = pl.program_id(1)
    pair = 2 * blk
    lane = lax.broadcasted_iota(jnp.int32, (1, LANES), 1)
    head0 = lane < dh

    def head_rms(x, g):
        x2 = x * x
        s0 = jnp.sum(jnp.where(head0, x2, 0.0), axis=-1, keepdims=True)
        s1 = jnp.sum(jnp.where(head0, 0.0, x2), axis=-1, keepdims=True)
        ms = jnp.where(head0, s0, s1) * (1.0 / dh)
        return x * lax.rsqrt(ms + RMS_EPS) * g

    @pl.when(i == 0)
    def _():
        def prep(j, c):
            r0 = pl.multiple_of(j * blk, blk)
            onehot = jnp.where(lane == dh + j, 1.0, 0.0)
            for p in range(nheads // 2):
                cols = slice(p * LANES, (p + 1) * LANES)
                kn = head_rms(k_ref[pl.ds(r0, blk), cols].astype(F32), gk_ref[...])
                for hh, kh in ((0, kn), (1, pltpu.roll(kn, dh, axis=1))):
                    h = 2 * p + hh
                    kmean_ref[h, pl.ds(j, 1), :] = jnp.mean(jnp.where(head0, kh, 0.0), axis=0, keepdims=True)
                    kn_ref[h, pl.ds(r0, blk), :] = jnp.where(head0, kh, onehot).astype(BF16)
                v_t = v_ref[pl.ds(r0, blk), cols].astype(F32).T.astype(BF16)
                for hh in range(2):
                    vt_ref[2 * p + hh, 0:dh, pl.ds(r0, blk)] = v_t[hh * dh:(hh + 1) * dh, :]
                    vt_ref[2 * p + hh, dh:, pl.ds(r0, blk)] = jnp.ones((vt_ref.shape[1] - dh, blk), BF16)
            return c

        lax.fori_loop(0, nb, prep, 0)

    jidx = lax.broadcasted_iota(jnp.int32, (nb, blk), 0)
    key_i = lax.broadcasted_iota(jnp.int32, (blk, blk), 0)
    qry_i = lax.broadcasted_iota(jnp.int32, (blk, blk), 1)
    causal = key_i <= qry_i
    r_own = pl.multiple_of(i * blk, blk)
    qk_scale = dh ** -0.5 * LOG2_E
    for p in range(nheads // 2):
        cols = slice(p * LANES, (p + 1) * LANES)
        qn_t = head_rms(q_ref[:, cols].astype(F32), gq_ref[...]).T
        for hh in range(2):
            h = 2 * p + hh
            q_t = qn_t[hh * dh:(hh + 1) * dh, :]
            gate = jnp.dot(kmean_ref[h], jnp.concatenate([q_t, jnp.zeros((LANES - dh, blk), F32)], axis=0),
                           precision=HIGHEST, preferred_element_type=F32)
            rank = jnp.zeros((nb, blk), F32)
            for jp in range(nb):
                row = gate[jp:jp + 1, :]
                beats = (row > gate) | ((row == gate) & (jidx > jp))
                rank = rank + jnp.where(beats, jnp.where(jp < i, 1.0, 0.0), 0.0)
            sel = (rank < topk) & (jidx < i)
            q_s = q_t * qk_scale
            pad = jnp.zeros((LANES - dh - nb, blk), F32)
            qaug_ref[h] = jnp.concatenate([q_s, jnp.where(sel, 0.0, MASK_BIAS), pad], axis=0).astype(BF16)
            qaug_own = jnp.concatenate([q_s, jnp.where(jidx == i, 0.0, MASK_BIAS), pad], axis=0).astype(BF16)
            st = jnp.dot(kn_ref[h, pl.ds(r_own, blk), :], qaug_own, preferred_element_type=F32)
            st = jnp.where(causal, st, -jnp.inf)
            s_ref[h, 0:blk, :] = st
            m_ref[h] = jnp.max(st, axis=0, keepdims=True)

    def finish_own(h):
        pr = jnp.exp2(s_ref[h, 0:blk, :] - m_ref[h]).astype(BF16)
        acc_ref[h] = jnp.dot(vt_ref[h, :, pl.ds(r_own, blk)], pr, preferred_element_type=F32)

    def score_pair(u, h):
        r0 = pl.multiple_of(u * pair, pair)
        st = jnp.dot(kn_ref[h, pl.ds(r0, pair), :], qaug_ref[h], preferred_element_type=F32)
        m_old = m_ref[h]
        m_new = jnp.maximum(m_old, jnp.max(st, axis=0, keepdims=True))
        s_ref[h] = st
        alpha_ref[h] = jnp.exp2(m_old - m_new)
        m_ref[h] = m_new

    def finish_pair(u, h):
        r0 = pl.multiple_of(u * pair, pair)
        pr = jnp.exp2(s_ref[h] - m_ref[h]).astype(BF16)
        acc_ref[h] = alpha_ref[h] * acc_ref[h] + jnp.dot(vt_ref[h, :, pl.ds(r0, pair)], pr,
                                                         preferred_element_type=F32)

    n_pairs = jnp.maximum((i + 1) // 2, 1)
    for h in range(nheads):
        finish_own(h)
        score_pair(0, h)

    def body(u, c):
        for h in range(nheads):
            finish_pair(u - 1, h)
            score_pair(u, h)
        return c

    lax.fori_loop(1, n_pairs, body, 0)
    for h in range(nheads):
        finish_pair(n_pairs - 1, h)

    for p in range(nheads // 2):
        a0 = acc_ref[2 * p]
        a1 = acc_ref[2 * p + 1]
        ot = jnp.concatenate([a0[0:dh] / a0[dh:dh + 1], a1[0:dh] / a1[dh:dh + 1]], axis=0)
        o_ref[:, p * LANES:(p + 1) * LANES] = ot.T.astype(o_ref.dtype)


def moba(proj, gq2, gk2, *, batch, seq):
    nb = seq // MOBA_BLOCK
    blk = MOBA_BLOCK
    dh = MOBA_HEAD_DIM
    nheads = MOBA_HEADS
    width = nheads * dh
    wb = width // LANES
    assert dh + nb <= LANES and 2 * dh == LANES and nb % 2 == 0
    v_rows = dh + 2 * SUBLANES
    body = functools.partial(_moba_body, nb=nb, blk=blk, dh=dh, topk=MOBA_TOPK, nheads=nheads)
    return pl.pallas_call(
        body,
        grid=(batch, nb),
        in_specs=[
            pl.BlockSpec((blk, width), lambda b, i: (b * nb + i, COL_QA // wb)),
            pl.BlockSpec((seq, width), lambda b, i: (b, COL_KA // wb)),
            pl.BlockSpec((seq, width), lambda b, i: (b, COL_VA // wb)),
            pl.BlockSpec((1, LANES), lambda b, i: (0, 0)),
            pl.BlockSpec((1, LANES), lambda b, i: (0, 0)),
        ],
        out_specs=pl.BlockSpec((blk, width), lambda b, i: (b * nb + i, 0)),
        out_shape=jax.ShapeDtypeStruct((batch * seq, width), BF16),
        scratch_shapes=[
            pltpu.VMEM((nheads, seq, LANES), BF16),
            pltpu.VMEM((nheads, v_rows, seq), BF16),
            pltpu.VMEM((nheads, nb, LANES), F32),
            pltpu.VMEM((nheads, LANES, blk), BF16),
            pltpu.VMEM((nheads, 2 * blk, blk), F32),
            pltpu.VMEM((nheads, 1, blk), F32),
            pltpu.VMEM((nheads, 1, blk), F32),
            pltpu.VMEM((nheads, v_rows, blk), F32),
        ],
        compiler_params=_params("arbitrary", "arbitrary"),
        name="moba",
    )(proj, proj, proj, gq2, gk2)


def _log_sigmoid(x):
    return jnp.minimum(x, 0.0) - jnp.log(1.0 + jnp.exp(-jnp.abs(x)))


def _dot_tri(tri, x, tri_left):
    out = None
    for _ in range(3):
        piece = x.astype(BF16)
        x = x - piece.astype(F32)
        term = (jnp.dot(tri, piece, preferred_element_type=F32) if tri_left
                else jnp.dot(piece, tri, preferred_element_type=F32))
        out = term if out is None else out + term
    return out


def _mlstm_body(qr_ref, kr_ref, v_ref, og_ref, gcol_ref, grow_ref, brow_ref, bcol_ref,
                cwq_ref, cwk_ref, cbq_ref, cbk_ref, gh_ref, o_ref,
                qx_ref, kx_ref, c_ref, m_ref, *, chunk, dk, nh):
    L = chunk
    width = nh * dk

    @pl.when(pl.program_id(1) == 0)
    def _():
        qx_ref[0:SUBLANES, :] = jnp.zeros((SUBLANES, width), F32)
        kx_ref[0:SUBLANES, :] = jnp.zeros((SUBLANES, width), F32)
        c_ref[...] = jnp.zeros_like(c_ref)
        m_ref[...] = jnp.zeros_like(m_ref)

    qx_ref[SUBLANES:SUBLANES + L, :] = qr_ref[...].astype(F32)
    kx_ref[SUBLANES:SUBLANES + L, :] = kr_ref[...].astype(F32)

    def conv_silu(x_ref, w_ref, b_ref):
        acc = b_ref[...] + w_ref[0:1, :] * x_ref[pl.ds(SUBLANES - CONV_WIDTH + 1, L), :]
        for j in range(1, CONV_WIDTH):
            acc = acc + w_ref[j:j + 1, :] * x_ref[pl.ds(SUBLANES - CONV_WIDTH + 1 + j, L), :]
        return acc * _sigmoid(acc)

    q_all = conv_silu(qx_ref, cwq_ref, cbq_ref)
    k_all = conv_silu(kx_ref, cwk_ref, cbk_ref) * (dk ** -0.5)
    qx_ref[0:SUBLANES, :] = qx_ref[L:L + SUBLANES, :]
    kx_ref[0:SUBLANES, :] = kx_ref[L:L + SUBLANES, :]

    pre_col = gcol_ref[...] + brow_ref[...]
    pre_row = grow_ref[0] + bcol_ref[...]
    t_i = lax.broadcasted_iota(jnp.int32, (L, L), 0)
    s_i = lax.broadcasted_iota(jnp.int32, (L, L), 1)
    tril = s_i <= t_i
    bcum_cols = _dot_tri(jnp.where(tril, 1.0, 0.0).astype(BF16), _log_sigmoid(pre_col), True)
    bcum_rows = _dot_tri(jnp.where(t_i <= s_i, 1.0, 0.0).astype(BF16), _log_sigmoid(pre_row), False)
    ones = jnp.ones((L, dk), BF16)

    for h in range(nh):
        cols = slice(h * dk, (h + 1) * dk)
        q = q_all[:, cols]
        k = k_all[:, cols]
        i_col = pre_col[:, h:h + 1]
        i_row = pre_row[h:h + 1, :]
        bcum_col = bcum_cols[:, nh + h:nh + h + 1]
        bcum_row = bcum_rows[nh + h:nh + h + 1, :]

        m_prev = m_ref[h, 0:1, 0:1]
        a_col = bcum_col + m_prev
        dmat = jnp.where(tril, bcum_col - bcum_row + i_row, -jnp.inf)
        m_t = jnp.maximum(a_col, jnp.max(dmat, axis=-1, keepdims=True))
        dw = jnp.exp(dmat - m_t)
        aw = jnp.exp(a_col - m_t)

        qb = q.astype(BF16)
        kb = k.astype(BF16)
        v_aug = jnp.concatenate([v_ref[:, cols], ones], axis=-1)
        sqk = _nt_dot(qb, kb) * dw
        num_aug = (aw * jnp.dot(qb, c_ref[h].astype(BF16), preferred_element_type=F32)
                   + jnp.dot(sqk.astype(BF16), v_aug, preferred_element_type=F32))
        den = num_aug[:, dk:dk + 1]
        hc = num_aug[:, 0:dk] / jnp.maximum(jnp.abs(den), jnp.exp(-m_t))

        b_last = bcum_col[L - 1:L, :]
        g_col = b_last - bcum_col + i_col
        m_new = jnp.maximum(b_last + m_prev, jnp.max(g_col, axis=0, keepdims=True))
        w_c = jnp.exp(b_last + m_prev - m_new)
        kw_t = (k * jnp.exp(g_col - m_new)).T.astype(BF16)
        c_ref[h] = w_c * c_ref[h] + jnp.dot(kw_t, v_aug, preferred_element_type=F32)
        m_ref[h] = jnp.broadcast_to(m_new, (1, LANES))

        o_ref[:, cols] = (_rms(hc, gh_ref[...]) * _sigmoid(og_ref[:, cols].astype(F32))).astype(o_ref.dtype)


def mlstm(proj, gates_col, gates_row, bias_row, bias_col, conv_w, conv_b, gh, *, batch, seq, chunk):
    nh = MLSTM_HEADS
    dk = MLSTM_DIM
    width = nh * dk
    wb = width // LANES
    nc = seq // chunk
    body = functools.partial(_mlstm_body, chunk=chunk, dk=dk, nh=nh)

    def rows(col0):
        return pl.BlockSpec((chunk, width), lambda b, c: (b * nc + c, col0 // wb))

    return pl.pallas_call(
        body,
        grid=(batch, nc),
        in_specs=[
            rows(COL_QM), rows(COL_KM), rows(COL_VM), rows(COL_OM),
            pl.BlockSpec((chunk, LANES), lambda b, c: (b * nc + c, 0)),
            pl.BlockSpec((1, SUBLANES, chunk), lambda b, c: (b, 0, c)),
            pl.BlockSpec((1, LANES), lambda b, c: (0, 0)),
            pl.BlockSpec((SUBLANES, 1), lambda b, c: (0, 0)),
            pl.BlockSpec((CONV_WIDTH, width), lambda b, c: (0, 0)),
            pl.BlockSpec((CONV_WIDTH, width), lambda b, c: (0, 1)),
            pl.BlockSpec((1, width), lambda b, c: (0, 0)),
            pl.BlockSpec((1, width), lambda b, c: (0, 1)),
            pl.BlockSpec((1, LANES), lambda b, c: (0, 0)),
        ],
        out_specs=pl.BlockSpec((chunk, width), lambda b, c: (b * nc + c, 0)),
        out_shape=jax.ShapeDtypeStruct((batch * seq, width), BF16),
        scratch_shapes=[
            pltpu.VMEM((chunk + 2 * SUBLANES, width), F32),
            pltpu.VMEM((chunk + 2 * SUBLANES, width), F32),
            pltpu.VMEM((nh, dk, 2 * dk), F32),
            pltpu.VMEM((nh, 1, LANES), F32),
        ],
        compiler_params=_params("arbitrary", "arbitrary"),
        name="mlstm",
    )(proj, proj, proj, proj, gates_col, gates_row, bias_row, bias_col,
      conv_w, conv_w, conv_b, conv_b, gh)


def _merge_body(ya_ref, yb_ref, ga_ref, gb_ref, x_ref, woa_ref, wob_ref, wout_ref, gffn_ref, *rest, moe):
    a = jnp.dot(ya_ref[...], woa_ref[...], preferred_element_type=F32)
    b = jnp.dot(yb_ref[...], wob_ref[...], preferred_element_type=F32)
    mixed = _sigmoid(ga_ref[...].astype(F32)) * a + _sigmoid(gb_ref[...].astype(F32)) * b
    x1 = x_ref[...] + jnp.dot(mixed.astype(BF16), wout_ref[...], preferred_element_type=F32)
    hf = _rms(x1, gffn_ref[...])
    if moe:
        wr_ref, x1_ref, hf_ref, lg_ref = rest
        hf_ref[...] = hf
        lg_ref[...] = _dot_split(hf, wr_ref[...])
    else:
        x1_ref, hf_ref = rest
        hf_ref[...] = hf.astype(BF16)
    x1_ref[...] = x1


def merge(ya, yb, proj, x2, woa, wob, wout, gffn, wr, *, tm):
    t, d = x2.shape
    moe = wr is not None
    full = lambda m: (0, 0)
    in_specs = [
        pl.BlockSpec((tm, ya.shape[1]), lambda m: (m, 0)),
        pl.BlockSpec((tm, yb.shape[1]), lambda m: (m, 0)),
        pl.BlockSpec((tm, d), lambda m: (m, COL_GA * LANES // d)),
        pl.BlockSpec((tm, d), lambda m: (m, COL_GB * LANES // d)),
        pl.BlockSpec((tm, d), lambda m: (m, 0)),
        pl.BlockSpec(woa.shape, full), pl.BlockSpec(wob.shape, full), pl.BlockSpec(wout.shape, full),
        pl.BlockSpec((1, d), full),
    ]
    args = [ya, yb, proj, proj, x2, woa, wob, wout, gffn]
    out_specs = [pl.BlockSpec((tm, d), lambda m: (m, 0)), pl.BlockSpec((tm, d), lambda m: (m, 0))]
    out_shape = [jax.ShapeDtypeStruct((t, d), F32), jax.ShapeDtypeStruct((t, d), F32 if moe else BF16)]
    if moe:
        in_specs.append(pl.BlockSpec(wr.shape, full))
        args.append(wr)
        out_specs.append(pl.BlockSpec((tm, LANES), lambda m: (m, 0)))
        out_shape.append(jax.ShapeDtypeStruct((t, LANES), F32))
    return pl.pallas_call(
        functools.partial(_merge_body, moe=moe),
        grid=(t // tm,),
        in_specs=in_specs, out_specs=out_specs, out_shape=out_shape,
        compiler_params=_params("arbitrary"),
        name="merge_moe" if moe else "merge",
    )(*args)


def _swiglu(x, w1_ref, w3_ref, w2_ref, g_ref, fc):
    dff = g_ref.shape[1]
    for f0 in range(0, dff, fc):
        a = jnp.dot(x, w1_ref[:, f0:f0 + fc], preferred_element_type=F32)
        b = jnp.dot(x, w3_ref[:, f0:f0 + fc], preferred_element_type=F32)
        g_ref[:, f0:f0 + fc] = (a * _sigmoid(a) * b).astype(BF16)
    return jnp.dot(g_ref[...], w2_ref[...], preferred_element_type=F32)


def _ple(x, p_ref, g_ref, wg_ref, wp_ref):
    gate = _sigmoid(jnp.dot(_rms(x, g_ref[...]).astype(BF16), wg_ref[...], preferred_element_type=F32))
    emb = jnp.dot(p_ref[...].astype(BF16), wp_ref[...], preferred_element_type=F32)
    return x + gate * emb


def _ffn_body(hf_ref, x1_ref, w1_ref, w3_ref, w2_ref, p_ref, g_ref, wg_ref, wp_ref, o_ref, act_ref, *, fc):
    x2 = x1_ref[...] + _swiglu(hf_ref[...], w1_ref, w3_ref, w2_ref, act_ref, fc)
    o_ref[...] = _ple(x2, p_ref, g_ref, wg_ref, wp_ref)


def dense_ffn(hf, x1, w1, w3, w2, p2, g, wg, wp, *, tm, fc):
    t, d = x1.shape
    dff = w1.shape[1]
    resident = dict(pipeline_mode=pl.Buffered(1))
    full = lambda m: (0, 0)
    return pl.pallas_call(
        functools.partial(_ffn_body, fc=fc),
        grid=(t // tm,),
        in_specs=[
            pl.BlockSpec((tm, d), lambda m: (m, 0)),
            pl.BlockSpec((tm, d), lambda m: (m, 0)),
            pl.BlockSpec((d, dff), full, **resident),
            pl.BlockSpec((d, dff), full, **resident),
            pl.BlockSpec((dff, d), full, **resident),
            pl.BlockSpec((tm, p2.shape[1]), lambda m: (m, 0)),
            pl.BlockSpec((1, d), full),
            pl.BlockSpec(wg.shape, full, **resident),
            pl.BlockSpec(wp.shape, full, **resident),
        ],
        out_specs=pl.BlockSpec((tm, d), lambda m: (m, 0)),
        out_shape=jax.ShapeDtypeStruct((t, d), F32),
        scratch_shapes=[pltpu.VMEM((tm, dff), BF16)],
        compiler_params=_params("arbitrary"),
        name="dense_ffn",
    )(hf, x1, w1, w3, w2, p2, g, wg, wp)


META_E0, META_E1, META_G0, META_G1, META_R0, META_R1 = 0, 1, 2, 3, 4, 5


def _route_body(lg_ref, meta_ref, cnt_ref, carry_ref, *, tm, ne):
    @pl.when(pl.program_id(0) == 0)
    def _():
        carry_ref[...] = jnp.zeros_like(carry_ref)

    lane = lax.broadcasted_iota(jnp.int32, (tm, LANES), 1)
    lanef = lane.astype(F32)
    lg = jnp.where(lane < ne, lg_ref[...], -jnp.inf)
    m1 = jnp.max(lg, axis=-1, keepdims=True)
    e1 = jnp.min(jnp.where(lg == m1, lanef, float(LANES)), axis=-1, keepdims=True)
    lg2 = jnp.where(lanef == e1, -jnp.inf, lg)
    m2 = jnp.max(lg2, axis=-1, keepdims=True)
    e2 = jnp.min(jnp.where(lg2 == m2, lanef, float(LANES)), axis=-1, keepdims=True)
    ex = jnp.exp(m2 - m1)
    g1 = 1.0 / (1.0 + ex)
    g2 = ex / (1.0 + ex)
    onehot = jnp.where((lanef == e1) | (lanef == e2), 1.0, 0.0)
    row = lax.broadcasted_iota(jnp.int32, (tm, tm), 0)
    col = lax.broadcasted_iota(jnp.int32, (tm, tm), 1)
    before = jnp.dot(jnp.where(col < row, 1.0, 0.0).astype(BF16), onehot.astype(BF16),
                     preferred_element_type=F32) + carry_ref[...]
    r1 = jnp.sum(jnp.where(lanef == e1, before, 0.0), axis=-1, keepdims=True)
    r2 = jnp.sum(jnp.where(lanef == e2, before, 0.0), axis=-1, keepdims=True)
    carry_ref[...] += jnp.sum(onehot, axis=0, keepdims=True)
    meta = jnp.zeros((tm, LANES), F32)
    for pos, val in ((META_E0, e1), (META_E1, e2), (META_G0, g1), (META_G1, g2), (META_R0, r1), (META_R1, r2)):
        meta = jnp.where(lane == pos, val, meta)
    meta_ref[...] = meta
    cnt_ref[...] = jnp.broadcast_to(carry_ref[...], cnt_ref.shape)


def route(logits, *, tm):
    t = logits.shape[0]
    return pl.pallas_call(
        functools.partial(_route_body, tm=tm, ne=N_EXPERTS),
        grid=(t // tm,),
        in_specs=[pl.BlockSpec((tm, LANES), lambda m: (m, 0))],
        out_specs=[pl.BlockSpec((tm, LANES), lambda m: (m, 0)),
                   pl.BlockSpec((SUBLANES, LANES), lambda m: (0, 0))],
        out_shape=[jax.ShapeDtypeStruct((t, LANES), F32), jax.ShapeDtypeStruct((SUBLANES, LANES), F32)],
        scratch_shapes=[pltpu.VMEM((1, LANES), F32)],
        compiler_params=_params("arbitrary"),
        name="route",
    )(logits)


def _dispatch_body(dest_ref, hf_ref, xs_in_ref, xs_ref, sem, *, tm, topk):
    del xs_in_ref
    base = pl.program_id(0) * tm * topk

    def issue(r, c):
        for k in range(topk):
            d = dest_ref[base + topk * r + k]
            pltpu.make_async_copy(hf_ref.at[pl.ds(r, 1)], xs_ref.at[pl.ds(d, 1)], sem).start(priority=k % 2)
        return c

    lax.fori_loop(0, tm, issue, 0, unroll=ROW_DMA_UNROLL)
    for k in range(topk):
        pltpu.make_async_copy(hf_ref, xs_ref.at[pl.ds(0, tm)], sem).wait()


def dispatch(dest, hf, xs_init, *, tm, topk):
    t, d = hf.shape
    return pl.pallas_call(
        functools.partial(_dispatch_body, tm=tm, topk=topk),
        grid_spec=pltpu.PrefetchScalarGridSpec(
            num_scalar_prefetch=1,
            grid=(t // tm,),
            in_specs=[pl.BlockSpec((tm, d), lambda m, dest: (m, 0)),
                      pl.BlockSpec(memory_space=pl.ANY)],
            out_specs=pl.BlockSpec(memory_space=pl.ANY),
            scratch_shapes=[pltpu.SemaphoreType.DMA],
        ),
        out_shape=jax.ShapeDtypeStruct(xs_init.shape, xs_init.dtype),
        input_output_aliases={2: 0},
        compiler_params=_params("arbitrary"),
        name="moe_dispatch",
    )(dest, hf, xs_init)


def _experts_body(te_ref, na_ref, xs_ref, w1_ref, w3_ref, w2_ref, y_ref, xb_ref, g_ref, *, fc):
    del te_ref
    f = pl.program_id(1)

    @pl.when(pl.program_id(0) >= na_ref[0])
    def _():
        y_ref[...] = jnp.zeros_like(y_ref)

    @pl.when(pl.program_id(0) < na_ref[0])
    def _():
        @pl.when(f == 0)
        def _():
            xb_ref[...] = xs_ref[...].astype(BF16)

        y = _swiglu(xb_ref[...], w1_ref.at[0], w3_ref.at[0], w2_ref.at[0], g_ref, fc)

        @pl.when(f == 0)
        def _():
            y_ref[...] = y

        @pl.when(f > 0)
        def _():
            y_ref[...] += y


def experts(tile_expert, n_active, xs, w1, w3, w2, *, tm, tf, fc):
    n_rows, d = xs.shape
    dff = w1.shape[2]
    row_tile = lambda i, f, te, na: (jnp.minimum(i, na[0] - 1), 0)
    ftile = lambda i, f, na: jnp.where(i < na[0], f, dff // tf - 1)
    return pl.pallas_call(
        functools.partial(_experts_body, fc=fc),
        grid_spec=pltpu.PrefetchScalarGridSpec(
            num_scalar_prefetch=2,
            grid=(n_rows // tm, dff // tf),
            in_specs=[
                pl.BlockSpec((tm, d), row_tile),
                pl.BlockSpec((1, d, tf), lambda i, f, te, na: (te[i], 0, ftile(i, f, na))),
                pl.BlockSpec((1, d, tf), lambda i, f, te, na: (te[i], 0, ftile(i, f, na))),
                pl.BlockSpec((1, tf, d), lambda i, f, te, na: (te[i], ftile(i, f, na), 0)),
            ],
            out_specs=pl.BlockSpec((tm, d), lambda i, f, te, na: (i, 0)),
            scratch_shapes=[pltpu.VMEM((tm, d), BF16), pltpu.VMEM((tm, tf), BF16)],
        ),
        out_shape=jax.ShapeDtypeStruct((n_rows, d), F32),
        compiler_params=_params("arbitrary", "arbitrary"),
        name="moe_experts",
    )(tile_expert, n_active, xs, w1, w3, w2)


def _combine_body(dest_ref, x1_ref, meta_ref, p_ref, g_ref, wg_ref, wp_ref, y_ref, o_ref, buf_ref, sem,
                  *, tm, topk):
    base = pl.program_id(0) * tm * topk

    def issue(r, c):
        for k in range(topk):
            d = dest_ref[base + topk * r + k]
            pltpu.make_async_copy(y_ref.at[pl.ds(d, 1)], buf_ref.at[k, pl.ds(r, 1)], sem).start(priority=k % 2)
        return c

    lax.fori_loop(0, tm, issue, 0, unroll=ROW_DMA_UNROLL)
    for k in range(topk):
        pltpu.make_async_copy(y_ref.at[pl.ds(0, tm)], buf_ref.at[k], sem).wait()
    meta = meta_ref[...]
    g0 = meta[:, META_G0:META_G0 + 1]
    g1 = meta[:, META_G1:META_G1 + 1]
    x2 = x1_ref[...] + (g0 * buf_ref[0] + g1 * buf_ref[1])
    o_ref[...] = _ple(x2, p_ref, g_ref, wg_ref, wp_ref)


def combine(dest, x1, meta, y, p2, g, wg, wp, *, tm, topk):
    t, d = x1.shape
    full = lambda m, dest: (0, 0)
    return pl.pallas_call(
        functools.partial(_combine_body, tm=tm, topk=topk),
        grid_spec=pltpu.PrefetchScalarGridSpec(
            num_scalar_prefetch=1,
            grid=(t // tm,),
            in_specs=[pl.BlockSpec((tm, d), lambda m, dest: (m, 0)),
                      pl.BlockSpec((tm, LANES), lambda m, dest: (m, 0)),
                      pl.BlockSpec((tm, p2.shape[1]), lambda m, dest: (m, 0)),
                      pl.BlockSpec((1, d), full), pl.BlockSpec(wg.shape, full), pl.BlockSpec(wp.shape, full),
                      pl.BlockSpec(memory_space=pl.ANY)],
            out_specs=pl.BlockSpec((tm, d), lambda m, dest: (m, 0)),
            scratch_shapes=[pltpu.VMEM((topk, tm, d), F32), pltpu.SemaphoreType.DMA],
        ),
        out_shape=jax.ShapeDtypeStruct((t, d), F32),
        compiler_params=_params("arbitrary"),
        name="moe_combine",
    )(dest, x1, meta, p2, g, wg, wp, y)


def moe_ffn(hf, x1, logits, w1, w3, w2, ple_args, *, tm_route, tm_rows, tf, tm_move):
    t, d = x1.shape
    topk = 2
    meta, cnt = route(logits, tm=tm_route)
    counts = cnt[0, :N_EXPERTS].astype(jnp.int32)
    padded = ((counts + tm_rows - 1) // tm_rows) * tm_rows
    pad_end = jnp.cumsum(padded)
    pad_start = pad_end - padded
    eidx = meta[:, META_E0:META_E1 + 1].astype(jnp.int32)
    rank = meta[:, META_R0:META_R1 + 1].astype(jnp.int32)
    dest = (pad_start[eidx] + rank).reshape(t * topk)
    n_tiles = -(-(t * topk) // tm_rows) + N_EXPERTS
    tile_start = jnp.arange(n_tiles, dtype=jnp.int32) * tm_rows
    tile_expert = jnp.minimum(jnp.sum(tile_start[:, None] >= pad_end[None, :], axis=1),
                              N_EXPERTS - 1).astype(jnp.int32)
    xs = dispatch(dest, hf, jnp.zeros((n_tiles * tm_rows, d), F32), tm=tm_move, topk=topk)
    n_active = (pad_end[N_EXPERTS - 1:] // tm_rows).astype(jnp.int32)
    y = experts(tile_expert, n_active, xs, w1, w3, w2, tm=tm_rows, tf=tf, fc=256)
    return combine(dest, x1, meta, y, *ple_args, tm=tm_move, topk=topk)


def _tile2(g):
    return jnp.concatenate([g, g]).reshape(1, 2 * g.shape[0])


def kernel(x, p, g_mix, w_in, g_q, g_k, conv_w, conv_b, b_i, b_f, g_h, w_oa, w_ob, w_out, g_ffn, w_d1, w_d3,
           w_d2, w_router, w_e1, w_e3, w_e2, g_ple, w_ple_gate, w_ple_proj):
    batch, seq, d = x.shape
    depth = w_in.shape[0]
    t = batch * seq
    nh = MLSTM_HEADS
    x2 = x.reshape(t, d)
    c_q, c_k, c_v = 0, 512, 1024
    c_qk, c_vm, c_om, c_i, c_f, c_ga, c_gb, c_end = 1536, 2560, 3072, 3584, 3588, 3592, 4616, 5640

    for l in range(depth):
        w = w_in[l]
        w_main = jnp.concatenate([w[:, c_ga:c_gb], w[:, c_gb:c_end], w[:, c_q:c_i]], axis=1).astype(BF16)
        w_if = jnp.pad(w[:, c_i:c_ga], ((0, 0), (0, LANES - 2 * nh))).astype(BF16)
        proj, gif = in_proj(x2, g_mix[l].reshape(1, d), w_main, w_if, tm=2048, tn=512)

        ya = moba(proj, _tile2(g_q[l]), _tile2(g_k[l]), batch=batch, seq=seq)

        bias = jnp.concatenate([b_i[l], b_f[l]])
        bias_row = jnp.pad(bias, (0, LANES - 2 * nh)).reshape(1, LANES)
        bias_col = bias.reshape(2 * nh, 1)
        gates_row = gif[:, :2 * nh].reshape(batch, seq, 2 * nh).transpose(0, 2, 1)
        yb = mlstm(proj, gif, gates_row, bias_row, bias_col, conv_w[l], conv_b[l].reshape(1, -1),
                   g_h[l].reshape(1, -1), batch=batch, seq=seq, chunk=256)

        j = l // 2
        moe = l % 2 == 1
        wr = jnp.pad(w_router[j], ((0, 0), (0, LANES - N_EXPERTS))) if moe else None
        outs = merge(ya, yb, proj, x2, w_oa[l].astype(BF16), w_ob[l].astype(BF16), w_out[l].astype(BF16),
                     g_ffn[l].reshape(1, d), wr, tm=512)
        ple_args = (p[l].reshape(t, -1), g_ple[l].reshape(1, d), w_ple_gate[l].astype(BF16),
                    w_ple_proj[l].astype(BF16))
        if moe:
            x1, hf, logits = outs
            x2 = moe_ffn(hf, x1, logits, w_e1[j].astype(BF16), w_e3[j].astype(BF16), w_e2[j].astype(BF16),
                         ple_args, tm_route=512, tm_rows=512, tf=1792, tm_move=512)
        else:
            x1, hf = outs
            x2 = dense_ffn(hf, x1, w_d1[j].astype(BF16), w_d3[j].astype(BF16), w_d2[j].astype(BF16),
                           *ple_args, tm=1024, fc=256)
    return x2.reshape(batch, seq, d)
```

```python
import functools

import jax
import jax.numpy as jnp
from jax import lax
from jax.experimental import pallas as pl
from jax.experimental.pallas import tpu as pltpu

F32 = jnp.float32
BF16 = jnp.bfloat16
HIGHEST = lax.Precision.HIGHEST

RMS_EPS = 1e-6
LANES = 128
SUBLANES = 8

MOBA_HEADS = 8
MOBA_HEAD_DIM = 64
MOBA_BLOCK = 256
MOBA_TOPK = 3
MLSTM_HEADS = 4
MLSTM_DIM = 128
CONV_WIDTH = 4
N_EXPERTS = 8

COL_GA, COL_GB = 0, 8
COL_QA, COL_KA, COL_VA = 16, 20, 24
COL_QM, COL_KM, COL_VM, COL_OM = 28, 32, 36, 40
N_PROJ = 44 * LANES

VMEM_LIMIT = 56 * 1024 * 1024
ROW_DMA_UNROLL = 8


def _params(*sem):
    return pltpu.CompilerParams(dimension_semantics=sem, vmem_limit_bytes=VMEM_LIMIT)


def _sigmoid(x):
    return 1.0 / (1.0 + jnp.exp(-x))


def _rms(x, g):
    return x * lax.rsqrt(jnp.mean(x * x, axis=-1, keepdims=True) + RMS_EPS) * g


def _split_bf16(x):
    hi = x.astype(BF16)
    return hi, (x - hi.astype(F32)).astype(BF16)


def _dot_split(a, b):
    ah, al = _split_bf16(a)
    bh, bl = _split_bf16(b)
    return (jnp.dot(ah, bh, preferred_element_type=F32) + jnp.dot(ah, bl, preferred_element_type=F32)
            + jnp.dot(al, bh, preferred_element_type=F32))


def _nt_dot(a, b, **kw):
    return lax.dot_general(a, b, (((1,), (1,)), ((), ())), preferred_element_type=F32, **kw)


def _in_proj_body(x_ref, g_ref, wa_ref, wb_ref, wif_ref, o_ref, oif_ref, h_ref, *, na):
    j = pl.program_id(1)

    @pl.when(j == 0)
    def _():
        h = _rms(x_ref[...], g_ref[...]).astype(BF16)
        h_ref[...] = h
        oif_ref[...] = jnp.dot(h, wif_ref[...], preferred_element_type=F32)

    @pl.when(j < na)
    def _():
        o_ref[...] = jnp.dot(h_ref[...], wa_ref[...], preferred_element_type=F32).astype(o_ref.dtype)

    @pl.when(j >= na)
    def _():
        o_ref[...] = jnp.dot(h_ref[...], wb_ref[...], preferred_element_type=F32).astype(o_ref.dtype)


def in_proj(x2, g, wa, wb, wif, *, tm, tn):
    t, d = x2.shape
    na = wa.shape[1] // tn
    n = wa.shape[1] + wb.shape[1]
    return pl.pallas_call(
        functools.partial(_in_proj_body, na=na),
        grid=(t // tm, n // tn),
        in_specs=[
            pl.BlockSpec((tm, d), lambda m, j: (m, 0)),
            pl.BlockSpec((1, d), lambda m, j: (0, 0)),
            pl.BlockSpec((d, tn), lambda m, j: (0, jnp.minimum(j, na - 1))),
            pl.BlockSpec((d, tn), lambda m, j: (0, jnp.maximum(j - na, 0))),
            pl.BlockSpec((d, LANES), lambda m, j: (0, 0)),
        ],
        out_specs=[
            pl.BlockSpec((tm, tn), lambda m, j: (m, j)),
            pl.BlockSpec((tm, LANES), lambda m, j: (m, 0)),
        ],
        out_shape=[jax.ShapeDtypeStruct((t, n), BF16), jax.ShapeDtypeStruct((t, LANES), F32)],
        scratch_shapes=[pltpu.VMEM((tm, d), BF16)],
        compiler_params=_params("arbitrary", "arbitrary"),
        name="in_proj",
    )(x2, g, wa, wb, wif)


MASK_BIAS = -1e30
LOG2_E = 1.4426950408889634


def _moba_body(q_ref, k_ref, v_ref, gq_ref, gk_ref, o_ref,
               kn_ref, vt_ref, kmean_ref, qaug_ref, s_ref, m_ref, alpha_ref, acc_ref,
               *, nb, blk, dh, topk, nheads):
    i = pl.program_id(1)
    pair = 2 * blk
    lane = lax.broadcasted_iota(jnp.int32, (1, LANES), 1)
    head0 = lane < dh

    def head_rms(x, g):
        x2 = x * x
        s0 = jnp.sum(jnp.where(head0, x2, 0.0), axis=-1, keepdims=True)
        s1 = jnp.sum(jnp.where(head0, 0.0, x2), axis=-1, keepdims=True)
        ms = jnp.where(head0, s0, s1) * (1.0 / dh)
        return x * lax.rsqrt(ms + RMS_EPS) * g

    @pl.when(i == 0)
    def _():
        def prep(j, c):
            r0 = pl.multiple_of(j * blk, blk)
            onehot = jnp.where(lane == dh + j, 1.0, 0.0)
            for p in range(nheads // 2):
                cols = slice(p * LANES, (p + 1) * LANES)
                kn = head_rms(k_ref[pl.ds(r0, blk), cols].astype(F32), gk_ref[...])
                for hh, kh in ((0, kn), (1, pltpu.roll(kn, dh, axis=1))):
                    h = 2 * p + hh
                    kmean_ref[h, pl.ds(j, 1), :] = jnp.mean(jnp.where(head0, kh, 0.0), axis=0, keepdims=True)
                    kn_ref[h, pl.ds(r0, blk), :] = jnp.where(head0, kh, onehot).astype(BF16)
                v_t = v_ref[pl.ds(r0, blk), cols].astype(F32).T.astype(BF16)
                for hh in range(2):
                    vt_ref[2 * p + hh, 0:dh, pl.ds(r0, blk)] = v_t[hh * dh:(hh + 1) * dh, :]
                    vt_ref[2 * p + hh, dh:, pl.ds(r0, blk)] = jnp.ones((vt_ref.shape[1] - dh, blk), BF16)
            return c

        lax.fori_loop(0, nb, prep, 0)

    jidx = lax.broadcasted_iota(jnp.int32, (nb, blk), 0)
    key_i = lax.broadcasted_iota(jnp.int32, (blk, blk), 0)
    qry_i = lax.broadcasted_iota(jnp.int32, (blk, blk), 1)
    causal = key_i <= qry_i
    r_own = pl.multiple_of(i * blk, blk)
    qk_scale = dh ** -0.5 * LOG2_E
    for p in range(nheads // 2):
        cols = slice(p * LANES, (p + 1) * LANES)
        qn_t = head_rms(q_ref[:, cols].astype(F32), gq_ref[...]).T
        for hh in range(2):
            h = 2 * p + hh
            q_t = qn_t[hh * dh:(hh + 1) * dh, :]
            gate = jnp.dot(kmean_ref[h], jnp.concatenate([q_t, jnp.zeros((LANES - dh, blk), F32)], axis=0),
                           precision=HIGHEST, preferred_element_type=F32)
            rank = jnp.zeros((nb, blk), F32)
            for jp in range(nb):
                row = gate[jp:jp + 1, :]
                beats = (row > gate) | ((row == gate) & (jidx > jp))
                rank = rank + jnp.where(beats, jnp.where(jp < i, 1.0, 0.0), 0.0)
            sel = (rank < topk) & (jidx < i)
            q_s = q_t * qk_scale
            pad = jnp.zeros((LANES - dh - nb, blk), F32)
            qaug_ref[h] = jnp.concatenate([q_s, jnp.where(sel, 0.0, MASK_BIAS), pad], axis=0).astype(BF16)
            qaug_own = jnp.concatenate([q_s, jnp.where(jidx == i, 0.0, MASK_BIAS), pad], axis=0).astype(BF16)
            st = jnp.dot(kn_ref[h, pl.ds(r_own, blk), :], qaug_own, preferred_element_type=F32)
            st = jnp.where(causal, st, -jnp.inf)
            s_ref[h, 0:blk, :] = st
            m_ref[h] = jnp.max(st, axis=0, keepdims=True)

    def finish_own(h):
        pr = jnp.exp2(s_ref[h, 0:blk, :] - m_ref[h]).astype(BF16)
        acc_ref[h] = jnp.dot(vt_ref[h, :, pl.ds(r_own, blk)], pr, preferred_element_type=F32)

    def score_pair(u, h):
        r0 = pl.multiple_of(u * pair, pair)
        st = jnp.dot(kn_ref[h, pl.ds(r0, pair), :], qaug_ref[h], preferred_element_type=F32)
        m_old = m_ref[h]
        m_new = jnp.maximum(m_old, jnp.max(st, axis=0, keepdims=True))
        s_ref[h] = st
        alpha_ref[h] = jnp.exp2(m_old - m_new)
        m_ref[h] = m_new

    def finish_pair(u, h):
        r0 = pl.multiple_of(u * pair, pair)
        pr = jnp.exp2(s_ref[h] - m_ref[h]).astype(BF16)
        acc_ref[h] = alpha_ref[h] * acc_ref[h] + jnp.dot(vt_ref[h, :, pl.ds(r0, pair)], pr,
                                                         preferred_element_type=F32)

    n_pairs = jnp.maximum((i + 1) // 2, 1)
    for h in range(nheads):
        finish_own(h)
        score_pair(0, h)

    def body(u, c):
        for h in range(nheads):
            finish_pair(u - 1, h)
            score_pair(u, h)
        return c

    lax.fori_loop(1, n_pairs, body, 0)
    for h in range(nheads):
        finish_pair(n_pairs - 1, h)

    for p in range(nheads // 2):
        a0 = acc_ref[2 * p]
        a1 = acc_ref[2 * p + 1]
        ot = jnp.concatenate([a0[0:dh] / a0[dh:dh + 1], a1[0:dh] / a1[dh:dh + 1]], axis=0)
        o_ref[:, p * LANES:(p + 1) * LANES] = ot.T.astype(o_ref.dtype)


def moba(proj, gq2, gk2, *, batch, seq):
    nb = seq // MOBA_BLOCK
    blk = MOBA_BLOCK
    dh = MOBA_HEAD_DIM
    nheads = MOBA_HEADS
    width = nheads * dh
    wb = width // LANES
    assert dh + nb <= LANES and 2 * dh == LANES and nb % 2 == 0
    v_rows = dh + 2 * SUBLANES
    body = functools.partial(_moba_body, nb=nb, blk=blk, dh=dh, topk=MOBA_TOPK, nheads=nheads)
    return pl.pallas_call(
        body,
        grid=(batch, nb),
        in_specs=[
            pl.BlockSpec((blk, width), lambda b, i: (b * nb + i, COL_QA // wb)),
            pl.BlockSpec((seq, width), lambda b, i: (b, COL_KA // wb)),
            pl.BlockSpec((seq, width), lambda b, i: (b, COL_VA // wb)),
            pl.BlockSpec((1, LANES), lambda b, i: (0, 0)),
            pl.BlockSpec((1, LANES), lambda b, i: (0, 0)),
        ],
        out_specs=pl.BlockSpec((blk, width), lambda b, i: (b * nb + i, 0)),
        out_shape=jax.ShapeDtypeStruct((batch * seq, width), BF16),
        scratch_shapes=[
            pltpu.VMEM((nheads, seq, LANES), BF16),
            pltpu.VMEM((nheads, v_rows, seq), BF16),
            pltpu.VMEM((nheads, nb, LANES), F32),
            pltpu.VMEM((nheads, LANES, blk), BF16),
            pltpu.VMEM((nheads, 2 * blk, blk), F32),
            pltpu.VMEM((nheads, 1, blk), F32),
            pltpu.VMEM((nheads, 1, blk), F32),
            pltpu.VMEM((nheads, v_rows, blk), F32),
        ],
        compiler_params=_params("arbitrary", "arbitrary"),
        name="moba",
    )(proj, proj, proj, gq2, gk2)


def _log_sigmoid(x):
    return jnp.minimum(x, 0.0) - jnp.log(1.0 + jnp.exp(-jnp.abs(x)))


def _dot_tri(tri, x, tri_left):
    out = None
    for _ in range(3):
        piece = x.astype(BF16)
        x = x - piece.astype(F32)
        term = (jnp.dot(tri, piece, preferred_element_type=F32) if tri_left
                else jnp.dot(piece, tri, preferred_element_type=F32))
        out = term if out is None else out + term
    return out


def _mlstm_body(qr_ref, kr_ref, v_ref, og_ref, gcol_ref, grow_ref, brow_ref, bcol_ref,
                cwq_ref, cwk_ref, cbq_ref, cbk_ref, gh_ref, o_ref,
                qx_ref, kx_ref, c_ref, m_ref, *, chunk, dk, nh):
    L = chunk
    width = nh * dk

    @pl.when(pl.program_id(1) == 0)
    def _():
        qx_ref[0:SUBLANES, :] = jnp.zeros((SUBLANES, width), F32)
        kx_ref[0:SUBLANES, :] = jnp.zeros((SUBLANES, width), F32)
        c_ref[...] = jnp.zeros_like(c_ref)
        m_ref[...] = jnp.zeros_like(m_ref)

    qx_ref[SUBLANES:SUBLANES + L, :] = qr_ref[...].astype(F32)
    kx_ref[SUBLANES:SUBLANES + L, :] = kr_ref[...].astype(F32)

    def conv_silu(x_ref, w_ref, b_ref):
        acc = b_ref[...] + w_ref[0:1, :] * x_ref[pl.ds(SUBLANES - CONV_WIDTH + 1, L), :]
        for j in range(1, CONV_WIDTH):
            acc = acc + w_ref[j:j + 1, :] * x_ref[pl.ds(SUBLANES - CONV_WIDTH + 1 + j, L), :]
        return acc * _sigmoid(acc)

    q_all = conv_silu(qx_ref, cwq_ref, cbq_ref)
    k_all = conv_silu(kx_ref, cwk_ref, cbk_ref) * (dk ** -0.5)
    qx_ref[0:SUBLANES, :] = qx_ref[L:L + SUBLANES, :]
    kx_ref[0:SUBLANES, :] = kx_ref[L:L + SUBLANES, :]

    pre_col = gcol_ref[...] + brow_ref[...]
    pre_row = grow_ref[0] + bcol_ref[...]
    t_i = lax.broadcasted_iota(jnp.int32, (L, L), 0)
    s_i = lax.broadcasted_iota(jnp.int32, (L, L), 1)
    tril = s_i <= t_i
    bcum_cols = _dot_tri(jnp.where(tril, 1.0, 0.0).astype(BF16), _log_sigmoid(pre_col), True)
    bcum_rows = _dot_tri(jnp.where(t_i <= s_i, 1.0, 0.0).astype(BF16), _log_sigmoid(pre_row), False)
    ones = jnp.ones((L, dk), BF16)

    for h in range(nh):
        cols = slice(h * dk, (h + 1) * dk)
        q = q_all[:, cols]
        k = k_all[:, cols]
        i_col = pre_col[:, h:h + 1]
        i_row = pre_row[h:h + 1, :]
        bcum_col = bcum_cols[:, nh + h:nh + h + 1]
        bcum_row = bcum_rows[nh + h:nh + h + 1, :]

        m_prev = m_ref[h, 0:1, 0:1]
        a_col = bcum_col + m_prev
        dmat = jnp.where(tril, bcum_col - bcum_row + i_row, -jnp.inf)
        m_t = jnp.maximum(a_col, jnp.max(dmat, axis=-1, keepdims=True))
        dw = jnp.exp(dmat - m_t)
        aw = jnp.exp(a_col - m_t)

        qb = q.astype(BF16)
        kb = k.astype(BF16)
        v_aug = jnp.concatenate([v_ref[:, cols], ones], axis=-1)
        sqk = _nt_dot(qb, kb) * dw
        num_aug = (aw * jnp.dot(qb, c_ref[h].astype(BF16), preferred_element_type=F32)
                   + jnp.dot(sqk.astype(BF16), v_aug, preferred_element_type=F32))
        den = num_aug[:, dk:dk + 1]
        hc = num_aug[:, 0:dk] / jnp.maximum(jnp.abs(den), jnp.exp(-m_t))

        b_last = bcum_col[L - 1:L, :]
        g_col = b_last - bcum_col + i_col
        m_new = jnp.maximum(b_last + m_prev, jnp.max(g_col, axis=0, keepdims=True))
        w_c = jnp.exp(b_last + m_prev - m_new)
        kw_t = (k * jnp.exp(g_col - m_new)).T.astype(BF16)
        c_ref[h] = w_c * c_ref[h] + jnp.dot(kw_t, v_aug, preferred_element_type=F32)
        m_ref[h] = jnp.broadcast_to(m_new, (1, LANES))

        o_ref[:, cols] = (_rms(hc, gh_ref[...]) * _sigmoid(og_ref[:, cols].astype(F32))).astype(o_ref.dtype)


def mlstm(proj, gates_col, gates_row, bias_row, bias_col, conv_w, conv_b, gh, *, batch, seq, chunk):
    nh = MLSTM_HEADS
    dk = MLSTM_DIM
    width = nh * dk
    wb = width // LANES
    nc = seq // chunk
    body = functools.partial(_mlstm_body, chunk=chunk, dk=dk, nh=nh)

    def rows(col0):
        return pl.BlockSpec((chunk, width), lambda b, c: (b * nc + c, col0 // wb))

    return pl.pallas_call(
        body,
        grid=(batch, nc),
        in_specs=[
            rows(COL_QM), rows(COL_KM), rows(COL_VM), rows(COL_OM),
            pl.BlockSpec((chunk, LANES), lambda b, c: (b * nc + c, 0)),
            pl.BlockSpec((1, SUBLANES, chunk), lambda b, c: (b, 0, c)),
            pl.BlockSpec((1, LANES), lambda b, c: (0, 0)),
            pl.BlockSpec((SUBLANES, 1), lambda b, c: (0, 0)),
            pl.BlockSpec((CONV_WIDTH, width), lambda b, c: (0, 0)),
            pl.BlockSpec((CONV_WIDTH, width), lambda b, c: (0, 1)),
            pl.BlockSpec((1, width), lambda b, c: (0, 0)),
            pl.BlockSpec((1, width), lambda b, c: (0, 1)),
            pl.BlockSpec((1, LANES), lambda b, c: (0, 0)),
        ],
        out_specs=pl.BlockSpec((chunk, width), lambda b, c: (b * nc + c, 0)),
        out_shape=jax.ShapeDtypeStruct((batch * seq, width), BF16),
        scratch_shapes=[
            pltpu.VMEM((chunk + 2 * SUBLANES, width), F32),
            pltpu.VMEM((chunk + 2 * SUBLANES, width), F32),
            pltpu.VMEM((nh, dk, 2 * dk), F32),
            pltpu.VMEM((nh, 1, LANES), F32),
        ],
        compiler_params=_params("arbitrary", "arbitrary"),
        name="mlstm",
    )(proj, proj, proj, proj, gates_col, gates_row, bias_row, bias_col,
      conv_w, conv_w, conv_b, conv_b, gh)


def _merge_body(ya_ref, yb_ref, ga_ref, gb_ref, x_ref, woa_ref, wob_ref, wout_ref, gffn_ref, *rest, moe):
    a = jnp.dot(ya_ref[...], woa_ref[...], preferred_element_type=F32)
    b = jnp.dot(yb_ref[...], wob_ref[...], preferred_element_type=F32)
    mixed = _sigmoid(ga_ref[...].astype(F32)) * a + _sigmoid(gb_ref[...].astype(F32)) * b
    x1 = x_ref[...] + jnp.dot(mixed.astype(BF16), wout_ref[...], preferred_element_type=F32)
    hf = _rms(x1, gffn_ref[...])
    if moe:
        wr_ref, x1_ref, hf_ref, lg_ref = rest
        hf_ref[...] = hf
        lg_ref[...] = _dot_split(hf, wr_ref[...])
    else:
        x1_ref, hf_ref = rest
        hf_ref[...] = hf.astype(BF16)
    x1_ref[...] = x1


def merge(ya, yb, proj, x2, woa, wob, wout, gffn, wr, *, tm):
    t, d = x2.shape
    moe = wr is not None
    full = lambda m: (0, 0)
    in_specs = [
        pl.BlockSpec((tm, ya.shape[1]), lambda m: (m, 0)),
        pl.BlockSpec((tm, yb.shape[1]), lambda m: (m, 0)),
        pl.BlockSpec((tm, d), lambda m: (m, COL_GA * LANES // d)),
        pl.BlockSpec((tm, d), lambda m: (m, COL_GB * LANES // d)),
        pl.BlockSpec((tm, d), lambda m: (m, 0)),
        pl.BlockSpec(woa.shape, full), pl.BlockSpec(wob.shape, full), pl.BlockSpec(wout.shape, full),
        pl.BlockSpec((1, d), full),
    ]
    args = [ya, yb, proj, proj, x2, woa, wob, wout, gffn]
    out_specs = [pl.BlockSpec((tm, d), lambda m: (m, 0)), pl.BlockSpec((tm, d), lambda m: (m, 0))]
    out_shape = [jax.ShapeDtypeStruct((t, d), F32), jax.ShapeDtypeStruct((t, d), F32 if moe else BF16)]
    if moe:
        in_specs.append(pl.BlockSpec(wr.shape, full))
        args.append(wr)
        out_specs.append(pl.BlockSpec((tm, LANES), lambda m: (m, 0)))
        out_shape.append(jax.ShapeDtypeStruct((t, LANES), F32))
    return pl.pallas_call(
        functools.partial(_merge_body, moe=moe),
        grid=(t // tm,),
        in_specs=in_specs, out_specs=out_specs, out_shape=out_shape,
        compiler_params=_params("arbitrary"),
        name="merge_moe" if moe else "merge",
    )(*args)


def _swiglu(x, w1_ref, w3_ref, w2_ref, g_ref, fc):
    dff = g_ref.shape[1]
    for f0 in range(0, dff, fc):
        a = jnp.dot(x, w1_ref[:, f0:f0 + fc], preferred_element_type=F32)
        b = jnp.dot(x, w3_ref[:, f0:f0 + fc], preferred_element_type=F32)
        g_ref[:, f0:f0 + fc] = (a * _sigmoid(a) * b).astype(BF16)
    return jnp.dot(g_ref[...], w2_ref[...], preferred_element_type=F32)


def _ple(x, p_ref, g_ref, wg_ref, wp_ref):
    gate = _sigmoid(jnp.dot(_rms(x, g_ref[...]).astype(BF16), wg_ref[...], preferred_element_type=F32))
    emb = jnp.dot(p_ref[...].astype(BF16), wp_ref[...], preferred_element_type=F32)
    return x + gate * emb


def _ffn_body(hf_ref, x1_ref, w1_ref, w3_ref, w2_ref, p_ref, g_ref, wg_ref, wp_ref, o_ref, act_ref, *, fc):
    x2 = x1_ref[...] + _swiglu(hf_ref[...], w1_ref, w3_ref, w2_ref, act_ref, fc)
    o_ref[...] = _ple(x2, p_ref, g_ref, wg_ref, wp_ref)


def dense_ffn(hf, x1, w1, w3, w2, p2, p_row0, g, wg, wp, *, tm, fc):
    t, d = x1.shape
    dff = w1.shape[1]
    p_blk0 = p_row0 // tm
    resident = dict(pipeline_mode=pl.Buffered(1))
    full = lambda m: (0, 0)
    return pl.pallas_call(
        functools.partial(_ffn_body, fc=fc),
        grid=(t // tm,),
        in_specs=[
            pl.BlockSpec((tm, d), lambda m: (m, 0)),
            pl.BlockSpec((tm, d), lambda m: (m, 0)),
            pl.BlockSpec((d, dff), full, **resident),
            pl.BlockSpec((d, dff), full, **resident),
            pl.BlockSpec((dff, d), full, **resident),
            pl.BlockSpec((tm, p2.shape[1]), lambda m: (p_blk0 + m, 0)),
            pl.BlockSpec((1, d), full),
            pl.BlockSpec(wg.shape, full, **resident),
            pl.BlockSpec(wp.shape, full, **resident),
        ],
        out_specs=pl.BlockSpec((tm, d), lambda m: (m, 0)),
        out_shape=jax.ShapeDtypeStruct((t, d), F32),
        scratch_shapes=[pltpu.VMEM((tm, dff), BF16)],
        compiler_params=_params("arbitrary"),
        name="dense_ffn",
    )(hf, x1, w1, w3, w2, p2, g, wg, wp)


META_E0, META_E1, META_G0, META_G1, META_R0, META_R1 = 0, 1, 2, 3, 4, 5


def _route_body(lg_ref, meta_ref, tab_ref, cnt_ref, carry_ref, *, tm, ne):
    @pl.when(pl.program_id(0) == 0)
    def _():
        carry_ref[...] = jnp.zeros_like(carry_ref)

    lane = lax.broadcasted_iota(jnp.int32, (tm, LANES), 1)
    lanef = lane.astype(F32)
    lg = jnp.where(lane < ne, lg_ref[...], -jnp.inf)
    m1 = jnp.max(lg, axis=-1, keepdims=True)
    e1 = jnp.min(jnp.where(lg == m1, lanef, float(LANES)), axis=-1, keepdims=True)
    lg2 = jnp.where(lanef == e1, -jnp.inf, lg)
    m2 = jnp.max(lg2, axis=-1, keepdims=True)
    e2 = jnp.min(jnp.where(lg2 == m2, lanef, float(LANES)), axis=-1, keepdims=True)
    ex = jnp.exp(m2 - m1)
    g1 = 1.0 / (1.0 + ex)
    g2 = ex / (1.0 + ex)
    onehot = jnp.where((lanef == e1) | (lanef == e2), 1.0, 0.0)
    row = lax.broadcasted_iota(jnp.int32, (tm, tm), 0)
    col = lax.broadcasted_iota(jnp.int32, (tm, tm), 1)
    before = jnp.dot(jnp.where(col < row, 1.0, 0.0).astype(BF16), onehot.astype(BF16),
                     preferred_element_type=F32) + carry_ref[...]
    r1 = jnp.sum(jnp.where(lanef == e1, before, 0.0), axis=-1, keepdims=True)
    r2 = jnp.sum(jnp.where(lanef == e2, before, 0.0), axis=-1, keepdims=True)
    carry_ref[...] += jnp.sum(onehot, axis=0, keepdims=True)
    meta = jnp.zeros((tm, LANES), F32)
    for pos, val in ((META_E0, e1), (META_E1, e2), (META_G0, g1), (META_G1, g2), (META_R0, r1), (META_R1, r2)):
        meta = jnp.where(lane == pos, val, meta)
    meta_ref[...] = meta
    tab_ref[...] = meta.T[0:SUBLANES, :]
    cnt_ref[...] = jnp.broadcast_to(carry_ref[...], cnt_ref.shape)


def route(logits, *, tm):
    t = logits.shape[0]
    return pl.pallas_call(
        functools.partial(_route_body, tm=tm, ne=N_EXPERTS),
        grid=(t // tm,),
        in_specs=[pl.BlockSpec((tm, LANES), lambda m: (m, 0))],
        out_specs=[pl.BlockSpec((tm, LANES), lambda m: (m, 0)),
                   pl.BlockSpec((SUBLANES, tm), lambda m: (0, m)),
                   pl.BlockSpec((SUBLANES, LANES), lambda m: (0, 0))],
        out_shape=[jax.ShapeDtypeStruct((t, LANES), F32), jax.ShapeDtypeStruct((SUBLANES, t), F32),
                   jax.ShapeDtypeStruct((SUBLANES, LANES), F32)],
        scratch_shapes=[pltpu.VMEM((1, LANES), F32)],
        compiler_params=_params("arbitrary"),
        name="route",
    )(logits)


def _dispatch_body(dest_ref, pe_ref, na_ref, hf_ref, xs_ref, zero_ref, sem, zsem, *, tm, topk, tm_rows, n_tok):
    base = pl.program_id(0) * tm

    @pl.when(pl.program_id(0) == 0)
    def _():
        zero_ref[...] = jnp.zeros_like(zero_ref)
        n_tiles = xs_ref.shape[0] // tm_rows
        fills = []
        for e in range(N_EXPERTS):
            end = pe_ref[e]
            nonempty = end > (pe_ref[e - 1] if e else 0)
            fills.append((nonempty, pl.multiple_of(jnp.maximum(end - tm_rows, 0), tm_rows)))
        for tile in range(n_tiles):
            fills.append((tile >= na_ref[0], tile * tm_rows))
        for phase in ("start", "wait"):
            for cond, row0 in fills:
                @pl.when(cond)
                def _(row0=row0, phase=phase):
                    cp = pltpu.make_async_copy(zero_ref, xs_ref.at[pl.ds(row0, tm_rows)], zsem)
                    cp.start() if phase == "start" else cp.wait()

    def issue(r, c):
        for k in range(topk):
            d = dest_ref[k * n_tok + base + r]
            pltpu.make_async_copy(hf_ref.at[pl.ds(r, 1)], xs_ref.at[pl.ds(d, 1)], sem).start(priority=k % 2)
        return c

    lax.fori_loop(0, tm, issue, 0, unroll=ROW_DMA_UNROLL)
    for k in range(topk):
        pltpu.make_async_copy(hf_ref, xs_ref.at[pl.ds(0, tm)], sem).wait()


def dispatch(dest, pad_end, n_active, hf, *, n_rows, tm, topk, tm_rows):
    t, d = hf.shape
    return pl.pallas_call(
        functools.partial(_dispatch_body, tm=tm, topk=topk, tm_rows=tm_rows, n_tok=t),
        grid_spec=pltpu.PrefetchScalarGridSpec(
            num_scalar_prefetch=3,
            grid=(t // tm,),
            in_specs=[pl.BlockSpec((tm, d), lambda m, *_: (m, 0))],
            out_specs=pl.BlockSpec(memory_space=pl.ANY),
            scratch_shapes=[pltpu.VMEM((tm_rows, d), F32), pltpu.SemaphoreType.DMA, pltpu.SemaphoreType.DMA],
        ),
        out_shape=jax.ShapeDtypeStruct((n_rows, d), F32),
        compiler_params=_params("arbitrary"),
        name="moe_dispatch",
    )(dest, pad_end, n_active, hf)


def _experts_body(te_ref, na_ref, xs_ref, w1_ref, w3_ref, w2_ref, y_ref, xb_ref, g_ref, *, fc):
    del te_ref
    f = pl.program_id(1)

    @pl.when(pl.program_id(0) >= na_ref[0])
    def _():
        y_ref[...] = jnp.zeros_like(y_ref)

    @pl.when(pl.program_id(0) < na_ref[0])
    def _():
        @pl.when(f == 0)
        def _():
            xb_ref[...] = xs_ref[...].astype(BF16)

        y = _swiglu(xb_ref[...], w1_ref.at[0], w3_ref.at[0], w2_ref.at[0], g_ref, fc)

        @pl.when(f == 0)
        def _():
            y_ref[...] = y

        @pl.when(f > 0)
        def _():
            y_ref[...] += y


def experts(tile_expert, n_active, xs, w1, w3, w2, *, tm, tf, fc):
    n_rows, d = xs.shape
    dff = w1.shape[2]
    row_tile = lambda i, f, te, na: (jnp.minimum(i, na[0] - 1), 0)
    ftile = lambda i, f, na: jnp.where(i < na[0], f, dff // tf - 1)
    return pl.pallas_call(
        functools.partial(_experts_body, fc=fc),
        grid_spec=pltpu.PrefetchScalarGridSpec(
            num_scalar_prefetch=2,
            grid=(n_rows // tm, dff // tf),
            in_specs=[
                pl.BlockSpec((tm, d), row_tile),
                pl.BlockSpec((1, d, tf), lambda i, f, te, na: (te[i], 0, ftile(i, f, na))),
                pl.BlockSpec((1, d, tf), lambda i, f, te, na: (te[i], 0, ftile(i, f, na))),
                pl.BlockSpec((1, tf, d), lambda i, f, te, na: (te[i], ftile(i, f, na), 0)),
            ],
            out_specs=pl.BlockSpec((tm, d), lambda i, f, te, na: (i, 0)),
            scratch_shapes=[pltpu.VMEM((tm, d), BF16), pltpu.VMEM((tm, tf), BF16)],
        ),
        out_shape=jax.ShapeDtypeStruct((n_rows, d), F32),
        compiler_params=_params("arbitrary", "arbitrary"),
        name="moe_experts",
    )(tile_expert, n_active, xs, w1, w3, w2)


def _combine_body(dest_ref, x1_ref, meta_ref, p_ref, g_ref, wg_ref, wp_ref, y_ref, o_ref, buf_ref, sem,
                  *, tm, topk):
    base = pl.program_id(0) * tm
    n_tok = pl.num_programs(0) * tm

    def issue(r, c):
        for k in range(topk):
            d = dest_ref[k * n_tok + base + r]
            pltpu.make_async_copy(y_ref.at[pl.ds(d, 1)], buf_ref.at[k, pl.ds(r, 1)], sem).start(priority=k % 2)
        return c

    lax.fori_loop(0, tm, issue, 0, unroll=ROW_DMA_UNROLL)
    for k in range(topk):
        pltpu.make_async_copy(y_ref.at[pl.ds(0, tm)], buf_ref.at[k], sem).wait()
    meta = meta_ref[...]
    g0 = meta[:, META_G0:META_G0 + 1]
    g1 = meta[:, META_G1:META_G1 + 1]
    x2 = x1_ref[...] + (g0 * buf_ref[0] + g1 * buf_ref[1])
    o_ref[...] = _ple(x2, p_ref, g_ref, wg_ref, wp_ref)


def combine(dest, x1, meta, y, p2, p_row0, g, wg, wp, *, tm, topk):
    t, d = x1.shape
    full = lambda m, dest: (0, 0)
    p_blk0 = p_row0 // tm
    return pl.pallas_call(
        functools.partial(_combine_body, tm=tm, topk=topk),
        grid_spec=pltpu.PrefetchScalarGridSpec(
            num_scalar_prefetch=1,
            grid=(t // tm,),
            in_specs=[pl.BlockSpec((tm, d), lambda m, dest: (m, 0)),
                      pl.BlockSpec((tm, LANES), lambda m, dest: (m, 0)),
                      pl.BlockSpec((tm, p2.shape[1]), lambda m, dest: (p_blk0 + m, 0)),
                      pl.BlockSpec((1, d), full), pl.BlockSpec(wg.shape, full), pl.BlockSpec(wp.shape, full),
                      pl.BlockSpec(memory_space=pl.ANY)],
            out_specs=pl.BlockSpec((tm, d), lambda m, dest: (m, 0)),
            scratch_shapes=[pltpu.VMEM((topk, tm, d), F32), pltpu.SemaphoreType.DMA],
        ),
        out_shape=jax.ShapeDtypeStruct((t, d), F32),
        compiler_params=_params("arbitrary"),
        name="moe_combine",
    )(dest, x1, meta, p2, g, wg, wp, y)


def moe_ffn(hf, x1, logits, w1, w3, w2, ple_args, *, tm_route, tm_rows, tf, tm_move):
    t, d = x1.shape
    topk = 2
    meta, tab, cnt = route(logits, tm=tm_route)
    counts = cnt[0, :N_EXPERTS].astype(jnp.int32)
    padded = ((counts + tm_rows - 1) // tm_rows) * tm_rows
    pad_end = jnp.cumsum(padded).astype(jnp.int32)
    pad_start = pad_end - padded
    eidx = tab[META_E0:META_E1 + 1].astype(jnp.int32)
    rank = tab[META_R0:META_R1 + 1].astype(jnp.int32)
    dest = (pad_start[eidx] + rank).reshape(topk * t)
    n_tiles = -(-(t * topk) // tm_rows) + N_EXPERTS
    tile_start = jnp.arange(n_tiles, dtype=jnp.int32) * tm_rows
    tile_expert = jnp.minimum(jnp.sum(tile_start[:, None] >= pad_end[None, :], axis=1),
                              N_EXPERTS - 1).astype(jnp.int32)
    n_active = pad_end[N_EXPERTS - 1:] // tm_rows
    xs = dispatch(dest, pad_end, n_active, hf, n_rows=n_tiles * tm_rows, tm=tm_move, topk=topk, tm_rows=tm_rows)
    y = experts(tile_expert, n_active, xs, w1, w3, w2, tm=tm_rows, tf=tf, fc=256)
    return combine(dest, x1, meta, y, *ple_args, tm=tm_move, topk=topk)


def _tile2(g):
    return jnp.concatenate([g, g]).reshape(1, 2 * g.shape[0])


def kernel(x, p, g_mix, w_in, g_q, g_k, conv_w, conv_b, b_i, b_f, g_h, w_oa, w_ob, w_out, g_ffn, w_d1, w_d3,
           w_d2, w_router, w_e1, w_e3, w_e2, g_ple, w_ple_gate, w_ple_proj):
    batch, seq, d = x.shape
    depth = w_in.shape[0]
    t = batch * seq
    nh = MLSTM_HEADS
    x2 = x.reshape(t, d)
    c_q, c_k, c_v = 0, 512, 1024
    c_qk, c_vm, c_om, c_i, c_f, c_ga, c_gb, c_end = 1536, 2560, 3072, 3584, 3588, 3592, 4616, 5640

    for l in range(depth):
        w = w_in[l]
        w_gates = w[:, c_ga:c_end].astype(BF16)
        w_rest = w[:, c_q:c_i].astype(BF16)
        w_if = jnp.pad(w[:, c_i:c_ga], ((0, 0), (0, LANES - 2 * nh))).astype(BF16)
        proj, gif = in_proj(x2, g_mix[l].reshape(1, d), w_gates, w_rest, w_if, tm=2048, tn=512)

        ya = moba(proj, _tile2(g_q[l]), _tile2(g_k[l]), batch=batch, seq=seq)

        bias = jnp.concatenate([b_i[l], b_f[l]])
        bias_row = jnp.pad(bias, (0, LANES - 2 * nh)).reshape(1, LANES)
        bias_col = bias.reshape(2 * nh, 1)
        gates_row = gif[:, :2 * nh].reshape(batch, seq, 2 * nh).transpose(0, 2, 1)
        yb = mlstm(proj, gif, gates_row, bias_row, bias_col, conv_w[l], conv_b[l].reshape(1, -1),
                   g_h[l].reshape(1, -1), batch=batch, seq=seq, chunk=256)

        j = l // 2
        moe = l % 2 == 1
        wr = jnp.pad(w_router[j], ((0, 0), (0, LANES - N_EXPERTS))) if moe else None
        outs = merge(ya, yb, proj, x2, w_oa[l].astype(BF16), w_ob[l].astype(BF16), w_out[l].astype(BF16),
                     g_ffn[l].reshape(1, d), wr, tm=512)
        ple_args = (p.reshape(depth * t, -1), l * t, g_ple[l].reshape(1, d), w_ple_gate[l].astype(BF16),
                    w_ple_proj[l].astype(BF16))
        if moe:
            x1, hf, logits = outs
            x2 = moe_ffn(hf, x1, logits, w_e1[j].astype(BF16), w_e3[j].astype(BF16), w_e2[j].astype(BF16),
                         ple_args, tm_route=512, tm_rows=512, tf=1792, tm_move=512)
        else:
            x1, hf = outs
            x2 = dense_ffn(hf, x1, w_d1[j].astype(BF16), w_d3[j].astype(BF16), w_d2[j].astype(BF16),
                           *ple_args, tm=1024, fc=256)
    return x2.reshape(batch, seq, d)
```

```python
import functools

import jax
import jax.numpy as jnp
from jax import lax
from jax.experimental import pallas as pl
from jax.experimental.pallas import tpu as pltpu

F32 = jnp.float32
BF16 = jnp.bfloat16
HIGHEST = lax.Precision.HIGHEST

RMS_EPS = 1e-6
LANES = 128
SUBLANES = 8

MOBA_HEADS = 8
MOBA_HEAD_DIM = 64
MOBA_BLOCK = 256
MOBA_TOPK = 3
MLSTM_HEADS = 4
MLSTM_DIM = 128
CONV_WIDTH = 4
N_EXPERTS = 8

COL_GA, COL_GB = 0, 8
COL_QA, COL_KA, COL_VA = 16, 20, 24
COL_QM, COL_KM, COL_VM, COL_OM = 28, 32, 36, 40
N_PROJ = 44 * LANES

VMEM_LIMIT = 56 * 1024 * 1024
ROW_DMA_UNROLL = 8


def _params(*sem):
    return pltpu.CompilerParams(dimension_semantics=sem, vmem_limit_bytes=VMEM_LIMIT)


def _sigmoid(x):
    return 1.0 / (1.0 + jnp.exp(-x))


def _rms(x, g):
    return x * lax.rsqrt(jnp.mean(x * x, axis=-1, keepdims=True) + RMS_EPS) * g


def _split_bf16(x):
    hi = x.astype(BF16)
    return hi, (x - hi.astype(F32)).astype(BF16)


def _dot_split(a, b):
    ah, al = _split_bf16(a)
    bh, bl = _split_bf16(b)
    return (jnp.dot(ah, bh, preferred_element_type=F32) + jnp.dot(ah, bl, preferred_element_type=F32)
            + jnp.dot(al, bh, preferred_element_type=F32))


def _nt_dot(a, b, **kw):
    return lax.dot_general(a, b, (((1,), (1,)), ((), ())), preferred_element_type=F32, **kw)


def _in_proj_body(x_ref, g_ref, wa_ref, wb_ref, wif_ref, o_ref, oif_ref, h_ref, *, na):
    j = pl.program_id(1)

    @pl.when(j == 0)
    def _():
        h = _rms(x_ref[...], g_ref[...]).astype(BF16)
        h_ref[...] = h
        oif_ref[...] = jnp.dot(h, wif_ref[...], preferred_element_type=F32)

    @pl.when(j < na)
    def _():
        o_ref[...] = jnp.dot(h_ref[...], wa_ref[...], preferred_element_type=F32).astype(o_ref.dtype)

    @pl.when(j >= na)
    def _():
        o_ref[...] = jnp.dot(h_ref[...], wb_ref[...], preferred_element_type=F32).astype(o_ref.dtype)


def in_proj(x2, g, wa, w_full, *, nb_cols, if_col, tm, tn):
    t, d = x2.shape
    na = wa.shape[1] // tn
    n = wa.shape[1] + nb_cols
    wb = wif = w_full
    return pl.pallas_call(
        functools.partial(_in_proj_body, na=na),
        grid=(t // tm, n // tn),
        in_specs=[
            pl.BlockSpec((tm, d), lambda m, j: (m, 0)),
            pl.BlockSpec((1, d), lambda m, j: (0, 0)),
            pl.BlockSpec((d, tn), lambda m, j: (0, jnp.minimum(j, na - 1))),
            pl.BlockSpec((d, tn), lambda m, j: (0, jnp.maximum(j - na, 0))),
            pl.BlockSpec((d, LANES), lambda m, j: (0, if_col // LANES)),
        ],
        out_specs=[
            pl.BlockSpec((tm, tn), lambda m, j: (m, j)),
            pl.BlockSpec((tm, LANES), lambda m, j: (m, 0)),
        ],
        out_shape=[jax.ShapeDtypeStruct((t, n), BF16), jax.ShapeDtypeStruct((t, LANES), F32)],
        scratch_shapes=[pltpu.VMEM((tm, d), BF16)],
        compiler_params=_params("arbitrary", "arbitrary"),
        name="in_proj",
    )(x2, g, wa, wb, wif)


MASK_BIAS = -1e30
LOG2_E = 1.4426950408889634


def _moba_body(q_ref, k_ref, v_ref, gq_ref, gk_ref, o_ref,
               kn_ref, vt_ref, kmean_ref, qaug_ref, s_ref, m_ref, alpha_ref, acc_ref,
               *, nb, blk, dh, topk, nheads):
    i = pl.program_id(1)
    pair = 2 * blk
    lane = lax.broadcasted_iota(jnp.int32, (1, LANES), 1)
    head0 = lane < dh

    def head_rms(x, g):
        x2 = x * x
        s0 = jnp.sum(jnp.where(head0, x2, 0.0), axis=-1, keepdims=True)
        s1 = jnp.sum(jnp.where(head0, 0.0, x2), axis=-1, keepdims=True)
        ms = jnp.where(head0, s0, s1) * (1.0 / dh)
        return x * lax.rsqrt(ms + RMS_EPS) * g

    @pl.when(i == 0)
    def _():
        def prep(j, c):
            r0 = pl.multiple_of(j * blk, blk)
            onehot = jnp.where(lane == dh + j, 1.0, 0.0)
            for p in range(nheads // 2):
                cols = slice(p * LANES, (p + 1) * LANES)
                kn = head_rms(k_ref[pl.ds(r0, blk), cols].astype(F32), gk_ref[...])
                for hh, kh in ((0, kn), (1, pltpu.roll(kn, dh, axis=1))):
                    h = 2 * p + hh
                    kmean_ref[h, pl.ds(j, 1), :] = jnp.mean(jnp.where(head0, kh, 0.0), axis=0, keepdims=True)
                    kn_ref[h, pl.ds(r0, blk), :] = jnp.where(head0, kh, onehot).astype(BF16)
                v_t = v_ref[pl.ds(r0, blk), cols].astype(F32).T.astype(BF16)
                for hh in range(2):
                    vt_ref[2 * p + hh, 0:dh, pl.ds(r0, blk)] = v_t[hh * dh:(hh + 1) * dh, :]
                    vt_ref[2 * p + hh, dh:, pl.ds(r0, blk)] = jnp.ones((vt_ref.shape[1] - dh, blk), BF16)
            return c

        lax.fori_loop(0, nb, prep, 0)

    jidx = lax.broadcasted_iota(jnp.int32, (nb, blk), 0)
    key_i = lax.broadcasted_iota(jnp.int32, (blk, blk), 0)
    qry_i = lax.broadcasted_iota(jnp.int32, (blk, blk), 1)
    causal = key_i <= qry_i
    r_own = pl.multiple_of(i * blk, blk)
    qk_scale = dh ** -0.5 * LOG2_E
    for p in range(nheads // 2):
        cols = slice(p * LANES, (p + 1) * LANES)
        qn_t = head_rms(q_ref[:, cols].astype(F32), gq_ref[...]).T
        for hh in range(2):
            h = 2 * p + hh
            q_t = qn_t[hh * dh:(hh + 1) * dh, :]
            gate = jnp.dot(kmean_ref[h], jnp.concatenate([q_t, jnp.zeros((LANES - dh, blk), F32)], axis=0),
                           precision=HIGHEST, preferred_element_type=F32)
            rank = jnp.zeros((nb, blk), F32)
            for jp in range(nb):
                row = gate[jp:jp + 1, :]
                beats = (row > gate) | ((row == gate) & (jidx > jp))
                rank = rank + jnp.where(beats, jnp.where(jp < i, 1.0, 0.0), 0.0)
            sel = (rank < topk) & (jidx < i)
            q_s = q_t * qk_scale
            pad = jnp.zeros((LANES - dh - nb, blk), F32)
            qaug_ref[h] = jnp.concatenate([q_s, jnp.where(sel, 0.0, MASK_BIAS), pad], axis=0).astype(BF16)
            qaug_own = jnp.concatenate([q_s, jnp.where(jidx == i, 0.0, MASK_BIAS), pad], axis=0).astype(BF16)
            st = jnp.dot(kn_ref[h, pl.ds(r_own, blk), :], qaug_own, preferred_element_type=F32)
            st = jnp.where(causal, st, -jnp.inf)
            s_ref[h, 0:blk, :] = st
            m_ref[h] = jnp.max(st, axis=0, keepdims=True)

    def finish_own(h):
        pr = jnp.exp2(s_ref[h, 0:blk, :] - m_ref[h]).astype(BF16)
        acc_ref[h] = jnp.dot(vt_ref[h, :, pl.ds(r_own, blk)], pr, preferred_element_type=F32)

    def score_pair(u, h):
        r0 = pl.multiple_of(u * pair, pair)
        st = jnp.dot(kn_ref[h, pl.ds(r0, pair), :], qaug_ref[h], preferred_element_type=F32)
        m_old = m_ref[h]
        m_new = jnp.maximum(m_old, jnp.max(st, axis=0, keepdims=True))
        s_ref[h] = st
        alpha_ref[h] = jnp.exp2(m_old - m_new)
        m_ref[h] = m_new

    def finish_pair(u, h):
        r0 = pl.multiple_of(u * pair, pair)
        pr = jnp.exp2(s_ref[h] - m_ref[h]).astype(BF16)
        acc_ref[h] = alpha_ref[h] * acc_ref[h] + jnp.dot(vt_ref[h, :, pl.ds(r0, pair)], pr,
                                                         preferred_element_type=F32)

    n_pairs = jnp.maximum((i + 1) // 2, 1)
    for h in range(nheads):
        finish_own(h)
        score_pair(0, h)

    def body(u, c):
        for h in range(nheads):
            finish_pair(u - 1, h)
            score_pair(u, h)
        return c

    lax.fori_loop(1, n_pairs, body, 0)
    for h in range(nheads):
        finish_pair(n_pairs - 1, h)

    for p in range(nheads // 2):
        a0 = acc_ref[2 * p]
        a1 = acc_ref[2 * p + 1]
        ot = jnp.concatenate([a0[0:dh] / a0[dh:dh + 1], a1[0:dh] / a1[dh:dh + 1]], axis=0)
        o_ref[:, p * LANES:(p + 1) * LANES] = ot.T.astype(o_ref.dtype)


def moba(proj, gq2, gk2, *, batch, seq):
    nb = seq // MOBA_BLOCK
    blk = MOBA_BLOCK
    dh = MOBA_HEAD_DIM
    nheads = MOBA_HEADS
    width = nheads * dh
    wb = width // LANES
    assert dh + nb <= LANES and 2 * dh == LANES and nb % 2 == 0
    v_rows = dh + 2 * SUBLANES
    body = functools.partial(_moba_body, nb=nb, blk=blk, dh=dh, topk=MOBA_TOPK, nheads=nheads)
    return pl.pallas_call(
        body,
        grid=(batch, nb),
        in_specs=[
            pl.BlockSpec((blk, width), lambda b, i: (b * nb + i, COL_QA // wb)),
            pl.BlockSpec((seq, width), lambda b, i: (b, COL_KA // wb)),
            pl.BlockSpec((seq, width), lambda b, i: (b, COL_VA // wb)),
            pl.BlockSpec((1, LANES), lambda b, i: (0, 0)),
            pl.BlockSpec((1, LANES), lambda b, i: (0, 0)),
        ],
        out_specs=pl.BlockSpec((blk, width), lambda b, i: (b * nb + i, 0)),
        out_shape=jax.ShapeDtypeStruct((batch * seq, width), BF16),
        scratch_shapes=[
            pltpu.VMEM((nheads, seq, LANES), BF16),
            pltpu.VMEM((nheads, v_rows, seq), BF16),
            pltpu.VMEM((nheads, nb, LANES), F32),
            pltpu.VMEM((nheads, LANES, blk), BF16),
            pltpu.VMEM((nheads, 2 * blk, blk), F32),
            pltpu.VMEM((nheads, 1, blk), F32),
            pltpu.VMEM((nheads, 1, blk), F32),
            pltpu.VMEM((nheads, v_rows, blk), F32),
        ],
        compiler_params=_params("arbitrary", "arbitrary"),
        name="moba",
    )(proj, proj, proj, gq2, gk2)


def _log_sigmoid(x):
    return jnp.minimum(x, 0.0) - jnp.log(1.0 + jnp.exp(-jnp.abs(x)))


def _dot_tri(tri, x, tri_left):
    out = None
    for _ in range(3):
        piece = x.astype(BF16)
        x = x - piece.astype(F32)
        term = (jnp.dot(tri, piece, preferred_element_type=F32) if tri_left
                else jnp.dot(piece, tri, preferred_element_type=F32))
        out = term if out is None else out + term
    return out


def _mlstm_body(qr_ref, kr_ref, v_ref, og_ref, gcol_ref, grow_ref, brow_ref, bcol_ref,
                cwq_ref, cwk_ref, cbq_ref, cbk_ref, gh_ref, o_ref,
                qx_ref, kx_ref, c_ref, m_ref, *, chunk, dk, nh):
    L = chunk
    width = nh * dk

    @pl.when(pl.program_id(1) == 0)
    def _():
        qx_ref[0:SUBLANES, :] = jnp.zeros((SUBLANES, width), F32)
        kx_ref[0:SUBLANES, :] = jnp.zeros((SUBLANES, width), F32)
        c_ref[...] = jnp.zeros_like(c_ref)
        m_ref[...] = jnp.zeros_like(m_ref)

    qx_ref[SUBLANES:SUBLANES + L, :] = qr_ref[...].astype(F32)
    kx_ref[SUBLANES:SUBLANES + L, :] = kr_ref[...].astype(F32)

    def conv_silu(x_ref, w_ref, b_ref):
        acc = b_ref[...] + w_ref[0:1, :] * x_ref[pl.ds(SUBLANES - CONV_WIDTH + 1, L), :]
        for j in range(1, CONV_WIDTH):
            acc = acc + w_ref[j:j + 1, :] * x_ref[pl.ds(SUBLANES - CONV_WIDTH + 1 + j, L), :]
        return acc * _sigmoid(acc)

    q_all = conv_silu(qx_ref, cwq_ref, cbq_ref)
    k_all = conv_silu(kx_ref, cwk_ref, cbk_ref) * (dk ** -0.5)
    qx_ref[0:SUBLANES, :] = qx_ref[L:L + SUBLANES, :]
    kx_ref[0:SUBLANES, :] = kx_ref[L:L + SUBLANES, :]

    pre_col = gcol_ref[...] + brow_ref[...]
    pre_row = grow_ref[0] + bcol_ref[...]
    t_i = lax.broadcasted_iota(jnp.int32, (L, L), 0)
    s_i = lax.broadcasted_iota(jnp.int32, (L, L), 1)
    tril = s_i <= t_i
    bcum_cols = _dot_tri(jnp.where(tril, 1.0, 0.0).astype(BF16), _log_sigmoid(pre_col), True)
    bcum_rows = _dot_tri(jnp.where(t_i <= s_i, 1.0, 0.0).astype(BF16), _log_sigmoid(pre_row), False)
    ones = jnp.ones((L, dk), BF16)

    for h in range(nh):
        cols = slice(h * dk, (h + 1) * dk)
        q = q_all[:, cols]
        k = k_all[:, cols]
        i_col = pre_col[:, h:h + 1]
        i_row = pre_row[h:h + 1, :]
        bcum_col = bcum_cols[:, nh + h:nh + h + 1]
        bcum_row = bcum_rows[nh + h:nh + h + 1, :]

        m_prev = m_ref[h, 0:1, 0:1]
        a_col = bcum_col + m_prev
        dmat = jnp.where(tril, bcum_col - bcum_row + i_row, -jnp.inf)
        m_t = jnp.maximum(a_col, jnp.max(dmat, axis=-1, keepdims=True))
        dw = jnp.exp(dmat - m_t)
        aw = jnp.exp(a_col - m_t)

        qb = q.astype(BF16)
        kb = k.astype(BF16)
        v_aug = jnp.concatenate([v_ref[:, cols], ones], axis=-1)
        sqk = _nt_dot(qb, kb) * dw
        num_aug = (aw * jnp.dot(qb, c_ref[h].astype(BF16), preferred_element_type=F32)
                   + jnp.dot(sqk.astype(BF16), v_aug, preferred_element_type=F32))
        den = num_aug[:, dk:dk + 1]
        hc = num_aug[:, 0:dk] / jnp.maximum(jnp.abs(den), jnp.exp(-m_t))

        b_last = bcum_col[L - 1:L, :]
        g_col = b_last - bcum_col + i_col
        m_new = jnp.maximum(b_last + m_prev, jnp.max(g_col, axis=0, keepdims=True))
        w_c = jnp.exp(b_last + m_prev - m_new)
        kw_t = (k * jnp.exp(g_col - m_new)).T.astype(BF16)
        c_ref[h] = w_c * c_ref[h] + jnp.dot(kw_t, v_aug, preferred_element_type=F32)
        m_ref[h] = jnp.broadcast_to(m_new, (1, LANES))

        o_ref[:, cols] = (_rms(hc, gh_ref[...]) * _sigmoid(og_ref[:, cols].astype(F32))).astype(o_ref.dtype)


def mlstm(proj, gates_col, gates_row, bias_row, bias_col, conv_w, conv_b, gh, *, batch, seq, chunk):
    nh = MLSTM_HEADS
    dk = MLSTM_DIM
    width = nh * dk
    wb = width // LANES
    nc = seq // chunk
    body = functools.partial(_mlstm_body, chunk=chunk, dk=dk, nh=nh)

    def rows(col0):
        return pl.BlockSpec((chunk, width), lambda b, c: (b * nc + c, col0 // wb))

    return pl.pallas_call(
        body,
        grid=(batch, nc),
        in_specs=[
            rows(COL_QM), rows(COL_KM), rows(COL_VM), rows(COL_OM),
            pl.BlockSpec((chunk, LANES), lambda b, c: (b * nc + c, 0)),
            pl.BlockSpec((1, SUBLANES, chunk), lambda b, c: (b, 0, c)),
            pl.BlockSpec((1, LANES), lambda b, c: (0, 0)),
            pl.BlockSpec((SUBLANES, 1), lambda b, c: (0, 0)),
            pl.BlockSpec((CONV_WIDTH, width), lambda b, c: (0, 0)),
            pl.BlockSpec((CONV_WIDTH, width), lambda b, c: (0, 1)),
            pl.BlockSpec((1, width), lambda b, c: (0, 0)),
            pl.BlockSpec((1, width), lambda b, c: (0, 1)),
            pl.BlockSpec((1, LANES), lambda b, c: (0, 0)),
        ],
        out_specs=pl.BlockSpec((chunk, width), lambda b, c: (b * nc + c, 0)),
        out_shape=jax.ShapeDtypeStruct((batch * seq, width), BF16),
        scratch_shapes=[
            pltpu.VMEM((chunk + 2 * SUBLANES, width), F32),
            pltpu.VMEM((chunk + 2 * SUBLANES, width), F32),
            pltpu.VMEM((nh, dk, 2 * dk), F32),
            pltpu.VMEM((nh, 1, LANES), F32),
        ],
        compiler_params=_params("arbitrary", "arbitrary"),
        name="mlstm",
    )(proj, proj, proj, proj, gates_col, gates_row, bias_row, bias_col,
      conv_w, conv_w, conv_b, conv_b, gh)


def _merge_body(ya_ref, yb_ref, ga_ref, gb_ref, x_ref, woa_ref, wob_ref, wout_ref, gffn_ref, *rest, moe):
    a = jnp.dot(ya_ref[...], woa_ref[...], preferred_element_type=F32)
    b = jnp.dot(yb_ref[...], wob_ref[...], preferred_element_type=F32)
    mixed = _sigmoid(ga_ref[...].astype(F32)) * a + _sigmoid(gb_ref[...].astype(F32)) * b
    x1 = x_ref[...] + jnp.dot(mixed.astype(BF16), wout_ref[...], preferred_element_type=F32)
    hf = _rms(x1, gffn_ref[...])
    if moe:
        wr_ref, x1_ref, hf_ref, lg_ref = rest
        hf_ref[...] = hf
        lg_ref[...] = _dot_split(hf, wr_ref[...])
    else:
        x1_ref, hf_ref = rest
        hf_ref[...] = hf.astype(BF16)
    x1_ref[...] = x1


def merge(ya, yb, proj, x2, woa, wob, wout, gffn, wr, *, tm):
    t, d = x2.shape
    moe = wr is not None
    full = lambda m: (0, 0)
    in_specs = [
        pl.BlockSpec((tm, ya.shape[1]), lambda m: (m, 0)),
        pl.BlockSpec((tm, yb.shape[1]), lambda m: (m, 0)),
        pl.BlockSpec((tm, d), lambda m: (m, COL_GA * LANES // d)),
        pl.BlockSpec((tm, d), lambda m: (m, COL_GB * LANES // d)),
        pl.BlockSpec((tm, d), lambda m: (m, 0)),
        pl.BlockSpec(woa.shape, full), pl.BlockSpec(wob.shape, full), pl.BlockSpec(wout.shape, full),
        pl.BlockSpec((1, d), full),
    ]
    args = [ya, yb, proj, proj, x2, woa, wob, wout, gffn]
    out_specs = [pl.BlockSpec((tm, d), lambda m: (m, 0)), pl.BlockSpec((tm, d), lambda m: (m, 0))]
    out_shape = [jax.ShapeDtypeStruct((t, d), F32), jax.ShapeDtypeStruct((t, d), F32 if moe else BF16)]
    if moe:
        in_specs.append(pl.BlockSpec(wr.shape, full))
        args.append(wr)
        out_specs.append(pl.BlockSpec((tm, LANES), lambda m: (m, 0)))
        out_shape.append(jax.ShapeDtypeStruct((t, LANES), F32))
    return pl.pallas_call(
        functools.partial(_merge_body, moe=moe),
        grid=(t // tm,),
        in_specs=in_specs, out_specs=out_specs, out_shape=out_shape,
        compiler_params=_params("arbitrary"),
        name="merge_moe" if moe else "merge",
    )(*args)


def _swiglu(x, w1_ref, w3_ref, w2_ref, g_ref, fc):
    dff = g_ref.shape[1]
    for f0 in range(0, dff, fc):
        a = jnp.dot(x, w1_ref[:, f0:f0 + fc], preferred_element_type=F32)
        b = jnp.dot(x, w3_ref[:, f0:f0 + fc], preferred_element_type=F32)
        g_ref[:, f0:f0 + fc] = (a * _sigmoid(a) * b).astype(BF16)
    return jnp.dot(g_ref[...], w2_ref[...], preferred_element_type=F32)


def _ple(x, p_ref, g_ref, wg_ref, wp_ref):
    gate = _sigmoid(jnp.dot(_rms(x, g_ref[...]).astype(BF16), wg_ref[...], preferred_element_type=F32))
    emb = jnp.dot(p_ref[...].astype(BF16), wp_ref[...], preferred_element_type=F32)
    return x + gate * emb


def _ffn_body(hf_ref, x1_ref, w1_ref, w3_ref, w2_ref, p_ref, g_ref, wg_ref, wp_ref, o_ref, act_ref, *, fc):
    x2 = x1_ref[...] + _swiglu(hf_ref[...], w1_ref, w3_ref, w2_ref, act_ref, fc)
    o_ref[...] = _ple(x2, p_ref, g_ref, wg_ref, wp_ref)


def dense_ffn(hf, x1, w1, w3, w2, p2, p_row0, g, wg, wp, *, tm, fc):
    t, d = x1.shape
    dff = w1.shape[1]
    p_blk0 = p_row0 // tm
    resident = dict(pipeline_mode=pl.Buffered(1))
    full = lambda m: (0, 0)
    return pl.pallas_call(
        functools.partial(_ffn_body, fc=fc),
        grid=(t // tm,),
        in_specs=[
            pl.BlockSpec((tm, d), lambda m: (m, 0)),
            pl.BlockSpec((tm, d), lambda m: (m, 0)),
            pl.BlockSpec((d, dff), full, **resident),
            pl.BlockSpec((d, dff), full, **resident),
            pl.BlockSpec((dff, d), full, **resident),
            pl.BlockSpec((tm, p2.shape[1]), lambda m: (p_blk0 + m, 0)),
            pl.BlockSpec((1, d), full),
            pl.BlockSpec(wg.shape, full, **resident),
            pl.BlockSpec(wp.shape, full, **resident),
        ],
        out_specs=pl.BlockSpec((tm, d), lambda m: (m, 0)),
        out_shape=jax.ShapeDtypeStruct((t, d), F32),
        scratch_shapes=[pltpu.VMEM((tm, dff), BF16)],
        compiler_params=_params("arbitrary"),
        name="dense_ffn",
    )(hf, x1, w1, w3, w2, p2, g, wg, wp)


META_E0, META_E1, META_G0, META_G1, META_R0, META_R1 = 0, 1, 2, 3, 4, 5


def _route_body(lg_ref, meta_ref, tab_ref, cnt_ref, carry_ref, *, tm, ne):
    @pl.when(pl.program_id(0) == 0)
    def _():
        carry_ref[...] = jnp.zeros_like(carry_ref)

    lane = lax.broadcasted_iota(jnp.int32, (tm, LANES), 1)
    lanef = lane.astype(F32)
    lg = jnp.where(lane < ne, lg_ref[...], -jnp.inf)
    m1 = jnp.max(lg, axis=-1, keepdims=True)
    e1 = jnp.min(jnp.where(lg == m1, lanef, float(LANES)), axis=-1, keepdims=True)
    lg2 = jnp.where(lanef == e1, -jnp.inf, lg)
    m2 = jnp.max(lg2, axis=-1, keepdims=True)
    e2 = jnp.min(jnp.where(lg2 == m2, lanef, float(LANES)), axis=-1, keepdims=True)
    ex = jnp.exp(m2 - m1)
    g1 = 1.0 / (1.0 + ex)
    g2 = ex / (1.0 + ex)
    onehot = jnp.where((lanef == e1) | (lanef == e2), 1.0, 0.0)
    row = lax.broadcasted_iota(jnp.int32, (tm, tm), 0)
    col = lax.broadcasted_iota(jnp.int32, (tm, tm), 1)
    before = jnp.dot(jnp.where(col < row, 1.0, 0.0).astype(BF16), onehot.astype(BF16),
                     preferred_element_type=F32) + carry_ref[...]
    r1 = jnp.sum(jnp.where(lanef == e1, before, 0.0), axis=-1, keepdims=True)
    r2 = jnp.sum(jnp.where(lanef == e2, before, 0.0), axis=-1, keepdims=True)
    carry_ref[...] += jnp.sum(onehot, axis=0, keepdims=True)
    meta = jnp.zeros((tm, LANES), F32)
    for pos, val in ((META_E0, e1), (META_E1, e2), (META_G0, g1), (META_G1, g2), (META_R0, r1), (META_R1, r2)):
        meta = jnp.where(lane == pos, val, meta)
    meta_ref[...] = meta
    tab_ref[...] = meta.T[0:SUBLANES, :]
    cnt_ref[...] = jnp.broadcast_to(carry_ref[...], cnt_ref.shape)


def route(logits, *, tm):
    t = logits.shape[0]
    return pl.pallas_call(
        functools.partial(_route_body, tm=tm, ne=N_EXPERTS),
        grid=(t // tm,),
        in_specs=[pl.BlockSpec((tm, LANES), lambda m: (m, 0))],
        out_specs=[pl.BlockSpec((tm, LANES), lambda m: (m, 0)),
                   pl.BlockSpec((SUBLANES, tm), lambda m: (0, m)),
                   pl.BlockSpec((SUBLANES, LANES), lambda m: (0, 0))],
        out_shape=[jax.ShapeDtypeStruct((t, LANES), F32), jax.ShapeDtypeStruct((SUBLANES, t), F32),
                   jax.ShapeDtypeStruct((SUBLANES, LANES), F32)],
        scratch_shapes=[pltpu.VMEM((1, LANES), F32)],
        compiler_params=_params("arbitrary"),
        name="route",
    )(logits)


def _dispatch_body(dest_ref, pe_ref, na_ref, hf_ref, xs_ref, zero_ref, sem, zsem, *, tm, topk, tm_rows, n_tok):
    base = pl.program_id(0) * tm

    @pl.when(pl.program_id(0) == 0)
    def _():
        zero_ref[...] = jnp.zeros_like(zero_ref)
        n_tiles = xs_ref.shape[0] // tm_rows
        fills = []
        for e in range(N_EXPERTS):
            end = pe_ref[e]
            nonempty = end > (pe_ref[e - 1] if e else 0)
            fills.append((nonempty, pl.multiple_of(jnp.maximum(end - tm_rows, 0), tm_rows)))
        for tile in range(n_tiles):
            fills.append((tile >= na_ref[0], tile * tm_rows))
        for phase in ("start", "wait"):
            for cond, row0 in fills:
                @pl.when(cond)
                def _(row0=row0, phase=phase):
                    cp = pltpu.make_async_copy(zero_ref, xs_ref.at[pl.ds(row0, tm_rows)], zsem)
                    cp.start() if phase == "start" else cp.wait()

    def issue(r, c):
        for k in range(topk):
            d = dest_ref[k * n_tok + base + r]
            pltpu.make_async_copy(hf_ref.at[pl.ds(r, 1)], xs_ref.at[pl.ds(d, 1)], sem).start(priority=k % 2)
        return c

    lax.fori_loop(0, tm, issue, 0, unroll=ROW_DMA_UNROLL)
    for k in range(topk):
        pltpu.make_async_copy(hf_ref, xs_ref.at[pl.ds(0, tm)], sem).wait()


def dispatch(dest, pad_end, n_active, hf, *, n_rows, tm, topk, tm_rows):
    t, d = hf.shape
    return pl.pallas_call(
        functools.partial(_dispatch_body, tm=tm, topk=topk, tm_rows=tm_rows, n_tok=t),
        grid_spec=pltpu.PrefetchScalarGridSpec(
            num_scalar_prefetch=3,
            grid=(t // tm,),
            in_specs=[pl.BlockSpec((tm, d), lambda m, *_: (m, 0))],
            out_specs=pl.BlockSpec(memory_space=pl.ANY),
            scratch_shapes=[pltpu.VMEM((tm_rows, d), F32), pltpu.SemaphoreType.DMA, pltpu.SemaphoreType.DMA],
        ),
        out_shape=jax.ShapeDtypeStruct((n_rows, d), F32),
        compiler_params=_params("arbitrary"),
        name="moe_dispatch",
    )(dest, pad_end, n_active, hf)


def _experts_body(te_ref, na_ref, xs_ref, w1_ref, w3_ref, w2_ref, y_ref, xb_ref, g_ref, *, fc):
    del te_ref
    f = pl.program_id(1)

    @pl.when(pl.program_id(0) >= na_ref[0])
    def _():
        y_ref[...] = jnp.zeros_like(y_ref)

    @pl.when(pl.program_id(0) < na_ref[0])
    def _():
        @pl.when(f == 0)
        def _():
            xb_ref[...] = xs_ref[...].astype(BF16)

        y = _swiglu(xb_ref[...], w1_ref.at[0], w3_ref.at[0], w2_ref.at[0], g_ref, fc)

        @pl.when(f == 0)
        def _():
            y_ref[...] = y

        @pl.when(f > 0)
        def _():
            y_ref[...] += y


def experts(tile_expert, n_active, xs, w1, w3, w2, *, tm, tf, fc):
    n_rows, d = xs.shape
    dff = w1.shape[2]
    row_tile = lambda i, f, te, na: (jnp.minimum(i, na[0] - 1), 0)
    ftile = lambda i, f, na: jnp.where(i < na[0], f, dff // tf - 1)
    return pl.pallas_call(
        functools.partial(_experts_body, fc=fc),
        grid_spec=pltpu.PrefetchScalarGridSpec(
            num_scalar_prefetch=2,
            grid=(n_rows // tm, dff // tf),
            in_specs=[
                pl.BlockSpec((tm, d), row_tile),
                pl.BlockSpec((1, d, tf), lambda i, f, te, na: (te[i], 0, ftile(i, f, na))),
                pl.BlockSpec((1, d, tf), lambda i, f, te, na: (te[i], 0, ftile(i, f, na))),
                pl.BlockSpec((1, tf, d), lambda i, f, te, na: (te[i], ftile(i, f, na), 0)),
            ],
            out_specs=pl.BlockSpec((tm, d), lambda i, f, te, na: (i, 0)),
            scratch_shapes=[pltpu.VMEM((tm, d), BF16), pltpu.VMEM((tm, tf), BF16)],
        ),
        out_shape=jax.ShapeDtypeStruct((n_rows, d), F32),
        compiler_params=_params("arbitrary", "arbitrary"),
        name="moe_experts",
    )(tile_expert, n_active, xs, w1, w3, w2)


def _combine_body(dest_ref, x1_ref, meta_ref, p_ref, g_ref, wg_ref, wp_ref, y_ref, o_ref, buf_ref, sem,
                  *, tm, topk):
    base = pl.program_id(0) * tm
    n_tok = pl.num_programs(0) * tm

    def issue(r, c):
        for k in range(topk):
            d = dest_ref[k * n_tok + base + r]
            pltpu.make_async_copy(y_ref.at[pl.ds(d, 1)], buf_ref.at[k, pl.ds(r, 1)], sem).start(priority=k % 2)
        return c

    lax.fori_loop(0, tm, issue, 0, unroll=ROW_DMA_UNROLL)
    for k in range(topk):
        pltpu.make_async_copy(y_ref.at[pl.ds(0, tm)], buf_ref.at[k], sem).wait()
    meta = meta_ref[...]
    g0 = meta[:, META_G0:META_G0 + 1]
    g1 = meta[:, META_G1:META_G1 + 1]
    x2 = x1_ref[...] + (g0 * buf_ref[0] + g1 * buf_ref[1])
    o_ref[...] = _ple(x2, p_ref, g_ref, wg_ref, wp_ref)


def combine(dest, x1, meta, y, p2, p_row0, g, wg, wp, *, tm, topk):
    t, d = x1.shape
    full = lambda m, dest: (0, 0)
    p_blk0 = p_row0 // tm
    return pl.pallas_call(
        functools.partial(_combine_body, tm=tm, topk=topk),
        grid_spec=pltpu.PrefetchScalarGridSpec(
            num_scalar_prefetch=1,
            grid=(t // tm,),
            in_specs=[pl.BlockSpec((tm, d), lambda m, dest: (m, 0)),
                      pl.BlockSpec((tm, LANES), lambda m, dest: (m, 0)),
                      pl.BlockSpec((tm, p2.shape[1]), lambda m, dest: (p_blk0 + m, 0)),
                      pl.BlockSpec((1, d), full), pl.BlockSpec(wg.shape, full), pl.BlockSpec(wp.shape, full),
                      pl.BlockSpec(memory_space=pl.ANY)],
            out_specs=pl.BlockSpec((tm, d), lambda m, dest: (m, 0)),
            scratch_shapes=[pltpu.VMEM((topk, tm, d), F32), pltpu.SemaphoreType.DMA],
        ),
        out_shape=jax.ShapeDtypeStruct((t, d), F32),
        compiler_params=_params("arbitrary"),
        name="moe_combine",
    )(dest, x1, meta, p2, g, wg, wp, y)


def moe_ffn(hf, x1, logits, w1, w3, w2, ple_args, *, tm_route, tm_rows, tf, tm_move):
    t, d = x1.shape
    topk = 2
    meta, tab, cnt = route(logits, tm=tm_route)
    counts = cnt[0, :N_EXPERTS].astype(jnp.int32)
    padded = ((counts + tm_rows - 1) // tm_rows) * tm_rows
    pad_end = jnp.cumsum(padded).astype(jnp.int32)
    pad_start = pad_end - padded
    eidx = tab[META_E0:META_E1 + 1].astype(jnp.int32)
    rank = tab[META_R0:META_R1 + 1].astype(jnp.int32)
    dest = rank
    for e in range(N_EXPERTS):
        dest = dest + jnp.where(eidx == e, pad_start[e], 0)
    dest = dest.reshape(topk * t)
    n_tiles = -(-(t * topk) // tm_rows) + N_EXPERTS
    tile_start = jnp.arange(n_tiles, dtype=jnp.int32) * tm_rows
    tile_expert = jnp.minimum(jnp.sum(tile_start[:, None] >= pad_end[None, :], axis=1),
                              N_EXPERTS - 1).astype(jnp.int32)
    n_active = pad_end[N_EXPERTS - 1:] // tm_rows
    xs = dispatch(dest, pad_end, n_active, hf, n_rows=n_tiles * tm_rows, tm=tm_move, topk=topk, tm_rows=tm_rows)
    y = experts(tile_expert, n_active, xs, w1, w3, w2, tm=tm_rows, tf=tf, fc=256)
    return combine(dest, x1, meta, y, *ple_args, tm=tm_move, topk=topk)


def _tile2(g):
    return jnp.concatenate([g, g]).reshape(1, 2 * g.shape[0])


def kernel(x, p, g_mix, w_in, g_q, g_k, conv_w, conv_b, b_i, b_f, g_h, w_oa, w_ob, w_out, g_ffn, w_d1, w_d3,
           w_d2, w_router, w_e1, w_e3, w_e2, g_ple, w_ple_gate, w_ple_proj):
    batch, seq, d = x.shape
    depth = w_in.shape[0]
    t = batch * seq
    nh = MLSTM_HEADS
    x2 = x.reshape(t, d)
    c_q, c_k, c_v = 0, 512, 1024
    c_qk, c_vm, c_om, c_i, c_f, c_ga, c_gb, c_end = 1536, 2560, 3072, 3584, 3588, 3592, 4616, 5640

    for l in range(depth):
        w = w_in[l]
        w_bf = w.astype(BF16)
        w_gates = w_bf[:, c_ga:c_end]
        proj, gif = in_proj(x2, g_mix[l].reshape(1, d), w_gates, w_bf, nb_cols=c_i, if_col=c_i,
                            tm=2048, tn=512)

        ya = moba(proj, _tile2(g_q[l]), _tile2(g_k[l]), batch=batch, seq=seq)

        bias = jnp.concatenate([b_i[l], b_f[l]])
        bias_row = jnp.pad(bias, (0, LANES - 2 * nh)).reshape(1, LANES)
        bias_col = bias.reshape(2 * nh, 1)
        gates_row = gif[:, :2 * nh].reshape(batch, seq, 2 * nh).transpose(0, 2, 1)
        yb = mlstm(proj, gif, gates_row, bias_row, bias_col, conv_w[l], conv_b[l].reshape(1, -1),
                   g_h[l].reshape(1, -1), batch=batch, seq=seq, chunk=256)

        j = l // 2
        moe = l % 2 == 1
        wr = jnp.pad(w_router[j], ((0, 0), (0, LANES - N_EXPERTS))) if moe else None
        outs = merge(ya, yb, proj, x2, w_oa[l].astype(BF16), w_ob[l].astype(BF16), w_out[l].astype(BF16),
                     g_ffn[l].reshape(1, d), wr, tm=512)
        ple_args = (p.reshape(depth * t, -1), l * t, g_ple[l].reshape(1, d), w_ple_gate[l].astype(BF16),
                    w_ple_proj[l].astype(BF16))
        if moe:
            x1, hf, logits = outs
            x2 = moe_ffn(hf, x1, logits, w_e1[j].astype(BF16), w_e3[j].astype(BF16), w_e2[j].astype(BF16),
                         ple_args, tm_route=512, tm_rows=512, tf=1792, tm_move=512)
        else:
            x1, hf = outs
            x2 = dense_ffn(hf, x1, w_d1[j].astype(BF16), w_d3[j].astype(BF16), w_d2[j].astype(BF16),
                           *ple_args, tm=1024, fc=256)
    return x2.reshape(batch, seq, d)
```

```python
import functools

import jax
import jax.numpy as jnp
from jax import lax
from jax.experimental import pallas as pl
from jax.experimental.pallas import tpu as pltpu

F32 = jnp.float32
BF16 = jnp.bfloat16
HIGHEST = lax.Precision.HIGHEST

RMS_EPS = 1e-6
LANES = 128
SUBLANES = 8

MOBA_HEADS = 8
MOBA_HEAD_DIM = 64
MOBA_BLOCK = 256
MOBA_TOPK = 3
MLSTM_HEADS = 4
MLSTM_DIM = 128
CONV_WIDTH = 4
N_EXPERTS = 8

COL_GA, COL_GB = 0, 8
COL_QA, COL_KA, COL_VA = 16, 20, 24
COL_QM, COL_KM, COL_VM, COL_OM = 28, 32, 36, 40
N_PROJ = 44 * LANES

VMEM_LIMIT = 56 * 1024 * 1024
ROW_DMA_UNROLL = 8


def _params(*sem):
    return pltpu.CompilerParams(dimension_semantics=sem, vmem_limit_bytes=VMEM_LIMIT)


def _sigmoid(x):
    return 1.0 / (1.0 + jnp.exp(-x))


def _rms(x, g):
    return x * lax.rsqrt(jnp.mean(x * x, axis=-1, keepdims=True) + RMS_EPS) * g


def _split_bf16(x):
    hi = x.astype(BF16)
    return hi, (x - hi.astype(F32)).astype(BF16)


def _dot_split(a, b):
    ah, al = _split_bf16(a)
    bh, bl = _split_bf16(b)
    return (jnp.dot(ah, bh, preferred_element_type=F32) + jnp.dot(ah, bl, preferred_element_type=F32)
            + jnp.dot(al, bh, preferred_element_type=F32))


def _nt_dot(a, b, **kw):
    return lax.dot_general(a, b, (((1,), (1,)), ((), ())), preferred_element_type=F32, **kw)


def _in_proj_body(x_ref, g_ref, wa_ref, wb_ref, o_ref, oif_ref, h_ref, *, nb_cols, if_col, tn):
    h_ref[...] = _rms(x_ref[...], g_ref[...]).astype(BF16)
    oif_ref[...] = jnp.dot(h_ref[...], wb_ref[:, if_col:if_col + LANES], preferred_element_type=F32)
    na = wa_ref.shape[1]
    for c0 in range(0, na + nb_cols, tn):
        w = wa_ref[:, c0:c0 + tn] if c0 < na else wb_ref[:, c0 - na:c0 - na + tn]
        o_ref[:, c0:c0 + tn] = jnp.dot(h_ref[...], w, preferred_element_type=F32).astype(o_ref.dtype)


def in_proj(x2, g, wa, w_full, *, nb_cols, if_col, tm, tn):
    t, d = x2.shape
    n = wa.shape[1] + nb_cols
    resident = dict(pipeline_mode=pl.Buffered(1))
    return pl.pallas_call(
        functools.partial(_in_proj_body, nb_cols=nb_cols, if_col=if_col, tn=tn),
        grid=(t // tm,),
        in_specs=[
            pl.BlockSpec((tm, d), lambda m: (m, 0)),
            pl.BlockSpec((1, d), lambda m: (0, 0)),
            pl.BlockSpec(wa.shape, lambda m: (0, 0), **resident),
            pl.BlockSpec(w_full.shape, lambda m: (0, 0), **resident),
        ],
        out_specs=[
            pl.BlockSpec((tm, n), lambda m: (m, 0)),
            pl.BlockSpec((tm, LANES), lambda m: (m, 0)),
        ],
        out_shape=[jax.ShapeDtypeStruct((t, n), BF16), jax.ShapeDtypeStruct((t, LANES), F32)],
        scratch_shapes=[pltpu.VMEM((tm, d), BF16)],
        compiler_params=_params("arbitrary"),
        name="in_proj",
    )(x2, g, wa, w_full)


MASK_BIAS = -1e30
LOG2_E = 1.4426950408889634


def _moba_body(q_ref, k_ref, v_ref, gq_ref, gk_ref, o_ref,
               kn_ref, vt_ref, kmean_ref, qaug_ref, s_ref, m_ref, alpha_ref, acc_ref,
               *, nb, blk, dh, topk, nheads):
    i = pl.program_id(1)
    pair = 2 * blk
    lane = lax.broadcasted_iota(jnp.int32, (1, LANES), 1)
    head0 = lane < dh

    same_head = (lax.broadcasted_iota(jnp.int32, (LANES, LANES), 0) // dh
                 == lax.broadcasted_iota(jnp.int32, (LANES, LANES), 1) // dh)
    head_ones = jnp.where(same_head, 1.0, 0.0).astype(BF16)

    def head_rms(x, g, on_mxu):
        x2 = x * x
        if on_mxu:
            hi, lo = _split_bf16(x2)
            ss = (jnp.dot(hi, head_ones, preferred_element_type=F32)
                  + jnp.dot(lo, head_ones, preferred_element_type=F32))
        else:
            s0 = jnp.sum(jnp.where(head0, x2, 0.0), axis=-1, keepdims=True)
            s1 = jnp.sum(jnp.where(head0, 0.0, x2), axis=-1, keepdims=True)
            ss = jnp.where(head0, s0, s1)
        return x * lax.rsqrt(ss * (1.0 / dh) + RMS_EPS) * g

    @pl.when(i == 0)
    def _():
        def prep(j, c):
            r0 = pl.multiple_of(j * blk, blk)
            onehot = jnp.where(lane == dh + j, 1.0, 0.0)
            for p in range(nheads // 2):
                cols = slice(p * LANES, (p + 1) * LANES)
                kn = head_rms(k_ref[pl.ds(r0, blk), cols].astype(F32), gk_ref[...], True)
                for hh, kh in ((0, kn), (1, pltpu.roll(kn, dh, axis=1))):
                    h = 2 * p + hh
                    kmean_ref[h, pl.ds(j, 1), :] = jnp.mean(jnp.where(head0, kh, 0.0), axis=0, keepdims=True)
                    kn_ref[h, pl.ds(r0, blk), :] = jnp.where(head0, kh, onehot).astype(BF16)
                v_t = v_ref[pl.ds(r0, blk), cols].astype(F32).T.astype(BF16)
                for hh in range(2):
                    vt_ref[2 * p + hh, 0:dh, pl.ds(r0, blk)] = v_t[hh * dh:(hh + 1) * dh, :]
                    vt_ref[2 * p + hh, dh:, pl.ds(r0, blk)] = jnp.ones((vt_ref.shape[1] - dh, blk), BF16)
            return c

        lax.fori_loop(0, nb, prep, 0)

    jidx = lax.broadcasted_iota(jnp.int32, (nb, blk), 0)
    key_i = lax.broadcasted_iota(jnp.int32, (blk, blk), 0)
    qry_i = lax.broadcasted_iota(jnp.int32, (blk, blk), 1)
    causal = key_i <= qry_i
    r_own = pl.multiple_of(i * blk, blk)
    qk_scale = dh ** -0.5 * LOG2_E
    for p in range(nheads // 2):
        cols = slice(p * LANES, (p + 1) * LANES)
        qn_t = head_rms(q_ref[:, cols].astype(F32), gq_ref[...], False).T
        for hh in range(2):
            h = 2 * p + hh
            q_t = qn_t[hh * dh:(hh + 1) * dh, :]
            gate = jnp.dot(kmean_ref[h], jnp.concatenate([q_t, jnp.zeros((LANES - dh, blk), F32)], axis=0),
                           precision=HIGHEST, preferred_element_type=F32)
            rank = jnp.zeros((nb, blk), F32)
            for jp in range(nb):
                row = gate[jp:jp + 1, :]
                beats = (row > gate) | ((row == gate) & (jidx > jp))
                rank = rank + jnp.where(beats, jnp.where(jp < i, 1.0, 0.0), 0.0)
            sel = (rank < topk) & (jidx < i)
            q_s = q_t * qk_scale
            pad = jnp.zeros((LANES - dh - nb, blk), F32)
            qaug_ref[h] = jnp.concatenate([q_s, jnp.where(sel, 0.0, MASK_BIAS), pad], axis=0).astype(BF16)
            qaug_own = jnp.concatenate([q_s, jnp.where(jidx == i, 0.0, MASK_BIAS), pad], axis=0).astype(BF16)
            st = jnp.dot(kn_ref[h, pl.ds(r_own, blk), :], qaug_own, preferred_element_type=F32)
            st = jnp.where(causal, st, -jnp.inf)
            s_ref[h, 0:blk, :] = st
            m_ref[h] = jnp.max(st, axis=0, keepdims=True)

    def finish_own(h):
        pr = jnp.exp2(s_ref[h, 0:blk, :] - m_ref[h]).astype(BF16)
        acc_ref[h] = jnp.dot(vt_ref[h, :, pl.ds(r_own, blk)], pr, preferred_element_type=F32)

    def score_pair(u, h):
        r0 = pl.multiple_of(u * pair, pair)
        st = jnp.dot(kn_ref[h, pl.ds(r0, pair), :], qaug_ref[h], preferred_element_type=F32)
        m_old = m_ref[h]
        m_new = jnp.maximum(m_old, jnp.max(st, axis=0, keepdims=True))
        s_ref[h] = st
        alpha_ref[h] = jnp.exp2(m_old - m_new)
        m_ref[h] = m_new

    def finish_pair(u, h):
        r0 = pl.multiple_of(u * pair, pair)
        pr = jnp.exp2(s_ref[h] - m_ref[h]).astype(BF16)
        acc_ref[h] = alpha_ref[h] * acc_ref[h] + jnp.dot(vt_ref[h, :, pl.ds(r0, pair)], pr,
                                                         preferred_element_type=F32)

    n_pairs = jnp.maximum((i + 1) // 2, 1)
    for h in range(nheads):
        finish_own(h)
        score_pair(0, h)

    def body(u, c):
        for h in range(nheads):
            finish_pair(u - 1, h)
            score_pair(u, h)
        return c

    lax.fori_loop(1, n_pairs, body, 0)
    for h in range(nheads):
        finish_pair(n_pairs - 1, h)

    for p in range(nheads // 2):
        a0 = acc_ref[2 * p]
        a1 = acc_ref[2 * p + 1]
        ot = jnp.concatenate([a0[0:dh] / a0[dh:dh + 1], a1[0:dh] / a1[dh:dh + 1]], axis=0)
        o_ref[:, p * LANES:(p + 1) * LANES] = ot.T.astype(o_ref.dtype)


def moba(proj, gq2, gk2, *, batch, seq):
    nb = seq // MOBA_BLOCK
    blk = MOBA_BLOCK
    dh = MOBA_HEAD_DIM
    nheads = MOBA_HEADS
    width = nheads * dh
    wb = width // LANES
    assert dh + nb <= LANES and 2 * dh == LANES and nb % 2 == 0
    v_rows = dh + 2 * SUBLANES
    body = functools.partial(_moba_body, nb=nb, blk=blk, dh=dh, topk=MOBA_TOPK, nheads=nheads)
    return pl.pallas_call(
        body,
        grid=(batch, nb),
        in_specs=[
            pl.BlockSpec((blk, width), lambda b, i: (b * nb + i, COL_QA // wb)),
            pl.BlockSpec((seq, width), lambda b, i: (b, COL_KA // wb)),
            pl.BlockSpec((seq, width), lambda b, i: (b, COL_VA // wb)),
            pl.BlockSpec((1, LANES), lambda b, i: (0, 0)),
            pl.BlockSpec((1, LANES), lambda b, i: (0, 0)),
        ],
        out_specs=pl.BlockSpec((blk, width), lambda b, i: (b * nb + i, 0)),
        out_shape=jax.ShapeDtypeStruct((batch * seq, width), BF16),
        scratch_shapes=[
            pltpu.VMEM((nheads, seq, LANES), BF16),
            pltpu.VMEM((nheads, v_rows, seq), BF16),
            pltpu.VMEM((nheads, nb, LANES), F32),
            pltpu.VMEM((nheads, LANES, blk), BF16),
            pltpu.VMEM((nheads, 2 * blk, blk), F32),
            pltpu.VMEM((nheads, 1, blk), F32),
            pltpu.VMEM((nheads, 1, blk), F32),
            pltpu.VMEM((nheads, v_rows, blk), F32),
        ],
        compiler_params=_params("arbitrary", "arbitrary"),
        name="moba",
    )(proj, proj, proj, gq2, gk2)


def _log_sigmoid(x):
    return jnp.minimum(x, 0.0) - jnp.log(1.0 + jnp.exp(-jnp.abs(x)))


def _dot_tri(tri, x, tri_left):
    out = None
    for _ in range(3):
        piece = x.astype(BF16)
        x = x - piece.astype(F32)
        term = (jnp.dot(tri, piece, preferred_element_type=F32) if tri_left
                else jnp.dot(piece, tri, preferred_element_type=F32))
        out = term if out is None else out + term
    return out


def _mlstm_body(qr_ref, kr_ref, v_ref, og_ref, gcol_ref, grow_ref, brow_ref, bcol_ref,
                cwq_ref, cwk_ref, cbq_ref, cbk_ref, gh_ref, o_ref,
                qx_ref, kx_ref, c_ref, m_ref, *, chunk, dk, nh):
    L = chunk
    width = nh * dk

    @pl.when(pl.program_id(1) == 0)
    def _():
        qx_ref[0:SUBLANES, :] = jnp.zeros((SUBLANES, width), F32)
        kx_ref[0:SUBLANES, :] = jnp.zeros((SUBLANES, width), F32)
        c_ref[...] = jnp.zeros_like(c_ref)
        m_ref[...] = jnp.zeros_like(m_ref)

    qx_ref[SUBLANES:SUBLANES + L, :] = qr_ref[...].astype(F32)
    kx_ref[SUBLANES:SUBLANES + L, :] = kr_ref[...].astype(F32)

    def conv_silu(x_ref, w_ref, b_ref):
        acc = b_ref[...] + w_ref[0:1, :] * x_ref[pl.ds(SUBLANES - CONV_WIDTH + 1, L), :]
        for j in range(1, CONV_WIDTH):
            acc = acc + w_ref[j:j + 1, :] * x_ref[pl.ds(SUBLANES - CONV_WIDTH + 1 + j, L), :]
        return acc * _sigmoid(acc)

    q_all = conv_silu(qx_ref, cwq_ref, cbq_ref)
    k_all = conv_silu(kx_ref, cwk_ref, cbk_ref) * (dk ** -0.5)
    qx_ref[0:SUBLANES, :] = qx_ref[L:L + SUBLANES, :]
    kx_ref[0:SUBLANES, :] = kx_ref[L:L + SUBLANES, :]

    pre_col = gcol_ref[...] + brow_ref[...]
    pre_row = grow_ref[0] + bcol_ref[...]
    t_i = lax.broadcasted_iota(jnp.int32, (L, L), 0)
    s_i = lax.broadcasted_iota(jnp.int32, (L, L), 1)
    tril = s_i <= t_i
    bcum_cols = _dot_tri(jnp.where(tril, 1.0, 0.0).astype(BF16), _log_sigmoid(pre_col), True)
    bcum_rows = _dot_tri(jnp.where(t_i <= s_i, 1.0, 0.0).astype(BF16), _log_sigmoid(pre_row), False)
    ones = jnp.ones((L, dk), BF16)

    for h in range(nh):
        cols = slice(h * dk, (h + 1) * dk)
        q = q_all[:, cols]
        k = k_all[:, cols]
        i_col = pre_col[:, h:h + 1]
        i_row = pre_row[h:h + 1, :]
        bcum_col = bcum_cols[:, nh + h:nh + h + 1]
        bcum_row = bcum_rows[nh + h:nh + h + 1, :]

        m_prev = m_ref[h, 0:1, 0:1]
        a_col = bcum_col + m_prev
        dmat = jnp.where(tril, bcum_col - bcum_row + i_row, -jnp.inf)
        m_t = jnp.maximum(a_col, jnp.max(dmat, axis=-1, keepdims=True))
        dw = jnp.exp(dmat - m_t)
        aw = jnp.exp(a_col - m_t)

        qb = q.astype(BF16)
        kb = k.astype(BF16)
        v_aug = jnp.concatenate([v_ref[:, cols], ones], axis=-1)
        sqk = _nt_dot(qb, kb) * dw
        num_aug = (aw * jnp.dot(qb, c_ref[h].astype(BF16), preferred_element_type=F32)
                   + jnp.dot(sqk.astype(BF16), v_aug, preferred_element_type=F32))
        den = num_aug[:, dk:dk + 1]
        hc = num_aug[:, 0:dk] / jnp.maximum(jnp.abs(den), jnp.exp(-m_t))

        b_last = bcum_col[L - 1:L, :]
        g_col = b_last - bcum_col + i_col
        m_new = jnp.maximum(b_last + m_prev, jnp.max(g_col, axis=0, keepdims=True))
        w_c = jnp.exp(b_last + m_prev - m_new)
        kw_t = (k * jnp.exp(g_col - m_new)).T.astype(BF16)
        c_ref[h] = w_c * c_ref[h] + jnp.dot(kw_t, v_aug, preferred_element_type=F32)
        m_ref[h] = jnp.broadcast_to(m_new, (1, LANES))

        o_ref[:, cols] = (_rms(hc, gh_ref[...]) * _sigmoid(og_ref[:, cols].astype(F32))).astype(o_ref.dtype)


def mlstm(proj, gates_col, gates_row, bias_row, bias_col, conv_w, conv_b, gh, *, batch, seq, chunk):
    nh = MLSTM_HEADS
    dk = MLSTM_DIM
    width = nh * dk
    wb = width // LANES
    nc = seq // chunk
    body = functools.partial(_mlstm_body, chunk=chunk, dk=dk, nh=nh)

    def rows(col0):
        return pl.BlockSpec((chunk, width), lambda b, c: (b * nc + c, col0 // wb))

    return pl.pallas_call(
        body,
        grid=(batch, nc),
        in_specs=[
            rows(COL_QM), rows(COL_KM), rows(COL_VM), rows(COL_OM),
            pl.BlockSpec((chunk, LANES), lambda b, c: (b * nc + c, 0)),
            pl.BlockSpec((1, SUBLANES, chunk), lambda b, c: (b, 0, c)),
            pl.BlockSpec((1, LANES), lambda b, c: (0, 0)),
            pl.BlockSpec((SUBLANES, 1), lambda b, c: (0, 0)),
            pl.BlockSpec((CONV_WIDTH, width), lambda b, c: (0, 0)),
            pl.BlockSpec((CONV_WIDTH, width), lambda b, c: (0, 1)),
            pl.BlockSpec((1, width), lambda b, c: (0, 0)),
            pl.BlockSpec((1, width), lambda b, c: (0, 1)),
            pl.BlockSpec((1, LANES), lambda b, c: (0, 0)),
        ],
        out_specs=pl.BlockSpec((chunk, width), lambda b, c: (b * nc + c, 0)),
        out_shape=jax.ShapeDtypeStruct((batch * seq, width), BF16),
        scratch_shapes=[
            pltpu.VMEM((chunk + 2 * SUBLANES, width), F32),
            pltpu.VMEM((chunk + 2 * SUBLANES, width), F32),
            pltpu.VMEM((nh, dk, 2 * dk), F32),
            pltpu.VMEM((nh, 1, LANES), F32),
        ],
        compiler_params=_params("arbitrary", "arbitrary"),
        name="mlstm",
    )(proj, proj, proj, proj, gates_col, gates_row, bias_row, bias_col,
      conv_w, conv_w, conv_b, conv_b, gh)


def _merge_body(ya_ref, yb_ref, ga_ref, gb_ref, x_ref, woa_ref, wob_ref, wout_ref, gffn_ref, *rest, moe):
    a = jnp.dot(ya_ref[...], woa_ref[...], preferred_element_type=F32)
    b = jnp.dot(yb_ref[...], wob_ref[...], preferred_element_type=F32)
    mixed = _sigmoid(ga_ref[...].astype(F32)) * a + _sigmoid(gb_ref[...].astype(F32)) * b
    x1 = x_ref[...] + jnp.dot(mixed.astype(BF16), wout_ref[...], preferred_element_type=F32)
    hf = _rms(x1, gffn_ref[...])
    if moe:
        wr_ref, x1_ref, hf_ref, lg_ref = rest
        hf_ref[...] = hf
        lg_ref[...] = _dot_split(hf, wr_ref[...])
    else:
        x1_ref, hf_ref = rest
        hf_ref[...] = hf.astype(BF16)
    x1_ref[...] = x1


def merge(ya, yb, proj, x2, woa, wob, wout, gffn, wr, *, tm):
    t, d = x2.shape
    moe = wr is not None
    full = lambda m: (0, 0)
    in_specs = [
        pl.BlockSpec((tm, ya.shape[1]), lambda m: (m, 0)),
        pl.BlockSpec((tm, yb.shape[1]), lambda m: (m, 0)),
        pl.BlockSpec((tm, d), lambda m: (m, COL_GA * LANES // d)),
        pl.BlockSpec((tm, d), lambda m: (m, COL_GB * LANES // d)),
        pl.BlockSpec((tm, d), lambda m: (m, 0)),
        pl.BlockSpec(woa.shape, full), pl.BlockSpec(wob.shape, full), pl.BlockSpec(wout.shape, full),
        pl.BlockSpec((1, d), full),
    ]
    args = [ya, yb, proj, proj, x2, woa, wob, wout, gffn]
    out_specs = [pl.BlockSpec((tm, d), lambda m: (m, 0)), pl.BlockSpec((tm, d), lambda m: (m, 0))]
    out_shape = [jax.ShapeDtypeStruct((t, d), F32), jax.ShapeDtypeStruct((t, d), F32 if moe else BF16)]
    if moe:
        in_specs.append(pl.BlockSpec(wr.shape, full))
        args.append(wr)
        out_specs.append(pl.BlockSpec((tm, LANES), lambda m: (m, 0)))
        out_shape.append(jax.ShapeDtypeStruct((t, LANES), F32))
    return pl.pallas_call(
        functools.partial(_merge_body, moe=moe),
        grid=(t // tm,),
        in_specs=in_specs, out_specs=out_specs, out_shape=out_shape,
        compiler_params=_params("arbitrary"),
        name="merge_moe" if moe else "merge",
    )(*args)


def _swiglu(x, w1_ref, w3_ref, w2_ref, g_ref, fc):
    dff = g_ref.shape[1]
    for f0 in range(0, dff, fc):
        a = jnp.dot(x, w1_ref[:, f0:f0 + fc], preferred_element_type=F32)
        b = jnp.dot(x, w3_ref[:, f0:f0 + fc], preferred_element_type=F32)
        g_ref[:, f0:f0 + fc] = (a * _sigmoid(a) * b).astype(BF16)
    return jnp.dot(g_ref[...], w2_ref[...], preferred_element_type=F32)


def _ple(x, p_ref, g_ref, wg_ref, wp_ref):
    gate = _sigmoid(jnp.dot(_rms(x, g_ref[...]).astype(BF16), wg_ref[...], preferred_element_type=F32))
    emb = jnp.dot(p_ref[...].astype(BF16), wp_ref[...], preferred_element_type=F32)
    return x + gate * emb


def _ffn_body(hf_ref, x1_ref, w1_ref, w3_ref, w2_ref, p_ref, g_ref, wg_ref, wp_ref, o_ref, act_ref, *, fc):
    x2 = x1_ref[...] + _swiglu(hf_ref[...], w1_ref, w3_ref, w2_ref, act_ref, fc)
    o_ref[...] = _ple(x2, p_ref, g_ref, wg_ref, wp_ref)


def dense_ffn(hf, x1, w1, w3, w2, p2, p_row0, g, wg, wp, *, tm, fc):
    t, d = x1.shape
    dff = w1.shape[1]
    p_blk0 = p_row0 // tm
    resident = dict(pipeline_mode=pl.Buffered(1))
    full = lambda m: (0, 0)
    return pl.pallas_call(
        functools.partial(_ffn_body, fc=fc),
        grid=(t // tm,),
        in_specs=[
            pl.BlockSpec((tm, d), lambda m: (m, 0)),
            pl.BlockSpec((tm, d), lambda m: (m, 0)),
            pl.BlockSpec((d, dff), full, **resident),
            pl.BlockSpec((d, dff), full, **resident),
            pl.BlockSpec((dff, d), full, **resident),
            pl.BlockSpec((tm, p2.shape[1]), lambda m: (p_blk0 + m, 0)),
            pl.BlockSpec((1, d), full),
            pl.BlockSpec(wg.shape, full, **resident),
            pl.BlockSpec(wp.shape, full, **resident),
        ],
        out_specs=pl.BlockSpec((tm, d), lambda m: (m, 0)),
        out_shape=jax.ShapeDtypeStruct((t, d), F32),
        scratch_shapes=[pltpu.VMEM((tm, dff), BF16)],
        compiler_params=_params("arbitrary"),
        name="dense_ffn",
    )(hf, x1, w1, w3, w2, p2, g, wg, wp)


META_E0, META_E1, META_G0, META_G1, META_R0, META_R1 = 0, 1, 2, 3, 4, 5


def _route_body(lg_ref, meta_ref, tab_ref, cnt_ref, carry_ref, *, tm, ne):
    @pl.when(pl.program_id(0) == 0)
    def _():
        carry_ref[...] = jnp.zeros_like(carry_ref)

    lane = lax.broadcasted_iota(jnp.int32, (tm, LANES), 1)
    lanef = lane.astype(F32)
    lg = jnp.where(lane < ne, lg_ref[...], -jnp.inf)
    m1 = jnp.max(lg, axis=-1, keepdims=True)
    e1 = jnp.min(jnp.where(lg == m1, lanef, float(LANES)), axis=-1, keepdims=True)
    lg2 = jnp.where(lanef == e1, -jnp.inf, lg)
    m2 = jnp.max(lg2, axis=-1, keepdims=True)
    e2 = jnp.min(jnp.where(lg2 == m2, lanef, float(LANES)), axis=-1, keepdims=True)
    ex = jnp.exp(m2 - m1)
    g1 = 1.0 / (1.0 + ex)
    g2 = ex / (1.0 + ex)
    onehot = jnp.where((lanef == e1) | (lanef == e2), 1.0, 0.0)
    row = lax.broadcasted_iota(jnp.int32, (tm, tm), 0)
    col = lax.broadcasted_iota(jnp.int32, (tm, tm), 1)
    before = jnp.dot(jnp.where(col < row, 1.0, 0.0).astype(BF16), onehot.astype(BF16),
                     preferred_element_type=F32) + carry_ref[...]
    r1 = jnp.sum(jnp.where(lanef == e1, before, 0.0), axis=-1, keepdims=True)
    r2 = jnp.sum(jnp.where(lanef == e2, before, 0.0), axis=-1, keepdims=True)
    carry_ref[...] += jnp.sum(onehot, axis=0, keepdims=True)
    meta = jnp.zeros((tm, LANES), F32)
    for pos, val in ((META_E0, e1), (META_E1, e2), (META_G0, g1), (META_G1, g2), (META_R0, r1), (META_R1, r2)):
        meta = jnp.where(lane == pos, val, meta)
    meta_ref[...] = meta
    tab_ref[...] = meta.T[0:SUBLANES, :]
    cnt_ref[...] = jnp.broadcast_to(carry_ref[...], cnt_ref.shape)


def route(logits, *, tm):
    t = logits.shape[0]
    return pl.pallas_call(
        functools.partial(_route_body, tm=tm, ne=N_EXPERTS),
        grid=(t // tm,),
        in_specs=[pl.BlockSpec((tm, LANES), lambda m: (m, 0))],
        out_specs=[pl.BlockSpec((tm, LANES), lambda m: (m, 0)),
                   pl.BlockSpec((SUBLANES, tm), lambda m: (0, m)),
                   pl.BlockSpec((SUBLANES, LANES), lambda m: (0, 0))],
        out_shape=[jax.ShapeDtypeStruct((t, LANES), F32), jax.ShapeDtypeStruct((SUBLANES, t), F32),
                   jax.ShapeDtypeStruct((SUBLANES, LANES), F32)],
        scratch_shapes=[pltpu.VMEM((1, LANES), F32)],
        compiler_params=_params("arbitrary"),
        name="route",
    )(logits)


def _dispatch_body(dest_ref, pe_ref, na_ref, hf_ref, xs_ref, zero_ref, sem, zsem, *, tm, topk, tm_rows, n_tok):
    base = pl.program_id(0) * tm

    @pl.when(pl.program_id(0) == 0)
    def _():
        zero_ref[...] = jnp.zeros_like(zero_ref)
        n_tiles = xs_ref.shape[0] // tm_rows
        fills = []
        for e in range(N_EXPERTS):
            end = pe_ref[e]
            nonempty = end > (pe_ref[e - 1] if e else 0)
            fills.append((nonempty, pl.multiple_of(jnp.maximum(end - tm_rows, 0), tm_rows)))
        for tile in range(n_tiles):
            fills.append((tile >= na_ref[0], tile * tm_rows))
        for phase in ("start", "wait"):
            for cond, row0 in fills:
                @pl.when(cond)
                def _(row0=row0, phase=phase):
                    cp = pltpu.make_async_copy(zero_ref, xs_ref.at[pl.ds(row0, tm_rows)], zsem)
                    cp.start() if phase == "start" else cp.wait()

    def issue(r, c):
        for k in range(topk):
            d = dest_ref[k * n_tok + base + r]
            pltpu.make_async_copy(hf_ref.at[pl.ds(r, 1)], xs_ref.at[pl.ds(d, 1)], sem).start(priority=k % 2)
        return c

    lax.fori_loop(0, tm, issue, 0, unroll=ROW_DMA_UNROLL)
    for k in range(topk):
        pltpu.make_async_copy(hf_ref, xs_ref.at[pl.ds(0, tm)], sem).wait()


def dispatch(dest, pad_end, n_active, hf, *, n_rows, tm, topk, tm_rows):
    t, d = hf.shape
    return pl.pallas_call(
        functools.partial(_dispatch_body, tm=tm, topk=topk, tm_rows=tm_rows, n_tok=t),
        grid_spec=pltpu.PrefetchScalarGridSpec(
            num_scalar_prefetch=3,
            grid=(t // tm,),
            in_specs=[pl.BlockSpec((tm, d), lambda m, *_: (m, 0))],
            out_specs=pl.BlockSpec(memory_space=pl.ANY),
            scratch_shapes=[pltpu.VMEM((tm_rows, d), F32), pltpu.SemaphoreType.DMA, pltpu.SemaphoreType.DMA],
        ),
        out_shape=jax.ShapeDtypeStruct((n_rows, d), F32),
        compiler_params=_params("arbitrary"),
        name="moe_dispatch",
    )(dest, pad_end, n_active, hf)


def _experts_body(te_ref, na_ref, xs_ref, w1_ref, w3_ref, w2_ref, y_ref, xb_ref, g_ref, *, fc):
    del te_ref
    f = pl.program_id(1)

    @pl.when(pl.program_id(0) >= na_ref[0])
    def _():
        y_ref[...] = jnp.zeros_like(y_ref)

    @pl.when(pl.program_id(0) < na_ref[0])
    def _():
        @pl.when(f == 0)
        def _():
            xb_ref[...] = xs_ref[...].astype(BF16)

        y = _swiglu(xb_ref[...], w1_ref.at[0], w3_ref.at[0], w2_ref.at[0], g_ref, fc)

        @pl.when(f == 0)
        def _():
            y_ref[...] = y

        @pl.when(f > 0)
        def _():
            y_ref[...] += y


def experts(tile_expert, n_active, xs, w1, w3, w2, *, tm, tf, fc):
    n_rows, d = xs.shape
    dff = w1.shape[2]
    row_tile = lambda i, f, te, na: (jnp.minimum(i, na[0] - 1), 0)
    ftile = lambda i, f, na: jnp.where(i < na[0], f, dff // tf - 1)
    return pl.pallas_call(
        functools.partial(_experts_body, fc=fc),
        grid_spec=pltpu.PrefetchScalarGridSpec(
            num_scalar_prefetch=2,
            grid=(n_rows // tm, dff // tf),
            in_specs=[
                pl.BlockSpec((tm, d), row_tile),
                pl.BlockSpec((1, d, tf), lambda i, f, te, na: (te[i], 0, ftile(i, f, na))),
                pl.BlockSpec((1, d, tf), lambda i, f, te, na: (te[i], 0, ftile(i, f, na))),
                pl.BlockSpec((1, tf, d), lambda i, f, te, na: (te[i], ftile(i, f, na), 0)),
            ],
            out_specs=pl.BlockSpec((tm, d), lambda i, f, te, na: (i, 0)),
            scratch_shapes=[pltpu.VMEM((tm, d), BF16), pltpu.VMEM((tm, tf), BF16)],
        ),
        out_shape=jax.ShapeDtypeStruct((n_rows, d), F32),
        compiler_params=_params("arbitrary", "arbitrary"),
        name="moe_experts",
    )(tile_expert, n_active, xs, w1, w3, w2)


def _combine_body(dest_ref, x1_ref, meta_ref, p_ref, g_ref, wg_ref, wp_ref, y_ref, o_ref, buf_ref, sem,
                  *, tm, topk):
    base = pl.program_id(0) * tm
    n_tok = pl.num_programs(0) * tm

    def issue(r, c):
        for k in range(topk):
            d = dest_ref[k * n_tok + base + r]
            pltpu.make_async_copy(y_ref.at[pl.ds(d, 1)], buf_ref.at[k, pl.ds(r, 1)], sem).start(priority=k % 2)
        return c

    lax.fori_loop(0, tm, issue, 0, unroll=ROW_DMA_UNROLL)
    for k in range(topk):
        pltpu.make_async_copy(y_ref.at[pl.ds(0, tm)], buf_ref.at[k], sem).wait()
    meta = meta_ref[...]
    g0 = meta[:, META_G0:META_G0 + 1]
    g1 = meta[:, META_G1:META_G1 + 1]
    x2 = x1_ref[...] + (g0 * buf_ref[0] + g1 * buf_ref[1])
    o_ref[...] = _ple(x2, p_ref, g_ref, wg_ref, wp_ref)


def combine(dest, x1, meta, y, p2, p_row0, g, wg, wp, *, tm, topk):
    t, d = x1.shape
    full = lambda m, dest: (0, 0)
    p_blk0 = p_row0 // tm
    return pl.pallas_call(
        functools.partial(_combine_body, tm=tm, topk=topk),
        grid_spec=pltpu.PrefetchScalarGridSpec(
            num_scalar_prefetch=1,
            grid=(t // tm,),
            in_specs=[pl.BlockSpec((tm, d), lambda m, dest: (m, 0)),
                      pl.BlockSpec((tm, LANES), lambda m, dest: (m, 0)),
                      pl.BlockSpec((tm, p2.shape[1]), lambda m, dest: (p_blk0 + m, 0)),
                      pl.BlockSpec((1, d), full), pl.BlockSpec(wg.shape, full), pl.BlockSpec(wp.shape, full),
                      pl.BlockSpec(memory_space=pl.ANY)],
            out_specs=pl.BlockSpec((tm, d), lambda m, dest: (m, 0)),
            scratch_shapes=[pltpu.VMEM((topk, tm, d), F32), pltpu.SemaphoreType.DMA],
        ),
        out_shape=jax.ShapeDtypeStruct((t, d), F32),
        compiler_params=_params("arbitrary"),
        name="moe_combine",
    )(dest, x1, meta, p2, g, wg, wp, y)


def moe_ffn(hf, x1, logits, w1, w3, w2, ple_args, *, tm_route, tm_rows, tf, tm_move):
    t, d = x1.shape
    topk = 2
    meta, tab, cnt = route(logits, tm=tm_route)
    counts = cnt[0, :N_EXPERTS].astype(jnp.int32)
    padded = ((counts + tm_rows - 1) // tm_rows) * tm_rows
    pad_end = jnp.cumsum(padded).astype(jnp.int32)
    pad_start = pad_end - padded
    eidx = tab[META_E0:META_E1 + 1].astype(jnp.int32)
    rank = tab[META_R0:META_R1 + 1].astype(jnp.int32)
    dest = rank
    for e in range(N_EXPERTS):
        dest = dest + jnp.where(eidx == e, pad_start[e], 0)
    dest = dest.reshape(topk * t)
    n_tiles = -(-(t * topk) // tm_rows) + N_EXPERTS
    tile_start = jnp.arange(n_tiles, dtype=jnp.int32) * tm_rows
    tile_expert = jnp.minimum(jnp.sum(tile_start[:, None] >= pad_end[None, :], axis=1),
                              N_EXPERTS - 1).astype(jnp.int32)
    n_active = pad_end[N_EXPERTS - 1:] // tm_rows
    xs = dispatch(dest, pad_end, n_active, hf, n_rows=n_tiles * tm_rows, tm=tm_move, topk=topk, tm_rows=tm_rows)
    y = experts(tile_expert, n_active, xs, w1, w3, w2, tm=tm_rows, tf=tf, fc=256)
    return combine(dest, x1, meta, y, *ple_args, tm=tm_move, topk=topk)


def _tile2(g):
    return jnp.concatenate([g, g]).reshape(1, 2 * g.shape[0])


def kernel(x, p, g_mix, w_in, g_q, g_k, conv_w, conv_b, b_i, b_f, g_h, w_oa, w_ob, w_out, g_ffn, w_d1, w_d3,
           w_d2, w_router, w_e1, w_e3, w_e2, g_ple, w_ple_gate, w_ple_proj):
    batch, seq, d = x.shape
    depth = w_in.shape[0]
    t = batch * seq
    nh = MLSTM_HEADS
    x2 = x.reshape(t, d)
    c_q, c_k, c_v = 0, 512, 1024
    c_qk, c_vm, c_om, c_i, c_f, c_ga, c_gb, c_end = 1536, 2560, 3072, 3584, 3588, 3592, 4616, 5640

    for l in range(depth):
        w = w_in[l]
        w_bf = w.astype(BF16)
        w_gates = w_bf[:, c_ga:c_end]
        proj, gif = in_proj(x2, g_mix[l].reshape(1, d), w_gates, w_bf, nb_cols=c_i, if_col=c_i,
                            tm=1024, tn=512)

        ya = moba(proj, _tile2(g_q[l]), _tile2(g_k[l]), batch=batch, seq=seq)

        bias = jnp.concatenate([b_i[l], b_f[l]])
        bias_row = jnp.pad(bias, (0, LANES - 2 * nh)).reshape(1, LANES)
        bias_col = bias.reshape(2 * nh, 1)
        gates_row = gif[:, :2 * nh].reshape(batch, seq, 2 * nh).transpose(0, 2, 1)
        yb = mlstm(proj, gif, gates_row, bias_row, bias_col, conv_w[l], conv_b[l].reshape(1, -1),
                   g_h[l].reshape(1, -1), batch=batch, seq=seq, chunk=256)

        j = l // 2
        moe = l % 2 == 1
        wr = jnp.pad(w_router[j], ((0, 0), (0, LANES - N_EXPERTS))) if moe else None
        outs = merge(ya, yb, proj, x2, w_oa[l].astype(BF16), w_ob[l].astype(BF16), w_out[l].astype(BF16),
                     g_ffn[l].reshape(1, d), wr, tm=1024)
        ple_args = (p.reshape(depth * t, -1), l * t, g_ple[l].reshape(1, d), w_ple_gate[l].astype(BF16),
                    w_ple_proj[l].astype(BF16))
        if moe:
            x1, hf, logits = outs
            x2 = moe_ffn(hf, x1, logits, w_e1[j].astype(BF16), w_e3[j].astype(BF16), w_e2[j].astype(BF16),
                         ple_args, tm_route=512, tm_rows=512, tf=1792, tm_move=512)
        else:
            x1, hf = outs
            x2 = dense_ffn(hf, x1, w_d1[j].astype(BF16), w_d3[j].astype(BF16), w_d2[j].astype(BF16),
                           *ple_args, tm=1024, fc=256)
    return x2.reshape(batch, seq, d)
```

```python
import functools

import jax
import jax.numpy as jnp
from jax import lax
from jax.experimental import pallas as pl
from jax.experimental.pallas import tpu as pltpu

F32 = jnp.float32
BF16 = jnp.bfloat16
HIGHEST = lax.Precision.HIGHEST

RMS_EPS = 1e-6
LANES = 128
SUBLANES = 8

MOBA_HEADS = 8
MOBA_HEAD_DIM = 64
MOBA_BLOCK = 256
MOBA_TOPK = 3
MLSTM_HEADS = 4
MLSTM_DIM = 128
CONV_WIDTH = 4
N_EXPERTS = 8

COL_GA, COL_GB = 0, 8
COL_QA, COL_KA, COL_VA = 16, 20, 24
COL_QM, COL_KM, COL_VM, COL_OM = 28, 32, 36, 40
N_PROJ = 44 * LANES

VMEM_LIMIT = 56 * 1024 * 1024
ROW_DMA_UNROLL = 8


def _params(*sem):
    return pltpu.CompilerParams(dimension_semantics=sem, vmem_limit_bytes=VMEM_LIMIT)


def _sigmoid(x):
    return 1.0 / (1.0 + jnp.exp(-x))


def _rms(x, g):
    return x * lax.rsqrt(jnp.mean(x * x, axis=-1, keepdims=True) + RMS_EPS) * g


def _split_bf16(x):
    hi = x.astype(BF16)
    return hi, (x - hi.astype(F32)).astype(BF16)


def _dot_split(a, b):
    ah, al = _split_bf16(a)
    bh, bl = _split_bf16(b)
    return (jnp.dot(ah, bh, preferred_element_type=F32) + jnp.dot(ah, bl, preferred_element_type=F32)
            + jnp.dot(al, bh, preferred_element_type=F32))


def _store_row_tiles(ref, x):
    rows = x.shape[0]
    for s in range(SUBLANES):
        ref[pl.ds(s, rows, stride=SUBLANES), :] = x[:, s * LANES:(s + 1) * LANES]


def _load_row_tiles(ref, rows):
    return jnp.concatenate([ref[pl.ds(s, rows, stride=SUBLANES), :] for s in range(SUBLANES)], axis=-1)


def _nt_dot(a, b, **kw):
    return lax.dot_general(a, b, (((1,), (1,)), ((), ())), preferred_element_type=F32, **kw)


def _in_proj_body(x_ref, g_ref, wa_ref, wb_ref, o_ref, oif_ref, h_ref, *, nb_cols, if_col, tn):
    h_ref[...] = _rms(x_ref[...], g_ref[...]).astype(BF16)
    oif_ref[...] = jnp.dot(h_ref[...], wb_ref[:, if_col:if_col + LANES], preferred_element_type=F32)
    na = wa_ref.shape[1]
    for c0 in range(0, na + nb_cols, tn):
        w = wa_ref[:, c0:c0 + tn] if c0 < na else wb_ref[:, c0 - na:c0 - na + tn]
        o_ref[:, c0:c0 + tn] = jnp.dot(h_ref[...], w, preferred_element_type=F32).astype(o_ref.dtype)


def in_proj(x2, g, wa, w_full, *, nb_cols, if_col, tm, tn):
    t, d = x2.shape
    n = wa.shape[1] + nb_cols
    resident = dict(pipeline_mode=pl.Buffered(1))
    return pl.pallas_call(
        functools.partial(_in_proj_body, nb_cols=nb_cols, if_col=if_col, tn=tn),
        grid=(t // tm,),
        in_specs=[
            pl.BlockSpec((tm, d), lambda m: (m, 0)),
            pl.BlockSpec((1, d), lambda m: (0, 0)),
            pl.BlockSpec(wa.shape, lambda m: (0, 0), **resident),
            pl.BlockSpec(w_full.shape, lambda m: (0, 0), **resident),
        ],
        out_specs=[
            pl.BlockSpec((tm, n), lambda m: (m, 0)),
            pl.BlockSpec((tm, LANES), lambda m: (m, 0)),
        ],
        out_shape=[jax.ShapeDtypeStruct((t, n), BF16), jax.ShapeDtypeStruct((t, LANES), F32)],
        scratch_shapes=[pltpu.VMEM((tm, d), BF16)],
        compiler_params=_params("arbitrary"),
        name="in_proj",
    )(x2, g, wa, w_full)


MASK_BIAS = -1e30
LOG2_E = 1.4426950408889634


def _moba_body(q_ref, k_ref, v_ref, gq_ref, gk_ref, o_ref,
               kn_ref, vt_ref, kmean_ref, qaug_ref, s_ref, m_ref, alpha_ref, acc_ref,
               *, nb, blk, dh, topk, nheads):
    i = pl.program_id(1)
    pair = 2 * blk
    lane = lax.broadcasted_iota(jnp.int32, (1, LANES), 1)
    head0 = lane < dh

    same_head = (lax.broadcasted_iota(jnp.int32, (LANES, LANES), 0) // dh
                 == lax.broadcasted_iota(jnp.int32, (LANES, LANES), 1) // dh)
    head_ones = jnp.where(same_head, 1.0, 0.0).astype(BF16)

    def head_rms(x, g, on_mxu):
        x2 = x * x
        if on_mxu:
            hi, lo = _split_bf16(x2)
            ss = (jnp.dot(hi, head_ones, preferred_element_type=F32)
                  + jnp.dot(lo, head_ones, preferred_element_type=F32))
        else:
            s0 = jnp.sum(jnp.where(head0, x2, 0.0), axis=-1, keepdims=True)
            s1 = jnp.sum(jnp.where(head0, 0.0, x2), axis=-1, keepdims=True)
            ss = jnp.where(head0, s0, s1)
        return x * lax.rsqrt(ss * (1.0 / dh) + RMS_EPS) * g

    @pl.when(i == 0)
    def _():
        def prep(j, c):
            r0 = pl.multiple_of(j * blk, blk)
            onehot = jnp.where(lane == dh + j, 1.0, 0.0)
            for p in range(nheads // 2):
                cols = slice(p * LANES, (p + 1) * LANES)
                kn = head_rms(k_ref[pl.ds(r0, blk), cols].astype(F32), gk_ref[...], True)
                for hh, kh in ((0, kn), (1, pltpu.roll(kn, dh, axis=1))):
                    h = 2 * p + hh
                    kmean_ref[h, pl.ds(j, 1), :] = jnp.mean(jnp.where(head0, kh, 0.0), axis=0, keepdims=True)
                    kn_ref[h, pl.ds(r0, blk), :] = jnp.where(head0, kh, onehot).astype(BF16)
                v_t = v_ref[pl.ds(r0, blk), cols].astype(F32).T.astype(BF16)
                for hh in range(2):
                    vt_ref[2 * p + hh, 0:dh, pl.ds(r0, blk)] = v_t[hh * dh:(hh + 1) * dh, :]
                    vt_ref[2 * p + hh, dh:, pl.ds(r0, blk)] = jnp.ones((vt_ref.shape[1] - dh, blk), BF16)
            return c

        lax.fori_loop(0, nb, prep, 0)

    jidx = lax.broadcasted_iota(jnp.int32, (nb, blk), 0)
    key_i = lax.broadcasted_iota(jnp.int32, (blk, blk), 0)
    qry_i = lax.broadcasted_iota(jnp.int32, (blk, blk), 1)
    causal = key_i <= qry_i
    r_own = pl.multiple_of(i * blk, blk)
    qk_scale = dh ** -0.5 * LOG2_E
    for p in range(nheads // 2):
        cols = slice(p * LANES, (p + 1) * LANES)
        qn_t = head_rms(q_ref[:, cols].astype(F32), gq_ref[...], False).T
        for hh in range(2):
            h = 2 * p + hh
            q_t = qn_t[hh * dh:(hh + 1) * dh, :]
            gate = jnp.dot(kmean_ref[h], jnp.concatenate([q_t, jnp.zeros((LANES - dh, blk), F32)], axis=0),
                           precision=HIGHEST, preferred_element_type=F32)
            rank = jnp.zeros((nb, blk), F32)
            for jp in range(nb):
                row = gate[jp:jp + 1, :]
                beats = (row > gate) | ((row == gate) & (jidx > jp))
                rank = rank + jnp.where(beats, jnp.where(jp < i, 1.0, 0.0), 0.0)
            sel = (rank < topk) & (jidx < i)
            q_s = q_t * qk_scale
            pad = jnp.zeros((LANES - dh - nb, blk), F32)
            qaug_ref[h] = jnp.concatenate([q_s, jnp.where(sel, 0.0, MASK_BIAS), pad], axis=0).astype(BF16)
            qaug_own = jnp.concatenate([q_s, jnp.where(jidx == i, 0.0, MASK_BIAS), pad], axis=0).astype(BF16)
            st = jnp.dot(kn_ref[h, pl.ds(r_own, blk), :], qaug_own, preferred_element_type=F32)
            st = jnp.where(causal, st, -jnp.inf)
            s_ref[h, 0:blk, :] = st
            m_ref[h] = jnp.max(st, axis=0, keepdims=True)

    def finish_own(h):
        pr = jnp.exp2(s_ref[h, 0:blk, :] - m_ref[h]).astype(BF16)
        acc_ref[h] = jnp.dot(vt_ref[h, :, pl.ds(r_own, blk)], pr, preferred_element_type=F32)

    def score_pair(u, h):
        r0 = pl.multiple_of(u * pair, pair)
        st = jnp.dot(kn_ref[h, pl.ds(r0, pair), :], qaug_ref[h], preferred_element_type=F32)
        m_old = m_ref[h]
        m_new = jnp.maximum(m_old, jnp.max(st, axis=0, keepdims=True))
        s_ref[h] = st
        alpha_ref[h] = jnp.exp2(m_old - m_new)
        m_ref[h] = m_new

    def finish_pair(u, h):
        r0 = pl.multiple_of(u * pair, pair)
        pr = jnp.exp2(s_ref[h] - m_ref[h]).astype(BF16)
        acc_ref[h] = alpha_ref[h] * acc_ref[h] + jnp.dot(vt_ref[h, :, pl.ds(r0, pair)], pr,
                                                         preferred_element_type=F32)

    n_pairs = jnp.maximum((i + 1) // 2, 1)
    for h in range(nheads):
        finish_own(h)
        score_pair(0, h)

    def body(u, c):
        for h in range(nheads):
            finish_pair(u - 1, h)
            score_pair(u, h)
        return c

    lax.fori_loop(1, n_pairs, body, 0)
    for h in range(nheads):
        finish_pair(n_pairs - 1, h)

    for p in range(nheads // 2):
        a0 = acc_ref[2 * p]
        a1 = acc_ref[2 * p + 1]
        ot = jnp.concatenate([a0[0:dh] / a0[dh:dh + 1], a1[0:dh] / a1[dh:dh + 1]], axis=0)
        o_ref[:, p * LANES:(p + 1) * LANES] = ot.T.astype(o_ref.dtype)


def moba(proj, gq2, gk2, *, batch, seq):
    nb = seq // MOBA_BLOCK
    blk = MOBA_BLOCK
    dh = MOBA_HEAD_DIM
    nheads = MOBA_HEADS
    width = nheads * dh
    wb = width // LANES
    assert dh + nb <= LANES and 2 * dh == LANES and nb % 2 == 0
    v_rows = dh + 2 * SUBLANES
    body = functools.partial(_moba_body, nb=nb, blk=blk, dh=dh, topk=MOBA_TOPK, nheads=nheads)
    return pl.pallas_call(
        body,
        grid=(batch, nb),
        in_specs=[
            pl.BlockSpec((blk, width), lambda b, i: (b * nb + i, COL_QA // wb)),
            pl.BlockSpec((seq, width), lambda b, i: (b, COL_KA // wb)),
            pl.BlockSpec((seq, width), lambda b, i: (b, COL_VA // wb)),
            pl.BlockSpec((1, LANES), lambda b, i: (0, 0)),
            pl.BlockSpec((1, LANES), lambda b, i: (0, 0)),
        ],
        out_specs=pl.BlockSpec((blk, width), lambda b, i: (b * nb + i, 0)),
        out_shape=jax.ShapeDtypeStruct((batch * seq, width), BF16),
        scratch_shapes=[
            pltpu.VMEM((nheads, seq, LANES), BF16),
            pltpu.VMEM((nheads, v_rows, seq), BF16),
            pltpu.VMEM((nheads, nb, LANES), F32),
            pltpu.VMEM((nheads, LANES, blk), BF16),
            pltpu.VMEM((nheads, 2 * blk, blk), F32),
            pltpu.VMEM((nheads, 1, blk), F32),
            pltpu.VMEM((nheads, 1, blk), F32),
            pltpu.VMEM((nheads, v_rows, blk), F32),
        ],
        compiler_params=_params("arbitrary", "arbitrary"),
        name="moba",
    )(proj, proj, proj, gq2, gk2)


def _log_sigmoid(x):
    return jnp.minimum(x, 0.0) - jnp.log(1.0 + jnp.exp(-jnp.abs(x)))


def _dot_tri(tri, x, tri_left):
    out = None
    for _ in range(3):
        piece = x.astype(BF16)
        x = x - piece.astype(F32)
        term = (jnp.dot(tri, piece, preferred_element_type=F32) if tri_left
                else jnp.dot(piece, tri, preferred_element_type=F32))
        out = term if out is None else out + term
    return out


def _mlstm_body(qr_ref, kr_ref, v_ref, og_ref, gcol_ref, grow_ref, brow_ref, bcol_ref,
                cwq_ref, cwk_ref, cbq_ref, cbk_ref, gh_ref, o_ref,
                qx_ref, kx_ref, c_ref, m_ref, *, chunk, dk, nh):
    L = chunk
    width = nh * dk

    @pl.when(pl.program_id(1) == 0)
    def _():
        qx_ref[0:SUBLANES, :] = jnp.zeros((SUBLANES, width), F32)
        kx_ref[0:SUBLANES, :] = jnp.zeros((SUBLANES, width), F32)
        c_ref[...] = jnp.zeros_like(c_ref)
        m_ref[...] = jnp.zeros_like(m_ref)

    qx_ref[SUBLANES:SUBLANES + L, :] = qr_ref[...].astype(F32)
    kx_ref[SUBLANES:SUBLANES + L, :] = kr_ref[...].astype(F32)

    def conv_silu(x_ref, w_ref, b_ref):
        acc = b_ref[...] + w_ref[0:1, :] * x_ref[pl.ds(SUBLANES - CONV_WIDTH + 1, L), :]
        for j in range(1, CONV_WIDTH):
            acc = acc + w_ref[j:j + 1, :] * x_ref[pl.ds(SUBLANES - CONV_WIDTH + 1 + j, L), :]
        return acc * _sigmoid(acc)

    q_all = conv_silu(qx_ref, cwq_ref, cbq_ref)
    k_all = conv_silu(kx_ref, cwk_ref, cbk_ref) * (dk ** -0.5)
    qx_ref[0:SUBLANES, :] = qx_ref[L:L + SUBLANES, :]
    kx_ref[0:SUBLANES, :] = kx_ref[L:L + SUBLANES, :]

    pre_col = gcol_ref[...] + brow_ref[...]
    pre_row = grow_ref[0] + bcol_ref[...]
    t_i = lax.broadcasted_iota(jnp.int32, (L, L), 0)
    s_i = lax.broadcasted_iota(jnp.int32, (L, L), 1)
    tril = s_i <= t_i
    bcum_cols = _dot_tri(jnp.where(tril, 1.0, 0.0).astype(BF16), _log_sigmoid(pre_col), True)
    bcum_rows = _dot_tri(jnp.where(t_i <= s_i, 1.0, 0.0).astype(BF16), _log_sigmoid(pre_row), False)
    ones = jnp.ones((L, dk), BF16)

    for h in range(nh):
        cols = slice(h * dk, (h + 1) * dk)
        q = q_all[:, cols]
        k = k_all[:, cols]
        i_col = pre_col[:, h:h + 1]
        i_row = pre_row[h:h + 1, :]
        bcum_col = bcum_cols[:, nh + h:nh + h + 1]
        bcum_row = bcum_rows[nh + h:nh + h + 1, :]

        m_prev = m_ref[h, 0:1, 0:1]
        a_col = bcum_col + m_prev
        dmat = jnp.where(tril, bcum_col - bcum_row + i_row, -jnp.inf)
        m_t = jnp.maximum(a_col, jnp.max(dmat, axis=-1, keepdims=True))
        dw = jnp.exp(dmat - m_t)
        aw = jnp.exp(a_col - m_t)

        qb = q.astype(BF16)
        kb = k.astype(BF16)
        v_aug = jnp.concatenate([v_ref[:, cols], ones], axis=-1)
        sqk = _nt_dot(qb, kb) * dw
        num_aug = (aw * jnp.dot(qb, c_ref[h].astype(BF16), preferred_element_type=F32)
                   + jnp.dot(sqk.astype(BF16), v_aug, preferred_element_type=F32))
        den = num_aug[:, dk:dk + 1]
        hc = num_aug[:, 0:dk] / jnp.maximum(jnp.abs(den), jnp.exp(-m_t))

        b_last = bcum_col[L - 1:L, :]
        g_col = b_last - bcum_col + i_col
        m_new = jnp.maximum(b_last + m_prev, jnp.max(g_col, axis=0, keepdims=True))
        w_c = jnp.exp(b_last + m_prev - m_new)
        kw_t = (k * jnp.exp(g_col - m_new)).T.astype(BF16)
        c_ref[h] = w_c * c_ref[h] + jnp.dot(kw_t, v_aug, preferred_element_type=F32)
        m_ref[h] = jnp.broadcast_to(m_new, (1, LANES))

        o_ref[:, cols] = (_rms(hc, gh_ref[...]) * _sigmoid(og_ref[:, cols].astype(F32))).astype(o_ref.dtype)


def mlstm(proj, gates_col, gates_row, bias_row, bias_col, conv_w, conv_b, gh, *, batch, seq, chunk):
    nh = MLSTM_HEADS
    dk = MLSTM_DIM
    width = nh * dk
    wb = width // LANES
    nc = seq // chunk
    body = functools.partial(_mlstm_body, chunk=chunk, dk=dk, nh=nh)

    def rows(col0):
        return pl.BlockSpec((chunk, width), lambda b, c: (b * nc + c, col0 // wb))

    return pl.pallas_call(
        body,
        grid=(batch, nc),
        in_specs=[
            rows(COL_QM), rows(COL_KM), rows(COL_VM), rows(COL_OM),
            pl.BlockSpec((chunk, LANES), lambda b, c: (b * nc + c, 0)),
            pl.BlockSpec((1, SUBLANES, chunk), lambda b, c: (b, 0, c)),
            pl.BlockSpec((1, LANES), lambda b, c: (0, 0)),
            pl.BlockSpec((SUBLANES, 1), lambda b, c: (0, 0)),
            pl.BlockSpec((CONV_WIDTH, width), lambda b, c: (0, 0)),
            pl.BlockSpec((CONV_WIDTH, width), lambda b, c: (0, 1)),
            pl.BlockSpec((1, width), lambda b, c: (0, 0)),
            pl.BlockSpec((1, width), lambda b, c: (0, 1)),
            pl.BlockSpec((1, LANES), lambda b, c: (0, 0)),
        ],
        out_specs=pl.BlockSpec((chunk, width), lambda b, c: (b * nc + c, 0)),
        out_shape=jax.ShapeDtypeStruct((batch * seq, width), BF16),
        scratch_shapes=[
            pltpu.VMEM((chunk + 2 * SUBLANES, width), F32),
            pltpu.VMEM((chunk + 2 * SUBLANES, width), F32),
            pltpu.VMEM((nh, dk, 2 * dk), F32),
            pltpu.VMEM((nh, 1, LANES), F32),
        ],
        compiler_params=_params("arbitrary", "arbitrary"),
        name="mlstm",
    )(proj, proj, proj, proj, gates_col, gates_row, bias_row, bias_col,
      conv_w, conv_w, conv_b, conv_b, gh)


def _merge_body(ya_ref, yb_ref, ga_ref, gb_ref, x_ref, woa_ref, wob_ref, wout_ref, gffn_ref, *rest, moe):
    a = jnp.dot(ya_ref[...], woa_ref[...], preferred_element_type=F32)
    b = jnp.dot(yb_ref[...], wob_ref[...], preferred_element_type=F32)
    mixed = _sigmoid(ga_ref[...].astype(F32)) * a + _sigmoid(gb_ref[...].astype(F32)) * b
    x1 = x_ref[...] + jnp.dot(mixed.astype(BF16), wout_ref[...], preferred_element_type=F32)
    hf = _rms(x1, gffn_ref[...])
    if moe:
        wr_ref, x1_ref, hf_ref, lg_ref = rest
        _store_row_tiles(hf_ref, hf)
        lg_ref[...] = _dot_split(hf, wr_ref[...])
    else:
        x1_ref, hf_ref = rest
        hf_ref[...] = hf.astype(BF16)
    x1_ref[...] = x1


def merge(ya, yb, proj, x2, woa, wob, wout, gffn, wr, *, tm):
    t, d = x2.shape
    moe = wr is not None
    full = lambda m: (0, 0)
    in_specs = [
        pl.BlockSpec((tm, ya.shape[1]), lambda m: (m, 0)),
        pl.BlockSpec((tm, yb.shape[1]), lambda m: (m, 0)),
        pl.BlockSpec((tm, d), lambda m: (m, COL_GA * LANES // d)),
        pl.BlockSpec((tm, d), lambda m: (m, COL_GB * LANES // d)),
        pl.BlockSpec((tm, d), lambda m: (m, 0)),
        pl.BlockSpec(woa.shape, full), pl.BlockSpec(wob.shape, full), pl.BlockSpec(wout.shape, full),
        pl.BlockSpec((1, d), full),
    ]
    args = [ya, yb, proj, proj, x2, woa, wob, wout, gffn]
    out_specs = [pl.BlockSpec((tm, d), lambda m: (m, 0)), pl.BlockSpec((tm, d), lambda m: (m, 0))]
    out_shape = [jax.ShapeDtypeStruct((t, d), F32), jax.ShapeDtypeStruct((t, d), BF16)]
    if moe:
        out_specs[1] = pl.BlockSpec((tm * SUBLANES, LANES), lambda m: (m, 0))
        out_shape[1] = jax.ShapeDtypeStruct((t * SUBLANES, LANES), F32)
    if moe:
        in_specs.append(pl.BlockSpec(wr.shape, full))
        args.append(wr)
        out_specs.append(pl.BlockSpec((tm, LANES), lambda m: (m, 0)))
        out_shape.append(jax.ShapeDtypeStruct((t, LANES), F32))
    return pl.pallas_call(
        functools.partial(_merge_body, moe=moe),
        grid=(t // tm,),
        in_specs=in_specs, out_specs=out_specs, out_shape=out_shape,
        compiler_params=_params("arbitrary"),
        name="merge_moe" if moe else "merge",
    )(*args)


def _swiglu(x, w1_ref, w3_ref, w2_ref, g_ref, fc):
    dff = g_ref.shape[1]
    for f0 in range(0, dff, fc):
        a = jnp.dot(x, w1_ref[:, f0:f0 + fc], preferred_element_type=F32)
        b = jnp.dot(x, w3_ref[:, f0:f0 + fc], preferred_element_type=F32)
        g_ref[:, f0:f0 + fc] = (a * _sigmoid(a) * b).astype(BF16)
    return jnp.dot(g_ref[...], w2_ref[...], preferred_element_type=F32)


def _ple(x, p_ref, g_ref, wg_ref, wp_ref):
    gate = _sigmoid(jnp.dot(_rms(x, g_ref[...]).astype(BF16), wg_ref[...], preferred_element_type=F32))
    emb = jnp.dot(p_ref[...].astype(BF16), wp_ref[...], preferred_element_type=F32)
    return x + gate * emb


def _ffn_body(hf_ref, x1_ref, w1_ref, w3_ref, w2_ref, p_ref, g_ref, wg_ref, wp_ref, o_ref, act_ref, *, fc):
    x2 = x1_ref[...] + _swiglu(hf_ref[...], w1_ref, w3_ref, w2_ref, act_ref, fc)
    o_ref[...] = _ple(x2, p_ref, g_ref, wg_ref, wp_ref)


def dense_ffn(hf, x1, w1, w3, w2, p2, p_row0, g, wg, wp, *, tm, fc):
    t, d = x1.shape
    dff = w1.shape[1]
    p_blk0 = p_row0 // tm
    resident = dict(pipeline_mode=pl.Buffered(1))
    full = lambda m: (0, 0)
    return pl.pallas_call(
        functools.partial(_ffn_body, fc=fc),
        grid=(t // tm,),
        in_specs=[
            pl.BlockSpec((tm, d), lambda m: (m, 0)),
            pl.BlockSpec((tm, d), lambda m: (m, 0)),
            pl.BlockSpec((d, dff), full, **resident),
            pl.BlockSpec((d, dff), full, **resident),
            pl.BlockSpec((dff, d), full, **resident),
            pl.BlockSpec((tm, p2.shape[1]), lambda m: (p_blk0 + m, 0)),
            pl.BlockSpec((1, d), full),
            pl.BlockSpec(wg.shape, full, **resident),
            pl.BlockSpec(wp.shape, full, **resident),
        ],
        out_specs=pl.BlockSpec((tm, d), lambda m: (m, 0)),
        out_shape=jax.ShapeDtypeStruct((t, d), F32),
        scratch_shapes=[pltpu.VMEM((tm, dff), BF16)],
        compiler_params=_params("arbitrary"),
        name="dense_ffn",
    )(hf, x1, w1, w3, w2, p2, g, wg, wp)


META_E0, META_E1, META_G0, META_G1, META_R0, META_R1 = 0, 1, 2, 3, 4, 5


def _route_body(lg_ref, meta_ref, tab_ref, cnt_ref, carry_ref, *, tm, ne):
    @pl.when(pl.program_id(0) == 0)
    def _():
        carry_ref[...] = jnp.zeros_like(carry_ref)

    lane = lax.broadcasted_iota(jnp.int32, (tm, LANES), 1)
    lanef = lane.astype(F32)
    lg = jnp.where(lane < ne, lg_ref[...], -jnp.inf)
    m1 = jnp.max(lg, axis=-1, keepdims=True)
    e1 = jnp.min(jnp.where(lg == m1, lanef, float(LANES)), axis=-1, keepdims=True)
    lg2 = jnp.where(lanef == e1, -jnp.inf, lg)
    m2 = jnp.max(lg2, axis=-1, keepdims=True)
    e2 = jnp.min(jnp.where(lg2 == m2, lanef, float(LANES)), axis=-1, keepdims=True)
    ex = jnp.exp(m2 - m1)
    g1 = 1.0 / (1.0 + ex)
    g2 = ex / (1.0 + ex)
    onehot = jnp.where((lanef == e1) | (lanef == e2), 1.0, 0.0)
    row = lax.broadcasted_iota(jnp.int32, (tm, tm), 0)
    col = lax.broadcasted_iota(jnp.int32, (tm, tm), 1)
    before = jnp.dot(jnp.where(col < row, 1.0, 0.0).astype(BF16), onehot.astype(BF16),
                     preferred_element_type=F32) + carry_ref[...]
    r1 = jnp.sum(jnp.where(lanef == e1, before, 0.0), axis=-1, keepdims=True)
    r2 = jnp.sum(jnp.where(lanef == e2, before, 0.0), axis=-1, keepdims=True)
    carry_ref[...] += jnp.sum(onehot, axis=0, keepdims=True)
    meta = jnp.zeros((tm, LANES), F32)
    for pos, val in ((META_E0, e1), (META_E1, e2), (META_G0, g1), (META_G1, g2), (META_R0, r1), (META_R1, r2)):
        meta = jnp.where(lane == pos, val, meta)
    meta_ref[...] = meta
    tab_ref[...] = meta.T[0:SUBLANES, :]
    cnt_ref[...] = jnp.broadcast_to(carry_ref[...], cnt_ref.shape)


def route(logits, *, tm):
    t = logits.shape[0]
    return pl.pallas_call(
        functools.partial(_route_body, tm=tm, ne=N_EXPERTS),
        grid=(t // tm,),
        in_specs=[pl.BlockSpec((tm, LANES), lambda m: (m, 0))],
        out_specs=[pl.BlockSpec((tm, LANES), lambda m: (m, 0)),
                   pl.BlockSpec((SUBLANES, tm), lambda m: (0, m)),
                   pl.BlockSpec((SUBLANES, LANES), lambda m: (0, 0))],
        out_shape=[jax.ShapeDtypeStruct((t, LANES), F32), jax.ShapeDtypeStruct((SUBLANES, t), F32),
                   jax.ShapeDtypeStruct((SUBLANES, LANES), F32)],
        scratch_shapes=[pltpu.VMEM((1, LANES), F32)],
        compiler_params=_params("arbitrary"),
        name="route",
    )(logits)


def _dispatch_body(dest_ref, pe_ref, na_ref, hf_ref, xs_ref, zero_ref, sem, zsem, *, tm, topk, tm_rows, n_tok):
    base = pl.program_id(0) * tm

    @pl.when(pl.program_id(0) == 0)
    def _():
        zero_ref[...] = jnp.zeros_like(zero_ref)
        n_tiles = xs_ref.shape[0] // (tm_rows * SUBLANES)
        fills = []
        for e in range(N_EXPERTS):
            end = pe_ref[e]
            nonempty = end > (pe_ref[e - 1] if e else 0)
            fills.append((nonempty, pl.multiple_of(jnp.maximum(end - tm_rows, 0), tm_rows)))
        for tile in range(n_tiles):
            fills.append((tile >= na_ref[0], tile * tm_rows))
        for phase in ("start", "wait"):
            for cond, row0 in fills:
                @pl.when(cond)
                def _(row0=row0, phase=phase):
                    cp = pltpu.make_async_copy(
                        zero_ref, xs_ref.at[pl.ds(row0 * SUBLANES, tm_rows * SUBLANES)], zsem)
                    cp.start() if phase == "start" else cp.wait()

    def issue(r, c):
        for k in range(topk):
            d = dest_ref[k * n_tok + base + r]
            pltpu.make_async_copy(_row_tile(hf_ref, r), _row_tile(xs_ref, d), sem).start(priority=k % 2)
        return c

    lax.fori_loop(0, tm, issue, 0, unroll=ROW_DMA_UNROLL)
    for k in range(topk):
        pltpu.make_async_copy(hf_ref, xs_ref.at[pl.ds(0, tm * SUBLANES)], sem).wait()


def _row_tile(ref, r):
    return ref.at[pl.ds(pl.multiple_of(r * SUBLANES, SUBLANES), SUBLANES)]


def dispatch(dest, pad_end, n_active, hf, *, n_rows, tm, topk, tm_rows):
    t = hf.shape[0] // SUBLANES
    return pl.pallas_call(
        functools.partial(_dispatch_body, tm=tm, topk=topk, tm_rows=tm_rows, n_tok=t),
        grid_spec=pltpu.PrefetchScalarGridSpec(
            num_scalar_prefetch=3,
            grid=(t // tm,),
            in_specs=[pl.BlockSpec((tm * SUBLANES, LANES), lambda m, *_: (m, 0))],
            out_specs=pl.BlockSpec(memory_space=pl.ANY),
            scratch_shapes=[pltpu.VMEM((tm_rows * SUBLANES, LANES), F32),
                            pltpu.SemaphoreType.DMA, pltpu.SemaphoreType.DMA],
        ),
        out_shape=jax.ShapeDtypeStruct((n_rows * SUBLANES, LANES), F32),
        compiler_params=_params("arbitrary"),
        name="moe_dispatch",
    )(dest, pad_end, n_active, hf)


def _experts_body(te_ref, na_ref, xs_ref, w1_ref, w3_ref, w2_ref, y_ref, xb_ref, g_ref, acc_ref, *, fc):
    del te_ref
    f = pl.program_id(1)
    tm = xb_ref.shape[0]

    @pl.when(pl.program_id(0) >= na_ref[0])
    def _():
        y_ref[...] = jnp.zeros_like(y_ref)

    @pl.when(pl.program_id(0) < na_ref[0])
    def _():
        @pl.when(f == 0)
        def _():
            xb_ref[...] = _load_row_tiles(xs_ref, tm).astype(BF16)

        y = _swiglu(xb_ref[...], w1_ref.at[0], w3_ref.at[0], w2_ref.at[0], g_ref, fc)

        @pl.when(f == 0)
        def _():
            acc_ref[...] = y

        @pl.when((f > 0) & (f < pl.num_programs(1) - 1))
        def _():
            acc_ref[...] += y

        @pl.when(f == pl.num_programs(1) - 1)
        def _():
            _store_row_tiles(y_ref, acc_ref[...] + y)


def experts(tile_expert, n_active, xs, w1, w3, w2, *, tm, tf, fc):
    n_rows = xs.shape[0] // SUBLANES
    d = w1.shape[1]
    dff = w1.shape[2]
    assert dff // tf >= 2
    row_tile = lambda i, f, te, na: (jnp.minimum(i, na[0] - 1), 0)
    ftile = lambda i, f, na: jnp.where(i < na[0], f, dff // tf - 1)
    return pl.pallas_call(
        functools.partial(_experts_body, fc=fc),
        grid_spec=pltpu.PrefetchScalarGridSpec(
            num_scalar_prefetch=2,
            grid=(n_rows // tm, dff // tf),
            in_specs=[
                pl.BlockSpec((tm * SUBLANES, LANES), row_tile),
                pl.BlockSpec((1, d, tf), lambda i, f, te, na: (te[i], 0, ftile(i, f, na))),
                pl.BlockSpec((1, d, tf), lambda i, f, te, na: (te[i], 0, ftile(i, f, na))),
                pl.BlockSpec((1, tf, d), lambda i, f, te, na: (te[i], ftile(i, f, na), 0)),
            ],
            out_specs=pl.BlockSpec((tm * SUBLANES, LANES), lambda i, f, te, na: (i, 0)),
            scratch_shapes=[pltpu.VMEM((tm, d), BF16), pltpu.VMEM((tm, tf), BF16), pltpu.VMEM((tm, d), F32)],
        ),
        out_shape=jax.ShapeDtypeStruct((n_rows * SUBLANES, LANES), F32),
        compiler_params=_params("arbitrary", "arbitrary"),
        name="moe_experts",
    )(tile_expert, n_active, xs, w1, w3, w2)


def _combine_body(dest_ref, x1_ref, meta_ref, p_ref, g_ref, wg_ref, wp_ref, y_ref, o_ref, buf_ref, sem,
                  *, tm, topk):
    base = pl.program_id(0) * tm
    n_tok = pl.num_programs(0) * tm

    def issue(r, c):
        for k in range(topk):
            d = dest_ref[k * n_tok + base + r]
            pltpu.make_async_copy(_row_tile(y_ref, d), _row_tile(buf_ref.at[k], r), sem).start(priority=k % 2)
        return c

    lax.fori_loop(0, tm, issue, 0, unroll=ROW_DMA_UNROLL)
    for k in range(topk):
        pltpu.make_async_copy(y_ref.at[pl.ds(0, tm * SUBLANES)], buf_ref.at[k], sem).wait()
    meta = meta_ref[...]
    g0 = meta[:, META_G0:META_G0 + 1]
    g1 = meta[:, META_G1:META_G1 + 1]
    x2 = x1_ref[...] + (g0 * _load_row_tiles(buf_ref.at[0], tm) + g1 * _load_row_tiles(buf_ref.at[1], tm))
    o_ref[...] = _ple(x2, p_ref, g_ref, wg_ref, wp_ref)


def combine(dest, x1, meta, y, p2, p_row0, g, wg, wp, *, tm, topk):
    t, d = x1.shape
    full = lambda m, dest: (0, 0)
    p_blk0 = p_row0 // tm
    return pl.pallas_call(
        functools.partial(_combine_body, tm=tm, topk=topk),
        grid_spec=pltpu.PrefetchScalarGridSpec(
            num_scalar_prefetch=1,
            grid=(t // tm,),
            in_specs=[pl.BlockSpec((tm, d), lambda m, dest: (m, 0)),
                      pl.BlockSpec((tm, LANES), lambda m, dest: (m, 0)),
                      pl.BlockSpec((tm, p2.shape[1]), lambda m, dest: (p_blk0 + m, 0)),
                      pl.BlockSpec((1, d), full), pl.BlockSpec(wg.shape, full), pl.BlockSpec(wp.shape, full),
                      pl.BlockSpec(memory_space=pl.ANY)],
            out_specs=pl.BlockSpec((tm, d), lambda m, dest: (m, 0)),
            scratch_shapes=[pltpu.VMEM((topk, tm * SUBLANES, LANES), F32), pltpu.SemaphoreType.DMA],
        ),
        out_shape=jax.ShapeDtypeStruct((t, d), F32),
        compiler_params=_params("arbitrary"),
        name="moe_combine",
    )(dest, x1, meta, p2, g, wg, wp, y)


def moe_ffn(hf, x1, logits, w1, w3, w2, ple_args, *, tm_route, tm_rows, tf, tm_move):
    t, d = x1.shape
    topk = 2
    meta, tab, cnt = route(logits, tm=tm_route)
    counts = cnt[0, :N_EXPERTS].astype(jnp.int32)
    padded = ((counts + tm_rows - 1) // tm_rows) * tm_rows
    pad_end = jnp.cumsum(padded).astype(jnp.int32)
    pad_start = pad_end - padded
    eidx = tab[META_E0:META_E1 + 1].astype(jnp.int32)
    rank = tab[META_R0:META_R1 + 1].astype(jnp.int32)
    dest = rank
    for e in range(N_EXPERTS):
        dest = dest + jnp.where(eidx == e, pad_start[e], 0)
    dest = dest.reshape(topk * t)
    n_tiles = -(-(t * topk) // tm_rows) + N_EXPERTS
    tile_start = jnp.arange(n_tiles, dtype=jnp.int32) * tm_rows
    tile_expert = jnp.minimum(jnp.sum(tile_start[:, None] >= pad_end[None, :], axis=1),
                              N_EXPERTS - 1).astype(jnp.int32)
    n_active = pad_end[N_EXPERTS - 1:] // tm_rows
    xs = dispatch(dest, pad_end, n_active, hf, n_rows=n_tiles * tm_rows, tm=tm_move, topk=topk, tm_rows=tm_rows)
    y = experts(tile_expert, n_active, xs, w1, w3, w2, tm=tm_rows, tf=tf, fc=256)
    return combine(dest, x1, meta, y, *ple_args, tm=tm_move, topk=topk)


def _tile2(g):
    return jnp.concatenate([g, g]).reshape(1, 2 * g.shape[0])


def kernel(x, p, g_mix, w_in, g_q, g_k, conv_w, conv_b, b_i, b_f, g_h, w_oa, w_ob, w_out, g_ffn, w_d1, w_d3,
           w_d2, w_router, w_e1, w_e3, w_e2, g_ple, w_ple_gate, w_ple_proj):
    batch, seq, d = x.shape
    depth = w_in.shape[0]
    t = batch * seq
    nh = MLSTM_HEADS
    x2 = x.reshape(t, d)
    c_q, c_k, c_v = 0, 512, 1024
    c_qk, c_vm, c_om, c_i, c_f, c_ga, c_gb, c_end = 1536, 2560, 3072, 3584, 3588, 3592, 4616, 5640

    for l in range(depth):
        w = w_in[l]
        w_bf = w.astype(BF16)
        w_gates = w_bf[:, c_ga:c_end]
        proj, gif = in_proj(x2, g_mix[l].reshape(1, d), w_gates, w_bf, nb_cols=c_i, if_col=c_i,
                            tm=1024, tn=512)

        ya = moba(proj, _tile2(g_q[l]), _tile2(g_k[l]), batch=batch, seq=seq)

        bias = jnp.concatenate([b_i[l], b_f[l]])
        bias_row = jnp.pad(bias, (0, LANES - 2 * nh)).reshape(1, LANES)
        bias_col = bias.reshape(2 * nh, 1)
        gates_row = gif[:, :2 * nh].reshape(batch, seq, 2 * nh).transpose(0, 2, 1)
        yb = mlstm(proj, gif, gates_row, bias_row, bias_col, conv_w[l], conv_b[l].reshape(1, -1),
                   g_h[l].reshape(1, -1), batch=batch, seq=seq, chunk=256)

        j = l // 2
        moe = l % 2 == 1
        wr = jnp.pad(w_router[j], ((0, 0), (0, LANES - N_EXPERTS))) if moe else None
        outs = merge(ya, yb, proj, x2, w_oa[l].astype(BF16), w_ob[l].astype(BF16), w_out[l].astype(BF16),
                     g_ffn[l].reshape(1, d), wr, tm=1024)
        ple_args = (p.reshape(depth * t, -1), l * t, g_ple[l].reshape(1, d), w_ple_gate[l].astype(BF16),
                    w_ple_proj[l].astype(BF16))
        if moe:
            x1, hf, logits = outs
            x2 = moe_ffn(hf, x1, logits, w_e1[j].astype(BF16), w_e3[j].astype(BF16), w_e2[j].astype(BF16),
                         ple_args, tm_route=512, tm_rows=512, tf=1792, tm_move=512)
        else:
            x1, hf = outs
            x2 = dense_ffn(hf, x1, w_d1[j].astype(BF16), w_d3[j].astype(BF16), w_d2[j].astype(BF16),
                           *ple_args, tm=1024, fc=256)
    return x2.reshape(batch, seq, d)
```

```python
import functools

import jax
import jax.numpy as jnp
from jax import lax
from jax.experimental import pallas as pl
from jax.experimental.pallas import tpu as pltpu

F32 = jnp.float32
BF16 = jnp.bfloat16
HIGHEST = lax.Precision.HIGHEST

RMS_EPS = 1e-6
LANES = 128
SUBLANES = 8

MOBA_HEADS = 8
MOBA_HEAD_DIM = 64
MOBA_BLOCK = 256
MOBA_TOPK = 3
MLSTM_HEADS = 4
MLSTM_DIM = 128
CONV_WIDTH = 4
N_EXPERTS = 8

COL_GA, COL_GB = 0, 8
COL_QA, COL_KA, COL_VA = 16, 20, 24
COL_QM, COL_KM, COL_VM, COL_OM = 28, 32, 36, 40
N_PROJ = 44 * LANES

VMEM_LIMIT = 56 * 1024 * 1024
ROW_DMA_UNROLL = 8


def _params(*sem):
    return pltpu.CompilerParams(dimension_semantics=sem, vmem_limit_bytes=VMEM_LIMIT)


def _sigmoid(x):
    return 1.0 / (1.0 + jnp.exp(-x))


def _rms(x, g):
    return x * lax.rsqrt(jnp.mean(x * x, axis=-1, keepdims=True) + RMS_EPS) * g


def _split_bf16(x):
    hi = x.astype(BF16)
    return hi, (x - hi.astype(F32)).astype(BF16)


def _dot_split(a, b):
    ah, al = _split_bf16(a)
    bh, bl = _split_bf16(b)
    return (jnp.dot(ah, bh, preferred_element_type=F32) + jnp.dot(ah, bl, preferred_element_type=F32)
            + jnp.dot(al, bh, preferred_element_type=F32))


def _store_row_tiles(ref, x):
    rows = x.shape[0]
    for s in range(SUBLANES):
        ref[pl.ds(s, rows, stride=SUBLANES), :] = x[:, s * LANES:(s + 1) * LANES]


def _load_row_tiles(ref, rows):
    return jnp.concatenate([ref[pl.ds(s, rows, stride=SUBLANES), :] for s in range(SUBLANES)], axis=-1)


def _nt_dot(a, b, **kw):
    return lax.dot_general(a, b, (((1,), (1,)), ((), ())), preferred_element_type=F32, **kw)


def _in_proj_body(x_ref, g_ref, wa_ref, wb_ref, o_ref, oif_ref, h_ref, *, nb_cols, if_col, tn):
    h_ref[...] = _rms(x_ref[...], g_ref[...]).astype(BF16)
    oif_ref[...] = _nt_dot(h_ref[...], wb_ref[if_col:if_col + LANES, :])
    na = wa_ref.shape[0]
    for c0 in range(0, na + nb_cols, tn):
        w = wa_ref[c0:c0 + tn, :] if c0 < na else wb_ref[c0 - na:c0 - na + tn, :]
        o_ref[:, c0:c0 + tn] = _nt_dot(h_ref[...], w).astype(o_ref.dtype)


def in_proj(x2, g, wa, w_full, *, nb_cols, if_col, tm, tn):
    t, d = x2.shape
    n = wa.shape[0] + nb_cols
    resident = dict(pipeline_mode=pl.Buffered(1))
    return pl.pallas_call(
        functools.partial(_in_proj_body, nb_cols=nb_cols, if_col=if_col, tn=tn),
        grid=(t // tm,),
        in_specs=[
            pl.BlockSpec((tm, d), lambda m: (m, 0)),
            pl.BlockSpec((1, d), lambda m: (0, 0)),
            pl.BlockSpec(wa.shape, lambda m: (0, 0), **resident),
            pl.BlockSpec(w_full.shape, lambda m: (0, 0), **resident),
        ],
        out_specs=[
            pl.BlockSpec((tm, n), lambda m: (m, 0)),
            pl.BlockSpec((tm, LANES), lambda m: (m, 0)),
        ],
        out_shape=[jax.ShapeDtypeStruct((t, n), BF16), jax.ShapeDtypeStruct((t, LANES), F32)],
        scratch_shapes=[pltpu.VMEM((tm, d), BF16)],
        compiler_params=_params("arbitrary"),
        name="in_proj",
    )(x2, g, wa, w_full)


MASK_BIAS = -1e30
LOG2_E = 1.4426950408889634


def _moba_body(q_ref, k_ref, v_ref, gq_ref, gk_ref, o_ref,
               kn_ref, vt_ref, kmean_ref, qaug_ref, s_ref, m_ref, alpha_ref, acc_ref,
               *, nb, blk, dh, topk, nheads):
    i = pl.program_id(1)
    pair = 2 * blk
    lane = lax.broadcasted_iota(jnp.int32, (1, LANES), 1)
    head0 = lane < dh

    same_head = (lax.broadcasted_iota(jnp.int32, (LANES, LANES), 0) // dh
                 == lax.broadcasted_iota(jnp.int32, (LANES, LANES), 1) // dh)
    head_ones = jnp.where(same_head, 1.0, 0.0).astype(BF16)

    def head_rms(x, g, on_mxu):
        x2 = x * x
        if on_mxu:
            hi, lo = _split_bf16(x2)
            ss = (jnp.dot(hi, head_ones, preferred_element_type=F32)
                  + jnp.dot(lo, head_ones, preferred_element_type=F32))
        else:
            s0 = jnp.sum(jnp.where(head0, x2, 0.0), axis=-1, keepdims=True)
            s1 = jnp.sum(jnp.where(head0, 0.0, x2), axis=-1, keepdims=True)
            ss = jnp.where(head0, s0, s1)
        return x * lax.rsqrt(ss * (1.0 / dh) + RMS_EPS) * g

    @pl.when(i == 0)
    def _():
        def prep(j, c):
            r0 = pl.multiple_of(j * blk, blk)
            onehot = jnp.where(lane == dh + j, 1.0, 0.0)
            for p in range(nheads // 2):
                cols = slice(p * LANES, (p + 1) * LANES)
                kn = head_rms(k_ref[pl.ds(r0, blk), cols].astype(F32), gk_ref[...], True)
                for hh, kh in ((0, kn), (1, pltpu.roll(kn, dh, axis=1))):
                    h = 2 * p + hh
                    kmean_ref[h, pl.ds(j, 1), :] = jnp.mean(jnp.where(head0, kh, 0.0), axis=0, keepdims=True)
                    kn_ref[h, pl.ds(r0, blk), :] = jnp.where(head0, kh, onehot).astype(BF16)
                v_t = v_ref[pl.ds(r0, blk), cols].astype(F32).T.astype(BF16)
                for hh in range(2):
                    vt_ref[2 * p + hh, 0:dh, pl.ds(r0, blk)] = v_t[hh * dh:(hh + 1) * dh, :]
                    vt_ref[2 * p + hh, dh:, pl.ds(r0, blk)] = jnp.ones((vt_ref.shape[1] - dh, blk), BF16)
            return c

        lax.fori_loop(0, nb, prep, 0)

    jidx = lax.broadcasted_iota(jnp.int32, (nb, blk), 0)
    key_i = lax.broadcasted_iota(jnp.int32, (blk, blk), 0)
    qry_i = lax.broadcasted_iota(jnp.int32, (blk, blk), 1)
    causal = key_i <= qry_i
    r_own = pl.multiple_of(i * blk, blk)
    qk_scale = dh ** -0.5 * LOG2_E
    for p in range(nheads // 2):
        cols = slice(p * LANES, (p + 1) * LANES)
        qn_t = head_rms(q_ref[:, cols].astype(F32), gq_ref[...], False).T
        for hh in range(2):
            h = 2 * p + hh
            q_t = qn_t[hh * dh:(hh + 1) * dh, :]
            gate = jnp.dot(kmean_ref[h], jnp.concatenate([q_t, jnp.zeros((LANES - dh, blk), F32)], axis=0),
                           precision=HIGHEST, preferred_element_type=F32)
            rank = jnp.zeros((nb, blk), F32)
            for jp in range(nb):
                row = gate[jp:jp + 1, :]
                beats = (row > gate) | ((row == gate) & (jidx > jp))
                rank = rank + jnp.where(beats, jnp.where(jp < i, 1.0, 0.0), 0.0)
            sel = (rank < topk) & (jidx < i)
            q_s = q_t * qk_scale
            pad = jnp.zeros((LANES - dh - nb, blk), F32)
            qaug_ref[h] = jnp.concatenate([q_s, jnp.where(sel, 0.0, MASK_BIAS), pad], axis=0).astype(BF16)
            qaug_own = jnp.concatenate([q_s, jnp.where(jidx == i, 0.0, MASK_BIAS), pad], axis=0).astype(BF16)
            st = jnp.dot(kn_ref[h, pl.ds(r_own, blk), :], qaug_own, preferred_element_type=F32)
            st = jnp.where(causal, st, -jnp.inf)
            s_ref[h, 0:blk, :] = st
            m_ref[h] = jnp.max(st, axis=0, keepdims=True)

    def finish_own(h):
        pr = jnp.exp2(s_ref[h, 0:blk, :] - m_ref[h]).astype(BF16)
        acc_ref[h] = jnp.dot(vt_ref[h, :, pl.ds(r_own, blk)], pr, preferred_element_type=F32)

    def score_pair(u, h):
        r0 = pl.multiple_of(u * pair, pair)
        st = jnp.dot(kn_ref[h, pl.ds(r0, pair), :], qaug_ref[h], preferred_element_type=F32)
        m_old = m_ref[h]
        m_new = jnp.maximum(m_old, jnp.max(st, axis=0, keepdims=True))
        s_ref[h] = st
        alpha_ref[h] = jnp.exp2(m_old - m_new)
        m_ref[h] = m_new

    def finish_pair(u, h):
        r0 = pl.multiple_of(u * pair, pair)
        pr = jnp.exp2(s_ref[h] - m_ref[h]).astype(BF16)
        acc_ref[h] = alpha_ref[h] * acc_ref[h] + jnp.dot(vt_ref[h, :, pl.ds(r0, pair)], pr,
                                                         preferred_element_type=F32)

    n_pairs = jnp.maximum((i + 1) // 2, 1)
    for h in range(nheads):
        finish_own(h)
        score_pair(0, h)

    def body(u, c):
        for h in range(nheads):
            finish_pair(u - 1, h)
            score_pair(u, h)
        return c

    lax.fori_loop(1, n_pairs, body, 0)
    for h in range(nheads):
        finish_pair(n_pairs - 1, h)

    for p in range(nheads // 2):
        a0 = acc_ref[2 * p]
        a1 = acc_ref[2 * p + 1]
        ot = jnp.concatenate([a0[0:dh] / a0[dh:dh + 1], a1[0:dh] / a1[dh:dh + 1]], axis=0)
        o_ref[:, p * LANES:(p + 1) * LANES] = ot.T.astype(o_ref.dtype)


def moba(proj, gq2, gk2, *, batch, seq):
    nb = seq // MOBA_BLOCK
    blk = MOBA_BLOCK
    dh = MOBA_HEAD_DIM
    nheads = MOBA_HEADS
    width = nheads * dh
    wb = width // LANES
    assert dh + nb <= LANES and 2 * dh == LANES and nb % 2 == 0
    v_rows = dh + 2 * SUBLANES
    body = functools.partial(_moba_body, nb=nb, blk=blk, dh=dh, topk=MOBA_TOPK, nheads=nheads)
    return pl.pallas_call(
        body,
        grid=(batch, nb),
        in_specs=[
            pl.BlockSpec((blk, width), lambda b, i: (b * nb + i, COL_QA // wb)),
            pl.BlockSpec((seq, width), lambda b, i: (b, COL_KA // wb)),
            pl.BlockSpec((seq, width), lambda b, i: (b, COL_VA // wb)),
            pl.BlockSpec((1, LANES), lambda b, i: (0, 0)),
            pl.BlockSpec((1, LANES), lambda b, i: (0, 0)),
        ],
        out_specs=pl.BlockSpec((blk, width), lambda b, i: (b * nb + i, 0)),
        out_shape=jax.ShapeDtypeStruct((batch * seq, width), BF16),
        scratch_shapes=[
            pltpu.VMEM((nheads, seq, LANES), BF16),
            pltpu.VMEM((nheads, v_rows, seq), BF16),
            pltpu.VMEM((nheads, nb, LANES), F32),
            pltpu.VMEM((nheads, LANES, blk), BF16),
            pltpu.VMEM((nheads, 2 * blk, blk), F32),
            pltpu.VMEM((nheads, 1, blk), F32),
            pltpu.VMEM((nheads, 1, blk), F32),
            pltpu.VMEM((nheads, v_rows, blk), F32),
        ],
        compiler_params=_params("arbitrary", "arbitrary"),
        name="moba",
    )(proj, proj, proj, gq2, gk2)


def _log_sigmoid(x):
    return jnp.minimum(x, 0.0) - jnp.log(1.0 + jnp.exp(-jnp.abs(x)))


def _dot_tri(tri, x, tri_left):
    out = None
    for _ in range(3):
        piece = x.astype(BF16)
        x = x - piece.astype(F32)
        term = (jnp.dot(tri, piece, preferred_element_type=F32) if tri_left
                else jnp.dot(piece, tri, preferred_element_type=F32))
        out = term if out is None else out + term
    return out


def _mlstm_body(qr_ref, kr_ref, v_ref, og_ref, gcol_ref, grow_ref, brow_ref, bcol_ref,
                cwq_ref, cwk_ref, cbq_ref, cbk_ref, gh_ref, o_ref,
                qx_ref, kx_ref, c_ref, m_ref, *, chunk, dk, nh):
    L = chunk
    width = nh * dk

    @pl.when(pl.program_id(1) == 0)
    def _():
        qx_ref[0:SUBLANES, :] = jnp.zeros((SUBLANES, width), F32)
        kx_ref[0:SUBLANES, :] = jnp.zeros((SUBLANES, width), F32)
        c_ref[...] = jnp.zeros_like(c_ref)
        m_ref[...] = jnp.zeros_like(m_ref)

    qx_ref[SUBLANES:SUBLANES + L, :] = qr_ref[...].astype(F32)
    kx_ref[SUBLANES:SUBLANES + L, :] = kr_ref[...].astype(F32)

    def conv_silu(x_ref, w_ref, b_ref):
        acc = b_ref[...] + w_ref[0:1, :] * x_ref[pl.ds(SUBLANES - CONV_WIDTH + 1, L), :]
        for j in range(1, CONV_WIDTH):
            acc = acc + w_ref[j:j + 1, :] * x_ref[pl.ds(SUBLANES - CONV_WIDTH + 1 + j, L), :]
        return acc * _sigmoid(acc)

    q_all = conv_silu(qx_ref, cwq_ref, cbq_ref)
    k_all = conv_silu(kx_ref, cwk_ref, cbk_ref) * (dk ** -0.5)
    qx_ref[0:SUBLANES, :] = qx_ref[L:L + SUBLANES, :]
    kx_ref[0:SUBLANES, :] = kx_ref[L:L + SUBLANES, :]

    pre_col = gcol_ref[...] + brow_ref[...]
    pre_row = grow_ref[0] + bcol_ref[...]
    t_i = lax.broadcasted_iota(jnp.int32, (L, L), 0)
    s_i = lax.broadcasted_iota(jnp.int32, (L, L), 1)
    tril = s_i <= t_i
    bcum_cols = _dot_tri(jnp.where(tril, 1.0, 0.0).astype(BF16), _log_sigmoid(pre_col), True)
    bcum_rows = _dot_tri(jnp.where(t_i <= s_i, 1.0, 0.0).astype(BF16), _log_sigmoid(pre_row), False)
    ones = jnp.ones((L, dk), BF16)

    for h in range(nh):
        cols = slice(h * dk, (h + 1) * dk)
        q = q_all[:, cols]
        k = k_all[:, cols]
        i_col = pre_col[:, h:h + 1]
        i_row = pre_row[h:h + 1, :]
        bcum_col = bcum_cols[:, nh + h:nh + h + 1]
        bcum_row = bcum_rows[nh + h:nh + h + 1, :]

        m_prev = m_ref[h, 0:1, 0:1]
        a_col = bcum_col + m_prev
        dmat = jnp.where(tril, bcum_col - bcum_row + i_row, -jnp.inf)
        m_t = jnp.maximum(a_col, jnp.max(dmat, axis=-1, keepdims=True))
        dw = jnp.exp(dmat - m_t)
        aw = jnp.exp(a_col - m_t)

        qb = q.astype(BF16)
        kb = k.astype(BF16)
        v_aug = jnp.concatenate([v_ref[:, cols], ones], axis=-1)
        sqk = _nt_dot(qb, kb) * dw
        num_aug = (aw * jnp.dot(qb, c_ref[h].astype(BF16), preferred_element_type=F32)
                   + jnp.dot(sqk.astype(BF16), v_aug, preferred_element_type=F32))
        den = num_aug[:, dk:dk + 1]
        hc = num_aug[:, 0:dk] / jnp.maximum(jnp.abs(den), jnp.exp(-m_t))

        b_last = bcum_col[L - 1:L, :]
        g_col = b_last - bcum_col + i_col
        m_new = jnp.maximum(b_last + m_prev, jnp.max(g_col, axis=0, keepdims=True))
        w_c = jnp.exp(b_last + m_prev - m_new)
        kw_t = (k * jnp.exp(g_col - m_new)).T.astype(BF16)
        c_ref[h] = w_c * c_ref[h] + jnp.dot(kw_t, v_aug, preferred_element_type=F32)
        m_ref[h] = jnp.broadcast_to(m_new, (1, LANES))

        o_ref[:, cols] = (_rms(hc, gh_ref[...]) * _sigmoid(og_ref[:, cols].astype(F32))).astype(o_ref.dtype)


def mlstm(proj, gates_col, gates_row, bias_row, bias_col, conv_w, conv_b, gh, *, batch, seq, chunk):
    nh = MLSTM_HEADS
    dk = MLSTM_DIM
    width = nh * dk
    wb = width // LANES
    nc = seq // chunk
    body = functools.partial(_mlstm_body, chunk=chunk, dk=dk, nh=nh)

    def rows(col0):
        return pl.BlockSpec((chunk, width), lambda b, c: (b * nc + c, col0 // wb))

    return pl.pallas_call(
        body,
        grid=(batch, nc),
        in_specs=[
            rows(COL_QM), rows(COL_KM), rows(COL_VM), rows(COL_OM),
            pl.BlockSpec((chunk, LANES), lambda b, c: (b * nc + c, 0)),
            pl.BlockSpec((1, SUBLANES, chunk), lambda b, c: (b, 0, c)),
            pl.BlockSpec((1, LANES), lambda b, c: (0, 0)),
            pl.BlockSpec((SUBLANES, 1), lambda b, c: (0, 0)),
            pl.BlockSpec((CONV_WIDTH, width), lambda b, c: (0, 0)),
            pl.BlockSpec((CONV_WIDTH, width), lambda b, c: (0, 1)),
            pl.BlockSpec((1, width), lambda b, c: (0, 0)),
            pl.BlockSpec((1, width), lambda b, c: (0, 1)),
            pl.BlockSpec((1, LANES), lambda b, c: (0, 0)),
        ],
        out_specs=pl.BlockSpec((chunk, width), lambda b, c: (b * nc + c, 0)),
        out_shape=jax.ShapeDtypeStruct((batch * seq, width), BF16),
        scratch_shapes=[
            pltpu.VMEM((chunk + 2 * SUBLANES, width), F32),
            pltpu.VMEM((chunk + 2 * SUBLANES, width), F32),
            pltpu.VMEM((nh, dk, 2 * dk), F32),
            pltpu.VMEM((nh, 1, LANES), F32),
        ],
        compiler_params=_params("arbitrary", "arbitrary"),
        name="mlstm",
    )(proj, proj, proj, proj, gates_col, gates_row, bias_row, bias_col,
      conv_w, conv_w, conv_b, conv_b, gh)


def _merge_body(ya_ref, yb_ref, ga_ref, gb_ref, x_ref, woa_ref, wob_ref, wout_ref, gffn_ref, *rest, moe):
    a = jnp.dot(ya_ref[...], woa_ref[...], preferred_element_type=F32)
    b = jnp.dot(yb_ref[...], wob_ref[...], preferred_element_type=F32)
    mixed = _sigmoid(ga_ref[...].astype(F32)) * a + _sigmoid(gb_ref[...].astype(F32)) * b
    x1 = x_ref[...] + jnp.dot(mixed.astype(BF16), wout_ref[...], preferred_element_type=F32)
    hf = _rms(x1, gffn_ref[...])
    if moe:
        wr_ref, x1_ref, hf_ref, lg_ref = rest
        _store_row_tiles(hf_ref, hf)
        lg_ref[...] = _dot_split(hf, wr_ref[...])
    else:
        x1_ref, hf_ref = rest
        hf_ref[...] = hf.astype(BF16)
    x1_ref[...] = x1


def merge(ya, yb, proj, x2, woa, wob, wout, gffn, wr, *, tm):
    t, d = x2.shape
    moe = wr is not None
    full = lambda m: (0, 0)
    in_specs = [
        pl.BlockSpec((tm, ya.shape[1]), lambda m: (m, 0)),
        pl.BlockSpec((tm, yb.shape[1]), lambda m: (m, 0)),
        pl.BlockSpec((tm, d), lambda m: (m, COL_GA * LANES // d)),
        pl.BlockSpec((tm, d), lambda m: (m, COL_GB * LANES // d)),
        pl.BlockSpec((tm, d), lambda m: (m, 0)),
        pl.BlockSpec(woa.shape, full), pl.BlockSpec(wob.shape, full), pl.BlockSpec(wout.shape, full),
        pl.BlockSpec((1, d), full),
    ]
    args = [ya, yb, proj, proj, x2, woa, wob, wout, gffn]
    out_specs = [pl.BlockSpec((tm, d), lambda m: (m, 0)), pl.BlockSpec((tm, d), lambda m: (m, 0))]
    out_shape = [jax.ShapeDtypeStruct((t, d), F32), jax.ShapeDtypeStruct((t, d), BF16)]
    if moe:
        out_specs[1] = pl.BlockSpec((tm * SUBLANES, LANES), lambda m: (m, 0))
        out_shape[1] = jax.ShapeDtypeStruct((t * SUBLANES, LANES), F32)
    if moe:
        in_specs.append(pl.BlockSpec(wr.shape, full))
        args.append(wr)
        out_specs.append(pl.BlockSpec((tm, LANES), lambda m: (m, 0)))
        out_shape.append(jax.ShapeDtypeStruct((t, LANES), F32))
    return pl.pallas_call(
        functools.partial(_merge_body, moe=moe),
        grid=(t // tm,),
        in_specs=in_specs, out_specs=out_specs, out_shape=out_shape,
        compiler_params=_params("arbitrary"),
        name="merge_moe" if moe else "merge",
    )(*args)


def _swiglu_gate(x, w1_ref, w3_ref, g_ref, fc):
    dff = g_ref.shape[1]
    for f0 in range(0, dff, fc):
        a = jnp.dot(x, w1_ref[:, f0:f0 + fc], preferred_element_type=F32)
        b = jnp.dot(x, w3_ref[:, f0:f0 + fc], preferred_element_type=F32)
        g_ref[:, f0:f0 + fc] = (a * _sigmoid(a) * b).astype(BF16)


def _swiglu(x, w1_ref, w3_ref, w2_ref, g_ref, fc):
    _swiglu_gate(x, w1_ref, w3_ref, g_ref, fc)
    return jnp.dot(g_ref[...], w2_ref[...], preferred_element_type=F32)


def _ple(x, p_ref, g_ref, wg_ref, wp_ref):
    gate = _sigmoid(jnp.dot(_rms(x, g_ref[...]).astype(BF16), wg_ref[...], preferred_element_type=F32))
    emb = jnp.dot(p_ref[...].astype(BF16), wp_ref[...], preferred_element_type=F32)
    return x + gate * emb


def _ffn_body(hf_ref, x1_ref, w1_ref, w3_ref, w2_ref, p_ref, g_ref, wg_ref, wp_ref, o_ref, act_ref, *, fc):
    x2 = x1_ref[...] + _swiglu(hf_ref[...], w1_ref, w3_ref, w2_ref, act_ref, fc)
    o_ref[...] = _ple(x2, p_ref, g_ref, wg_ref, wp_ref)


def dense_ffn(hf, x1, w1, w3, w2, p2, p_row0, g, wg, wp, *, tm, fc):
    t, d = x1.shape
    dff = w1.shape[1]
    p_blk0 = p_row0 // tm
    resident = dict(pipeline_mode=pl.Buffered(1))
    full = lambda m: (0, 0)
    return pl.pallas_call(
        functools.partial(_ffn_body, fc=fc),
        grid=(t // tm,),
        in_specs=[
            pl.BlockSpec((tm, d), lambda m: (m, 0)),
            pl.BlockSpec((tm, d), lambda m: (m, 0)),
            pl.BlockSpec((d, dff), full, **resident),
            pl.BlockSpec((d, dff), full, **resident),
            pl.BlockSpec((dff, d), full, **resident),
            pl.BlockSpec((tm, p2.shape[1]), lambda m: (p_blk0 + m, 0)),
            pl.BlockSpec((1, d), full),
            pl.BlockSpec(wg.shape, full, **resident),
            pl.BlockSpec(wp.shape, full, **resident),
        ],
        out_specs=pl.BlockSpec((tm, d), lambda m: (m, 0)),
        out_shape=jax.ShapeDtypeStruct((t, d), F32),
        scratch_shapes=[pltpu.VMEM((tm, dff), BF16)],
        compiler_params=_params("arbitrary"),
        name="dense_ffn",
    )(hf, x1, w1, w3, w2, p2, g, wg, wp)


META_E0, META_E1, META_G0, META_G1, META_R0, META_R1 = 0, 1, 2, 3, 4, 5


def _route_body(lg_ref, meta_ref, tab_ref, cnt_ref, carry_ref, *, tm, ne):
    @pl.when(pl.program_id(0) == 0)
    def _():
        carry_ref[...] = jnp.zeros_like(carry_ref)

    lane = lax.broadcasted_iota(jnp.int32, (tm, LANES), 1)
    lanef = lane.astype(F32)
    lg = jnp.where(lane < ne, lg_ref[...], -jnp.inf)
    m1 = jnp.max(lg, axis=-1, keepdims=True)
    e1 = jnp.min(jnp.where(lg == m1, lanef, float(LANES)), axis=-1, keepdims=True)
    lg2 = jnp.where(lanef == e1, -jnp.inf, lg)
    m2 = jnp.max(lg2, axis=-1, keepdims=True)
    e2 = jnp.min(jnp.where(lg2 == m2, lanef, float(LANES)), axis=-1, keepdims=True)
    ex = jnp.exp(m2 - m1)
    g1 = 1.0 / (1.0 + ex)
    g2 = ex / (1.0 + ex)
    onehot = jnp.where((lanef == e1) | (lanef == e2), 1.0, 0.0)
    row = lax.broadcasted_iota(jnp.int32, (tm, tm), 0)
    col = lax.broadcasted_iota(jnp.int32, (tm, tm), 1)
    before = jnp.dot(jnp.where(col < row, 1.0, 0.0).astype(BF16), onehot.astype(BF16),
                     preferred_element_type=F32) + carry_ref[...]
    r1 = jnp.sum(jnp.where(lanef == e1, before, 0.0), axis=-1, keepdims=True)
    r2 = jnp.sum(jnp.where(lanef == e2, before, 0.0), axis=-1, keepdims=True)
    carry_ref[...] += jnp.sum(onehot, axis=0, keepdims=True)
    meta = jnp.zeros((tm, LANES), F32)
    for pos, val in ((META_E0, e1), (META_E1, e2), (META_G0, g1), (META_G1, g2), (META_R0, r1), (META_R1, r2)):
        meta = jnp.where(lane == pos, val, meta)
    meta_ref[...] = meta
    tab_ref[...] = meta.T[0:SUBLANES, :]
    cnt_ref[...] = jnp.broadcast_to(carry_ref[...], cnt_ref.shape)


def route(logits, *, tm):
    t = logits.shape[0]
    return pl.pallas_call(
        functools.partial(_route_body, tm=tm, ne=N_EXPERTS),
        grid=(t // tm,),
        in_specs=[pl.BlockSpec((tm, LANES), lambda m: (m, 0))],
        out_specs=[pl.BlockSpec((tm, LANES), lambda m: (m, 0)),
                   pl.BlockSpec((SUBLANES, tm), lambda m: (0, m)),
                   pl.BlockSpec((SUBLANES, LANES), lambda m: (0, 0))],
        out_shape=[jax.ShapeDtypeStruct((t, LANES), F32), jax.ShapeDtypeStruct((SUBLANES, t), F32),
                   jax.ShapeDtypeStruct((SUBLANES, LANES), F32)],
        scratch_shapes=[pltpu.VMEM((1, LANES), F32)],
        compiler_params=_params("arbitrary"),
        name="route",
    )(logits)


def _dispatch_body(dest_ref, pe_ref, na_ref, hf_ref, xs_ref, zero_ref, sem, zsem, *, tm, topk, tm_rows, n_tok):
    base = pl.program_id(0) * tm

    @pl.when(pl.program_id(0) == 0)
    def _():
        zero_ref[...] = jnp.zeros_like(zero_ref)
        n_tiles = xs_ref.shape[0] // (tm_rows * SUBLANES)
        fills = []
        for e in range(N_EXPERTS):
            end = pe_ref[e]
            nonempty = end > (pe_ref[e - 1] if e else 0)
            fills.append((nonempty, pl.multiple_of(jnp.maximum(end - tm_rows, 0), tm_rows)))
        for tile in range(n_tiles):
            fills.append((tile >= na_ref[0], tile * tm_rows))
        for phase in ("start", "wait"):
            for cond, row0 in fills:
                @pl.when(cond)
                def _(row0=row0, phase=phase):
                    cp = pltpu.make_async_copy(
                        zero_ref, xs_ref.at[pl.ds(row0 * SUBLANES, tm_rows * SUBLANES)], zsem)
                    cp.start() if phase == "start" else cp.wait()

    def issue(r, c):
        for k in range(topk):
            d = dest_ref[k * n_tok + base + r]
            pltpu.make_async_copy(_row_tile(hf_ref, r), _row_tile(xs_ref, d), sem).start(priority=k % 2)
        return c

    lax.fori_loop(0, tm, issue, 0, unroll=ROW_DMA_UNROLL)
    for k in range(topk):
        pltpu.make_async_copy(hf_ref, xs_ref.at[pl.ds(0, tm * SUBLANES)], sem).wait()


def _row_tile(ref, r):
    return ref.at[pl.ds(pl.multiple_of(r * SUBLANES, SUBLANES), SUBLANES)]


def dispatch(dest, pad_end, n_active, hf, *, n_rows, tm, topk, tm_rows):
    t = hf.shape[0] // SUBLANES
    return pl.pallas_call(
        functools.partial(_dispatch_body, tm=tm, topk=topk, tm_rows=tm_rows, n_tok=t),
        grid_spec=pltpu.PrefetchScalarGridSpec(
            num_scalar_prefetch=3,
            grid=(t // tm,),
            in_specs=[pl.BlockSpec((tm * SUBLANES, LANES), lambda m, *_: (m, 0))],
            out_specs=pl.BlockSpec(memory_space=pl.ANY),
            scratch_shapes=[pltpu.VMEM((tm_rows * SUBLANES, LANES), F32),
                            pltpu.SemaphoreType.DMA, pltpu.SemaphoreType.DMA],
        ),
        out_shape=jax.ShapeDtypeStruct((n_rows * SUBLANES, LANES), F32),
        compiler_params=_params("arbitrary"),
        name="moe_dispatch",
    )(dest, pad_end, n_active, hf)


def _experts_body(te_ref, na_ref, xs_ref, xs_next_ref, w1_ref, w3_ref, w2_ref, y_ref, xb_ref, g_ref, acc_ref,
                  *, fc):
    del te_ref
    i = pl.program_id(0)
    f = pl.program_id(1)
    last = pl.num_programs(1) - 1
    tm = xb_ref.shape[0]

    @pl.when(i >= na_ref[0])
    def _():
        y_ref[...] = jnp.zeros_like(y_ref)

    @pl.when(i < na_ref[0])
    def _():
        @pl.when((i == 0) & (f == 0))
        def _():
            xb_ref[...] = _load_row_tiles(xs_ref, tm).astype(BF16)

        _swiglu_gate(xb_ref[...], w1_ref.at[0], w3_ref.at[0], g_ref, fc)

        @pl.when(f < last)
        def _():
            y = jnp.dot(g_ref[...], w2_ref[0], preferred_element_type=F32)

            @pl.when(f == 0)
            def _():
                acc_ref[...] = y

            @pl.when(f > 0)
            def _():
                acc_ref[...] += y

        @pl.when(f == last)
        def _():
            xb_ref[...] = _load_row_tiles(xs_next_ref, tm).astype(BF16)
            y = jnp.dot(g_ref[...], w2_ref[0], preferred_element_type=F32)
            _store_row_tiles(y_ref, acc_ref[...] + y)


def experts(tile_expert, n_active, xs, w1, w3, w2, *, tm, tf, fc):
    n_rows = xs.shape[0] // SUBLANES
    d = w1.shape[1]
    dff = w1.shape[2]
    assert dff // tf >= 2
    row_tile = lambda i, f, te, na: (jnp.minimum(i, na[0] - 1), 0)
    ftile = lambda i, f, na: jnp.where(i < na[0], f, dff // tf - 1)
    return pl.pallas_call(
        functools.partial(_experts_body, fc=fc),
        grid_spec=pltpu.PrefetchScalarGridSpec(
            num_scalar_prefetch=2,
            grid=(n_rows // tm, dff // tf),
            in_specs=[
                pl.BlockSpec((tm * SUBLANES, LANES), row_tile),
                pl.BlockSpec((tm * SUBLANES, LANES), lambda i, f, te, na: (jnp.minimum(i + 1, na[0] - 1), 0)),
                pl.BlockSpec((1, d, tf), lambda i, f, te, na: (te[i], 0, ftile(i, f, na))),
                pl.BlockSpec((1, d, tf), lambda i, f, te, na: (te[i], 0, ftile(i, f, na))),
                pl.BlockSpec((1, tf, d), lambda i, f, te, na: (te[i], ftile(i, f, na), 0)),
            ],
            out_specs=pl.BlockSpec((tm * SUBLANES, LANES), lambda i, f, te, na: (i, 0)),
            scratch_shapes=[pltpu.VMEM((tm, d), BF16), pltpu.VMEM((tm, tf), BF16), pltpu.VMEM((tm, d), F32)],
        ),
        out_shape=jax.ShapeDtypeStruct((n_rows * SUBLANES, LANES), F32),
        compiler_params=_params("arbitrary", "arbitrary"),
        name="moe_experts",
    )(tile_expert, n_active, xs, xs, w1, w3, w2)


def _combine_body(dest_ref, x1_ref, meta_ref, p_ref, g_ref, wg_ref, wp_ref, y_ref, o_ref, buf_ref, sem,
                  *, tm, topk):
    base = pl.program_id(0) * tm
    n_tok = pl.num_programs(0) * tm

    def issue(r, c):
        for k in range(topk):
            d = dest_ref[k * n_tok + base + r]
            pltpu.make_async_copy(_row_tile(y_ref, d), _row_tile(buf_ref.at[k], r), sem).start(priority=k % 2)
        return c

    lax.fori_loop(0, tm, issue, 0, unroll=ROW_DMA_UNROLL)
    for k in range(topk):
        pltpu.make_async_copy(y_ref.at[pl.ds(0, tm * SUBLANES)], buf_ref.at[k], sem).wait()
    meta = meta_ref[...]
    g0 = meta[:, META_G0:META_G0 + 1]
    g1 = meta[:, META_G1:META_G1 + 1]
    x2 = x1_ref[...] + (g0 * _load_row_tiles(buf_ref.at[0], tm) + g1 * _load_row_tiles(buf_ref.at[1], tm))
    o_ref[...] = _ple(x2, p_ref, g_ref, wg_ref, wp_ref)


def combine(dest, x1, meta, y, p2, p_row0, g, wg, wp, *, tm, topk):
    t, d = x1.shape
    full = lambda m, dest: (0, 0)
    p_blk0 = p_row0 // tm
    return pl.pallas_call(
        functools.partial(_combine_body, tm=tm, topk=topk),
        grid_spec=pltpu.PrefetchScalarGridSpec(
            num_scalar_prefetch=1,
            grid=(t // tm,),
            in_specs=[pl.BlockSpec((tm, d), lambda m, dest: (m, 0)),
                      pl.BlockSpec((tm, LANES), lambda m, dest: (m, 0)),
                      pl.BlockSpec((tm, p2.shape[1]), lambda m, dest: (p_blk0 + m, 0)),
                      pl.BlockSpec((1, d), full), pl.BlockSpec(wg.shape, full), pl.BlockSpec(wp.shape, full),
                      pl.BlockSpec(memory_space=pl.ANY)],
            out_specs=pl.BlockSpec((tm, d), lambda m, dest: (m, 0)),
            scratch_shapes=[pltpu.VMEM((topk, tm * SUBLANES, LANES), F32), pltpu.SemaphoreType.DMA],
        ),
        out_shape=jax.ShapeDtypeStruct((t, d), F32),
        compiler_params=_params("arbitrary"),
        name="moe_combine",
    )(dest, x1, meta, p2, g, wg, wp, y)


def moe_ffn(hf, x1, logits, w1, w3, w2, ple_args, *, tm_route, tm_rows, tf, tm_move):
    t, d = x1.shape
    topk = 2
    meta, tab, cnt = route(logits, tm=tm_route)
    counts = cnt[0, :N_EXPERTS].astype(jnp.int32)
    padded = ((counts + tm_rows - 1) // tm_rows) * tm_rows
    pad_end = jnp.cumsum(padded).astype(jnp.int32)
    pad_start = pad_end - padded
    eidx = tab[META_E0:META_E1 + 1].astype(jnp.int32)
    rank = tab[META_R0:META_R1 + 1].astype(jnp.int32)
    dest = rank
    for e in range(N_EXPERTS):
        dest = dest + jnp.where(eidx == e, pad_start[e], 0)
    dest = dest.reshape(topk * t)
    n_tiles = -(-(t * topk) // tm_rows) + N_EXPERTS
    tile_start = jnp.arange(n_tiles, dtype=jnp.int32) * tm_rows
    tile_expert = jnp.minimum(jnp.sum(tile_start[:, None] >= pad_end[None, :], axis=1),
                              N_EXPERTS - 1).astype(jnp.int32)
    n_active = pad_end[N_EXPERTS - 1:] // tm_rows
    xs = dispatch(dest, pad_end, n_active, hf, n_rows=n_tiles * tm_rows, tm=tm_move, topk=topk, tm_rows=tm_rows)
    y = experts(tile_expert, n_active, xs, w1, w3, w2, tm=tm_rows, tf=tf, fc=256)
    return combine(dest, x1, meta, y, *ple_args, tm=tm_move, topk=topk)


def _tile2(g):
    return jnp.concatenate([g, g]).reshape(1, 2 * g.shape[0])


def kernel(x, p, g_mix, w_in, g_q, g_k, conv_w, conv_b, b_i, b_f, g_h, w_oa, w_ob, w_out, g_ffn, w_d1, w_d3,
           w_d2, w_router, w_e1, w_e3, w_e2, g_ple, w_ple_gate, w_ple_proj):
    batch, seq, d = x.shape
    depth = w_in.shape[0]
    t = batch * seq
    nh = MLSTM_HEADS
    x2 = x.reshape(t, d)
    c_q, c_k, c_v = 0, 512, 1024
    c_qk, c_vm, c_om, c_i, c_f, c_ga, c_gb, c_end = 1536, 2560, 3072, 3584, 3588, 3592, 4616, 5640

    w_in_t = jnp.swapaxes(w_in, 1, 2).astype(BF16)
    for l in range(depth):
        w_bf = w_in_t[l]
        w_gates = w_bf[c_ga:c_end]
        proj, gif = in_proj(x2, g_mix[l].reshape(1, d), w_gates, w_bf, nb_cols=c_i, if_col=c_i,
                            tm=1024, tn=512)

        ya = moba(proj, _tile2(g_q[l]), _tile2(g_k[l]), batch=batch, seq=seq)

        bias = jnp.concatenate([b_i[l], b_f[l]])
        bias_row = jnp.pad(bias, (0, LANES - 2 * nh)).reshape(1, LANES)
        bias_col = bias.reshape(2 * nh, 1)
        gates_row = gif[:, :2 * nh].reshape(batch, seq, 2 * nh).transpose(0, 2, 1)
        yb = mlstm(proj, gif, gates_row, bias_row, bias_col, conv_w[l], conv_b[l].reshape(1, -1),
                   g_h[l].reshape(1, -1), batch=batch, seq=seq, chunk=256)

        j = l // 2
        moe = l % 2 == 1
        wr = jnp.pad(w_router[j], ((0, 0), (0, LANES - N_EXPERTS))) if moe else None
        outs = merge(ya, yb, proj, x2, w_oa[l].astype(BF16), w_ob[l].astype(BF16), w_out[l].astype(BF16),
                     g_ffn[l].reshape(1, d), wr, tm=1024)
        ple_args = (p.reshape(depth * t, -1), l * t, g_ple[l].reshape(1, d), w_ple_gate[l].astype(BF16),
                    w_ple_proj[l].astype(BF16))
        if moe:
            x1, hf, logits = outs
            x2 = moe_ffn(hf, x1, logits, w_e1[j].astype(BF16), w_e3[j].astype(BF16), w_e2[j].astype(BF16),
                         ple_args, tm_route=512, tm_rows=512, tf=1792, tm_move=512)
        else:
            x1, hf = outs
            x2 = dense_ffn(hf, x1, w_d1[j].astype(BF16), w_d3[j].astype(BF16), w_d2[j].astype(BF16),
                           *ple_args, tm=1024, fc=256)
    return x2.reshape(batch, seq, d)
```

```python
import functools

import jax
import jax.numpy as jnp
from jax import lax
from jax.experimental import pallas as pl
from jax.experimental.pallas import tpu as pltpu

F32 = jnp.float32
BF16 = jnp.bfloat16
HIGHEST = lax.Precision.HIGHEST

RMS_EPS = 1e-6
LANES = 128
SUBLANES = 8

MOBA_HEADS = 8
MOBA_HEAD_DIM = 64
MOBA_BLOCK = 256
MOBA_TOPK = 3
MLSTM_HEADS = 4
MLSTM_DIM = 128
CONV_WIDTH = 4
N_EXPERTS = 8

COL_GA, COL_GB = 0, 8
COL_QA, COL_KA, COL_VA = 16, 20, 24
COL_QM, COL_KM, COL_VM, COL_OM = 28, 32, 36, 40
N_PROJ = 44 * LANES

VMEM_LIMIT = 56 * 1024 * 1024
ROW_DMA_UNROLL = 8


def _params(*sem):
    return pltpu.CompilerParams(dimension_semantics=sem, vmem_limit_bytes=VMEM_LIMIT)


def _sigmoid(x):
    return 1.0 / (1.0 + jnp.exp(-x))


def _rms(x, g):
    return x * lax.rsqrt(jnp.mean(x * x, axis=-1, keepdims=True) + RMS_EPS) * g


def _split_bf16(x):
    hi = x.astype(BF16)
    return hi, (x - hi.astype(F32)).astype(BF16)


def _dot_split(a, b):
    ah, al = _split_bf16(a)
    bh, bl = _split_bf16(b)
    return (jnp.dot(ah, bh, preferred_element_type=F32) + jnp.dot(ah, bl, preferred_element_type=F32)
            + jnp.dot(al, bh, preferred_element_type=F32))


def _store_row_tiles(ref, x):
    ref[...] = x.reshape(x.shape[0] * SUBLANES, LANES)


def _load_row_tiles(ref, rows):
    return ref[...].reshape(rows, SUBLANES * LANES)


def _nt_dot(a, b, **kw):
    return lax.dot_general(a, b, (((1,), (1,)), ((), ())), preferred_element_type=F32, **kw)


def _in_proj_body(x_ref, g_ref, wa_ref, wb_ref, o_ref, oif_ref, h_ref, *, nb_cols, if_col, tn):
    h_ref[...] = _rms(x_ref[...], g_ref[...]).astype(BF16)
    oif_ref[...] = _nt_dot(h_ref[...], wb_ref[if_col:if_col + LANES, :])
    na = wa_ref.shape[0]
    for c0 in range(0, na + nb_cols, tn):
        w = wa_ref[c0:c0 + tn, :] if c0 < na else wb_ref[c0 - na:c0 - na + tn, :]
        o_ref[:, c0:c0 + tn] = _nt_dot(h_ref[...], w).astype(o_ref.dtype)


def in_proj(x2, g, wa, w_full, *, nb_cols, if_col, tm, tn):
    t, d = x2.shape
    n = wa.shape[0] + nb_cols
    resident = dict(pipeline_mode=pl.Buffered(1))
    return pl.pallas_call(
        functools.partial(_in_proj_body, nb_cols=nb_cols, if_col=if_col, tn=tn),
        grid=(t // tm,),
        in_specs=[
            pl.BlockSpec((tm, d), lambda m: (m, 0)),
            pl.BlockSpec((1, d), lambda m: (0, 0)),
            pl.BlockSpec(wa.shape, lambda m: (0, 0), **resident),
            pl.BlockSpec(w_full.shape, lambda m: (0, 0), **resident),
        ],
        out_specs=[
            pl.BlockSpec((tm, n), lambda m: (m, 0)),
            pl.BlockSpec((tm, LANES), lambda m: (m, 0)),
        ],
        out_shape=[jax.ShapeDtypeStruct((t, n), BF16), jax.ShapeDtypeStruct((t, LANES), F32)],
        scratch_shapes=[pltpu.VMEM((tm, d), BF16)],
        compiler_params=_params("arbitrary"),
        name="in_proj",
    )(x2, g, wa, w_full)


MASK_BIAS = -1e30
LOG2_E = 1.4426950408889634


def _moba_body(q_ref, k_ref, v_ref, gq_ref, gk_ref, o_ref,
               kn_ref, vt_ref, kmean_ref, qaug_ref, s_ref, m_ref, alpha_ref, acc_ref,
               *, nb, blk, dh, topk, nheads):
    i = pl.program_id(1)
    pair = 2 * blk
    lane = lax.broadcasted_iota(jnp.int32, (1, LANES), 1)
    head0 = lane < dh

    same_head = (lax.broadcasted_iota(jnp.int32, (LANES, LANES), 0) // dh
                 == lax.broadcasted_iota(jnp.int32, (LANES, LANES), 1) // dh)
    head_ones = jnp.where(same_head, 1.0, 0.0).astype(BF16)

    def head_rms(x, g, on_mxu):
        x2 = x * x
        if on_mxu:
            hi, lo = _split_bf16(x2)
            ss = (jnp.dot(hi, head_ones, preferred_element_type=F32)
                  + jnp.dot(lo, head_ones, preferred_element_type=F32))
        else:
            s0 = jnp.sum(jnp.where(head0, x2, 0.0), axis=-1, keepdims=True)
            s1 = jnp.sum(jnp.where(head0, 0.0, x2), axis=-1, keepdims=True)
            ss = jnp.where(head0, s0, s1)
        return x * lax.rsqrt(ss * (1.0 / dh) + RMS_EPS) * g

    @pl.when(i == 0)
    def _():
        def prep(j, c):
            r0 = pl.multiple_of(j * blk, blk)
            onehot = jnp.where(lane == dh + j, 1.0, 0.0)
            for p in range(nheads // 2):
                cols = slice(p * LANES, (p + 1) * LANES)
                kn = head_rms(k_ref[pl.ds(r0, blk), cols].astype(F32), gk_ref[...], True)
                for hh, kh in ((0, kn), (1, pltpu.roll(kn, dh, axis=1))):
                    h = 2 * p + hh
                    kmean_ref[h, pl.ds(j, 1), :] = jnp.mean(jnp.where(head0, kh, 0.0), axis=0, keepdims=True)
                    kn_ref[h, pl.ds(r0, blk), :] = jnp.where(head0, kh, onehot).astype(BF16)
                v_t = v_ref[pl.ds(r0, blk), cols].astype(F32).T.astype(BF16)
                for hh in range(2):
                    vt_ref[2 * p + hh, 0:dh, pl.ds(r0, blk)] = v_t[hh * dh:(hh + 1) * dh, :]
                    vt_ref[2 * p + hh, dh:, pl.ds(r0, blk)] = jnp.ones((vt_ref.shape[1] - dh, blk), BF16)
            return c

        lax.fori_loop(0, nb, prep, 0)

    jidx = lax.broadcasted_iota(jnp.int32, (nb, blk), 0)
    key_i = lax.broadcasted_iota(jnp.int32, (blk, blk), 0)
    qry_i = lax.broadcasted_iota(jnp.int32, (blk, blk), 1)
    causal = key_i <= qry_i
    r_own = pl.multiple_of(i * blk, blk)
    qk_scale = dh ** -0.5 * LOG2_E
    for p in range(nheads // 2):
        cols = slice(p * LANES, (p + 1) * LANES)
        qn_t = head_rms(q_ref[:, cols].astype(F32), gq_ref[...], False).T
        for hh in range(2):
            h = 2 * p + hh
            q_t = qn_t[hh * dh:(hh + 1) * dh, :]
            gate = jnp.dot(kmean_ref[h], jnp.concatenate([q_t, jnp.zeros((LANES - dh, blk), F32)], axis=0),
                           precision=HIGHEST, preferred_element_type=F32)
            rank = jnp.zeros((nb, blk), F32)
            for jp in range(nb):
                row = gate[jp:jp + 1, :]
                beats = (row > gate) | ((row == gate) & (jidx > jp))
                rank = rank + jnp.where(beats, jnp.where(jp < i, 1.0, 0.0), 0.0)
            sel = (rank < topk) & (jidx < i)
            q_s = q_t * qk_scale
            pad = jnp.zeros((LANES - dh - nb, blk), F32)
            qaug_ref[h] = jnp.concatenate([q_s, jnp.where(sel, 0.0, MASK_BIAS), pad], axis=0).astype(BF16)
            qaug_own = jnp.concatenate([q_s, jnp.where(jidx == i, 0.0, MASK_BIAS), pad], axis=0).astype(BF16)
            st = jnp.dot(kn_ref[h, pl.ds(r_own, blk), :], qaug_own, preferred_element_type=F32)
            st = jnp.where(causal, st, -jnp.inf)
            s_ref[h, 0:blk, :] = st
            m_ref[h] = jnp.max(st, axis=0, keepdims=True)

    def finish_own(h):
        pr = jnp.exp2(s_ref[h, 0:blk, :] - m_ref[h]).astype(BF16)
        acc_ref[h] = jnp.dot(vt_ref[h, :, pl.ds(r_own, blk)], pr, preferred_element_type=F32)

    def score_pair(u, h):
        r0 = pl.multiple_of(u * pair, pair)
        st = jnp.dot(kn_ref[h, pl.ds(r0, pair), :], qaug_ref[h], preferred_element_type=F32)
        m_old = m_ref[h]
        m_new = jnp.maximum(m_old, jnp.max(st, axis=0, keepdims=True))
        s_ref[h] = st
        alpha_ref[h] = jnp.exp2(m_old - m_new)
        m_ref[h] = m_new

    def finish_pair(u, h):
        r0 = pl.multiple_of(u * pair, pair)
        pr = jnp.exp2(s_ref[h] - m_ref[h]).astype(BF16)
        acc_ref[h] = alpha_ref[h] * acc_ref[h] + jnp.dot(vt_ref[h, :, pl.ds(r0, pair)], pr,
                                                         preferred_element_type=F32)

    n_pairs = jnp.maximum((i + 1) // 2, 1)
    for h in range(nheads):
        finish_own(h)
        score_pair(0, h)

    def body(u, c):
        for h in range(nheads):
            finish_pair(u - 1, h)
            score_pair(u, h)
        return c

    lax.fori_loop(1, n_pairs, body, 0)
    for h in range(nheads):
        finish_pair(n_pairs - 1, h)

    for p in range(nheads // 2):
        a0 = acc_ref[2 * p]
        a1 = acc_ref[2 * p + 1]
        ot = jnp.concatenate([a0[0:dh] / a0[dh:dh + 1], a1[0:dh] / a1[dh:dh + 1]], axis=0)
        o_ref[:, p * LANES:(p + 1) * LANES] = ot.T.astype(o_ref.dtype)


def moba(proj, gq2, gk2, *, batch, seq):
    nb = seq // MOBA_BLOCK
    blk = MOBA_BLOCK
    dh = MOBA_HEAD_DIM
    nheads = MOBA_HEADS
    width = nheads * dh
    wb = width // LANES
    assert dh + nb <= LANES and 2 * dh == LANES and nb % 2 == 0
    v_rows = dh + 2 * SUBLANES
    body = functools.partial(_moba_body, nb=nb, blk=blk, dh=dh, topk=MOBA_TOPK, nheads=nheads)
    return pl.pallas_call(
        body,
        grid=(batch, nb),
        in_specs=[
            pl.BlockSpec((blk, width), lambda b, i: (b * nb + i, COL_QA // wb)),
            pl.BlockSpec((seq, width), lambda b, i: (b, COL_KA // wb)),
            pl.BlockSpec((seq, width), lambda b, i: (b, COL_VA // wb)),
            pl.BlockSpec((1, LANES), lambda b, i: (0, 0)),
            pl.BlockSpec((1, LANES), lambda b, i: (0, 0)),
        ],
        out_specs=pl.BlockSpec((blk, width), lambda b, i: (b * nb + i, 0)),
        out_shape=jax.ShapeDtypeStruct((batch * seq, width), BF16),
        scratch_shapes=[
            pltpu.VMEM((nheads, seq, LANES), BF16),
            pltpu.VMEM((nheads, v_rows, seq), BF16),
            pltpu.VMEM((nheads, nb, LANES), F32),
            pltpu.VMEM((nheads, LANES, blk), BF16),
            pltpu.VMEM((nheads, 2 * blk, blk), F32),
            pltpu.VMEM((nheads, 1, blk), F32),
            pltpu.VMEM((nheads, 1, blk), F32),
            pltpu.VMEM((nheads, v_rows, blk), F32),
        ],
        compiler_params=_params("arbitrary", "arbitrary"),
        name="moba",
    )(proj, proj, proj, gq2, gk2)


def _log_sigmoid(x):
    return jnp.minimum(x, 0.0) - jnp.log(1.0 + jnp.exp(-jnp.abs(x)))


def _dot_tri(tri, x, tri_left):
    out = None
    for _ in range(3):
        piece = x.astype(BF16)
        x = x - piece.astype(F32)
        term = (jnp.dot(tri, piece, preferred_element_type=F32) if tri_left
                else jnp.dot(piece, tri, preferred_element_type=F32))
        out = term if out is None else out + term
    return out


def _mlstm_body(qr_ref, kr_ref, v_ref, og_ref, gcol_ref, grow_ref, brow_ref, bcol_ref,
                cwq_ref, cwk_ref, cbq_ref, cbk_ref, gh_ref, o_ref,
                qx_ref, kx_ref, c_ref, m_ref, *, chunk, dk, nh):
    L = chunk
    width = nh * dk

    @pl.when(pl.program_id(1) == 0)
    def _():
        qx_ref[0:SUBLANES, :] = jnp.zeros((SUBLANES, width), F32)
        kx_ref[0:SUBLANES, :] = jnp.zeros((SUBLANES, width), F32)
        c_ref[...] = jnp.zeros_like(c_ref)
        m_ref[...] = jnp.zeros_like(m_ref)

    qx_ref[SUBLANES:SUBLANES + L, :] = qr_ref[...].astype(F32)
    kx_ref[SUBLANES:SUBLANES + L, :] = kr_ref[...].astype(F32)

    def conv_silu(x_ref, w_ref, b_ref):
        acc = b_ref[...] + w_ref[0:1, :] * x_ref[pl.ds(SUBLANES - CONV_WIDTH + 1, L), :]
        for j in range(1, CONV_WIDTH):
            acc = acc + w_ref[j:j + 1, :] * x_ref[pl.ds(SUBLANES - CONV_WIDTH + 1 + j, L), :]
        return acc * _sigmoid(acc)

    q_all = conv_silu(qx_ref, cwq_ref, cbq_ref)
    k_all = conv_silu(kx_ref, cwk_ref, cbk_ref) * (dk ** -0.5)
    qx_ref[0:SUBLANES, :] = qx_ref[L:L + SUBLANES, :]
    kx_ref[0:SUBLANES, :] = kx_ref[L:L + SUBLANES, :]

    pre_col = gcol_ref[...] + brow_ref[...]
    pre_row = grow_ref[0] + bcol_ref[...]
    t_i = lax.broadcasted_iota(jnp.int32, (L, L), 0)
    s_i = lax.broadcasted_iota(jnp.int32, (L, L), 1)
    tril = s_i <= t_i
    bcum_cols = _dot_tri(jnp.where(tril, 1.0, 0.0).astype(BF16), _log_sigmoid(pre_col), True)
    bcum_rows = _dot_tri(jnp.where(t_i <= s_i, 1.0, 0.0).astype(BF16), _log_sigmoid(pre_row), False)
    ones = jnp.ones((L, dk), BF16)

    for h in range(nh):
        cols = slice(h * dk, (h + 1) * dk)
        q = q_all[:, cols]
        k = k_all[:, cols]
        i_col = pre_col[:, h:h + 1]
        i_row = pre_row[h:h + 1, :]
        bcum_col = bcum_cols[:, nh + h:nh + h + 1]
        bcum_row = bcum_rows[nh + h:nh + h + 1, :]

        m_prev = m_ref[h, 0:1, 0:1]
        a_col = bcum_col + m_prev
        dmat = jnp.where(tril, bcum_col - bcum_row + i_row, -jnp.inf)
        m_t = jnp.maximum(a_col, jnp.max(dmat, axis=-1, keepdims=True))
        dw = jnp.exp(dmat - m_t)
        aw = jnp.exp(a_col - m_t)

        qb = q.astype(BF16)
        kb = k.astype(BF16)
        v_aug = jnp.concatenate([v_ref[:, cols], ones], axis=-1)
        sqk = _nt_dot(qb, kb) * dw
        num_aug = (aw * jnp.dot(qb, c_ref[h].astype(BF16), preferred_element_type=F32)
                   + jnp.dot(sqk.astype(BF16), v_aug, preferred_element_type=F32))
        den = num_aug[:, dk:dk + 1]
        hc = num_aug[:, 0:dk] / jnp.maximum(jnp.abs(den), jnp.exp(-m_t))

        b_last = bcum_col[L - 1:L, :]
        g_col = b_last - bcum_col + i_col
        m_new = jnp.maximum(b_last + m_prev, jnp.max(g_col, axis=0, keepdims=True))
        w_c = jnp.exp(b_last + m_prev - m_new)
        kw_t = (k * jnp.exp(g_col - m_new)).T.astype(BF16)
        c_ref[h] = w_c * c_ref[h] + jnp.dot(kw_t, v_aug, preferred_element_type=F32)
        m_ref[h] = jnp.broadcast_to(m_new, (1, LANES))

        o_ref[:, cols] = (_rms(hc, gh_ref[...]) * _sigmoid(og_ref[:, cols].astype(F32))).astype(o_ref.dtype)


def mlstm(proj, gates_col, gates_row, bias_row, bias_col, conv_w, conv_b, gh, *, batch, seq, chunk):
    nh = MLSTM_HEADS
    dk = MLSTM_DIM
    width = nh * dk
    wb = width // LANES
    nc = seq // chunk
    body = functools.partial(_mlstm_body, chunk=chunk, dk=dk, nh=nh)

    def rows(col0):
        return pl.BlockSpec((chunk, width), lambda b, c: (b * nc + c, col0 // wb))

    return pl.pallas_call(
        body,
        grid=(batch, nc),
        in_specs=[
            rows(COL_QM), rows(COL_KM), rows(COL_VM), rows(COL_OM),
            pl.BlockSpec((chunk, LANES), lambda b, c: (b * nc + c, 0)),
            pl.BlockSpec((1, SUBLANES, chunk), lambda b, c: (b, 0, c)),
            pl.BlockSpec((1, LANES), lambda b, c: (0, 0)),
            pl.BlockSpec((SUBLANES, 1), lambda b, c: (0, 0)),
            pl.BlockSpec((CONV_WIDTH, width), lambda b, c: (0, 0)),
            pl.BlockSpec((CONV_WIDTH, width), lambda b, c: (0, 1)),
            pl.BlockSpec((1, width), lambda b, c: (0, 0)),
            pl.BlockSpec((1, width), lambda b, c: (0, 1)),
            pl.BlockSpec((1, LANES), lambda b, c: (0, 0)),
        ],
        out_specs=pl.BlockSpec((chunk, width), lambda b, c: (b * nc + c, 0)),
        out_shape=jax.ShapeDtypeStruct((batch * seq, width), BF16),
        scratch_shapes=[
            pltpu.VMEM((chunk + 2 * SUBLANES, width), F32),
            pltpu.VMEM((chunk + 2 * SUBLANES, width), F32),
            pltpu.VMEM((nh, dk, 2 * dk), F32),
            pltpu.VMEM((nh, 1, LANES), F32),
        ],
        compiler_params=_params("arbitrary", "arbitrary"),
        name="mlstm",
    )(proj, proj, proj, proj, gates_col, gates_row, bias_row, bias_col,
      conv_w, conv_w, conv_b, conv_b, gh)


def _merge_body(ya_ref, yb_ref, ga_ref, gb_ref, x_ref, woa_ref, wob_ref, wout_ref, gffn_ref, *rest, moe):
    a = jnp.dot(ya_ref[...], woa_ref[...], preferred_element_type=F32)
    b = jnp.dot(yb_ref[...], wob_ref[...], preferred_element_type=F32)
    mixed = _sigmoid(ga_ref[...].astype(F32)) * a + _sigmoid(gb_ref[...].astype(F32)) * b
    x1 = x_ref[...] + jnp.dot(mixed.astype(BF16), wout_ref[...], preferred_element_type=F32)
    hf = _rms(x1, gffn_ref[...])
    if moe:
        wr_ref, x1_ref, hf_ref, lg_ref = rest
        _store_row_tiles(hf_ref, hf)
        lg_ref[...] = _dot_split(hf, wr_ref[...])
    else:
        x1_ref, hf_ref = rest
        hf_ref[...] = hf.astype(BF16)
    x1_ref[...] = x1


def merge(ya, yb, proj, x2, woa, wob, wout, gffn, wr, *, tm):
    t, d = x2.shape
    moe = wr is not None
    full = lambda m: (0, 0)
    in_specs = [
        pl.BlockSpec((tm, ya.shape[1]), lambda m: (m, 0)),
        pl.BlockSpec((tm, yb.shape[1]), lambda m: (m, 0)),
        pl.BlockSpec((tm, d), lambda m: (m, COL_GA * LANES // d)),
        pl.BlockSpec((tm, d), lambda m: (m, COL_GB * LANES // d)),
        pl.BlockSpec((tm, d), lambda m: (m, 0)),
        pl.BlockSpec(woa.shape, full), pl.BlockSpec(wob.shape, full), pl.BlockSpec(wout.shape, full),
        pl.BlockSpec((1, d), full),
    ]
    args = [ya, yb, proj, proj, x2, woa, wob, wout, gffn]
    out_specs = [pl.BlockSpec((tm, d), lambda m: (m, 0)), pl.BlockSpec((tm, d), lambda m: (m, 0))]
    out_shape = [jax.ShapeDtypeStruct((t, d), F32), jax.ShapeDtypeStruct((t, d), BF16)]
    if moe:
        out_specs[1] = pl.BlockSpec((tm * SUBLANES, LANES), lambda m: (m, 0))
        out_shape[1] = jax.ShapeDtypeStruct((t * SUBLANES, LANES), F32)
    if moe:
        in_specs.append(pl.BlockSpec(wr.shape, full))
        args.append(wr)
        out_specs.append(pl.BlockSpec((tm, LANES), lambda m: (m, 0)))
        out_shape.append(jax.ShapeDtypeStruct((t, LANES), F32))
    return pl.pallas_call(
        functools.partial(_merge_body, moe=moe),
        grid=(t // tm,),
        in_specs=in_specs, out_specs=out_specs, out_shape=out_shape,
        compiler_params=_params("arbitrary"),
        name="merge_moe" if moe else "merge",
    )(*args)


def _swiglu_gate(x, w1_ref, w3_ref, g_ref, fc):
    dff = g_ref.shape[1]
    for f0 in range(0, dff, fc):
        a = jnp.dot(x, w1_ref[:, f0:f0 + fc], preferred_element_type=F32)
        b = jnp.dot(x, w3_ref[:, f0:f0 + fc], preferred_element_type=F32)
        g_ref[:, f0:f0 + fc] = (a * _sigmoid(a) * b).astype(BF16)


def _swiglu(x, w1_ref, w3_ref, w2_ref, g_ref, fc):
    _swiglu_gate(x, w1_ref, w3_ref, g_ref, fc)
    return jnp.dot(g_ref[...], w2_ref[...], preferred_element_type=F32)


def _ple(x, p_ref, g_ref, wg_ref, wp_ref):
    gate = _sigmoid(jnp.dot(_rms(x, g_ref[...]).astype(BF16), wg_ref[...], preferred_element_type=F32))
    emb = jnp.dot(p_ref[...].astype(BF16), wp_ref[...], preferred_element_type=F32)
    return x + gate * emb


def _ffn_body(hf_ref, x1_ref, w1_ref, w3_ref, w2_ref, p_ref, g_ref, wg_ref, wp_ref, o_ref, act_ref, *, fc):
    x2 = x1_ref[...] + _swiglu(hf_ref[...], w1_ref, w3_ref, w2_ref, act_ref, fc)
    o_ref[...] = _ple(x2, p_ref, g_ref, wg_ref, wp_ref)


def dense_ffn(hf, x1, w1, w3, w2, p2, p_row0, g, wg, wp, *, tm, fc):
    t, d = x1.shape
    dff = w1.shape[1]
    p_blk0 = p_row0 // tm
    resident = dict(pipeline_mode=pl.Buffered(1))
    full = lambda m: (0, 0)
    return pl.pallas_call(
        functools.partial(_ffn_body, fc=fc),
        grid=(t // tm,),
        in_specs=[
            pl.BlockSpec((tm, d), lambda m: (m, 0)),
            pl.BlockSpec((tm, d), lambda m: (m, 0)),
            pl.BlockSpec((d, dff), full, **resident),
            pl.BlockSpec((d, dff), full, **resident),
            pl.BlockSpec((dff, d), full, **resident),
            pl.BlockSpec((tm, p2.shape[1]), lambda m: (p_blk0 + m, 0)),
            pl.BlockSpec((1, d), full),
            pl.BlockSpec(wg.shape, full, **resident),
            pl.BlockSpec(wp.shape, full, **resident),
        ],
        out_specs=pl.BlockSpec((tm, d), lambda m: (m, 0)),
        out_shape=jax.ShapeDtypeStruct((t, d), F32),
        scratch_shapes=[pltpu.VMEM((tm, dff), BF16)],
        compiler_params=_params("arbitrary"),
        name="dense_ffn",
    )(hf, x1, w1, w3, w2, p2, g, wg, wp)


META_E0, META_E1, META_G0, META_G1, META_R0, META_R1 = 0, 1, 2, 3, 4, 5


def _route_body(lg_ref, meta_ref, tab_ref, cnt_ref, carry_ref, *, tm, ne):
    @pl.when(pl.program_id(0) == 0)
    def _():
        carry_ref[...] = jnp.zeros_like(carry_ref)

    lane = lax.broadcasted_iota(jnp.int32, (tm, LANES), 1)
    lanef = lane.astype(F32)
    lg = jnp.where(lane < ne, lg_ref[...], -jnp.inf)
    m1 = jnp.max(lg, axis=-1, keepdims=True)
    e1 = jnp.min(jnp.where(lg == m1, lanef, float(LANES)), axis=-1, keepdims=True)
    lg2 = jnp.where(lanef == e1, -jnp.inf, lg)
    m2 = jnp.max(lg2, axis=-1, keepdims=True)
    e2 = jnp.min(jnp.where(lg2 == m2, lanef, float(LANES)), axis=-1, keepdims=True)
    ex = jnp.exp(m2 - m1)
    g1 = 1.0 / (1.0 + ex)
    g2 = ex / (1.0 + ex)
    onehot = jnp.where((lanef == e1) | (lanef == e2), 1.0, 0.0)
    row = lax.broadcasted_iota(jnp.int32, (tm, tm), 0)
    col = lax.broadcasted_iota(jnp.int32, (tm, tm), 1)
    before = jnp.dot(jnp.where(col < row, 1.0, 0.0).astype(BF16), onehot.astype(BF16),
                     preferred_element_type=F32) + carry_ref[...]
    r1 = jnp.sum(jnp.where(lanef == e1, before, 0.0), axis=-1, keepdims=True)
    r2 = jnp.sum(jnp.where(lanef == e2, before, 0.0), axis=-1, keepdims=True)
    carry_ref[...] += jnp.sum(onehot, axis=0, keepdims=True)
    meta = jnp.zeros((tm, LANES), F32)
    for pos, val in ((META_E0, e1), (META_E1, e2), (META_G0, g1), (META_G1, g2), (META_R0, r1), (META_R1, r2)):
        meta = jnp.where(lane == pos, val, meta)
    meta_ref[...] = meta
    tab_ref[...] = meta.T[0:SUBLANES, :]
    cnt_ref[...] = jnp.broadcast_to(carry_ref[...], cnt_ref.shape)


def route(logits, *, tm):
    t = logits.shape[0]
    return pl.pallas_call(
        functools.partial(_route_body, tm=tm, ne=N_EXPERTS),
        grid=(t // tm,),
        in_specs=[pl.BlockSpec((tm, LANES), lambda m: (m, 0))],
        out_specs=[pl.BlockSpec((tm, LANES), lambda m: (m, 0)),
                   pl.BlockSpec((SUBLANES, tm), lambda m: (0, m)),
                   pl.BlockSpec((SUBLANES, LANES), lambda m: (0, 0))],
        out_shape=[jax.ShapeDtypeStruct((t, LANES), F32), jax.ShapeDtypeStruct((SUBLANES, t), F32),
                   jax.ShapeDtypeStruct((SUBLANES, LANES), F32)],
        scratch_shapes=[pltpu.VMEM((1, LANES), F32)],
        compiler_params=_params("arbitrary"),
        name="route",
    )(logits)


def _dispatch_body(dest_ref, pe_ref, na_ref, hf_ref, xs_ref, zero_ref, sem, zsem, *, tm, topk, tm_rows, n_tok):
    base = pl.program_id(0) * tm

    @pl.when(pl.program_id(0) == 0)
    def _():
        zero_ref[...] = jnp.zeros_like(zero_ref)
        n_tiles = xs_ref.shape[0] // (tm_rows * SUBLANES)
        fills = []
        for e in range(N_EXPERTS):
            end = pe_ref[e]
            nonempty = end > (pe_ref[e - 1] if e else 0)
            fills.append((nonempty, pl.multiple_of(jnp.maximum(end - tm_rows, 0), tm_rows)))
        for tile in range(n_tiles):
            fills.append((tile >= na_ref[0], tile * tm_rows))
        for phase in ("start", "wait"):
            for cond, row0 in fills:
                @pl.when(cond)
                def _(row0=row0, phase=phase):
                    cp = pltpu.make_async_copy(
                        zero_ref, xs_ref.at[pl.ds(row0 * SUBLANES, tm_rows * SUBLANES)], zsem)
                    cp.start() if phase == "start" else cp.wait()

    def issue(r, c):
        for k in range(topk):
            d = dest_ref[k * n_tok + base + r]
            pltpu.make_async_copy(_row_tile(hf_ref, r), _row_tile(xs_ref, d), sem).start(priority=k % 2)
        return c

    lax.fori_loop(0, tm, issue, 0, unroll=ROW_DMA_UNROLL)
    for k in range(topk):
        pltpu.make_async_copy(hf_ref, xs_ref.at[pl.ds(0, tm * SUBLANES)], sem).wait()


def _row_tile(ref, r):
    return ref.at[pl.ds(pl.multiple_of(r * SUBLANES, SUBLANES), SUBLANES)]


def dispatch(dest, pad_end, n_active, hf, *, n_rows, tm, topk, tm_rows):
    t = hf.shape[0] // SUBLANES
    return pl.pallas_call(
        functools.partial(_dispatch_body, tm=tm, topk=topk, tm_rows=tm_rows, n_tok=t),
        grid_spec=pltpu.PrefetchScalarGridSpec(
            num_scalar_prefetch=3,
            grid=(t // tm,),
            in_specs=[pl.BlockSpec((tm * SUBLANES, LANES), lambda m, *_: (m, 0))],
            out_specs=pl.BlockSpec(memory_space=pl.ANY),
            scratch_shapes=[pltpu.VMEM((tm_rows * SUBLANES, LANES), F32),
                            pltpu.SemaphoreType.DMA, pltpu.SemaphoreType.DMA],
        ),
        out_shape=jax.ShapeDtypeStruct((n_rows * SUBLANES, LANES), F32),
        compiler_params=_params("arbitrary"),
        name="moe_dispatch",
    )(dest, pad_end, n_active, hf)


def _experts_body(te_ref, na_ref, xs_ref, xs_next_ref, w1_ref, w3_ref, w2_ref, y_ref, xb_ref, g_ref, acc_ref,
                  *, fc):
    del te_ref
    i = pl.program_id(0)
    f = pl.program_id(1)
    last = pl.num_programs(1) - 1
    tm = xb_ref.shape[0]

    @pl.when(i >= na_ref[0])
    def _():
        y_ref[...] = jnp.zeros_like(y_ref)

    @pl.when(i < na_ref[0])
    def _():
        @pl.when((i == 0) & (f == 0))
        def _():
            xb_ref[...] = _load_row_tiles(xs_ref, tm).astype(BF16)

        _swiglu_gate(xb_ref[...], w1_ref.at[0], w3_ref.at[0], g_ref, fc)

        @pl.when(f < last)
        def _():
            y = jnp.dot(g_ref[...], w2_ref[0].astype(BF16), preferred_element_type=F32)

            @pl.when(f == 0)
            def _():
                acc_ref[...] = y

            @pl.when(f > 0)
            def _():
                acc_ref[...] += y

        @pl.when(f == last)
        def _():
            xb_ref[...] = _load_row_tiles(xs_next_ref, tm).astype(BF16)
            y = jnp.dot(g_ref[...], w2_ref[0].astype(BF16), preferred_element_type=F32)
            _store_row_tiles(y_ref, acc_ref[...] + y)


def experts(tile_expert, n_active, xs, w1, w3, w2, *, tm, tf, fc):
    n_rows = xs.shape[0] // SUBLANES
    d = w1.shape[1]
    dff = w1.shape[2]
    assert dff // tf >= 2
    row_tile = lambda i, f, te, na: (jnp.minimum(i, na[0] - 1), 0)
    ftile = lambda i, f, na: jnp.where(i < na[0], f, dff // tf - 1)
    return pl.pallas_call(
        functools.partial(_experts_body, fc=fc),
        grid_spec=pltpu.PrefetchScalarGridSpec(
            num_scalar_prefetch=2,
            grid=(n_rows // tm, dff // tf),
            in_specs=[
                pl.BlockSpec((tm * SUBLANES, LANES), row_tile),
                pl.BlockSpec((tm * SUBLANES, LANES), lambda i, f, te, na: (jnp.minimum(i + 1, na[0] - 1), 0)),
                pl.BlockSpec((1, d, tf), lambda i, f, te, na: (te[i], 0, ftile(i, f, na))),
                pl.BlockSpec((1, d, tf), lambda i, f, te, na: (te[i], 0, ftile(i, f, na))),
                pl.BlockSpec((1, tf, d), lambda i, f, te, na: (te[i], ftile(i, f, na), 0)),
            ],
            out_specs=pl.BlockSpec((tm * SUBLANES, LANES), lambda i, f, te, na: (i, 0)),
            scratch_shapes=[pltpu.VMEM((tm, d), BF16), pltpu.VMEM((tm, tf), BF16), pltpu.VMEM((tm, d), F32)],
        ),
        out_shape=jax.ShapeDtypeStruct((n_rows * SUBLANES, LANES), F32),
        compiler_params=_params("arbitrary", "arbitrary"),
        name="moe_experts",
    )(tile_expert, n_active, xs, xs, w1, w3, w2)


def _combine_body(dest_ref, x1_ref, meta_ref, p_ref, g_ref, wg_ref, wp_ref, y_ref, o_ref, buf_ref, sem,
                  *, tm, topk):
    base = pl.program_id(0) * tm
    n_tok = pl.num_programs(0) * tm

    def issue(r, c):
        for k in range(topk):
            d = dest_ref[k * n_tok + base + r]
            pltpu.make_async_copy(_row_tile(y_ref, d), _row_tile(buf_ref.at[k], r), sem).start(priority=k % 2)
        return c

    lax.fori_loop(0, tm, issue, 0, unroll=ROW_DMA_UNROLL)
    for k in range(topk):
        pltpu.make_async_copy(y_ref.at[pl.ds(0, tm * SUBLANES)], buf_ref.at[k], sem).wait()
    meta = meta_ref[...]
    g0 = meta[:, META_G0:META_G0 + 1]
    g1 = meta[:, META_G1:META_G1 + 1]
    x2 = x1_ref[...] + (g0 * _load_row_tiles(buf_ref.at[0], tm) + g1 * _load_row_tiles(buf_ref.at[1], tm))
    o_ref[...] = _ple(x2, p_ref, g_ref, wg_ref, wp_ref)


def combine(dest, x1, meta, y, p2, p_row0, g, wg, wp, *, tm, topk):
    t, d = x1.shape
    full = lambda m, dest: (0, 0)
    p_blk0 = p_row0 // tm
    return pl.pallas_call(
        functools.partial(_combine_body, tm=tm, topk=topk),
        grid_spec=pltpu.PrefetchScalarGridSpec(
            num_scalar_prefetch=1,
            grid=(t // tm,),
            in_specs=[pl.BlockSpec((tm, d), lambda m, dest: (m, 0)),
                      pl.BlockSpec((tm, LANES), lambda m, dest: (m, 0)),
                      pl.BlockSpec((tm, p2.shape[1]), lambda m, dest: (p_blk0 + m, 0)),
                      pl.BlockSpec((1, d), full), pl.BlockSpec(wg.shape, full), pl.BlockSpec(wp.shape, full),
                      pl.BlockSpec(memory_space=pl.ANY)],
            out_specs=pl.BlockSpec((tm, d), lambda m, dest: (m, 0)),
            scratch_shapes=[pltpu.VMEM((topk, tm * SUBLANES, LANES), F32), pltpu.SemaphoreType.DMA],
        ),
        out_shape=jax.ShapeDtypeStruct((t, d), F32),
        compiler_params=_params("arbitrary"),
        name="moe_combine",
    )(dest, x1, meta, p2, g, wg, wp, y)


def moe_ffn(hf, x1, logits, w1, w3, w2, ple_args, *, tm_route, tm_rows, tf, tm_move):
    t, d = x1.shape
    topk = 2
    meta, tab, cnt = route(logits, tm=tm_route)
    counts = cnt[0, :N_EXPERTS].astype(jnp.int32)
    padded = ((counts + tm_rows - 1) // tm_rows) * tm_rows
    pad_end = jnp.cumsum(padded).astype(jnp.int32)
    pad_start = pad_end - padded
    eidx = tab[META_E0:META_E1 + 1].astype(jnp.int32)
    rank = tab[META_R0:META_R1 + 1].astype(jnp.int32)
    dest = rank
    for e in range(N_EXPERTS):
        dest = dest + jnp.where(eidx == e, pad_start[e], 0)
    dest = dest.reshape(topk * t)
    n_tiles = -(-(t * topk) // tm_rows) + N_EXPERTS
    tile_start = jnp.arange(n_tiles, dtype=jnp.int32) * tm_rows
    tile_expert = jnp.minimum(jnp.sum(tile_start[:, None] >= pad_end[None, :], axis=1),
                              N_EXPERTS - 1).astype(jnp.int32)
    n_active = pad_end[N_EXPERTS - 1:] // tm_rows
    xs = dispatch(dest, pad_end, n_active, hf, n_rows=n_tiles * tm_rows, tm=tm_move, topk=topk, tm_rows=tm_rows)
    y = experts(tile_expert, n_active, xs, w1, w3, w2, tm=tm_rows, tf=tf, fc=256)
    return combine(dest, x1, meta, y, *ple_args, tm=tm_move, topk=topk)


def _tile2(g):
    return jnp.concatenate([g, g]).reshape(1, 2 * g.shape[0])


def kernel(x, p, g_mix, w_in, g_q, g_k, conv_w, conv_b, b_i, b_f, g_h, w_oa, w_ob, w_out, g_ffn, w_d1, w_d3,
           w_d2, w_router, w_e1, w_e3, w_e2, g_ple, w_ple_gate, w_ple_proj):
    batch, seq, d = x.shape
    depth = w_in.shape[0]
    t = batch * seq
    nh = MLSTM_HEADS
    x2 = x.reshape(t, d)
    c_q, c_k, c_v = 0, 512, 1024
    c_qk, c_vm, c_om, c_i, c_f, c_ga, c_gb, c_end = 1536, 2560, 3072, 3584, 3588, 3592, 4616, 5640

    w_in_t = jnp.swapaxes(w_in, 1, 2).astype(BF16)
    for l in range(depth):
        w_bf = w_in_t[l]
        w_gates = w_bf[c_ga:c_end]
        proj, gif = in_proj(x2, g_mix[l].reshape(1, d), w_gates, w_bf, nb_cols=c_i, if_col=c_i,
                            tm=1024, tn=512)

        ya = moba(proj, _tile2(g_q[l]), _tile2(g_k[l]), batch=batch, seq=seq)

        bias = jnp.concatenate([b_i[l], b_f[l]])
        bias_row = jnp.pad(bias, (0, LANES - 2 * nh)).reshape(1, LANES)
        bias_col = bias.reshape(2 * nh, 1)
        gates_row = gif[:, :2 * nh].reshape(batch, seq, 2 * nh).transpose(0, 2, 1)
        yb = mlstm(proj, gif, gates_row, bias_row, bias_col, conv_w[l], conv_b[l].reshape(1, -1),
                   g_h[l].reshape(1, -1), batch=batch, seq=seq, chunk=256)

        j = l // 2
        moe = l % 2 == 1
        wr = jnp.pad(w_router[j], ((0, 0), (0, LANES - N_EXPERTS))) if moe else None
        outs = merge(ya, yb, proj, x2, w_oa[l].astype(BF16), w_ob[l].astype(BF16), w_out[l].astype(BF16),
                     g_ffn[l].reshape(1, d), wr, tm=1024)
        ple_args = (p.reshape(depth * t, -1), l * t, g_ple[l].reshape(1, d), w_ple_gate[l].astype(BF16),
                    w_ple_proj[l].astype(BF16))
        if moe:
            x1, hf, logits = outs
            x2 = moe_ffn(hf, x1, logits, w_e1[j].astype(BF16), w_e3[j].astype(BF16), w_e2[j],
                         ple_args, tm_route=512, tm_rows=512, tf=1792, tm_move=512)
        else:
            x1, hf = outs
            x2 = dense_ffn(hf, x1, w_d1[j].astype(BF16), w_d3[j].astype(BF16), w_d2[j].astype(BF16),
                           *ple_args, tm=1024, fc=256)
    return x2.reshape(batch, seq, d)
```

```python
import functools

import jax
import jax.numpy as jnp
from jax import lax
from jax.experimental import pallas as pl
from jax.experimental.pallas import tpu as pltpu

F32 = jnp.float32
BF16 = jnp.bfloat16
HIGHEST = lax.Precision.HIGHEST

RMS_EPS = 1e-6
LANES = 128
SUBLANES = 8

MOBA_HEADS = 8
MOBA_HEAD_DIM = 64
MOBA_BLOCK = 256
MOBA_TOPK = 3
MLSTM_HEADS = 4
MLSTM_DIM = 128
CONV_WIDTH = 4
N_EXPERTS = 8

COL_GA, COL_GB = 0, 8
COL_QA, COL_KA, COL_VA = 16, 20, 24
COL_QM, COL_KM, COL_VM, COL_OM = 28, 32, 36, 40
N_PROJ = 44 * LANES

VMEM_LIMIT = 56 * 1024 * 1024
ROW_DMA_UNROLL = 8


def _params(*sem):
    return pltpu.CompilerParams(dimension_semantics=sem, vmem_limit_bytes=VMEM_LIMIT)


def _sigmoid(x):
    return 1.0 / (1.0 + jnp.exp(-x))


def _rms(x, g):
    return x * lax.rsqrt(jnp.mean(x * x, axis=-1, keepdims=True) + RMS_EPS) * g


def _split_bf16(x):
    hi = x.astype(BF16)
    return hi, (x - hi.astype(F32)).astype(BF16)


def _dot_split(a, b):
    ah, al = _split_bf16(a)
    bh, bl = _split_bf16(b)
    return (jnp.dot(ah, bh, preferred_element_type=F32) + jnp.dot(ah, bl, preferred_element_type=F32)
            + jnp.dot(al, bh, preferred_element_type=F32))


def _store_row_tiles(ref, x):
    ref[...] = x.reshape(x.shape[0] * SUBLANES, LANES)


def _load_row_tiles(ref, rows):
    return ref[...].reshape(rows, SUBLANES * LANES)


def _nt_dot(a, b, **kw):
    return lax.dot_general(a, b, (((1,), (1,)), ((), ())), preferred_element_type=F32, **kw)


def _in_proj_body(x_ref, g_ref, wa_ref, wb_ref, o_ref, oif_ref, h_ref, *, nb_cols, if_col, tn):
    h_ref[...] = _rms(x_ref[...], g_ref[...]).astype(BF16)
    oif_ref[...] = _nt_dot(h_ref[...], wb_ref[if_col:if_col + LANES, :])
    na = wa_ref.shape[0]
    for c0 in range(0, na + nb_cols, tn):
        w = wa_ref[c0:c0 + tn, :] if c0 < na else wb_ref[c0 - na:c0 - na + tn, :]
        o_ref[:, c0:c0 + tn] = _nt_dot(h_ref[...], w).astype(o_ref.dtype)


def in_proj(x2, g, wa, w_full, *, nb_cols, if_col, tm, tn):
    t, d = x2.shape
    n = wa.shape[0] + nb_cols
    resident = dict(pipeline_mode=pl.Buffered(1))
    return pl.pallas_call(
        functools.partial(_in_proj_body, nb_cols=nb_cols, if_col=if_col, tn=tn),
        grid=(t // tm,),
        in_specs=[
            pl.BlockSpec((tm, d), lambda m: (m, 0)),
            pl.BlockSpec((1, d), lambda m: (0, 0)),
            pl.BlockSpec(wa.shape, lambda m: (0, 0), **resident),
            pl.BlockSpec(w_full.shape, lambda m: (0, 0), **resident),
        ],
        out_specs=[
            pl.BlockSpec((tm, n), lambda m: (m, 0)),
            pl.BlockSpec((tm, LANES), lambda m: (m, 0)),
        ],
        out_shape=[jax.ShapeDtypeStruct((t, n), BF16), jax.ShapeDtypeStruct((t, LANES), F32)],
        scratch_shapes=[pltpu.VMEM((tm, d), BF16)],
        compiler_params=_params("arbitrary"),
        name="in_proj",
    )(x2, g, wa, w_full)


MASK_BIAS = -1e30
LOG2_E = 1.4426950408889634


def _moba_body(q_ref, k_ref, v_ref, gq_ref, gk_ref, o_ref,
               kn_ref, vt_ref, kmean_ref, qaug_ref, s_ref, m_ref, alpha_ref, acc_ref,
               *, nb, blk, dh, topk, nheads):
    i = pl.program_id(1)
    pair = 2 * blk
    lane = lax.broadcasted_iota(jnp.int32, (1, LANES), 1)
    head0 = lane < dh

    same_head = (lax.broadcasted_iota(jnp.int32, (LANES, LANES), 0) // dh
                 == lax.broadcasted_iota(jnp.int32, (LANES, LANES), 1) // dh)
    head_ones = jnp.where(same_head, 1.0, 0.0).astype(BF16)

    def head_rms(x, g, on_mxu):
        x2 = x * x
        if on_mxu:
            hi, lo = _split_bf16(x2)
            ss = (jnp.dot(hi, head_ones, preferred_element_type=F32)
                  + jnp.dot(lo, head_ones, preferred_element_type=F32))
        else:
            s0 = jnp.sum(jnp.where(head0, x2, 0.0), axis=-1, keepdims=True)
            s1 = jnp.sum(jnp.where(head0, 0.0, x2), axis=-1, keepdims=True)
            ss = jnp.where(head0, s0, s1)
        return x * lax.rsqrt(ss * (1.0 / dh) + RMS_EPS) * g

    @pl.when(i == 0)
    def _():
        def prep(j, c):
            r0 = pl.multiple_of(j * blk, blk)
            onehot = jnp.where(lane == dh + j, 1.0, 0.0)
            for p in range(nheads // 2):
                cols = slice(p * LANES, (p + 1) * LANES)
                kn = head_rms(k_ref[pl.ds(r0, blk), cols].astype(F32), gk_ref[...], True)
                for hh, kh in ((0, kn), (1, pltpu.roll(kn, dh, axis=1))):
                    h = 2 * p + hh
                    kmean_ref[h, pl.ds(j, 1), :] = jnp.mean(jnp.where(head0, kh, 0.0), axis=0, keepdims=True)
                    kn_ref[h, pl.ds(r0, blk), :] = jnp.where(head0, kh, onehot).astype(BF16)
                v_t = v_ref[pl.ds(r0, blk), cols].astype(F32).T.astype(BF16)
                for hh in range(2):
                    vt_ref[2 * p + hh, 0:dh, pl.ds(r0, blk)] = v_t[hh * dh:(hh + 1) * dh, :]
                    vt_ref[2 * p + hh, dh:, pl.ds(r0, blk)] = jnp.ones((vt_ref.shape[1] - dh, blk), BF16)
            return c

        lax.fori_loop(0, nb, prep, 0)

    jidx = lax.broadcasted_iota(jnp.int32, (nb, blk), 0)
    key_i = lax.broadcasted_iota(jnp.int32, (blk, blk), 0)
    qry_i = lax.broadcasted_iota(jnp.int32, (blk, blk), 1)
    causal = key_i <= qry_i
    r_own = pl.multiple_of(i * blk, blk)
    qk_scale = dh ** -0.5 * LOG2_E
    for p in range(nheads // 2):
        cols = slice(p * LANES, (p + 1) * LANES)
        qn_t = head_rms(q_ref[:, cols].astype(F32), gq_ref[...], False).T
        for hh in range(2):
            h = 2 * p + hh
            q_t = qn_t[hh * dh:(hh + 1) * dh, :]
            gate = jnp.dot(kmean_ref[h], jnp.concatenate([q_t, jnp.zeros((LANES - dh, blk), F32)], axis=0),
                           precision=HIGHEST, preferred_element_type=F32)
            rank = jnp.zeros((nb, blk), F32)
            for jp in range(nb):
                row = gate[jp:jp + 1, :]
                beats = (row > gate) | ((row == gate) & (jidx > jp))
                rank = rank + jnp.where(beats, jnp.where(jp < i, 1.0, 0.0), 0.0)
            sel = (rank < topk) & (jidx < i)
            q_s = q_t * qk_scale
            pad = jnp.zeros((LANES - dh - nb, blk), F32)
            qaug_ref[h] = jnp.concatenate([q_s, jnp.where(sel, 0.0, MASK_BIAS), pad], axis=0).astype(BF16)
            qaug_own = jnp.concatenate([q_s, jnp.where(jidx == i, 0.0, MASK_BIAS), pad], axis=0).astype(BF16)
            st = jnp.dot(kn_ref[h, pl.ds(r_own, blk), :], qaug_own, preferred_element_type=F32)
            st = jnp.where(causal, st, -jnp.inf)
            s_ref[h, 0:blk, :] = st
            m_ref[h] = jnp.max(st, axis=0, keepdims=True)

    def finish_own(h):
        pr = jnp.exp2(s_ref[h, 0:blk, :] - m_ref[h]).astype(BF16)
        acc_ref[h] = jnp.dot(vt_ref[h, :, pl.ds(r_own, blk)], pr, preferred_element_type=F32)

    def score_pair(u, h):
        r0 = pl.multiple_of(u * pair, pair)
        st = jnp.dot(kn_ref[h, pl.ds(r0, pair), :], qaug_ref[h], preferred_element_type=F32)
        m_old = m_ref[h]
        m_new = jnp.maximum(m_old, jnp.max(st, axis=0, keepdims=True))
        s_ref[h] = st
        alpha_ref[h] = jnp.exp2(m_old - m_new)
        m_ref[h] = m_new

    def finish_pair(u, h):
        r0 = pl.multiple_of(u * pair, pair)
        pr = jnp.exp2(s_ref[h] - m_ref[h]).astype(BF16)
        acc_ref[h] = alpha_ref[h] * acc_ref[h] + jnp.dot(vt_ref[h, :, pl.ds(r0, pair)], pr,
                                                         preferred_element_type=F32)

    n_pairs = jnp.maximum((i + 1) // 2, 1)
    for h in range(nheads):
        finish_own(h)
        score_pair(0, h)

    def body(u, c):
        for h in range(nheads):
            finish_pair(u - 1, h)
            score_pair(u, h)
        return c

    lax.fori_loop(1, n_pairs, body, 0)
    for h in range(nheads):
        finish_pair(n_pairs - 1, h)

    for p in range(nheads // 2):
        a0 = acc_ref[2 * p]
        a1 = acc_ref[2 * p + 1]
        ot = jnp.concatenate([a0[0:dh] / a0[dh:dh + 1], a1[0:dh] / a1[dh:dh + 1]], axis=0)
        o_ref[:, p * LANES:(p + 1) * LANES] = ot.T.astype(o_ref.dtype)


def moba(proj, gq2, gk2, *, batch, seq):
    nb = seq // MOBA_BLOCK
    blk = MOBA_BLOCK
    dh = MOBA_HEAD_DIM
    nheads = MOBA_HEADS
    width = nheads * dh
    wb = width // LANES
    assert dh + nb <= LANES and 2 * dh == LANES and nb % 2 == 0
    v_rows = dh + 2 * SUBLANES
    body = functools.partial(_moba_body, nb=nb, blk=blk, dh=dh, topk=MOBA_TOPK, nheads=nheads)
    return pl.pallas_call(
        body,
        grid=(batch, nb),
        in_specs=[
            pl.BlockSpec((blk, width), lambda b, i: (b * nb + i, COL_QA // wb)),
            pl.BlockSpec((seq, width), lambda b, i: (b, COL_KA // wb)),
            pl.BlockSpec((seq, width), lambda b, i: (b, COL_VA // wb)),
            pl.BlockSpec((1, LANES), lambda b, i: (0, 0)),
            pl.BlockSpec((1, LANES), lambda b, i: (0, 0)),
        ],
        out_specs=pl.BlockSpec((blk, width), lambda b, i: (b * nb + i, 0)),
        out_shape=jax.ShapeDtypeStruct((batch * seq, width), BF16),
        scratch_shapes=[
            pltpu.VMEM((nheads, seq, LANES), BF16),
            pltpu.VMEM((nheads, v_rows, seq), BF16),
            pltpu.VMEM((nheads, nb, LANES), F32),
            pltpu.VMEM((nheads, LANES, blk), BF16),
            pltpu.VMEM((nheads, 2 * blk, blk), F32),
            pltpu.VMEM((nheads, 1, blk), F32),
            pltpu.VMEM((nheads, 1, blk), F32),
            pltpu.VMEM((nheads, v_rows, blk), F32),
        ],
        compiler_params=_params("arbitrary", "arbitrary"),
        name="moba",
    )(proj, proj, proj, gq2, gk2)


def _log_sigmoid(x):
    return jnp.minimum(x, 0.0) - jnp.log(1.0 + jnp.exp(-jnp.abs(x)))


def _dot_tri(tri, x, tri_left):
    out = None
    for _ in range(3):
        piece = x.astype(BF16)
        x = x - piece.astype(F32)
        term = (jnp.dot(tri, piece, preferred_element_type=F32) if tri_left
                else jnp.dot(piece, tri, preferred_element_type=F32))
        out = term if out is None else out + term
    return out


def _mlstm_body(qr_ref, kr_ref, v_ref, og_ref, gcol_ref, grow_ref, brow_ref, bcol_ref,
                cwq_ref, cwk_ref, cbq_ref, cbk_ref, gh_ref, o_ref,
                qx_ref, kx_ref, c_ref, m_ref, *, chunk, dk, nh):
    L = chunk
    width = nh * dk

    @pl.when(pl.program_id(1) == 0)
    def _():
        qx_ref[0:SUBLANES, :] = jnp.zeros((SUBLANES, width), F32)
        kx_ref[0:SUBLANES, :] = jnp.zeros((SUBLANES, width), F32)
        c_ref[...] = jnp.zeros_like(c_ref)
        m_ref[...] = jnp.zeros_like(m_ref)

    qx_ref[SUBLANES:SUBLANES + L, :] = qr_ref[...].astype(F32)
    kx_ref[SUBLANES:SUBLANES + L, :] = kr_ref[...].astype(F32)

    def conv_silu(x_ref, w_ref, b_ref):
        acc = b_ref[...] + w_ref[0:1, :] * x_ref[pl.ds(SUBLANES - CONV_WIDTH + 1, L), :]
        for j in range(1, CONV_WIDTH):
            acc = acc + w_ref[j:j + 1, :] * x_ref[pl.ds(SUBLANES - CONV_WIDTH + 1 + j, L), :]
        return acc * _sigmoid(acc)

    q_all = conv_silu(qx_ref, cwq_ref, cbq_ref)
    k_all = conv_silu(kx_ref, cwk_ref, cbk_ref) * (dk ** -0.5)
    qx_ref[0:SUBLANES, :] = qx_ref[L:L + SUBLANES, :]
    kx_ref[0:SUBLANES, :] = kx_ref[L:L + SUBLANES, :]

    pre_col = gcol_ref[...] + brow_ref[...]
    pre_row = grow_ref[0] + bcol_ref[...]
    t_i = lax.broadcasted_iota(jnp.int32, (L, L), 0)
    s_i = lax.broadcasted_iota(jnp.int32, (L, L), 1)
    tril = s_i <= t_i
    bcum_cols = _dot_tri(jnp.where(tril, 1.0, 0.0).astype(BF16), _log_sigmoid(pre_col), True)
    bcum_rows = _dot_tri(jnp.where(t_i <= s_i, 1.0, 0.0).astype(BF16), _log_sigmoid(pre_row), False)
    ones = jnp.ones((L, dk), BF16)

    for h in range(nh):
        cols = slice(h * dk, (h + 1) * dk)
        q = q_all[:, cols]
        k = k_all[:, cols]
        i_col = pre_col[:, h:h + 1]
        i_row = pre_row[h:h + 1, :]
        bcum_col = bcum_cols[:, nh + h:nh + h + 1]
        bcum_row = bcum_rows[nh + h:nh + h + 1, :]

        m_prev = m_ref[h, 0:1, 0:1]
        a_col = bcum_col + m_prev
        dmat = jnp.where(tril, bcum_col - bcum_row + i_row, -jnp.inf)
        m_t = jnp.maximum(a_col, jnp.max(dmat, axis=-1, keepdims=True))
        dw = jnp.exp(dmat - m_t)
        aw = jnp.exp(a_col - m_t)

        qb = q.astype(BF16)
        kb = k.astype(BF16)
        v_aug = jnp.concatenate([v_ref[:, cols], ones], axis=-1)
        sqk = _nt_dot(qb, kb) * dw
        num_aug = (aw * jnp.dot(qb, c_ref[h].astype(BF16), preferred_element_type=F32)
                   + jnp.dot(sqk.astype(BF16), v_aug, preferred_element_type=F32))
        den = num_aug[:, dk:dk + 1]
        hc = num_aug[:, 0:dk] / jnp.maximum(jnp.abs(den), jnp.exp(-m_t))

        b_last = bcum_col[L - 1:L, :]
        g_col = b_last - bcum_col + i_col
        m_new = jnp.maximum(b_last + m_prev, jnp.max(g_col, axis=0, keepdims=True))
        w_c = jnp.exp(b_last + m_prev - m_new)
        kw_t = (k * jnp.exp(g_col - m_new)).T.astype(BF16)
        c_ref[h] = w_c * c_ref[h] + jnp.dot(kw_t, v_aug, preferred_element_type=F32)
        m_ref[h] = jnp.broadcast_to(m_new, (1, LANES))

        o_ref[:, cols] = (_rms(hc, gh_ref[...]) * _sigmoid(og_ref[:, cols].astype(F32))).astype(o_ref.dtype)


def mlstm(proj, gates_col, gates_row, bias_row, bias_col, conv_w, conv_b, gh, *, batch, seq, chunk):
    nh = MLSTM_HEADS
    dk = MLSTM_DIM
    width = nh * dk
    wb = width // LANES
    nc = seq // chunk
    body = functools.partial(_mlstm_body, chunk=chunk, dk=dk, nh=nh)

    def rows(col0):
        return pl.BlockSpec((chunk, width), lambda b, c: (b * nc + c, col0 // wb))

    return pl.pallas_call(
        body,
        grid=(batch, nc),
        in_specs=[
            rows(COL_QM), rows(COL_KM), rows(COL_VM), rows(COL_OM),
            pl.BlockSpec((chunk, LANES), lambda b, c: (b * nc + c, 0)),
            pl.BlockSpec((1, SUBLANES, chunk), lambda b, c: (b, 0, c)),
            pl.BlockSpec((1, LANES), lambda b, c: (0, 0)),
            pl.BlockSpec((SUBLANES, 1), lambda b, c: (0, 0)),
            pl.BlockSpec((CONV_WIDTH, width), lambda b, c: (0, 0)),
            pl.BlockSpec((CONV_WIDTH, width), lambda b, c: (0, 1)),
            pl.BlockSpec((1, width), lambda b, c: (0, 0)),
            pl.BlockSpec((1, width), lambda b, c: (0, 1)),
            pl.BlockSpec((1, LANES), lambda b, c: (0, 0)),
        ],
        out_specs=pl.BlockSpec((chunk, width), lambda b, c: (b * nc + c, 0)),
        out_shape=jax.ShapeDtypeStruct((batch * seq, width), BF16),
        scratch_shapes=[
            pltpu.VMEM((chunk + 2 * SUBLANES, width), F32),
            pltpu.VMEM((chunk + 2 * SUBLANES, width), F32),
            pltpu.VMEM((nh, dk, 2 * dk), F32),
            pltpu.VMEM((nh, 1, LANES), F32),
        ],
        compiler_params=_params("arbitrary", "arbitrary"),
        name="mlstm",
    )(proj, proj, proj, proj, gates_col, gates_row, bias_row, bias_col,
      conv_w, conv_w, conv_b, conv_b, gh)


def _merge_body(ya_ref, yb_ref, ga_ref, gb_ref, x_ref, woa_ref, wob_ref, wout_ref, gffn_ref, *rest, moe):
    a = jnp.dot(ya_ref[...], woa_ref[...], preferred_element_type=F32)
    b = jnp.dot(yb_ref[...], wob_ref[...], preferred_element_type=F32)
    mixed = _sigmoid(ga_ref[...].astype(F32)) * a + _sigmoid(gb_ref[...].astype(F32)) * b
    x1 = x_ref[...] + jnp.dot(mixed.astype(BF16), wout_ref[...], preferred_element_type=F32)
    hf = _rms(x1, gffn_ref[...])
    if moe:
        wr_ref, x1_ref, hf_ref, lg_ref = rest
        _store_row_tiles(hf_ref, hf)
        lg_ref[...] = _dot_split(hf, wr_ref[...])
    else:
        x1_ref, hf_ref = rest
        hf_ref[...] = hf.astype(BF16)
    x1_ref[...] = x1


def merge(ya, yb, proj, x2, woa, wob, wout, gffn, wr, *, tm):
    t, d = x2.shape
    moe = wr is not None
    full = lambda m: (0, 0)
    in_specs = [
        pl.BlockSpec((tm, ya.shape[1]), lambda m: (m, 0)),
        pl.BlockSpec((tm, yb.shape[1]), lambda m: (m, 0)),
        pl.BlockSpec((tm, d), lambda m: (m, COL_GA * LANES // d)),
        pl.BlockSpec((tm, d), lambda m: (m, COL_GB * LANES // d)),
        pl.BlockSpec((tm, d), lambda m: (m, 0)),
        pl.BlockSpec(woa.shape, full), pl.BlockSpec(wob.shape, full), pl.BlockSpec(wout.shape, full),
        pl.BlockSpec((1, d), full),
    ]
    args = [ya, yb, proj, proj, x2, woa, wob, wout, gffn]
    out_specs = [pl.BlockSpec((tm, d), lambda m: (m, 0)), pl.BlockSpec((tm, d), lambda m: (m, 0))]
    out_shape = [jax.ShapeDtypeStruct((t, d), F32), jax.ShapeDtypeStruct((t, d), BF16)]
    if moe:
        out_specs[1] = pl.BlockSpec((tm * SUBLANES, LANES), lambda m: (m, 0))
        out_shape[1] = jax.ShapeDtypeStruct((t * SUBLANES, LANES), F32)
    if moe:
        in_specs.append(pl.BlockSpec(wr.shape, full))
        args.append(wr)
        out_specs.append(pl.BlockSpec((tm, LANES), lambda m: (m, 0)))
        out_shape.append(jax.ShapeDtypeStruct((t, LANES), F32))
    return pl.pallas_call(
        functools.partial(_merge_body, moe=moe),
        grid=(t // tm,),
        in_specs=in_specs, out_specs=out_specs, out_shape=out_shape,
        compiler_params=_params("arbitrary"),
        name="merge_moe" if moe else "merge",
    )(*args)


def _swiglu_gate(x, w1_ref, w3_ref, g_ref, fc):
    dff = g_ref.shape[1]
    for f0 in range(0, dff, fc):
        a = jnp.dot(x, w1_ref[:, f0:f0 + fc].astype(BF16), preferred_element_type=F32)
        b = jnp.dot(x, w3_ref[:, f0:f0 + fc].astype(BF16), preferred_element_type=F32)
        g_ref[:, f0:f0 + fc] = (a * _sigmoid(a) * b).astype(BF16)


def _swiglu(x, w1_ref, w3_ref, w2_ref, g_ref, fc):
    _swiglu_gate(x, w1_ref, w3_ref, g_ref, fc)
    return jnp.dot(g_ref[...], w2_ref[...].astype(BF16), preferred_element_type=F32)


def _ple(x, p_ref, g_ref, wg_ref, wp_ref):
    gate = _sigmoid(jnp.dot(_rms(x, g_ref[...]).astype(BF16), wg_ref[...], preferred_element_type=F32))
    emb = jnp.dot(p_ref[...].astype(BF16), wp_ref[...], preferred_element_type=F32)
    return x + gate * emb


def _ffn_body(hf_ref, x1_ref, w1_ref, w3_ref, w2_ref, p_ref, g_ref, wg_ref, wp_ref, o_ref, act_ref, *, fc):
    x2 = x1_ref[...] + _swiglu(hf_ref[...], w1_ref, w3_ref, w2_ref, act_ref, fc)
    o_ref[...] = _ple(x2, p_ref, g_ref, wg_ref, wp_ref)


def dense_ffn(hf, x1, w1, w3, w2, p2, p_row0, g, wg, wp, *, tm, fc):
    t, d = x1.shape
    dff = w1.shape[1]
    p_blk0 = p_row0 // tm
    resident = dict(pipeline_mode=pl.Buffered(1))
    full = lambda m: (0, 0)
    return pl.pallas_call(
        functools.partial(_ffn_body, fc=fc),
        grid=(t // tm,),
        in_specs=[
            pl.BlockSpec((tm, d), lambda m: (m, 0)),
            pl.BlockSpec((tm, d), lambda m: (m, 0)),
            pl.BlockSpec((d, dff), full, **resident),
            pl.BlockSpec((d, dff), full, **resident),
            pl.BlockSpec((dff, d), full, **resident),
            pl.BlockSpec((tm, p2.shape[1]), lambda m: (p_blk0 + m, 0)),
            pl.BlockSpec((1, d), full),
            pl.BlockSpec(wg.shape, full, **resident),
            pl.BlockSpec(wp.shape, full, **resident),
        ],
        out_specs=pl.BlockSpec((tm, d), lambda m: (m, 0)),
        out_shape=jax.ShapeDtypeStruct((t, d), F32),
        scratch_shapes=[pltpu.VMEM((tm, dff), BF16)],
        compiler_params=_params("arbitrary"),
        name="dense_ffn",
    )(hf, x1, w1, w3, w2, p2, g, wg, wp)


META_E0, META_E1, META_G0, META_G1, META_R0, META_R1 = 0, 1, 2, 3, 4, 5


def _route_body(lg_ref, meta_ref, tab_ref, cnt_ref, carry_ref, *, tm, ne):
    @pl.when(pl.program_id(0) == 0)
    def _():
        carry_ref[...] = jnp.zeros_like(carry_ref)

    lane = lax.broadcasted_iota(jnp.int32, (tm, LANES), 1)
    lanef = lane.astype(F32)
    lg = jnp.where(lane < ne, lg_ref[...], -jnp.inf)
    m1 = jnp.max(lg, axis=-1, keepdims=True)
    e1 = jnp.min(jnp.where(lg == m1, lanef, float(LANES)), axis=-1, keepdims=True)
    lg2 = jnp.where(lanef == e1, -jnp.inf, lg)
    m2 = jnp.max(lg2, axis=-1, keepdims=True)
    e2 = jnp.min(jnp.where(lg2 == m2, lanef, float(LANES)), axis=-1, keepdims=True)
    ex = jnp.exp(m2 - m1)
    g1 = 1.0 / (1.0 + ex)
    g2 = ex / (1.0 + ex)
    onehot = jnp.where((lanef == e1) | (lanef == e2), 1.0, 0.0)
    row = lax.broadcasted_iota(jnp.int32, (tm, tm), 0)
    col = lax.broadcasted_iota(jnp.int32, (tm, tm), 1)
    before = jnp.dot(jnp.where(col < row, 1.0, 0.0).astype(BF16), onehot.astype(BF16),
                     preferred_element_type=F32) + carry_ref[...]
    r1 = jnp.sum(jnp.where(lanef == e1, before, 0.0), axis=-1, keepdims=True)
    r2 = jnp.sum(jnp.where(lanef == e2, before, 0.0), axis=-1, keepdims=True)
    carry_ref[...] += jnp.sum(onehot, axis=0, keepdims=True)
    meta = jnp.zeros((tm, LANES), F32)
    for pos, val in ((META_E0, e1), (META_E1, e2), (META_G0, g1), (META_G1, g2), (META_R0, r1), (META_R1, r2)):
        meta = jnp.where(lane == pos, val, meta)
    meta_ref[...] = meta
    tab_ref[...] = meta.T[0:SUBLANES, :]
    cnt_ref[...] = jnp.broadcast_to(carry_ref[...], cnt_ref.shape)


def route(logits, *, tm):
    t = logits.shape[0]
    return pl.pallas_call(
        functools.partial(_route_body, tm=tm, ne=N_EXPERTS),
        grid=(t // tm,),
        in_specs=[pl.BlockSpec((tm, LANES), lambda m: (m, 0))],
        out_specs=[pl.BlockSpec((tm, LANES), lambda m: (m, 0)),
                   pl.BlockSpec((SUBLANES, tm), lambda m: (0, m)),
                   pl.BlockSpec((SUBLANES, LANES), lambda m: (0, 0))],
        out_shape=[jax.ShapeDtypeStruct((t, LANES), F32), jax.ShapeDtypeStruct((SUBLANES, t), F32),
                   jax.ShapeDtypeStruct((SUBLANES, LANES), F32)],
        scratch_shapes=[pltpu.VMEM((1, LANES), F32)],
        compiler_params=_params("arbitrary"),
        name="route",
    )(logits)


def _dispatch_body(dest_ref, pe_ref, na_ref, hf_ref, xs_ref, zero_ref, sem, zsem, *, tm, topk, tm_rows, n_tok):
    base = pl.program_id(0) * tm

    @pl.when(pl.program_id(0) == 0)
    def _():
        zero_ref[...] = jnp.zeros_like(zero_ref)
        n_tiles = xs_ref.shape[0] // (tm_rows * SUBLANES)
        fills = []
        for e in range(N_EXPERTS):
            end = pe_ref[e]
            nonempty = end > (pe_ref[e - 1] if e else 0)
            fills.append((nonempty, pl.multiple_of(jnp.maximum(end - tm_rows, 0), tm_rows)))
        for tile in range(n_tiles):
            fills.append((tile >= na_ref[0], tile * tm_rows))
        for phase in ("start", "wait"):
            for cond, row0 in fills:
                @pl.when(cond)
                def _(row0=row0, phase=phase):
                    cp = pltpu.make_async_copy(
                        zero_ref, xs_ref.at[pl.ds(row0 * SUBLANES, tm_rows * SUBLANES)], zsem)
                    cp.start() if phase == "start" else cp.wait()

    def issue(r, c):
        for k in range(topk):
            d = dest_ref[k * n_tok + base + r]
            pltpu.make_async_copy(_row_tile(hf_ref, r), _row_tile(xs_ref, d), sem).start(priority=k % 2)
        return c

    lax.fori_loop(0, tm, issue, 0, unroll=ROW_DMA_UNROLL)
    for k in range(topk):
        pltpu.make_async_copy(hf_ref, xs_ref.at[pl.ds(0, tm * SUBLANES)], sem).wait()


def _row_tile(ref, r):
    return ref.at[pl.ds(pl.multiple_of(r * SUBLANES, SUBLANES), SUBLANES)]


def dispatch(dest, pad_end, n_active, hf, *, n_rows, tm, topk, tm_rows):
    t = hf.shape[0] // SUBLANES
    return pl.pallas_call(
        functools.partial(_dispatch_body, tm=tm, topk=topk, tm_rows=tm_rows, n_tok=t),
        grid_spec=pltpu.PrefetchScalarGridSpec(
            num_scalar_prefetch=3,
            grid=(t // tm,),
            in_specs=[pl.BlockSpec((tm * SUBLANES, LANES), lambda m, *_: (m, 0))],
            out_specs=pl.BlockSpec(memory_space=pl.ANY),
            scratch_shapes=[pltpu.VMEM((tm_rows * SUBLANES, LANES), F32),
                            pltpu.SemaphoreType.DMA, pltpu.SemaphoreType.DMA],
        ),
        out_shape=jax.ShapeDtypeStruct((n_rows * SUBLANES, LANES), F32),
        compiler_params=_params("arbitrary"),
        name="moe_dispatch",
    )(dest, pad_end, n_active, hf)


def _experts_body(te_ref, na_ref, xs_ref, w1_ref, w3_ref, w2_ref, y_ref, xb_ref, g_ref, acc_ref, *, fc):
    del te_ref
    i = pl.program_id(0)
    f = pl.program_id(1)
    last = pl.num_programs(1) - 1
    tm = xb_ref.shape[0]

    @pl.when(i >= na_ref[0])
    def _():
        y_ref[...] = jnp.zeros_like(y_ref)

    @pl.when(i < na_ref[0])
    def _():
        @pl.when(f == 0)
        def _():
            xb_ref[...] = _load_row_tiles(xs_ref, tm).astype(BF16)

        y = _swiglu(xb_ref[...], w1_ref.at[0], w3_ref.at[0], w2_ref.at[0], g_ref, fc)

        @pl.when(f == 0)
        def _():
            acc_ref[...] = y

        @pl.when((f > 0) & (f < last))
        def _():
            acc_ref[...] += y

        @pl.when(f == last)
        def _():
            _store_row_tiles(y_ref, acc_ref[...] + y)


def experts(tile_expert, n_active, xs, w1, w3, w2, *, tm, tf, fc):
    n_rows = xs.shape[0] // SUBLANES
    d = w1.shape[1]
    dff = w1.shape[2]
    assert dff // tf >= 2
    row_tile = lambda i, f, te, na: (jnp.minimum(i, na[0] - 1), 0)
    ftile = lambda i, f, na: jnp.where(i < na[0], f, dff // tf - 1)
    return pl.pallas_call(
        functools.partial(_experts_body, fc=fc),
        grid_spec=pltpu.PrefetchScalarGridSpec(
            num_scalar_prefetch=2,
            grid=(n_rows // tm, dff // tf),
            in_specs=[
                pl.BlockSpec((tm * SUBLANES, LANES), row_tile),
                pl.BlockSpec((1, d, tf), lambda i, f, te, na: (te[i], 0, ftile(i, f, na))),
                pl.BlockSpec((1, d, tf), lambda i, f, te, na: (te[i], 0, ftile(i, f, na))),
                pl.BlockSpec((1, tf, d), lambda i, f, te, na: (te[i], ftile(i, f, na), 0)),
            ],
            out_specs=pl.BlockSpec((tm * SUBLANES, LANES), lambda i, f, te, na: (i, 0)),
            scratch_shapes=[pltpu.VMEM((tm, d), BF16), pltpu.VMEM((tm, tf), BF16), pltpu.VMEM((tm, d), F32)],
        ),
        out_shape=jax.ShapeDtypeStruct((n_rows * SUBLANES, LANES), F32),
        compiler_params=_params("arbitrary", "arbitrary"),
        name="moe_experts",
    )(tile_expert, n_active, xs, w1, w3, w2)


def _combine_body(dest_ref, x1_ref, meta_ref, p_ref, g_ref, wg_ref, wp_ref, y_ref, o_ref, buf_ref, sem,
                  *, tm, topk):
    base = pl.program_id(0) * tm
    n_tok = pl.num_programs(0) * tm

    def issue(r, c):
        for k in range(topk):
            d = dest_ref[k * n_tok + base + r]
            pltpu.make_async_copy(_row_tile(y_ref, d), _row_tile(buf_ref.at[k], r), sem).start(priority=k % 2)
        return c

    lax.fori_loop(0, tm, issue, 0, unroll=ROW_DMA_UNROLL)
    for k in range(topk):
        pltpu.make_async_copy(y_ref.at[pl.ds(0, tm * SUBLANES)], buf_ref.at[k], sem).wait()
    meta = meta_ref[...]
    g0 = meta[:, META_G0:META_G0 + 1]
    g1 = meta[:, META_G1:META_G1 + 1]
    x2 = x1_ref[...] + (g0 * _load_row_tiles(buf_ref.at[0], tm) + g1 * _load_row_tiles(buf_ref.at[1], tm))
    o_ref[...] = _ple(x2, p_ref, g_ref, wg_ref, wp_ref)


def combine(dest, x1, meta, y, p2, p_row0, g, wg, wp, *, tm, topk):
    t, d = x1.shape
    full = lambda m, dest: (0, 0)
    p_blk0 = p_row0 // tm
    return pl.pallas_call(
        functools.partial(_combine_body, tm=tm, topk=topk),
        grid_spec=pltpu.PrefetchScalarGridSpec(
            num_scalar_prefetch=1,
            grid=(t // tm,),
            in_specs=[pl.BlockSpec((tm, d), lambda m, dest: (m, 0)),
                      pl.BlockSpec((tm, LANES), lambda m, dest: (m, 0)),
                      pl.BlockSpec((tm, p2.shape[1]), lambda m, dest: (p_blk0 + m, 0)),
                      pl.BlockSpec((1, d), full), pl.BlockSpec(wg.shape, full), pl.BlockSpec(wp.shape, full),
                      pl.BlockSpec(memory_space=pl.ANY)],
            out_specs=pl.BlockSpec((tm, d), lambda m, dest: (m, 0)),
            scratch_shapes=[pltpu.VMEM((topk, tm * SUBLANES, LANES), F32), pltpu.SemaphoreType.DMA],
        ),
        out_shape=jax.ShapeDtypeStruct((t, d), F32),
        compiler_params=_params("arbitrary"),
        name="moe_combine",
    )(dest, x1, meta, p2, g, wg, wp, y)


def moe_ffn(hf, x1, logits, w1, w3, w2, ple_args, *, tm_route, tm_rows, tf, tm_move):
    t, d = x1.shape
    topk = 2
    meta, tab, cnt = route(logits, tm=tm_route)
    counts = cnt[0, :N_EXPERTS].astype(jnp.int32)
    padded = ((counts + tm_rows - 1) // tm_rows) * tm_rows
    pad_end = jnp.cumsum(padded).astype(jnp.int32)
    pad_start = pad_end - padded
    eidx = tab[META_E0:META_E1 + 1].astype(jnp.int32)
    rank = tab[META_R0:META_R1 + 1].astype(jnp.int32)
    dest = rank
    for e in range(N_EXPERTS):
        dest = dest + jnp.where(eidx == e, pad_start[e], 0)
    dest = dest.reshape(topk * t)
    n_tiles = -(-(t * topk) // tm_rows) + N_EXPERTS
    tile_start = jnp.arange(n_tiles, dtype=jnp.int32) * tm_rows
    tile_expert = jnp.minimum(jnp.sum(tile_start[:, None] >= pad_end[None, :], axis=1),
                              N_EXPERTS - 1).astype(jnp.int32)
    n_active = pad_end[N_EXPERTS - 1:] // tm_rows
    xs = dispatch(dest, pad_end, n_active, hf, n_rows=n_tiles * tm_rows, tm=tm_move, topk=topk, tm_rows=tm_rows)
    y = experts(tile_expert, n_active, xs, w1, w3, w2, tm=tm_rows, tf=tf, fc=256)
    return combine(dest, x1, meta, y, *ple_args, tm=tm_move, topk=topk)


def _tile2(g):
    return jnp.concatenate([g, g]).reshape(1, 2 * g.shape[0])


def kernel(x, p, g_mix, w_in, g_q, g_k, conv_w, conv_b, b_i, b_f, g_h, w_oa, w_ob, w_out, g_ffn, w_d1, w_d3,
           w_d2, w_router, w_e1, w_e3, w_e2, g_ple, w_ple_gate, w_ple_proj):
    batch, seq, d = x.shape
    depth = w_in.shape[0]
    t = batch * seq
    nh = MLSTM_HEADS
    x2 = x.reshape(t, d)
    c_q, c_k, c_v = 0, 512, 1024
    c_qk, c_vm, c_om, c_i, c_f, c_ga, c_gb, c_end = 1536, 2560, 3072, 3584, 3588, 3592, 4616, 5640

    w_in_t = jnp.swapaxes(w_in, 1, 2).astype(BF16)
    for l in range(depth):
        w_bf = w_in_t[l]
        w_gates = w_bf[c_ga:c_end]
        proj, gif = in_proj(x2, g_mix[l].reshape(1, d), w_gates, w_bf, nb_cols=c_i, if_col=c_i,
                            tm=1024, tn=512)

        ya = moba(proj, _tile2(g_q[l]), _tile2(g_k[l]), batch=batch, seq=seq)

        bias = jnp.concatenate([b_i[l], b_f[l]])
        bias_row = jnp.pad(bias, (0, LANES - 2 * nh)).reshape(1, LANES)
        bias_col = bias.reshape(2 * nh, 1)
        gates_row = gif[:, :2 * nh].reshape(batch, seq, 2 * nh).transpose(0, 2, 1)
        yb = mlstm(proj, gif, gates_row, bias_row, bias_col, conv_w[l], conv_b[l].reshape(1, -1),
                   g_h[l].reshape(1, -1), batch=batch, seq=seq, chunk=256)

        j = l // 2
        moe = l % 2 == 1
        wr = jnp.pad(w_router[j], ((0, 0), (0, LANES - N_EXPERTS))) if moe else None
        outs = merge(ya, yb, proj, x2, w_oa[l].astype(BF16), w_ob[l].astype(BF16), w_out[l].astype(BF16),
                     g_ffn[l].reshape(1, d), wr, tm=1024)
        ple_args = (p.reshape(depth * t, -1), l * t, g_ple[l].reshape(1, d), w_ple_gate[l].astype(BF16),
                    w_ple_proj[l].astype(BF16))
        if moe:
            x1, hf, logits = outs
            x2 = moe_ffn(hf, x1, logits, w_e1[j], w_e3[j].astype(BF16), w_e2[j],
                         ple_args, tm_route=512, tm_rows=512, tf=1792, tm_move=512)
        else:
            x1, hf = outs
            x2 = dense_ffn(hf, x1, w_d1[j].astype(BF16), w_d3[j].astype(BF16), w_d2[j].astype(BF16),
                           *ple_args, tm=1024, fc=256)
    return x2.reshape(batch, seq, d)
```

```python
import functools

import jax
import jax.numpy as jnp
from jax import lax
from jax.experimental import pallas as pl
from jax.experimental.pallas import tpu as pltpu

F32 = jnp.float32
BF16 = jnp.bfloat16

RMS_EPS = 1e-6
LANES = 128
SUBLANES = 8

MOBA_HEADS = 8
MOBA_HEAD_DIM = 64
MOBA_BLOCK = 256
MOBA_TOPK = 3
MLSTM_HEADS = 4
MLSTM_DIM = 128
CONV_WIDTH = 4
N_EXPERTS = 8

COL_GA, COL_GB = 0, 8
COL_QA, COL_KA, COL_VA = 16, 20, 24
COL_QM, COL_KM, COL_VM, COL_OM = 28, 32, 36, 40
N_PROJ = 44 * LANES

VMEM_LIMIT = 56 * 1024 * 1024
ROW_DMA_UNROLL = 8


def _params(*sem):
    return pltpu.CompilerParams(dimension_semantics=sem, vmem_limit_bytes=VMEM_LIMIT)


def _sigmoid(x):
    return 1.0 / (1.0 + jnp.exp(-x))


def _rms(x, g):
    return x * lax.rsqrt(jnp.mean(x * x, axis=-1, keepdims=True) + RMS_EPS) * g


def _split_bf16(x):
    hi = x.astype(BF16)
    return hi, (x - hi.astype(F32)).astype(BF16)


def _store_row_tiles(ref, x):
    ref[...] = x.reshape(x.shape[0] * SUBLANES, LANES)


def _load_row_tiles(ref, rows):
    return ref[...].reshape(rows, SUBLANES * LANES)


def _nt_dot(a, b, **kw):
    return lax.dot_general(a, b, (((1,), (1,)), ((), ())), preferred_element_type=F32, **kw)


def _in_proj_body(x_ref, g_ref, wa_ref, wb_ref, o_ref, oif_ref, h_ref, *, nb_cols, if_col, tn):
    h_ref[...] = _rms(x_ref[...], g_ref[...]).astype(BF16)
    oif_ref[...] = _nt_dot(h_ref[...], wb_ref[if_col:if_col + LANES, :])
    na = wa_ref.shape[0]
    for c0 in range(0, na + nb_cols, tn):
        w = wa_ref[c0:c0 + tn, :] if c0 < na else wb_ref[c0 - na:c0 - na + tn, :]
        o_ref[:, c0:c0 + tn] = _nt_dot(h_ref[...], w).astype(o_ref.dtype)


def in_proj(x2, g, wa, w_full, *, nb_cols, if_col, tm, tn):
    t, d = x2.shape
    n = wa.shape[0] + nb_cols
    resident = dict(pipeline_mode=pl.Buffered(1))
    return pl.pallas_call(
        functools.partial(_in_proj_body, nb_cols=nb_cols, if_col=if_col, tn=tn),
        grid=(t // tm,),
        in_specs=[
            pl.BlockSpec((tm, d), lambda m: (m, 0)),
            pl.BlockSpec((1, d), lambda m: (0, 0)),
            pl.BlockSpec(wa.shape, lambda m: (0, 0), **resident),
            pl.BlockSpec(w_full.shape, lambda m: (0, 0), **resident),
        ],
        out_specs=[
            pl.BlockSpec((tm, n), lambda m: (m, 0)),
            pl.BlockSpec((tm, LANES), lambda m: (m, 0)),
        ],
        out_shape=[jax.ShapeDtypeStruct((t, n), BF16), jax.ShapeDtypeStruct((t, LANES), F32)],
        scratch_shapes=[pltpu.VMEM((tm, d), BF16)],
        compiler_params=_params("arbitrary"),
        name="in_proj",
    )(x2, g, wa, w_full)


MASK_BIAS = -1e30
LOG2_E = 1.4426950408889634


def _moba_body(q_ref, k_ref, v_ref, gq_ref, gk_ref, o_ref,
               kn_ref, vt_ref, kmean_ref, qaug_ref, s_ref, m_ref, alpha_ref, acc_ref,
               *, nb, blk, dh, topk, nheads):
    i = pl.program_id(1)
    pair = 2 * blk
    lane = lax.broadcasted_iota(jnp.int32, (1, LANES), 1)
    head0 = lane < dh

    same_head = (lax.broadcasted_iota(jnp.int32, (LANES, LANES), 0) // dh
                 == lax.broadcasted_iota(jnp.int32, (LANES, LANES), 1) // dh)
    head_ones = jnp.where(same_head, 1.0, 0.0).astype(BF16)

    def head_rms(x, g, on_mxu):
        x2 = x * x
        if on_mxu:
            hi, lo = _split_bf16(x2)
            ss = (jnp.dot(hi, head_ones, preferred_element_type=F32)
                  + jnp.dot(lo, head_ones, preferred_element_type=F32))
        else:
            s0 = jnp.sum(jnp.where(head0, x2, 0.0), axis=-1, keepdims=True)
            s1 = jnp.sum(jnp.where(head0, 0.0, x2), axis=-1, keepdims=True)
            ss = jnp.where(head0, s0, s1)
        return x * lax.rsqrt(ss * (1.0 / dh) + RMS_EPS) * g

    @pl.when(i == 0)
    def _():
        def prep(j, c):
            r0 = pl.multiple_of(j * blk, blk)
            onehot = jnp.where(lane == dh + j, 1.0, 0.0)
            for p in range(nheads // 2):
                cols = slice(p * LANES, (p + 1) * LANES)
                kn = head_rms(k_ref[pl.ds(r0, blk), cols].astype(F32), gk_ref[...], True)
                for hh, kh in ((0, kn), (1, pltpu.roll(kn, dh, axis=1))):
                    h = 2 * p + hh
                    kmean_ref[h, pl.ds(j, 1), :] = jnp.mean(jnp.where(head0, kh, 0.0), axis=0, keepdims=True)
                    kn_ref[h, pl.ds(r0, blk), :] = jnp.where(head0, kh, onehot).astype(BF16)
                v_t = v_ref[pl.ds(r0, blk), cols].astype(F32).T.astype(BF16)
                for hh in range(2):
                    vt_ref[2 * p + hh, 0:dh, pl.ds(r0, blk)] = v_t[hh * dh:(hh + 1) * dh, :]
                    vt_ref[2 * p + hh, dh:, pl.ds(r0, blk)] = jnp.ones((vt_ref.shape[1] - dh, blk), BF16)
            return c

        lax.fori_loop(0, nb, prep, 0)

    jidx = lax.broadcasted_iota(jnp.int32, (nb, blk), 0)
    key_i = lax.broadcasted_iota(jnp.int32, (blk, blk), 0)
    qry_i = lax.broadcasted_iota(jnp.int32, (blk, blk), 1)
    causal = key_i <= qry_i
    r_own = pl.multiple_of(i * blk, blk)
    qk_scale = dh ** -0.5 * LOG2_E
    for p in range(nheads // 2):
        cols = slice(p * LANES, (p + 1) * LANES)
        qn_t = head_rms(q_ref[:, cols].astype(F32), gq_ref[...], False).T
        for hh in range(2):
            h = 2 * p + hh
            q_t = qn_t[hh * dh:(hh + 1) * dh, :]
            gate = jnp.dot(kmean_ref[h].astype(BF16),
                           jnp.concatenate([q_t, jnp.zeros((LANES - dh, blk), F32)], axis=0).astype(BF16),
                           preferred_element_type=F32)
            rank = jnp.zeros((nb, blk), F32)
            for jp in range(nb):
                row = gate[jp:jp + 1, :]
                beats = (row > gate) | ((row == gate) & (jidx > jp))
                rank = rank + jnp.where(beats, jnp.where(jp < i, 1.0, 0.0), 0.0)
            sel = (rank < topk) & (jidx < i)
            q_s = q_t * qk_scale
            pad = jnp.zeros((LANES - dh - nb, blk), F32)
            qaug_ref[h] = jnp.concatenate([q_s, jnp.where(sel, 0.0, MASK_BIAS), pad], axis=0).astype(BF16)
            qaug_own = jnp.concatenate([q_s, jnp.where(jidx == i, 0.0, MASK_BIAS), pad], axis=0).astype(BF16)
            st = jnp.dot(kn_ref[h, pl.ds(r_own, blk), :], qaug_own, preferred_element_type=F32)
            st = jnp.where(causal, st, -jnp.inf)
            s_ref[h, 0:blk, :] = st
            m_ref[h] = jnp.max(st, axis=0, keepdims=True)

    def finish_own(h):
        pr = jnp.exp2(s_ref[h, 0:blk, :] - m_ref[h]).astype(BF16)
        acc_ref[h] = jnp.dot(vt_ref[h, :, pl.ds(r_own, blk)], pr, preferred_element_type=F32)

    def score_pair(u, h):
        r0 = pl.multiple_of(u * pair, pair)
        st = jnp.dot(kn_ref[h, pl.ds(r0, pair), :], qaug_ref[h], preferred_element_type=F32)
        m_old = m_ref[h]
        m_new = jnp.maximum(m_old, jnp.max(st, axis=0, keepdims=True))
        s_ref[h] = st
        alpha_ref[h] = jnp.exp2(m_old - m_new)
        m_ref[h] = m_new

    def finish_pair(u, h):
        r0 = pl.multiple_of(u * pair, pair)
        pr = jnp.exp2(s_ref[h] - m_ref[h]).astype(BF16)
        acc_ref[h] = alpha_ref[h] * acc_ref[h] + jnp.dot(vt_ref[h, :, pl.ds(r0, pair)], pr,
                                                         preferred_element_type=F32)

    n_pairs = jnp.maximum((i + 1) // 2, 1)
    for h in range(nheads):
        finish_own(h)
        score_pair(0, h)

    def body(u, c):
        for h in range(nheads):
            finish_pair(u - 1, h)
            score_pair(u, h)
        return c

    lax.fori_loop(1, n_pairs, body, 0)
    for h in range(nheads):
        finish_pair(n_pairs - 1, h)

    for p in range(nheads // 2):
        a0 = acc_ref[2 * p]
        a1 = acc_ref[2 * p + 1]
        ot = jnp.concatenate([a0[0:dh] / a0[dh:dh + 1], a1[0:dh] / a1[dh:dh + 1]], axis=0)
        o_ref[:, p * LANES:(p + 1) * LANES] = ot.T.astype(o_ref.dtype)


def moba(proj, gq2, gk2, *, batch, seq):
    nb = seq // MOBA_BLOCK
    blk = MOBA_BLOCK
    dh = MOBA_HEAD_DIM
    nheads = MOBA_HEADS
    width = nheads * dh
    wb = width // LANES
    assert dh + nb <= LANES and 2 * dh == LANES and nb % 2 == 0
    v_rows = dh + 2 * SUBLANES
    body = functools.partial(_moba_body, nb=nb, blk=blk, dh=dh, topk=MOBA_TOPK, nheads=nheads)
    return pl.pallas_call(
        body,
        grid=(batch, nb),
        in_specs=[
            pl.BlockSpec((blk, width), lambda b, i: (b * nb + i, COL_QA // wb)),
            pl.BlockSpec((seq, width), lambda b, i: (b, COL_KA // wb)),
            pl.BlockSpec((seq, width), lambda b, i: (b, COL_VA // wb)),
            pl.BlockSpec((1, LANES), lambda b, i: (0, 0)),
            pl.BlockSpec((1, LANES), lambda b, i: (0, 0)),
        ],
        out_specs=pl.BlockSpec((blk, width), lambda b, i: (b * nb + i, 0)),
        out_shape=jax.ShapeDtypeStruct((batch * seq, width), BF16),
        scratch_shapes=[
            pltpu.VMEM((nheads, seq, LANES), BF16),
            pltpu.VMEM((nheads, v_rows, seq), BF16),
            pltpu.VMEM((nheads, nb, LANES), F32),
            pltpu.VMEM((nheads, LANES, blk), BF16),
            pltpu.VMEM((nheads, 2 * blk, blk), F32),
            pltpu.VMEM((nheads, 1, blk), F32),
            pltpu.VMEM((nheads, 1, blk), F32),
            pltpu.VMEM((nheads, v_rows, blk), F32),
        ],
        compiler_params=_params("arbitrary", "arbitrary"),
        name="moba",
    )(proj, proj, proj, gq2, gk2)


def _log_sigmoid(x):
    return jnp.minimum(x, 0.0) - jnp.log(1.0 + jnp.exp(-jnp.abs(x)))


def _dot_tri(tri, x, tri_left):
    out = None
    for _ in range(3):
        piece = x.astype(BF16)
        x = x - piece.astype(F32)
        term = (jnp.dot(tri, piece, preferred_element_type=F32) if tri_left
                else jnp.dot(piece, tri, preferred_element_type=F32))
        out = term if out is None else out + term
    return out


def _mlstm_body(qr_ref, kr_ref, v_ref, og_ref, gcol_ref, grow_ref, brow_ref, bcol_ref,
                cwq_ref, cwk_ref, cbq_ref, cbk_ref, gh_ref, o_ref,
                qx_ref, kx_ref, c_ref, m_ref, *, chunk, dk, nh):
    L = chunk
    width = nh * dk

    @pl.when(pl.program_id(1) == 0)
    def _():
        qx_ref[0:SUBLANES, :] = jnp.zeros((SUBLANES, width), F32)
        kx_ref[0:SUBLANES, :] = jnp.zeros((SUBLANES, width), F32)
        c_ref[...] = jnp.zeros_like(c_ref)
        m_ref[...] = jnp.zeros_like(m_ref)

    qx_ref[SUBLANES:SUBLANES + L, :] = qr_ref[...].astype(F32)
    kx_ref[SUBLANES:SUBLANES + L, :] = kr_ref[...].astype(F32)

    def conv_silu(x_ref, w_ref, b_ref):
        acc = b_ref[...] + w_ref[0:1, :] * x_ref[pl.ds(SUBLANES - CONV_WIDTH + 1, L), :]
        for j in range(1, CONV_WIDTH):
            acc = acc + w_ref[j:j + 1, :] * x_ref[pl.ds(SUBLANES - CONV_WIDTH + 1 + j, L), :]
        return acc * _sigmoid(acc)

    q_all = conv_silu(qx_ref, cwq_ref, cbq_ref)
    k_all = conv_silu(kx_ref, cwk_ref, cbk_ref) * (dk ** -0.5)
    qx_ref[0:SUBLANES, :] = qx_ref[L:L + SUBLANES, :]
    kx_ref[0:SUBLANES, :] = kx_ref[L:L + SUBLANES, :]

    pre_col = gcol_ref[...] + brow_ref[...]
    pre_row = grow_ref[0] + bcol_ref[...]
    t_i = lax.broadcasted_iota(jnp.int32, (L, L), 0)
    s_i = lax.broadcasted_iota(jnp.int32, (L, L), 1)
    tril = s_i <= t_i
    bcum_cols = _dot_tri(jnp.where(tril, 1.0, 0.0).astype(BF16), _log_sigmoid(pre_col), True)
    bcum_rows = _dot_tri(jnp.where(t_i <= s_i, 1.0, 0.0).astype(BF16), _log_sigmoid(pre_row), False)
    ones = jnp.ones((L, dk), BF16)

    for h in range(nh):
        cols = slice(h * dk, (h + 1) * dk)
        q = q_all[:, cols]
        k = k_all[:, cols]
        i_col = pre_col[:, h:h + 1]
        i_row = pre_row[h:h + 1, :]
        bcum_col = bcum_cols[:, nh + h:nh + h + 1]
        bcum_row = bcum_rows[nh + h:nh + h + 1, :]

        m_prev = m_ref[h, 0:1, 0:1]
        a_col = bcum_col + m_prev
        dmat = jnp.where(tril, bcum_col - bcum_row + i_row, -jnp.inf)
        m_t = jnp.maximum(a_col, jnp.max(dmat, axis=-1, keepdims=True))
        dw = jnp.exp(dmat - m_t)
        aw = jnp.exp(a_col - m_t)

        qb = q.astype(BF16)
        kb = k.astype(BF16)
        v_aug = jnp.concatenate([v_ref[:, cols], ones], axis=-1)
        sqk = _nt_dot(qb, kb) * dw
        num_aug = (aw * jnp.dot(qb, c_ref[h].astype(BF16), preferred_element_type=F32)
                   + jnp.dot(sqk.astype(BF16), v_aug, preferred_element_type=F32))
        den = num_aug[:, dk:dk + 1]
        hc = num_aug[:, 0:dk] / jnp.maximum(jnp.abs(den), jnp.exp(-m_t))

        b_last = bcum_col[L - 1:L, :]
        g_col = b_last - bcum_col + i_col
        m_new = jnp.maximum(b_last + m_prev, jnp.max(g_col, axis=0, keepdims=True))
        w_c = jnp.exp(b_last + m_prev - m_new)
        kw_t = (k * jnp.exp(g_col - m_new)).T.astype(BF16)
        c_ref[h] = w_c * c_ref[h] + jnp.dot(kw_t, v_aug, preferred_element_type=F32)
        m_ref[h] = jnp.broadcast_to(m_new, (1, LANES))

        o_ref[:, cols] = (_rms(hc, gh_ref[...]) * _sigmoid(og_ref[:, cols].astype(F32))).astype(o_ref.dtype)


def mlstm(proj, gates_col, gates_row, bias_row, bias_col, conv_w, conv_b, gh, *, batch, seq, chunk):
    nh = MLSTM_HEADS
    dk = MLSTM_DIM
    width = nh * dk
    wb = width // LANES
    nc = seq // chunk
    body = functools.partial(_mlstm_body, chunk=chunk, dk=dk, nh=nh)

    def rows(col0):
        return pl.BlockSpec((chunk, width), lambda b, c: (b * nc + c, col0 // wb))

    return pl.pallas_call(
        body,
        grid=(batch, nc),
        in_specs=[
            rows(COL_QM), rows(COL_KM), rows(COL_VM), rows(COL_OM),
            pl.BlockSpec((chunk, LANES), lambda b, c: (b * nc + c, 0)),
            pl.BlockSpec((1, SUBLANES, chunk), lambda b, c: (b, 0, c)),
            pl.BlockSpec((1, LANES), lambda b, c: (0, 0)),
            pl.BlockSpec((SUBLANES, 1), lambda b, c: (0, 0)),
            pl.BlockSpec((CONV_WIDTH, width), lambda b, c: (0, 0)),
            pl.BlockSpec((CONV_WIDTH, width), lambda b, c: (0, 1)),
            pl.BlockSpec((1, width), lambda b, c: (0, 0)),
            pl.BlockSpec((1, width), lambda b, c: (0, 1)),
            pl.BlockSpec((1, LANES), lambda b, c: (0, 0)),
        ],
        out_specs=pl.BlockSpec((chunk, width), lambda b, c: (b * nc + c, 0)),
        out_shape=jax.ShapeDtypeStruct((batch * seq, width), BF16),
        scratch_shapes=[
            pltpu.VMEM((chunk + 2 * SUBLANES, width), F32),
            pltpu.VMEM((chunk + 2 * SUBLANES, width), F32),
            pltpu.VMEM((nh, dk, 2 * dk), F32),
            pltpu.VMEM((nh, 1, LANES), F32),
        ],
        compiler_params=_params("arbitrary", "arbitrary"),
        name="mlstm",
    )(proj, proj, proj, proj, gates_col, gates_row, bias_row, bias_col,
      conv_w, conv_w, conv_b, conv_b, gh)


def _merge_body(ya_ref, yb_ref, ga_ref, gb_ref, x_ref, woa_ref, wob_ref, wout_ref, gffn_ref, *rest, moe):
    a = jnp.dot(ya_ref[...], woa_ref[...], preferred_element_type=F32)
    b = jnp.dot(yb_ref[...], wob_ref[...], preferred_element_type=F32)
    mixed = _sigmoid(ga_ref[...].astype(F32)) * a + _sigmoid(gb_ref[...].astype(F32)) * b
    x1 = x_ref[...] + jnp.dot(mixed.astype(BF16), wout_ref[...], preferred_element_type=F32)
    hf = _rms(x1, gffn_ref[...])
    if moe:
        wr_ref, x1_ref, hf_ref, lg_ref = rest
        _store_row_tiles(hf_ref, hf)
        lg_ref[...] = jnp.dot(hf.astype(BF16), wr_ref[...].astype(BF16), preferred_element_type=F32)
    else:
        x1_ref, hf_ref = rest
        hf_ref[...] = hf.astype(BF16)
    x1_ref[...] = x1


def merge(ya, yb, proj, x2, woa, wob, wout, gffn, wr, *, tm):
    t, d = x2.shape
    moe = wr is not None
    full = lambda m: (0, 0)
    in_specs = [
        pl.BlockSpec((tm, ya.shape[1]), lambda m: (m, 0)),
        pl.BlockSpec((tm, yb.shape[1]), lambda m: (m, 0)),
        pl.BlockSpec((tm, d), lambda m: (m, COL_GA * LANES // d)),
        pl.BlockSpec((tm, d), lambda m: (m, COL_GB * LANES // d)),
        pl.BlockSpec((tm, d), lambda m: (m, 0)),
        pl.BlockSpec(woa.shape, full), pl.BlockSpec(wob.shape, full), pl.BlockSpec(wout.shape, full),
        pl.BlockSpec((1, d), full),
    ]
    args = [ya, yb, proj, proj, x2, woa, wob, wout, gffn]
    out_specs = [pl.BlockSpec((tm, d), lambda m: (m, 0)), pl.BlockSpec((tm, d), lambda m: (m, 0))]
    out_shape = [jax.ShapeDtypeStruct((t, d), F32), jax.ShapeDtypeStruct((t, d), BF16)]
    if moe:
        out_specs[1] = pl.BlockSpec((tm * SUBLANES, LANES), lambda m: (m, 0))
        out_shape[1] = jax.ShapeDtypeStruct((t * SUBLANES, LANES), F32)
    if moe:
        in_specs.append(pl.BlockSpec(wr.shape, full))
        args.append(wr)
        out_specs.append(pl.BlockSpec((tm, LANES), lambda m: (m, 0)))
        out_shape.append(jax.ShapeDtypeStruct((t, LANES), F32))
    return pl.pallas_call(
        functools.partial(_merge_body, moe=moe),
        grid=(t // tm,),
        in_specs=in_specs, out_specs=out_specs, out_shape=out_shape,
        compiler_params=_params("arbitrary"),
        name="merge_moe" if moe else "merge",
    )(*args)


def _swiglu_gate(x, w1_ref, w3_ref, g_ref, fc):
    dff = g_ref.shape[1]
    for f0 in range(0, dff, fc):
        a = jnp.dot(x, w1_ref[:, f0:f0 + fc].astype(BF16), preferred_element_type=F32)
        b = jnp.dot(x, w3_ref[:, f0:f0 + fc].astype(BF16), preferred_element_type=F32)
        g_ref[:, f0:f0 + fc] = (a * _sigmoid(a) * b).astype(BF16)


def _swiglu(x, w1_ref, w3_ref, w2_ref, g_ref, fc):
    _swiglu_gate(x, w1_ref, w3_ref, g_ref, fc)
    return jnp.dot(g_ref[...], w2_ref[...].astype(BF16), preferred_element_type=F32)


def _ple(x, p_ref, g_ref, wg_ref, wp_ref):
    gate = _sigmoid(jnp.dot(_rms(x, g_ref[...]).astype(BF16), wg_ref[...], preferred_element_type=F32))
    emb = jnp.dot(p_ref[...].astype(BF16), wp_ref[...], preferred_element_type=F32)
    return x + gate * emb


def _ffn_body(hf_ref, x1_ref, w1_ref, w3_ref, w2_ref, p_ref, g_ref, wg_ref, wp_ref, o_ref, act_ref, *, fc):
    x2 = x1_ref[...] + _swiglu(hf_ref[...], w1_ref, w3_ref, w2_ref, act_ref, fc)
    o_ref[...] = _ple(x2, p_ref, g_ref, wg_ref, wp_ref)


def dense_ffn(hf, x1, w1, w3, w2, p2, p_row0, g, wg, wp, *, tm, fc):
    t, d = x1.shape
    dff = w1.shape[1]
    p_blk0 = p_row0 // tm
    resident = dict(pipeline_mode=pl.Buffered(1))
    full = lambda m: (0, 0)
    return pl.pallas_call(
        functools.partial(_ffn_body, fc=fc),
        grid=(t // tm,),
        in_specs=[
            pl.BlockSpec((tm, d), lambda m: (m, 0)),
            pl.BlockSpec((tm, d), lambda m: (m, 0)),
            pl.BlockSpec((d, dff), full, **resident),
            pl.BlockSpec((d, dff), full, **resident),
            pl.BlockSpec((dff, d), full, **resident),
            pl.BlockSpec((tm, p2.shape[1]), lambda m: (p_blk0 + m, 0)),
            pl.BlockSpec((1, d), full),
            pl.BlockSpec(wg.shape, full, **resident),
            pl.BlockSpec(wp.shape, full, **resident),
        ],
        out_specs=pl.BlockSpec((tm, d), lambda m: (m, 0)),
        out_shape=jax.ShapeDtypeStruct((t, d), F32),
        scratch_shapes=[pltpu.VMEM((tm, dff), BF16)],
        compiler_params=_params("arbitrary"),
        name="dense_ffn",
    )(hf, x1, w1, w3, w2, p2, g, wg, wp)


META_E0, META_E1, META_G0, META_G1, META_R0, META_R1 = 0, 1, 2, 3, 4, 5


def _route_body(lg_ref, meta_ref, tab_ref, cnt_ref, carry_ref, *, tm, ne):
    @pl.when(pl.program_id(0) == 0)
    def _():
        carry_ref[...] = jnp.zeros_like(carry_ref)

    lane = lax.broadcasted_iota(jnp.int32, (tm, LANES), 1)
    lanef = lane.astype(F32)
    lg = jnp.where(lane < ne, lg_ref[...], -jnp.inf)
    m1 = jnp.max(lg, axis=-1, keepdims=True)
    e1 = jnp.min(jnp.where(lg == m1, lanef, float(LANES)), axis=-1, keepdims=True)
    lg2 = jnp.where(lanef == e1, -jnp.inf, lg)
    m2 = jnp.max(lg2, axis=-1, keepdims=True)
    e2 = jnp.min(jnp.where(lg2 == m2, lanef, float(LANES)), axis=-1, keepdims=True)
    ex = jnp.exp(m2 - m1)
    g1 = 1.0 / (1.0 + ex)
    g2 = ex / (1.0 + ex)
    onehot = jnp.where((lanef == e1) | (lanef == e2), 1.0, 0.0)
    row = lax.broadcasted_iota(jnp.int32, (tm, tm), 0)
    col = lax.broadcasted_iota(jnp.int32, (tm, tm), 1)
    before = jnp.dot(jnp.where(col < row, 1.0, 0.0).astype(BF16), onehot.astype(BF16),
                     preferred_element_type=F32) + carry_ref[...]
    r1 = jnp.sum(jnp.where(lanef == e1, before, 0.0), axis=-1, keepdims=True)
    r2 = jnp.sum(jnp.where(lanef == e2, before, 0.0), axis=-1, keepdims=True)
    carry_ref[...] += jnp.sum(onehot, axis=0, keepdims=True)
    meta = jnp.zeros((tm, LANES), F32)
    for pos, val in ((META_E0, e1), (META_E1, e2), (META_G0, g1), (META_G1, g2), (META_R0, r1), (META_R1, r2)):
        meta = jnp.where(lane == pos, val, meta)
    meta_ref[...] = meta
    tab_ref[...] = meta.T[0:SUBLANES, :]
    cnt_ref[...] = jnp.broadcast_to(carry_ref[...], cnt_ref.shape)


def route(logits, *, tm):
    t = logits.shape[0]
    return pl.pallas_call(
        functools.partial(_route_body, tm=tm, ne=N_EXPERTS),
        grid=(t // tm,),
        in_specs=[pl.BlockSpec((tm, LANES), lambda m: (m, 0))],
        out_specs=[pl.BlockSpec((tm, LANES), lambda m: (m, 0)),
                   pl.BlockSpec((SUBLANES, tm), lambda m: (0, m)),
                   pl.BlockSpec((SUBLANES, LANES), lambda m: (0, 0))],
        out_shape=[jax.ShapeDtypeStruct((t, LANES), F32), jax.ShapeDtypeStruct((SUBLANES, t), F32),
                   jax.ShapeDtypeStruct((SUBLANES, LANES), F32)],
        scratch_shapes=[pltpu.VMEM((1, LANES), F32)],
        compiler_params=_params("arbitrary"),
        name="route",
    )(logits)


def _dispatch_body(dest_ref, pe_ref, na_ref, hf_ref, xs_ref, zero_ref, sem, zsem, *, tm, topk, tm_rows, n_tok):
    base = pl.program_id(0) * tm

    @pl.when(pl.program_id(0) == 0)
    def _():
        zero_ref[...] = jnp.zeros_like(zero_ref)
        n_tiles = xs_ref.shape[0] // (tm_rows * SUBLANES)
        fills = []
        for e in range(N_EXPERTS):
            end = pe_ref[e]
            nonempty = end > (pe_ref[e - 1] if e else 0)
            fills.append((nonempty, pl.multiple_of(jnp.maximum(end - tm_rows, 0), tm_rows)))
        for tile in range(n_tiles):
            fills.append((tile >= na_ref[0], tile * tm_rows))
        for phase in ("start", "wait"):
            for cond, row0 in fills:
                @pl.when(cond)
                def _(row0=row0, phase=phase):
                    cp = pltpu.make_async_copy(
                        zero_ref, xs_ref.at[pl.ds(row0 * SUBLANES, tm_rows * SUBLANES)], zsem)
                    cp.start() if phase == "start" else cp.wait()

    def issue(r, c):
        for k in range(topk):
            d = dest_ref[k * n_tok + base + r]
            pltpu.make_async_copy(_row_tile(hf_ref, r), _row_tile(xs_ref, d), sem).start(priority=k % 2)
        return c

    lax.fori_loop(0, tm, issue, 0, unroll=ROW_DMA_UNROLL)
    for k in range(topk):
        pltpu.make_async_copy(hf_ref, xs_ref.at[pl.ds(0, tm * SUBLANES)], sem).wait()


def _row_tile(ref, r):
    return ref.at[pl.ds(pl.multiple_of(r * SUBLANES, SUBLANES), SUBLANES)]


def dispatch(dest, pad_end, n_active, hf, *, n_rows, tm, topk, tm_rows):
    t = hf.shape[0] // SUBLANES
    return pl.pallas_call(
        functools.partial(_dispatch_body, tm=tm, topk=topk, tm_rows=tm_rows, n_tok=t),
        grid_spec=pltpu.PrefetchScalarGridSpec(
            num_scalar_prefetch=3,
            grid=(t // tm,),
            in_specs=[pl.BlockSpec((tm * SUBLANES, LANES), lambda m, *_: (m, 0))],
            out_specs=pl.BlockSpec(memory_space=pl.ANY),
            scratch_shapes=[pltpu.VMEM((tm_rows * SUBLANES, LANES), F32),
                            pltpu.SemaphoreType.DMA, pltpu.SemaphoreType.DMA],
        ),
        out_shape=jax.ShapeDtypeStruct((n_rows * SUBLANES, LANES), F32),
        compiler_params=_params("arbitrary"),
        name="moe_dispatch",
    )(dest, pad_end, n_active, hf)


def _experts_body(te_ref, na_ref, xs_ref, w1_ref, w3_ref, w2_ref, y_ref, xb_ref, g_ref, acc_ref, *, fc):
    del te_ref
    i = pl.program_id(0)
    f = pl.program_id(1)
    last = pl.num_programs(1) - 1
    tm = xb_ref.shape[0]

    @pl.when(i >= na_ref[0])
    def _():
        y_ref[...] = jnp.zeros_like(y_ref)

    @pl.when(i < na_ref[0])
    def _():
        @pl.when(f == 0)
        def _():
            xb_ref[...] = _load_row_tiles(xs_ref, tm).astype(BF16)

        y = _swiglu(xb_ref[...], w1_ref.at[0], w3_ref.at[0], w2_ref.at[0], g_ref, fc)

        @pl.when(f == 0)
        def _():
            acc_ref[...] = y

        @pl.when((f > 0) & (f < last))
        def _():
            acc_ref[...] += y

        @pl.when(f == last)
        def _():
            _store_row_tiles(y_ref, acc_ref[...] + y)


def experts(tile_expert, n_active, xs, w1, w3, w2, *, tm, tf, fc):
    n_rows = xs.shape[0] // SUBLANES
    d = w1.shape[1]
    dff = w1.shape[2]
    assert dff // tf >= 2
    row_tile = lambda i, f, te, na: (jnp.minimum(i, na[0] - 1), 0)
    ftile = lambda i, f, na: jnp.where(i < na[0], f, dff // tf - 1)
    return pl.pallas_call(
        functools.partial(_experts_body, fc=fc),
        grid_spec=pltpu.PrefetchScalarGridSpec(
            num_scalar_prefetch=2,
            grid=(n_rows // tm, dff // tf),
            in_specs=[
                pl.BlockSpec((tm * SUBLANES, LANES), row_tile),
                pl.BlockSpec((1, d, tf), lambda i, f, te, na: (te[i], 0, ftile(i, f, na))),
                pl.BlockSpec((1, d, tf), lambda i, f, te, na: (te[i], 0, ftile(i, f, na))),
                pl.BlockSpec((1, tf, d), lambda i, f, te, na: (te[i], ftile(i, f, na), 0)),
            ],
            out_specs=pl.BlockSpec((tm * SUBLANES, LANES), lambda i, f, te, na: (i, 0)),
            scratch_shapes=[pltpu.VMEM((tm, d), BF16), pltpu.VMEM((tm, tf), BF16), pltpu.VMEM((tm, d), F32)],
        ),
        out_shape=jax.ShapeDtypeStruct((n_rows * SUBLANES, LANES), F32),
        compiler_params=_params("arbitrary", "arbitrary"),
        name="moe_experts",
    )(tile_expert, n_active, xs, w1, w3, w2)


def _combine_body(dest_ref, x1_ref, meta_ref, p_ref, g_ref, wg_ref, wp_ref, y_ref, o_ref, buf_ref, sem,
                  *, tm, topk):
    base = pl.program_id(0) * tm
    n_tok = pl.num_programs(0) * tm

    def issue(r, c):
        for k in range(topk):
            d = dest_ref[k * n_tok + base + r]
            pltpu.make_async_copy(_row_tile(y_ref, d), _row_tile(buf_ref.at[k], r), sem).start(priority=k % 2)
        return c

    lax.fori_loop(0, tm, issue, 0, unroll=ROW_DMA_UNROLL)
    for k in range(topk):
        pltpu.make_async_copy(y_ref.at[pl.ds(0, tm * SUBLANES)], buf_ref.at[k], sem).wait()
    meta = meta_ref[...]
    g0 = meta[:, META_G0:META_G0 + 1]
    g1 = meta[:, META_G1:META_G1 + 1]
    x2 = x1_ref[...] + (g0 * _load_row_tiles(buf_ref.at[0], tm) + g1 * _load_row_tiles(buf_ref.at[1], tm))
    o_ref[...] = _ple(x2, p_ref, g_ref, wg_ref, wp_ref)


def combine(dest, x1, meta, y, p2, p_row0, g, wg, wp, *, tm, topk):
    t, d = x1.shape
    full = lambda m, dest: (0, 0)
    p_blk0 = p_row0 // tm
    return pl.pallas_call(
        functools.partial(_combine_body, tm=tm, topk=topk),
        grid_spec=pltpu.PrefetchScalarGridSpec(
            num_scalar_prefetch=1,
            grid=(t // tm,),
            in_specs=[pl.BlockSpec((tm, d), lambda m, dest: (m, 0)),
                      pl.BlockSpec((tm, LANES), lambda m, dest: (m, 0)),
                      pl.BlockSpec((tm, p2.shape[1]), lambda m, dest: (p_blk0 + m, 0)),
                      pl.BlockSpec((1, d), full), pl.BlockSpec(wg.shape, full), pl.BlockSpec(wp.shape, full),
                      pl.BlockSpec(memory_space=pl.ANY)],
            out_specs=pl.BlockSpec((tm, d), lambda m, dest: (m, 0)),
            scratch_shapes=[pltpu.VMEM((topk, tm * SUBLANES, LANES), F32), pltpu.SemaphoreType.DMA],
        ),
        out_shape=jax.ShapeDtypeStruct((t, d), F32),
        compiler_params=_params("arbitrary"),
        name="moe_combine",
    )(dest, x1, meta, p2, g, wg, wp, y)


def moe_ffn(hf, x1, logits, w1, w3, w2, ple_args, *, tm_route, tm_rows, tf, tm_move):
    t, d = x1.shape
    topk = 2
    meta, tab, cnt = route(logits, tm=tm_route)
    counts = cnt[0, :N_EXPERTS].astype(jnp.int32)
    padded = ((counts + tm_rows - 1) // tm_rows) * tm_rows
    pad_end = jnp.cumsum(padded).astype(jnp.int32)
    pad_start = pad_end - padded
    eidx = tab[META_E0:META_E1 + 1].astype(jnp.int32)
    rank = tab[META_R0:META_R1 + 1].astype(jnp.int32)
    dest = rank
    for e in range(N_EXPERTS):
        dest = dest + jnp.where(eidx == e, pad_start[e], 0)
    dest = dest.reshape(topk * t)
    n_tiles = -(-(t * topk) // tm_rows) + N_EXPERTS
    tile_start = jnp.arange(n_tiles, dtype=jnp.int32) * tm_rows
    tile_expert = jnp.minimum(jnp.sum(tile_start[:, None] >= pad_end[None, :], axis=1),
                              N_EXPERTS - 1).astype(jnp.int32)
    n_active = pad_end[N_EXPERTS - 1:] // tm_rows
    xs = dispatch(dest, pad_end, n_active, hf, n_rows=n_tiles * tm_rows, tm=tm_move, topk=topk, tm_rows=tm_rows)
    y = experts(tile_expert, n_active, xs, w1, w3, w2, tm=tm_rows, tf=tf, fc=256)
    return combine(dest, x1, meta, y, *ple_args, tm=tm_move, topk=topk)


def _tile2(g):
    return jnp.concatenate([g, g]).reshape(1, 2 * g.shape[0])


def kernel(x, p, g_mix, w_in, g_q, g_k, conv_w, conv_b, b_i, b_f, g_h, w_oa, w_ob, w_out, g_ffn, w_d1, w_d3,
           w_d2, w_router, w_e1, w_e3, w_e2, g_ple, w_ple_gate, w_ple_proj):
    batch, seq, d = x.shape
    depth = w_in.shape[0]
    t = batch * seq
    nh = MLSTM_HEADS
    x2 = x.reshape(t, d)
    c_q, c_k, c_v = 0, 512, 1024
    c_qk, c_vm, c_om, c_i, c_f, c_ga, c_gb, c_end = 1536, 2560, 3072, 3584, 3588, 3592, 4616, 5640

    w_in_t = jnp.swapaxes(w_in, 1, 2).astype(BF16)
    for l in range(depth):
        w_bf = w_in_t[l]
        w_gates = w_bf[c_ga:c_end]
        proj, gif = in_proj(x2, g_mix[l].reshape(1, d), w_gates, w_bf, nb_cols=c_i, if_col=c_i,
                            tm=1024, tn=512)

        ya = moba(proj, _tile2(g_q[l]), _tile2(g_k[l]), batch=batch, seq=seq)

        bias = jnp.concatenate([b_i[l], b_f[l]])
        bias_row = jnp.pad(bias, (0, LANES - 2 * nh)).reshape(1, LANES)
        bias_col = bias.reshape(2 * nh, 1)
        gates_row = gif[:, :2 * nh].reshape(batch, seq, 2 * nh).transpose(0, 2, 1)
        yb = mlstm(proj, gif, gates_row, bias_row, bias_col, conv_w[l], conv_b[l].reshape(1, -1),
                   g_h[l].reshape(1, -1), batch=batch, seq=seq, chunk=256)

        j = l // 2
        moe = l % 2 == 1
        wr = jnp.pad(w_router[j], ((0, 0), (0, LANES - N_EXPERTS))) if moe else None
        outs = merge(ya, yb, proj, x2, w_oa[l].astype(BF16), w_ob[l].astype(BF16), w_out[l].astype(BF16),
                     g_ffn[l].reshape(1, d), wr, tm=1024)
        ple_args = (p.reshape(depth * t, -1), l * t, g_ple[l].reshape(1, d), w_ple_gate[l].astype(BF16),
                    w_ple_proj[l].astype(BF16))
        if moe:
            x1, hf, logits = outs
            x2 = moe_ffn(hf, x1, logits, w_e1[j], w_e3[j].astype(BF16), w_e2[j],
                         ple_args, tm_route=512, tm_rows=512, tf=1792, tm_move=512)
        else:
            x1, hf = outs
            x2 = dense_ffn(hf, x1, w_d1[j].astype(BF16), w_d3[j].astype(BF16), w_d2[j].astype(BF16),
                           *ple_args, tm=1024, fc=256)
    return x2.reshape(batch, seq, d)
```

```python
import functools

import jax
import jax.numpy as jnp
from jax import lax
from jax.experimental import pallas as pl
from jax.experimental.pallas import tpu as pltpu

F32 = jnp.float32
BF16 = jnp.bfloat16

RMS_EPS = 1e-6
LANES = 128
SUBLANES = 8

MOBA_HEADS = 8
MOBA_HEAD_DIM = 64
MOBA_BLOCK = 256
MOBA_TOPK = 3
MLSTM_HEADS = 4
MLSTM_DIM = 128
CONV_WIDTH = 4
N_EXPERTS = 8

COL_GA, COL_GB = 0, 8
COL_QA, COL_KA, COL_VA = 16, 20, 24
COL_QM, COL_KM, COL_VM, COL_OM = 28, 32, 36, 40
N_PROJ = 44 * LANES

VMEM_LIMIT = 56 * 1024 * 1024
ROW_DMA_UNROLL = 16


def _params(*sem):
    return pltpu.CompilerParams(dimension_semantics=sem, vmem_limit_bytes=VMEM_LIMIT)


def _sigmoid(x):
    return 1.0 / (1.0 + jnp.exp(-x))


def _rms(x, g):
    return x * lax.rsqrt(jnp.mean(x * x, axis=-1, keepdims=True) + RMS_EPS) * g


def _split_bf16(x):
    hi = x.astype(BF16)
    return hi, (x - hi.astype(F32)).astype(BF16)


def _store_row_tiles(ref, x):
    ref[...] = x.reshape(x.shape[0] * SUBLANES, LANES)


def _load_row_tiles(ref, rows):
    return ref[...].reshape(rows, SUBLANES * LANES)


def _nt_dot(a, b, **kw):
    return lax.dot_general(a, b, (((1,), (1,)), ((), ())), preferred_element_type=F32, **kw)


def _in_proj_body(x_ref, g_ref, wa_ref, wb_ref, o_ref, oif_ref, h_ref, *, nb_cols, if_col, tn):
    h_ref[...] = _rms(x_ref[...], g_ref[...]).astype(BF16)
    oif_ref[...] = _nt_dot(h_ref[...], wb_ref[if_col:if_col + LANES, :])
    na = wa_ref.shape[0]
    for c0 in range(0, na + nb_cols, tn):
        w = wa_ref[c0:c0 + tn, :] if c0 < na else wb_ref[c0 - na:c0 - na + tn, :]
        o_ref[:, c0:c0 + tn] = _nt_dot(h_ref[...], w).astype(o_ref.dtype)


def in_proj(x2, g, wa, w_full, *, nb_cols, if_col, tm, tn):
    t, d = x2.shape
    n = wa.shape[0] + nb_cols
    resident = dict(pipeline_mode=pl.Buffered(1))
    return pl.pallas_call(
        functools.partial(_in_proj_body, nb_cols=nb_cols, if_col=if_col, tn=tn),
        grid=(t // tm,),
        in_specs=[
            pl.BlockSpec((tm, d), lambda m: (m, 0)),
            pl.BlockSpec((1, d), lambda m: (0, 0)),
            pl.BlockSpec(wa.shape, lambda m: (0, 0), **resident),
            pl.BlockSpec(w_full.shape, lambda m: (0, 0), **resident),
        ],
        out_specs=[
            pl.BlockSpec((tm, n), lambda m: (m, 0)),
            pl.BlockSpec((tm, LANES), lambda m: (m, 0)),
        ],
        out_shape=[jax.ShapeDtypeStruct((t, n), BF16), jax.ShapeDtypeStruct((t, LANES), F32)],
        scratch_shapes=[pltpu.VMEM((tm, d), BF16)],
        compiler_params=_params("arbitrary"),
        name="in_proj",
    )(x2, g, wa, w_full)


MASK_BIAS = -1e30
LOG2_E = 1.4426950408889634


def _moba_body(q_ref, k_ref, v_ref, gq_ref, gk_ref, o_ref,
               kn_ref, vt_ref, kmean_ref, qaug_ref, s_ref, m_ref, alpha_ref, acc_ref,
               *, nb, blk, dh, topk, nheads):
    i = pl.program_id(1)
    pair = 2 * blk
    lane = lax.broadcasted_iota(jnp.int32, (1, LANES), 1)
    head0 = lane < dh

    same_head = (lax.broadcasted_iota(jnp.int32, (LANES, LANES), 0) // dh
                 == lax.broadcasted_iota(jnp.int32, (LANES, LANES), 1) // dh)
    head_ones = jnp.where(same_head, 1.0, 0.0).astype(BF16)

    def head_rms(x, g, on_mxu):
        x2 = x * x
        if on_mxu:
            hi, lo = _split_bf16(x2)
            ss = (jnp.dot(hi, head_ones, preferred_element_type=F32)
                  + jnp.dot(lo, head_ones, preferred_element_type=F32))
        else:
            s0 = jnp.sum(jnp.where(head0, x2, 0.0), axis=-1, keepdims=True)
            s1 = jnp.sum(jnp.where(head0, 0.0, x2), axis=-1, keepdims=True)
            ss = jnp.where(head0, s0, s1)
        return x * lax.rsqrt(ss * (1.0 / dh) + RMS_EPS) * g

    @pl.when(i == 0)
    def _():
        def prep(j, c):
            r0 = pl.multiple_of(j * blk, blk)
            onehot = jnp.where(lane == dh + j, 1.0, 0.0)
            for p in range(nheads // 2):
                cols = slice(p * LANES, (p + 1) * LANES)
                kn = head_rms(k_ref[pl.ds(r0, blk), cols].astype(F32), gk_ref[...], True)
                for hh, kh in ((0, kn), (1, pltpu.roll(kn, dh, axis=1))):
                    h = 2 * p + hh
                    kmean_ref[h, pl.ds(j, 1), :] = jnp.mean(jnp.where(head0, kh, 0.0), axis=0, keepdims=True)
                    kn_ref[h, pl.ds(r0, blk), :] = jnp.where(head0, kh, onehot).astype(BF16)
                v_t = v_ref[pl.ds(r0, blk), cols].astype(F32).T.astype(BF16)
                for hh in range(2):
                    vt_ref[2 * p + hh, 0:dh, pl.ds(r0, blk)] = v_t[hh * dh:(hh + 1) * dh, :]
                    vt_ref[2 * p + hh, dh:, pl.ds(r0, blk)] = jnp.ones((vt_ref.shape[1] - dh, blk), BF16)
            return c

        lax.fori_loop(0, nb, prep, 0)

    jidx = lax.broadcasted_iota(jnp.int32, (nb, blk), 0)
    key_i = lax.broadcasted_iota(jnp.int32, (blk, blk), 0)
    qry_i = lax.broadcasted_iota(jnp.int32, (blk, blk), 1)
    causal = key_i <= qry_i
    r_own = pl.multiple_of(i * blk, blk)
    qk_scale = dh ** -0.5 * LOG2_E
    for p in range(nheads // 2):
        cols = slice(p * LANES, (p + 1) * LANES)
        qn_t = head_rms(q_ref[:, cols].astype(F32), gq_ref[...], False).T
        for hh in range(2):
            h = 2 * p + hh
            q_t = qn_t[hh * dh:(hh + 1) * dh, :]
            gate = jnp.dot(kmean_ref[h].astype(BF16),
                           jnp.concatenate([q_t, jnp.zeros((LANES - dh, blk), F32)], axis=0).astype(BF16),
                           preferred_element_type=F32)
            rank = jnp.zeros((nb, blk), F32)
            for jp in range(nb):
                row = gate[jp:jp + 1, :]
                beats = (row > gate) | ((row == gate) & (jidx > jp))
                rank = rank + jnp.where(beats, jnp.where(jp < i, 1.0, 0.0), 0.0)
            sel = (rank < topk) & (jidx < i)
            q_s = q_t * qk_scale
            pad = jnp.zeros((LANES - dh - nb, blk), F32)
            qaug_ref[h] = jnp.concatenate([q_s, jnp.where(sel, 0.0, MASK_BIAS), pad], axis=0).astype(BF16)
            qaug_own = jnp.concatenate([q_s, jnp.where(jidx == i, 0.0, MASK_BIAS), pad], axis=0).astype(BF16)
            st = jnp.dot(kn_ref[h, pl.ds(r_own, blk), :], qaug_own, preferred_element_type=F32)
            st = jnp.where(causal, st, -jnp.inf)
            s_ref[h, 0:blk, :] = st
            m_ref[h] = jnp.max(st, axis=0, keepdims=True)

    def finish_own(h):
        pr = jnp.exp2(s_ref[h, 0:blk, :] - m_ref[h]).astype(BF16)
        acc_ref[h] = jnp.dot(vt_ref[h, :, pl.ds(r_own, blk)], pr, preferred_element_type=F32)

    def score_pair(u, h):
        r0 = pl.multiple_of(u * pair, pair)
        st = jnp.dot(kn_ref[h, pl.ds(r0, pair), :], qaug_ref[h], preferred_element_type=F32)
        m_old = m_ref[h]
        m_new = jnp.maximum(m_old, jnp.max(st, axis=0, keepdims=True))
        s_ref[h] = st
        alpha_ref[h] = jnp.exp2(m_old - m_new)
        m_ref[h] = m_new

    def finish_pair(u, h):
        r0 = pl.multiple_of(u * pair, pair)
        pr = jnp.exp2(s_ref[h] - m_ref[h]).astype(BF16)
        acc_ref[h] = alpha_ref[h] * acc_ref[h] + jnp.dot(vt_ref[h, :, pl.ds(r0, pair)], pr,
                                                         preferred_element_type=F32)

    n_pairs = jnp.maximum((i + 1) // 2, 1)
    for h in range(nheads):
        finish_own(h)
        score_pair(0, h)

    def body(u, c):
        for h in range(nheads):
            finish_pair(u - 1, h)
            score_pair(u, h)
        return c

    lax.fori_loop(1, n_pairs, body, 0)
    for h in range(nheads):
        finish_pair(n_pairs - 1, h)

    for p in range(nheads // 2):
        a0 = acc_ref[2 * p]
        a1 = acc_ref[2 * p + 1]
        ot = jnp.concatenate([a0[0:dh] / a0[dh:dh + 1], a1[0:dh] / a1[dh:dh + 1]], axis=0)
        o_ref[:, p * LANES:(p + 1) * LANES] = ot.T.astype(o_ref.dtype)


def moba(proj, gq2, gk2, *, batch, seq):
    nb = seq // MOBA_BLOCK
    blk = MOBA_BLOCK
    dh = MOBA_HEAD_DIM
    nheads = MOBA_HEADS
    width = nheads * dh
    wb = width // LANES
    assert dh + nb <= LANES and 2 * dh == LANES and nb % 2 == 0
    v_rows = dh + 2 * SUBLANES
    body = functools.partial(_moba_body, nb=nb, blk=blk, dh=dh, topk=MOBA_TOPK, nheads=nheads)
    return pl.pallas_call(
        body,
        grid=(batch, nb),
        in_specs=[
            pl.BlockSpec((blk, width), lambda b, i: (b * nb + i, COL_QA // wb)),
            pl.BlockSpec((seq, width), lambda b, i: (b, COL_KA // wb)),
            pl.BlockSpec((seq, width), lambda b, i: (b, COL_VA // wb)),
            pl.BlockSpec((1, LANES), lambda b, i: (0, 0)),
            pl.BlockSpec((1, LANES), lambda b, i: (0, 0)),
        ],
        out_specs=pl.BlockSpec((blk, width), lambda b, i: (b * nb + i, 0)),
        out_shape=jax.ShapeDtypeStruct((batch * seq, width), BF16),
        scratch_shapes=[
            pltpu.VMEM((nheads, seq, LANES), BF16),
            pltpu.VMEM((nheads, v_rows, seq), BF16),
            pltpu.VMEM((nheads, nb, LANES), F32),
            pltpu.VMEM((nheads, LANES, blk), BF16),
            pltpu.VMEM((nheads, 2 * blk, blk), F32),
            pltpu.VMEM((nheads, 1, blk), F32),
            pltpu.VMEM((nheads, 1, blk), F32),
            pltpu.VMEM((nheads, v_rows, blk), F32),
        ],
        compiler_params=_params("arbitrary", "arbitrary"),
        name="moba",
    )(proj, proj, proj, gq2, gk2)


def _log_sigmoid(x):
    return jnp.minimum(x, 0.0) - jnp.log(1.0 + jnp.exp(-jnp.abs(x)))


def _dot_tri(tri, x, tri_left):
    out = None
    for _ in range(3):
        piece = x.astype(BF16)
        x = x - piece.astype(F32)
        term = (jnp.dot(tri, piece, preferred_element_type=F32) if tri_left
                else jnp.dot(piece, tri, preferred_element_type=F32))
        out = term if out is None else out + term
    return out


def _mlstm_body(qr_ref, kr_ref, v_ref, og_ref, gcol_ref, grow_ref, brow_ref, bcol_ref,
                cwq_ref, cwk_ref, cbq_ref, cbk_ref, gh_ref, o_ref,
                qx_ref, kx_ref, c_ref, m_ref, *, chunk, dk, nh):
    L = chunk
    width = nh * dk

    @pl.when(pl.program_id(1) == 0)
    def _():
        qx_ref[0:SUBLANES, :] = jnp.zeros((SUBLANES, width), F32)
        kx_ref[0:SUBLANES, :] = jnp.zeros((SUBLANES, width), F32)
        c_ref[...] = jnp.zeros_like(c_ref)
        m_ref[...] = jnp.zeros_like(m_ref)

    qx_ref[SUBLANES:SUBLANES + L, :] = qr_ref[...].astype(F32)
    kx_ref[SUBLANES:SUBLANES + L, :] = kr_ref[...].astype(F32)

    def conv_silu(x_ref, w_ref, b_ref):
        acc = b_ref[...] + w_ref[0:1, :] * x_ref[pl.ds(SUBLANES - CONV_WIDTH + 1, L), :]
        for j in range(1, CONV_WIDTH):
            acc = acc + w_ref[j:j + 1, :] * x_ref[pl.ds(SUBLANES - CONV_WIDTH + 1 + j, L), :]
        return acc * _sigmoid(acc)

    q_all = conv_silu(qx_ref, cwq_ref, cbq_ref)
    k_all = conv_silu(kx_ref, cwk_ref, cbk_ref) * (dk ** -0.5)
    qx_ref[0:SUBLANES, :] = qx_ref[L:L + SUBLANES, :]
    kx_ref[0:SUBLANES, :] = kx_ref[L:L + SUBLANES, :]

    pre_col = gcol_ref[...] + brow_ref[...]
    pre_row = grow_ref[0] + bcol_ref[...]
    t_i = lax.broadcasted_iota(jnp.int32, (L, L), 0)
    s_i = lax.broadcasted_iota(jnp.int32, (L, L), 1)
    tril = s_i <= t_i
    bcum_cols = _dot_tri(jnp.where(tril, 1.0, 0.0).astype(BF16), _log_sigmoid(pre_col), True)
    bcum_rows = _dot_tri(jnp.where(t_i <= s_i, 1.0, 0.0).astype(BF16), _log_sigmoid(pre_row), False)
    ones = jnp.ones((L, dk), BF16)

    for h in range(nh):
        cols = slice(h * dk, (h + 1) * dk)
        q = q_all[:, cols]
        k = k_all[:, cols]
        i_col = pre_col[:, h:h + 1]
        i_row = pre_row[h:h + 1, :]
        bcum_col = bcum_cols[:, nh + h:nh + h + 1]
        bcum_row = bcum_rows[nh + h:nh + h + 1, :]

        m_prev = m_ref[h, 0:1, 0:1]
        a_col = bcum_col + m_prev
        dmat = jnp.where(tril, bcum_col - bcum_row + i_row, -jnp.inf)
        m_t = jnp.maximum(a_col, jnp.max(dmat, axis=-1, keepdims=True))
        dw = jnp.exp(dmat - m_t)
        aw = jnp.exp(a_col - m_t)

        qb = q.astype(BF16)
        kb = k.astype(BF16)
        v_aug = jnp.concatenate([v_ref[:, cols], ones], axis=-1)
        sqk = _nt_dot(qb, kb) * dw
        num_aug = (aw * jnp.dot(qb, c_ref[h].astype(BF16), preferred_element_type=F32)
                   + jnp.dot(sqk.astype(BF16), v_aug, preferred_element_type=F32))
        den = num_aug[:, dk:dk + 1]
        hc = num_aug[:, 0:dk] / jnp.maximum(jnp.abs(den), jnp.exp(-m_t))

        b_last = bcum_col[L - 1:L, :]
        g_col = b_last - bcum_col + i_col
        m_new = jnp.maximum(b_last + m_prev, jnp.max(g_col, axis=0, keepdims=True))
        w_c = jnp.exp(b_last + m_prev - m_new)
        kw_t = (k * jnp.exp(g_col - m_new)).T.astype(BF16)
        c_ref[h] = w_c * c_ref[h] + jnp.dot(kw_t, v_aug, preferred_element_type=F32)
        m_ref[h] = jnp.broadcast_to(m_new, (1, LANES))

        o_ref[:, cols] = (_rms(hc, gh_ref[...]) * _sigmoid(og_ref[:, cols].astype(F32))).astype(o_ref.dtype)


def mlstm(proj, gates_col, gates_row, bias_row, bias_col, conv_w, conv_b, gh, *, batch, seq, chunk):
    nh = MLSTM_HEADS
    dk = MLSTM_DIM
    width = nh * dk
    wb = width // LANES
    nc = seq // chunk
    body = functools.partial(_mlstm_body, chunk=chunk, dk=dk, nh=nh)

    def rows(col0):
        return pl.BlockSpec((chunk, width), lambda b, c: (b * nc + c, col0 // wb))

    return pl.pallas_call(
        body,
        grid=(batch, nc),
        in_specs=[
            rows(COL_QM), rows(COL_KM), rows(COL_VM), rows(COL_OM),
            pl.BlockSpec((chunk, LANES), lambda b, c: (b * nc + c, 0)),
            pl.BlockSpec((1, SUBLANES, chunk), lambda b, c: (b, 0, c)),
            pl.BlockSpec((1, LANES), lambda b, c: (0, 0)),
            pl.BlockSpec((SUBLANES, 1), lambda b, c: (0, 0)),
            pl.BlockSpec((CONV_WIDTH, width), lambda b, c: (0, 0)),
            pl.BlockSpec((CONV_WIDTH, width), lambda b, c: (0, 1)),
            pl.BlockSpec((1, width), lambda b, c: (0, 0)),
            pl.BlockSpec((1, width), lambda b, c: (0, 1)),
            pl.BlockSpec((1, LANES), lambda b, c: (0, 0)),
        ],
        out_specs=pl.BlockSpec((chunk, width), lambda b, c: (b * nc + c, 0)),
        out_shape=jax.ShapeDtypeStruct((batch * seq, width), BF16),
        scratch_shapes=[
            pltpu.VMEM((chunk + 2 * SUBLANES, width), F32),
            pltpu.VMEM((chunk + 2 * SUBLANES, width), F32),
            pltpu.VMEM((nh, dk, 2 * dk), F32),
            pltpu.VMEM((nh, 1, LANES), F32),
        ],
        compiler_params=_params("arbitrary", "arbitrary"),
        name="mlstm",
    )(proj, proj, proj, proj, gates_col, gates_row, bias_row, bias_col,
      conv_w, conv_w, conv_b, conv_b, gh)


def _merge_body(ya_ref, yb_ref, ga_ref, gb_ref, x_ref, woa_ref, wob_ref, wout_ref, gffn_ref, *rest, moe):
    a = jnp.dot(ya_ref[...], woa_ref[...], preferred_element_type=F32)
    b = jnp.dot(yb_ref[...], wob_ref[...], preferred_element_type=F32)
    mixed = _sigmoid(ga_ref[...].astype(F32)) * a + _sigmoid(gb_ref[...].astype(F32)) * b
    x1 = x_ref[...] + jnp.dot(mixed.astype(BF16), wout_ref[...], preferred_element_type=F32)
    hf = _rms(x1, gffn_ref[...])
    if moe:
        wr_ref, x1_ref, hf_ref, lg_ref = rest
        _store_row_tiles(hf_ref, hf)
        lg_ref[...] = jnp.dot(hf.astype(BF16), wr_ref[...].astype(BF16), preferred_element_type=F32)
    else:
        x1_ref, hf_ref = rest
        hf_ref[...] = hf.astype(BF16)
    x1_ref[...] = x1


def merge(ya, yb, proj, x2, woa, wob, wout, gffn, wr, *, tm):
    t, d = x2.shape
    moe = wr is not None
    full = lambda m: (0, 0)
    in_specs = [
        pl.BlockSpec((tm, ya.shape[1]), lambda m: (m, 0)),
        pl.BlockSpec((tm, yb.shape[1]), lambda m: (m, 0)),
        pl.BlockSpec((tm, d), lambda m: (m, COL_GA * LANES // d)),
        pl.BlockSpec((tm, d), lambda m: (m, COL_GB * LANES // d)),
        pl.BlockSpec((tm, d), lambda m: (m, 0)),
        pl.BlockSpec(woa.shape, full), pl.BlockSpec(wob.shape, full), pl.BlockSpec(wout.shape, full),
        pl.BlockSpec((1, d), full),
    ]
    args = [ya, yb, proj, proj, x2, woa, wob, wout, gffn]
    out_specs = [pl.BlockSpec((tm, d), lambda m: (m, 0)), pl.BlockSpec((tm, d), lambda m: (m, 0))]
    out_shape = [jax.ShapeDtypeStruct((t, d), F32), jax.ShapeDtypeStruct((t, d), BF16)]
    if moe:
        out_specs[1] = pl.BlockSpec((tm * SUBLANES, LANES), lambda m: (m, 0))
        out_shape[1] = jax.ShapeDtypeStruct((t * SUBLANES, LANES), F32)
    if moe:
        in_specs.append(pl.BlockSpec(wr.shape, full))
        args.append(wr)
        out_specs.append(pl.BlockSpec((tm, LANES), lambda m: (m, 0)))
        out_shape.append(jax.ShapeDtypeStruct((t, LANES), F32))
    return pl.pallas_call(
        functools.partial(_merge_body, moe=moe),
        grid=(t // tm,),
        in_specs=in_specs, out_specs=out_specs, out_shape=out_shape,
        compiler_params=_params("arbitrary"),
        name="merge_moe" if moe else "merge",
    )(*args)


def _swiglu_gate(x, w1_ref, w3_ref, g_ref, fc):
    dff = g_ref.shape[1]
    for f0 in range(0, dff, fc):
        a = jnp.dot(x, w1_ref[:, f0:f0 + fc].astype(BF16), preferred_element_type=F32)
        b = jnp.dot(x, w3_ref[:, f0:f0 + fc].astype(BF16), preferred_element_type=F32)
        g_ref[:, f0:f0 + fc] = (a * _sigmoid(a) * b).astype(BF16)


def _swiglu(x, w1_ref, w3_ref, w2_ref, g_ref, fc):
    _swiglu_gate(x, w1_ref, w3_ref, g_ref, fc)
    return jnp.dot(g_ref[...], w2_ref[...].astype(BF16), preferred_element_type=F32)


def _ple(x, p_ref, g_ref, wg_ref, wp_ref):
    gate = _sigmoid(jnp.dot(_rms(x, g_ref[...]).astype(BF16), wg_ref[...], preferred_element_type=F32))
    emb = jnp.dot(p_ref[...].astype(BF16), wp_ref[...], preferred_element_type=F32)
    return x + gate * emb


def _ffn_body(hf_ref, x1_ref, w1_ref, w3_ref, w2_ref, p_ref, g_ref, wg_ref, wp_ref, o_ref, act_ref, *, fc):
    x2 = x1_ref[...] + _swiglu(hf_ref[...], w1_ref, w3_ref, w2_ref, act_ref, fc)
    o_ref[...] = _ple(x2, p_ref, g_ref, wg_ref, wp_ref)


def dense_ffn(hf, x1, w1, w3, w2, p2, p_row0, g, wg, wp, *, tm, fc):
    t, d = x1.shape
    dff = w1.shape[1]
    p_blk0 = p_row0 // tm
    resident = dict(pipeline_mode=pl.Buffered(1))
    full = lambda m: (0, 0)
    return pl.pallas_call(
        functools.partial(_ffn_body, fc=fc),
        grid=(t // tm,),
        in_specs=[
            pl.BlockSpec((tm, d), lambda m: (m, 0)),
            pl.BlockSpec((tm, d), lambda m: (m, 0)),
            pl.BlockSpec((d, dff), full, **resident),
            pl.BlockSpec((d, dff), full, **resident),
            pl.BlockSpec((dff, d), full, **resident),
            pl.BlockSpec((tm, p2.shape[1]), lambda m: (p_blk0 + m, 0)),
            pl.BlockSpec((1, d), full),
            pl.BlockSpec(wg.shape, full, **resident),
            pl.BlockSpec(wp.shape, full, **resident),
        ],
        out_specs=pl.BlockSpec((tm, d), lambda m: (m, 0)),
        out_shape=jax.ShapeDtypeStruct((t, d), F32),
        scratch_shapes=[pltpu.VMEM((tm, dff), BF16)],
        compiler_params=_params("arbitrary"),
        name="dense_ffn",
    )(hf, x1, w1, w3, w2, p2, g, wg, wp)


META_E0, META_E1, META_G0, META_G1, META_R0, META_R1 = 0, 1, 2, 3, 4, 5


def _route_body(lg_ref, meta_ref, tab_ref, cnt_ref, carry_ref, *, tm, ne):
    @pl.when(pl.program_id(0) == 0)
    def _():
        carry_ref[...] = jnp.zeros_like(carry_ref)

    lane = lax.broadcasted_iota(jnp.int32, (tm, LANES), 1)
    lanef = lane.astype(F32)
    lg = jnp.where(lane < ne, lg_ref[...], -jnp.inf)
    m1 = jnp.max(lg, axis=-1, keepdims=True)
    e1 = jnp.min(jnp.where(lg == m1, lanef, float(LANES)), axis=-1, keepdims=True)
    lg2 = jnp.where(lanef == e1, -jnp.inf, lg)
    m2 = jnp.max(lg2, axis=-1, keepdims=True)
    e2 = jnp.min(jnp.where(lg2 == m2, lanef, float(LANES)), axis=-1, keepdims=True)
    ex = jnp.exp(m2 - m1)
    g1 = 1.0 / (1.0 + ex)
    g2 = ex / (1.0 + ex)
    onehot = jnp.where((lanef == e1) | (lanef == e2), 1.0, 0.0)
    row = lax.broadcasted_iota(jnp.int32, (tm, tm), 0)
    col = lax.broadcasted_iota(jnp.int32, (tm, tm), 1)
    before = jnp.dot(jnp.where(col < row, 1.0, 0.0).astype(BF16), onehot.astype(BF16),
                     preferred_element_type=F32) + carry_ref[...]
    r1 = jnp.sum(jnp.where(lanef == e1, before, 0.0), axis=-1, keepdims=True)
    r2 = jnp.sum(jnp.where(lanef == e2, before, 0.0), axis=-1, keepdims=True)
    carry_ref[...] += jnp.sum(onehot, axis=0, keepdims=True)
    meta = jnp.zeros((tm, LANES), F32)
    for pos, val in ((META_E0, e1), (META_E1, e2), (META_G0, g1), (META_G1, g2), (META_R0, r1), (META_R1, r2)):
        meta = jnp.where(lane == pos, val, meta)
    meta_ref[...] = meta
    tab_ref[...] = meta.T[0:SUBLANES, :]
    cnt_ref[...] = jnp.broadcast_to(carry_ref[...], cnt_ref.shape)


def route(logits, *, tm):
    t = logits.shape[0]
    return pl.pallas_call(
        functools.partial(_route_body, tm=tm, ne=N_EXPERTS),
        grid=(t // tm,),
        in_specs=[pl.BlockSpec((tm, LANES), lambda m: (m, 0))],
        out_specs=[pl.BlockSpec((tm, LANES), lambda m: (m, 0)),
                   pl.BlockSpec((SUBLANES, tm), lambda m: (0, m)),
                   pl.BlockSpec((SUBLANES, LANES), lambda m: (0, 0))],
        out_shape=[jax.ShapeDtypeStruct((t, LANES), F32), jax.ShapeDtypeStruct((SUBLANES, t), F32),
                   jax.ShapeDtypeStruct((SUBLANES, LANES), F32)],
        scratch_shapes=[pltpu.VMEM((1, LANES), F32)],
        compiler_params=_params("arbitrary"),
        name="route",
    )(logits)


def _dispatch_body(dest_ref, pe_ref, na_ref, hf_ref, xs_ref, zero_ref, sem, zsem, *, tm, topk, tm_rows, n_tok):
    base = pl.program_id(0) * tm

    @pl.when(pl.program_id(0) == 0)
    def _():
        zero_ref[...] = jnp.zeros_like(zero_ref)
        n_tiles = xs_ref.shape[0] // (tm_rows * SUBLANES)
        fills = []
        for e in range(N_EXPERTS):
            end = pe_ref[e]
            nonempty = end > (pe_ref[e - 1] if e else 0)
            fills.append((nonempty, pl.multiple_of(jnp.maximum(end - tm_rows, 0), tm_rows)))
        for tile in range(n_tiles):
            fills.append((tile >= na_ref[0], tile * tm_rows))
        for phase in ("start", "wait"):
            for cond, row0 in fills:
                @pl.when(cond)
                def _(row0=row0, phase=phase):
                    cp = pltpu.make_async_copy(
                        zero_ref, xs_ref.at[pl.ds(row0 * SUBLANES, tm_rows * SUBLANES)], zsem)
                    cp.start() if phase == "start" else cp.wait()

    def issue(r, c):
        for k in range(topk):
            d = dest_ref[k * n_tok + base + r]
            pltpu.make_async_copy(_row_tile(hf_ref, r), _row_tile(xs_ref, d), sem).start(priority=k % 2)
        return c

    lax.fori_loop(0, tm, issue, 0, unroll=ROW_DMA_UNROLL)
    for k in range(topk):
        pltpu.make_async_copy(hf_ref, xs_ref.at[pl.ds(0, tm * SUBLANES)], sem).wait()


def _row_tile(ref, r):
    return ref.at[pl.ds(pl.multiple_of(r * SUBLANES, SUBLANES), SUBLANES)]


def dispatch(dest, pad_end, n_active, hf, *, n_rows, tm, topk, tm_rows):
    t = hf.shape[0] // SUBLANES
    return pl.pallas_call(
        functools.partial(_dispatch_body, tm=tm, topk=topk, tm_rows=tm_rows, n_tok=t),
        grid_spec=pltpu.PrefetchScalarGridSpec(
            num_scalar_prefetch=3,
            grid=(t // tm,),
            in_specs=[pl.BlockSpec((tm * SUBLANES, LANES), lambda m, *_: (m, 0))],
            out_specs=pl.BlockSpec(memory_space=pl.ANY),
            scratch_shapes=[pltpu.VMEM((tm_rows * SUBLANES, LANES), F32),
                            pltpu.SemaphoreType.DMA, pltpu.SemaphoreType.DMA],
        ),
        out_shape=jax.ShapeDtypeStruct((n_rows * SUBLANES, LANES), F32),
        compiler_params=_params("arbitrary"),
        name="moe_dispatch",
    )(dest, pad_end, n_active, hf)


def _experts_body(te_ref, na_ref, xs_ref, w1_ref, w3_ref, w2_ref, y_ref, xb_ref, g_ref, acc_ref, *, fc):
    del te_ref
    i = pl.program_id(0)
    f = pl.program_id(1)
    last = pl.num_programs(1) - 1
    tm = xb_ref.shape[0]

    @pl.when(i >= na_ref[0])
    def _():
        y_ref[...] = jnp.zeros_like(y_ref)

    @pl.when(i < na_ref[0])
    def _():
        @pl.when(f == 0)
        def _():
            xb_ref[...] = _load_row_tiles(xs_ref, tm).astype(BF16)

        y = _swiglu(xb_ref[...], w1_ref.at[0], w3_ref.at[0], w2_ref.at[0], g_ref, fc)

        @pl.when(f == 0)
        def _():
            acc_ref[...] = y

        @pl.when((f > 0) & (f < last))
        def _():
            acc_ref[...] += y

        @pl.when(f == last)
        def _():
            _store_row_tiles(y_ref, acc_ref[...] + y)


def experts(tile_expert, n_active, xs, w1, w3, w2, *, tm, tf, fc):
    n_rows = xs.shape[0] // SUBLANES
    d = w1.shape[1]
    dff = w1.shape[2]
    assert dff // tf >= 2
    row_tile = lambda i, f, te, na: (jnp.minimum(i, na[0] - 1), 0)
    ftile = lambda i, f, na: jnp.where(i < na[0], f, dff // tf - 1)
    return pl.pallas_call(
        functools.partial(_experts_body, fc=fc),
        grid_spec=pltpu.PrefetchScalarGridSpec(
            num_scalar_prefetch=2,
            grid=(n_rows // tm, dff // tf),
            in_specs=[
                pl.BlockSpec((tm * SUBLANES, LANES), row_tile),
                pl.BlockSpec((1, d, tf), lambda i, f, te, na: (te[i], 0, ftile(i, f, na))),
                pl.BlockSpec((1, d, tf), lambda i, f, te, na: (te[i], 0, ftile(i, f, na))),
                pl.BlockSpec((1, tf, d), lambda i, f, te, na: (te[i], ftile(i, f, na), 0)),
            ],
            out_specs=pl.BlockSpec((tm * SUBLANES, LANES), lambda i, f, te, na: (i, 0)),
            scratch_shapes=[pltpu.VMEM((tm, d), BF16), pltpu.VMEM((tm, tf), BF16), pltpu.VMEM((tm, d), F32)],
        ),
        out_shape=jax.ShapeDtypeStruct((n_rows * SUBLANES, LANES), F32),
        compiler_params=_params("arbitrary", "arbitrary"),
        name="moe_experts",
    )(tile_expert, n_active, xs, w1, w3, w2)


def _combine_body(dest_ref, x1_ref, meta_ref, p_ref, g_ref, wg_ref, wp_ref, y_ref, o_ref, buf_ref, sem,
                  *, tm, topk):
    m = pl.program_id(0)
    n_steps = pl.num_programs(0)
    n_tok = n_steps * tm

    def start_gather(tile, slot):
        def issue(r, c):
            for k in range(topk):
                d = dest_ref[k * n_tok + tile * tm + r]
                pltpu.make_async_copy(_row_tile(y_ref, d), _row_tile(buf_ref.at[slot, k], r),
                                      sem.at[slot]).start(priority=k % 2)
            return c

        lax.fori_loop(0, tm, issue, 0, unroll=ROW_DMA_UNROLL)

    @pl.when(m == 0)
    def _():
        start_gather(0, 0)

    @pl.when(m + 1 < n_steps)
    def _():
        start_gather(m + 1, (m + 1) % 2)

    slot = m % 2
    for k in range(topk):
        pltpu.make_async_copy(y_ref.at[pl.ds(0, tm * SUBLANES)], buf_ref.at[slot, k], sem.at[slot]).wait()
    meta = meta_ref[...]
    g0 = meta[:, META_G0:META_G0 + 1]
    g1 = meta[:, META_G1:META_G1 + 1]
    x2 = x1_ref[...] + (g0 * _load_row_tiles(buf_ref.at[slot, 0], tm)
                        + g1 * _load_row_tiles(buf_ref.at[slot, 1], tm))
    o_ref[...] = _ple(x2, p_ref, g_ref, wg_ref, wp_ref)


def combine(dest, x1, meta, y, p2, p_row0, g, wg, wp, *, tm, topk):
    t, d = x1.shape
    full = lambda m, dest: (0, 0)
    p_blk0 = p_row0 // tm
    return pl.pallas_call(
        functools.partial(_combine_body, tm=tm, topk=topk),
        grid_spec=pltpu.PrefetchScalarGridSpec(
            num_scalar_prefetch=1,
            grid=(t // tm,),
            in_specs=[pl.BlockSpec((tm, d), lambda m, dest: (m, 0)),
                      pl.BlockSpec((tm, LANES), lambda m, dest: (m, 0)),
                      pl.BlockSpec((tm, p2.shape[1]), lambda m, dest: (p_blk0 + m, 0)),
                      pl.BlockSpec((1, d), full), pl.BlockSpec(wg.shape, full), pl.BlockSpec(wp.shape, full),
                      pl.BlockSpec(memory_space=pl.ANY)],
            out_specs=pl.BlockSpec((tm, d), lambda m, dest: (m, 0)),
            scratch_shapes=[pltpu.VMEM((2, topk, tm * SUBLANES, LANES), F32), pltpu.SemaphoreType.DMA((2,))],
        ),
        out_shape=jax.ShapeDtypeStruct((t, d), F32),
        compiler_params=_params("arbitrary"),
        name="moe_combine",
    )(dest, x1, meta, p2, g, wg, wp, y)


def moe_ffn(hf, x1, logits, w1, w3, w2, ple_args, *, tm_route, tm_rows, tf, tm_move):
    t, d = x1.shape
    topk = 2
    meta, tab, cnt = route(logits, tm=tm_route)
    counts = cnt[0, :N_EXPERTS].astype(jnp.int32)
    padded = ((counts + tm_rows - 1) // tm_rows) * tm_rows
    pad_end = jnp.cumsum(padded).astype(jnp.int32)
    pad_start = pad_end - padded
    eidx = tab[META_E0:META_E1 + 1].astype(jnp.int32)
    rank = tab[META_R0:META_R1 + 1].astype(jnp.int32)
    dest = rank
    for e in range(N_EXPERTS):
        dest = dest + jnp.where(eidx == e, pad_start[e], 0)
    dest = dest.reshape(topk * t)
    n_tiles = -(-(t * topk) // tm_rows) + N_EXPERTS
    tile_start = jnp.arange(n_tiles, dtype=jnp.int32) * tm_rows
    tile_expert = jnp.minimum(jnp.sum(tile_start[:, None] >= pad_end[None, :], axis=1),
                              N_EXPERTS - 1).astype(jnp.int32)
    n_active = pad_end[N_EXPERTS - 1:] // tm_rows
    xs = dispatch(dest, pad_end, n_active, hf, n_rows=n_tiles * tm_rows, tm=tm_move, topk=topk, tm_rows=tm_rows)
    y = experts(tile_expert, n_active, xs, w1, w3, w2, tm=tm_rows, tf=tf, fc=256)
    return combine(dest, x1, meta, y, *ple_args, tm=tm_move, topk=topk)


def _tile2(g):
    return jnp.concatenate([g, g]).reshape(1, 2 * g.shape[0])


def kernel(x, p, g_mix, w_in, g_q, g_k, conv_w, conv_b, b_i, b_f, g_h, w_oa, w_ob, w_out, g_ffn, w_d1, w_d3,
           w_d2, w_router, w_e1, w_e3, w_e2, g_ple, w_ple_gate, w_ple_proj):
    batch, seq, d = x.shape
    depth = w_in.shape[0]
    t = batch * seq
    nh = MLSTM_HEADS
    x2 = x.reshape(t, d)
    c_q, c_k, c_v = 0, 512, 1024
    c_qk, c_vm, c_om, c_i, c_f, c_ga, c_gb, c_end = 1536, 2560, 3072, 3584, 3588, 3592, 4616, 5640

    w_in_t = jnp.swapaxes(w_in, 1, 2).astype(BF16)
    for l in range(depth):
        w_bf = w_in_t[l]
        w_gates = w_bf[c_ga:c_end]
        proj, gif = in_proj(x2, g_mix[l].reshape(1, d), w_gates, w_bf, nb_cols=c_i, if_col=c_i,
                            tm=1024, tn=512)

        ya = moba(proj, _tile2(g_q[l]), _tile2(g_k[l]), batch=batch, seq=seq)

        bias = jnp.concatenate([b_i[l], b_f[l]])
        bias_row = jnp.pad(bias, (0, LANES - 2 * nh)).reshape(1, LANES)
        bias_col = bias.reshape(2 * nh, 1)
        gates_row = gif[:, :2 * nh].reshape(batch, seq, 2 * nh).transpose(0, 2, 1)
        yb = mlstm(proj, gif, gates_row, bias_row, bias_col, conv_w[l], conv_b[l].reshape(1, -1),
                   g_h[l].reshape(1, -1), batch=batch, seq=seq, chunk=256)

        j = l // 2
        moe = l % 2 == 1
        wr = jnp.pad(w_router[j], ((0, 0), (0, LANES - N_EXPERTS))) if moe else None
        outs = merge(ya, yb, proj, x2, w_oa[l].astype(BF16), w_ob[l].astype(BF16), w_out[l].astype(BF16),
                     g_ffn[l].reshape(1, d), wr, tm=1024)
        ple_args = (p.reshape(depth * t, -1), l * t, g_ple[l].reshape(1, d), w_ple_gate[l].astype(BF16),
                    w_ple_proj[l].astype(BF16))
        if moe:
            x1, hf, logits = outs
            x2 = moe_ffn(hf, x1, logits, w_e1[j], w_e3[j].astype(BF16), w_e2[j],
                         ple_args, tm_route=512, tm_rows=512, tf=1792, tm_move=512)
        else:
            x1, hf = outs
            x2 = dense_ffn(hf, x1, w_d1[j].astype(BF16), w_d3[j].astype(BF16), w_d2[j].astype(BF16),
                           *ple_args, tm=1024, fc=256)
    return x2.reshape(batch, seq, d)
```

```python
import functools

import jax
import jax.numpy as jnp
from jax import lax
from jax.experimental import pallas as pl
from jax.experimental.pallas import tpu as pltpu

F32 = jnp.float32
BF16 = jnp.bfloat16

RMS_EPS = 1e-6
LANES = 128
SUBLANES = 8

MOBA_HEADS = 8
MOBA_HEAD_DIM = 64
MOBA_BLOCK = 256
MOBA_TOPK = 3
MLSTM_HEADS = 4
MLSTM_DIM = 128
CONV_WIDTH = 4
N_EXPERTS = 8

COL_GA, COL_GB = 0, 8
COL_QA, COL_KA, COL_VA = 16, 20, 24
COL_QM, COL_KM, COL_VM, COL_OM = 28, 32, 36, 40
N_PROJ = 44 * LANES

VMEM_LIMIT = 56 * 1024 * 1024
ROW_DMA_UNROLL = 16


def _params(*sem):
    return pltpu.CompilerParams(dimension_semantics=sem, vmem_limit_bytes=VMEM_LIMIT)


def _sigmoid(x):
    return 1.0 / (1.0 + jnp.exp(-x))


def _rms(x, g):
    return x * lax.rsqrt(jnp.mean(x * x, axis=-1, keepdims=True) + RMS_EPS) * g


def _split_bf16(x):
    hi = x.astype(BF16)
    return hi, (x - hi.astype(F32)).astype(BF16)


def _store_row_tiles(ref, x):
    ref[...] = x.reshape(x.shape[0] * SUBLANES, LANES)


def _load_row_tiles(ref, rows):
    return ref[...].reshape(rows, SUBLANES * LANES)


def _nt_dot(a, b, **kw):
    return lax.dot_general(a, b, (((1,), (1,)), ((), ())), preferred_element_type=F32, **kw)


def _in_proj_body(x_ref, g_ref, wa_ref, wb_ref, o_ref, oif_ref, h_ref, *, nb_cols, if_col, tn):
    h_ref[...] = _rms(x_ref[...], g_ref[...]).astype(BF16)
    oif_ref[...] = _nt_dot(h_ref[...], wb_ref[if_col:if_col + LANES, :])
    na = wa_ref.shape[0]
    for c0 in range(0, na + nb_cols, tn):
        w = wa_ref[c0:c0 + tn, :] if c0 < na else wb_ref[c0 - na:c0 - na + tn, :]
        o_ref[:, c0:c0 + tn] = _nt_dot(h_ref[...], w).astype(o_ref.dtype)


def in_proj(x2, g, wa, w_full, *, nb_cols, if_col, tm, tn):
    t, d = x2.shape
    n = wa.shape[0] + nb_cols
    resident = dict(pipeline_mode=pl.Buffered(1))
    return pl.pallas_call(
        functools.partial(_in_proj_body, nb_cols=nb_cols, if_col=if_col, tn=tn),
        grid=(t // tm,),
        in_specs=[
            pl.BlockSpec((tm, d), lambda m: (m, 0)),
            pl.BlockSpec((1, d), lambda m: (0, 0)),
            pl.BlockSpec(wa.shape, lambda m: (0, 0), **resident),
            pl.BlockSpec(w_full.shape, lambda m: (0, 0), **resident),
        ],
        out_specs=[
            pl.BlockSpec((tm, n), lambda m: (m, 0)),
            pl.BlockSpec((tm, LANES), lambda m: (m, 0)),
        ],
        out_shape=[jax.ShapeDtypeStruct((t, n), BF16), jax.ShapeDtypeStruct((t, LANES), F32)],
        scratch_shapes=[pltpu.VMEM((tm, d), BF16)],
        compiler_params=_params("arbitrary"),
        name="in_proj",
    )(x2, g, wa, w_full)


MASK_BIAS = -1e30
LOG2_E = 1.4426950408889634


def _moba_body(q_ref, k_ref, v_ref, gq_ref, gk_ref, o_ref,
               kn_ref, vt_ref, kmean_ref, qaug_ref, s_ref, m_ref, alpha_ref, acc_ref,
               *, nb, blk, dh, topk, nheads):
    i = pl.program_id(1)
    pair = 2 * blk
    lane = lax.broadcasted_iota(jnp.int32, (1, LANES), 1)
    head0 = lane < dh

    same_head = (lax.broadcasted_iota(jnp.int32, (LANES, LANES), 0) // dh
                 == lax.broadcasted_iota(jnp.int32, (LANES, LANES), 1) // dh)
    head_ones = jnp.where(same_head, 1.0, 0.0).astype(BF16)

    def head_rms(x, g, on_mxu):
        x2 = x * x
        if on_mxu:
            hi, lo = _split_bf16(x2)
            ss = (jnp.dot(hi, head_ones, preferred_element_type=F32)
                  + jnp.dot(lo, head_ones, preferred_element_type=F32))
        else:
            s0 = jnp.sum(jnp.where(head0, x2, 0.0), axis=-1, keepdims=True)
            s1 = jnp.sum(jnp.where(head0, 0.0, x2), axis=-1, keepdims=True)
            ss = jnp.where(head0, s0, s1)
        return x * lax.rsqrt(ss * (1.0 / dh) + RMS_EPS) * g

    @pl.when(i == 0)
    def _():
        def prep(j, c):
            r0 = pl.multiple_of(j * blk, blk)
            onehot = jnp.where(lane == dh + j, 1.0, 0.0)
            for p in range(nheads // 2):
                cols = slice(p * LANES, (p + 1) * LANES)
                kn = head_rms(k_ref[pl.ds(r0, blk), cols].astype(F32), gk_ref[...], True)
                for hh, kh in ((0, kn), (1, pltpu.roll(kn, dh, axis=1))):
                    h = 2 * p + hh
                    kmean_ref[h, pl.ds(j, 1), :] = jnp.mean(jnp.where(head0, kh, 0.0), axis=0, keepdims=True)
                    kn_ref[h, pl.ds(r0, blk), :] = jnp.where(head0, kh, onehot).astype(BF16)
                v_t = v_ref[pl.ds(r0, blk), cols].astype(F32).T.astype(BF16)
                for hh in range(2):
                    vt_ref[2 * p + hh, 0:dh, pl.ds(r0, blk)] = v_t[hh * dh:(hh + 1) * dh, :]
                    vt_ref[2 * p + hh, dh:, pl.ds(r0, blk)] = jnp.ones((vt_ref.shape[1] - dh, blk), BF16)
            return c

        lax.fori_loop(0, nb, prep, 0)

    jidx = lax.broadcasted_iota(jnp.int32, (nb, blk), 0)
    key_i = lax.broadcasted_iota(jnp.int32, (blk, blk), 0)
    qry_i = lax.broadcasted_iota(jnp.int32, (blk, blk), 1)
    causal = key_i <= qry_i
    r_own = pl.multiple_of(i * blk, blk)
    qk_scale = dh ** -0.5 * LOG2_E
    for p in range(nheads // 2):
        cols = slice(p * LANES, (p + 1) * LANES)
        qn_t = head_rms(q_ref[:, cols].astype(F32), gq_ref[...], False).T
        for hh in range(2):
            h = 2 * p + hh
            q_t = qn_t[hh * dh:(hh + 1) * dh, :]
            gate = jnp.dot(kmean_ref[h].astype(BF16),
                           jnp.concatenate([q_t, jnp.zeros((LANES - dh, blk), F32)], axis=0).astype(BF16),
                           preferred_element_type=F32)
            rank = jnp.zeros((nb, blk), F32)
            for jp in range(nb):
                row = gate[jp:jp + 1, :]
                beats = (row > gate) | ((row == gate) & (jidx > jp))
                rank = rank + jnp.where(beats, jnp.where(jp < i, 1.0, 0.0), 0.0)
            sel = (rank < topk) & (jidx < i)
            q_s = q_t * qk_scale
            pad = jnp.zeros((LANES - dh - nb, blk), F32)
            qaug_ref[h] = jnp.concatenate([q_s, jnp.where(sel, 0.0, MASK_BIAS), pad], axis=0).astype(BF16)
            qaug_own = jnp.concatenate([q_s, jnp.where(jidx == i, 0.0, MASK_BIAS), pad], axis=0).astype(BF16)
            st = jnp.dot(kn_ref[h, pl.ds(r_own, blk), :], qaug_own, preferred_element_type=F32)
            st = jnp.where(causal, st, -jnp.inf)
            s_ref[h, 0:blk, :] = st
            m_ref[h] = jnp.max(st, axis=0, keepdims=True)

    def finish_own(h):
        pr = jnp.exp2(s_ref[h, 0:blk, :] - m_ref[h]).astype(BF16)
        acc_ref[h] = jnp.dot(vt_ref[h, :, pl.ds(r_own, blk)], pr, preferred_element_type=F32)

    def score_pair(u, h):
        r0 = pl.multiple_of(u * pair, pair)
        st = jnp.dot(kn_ref[h, pl.ds(r0, pair), :], qaug_ref[h], preferred_element_type=F32)
        m_old = m_ref[h]
        m_new = jnp.maximum(m_old, jnp.max(st, axis=0, keepdims=True))
        s_ref[h] = st
        alpha_ref[h] = jnp.exp2(m_old - m_new)
        m_ref[h] = m_new

    def finish_pair(u, h):
        r0 = pl.multiple_of(u * pair, pair)
        pr = jnp.exp2(s_ref[h] - m_ref[h]).astype(BF16)
        acc_ref[h] = alpha_ref[h] * acc_ref[h] + jnp.dot(vt_ref[h, :, pl.ds(r0, pair)], pr,
                                                         preferred_element_type=F32)

    n_pairs = jnp.maximum((i + 1) // 2, 1)
    for h in range(nheads):
        finish_own(h)
        score_pair(0, h)

    def body(u, c):
        for h in range(nheads):
            finish_pair(u - 1, h)
            score_pair(u, h)
        return c

    lax.fori_loop(1, n_pairs, body, 0)
    for h in range(nheads):
        finish_pair(n_pairs - 1, h)

    for p in range(nheads // 2):
        a0 = acc_ref[2 * p]
        a1 = acc_ref[2 * p + 1]
        ot = jnp.concatenate([a0[0:dh] / a0[dh:dh + 1], a1[0:dh] / a1[dh:dh + 1]], axis=0)
        o_ref[:, p * LANES:(p + 1) * LANES] = ot.T.astype(o_ref.dtype)


def moba(proj, gq2, gk2, *, batch, seq):
    nb = seq // MOBA_BLOCK
    blk = MOBA_BLOCK
    dh = MOBA_HEAD_DIM
    nheads = MOBA_HEADS
    width = nheads * dh
    wb = width // LANES
    assert dh + nb <= LANES and 2 * dh == LANES and nb % 2 == 0
    v_rows = dh + 2 * SUBLANES
    body = functools.partial(_moba_body, nb=nb, blk=blk, dh=dh, topk=MOBA_TOPK, nheads=nheads)
    return pl.pallas_call(
        body,
        grid=(batch, nb),
        in_specs=[
            pl.BlockSpec((blk, width), lambda b, i: (b * nb + i, COL_QA // wb)),
            pl.BlockSpec((seq, width), lambda b, i: (b, COL_KA // wb)),
            pl.BlockSpec((seq, width), lambda b, i: (b, COL_VA // wb)),
            pl.BlockSpec((1, LANES), lambda b, i: (0, 0)),
            pl.BlockSpec((1, LANES), lambda b, i: (0, 0)),
        ],
        out_specs=pl.BlockSpec((blk, width), lambda b, i: (b * nb + i, 0)),
        out_shape=jax.ShapeDtypeStruct((batch * seq, width), BF16),
        scratch_shapes=[
            pltpu.VMEM((nheads, seq, LANES), BF16),
            pltpu.VMEM((nheads, v_rows, seq), BF16),
            pltpu.VMEM((nheads, nb, LANES), F32),
            pltpu.VMEM((nheads, LANES, blk), BF16),
            pltpu.VMEM((nheads, 2 * blk, blk), F32),
            pltpu.VMEM((nheads, 1, blk), F32),
            pltpu.VMEM((nheads, 1, blk), F32),
            pltpu.VMEM((nheads, v_rows, blk), F32),
        ],
        compiler_params=_params("arbitrary", "arbitrary"),
        name="moba",
    )(proj, proj, proj, gq2, gk2)


def _log_sigmoid(x):
    return jnp.minimum(x, 0.0) - jnp.log(1.0 + jnp.exp(-jnp.abs(x)))


def _dot_tri(tri, x, tri_left):
    out = None
    for _ in range(3):
        piece = x.astype(BF16)
        x = x - piece.astype(F32)
        term = (jnp.dot(tri, piece, preferred_element_type=F32) if tri_left
                else jnp.dot(piece, tri, preferred_element_type=F32))
        out = term if out is None else out + term
    return out


def _mlstm_body(qr_ref, kr_ref, v_ref, og_ref, gcol_ref, grow_ref, brow_ref, bcol_ref,
                cwq_ref, cwk_ref, cbq_ref, cbk_ref, gh_ref, o_ref,
                qx_ref, kx_ref, c_ref, m_ref, *, chunk, dk, nh):
    L = chunk
    width = nh * dk

    @pl.when(pl.program_id(1) == 0)
    def _():
        qx_ref[0:SUBLANES, :] = jnp.zeros((SUBLANES, width), F32)
        kx_ref[0:SUBLANES, :] = jnp.zeros((SUBLANES, width), F32)
        c_ref[...] = jnp.zeros_like(c_ref)
        m_ref[...] = jnp.zeros_like(m_ref)

    qx_ref[SUBLANES:SUBLANES + L, :] = qr_ref[...].astype(F32)
    kx_ref[SUBLANES:SUBLANES + L, :] = kr_ref[...].astype(F32)

    def conv_silu(x_ref, w_ref, b_ref):
        acc = b_ref[...] + w_ref[0:1, :] * x_ref[pl.ds(SUBLANES - CONV_WIDTH + 1, L), :]
        for j in range(1, CONV_WIDTH):
            acc = acc + w_ref[j:j + 1, :] * x_ref[pl.ds(SUBLANES - CONV_WIDTH + 1 + j, L), :]
        return acc * _sigmoid(acc)

    q_all = conv_silu(qx_ref, cwq_ref, cbq_ref)
    k_all = conv_silu(kx_ref, cwk_ref, cbk_ref) * (dk ** -0.5)
    qx_ref[0:SUBLANES, :] = qx_ref[L:L + SUBLANES, :]
    kx_ref[0:SUBLANES, :] = kx_ref[L:L + SUBLANES, :]

    pre_col = gcol_ref[...] + brow_ref[...]
    pre_row = grow_ref[0] + bcol_ref[...]
    t_i = lax.broadcasted_iota(jnp.int32, (L, L), 0)
    s_i = lax.broadcasted_iota(jnp.int32, (L, L), 1)
    tril = s_i <= t_i
    bcum_cols = _dot_tri(jnp.where(tril, 1.0, 0.0).astype(BF16), _log_sigmoid(pre_col), True)
    bcum_rows = _dot_tri(jnp.where(t_i <= s_i, 1.0, 0.0).astype(BF16), _log_sigmoid(pre_row), False)
    ones = jnp.ones((L, dk), BF16)

    for h in range(nh):
        cols = slice(h * dk, (h + 1) * dk)
        q = q_all[:, cols]
        k = k_all[:, cols]
        i_col = pre_col[:, h:h + 1]
        i_row = pre_row[h:h + 1, :]
        bcum_col = bcum_cols[:, nh + h:nh + h + 1]
        bcum_row = bcum_rows[nh + h:nh + h + 1, :]

        m_prev = m_ref[h, 0:1, 0:1]
        a_col = bcum_col + m_prev
        dmat = jnp.where(tril, bcum_col - bcum_row + i_row, -jnp.inf)
        m_t = jnp.maximum(a_col, jnp.max(dmat, axis=-1, keepdims=True))
        dw = jnp.exp(dmat - m_t)
        aw = jnp.exp(a_col - m_t)

        qb = q.astype(BF16)
        kb = k.astype(BF16)
        v_aug = jnp.concatenate([v_ref[:, cols], ones], axis=-1)
        sqk = _nt_dot(qb, kb) * dw
        num_aug = (aw * jnp.dot(qb, c_ref[h].astype(BF16), preferred_element_type=F32)
                   + jnp.dot(sqk.astype(BF16), v_aug, preferred_element_type=F32))
        den = num_aug[:, dk:dk + 1]
        hc = num_aug[:, 0:dk] / jnp.maximum(jnp.abs(den), jnp.exp(-m_t))

        b_last = bcum_col[L - 1:L, :]
        g_col = b_last - bcum_col + i_col
        m_new = jnp.maximum(b_last + m_prev, jnp.max(g_col, axis=0, keepdims=True))
        w_c = jnp.exp(b_last + m_prev - m_new)
        kw_t = (k * jnp.exp(g_col - m_new)).T.astype(BF16)
        c_ref[h] = w_c * c_ref[h] + jnp.dot(kw_t, v_aug, preferred_element_type=F32)
        m_ref[h] = jnp.broadcast_to(m_new, (1, LANES))

        o_ref[:, cols] = (_rms(hc, gh_ref[...]) * _sigmoid(og_ref[:, cols].astype(F32))).astype(o_ref.dtype)


def mlstm(proj, gates_col, gates_row, bias_row, bias_col, conv_w, conv_b, gh, *, batch, seq, chunk):
    nh = MLSTM_HEADS
    dk = MLSTM_DIM
    width = nh * dk
    wb = width // LANES
    nc = seq // chunk
    body = functools.partial(_mlstm_body, chunk=chunk, dk=dk, nh=nh)

    def rows(col0):
        return pl.BlockSpec((chunk, width), lambda b, c: (b * nc + c, col0 // wb))

    return pl.pallas_call(
        body,
        grid=(batch, nc),
        in_specs=[
            rows(COL_QM), rows(COL_KM), rows(COL_VM), rows(COL_OM),
            pl.BlockSpec((chunk, LANES), lambda b, c: (b * nc + c, 0)),
            pl.BlockSpec((1, SUBLANES, chunk), lambda b, c: (b, 0, c)),
            pl.BlockSpec((1, LANES), lambda b, c: (0, 0)),
            pl.BlockSpec((SUBLANES, 1), lambda b, c: (0, 0)),
            pl.BlockSpec((CONV_WIDTH, width), lambda b, c: (0, 0)),
            pl.BlockSpec((CONV_WIDTH, width), lambda b, c: (0, 1)),
            pl.BlockSpec((1, width), lambda b, c: (0, 0)),
            pl.BlockSpec((1, width), lambda b, c: (0, 1)),
            pl.BlockSpec((1, LANES), lambda b, c: (0, 0)),
        ],
        out_specs=pl.BlockSpec((chunk, width), lambda b, c: (b * nc + c, 0)),
        out_shape=jax.ShapeDtypeStruct((batch * seq, width), BF16),
        scratch_shapes=[
            pltpu.VMEM((chunk + 2 * SUBLANES, width), F32),
            pltpu.VMEM((chunk + 2 * SUBLANES, width), F32),
            pltpu.VMEM((nh, dk, 2 * dk), F32),
            pltpu.VMEM((nh, 1, LANES), F32),
        ],
        compiler_params=_params("arbitrary", "arbitrary"),
        name="mlstm",
    )(proj, proj, proj, proj, gates_col, gates_row, bias_row, bias_col,
      conv_w, conv_w, conv_b, conv_b, gh)


def _merge_body(ya_ref, yb_ref, ga_ref, gb_ref, x_ref, woa_ref, wob_ref, wout_ref, gffn_ref, *rest, moe):
    a = jnp.dot(ya_ref[...], woa_ref[...], preferred_element_type=F32)
    b = jnp.dot(yb_ref[...], wob_ref[...], preferred_element_type=F32)
    mixed = _sigmoid(ga_ref[...].astype(F32)) * a + _sigmoid(gb_ref[...].astype(F32)) * b
    x1 = x_ref[...] + jnp.dot(mixed.astype(BF16), wout_ref[...], preferred_element_type=F32)
    hf = _rms(x1, gffn_ref[...])
    if moe:
        wr_ref, x1_ref, hf_ref, lg_ref = rest
        _store_row_tiles(hf_ref, hf)
        lg_ref[...] = jnp.dot(hf.astype(BF16), wr_ref[...].astype(BF16), preferred_element_type=F32)
    else:
        x1_ref, hf_ref = rest
        hf_ref[...] = hf.astype(BF16)
    x1_ref[...] = x1


def merge(ya, yb, proj, x2, woa, wob, wout, gffn, wr, *, tm):
    t, d = x2.shape
    moe = wr is not None
    full = lambda m: (0, 0)
    in_specs = [
        pl.BlockSpec((tm, ya.shape[1]), lambda m: (m, 0)),
        pl.BlockSpec((tm, yb.shape[1]), lambda m: (m, 0)),
        pl.BlockSpec((tm, d), lambda m: (m, COL_GA * LANES // d)),
        pl.BlockSpec((tm, d), lambda m: (m, COL_GB * LANES // d)),
        pl.BlockSpec((tm, d), lambda m: (m, 0)),
        pl.BlockSpec(woa.shape, full), pl.BlockSpec(wob.shape, full), pl.BlockSpec(wout.shape, full),
        pl.BlockSpec((1, d), full),
    ]
    args = [ya, yb, proj, proj, x2, woa, wob, wout, gffn]
    out_specs = [pl.BlockSpec((tm, d), lambda m: (m, 0)), pl.BlockSpec((tm, d), lambda m: (m, 0))]
    out_shape = [jax.ShapeDtypeStruct((t, d), F32), jax.ShapeDtypeStruct((t, d), BF16)]
    if moe:
        out_specs[1] = pl.BlockSpec((tm * SUBLANES, LANES), lambda m: (m, 0))
        out_shape[1] = jax.ShapeDtypeStruct((t * SUBLANES, LANES), F32)
    if moe:
        in_specs.append(pl.BlockSpec(wr.shape, full))
        args.append(wr)
        out_specs.append(pl.BlockSpec((tm, LANES), lambda m: (m, 0)))
        out_shape.append(jax.ShapeDtypeStruct((t, LANES), F32))
    return pl.pallas_call(
        functools.partial(_merge_body, moe=moe),
        grid=(t // tm,),
        in_specs=in_specs, out_specs=out_specs, out_shape=out_shape,
        compiler_params=_params("arbitrary"),
        name="merge_moe" if moe else "merge",
    )(*args)


def _swiglu_gate(x, w1_ref, w3_ref, g_ref, fc):
    dff = g_ref.shape[1]
    for f0 in range(0, dff, fc):
        a = jnp.dot(x, w1_ref[:, f0:f0 + fc].astype(BF16), preferred_element_type=F32)
        b = jnp.dot(x, w3_ref[:, f0:f0 + fc].astype(BF16), preferred_element_type=F32)
        g_ref[:, f0:f0 + fc] = (a * _sigmoid(a) * b).astype(BF16)


def _swiglu(x, w1_ref, w3_ref, w2_ref, g_ref, fc):
    _swiglu_gate(x, w1_ref, w3_ref, g_ref, fc)
    return jnp.dot(g_ref[...], w2_ref[...].astype(BF16), preferred_element_type=F32)


def _ple(x, p_ref, g_ref, wg_ref, wp_ref):
    gate = _sigmoid(jnp.dot(_rms(x, g_ref[...]).astype(BF16), wg_ref[...], preferred_element_type=F32))
    emb = jnp.dot(p_ref[...].astype(BF16), wp_ref[...], preferred_element_type=F32)
    return x + gate * emb


def _ffn_body(hf_ref, x1_ref, w1_ref, w3_ref, w2_ref, p_ref, g_ref, wg_ref, wp_ref, o_ref, act_ref, *, fc):
    x2 = x1_ref[...] + _swiglu(hf_ref[...], w1_ref, w3_ref, w2_ref, act_ref, fc)
    o_ref[...] = _ple(x2, p_ref, g_ref, wg_ref, wp_ref)


def dense_ffn(hf, x1, w1, w3, w2, p2, p_row0, g, wg, wp, *, tm, fc):
    t, d = x1.shape
    dff = w1.shape[1]
    p_blk0 = p_row0 // tm
    resident = dict(pipeline_mode=pl.Buffered(1))
    full = lambda m: (0, 0)
    return pl.pallas_call(
        functools.partial(_ffn_body, fc=fc),
        grid=(t // tm,),
        in_specs=[
            pl.BlockSpec((tm, d), lambda m: (m, 0)),
            pl.BlockSpec((tm, d), lambda m: (m, 0)),
            pl.BlockSpec((d, dff), full, **resident),
            pl.BlockSpec((d, dff), full, **resident),
            pl.BlockSpec((dff, d), full, **resident),
            pl.BlockSpec((tm, p2.shape[1]), lambda m: (p_blk0 + m, 0)),
            pl.BlockSpec((1, d), full),
            pl.BlockSpec(wg.shape, full, **resident),
            pl.BlockSpec(wp.shape, full, **resident),
        ],
        out_specs=pl.BlockSpec((tm, d), lambda m: (m, 0)),
        out_shape=jax.ShapeDtypeStruct((t, d), F32),
        scratch_shapes=[pltpu.VMEM((tm, dff), BF16)],
        compiler_params=_params("arbitrary"),
        name="dense_ffn",
    )(hf, x1, w1, w3, w2, p2, g, wg, wp)


META_E0, META_E1, META_G0, META_G1, META_R0, META_R1 = 0, 1, 2, 3, 4, 5


def _route_body(lg_ref, meta_ref, tab_ref, cnt_ref, carry_ref, *, tm, ne):
    @pl.when(pl.program_id(0) == 0)
    def _():
        carry_ref[...] = jnp.zeros_like(carry_ref)

    lane = lax.broadcasted_iota(jnp.int32, (tm, LANES), 1)
    lanef = lane.astype(F32)
    lg = jnp.where(lane < ne, lg_ref[...], -jnp.inf)
    m1 = jnp.max(lg, axis=-1, keepdims=True)
    e1 = jnp.min(jnp.where(lg == m1, lanef, float(LANES)), axis=-1, keepdims=True)
    lg2 = jnp.where(lanef == e1, -jnp.inf, lg)
    m2 = jnp.max(lg2, axis=-1, keepdims=True)
    e2 = jnp.min(jnp.where(lg2 == m2, lanef, float(LANES)), axis=-1, keepdims=True)
    ex = jnp.exp(m2 - m1)
    g1 = 1.0 / (1.0 + ex)
    g2 = ex / (1.0 + ex)
    onehot = jnp.where((lanef == e1) | (lanef == e2), 1.0, 0.0)
    row = lax.broadcasted_iota(jnp.int32, (tm, tm), 0)
    col = lax.broadcasted_iota(jnp.int32, (tm, tm), 1)
    before = jnp.dot(jnp.where(col < row, 1.0, 0.0).astype(BF16), onehot.astype(BF16),
                     preferred_element_type=F32) + carry_ref[...]
    r1 = jnp.sum(jnp.where(lanef == e1, before, 0.0), axis=-1, keepdims=True)
    r2 = jnp.sum(jnp.where(lanef == e2, before, 0.0), axis=-1, keepdims=True)
    carry_ref[...] += jnp.sum(onehot, axis=0, keepdims=True)
    meta = jnp.zeros((tm, LANES), F32)
    for pos, val in ((META_E0, e1), (META_E1, e2), (META_G0, g1), (META_G1, g2), (META_R0, r1), (META_R1, r2)):
        meta = jnp.where(lane == pos, val, meta)
    meta_ref[...] = meta
    tab_ref[...] = meta.T[0:SUBLANES, :]
    cnt_ref[...] = jnp.broadcast_to(carry_ref[...], cnt_ref.shape)


def route(logits, *, tm):
    t = logits.shape[0]
    return pl.pallas_call(
        functools.partial(_route_body, tm=tm, ne=N_EXPERTS),
        grid=(t // tm,),
        in_specs=[pl.BlockSpec((tm, LANES), lambda m: (m, 0))],
        out_specs=[pl.BlockSpec((tm, LANES), lambda m: (m, 0)),
                   pl.BlockSpec((SUBLANES, tm), lambda m: (0, m)),
                   pl.BlockSpec((SUBLANES, LANES), lambda m: (0, 0))],
        out_shape=[jax.ShapeDtypeStruct((t, LANES), F32), jax.ShapeDtypeStruct((SUBLANES, t), F32),
                   jax.ShapeDtypeStruct((SUBLANES, LANES), F32)],
        scratch_shapes=[pltpu.VMEM((1, LANES), F32)],
        compiler_params=_params("arbitrary"),
        name="route",
    )(logits)


def _dispatch_body(dest_ref, pe_ref, na_ref, hf_ref, xs_ref, zero_ref, sem, zsem, *, tm, topk, tm_rows, n_tok):
    base = pl.program_id(0) * tm

    @pl.when(pl.program_id(0) == 0)
    def _():
        zero_ref[...] = jnp.zeros_like(zero_ref)
        n_tiles = xs_ref.shape[0] // (tm_rows * SUBLANES)
        fills = []
        for e in range(N_EXPERTS):
            end = pe_ref[e]
            nonempty = end > (pe_ref[e - 1] if e else 0)
            fills.append((nonempty, pl.multiple_of(jnp.maximum(end - tm_rows, 0), tm_rows)))
        for tile in range(n_tiles):
            fills.append((tile >= na_ref[0], tile * tm_rows))
        for phase in ("start", "wait"):
            for cond, row0 in fills:
                @pl.when(cond)
                def _(row0=row0, phase=phase):
                    cp = pltpu.make_async_copy(
                        zero_ref, xs_ref.at[pl.ds(row0 * SUBLANES, tm_rows * SUBLANES)], zsem)
                    cp.start() if phase == "start" else cp.wait()

    def issue(r, c):
        for k in range(topk):
            d = dest_ref[k * n_tok + base + r]
            pltpu.make_async_copy(_row_tile(hf_ref, base + r), _row_tile(xs_ref, d), sem).start(priority=k % 2)
        return c

    lax.fori_loop(0, tm, issue, 0, unroll=ROW_DMA_UNROLL)

    def wait_step():
        for k in range(topk):
            pltpu.make_async_copy(hf_ref.at[pl.ds(0, tm * SUBLANES)], xs_ref.at[pl.ds(0, tm * SUBLANES)],
                                  sem).wait()

    @pl.when(pl.program_id(0) > 0)
    def _():
        wait_step()

    @pl.when(pl.program_id(0) == pl.num_programs(0) - 1)
    def _():
        wait_step()


def _row_tile(ref, r):
    return ref.at[pl.ds(pl.multiple_of(r * SUBLANES, SUBLANES), SUBLANES)]


def dispatch(dest, pad_end, n_active, hf, *, n_rows, tm, topk, tm_rows):
    t = hf.shape[0] // SUBLANES
    return pl.pallas_call(
        functools.partial(_dispatch_body, tm=tm, topk=topk, tm_rows=tm_rows, n_tok=t),
        grid_spec=pltpu.PrefetchScalarGridSpec(
            num_scalar_prefetch=3,
            grid=(t // tm,),
            in_specs=[pl.BlockSpec(memory_space=pl.ANY)],
            out_specs=pl.BlockSpec(memory_space=pl.ANY),
            scratch_shapes=[pltpu.VMEM((tm_rows * SUBLANES, LANES), F32),
                            pltpu.SemaphoreType.DMA, pltpu.SemaphoreType.DMA],
        ),
        out_shape=jax.ShapeDtypeStruct((n_rows * SUBLANES, LANES), F32),
        compiler_params=_params("arbitrary"),
        name="moe_dispatch",
    )(dest, pad_end, n_active, hf)


def _experts_body(te_ref, na_ref, xs_ref, w1_ref, w3_ref, w2_ref, y_ref, xb_ref, g_ref, acc_ref, *, fc):
    del te_ref
    i = pl.program_id(0)
    f = pl.program_id(1)
    last = pl.num_programs(1) - 1
    tm = xb_ref.shape[0]

    @pl.when(i >= na_ref[0])
    def _():
        y_ref[...] = jnp.zeros_like(y_ref)

    @pl.when(i < na_ref[0])
    def _():
        @pl.when(f == 0)
        def _():
            xb_ref[...] = _load_row_tiles(xs_ref, tm).astype(BF16)

        y = _swiglu(xb_ref[...], w1_ref.at[0], w3_ref.at[0], w2_ref.at[0], g_ref, fc)

        @pl.when(f == 0)
        def _():
            acc_ref[...] = y

        @pl.when((f > 0) & (f < last))
        def _():
            acc_ref[...] += y

        @pl.when(f == last)
        def _():
            _store_row_tiles(y_ref, acc_ref[...] + y)


def experts(tile_expert, n_active, xs, w1, w3, w2, *, tm, tf, fc):
    n_rows = xs.shape[0] // SUBLANES
    d = w1.shape[1]
    dff = w1.shape[2]
    assert dff // tf >= 2
    row_tile = lambda i, f, te, na: (jnp.minimum(i, na[0] - 1), 0)
    ftile = lambda i, f, na: jnp.where(i < na[0], f, dff // tf - 1)
    return pl.pallas_call(
        functools.partial(_experts_body, fc=fc),
        grid_spec=pltpu.PrefetchScalarGridSpec(
            num_scalar_prefetch=2,
            grid=(n_rows // tm, dff // tf),
            in_specs=[
                pl.BlockSpec((tm * SUBLANES, LANES), row_tile),
                pl.BlockSpec((1, d, tf), lambda i, f, te, na: (te[i], 0, ftile(i, f, na))),
                pl.BlockSpec((1, d, tf), lambda i, f, te, na: (te[i], 0, ftile(i, f, na))),
                pl.BlockSpec((1, tf, d), lambda i, f, te, na: (te[i], ftile(i, f, na), 0)),
            ],
            out_specs=pl.BlockSpec((tm * SUBLANES, LANES), lambda i, f, te, na: (i, 0)),
            scratch_shapes=[pltpu.VMEM((tm, d), BF16), pltpu.VMEM((tm, tf), BF16), pltpu.VMEM((tm, d), F32)],
        ),
        out_shape=jax.ShapeDtypeStruct((n_rows * SUBLANES, LANES), F32),
        compiler_params=_params("arbitrary", "arbitrary"),
        name="moe_experts",
    )(tile_expert, n_active, xs, w1, w3, w2)


def _combine_body(dest_ref, x1_ref, meta_ref, p_ref, g_ref, wg_ref, wp_ref, y_ref, o_ref, buf_ref, sem,
                  *, tm, topk):
    m = pl.program_id(0)
    n_steps = pl.num_programs(0)
    n_tok = n_steps * tm

    def start_gather(tile, slot):
        def issue(r, c):
            for k in range(topk):
                d = dest_ref[k * n_tok + tile * tm + r]
                pltpu.make_async_copy(_row_tile(y_ref, d), _row_tile(buf_ref.at[slot, k], r),
                                      sem.at[slot]).start(priority=k % 2)
            return c

        lax.fori_loop(0, tm, issue, 0, unroll=ROW_DMA_UNROLL)

    @pl.when(m == 0)
    def _():
        start_gather(0, 0)

    @pl.when(m + 1 < n_steps)
    def _():
        start_gather(m + 1, (m + 1) % 2)

    slot = m % 2
    for k in range(topk):
        pltpu.make_async_copy(y_ref.at[pl.ds(0, tm * SUBLANES)], buf_ref.at[slot, k], sem.at[slot]).wait()
    meta = meta_ref[...]
    g0 = meta[:, META_G0:META_G0 + 1]
    g1 = meta[:, META_G1:META_G1 + 1]
    x2 = x1_ref[...] + (g0 * _load_row_tiles(buf_ref.at[slot, 0], tm)
                        + g1 * _load_row_tiles(buf_ref.at[slot, 1], tm))
    o_ref[...] = _ple(x2, p_ref, g_ref, wg_ref, wp_ref)


def combine(dest, x1, meta, y, p2, p_row0, g, wg, wp, *, tm, topk):
    t, d = x1.shape
    full = lambda m, dest: (0, 0)
    p_blk0 = p_row0 // tm
    return pl.pallas_call(
        functools.partial(_combine_body, tm=tm, topk=topk),
        grid_spec=pltpu.PrefetchScalarGridSpec(
            num_scalar_prefetch=1,
            grid=(t // tm,),
            in_specs=[pl.BlockSpec((tm, d), lambda m, dest: (m, 0)),
                      pl.BlockSpec((tm, LANES), lambda m, dest: (m, 0)),
                      pl.BlockSpec((tm, p2.shape[1]), lambda m, dest: (p_blk0 + m, 0)),
                      pl.BlockSpec((1, d), full), pl.BlockSpec(wg.shape, full), pl.BlockSpec(wp.shape, full),
                      pl.BlockSpec(memory_space=pl.ANY)],
            out_specs=pl.BlockSpec((tm, d), lambda m, dest: (m, 0)),
            scratch_shapes=[pltpu.VMEM((2, topk, tm * SUBLANES, LANES), F32), pltpu.SemaphoreType.DMA((2,))],
        ),
        out_shape=jax.ShapeDtypeStruct((t, d), F32),
        compiler_params=_params("arbitrary"),
        name="moe_combine",
    )(dest, x1, meta, p2, g, wg, wp, y)


def moe_ffn(hf, x1, logits, w1, w3, w2, ple_args, *, tm_route, tm_rows, tf, tm_move):
    t, d = x1.shape
    topk = 2
    meta, tab, cnt = route(logits, tm=tm_route)
    counts = cnt[0, :N_EXPERTS].astype(jnp.int32)
    padded = ((counts + tm_rows - 1) // tm_rows) * tm_rows
    pad_end = jnp.cumsum(padded).astype(jnp.int32)
    pad_start = pad_end - padded
    eidx = tab[META_E0:META_E1 + 1].astype(jnp.int32)
    rank = tab[META_R0:META_R1 + 1].astype(jnp.int32)
    dest = rank
    for e in range(N_EXPERTS):
        dest = dest + jnp.where(eidx == e, pad_start[e], 0)
    dest = dest.reshape(topk * t)
    n_tiles = -(-(t * topk) // tm_rows) + N_EXPERTS
    tile_start = jnp.arange(n_tiles, dtype=jnp.int32) * tm_rows
    tile_expert = jnp.minimum(jnp.sum(tile_start[:, None] >= pad_end[None, :], axis=1),
                              N_EXPERTS - 1).astype(jnp.int32)
    n_active = pad_end[N_EXPERTS - 1:] // tm_rows
    xs = dispatch(dest, pad_end, n_active, hf, n_rows=n_tiles * tm_rows, tm=tm_move, topk=topk, tm_rows=tm_rows)
    y = experts(tile_expert, n_active, xs, w1, w3, w2, tm=tm_rows, tf=tf, fc=256)
    return combine(dest, x1, meta, y, *ple_args, tm=tm_move, topk=topk)


def _tile2(g):
    return jnp.concatenate([g, g]).reshape(1, 2 * g.shape[0])


def kernel(x, p, g_mix, w_in, g_q, g_k, conv_w, conv_b, b_i, b_f, g_h, w_oa, w_ob, w_out, g_ffn, w_d1, w_d3,
           w_d2, w_router, w_e1, w_e3, w_e2, g_ple, w_ple_gate, w_ple_proj):
    batch, seq, d = x.shape
    depth = w_in.shape[0]
    t = batch * seq
    nh = MLSTM_HEADS
    x2 = x.reshape(t, d)
    c_q, c_k, c_v = 0, 512, 1024
    c_qk, c_vm, c_om, c_i, c_f, c_ga, c_gb, c_end = 1536, 2560, 3072, 3584, 3588, 3592, 4616, 5640

    w_in_t = jnp.swapaxes(w_in, 1, 2).astype(BF16)
    for l in range(depth):
        w_bf = w_in_t[l]
        w_gates = w_bf[c_ga:c_end]
        proj, gif = in_proj(x2, g_mix[l].reshape(1, d), w_gates, w_bf, nb_cols=c_i, if_col=c_i,
                            tm=1024, tn=512)

        ya = moba(proj, _tile2(g_q[l]), _tile2(g_k[l]), batch=batch, seq=seq)

        bias = jnp.concatenate([b_i[l], b_f[l]])
        bias_row = jnp.pad(bias, (0, LANES - 2 * nh)).reshape(1, LANES)
        bias_col = bias.reshape(2 * nh, 1)
        gates_row = gif[:, :2 * nh].reshape(batch, seq, 2 * nh).transpose(0, 2, 1)
        yb = mlstm(proj, gif, gates_row, bias_row, bias_col, conv_w[l], conv_b[l].reshape(1, -1),
                   g_h[l].reshape(1, -1), batch=batch, seq=seq, chunk=256)

        j = l // 2
        moe = l % 2 == 1
        wr = jnp.pad(w_router[j], ((0, 0), (0, LANES - N_EXPERTS))) if moe else None
        outs = merge(ya, yb, proj, x2, w_oa[l].astype(BF16), w_ob[l].astype(BF16), w_out[l].astype(BF16),
                     g_ffn[l].reshape(1, d), wr, tm=1024)
        ple_args = (p.reshape(depth * t, -1), l * t, g_ple[l].reshape(1, d), w_ple_gate[l].astype(BF16),
                    w_ple_proj[l].astype(BF16))
        if moe:
            x1, hf, logits = outs
            x2 = moe_ffn(hf, x1, logits, w_e1[j], w_e3[j].astype(BF16), w_e2[j],
                         ple_args, tm_route=512, tm_rows=512, tf=1792, tm_move=512)
        else:
            x1, hf = outs
            x2 = dense_ffn(hf, x1, w_d1[j].astype(BF16), w_d3[j].astype(BF16), w_d2[j].astype(BF16),
                           *ple_args, tm=1024, fc=256)
    return x2.reshape(batch, seq, d)
```

```python
import functools

import jax
import jax.numpy as jnp
from jax import lax
from jax.experimental import pallas as pl
from jax.experimental.pallas import tpu as pltpu

F32 = jnp.float32
BF16 = jnp.bfloat16

RMS_EPS = 1e-6
LANES = 128
SUBLANES = 8

MOBA_HEADS = 8
MOBA_HEAD_DIM = 64
MOBA_BLOCK = 256
MOBA_TOPK = 3
MLSTM_HEADS = 4
MLSTM_DIM = 128
CONV_WIDTH = 4
N_EXPERTS = 8

COL_GA, COL_GB = 0, 8
COL_QA, COL_KA, COL_VA = 16, 20, 24
COL_QM, COL_KM, COL_VM, COL_OM = 28, 32, 36, 40
N_PROJ = 44 * LANES

VMEM_LIMIT = 56 * 1024 * 1024
ROW_DMA_UNROLL = 16


def _params(*sem):
    return pltpu.CompilerParams(dimension_semantics=sem, vmem_limit_bytes=VMEM_LIMIT)


def _sigmoid(x):
    return 1.0 / (1.0 + jnp.exp(-x))


def _rms(x, g):
    return x * lax.rsqrt(jnp.mean(x * x, axis=-1, keepdims=True) + RMS_EPS) * g


def _split_bf16(x):
    hi = x.astype(BF16)
    return hi, (x - hi.astype(F32)).astype(BF16)


def _store_row_tiles(ref, x):
    ref[...] = x.reshape(x.shape[0] * SUBLANES, LANES)


def _load_row_tiles(ref, rows):
    return ref[...].reshape(rows, SUBLANES * LANES)


def _nt_dot(a, b, **kw):
    return lax.dot_general(a, b, (((1,), (1,)), ((), ())), preferred_element_type=F32, **kw)


def _in_proj_body(x_ref, g_ref, wa_ref, wb_ref, o_ref, oif_ref, h_ref, *, nb_cols, if_col, tn):
    h_ref[...] = _rms(x_ref[...], g_ref[...]).astype(BF16)
    oif_ref[...] = _nt_dot(h_ref[...], wb_ref[if_col:if_col + LANES, :])
    na = wa_ref.shape[0]
    for c0 in range(0, na + nb_cols, tn):
        w = wa_ref[c0:c0 + tn, :] if c0 < na else wb_ref[c0 - na:c0 - na + tn, :]
        o_ref[:, c0:c0 + tn] = _nt_dot(h_ref[...], w).astype(o_ref.dtype)


def in_proj(x2, g, wa, w_full, *, nb_cols, if_col, tm, tn):
    t, d = x2.shape
    n = wa.shape[0] + nb_cols
    resident = dict(pipeline_mode=pl.Buffered(1))
    return pl.pallas_call(
        functools.partial(_in_proj_body, nb_cols=nb_cols, if_col=if_col, tn=tn),
        grid=(t // tm,),
        in_specs=[
            pl.BlockSpec((tm, d), lambda m: (m, 0)),
            pl.BlockSpec((1, d), lambda m: (0, 0)),
            pl.BlockSpec(wa.shape, lambda m: (0, 0), **resident),
            pl.BlockSpec(w_full.shape, lambda m: (0, 0), **resident),
        ],
        out_specs=[
            pl.BlockSpec((tm, n), lambda m: (m, 0)),
            pl.BlockSpec((tm, LANES), lambda m: (m, 0)),
        ],
        out_shape=[jax.ShapeDtypeStruct((t, n), BF16), jax.ShapeDtypeStruct((t, LANES), F32)],
        scratch_shapes=[pltpu.VMEM((tm, d), BF16)],
        compiler_params=_params("arbitrary"),
        name="in_proj",
    )(x2, g, wa, w_full)


MASK_BIAS = -1e30
LOG2_E = 1.4426950408889634


def _moba_body(q_ref, k_ref, v_ref, gq_ref, gk_ref, o_ref,
               kn_ref, vt_ref, kmean_ref, qaug_ref, s_ref, m_ref, alpha_ref, acc_ref,
               *, nb, blk, dh, topk, nheads):
    i = pl.program_id(1)
    pair = 2 * blk
    lane = lax.broadcasted_iota(jnp.int32, (1, LANES), 1)
    head0 = lane < dh

    same_head = (lax.broadcasted_iota(jnp.int32, (LANES, LANES), 0) // dh
                 == lax.broadcasted_iota(jnp.int32, (LANES, LANES), 1) // dh)
    head_ones = jnp.where(same_head, 1.0, 0.0).astype(BF16)

    def head_rms(x, g, on_mxu):
        x2 = x * x
        if on_mxu:
            hi, lo = _split_bf16(x2)
            ss = (jnp.dot(hi, head_ones, preferred_element_type=F32)
                  + jnp.dot(lo, head_ones, preferred_element_type=F32))
        else:
            s0 = jnp.sum(jnp.where(head0, x2, 0.0), axis=-1, keepdims=True)
            s1 = jnp.sum(jnp.where(head0, 0.0, x2), axis=-1, keepdims=True)
            ss = jnp.where(head0, s0, s1)
        return x * lax.rsqrt(ss * (1.0 / dh) + RMS_EPS) * g

    @pl.when(i == 0)
    def _():
        def prep(j, c):
            r0 = pl.multiple_of(j * blk, blk)
            onehot = jnp.where(lane == dh + j, 1.0, 0.0)
            for p in range(nheads // 2):
                cols = slice(p * LANES, (p + 1) * LANES)
                kn = head_rms(k_ref[pl.ds(r0, blk), cols].astype(F32), gk_ref[...], True)
                for hh, kh in ((0, kn), (1, pltpu.roll(kn, dh, axis=1))):
                    h = 2 * p + hh
                    kmean_ref[h, pl.ds(j, 1), :] = jnp.mean(jnp.where(head0, kh, 0.0), axis=0, keepdims=True)
                    kn_ref[h, pl.ds(r0, blk), :] = jnp.where(head0, kh, onehot).astype(BF16)
                v_t = v_ref[pl.ds(r0, blk), cols].astype(F32).T.astype(BF16)
                for hh in range(2):
                    vt_ref[2 * p + hh, 0:dh, pl.ds(r0, blk)] = v_t[hh * dh:(hh + 1) * dh, :]
                    vt_ref[2 * p + hh, dh:, pl.ds(r0, blk)] = jnp.ones((vt_ref.shape[1] - dh, blk), BF16)
            return c

        lax.fori_loop(0, nb, prep, 0)

    jidx = lax.broadcasted_iota(jnp.int32, (nb, blk), 0)
    key_i = lax.broadcasted_iota(jnp.int32, (blk, blk), 0)
    qry_i = lax.broadcasted_iota(jnp.int32, (blk, blk), 1)
    causal = key_i <= qry_i
    r_own = pl.multiple_of(i * blk, blk)
    qk_scale = dh ** -0.5 * LOG2_E
    for p in range(nheads // 2):
        cols = slice(p * LANES, (p + 1) * LANES)
        qn_t = head_rms(q_ref[:, cols].astype(F32), gq_ref[...], False).T
        for hh in range(2):
            h = 2 * p + hh
            q_t = qn_t[hh * dh:(hh + 1) * dh, :]
            gate = jnp.dot(kmean_ref[h].astype(BF16),
                           jnp.concatenate([q_t, jnp.zeros((LANES - dh, blk), F32)], axis=0).astype(BF16),
                           preferred_element_type=F32)
            rank = jnp.zeros((nb, blk), F32)
            for jp in range(nb):
                row = gate[jp:jp + 1, :]
                beats = (row > gate) | ((row == gate) & (jidx > jp))
                rank = rank + jnp.where(beats, jnp.where(jp < i, 1.0, 0.0), 0.0)
            sel = (rank < topk) & (jidx < i)
            q_s = q_t * qk_scale
            pad = jnp.zeros((LANES - dh - nb, blk), F32)
            qaug_ref[h] = jnp.concatenate([q_s, jnp.where(sel, 0.0, MASK_BIAS), pad], axis=0).astype(BF16)
            qaug_own = jnp.concatenate([q_s, jnp.where(jidx == i, 0.0, MASK_BIAS), pad], axis=0).astype(BF16)
            st = jnp.dot(kn_ref[h, pl.ds(r_own, blk), :], qaug_own, preferred_element_type=F32)
            st = jnp.where(causal, st, -jnp.inf)
            s_ref[h, 0:blk, :] = st
            m_ref[h] = jnp.max(st, axis=0, keepdims=True)

    def finish_own(h):
        pr = jnp.exp2(s_ref[h, 0:blk, :] - m_ref[h]).astype(BF16)
        acc_ref[h] = jnp.dot(vt_ref[h, :, pl.ds(r_own, blk)], pr, preferred_element_type=F32)

    def score_pair(u, h):
        r0 = pl.multiple_of(u * pair, pair)
        st = jnp.dot(kn_ref[h, pl.ds(r0, pair), :], qaug_ref[h], preferred_element_type=F32)
        m_old = m_ref[h]
        m_new = jnp.maximum(m_old, jnp.max(st, axis=0, keepdims=True))
        s_ref[h] = st
        alpha_ref[h] = jnp.exp2(m_old - m_new)
        m_ref[h] = m_new

    def finish_pair(u, h):
        r0 = pl.multiple_of(u * pair, pair)
        pr = jnp.exp2(s_ref[h] - m_ref[h]).astype(BF16)
        acc_ref[h] = alpha_ref[h] * acc_ref[h] + jnp.dot(vt_ref[h, :, pl.ds(r0, pair)], pr,
                                                         preferred_element_type=F32)

    n_pairs = jnp.maximum((i + 1) // 2, 1)
    for h in range(nheads):
        finish_own(h)
        score_pair(0, h)

    def body(u, c):
        for h in range(nheads):
            finish_pair(u - 1, h)
            score_pair(u, h)
        return c

    lax.fori_loop(1, n_pairs, body, 0)
    for h in range(nheads):
        finish_pair(n_pairs - 1, h)

    for p in range(nheads // 2):
        a0 = acc_ref[2 * p]
        a1 = acc_ref[2 * p + 1]
        ot = jnp.concatenate([a0[0:dh] / a0[dh:dh + 1], a1[0:dh] / a1[dh:dh + 1]], axis=0)
        o_ref[:, p * LANES:(p + 1) * LANES] = ot.T.astype(o_ref.dtype)


def moba(proj, gq2, gk2, *, batch, seq):
    nb = seq // MOBA_BLOCK
    blk = MOBA_BLOCK
    dh = MOBA_HEAD_DIM
    nheads = MOBA_HEADS
    width = nheads * dh
    wb = width // LANES
    assert dh + nb <= LANES and 2 * dh == LANES and nb % 2 == 0
    v_rows = dh + 2 * SUBLANES
    body = functools.partial(_moba_body, nb=nb, blk=blk, dh=dh, topk=MOBA_TOPK, nheads=nheads)
    return pl.pallas_call(
        body,
        grid=(batch, nb),
        in_specs=[
            pl.BlockSpec((blk, width), lambda b, i: (b * nb + i, COL_QA // wb)),
            pl.BlockSpec((seq, width), lambda b, i: (b, COL_KA // wb)),
            pl.BlockSpec((seq, width), lambda b, i: (b, COL_VA // wb)),
            pl.BlockSpec((1, LANES), lambda b, i: (0, 0)),
            pl.BlockSpec((1, LANES), lambda b, i: (0, 0)),
        ],
        out_specs=pl.BlockSpec((blk, width), lambda b, i: (b * nb + i, 0)),
        out_shape=jax.ShapeDtypeStruct((batch * seq, width), BF16),
        scratch_shapes=[
            pltpu.VMEM((nheads, seq, LANES), BF16),
            pltpu.VMEM((nheads, v_rows, seq), BF16),
            pltpu.VMEM((nheads, nb, LANES), F32),
            pltpu.VMEM((nheads, LANES, blk), BF16),
            pltpu.VMEM((nheads, 2 * blk, blk), F32),
            pltpu.VMEM((nheads, 1, blk), F32),
            pltpu.VMEM((nheads, 1, blk), F32),
            pltpu.VMEM((nheads, v_rows, blk), F32),
        ],
        compiler_params=_params("arbitrary", "arbitrary"),
        name="moba",
    )(proj, proj, proj, gq2, gk2)


def _log_sigmoid(x):
    return jnp.minimum(x, 0.0) - jnp.log(1.0 + jnp.exp(-jnp.abs(x)))


def _dot_tri(tri, x, tri_left):
    out = None
    for _ in range(3):
        piece = x.astype(BF16)
        x = x - piece.astype(F32)
        term = (jnp.dot(tri, piece, preferred_element_type=F32) if tri_left
                else jnp.dot(piece, tri, preferred_element_type=F32))
        out = term if out is None else out + term
    return out


def _mlstm_body(qr_ref, kr_ref, v_ref, og_ref, gcol_ref, grow_ref, brow_ref, bcol_ref,
                cwq_ref, cwk_ref, cbq_ref, cbk_ref, gh_ref, o_ref,
                qx_ref, kx_ref, c_ref, m_ref, *, chunk, dk, nh):
    L = chunk
    width = nh * dk

    @pl.when(pl.program_id(1) == 0)
    def _():
        qx_ref[0:SUBLANES, :] = jnp.zeros((SUBLANES, width), F32)
        kx_ref[0:SUBLANES, :] = jnp.zeros((SUBLANES, width), F32)
        c_ref[...] = jnp.zeros_like(c_ref)
        m_ref[...] = jnp.zeros_like(m_ref)

    qx_ref[SUBLANES:SUBLANES + L, :] = qr_ref[...].astype(F32)
    kx_ref[SUBLANES:SUBLANES + L, :] = kr_ref[...].astype(F32)

    def conv_silu(x_ref, w_ref, b_ref):
        acc = b_ref[...] + w_ref[0:1, :] * x_ref[pl.ds(SUBLANES - CONV_WIDTH + 1, L), :]
        for j in range(1, CONV_WIDTH):
            acc = acc + w_ref[j:j + 1, :] * x_ref[pl.ds(SUBLANES - CONV_WIDTH + 1 + j, L), :]
        return acc * _sigmoid(acc)

    q_all = conv_silu(qx_ref, cwq_ref, cbq_ref)
    k_all = conv_silu(kx_ref, cwk_ref, cbk_ref) * (dk ** -0.5)
    qx_ref[0:SUBLANES, :] = qx_ref[L:L + SUBLANES, :]
    kx_ref[0:SUBLANES, :] = kx_ref[L:L + SUBLANES, :]

    pre_col = gcol_ref[...] + brow_ref[...]
    pre_row = grow_ref[0] + bcol_ref[...]
    t_i = lax.broadcasted_iota(jnp.int32, (L, L), 0)
    s_i = lax.broadcasted_iota(jnp.int32, (L, L), 1)
    tril = s_i <= t_i
    bcum_cols = _dot_tri(jnp.where(tril, 1.0, 0.0).astype(BF16), _log_sigmoid(pre_col), True)
    bcum_rows = _dot_tri(jnp.where(t_i <= s_i, 1.0, 0.0).astype(BF16), _log_sigmoid(pre_row), False)
    ones = jnp.ones((L, dk), BF16)

    for h in range(nh):
        cols = slice(h * dk, (h + 1) * dk)
        q = q_all[:, cols]
        k = k_all[:, cols]
        i_col = pre_col[:, h:h + 1]
        i_row = pre_row[h:h + 1, :]
        bcum_col = bcum_cols[:, nh + h:nh + h + 1]
        bcum_row = bcum_rows[nh + h:nh + h + 1, :]

        m_prev = m_ref[h, 0:1, 0:1]
        a_col = bcum_col + m_prev
        dmat = jnp.where(tril, bcum_col - bcum_row + i_row, -jnp.inf)
        m_t = jnp.maximum(a_col, jnp.max(dmat, axis=-1, keepdims=True))
        dw = jnp.exp(dmat - m_t)
        aw = jnp.exp(a_col - m_t)

        qb = q.astype(BF16)
        kb = k.astype(BF16)
        v_aug = jnp.concatenate([v_ref[:, cols], ones], axis=-1)
        sqk = _nt_dot(qb, kb) * dw
        num_aug = (aw * jnp.dot(qb, c_ref[h].astype(BF16), preferred_element_type=F32)
                   + jnp.dot(sqk.astype(BF16), v_aug, preferred_element_type=F32))
        den = num_aug[:, dk:dk + 1]
        hc = num_aug[:, 0:dk] / jnp.maximum(jnp.abs(den), jnp.exp(-m_t))

        b_last = bcum_col[L - 1:L, :]
        g_col = b_last - bcum_col + i_col
        m_new = jnp.maximum(b_last + m_prev, jnp.max(g_col, axis=0, keepdims=True))
        w_c = jnp.exp(b_last + m_prev - m_new)
        kw_t = (k * jnp.exp(g_col - m_new)).T.astype(BF16)
        c_ref[h] = w_c * c_ref[h] + jnp.dot(kw_t, v_aug, preferred_element_type=F32)
        m_ref[h] = jnp.broadcast_to(m_new, (1, LANES))

        o_ref[:, cols] = (_rms(hc, gh_ref[...]) * _sigmoid(og_ref[:, cols].astype(F32))).astype(o_ref.dtype)


def mlstm(proj, gates_col, gates_row, bias_row, bias_col, conv_w, conv_b, gh, *, batch, seq, chunk):
    nh = MLSTM_HEADS
    dk = MLSTM_DIM
    width = nh * dk
    wb = width // LANES
    nc = seq // chunk
    body = functools.partial(_mlstm_body, chunk=chunk, dk=dk, nh=nh)

    def rows(col0):
        return pl.BlockSpec((chunk, width), lambda b, c: (b * nc + c, col0 // wb))

    return pl.pallas_call(
        body,
        grid=(batch, nc),
        in_specs=[
            rows(COL_QM), rows(COL_KM), rows(COL_VM), rows(COL_OM),
            pl.BlockSpec((chunk, LANES), lambda b, c: (b * nc + c, 0)),
            pl.BlockSpec((1, SUBLANES, chunk), lambda b, c: (b, 0, c)),
            pl.BlockSpec((1, LANES), lambda b, c: (0, 0)),
            pl.BlockSpec((SUBLANES, 1), lambda b, c: (0, 0)),
            pl.BlockSpec((CONV_WIDTH, width), lambda b, c: (0, 0)),
            pl.BlockSpec((CONV_WIDTH, width), lambda b, c: (0, 1)),
            pl.BlockSpec((1, width), lambda b, c: (0, 0)),
            pl.BlockSpec((1, width), lambda b, c: (0, 1)),
            pl.BlockSpec((1, LANES), lambda b, c: (0, 0)),
        ],
        out_specs=pl.BlockSpec((chunk, width), lambda b, c: (b * nc + c, 0)),
        out_shape=jax.ShapeDtypeStruct((batch * seq, width), BF16),
        scratch_shapes=[
            pltpu.VMEM((chunk + 2 * SUBLANES, width), F32),
            pltpu.VMEM((chunk + 2 * SUBLANES, width), F32),
            pltpu.VMEM((nh, dk, 2 * dk), F32),
            pltpu.VMEM((nh, 1, LANES), F32),
        ],
        compiler_params=_params("arbitrary", "arbitrary"),
        name="mlstm",
    )(proj, proj, proj, proj, gates_col, gates_row, bias_row, bias_col,
      conv_w, conv_w, conv_b, conv_b, gh)


def _merge_body(ya_ref, yb_ref, ga_ref, gb_ref, x_ref, woa_ref, wob_ref, wout_ref, gffn_ref, *rest, moe):
    a = jnp.dot(ya_ref[...], woa_ref[...], preferred_element_type=F32)
    b = jnp.dot(yb_ref[...], wob_ref[...], preferred_element_type=F32)
    mixed = _sigmoid(ga_ref[...].astype(F32)) * a + _sigmoid(gb_ref[...].astype(F32)) * b
    x1 = x_ref[...] + jnp.dot(mixed.astype(BF16), wout_ref[...], preferred_element_type=F32)
    hf = _rms(x1, gffn_ref[...])
    if moe:
        wr_ref, x1_ref, hf_ref, lg_ref = rest
        _store_row_tiles(hf_ref, hf)
        lg_ref[...] = jnp.dot(hf.astype(BF16), wr_ref[...].astype(BF16), preferred_element_type=F32)
    else:
        x1_ref, hf_ref = rest
        hf_ref[...] = hf.astype(BF16)
    x1_ref[...] = x1


def merge(ya, yb, proj, x2, woa, wob, wout, gffn, wr, *, tm):
    t, d = x2.shape
    moe = wr is not None
    full = lambda m: (0, 0)
    in_specs = [
        pl.BlockSpec((tm, ya.shape[1]), lambda m: (m, 0)),
        pl.BlockSpec((tm, yb.shape[1]), lambda m: (m, 0)),
        pl.BlockSpec((tm, d), lambda m: (m, COL_GA * LANES // d)),
        pl.BlockSpec((tm, d), lambda m: (m, COL_GB * LANES // d)),
        pl.BlockSpec((tm, d), lambda m: (m, 0)),
        pl.BlockSpec(woa.shape, full), pl.BlockSpec(wob.shape, full), pl.BlockSpec(wout.shape, full),
        pl.BlockSpec((1, d), full),
    ]
    args = [ya, yb, proj, proj, x2, woa, wob, wout, gffn]
    out_specs = [pl.BlockSpec((tm, d), lambda m: (m, 0)), pl.BlockSpec((tm, d), lambda m: (m, 0))]
    out_shape = [jax.ShapeDtypeStruct((t, d), F32), jax.ShapeDtypeStruct((t, d), BF16)]
    if moe:
        out_specs[1] = pl.BlockSpec((tm * SUBLANES, LANES), lambda m: (m, 0))
        out_shape[1] = jax.ShapeDtypeStruct((t * SUBLANES, LANES), F32)
    if moe:
        in_specs.append(pl.BlockSpec(wr.shape, full))
        args.append(wr)
        out_specs.append(pl.BlockSpec((tm, LANES), lambda m: (m, 0)))
        out_shape.append(jax.ShapeDtypeStruct((t, LANES), F32))
    return pl.pallas_call(
        functools.partial(_merge_body, moe=moe),
        grid=(t // tm,),
        in_specs=in_specs, out_specs=out_specs, out_shape=out_shape,
        compiler_params=_params("arbitrary"),
        name="merge_moe" if moe else "merge",
    )(*args)


def _swiglu_gate(x, w1_ref, w3_ref, g_ref, fc):
    dff = g_ref.shape[1]
    for f0 in range(0, dff, fc):
        a = jnp.dot(x, w1_ref[:, f0:f0 + fc].astype(BF16), preferred_element_type=F32)
        b = jnp.dot(x, w3_ref[:, f0:f0 + fc].astype(BF16), preferred_element_type=F32)
        g_ref[:, f0:f0 + fc] = (a * _sigmoid(a) * b).astype(BF16)


def _swiglu(x, w1_ref, w3_ref, w2_ref, g_ref, fc):
    _swiglu_gate(x, w1_ref, w3_ref, g_ref, fc)
    return jnp.dot(g_ref[...], w2_ref[...].astype(BF16), preferred_element_type=F32)


def _ple(x, p_ref, g_ref, wg_ref, wp_ref):
    gate = _sigmoid(jnp.dot(_rms(x, g_ref[...]).astype(BF16), wg_ref[...], preferred_element_type=F32))
    emb = jnp.dot(p_ref[...].astype(BF16), wp_ref[...], preferred_element_type=F32)
    return x + gate * emb


def _ffn_body(hf_ref, x1_ref, w1_ref, w3_ref, w2_ref, p_ref, g_ref, wg_ref, wp_ref, o_ref, act_ref, *, fc):
    x2 = x1_ref[...] + _swiglu(hf_ref[...], w1_ref, w3_ref, w2_ref, act_ref, fc)
    o_ref[...] = _ple(x2, p_ref, g_ref, wg_ref, wp_ref)


def dense_ffn(hf, x1, w1, w3, w2, p2, p_row0, g, wg, wp, *, tm, fc):
    t, d = x1.shape
    dff = w1.shape[1]
    p_blk0 = p_row0 // tm
    resident = dict(pipeline_mode=pl.Buffered(1))
    full = lambda m: (0, 0)
    return pl.pallas_call(
        functools.partial(_ffn_body, fc=fc),
        grid=(t // tm,),
        in_specs=[
            pl.BlockSpec((tm, d), lambda m: (m, 0)),
            pl.BlockSpec((tm, d), lambda m: (m, 0)),
            pl.BlockSpec((d, dff), full, **resident),
            pl.BlockSpec((d, dff), full, **resident),
            pl.BlockSpec((dff, d), full, **resident),
            pl.BlockSpec((tm, p2.shape[1]), lambda m: (p_blk0 + m, 0)),
            pl.BlockSpec((1, d), full),
            pl.BlockSpec(wg.shape, full, **resident),
            pl.BlockSpec(wp.shape, full, **resident),
        ],
        out_specs=pl.BlockSpec((tm, d), lambda m: (m, 0)),
        out_shape=jax.ShapeDtypeStruct((t, d), F32),
        scratch_shapes=[pltpu.VMEM((tm, dff), BF16)],
        compiler_params=_params("arbitrary"),
        name="dense_ffn",
    )(hf, x1, w1, w3, w2, p2, g, wg, wp)


META_E0, META_E1, META_G0, META_G1, META_R0, META_R1 = 0, 1, 2, 3, 4, 5


def _route_body(lg_ref, meta_ref, tab_ref, cnt_ref, carry_ref, *, tm, ne):
    @pl.when(pl.program_id(0) == 0)
    def _():
        carry_ref[...] = jnp.zeros_like(carry_ref)

    lane = lax.broadcasted_iota(jnp.int32, (tm, LANES), 1)
    lanef = lane.astype(F32)
    lg = jnp.where(lane < ne, lg_ref[...], -jnp.inf)
    m1 = jnp.max(lg, axis=-1, keepdims=True)
    e1 = jnp.min(jnp.where(lg == m1, lanef, float(LANES)), axis=-1, keepdims=True)
    lg2 = jnp.where(lanef == e1, -jnp.inf, lg)
    m2 = jnp.max(lg2, axis=-1, keepdims=True)
    e2 = jnp.min(jnp.where(lg2 == m2, lanef, float(LANES)), axis=-1, keepdims=True)
    ex = jnp.exp(m2 - m1)
    g1 = 1.0 / (1.0 + ex)
    g2 = ex / (1.0 + ex)
    onehot = jnp.where((lanef == e1) | (lanef == e2), 1.0, 0.0)
    row = lax.broadcasted_iota(jnp.int32, (tm, tm), 0)
    col = lax.broadcasted_iota(jnp.int32, (tm, tm), 1)
    before = jnp.dot(jnp.where(col < row, 1.0, 0.0).astype(BF16), onehot.astype(BF16),
                     preferred_element_type=F32) + carry_ref[...]
    r1 = jnp.sum(jnp.where(lanef == e1, before, 0.0), axis=-1, keepdims=True)
    r2 = jnp.sum(jnp.where(lanef == e2, before, 0.0), axis=-1, keepdims=True)
    carry_ref[...] += jnp.sum(onehot, axis=0, keepdims=True)
    meta = jnp.zeros((tm, LANES), F32)
    for pos, val in ((META_E0, e1), (META_E1, e2), (META_G0, g1), (META_G1, g2), (META_R0, r1), (META_R1, r2)):
        meta = jnp.where(lane == pos, val, meta)
    meta_ref[...] = meta
    tab_ref[...] = meta.T[0:SUBLANES, :]
    cnt_ref[...] = jnp.broadcast_to(carry_ref[...], cnt_ref.shape)


def route(logits, *, tm):
    t = logits.shape[0]
    return pl.pallas_call(
        functools.partial(_route_body, tm=tm, ne=N_EXPERTS),
        grid=(t // tm,),
        in_specs=[pl.BlockSpec((tm, LANES), lambda m: (m, 0))],
        out_specs=[pl.BlockSpec((tm, LANES), lambda m: (m, 0)),
                   pl.BlockSpec((SUBLANES, tm), lambda m: (0, m)),
                   pl.BlockSpec((SUBLANES, LANES), lambda m: (0, 0))],
        out_shape=[jax.ShapeDtypeStruct((t, LANES), F32), jax.ShapeDtypeStruct((SUBLANES, t), F32),
                   jax.ShapeDtypeStruct((SUBLANES, LANES), F32)],
        scratch_shapes=[pltpu.VMEM((1, LANES), F32)],
        compiler_params=_params("arbitrary"),
        name="route",
    )(logits)


def _dispatch_body(dest_ref, pe_ref, na_ref, hf_ref, xs_ref, zero_ref, src_ref, sem, zsem, bsem,
                   *, tm, topk, tm_rows, n_tok):
    m = pl.program_id(0)
    n_steps = pl.num_programs(0)
    base = m * tm
    slot = m % 2

    def stage(tile, s):
        return pltpu.make_async_copy(hf_ref.at[pl.ds(pl.multiple_of(tile * tm * SUBLANES, tm * SUBLANES),
                                                     tm * SUBLANES)], src_ref.at[s], bsem.at[s])

    def wait_rows(s):
        for k in range(topk):
            pltpu.make_async_copy(src_ref.at[s], xs_ref.at[pl.ds(0, tm * SUBLANES)], sem.at[s]).wait()

    @pl.when(m == 0)
    def _():
        stage(0, 0).start()

    @pl.when(m == 0)
    def _():
        zero_ref[...] = jnp.zeros_like(zero_ref)
        n_tiles = xs_ref.shape[0] // (tm_rows * SUBLANES)
        fills = []
        for e in range(N_EXPERTS):
            end = pe_ref[e]
            nonempty = end > (pe_ref[e - 1] if e else 0)
            fills.append((nonempty, pl.multiple_of(jnp.maximum(end - tm_rows, 0), tm_rows)))
        for tile in range(n_tiles):
            fills.append((tile >= na_ref[0], tile * tm_rows))
        for phase in ("start", "wait"):
            for cond, row0 in fills:
                @pl.when(cond)
                def _(row0=row0, phase=phase):
                    cp = pltpu.make_async_copy(
                        zero_ref, xs_ref.at[pl.ds(row0 * SUBLANES, tm_rows * SUBLANES)], zsem)
                    cp.start() if phase == "start" else cp.wait()

    stage(m, slot).wait()

    def issue(r, c):
        for k in range(topk):
            d = dest_ref[k * n_tok + base + r]
            pltpu.make_async_copy(_row_tile(src_ref.at[slot], r), _row_tile(xs_ref, d),
                                  sem.at[slot]).start(priority=k % 2)
        return c

    lax.fori_loop(0, tm, issue, 0, unroll=ROW_DMA_UNROLL)

    @pl.when(m > 0)
    def _():
        wait_rows(1 - slot)

    @pl.when(m + 1 < n_steps)
    def _():
        stage(m + 1, 1 - slot).start()

    @pl.when(m == n_steps - 1)
    def _():
        wait_rows(slot)


def _row_tile(ref, r):
    return ref.at[pl.ds(pl.multiple_of(r * SUBLANES, SUBLANES), SUBLANES)]


def dispatch(dest, pad_end, n_active, hf, *, n_rows, tm, topk, tm_rows):
    t = hf.shape[0] // SUBLANES
    return pl.pallas_call(
        functools.partial(_dispatch_body, tm=tm, topk=topk, tm_rows=tm_rows, n_tok=t),
        grid_spec=pltpu.PrefetchScalarGridSpec(
            num_scalar_prefetch=3,
            grid=(t // tm,),
            in_specs=[pl.BlockSpec(memory_space=pl.ANY)],
            out_specs=pl.BlockSpec(memory_space=pl.ANY),
            scratch_shapes=[pltpu.VMEM((tm_rows * SUBLANES, LANES), F32),
                            pltpu.VMEM((2, tm * SUBLANES, LANES), F32),
                            pltpu.SemaphoreType.DMA((2,)), pltpu.SemaphoreType.DMA, pltpu.SemaphoreType.DMA((2,))],
        ),
        out_shape=jax.ShapeDtypeStruct((n_rows * SUBLANES, LANES), F32),
        compiler_params=_params("arbitrary"),
        name="moe_dispatch",
    )(dest, pad_end, n_active, hf)


def _experts_body(te_ref, na_ref, xs_ref, w1_ref, w3_ref, w2_ref, y_ref, xb_ref, g_ref, acc_ref, *, fc):
    del te_ref
    i = pl.program_id(0)
    f = pl.program_id(1)
    last = pl.num_programs(1) - 1
    tm = xb_ref.shape[0]

    @pl.when(i >= na_ref[0])
    def _():
        y_ref[...] = jnp.zeros_like(y_ref)

    @pl.when(i < na_ref[0])
    def _():
        @pl.when(f == 0)
        def _():
            xb_ref[...] = _load_row_tiles(xs_ref, tm).astype(BF16)

        y = _swiglu(xb_ref[...], w1_ref.at[0], w3_ref.at[0], w2_ref.at[0], g_ref, fc)

        @pl.when(f == 0)
        def _():
            acc_ref[...] = y

        @pl.when((f > 0) & (f < last))
        def _():
            acc_ref[...] += y

        @pl.when(f == last)
        def _():
            _store_row_tiles(y_ref, acc_ref[...] + y)


def experts(tile_expert, n_active, xs, w1, w3, w2, *, tm, tf, fc):
    n_rows = xs.shape[0] // SUBLANES
    d = w1.shape[1]
    dff = w1.shape[2]
    assert dff // tf >= 2
    row_tile = lambda i, f, te, na: (jnp.minimum(i, na[0] - 1), 0)
    ftile = lambda i, f, na: jnp.where(i < na[0], f, dff // tf - 1)
    return pl.pallas_call(
        functools.partial(_experts_body, fc=fc),
        grid_spec=pltpu.PrefetchScalarGridSpec(
            num_scalar_prefetch=2,
            grid=(n_rows // tm, dff // tf),
            in_specs=[
                pl.BlockSpec((tm * SUBLANES, LANES), row_tile),
                pl.BlockSpec((1, d, tf), lambda i, f, te, na: (te[i], 0, ftile(i, f, na))),
                pl.BlockSpec((1, d, tf), lambda i, f, te, na: (te[i], 0, ftile(i, f, na))),
                pl.BlockSpec((1, tf, d), lambda i, f, te, na: (te[i], ftile(i, f, na), 0)),
            ],
            out_specs=pl.BlockSpec((tm * SUBLANES, LANES), lambda i, f, te, na: (i, 0)),
            scratch_shapes=[pltpu.VMEM((tm, d), BF16), pltpu.VMEM((tm, tf), BF16), pltpu.VMEM((tm, d), F32)],
        ),
        out_shape=jax.ShapeDtypeStruct((n_rows * SUBLANES, LANES), F32),
        compiler_params=_params("arbitrary", "arbitrary"),
        name="moe_experts",
    )(tile_expert, n_active, xs, w1, w3, w2)


def _combine_body(dest_ref, x1_ref, meta_ref, p_ref, g_ref, wg_ref, wp_ref, y_ref, o_ref, buf_ref, sem,
                  *, tm, topk):
    m = pl.program_id(0)
    n_steps = pl.num_programs(0)
    n_tok = n_steps * tm

    def start_gather(tile, slot):
        def issue(r, c):
            for k in range(topk):
                d = dest_ref[k * n_tok + tile * tm + r]
                pltpu.make_async_copy(_row_tile(y_ref, d), _row_tile(buf_ref.at[slot, k], r),
                                      sem.at[slot]).start(priority=k % 2)
            return c

        lax.fori_loop(0, tm, issue, 0, unroll=ROW_DMA_UNROLL)

    @pl.when(m == 0)
    def _():
        start_gather(0, 0)

    @pl.when(m + 1 < n_steps)
    def _():
        start_gather(m + 1, (m + 1) % 2)

    slot = m % 2
    for k in range(topk):
        pltpu.make_async_copy(y_ref.at[pl.ds(0, tm * SUBLANES)], buf_ref.at[slot, k], sem.at[slot]).wait()
    meta = meta_ref[...]
    g0 = meta[:, META_G0:META_G0 + 1]
    g1 = meta[:, META_G1:META_G1 + 1]
    x2 = x1_ref[...] + (g0 * _load_row_tiles(buf_ref.at[slot, 0], tm)
                        + g1 * _load_row_tiles(buf_ref.at[slot, 1], tm))
    o_ref[...] = _ple(x2, p_ref, g_ref, wg_ref, wp_ref)


def combine(dest, x1, meta, y, p2, p_row0, g, wg, wp, *, tm, topk):
    t, d = x1.shape
    full = lambda m, dest: (0, 0)
    p_blk0 = p_row0 // tm
    return pl.pallas_call(
        functools.partial(_combine_body, tm=tm, topk=topk),
        grid_spec=pltpu.PrefetchScalarGridSpec(
            num_scalar_prefetch=1,
            grid=(t // tm,),
            in_specs=[pl.BlockSpec((tm, d), lambda m, dest: (m, 0)),
                      pl.BlockSpec((tm, LANES), lambda m, dest: (m, 0)),
                      pl.BlockSpec((tm, p2.shape[1]), lambda m, dest: (p_blk0 + m, 0)),
                      pl.BlockSpec((1, d), full), pl.BlockSpec(wg.shape, full), pl.BlockSpec(wp.shape, full),
                      pl.BlockSpec(memory_space=pl.ANY)],
            out_specs=pl.BlockSpec((tm, d), lambda m, dest: (m, 0)),
            scratch_shapes=[pltpu.VMEM((2, topk, tm * SUBLANES, LANES), F32), pltpu.SemaphoreType.DMA((2,))],
        ),
        out_shape=jax.ShapeDtypeStruct((t, d), F32),
        compiler_params=_params("arbitrary"),
        name="moe_combine",
    )(dest, x1, meta, p2, g, wg, wp, y)


def moe_ffn(hf, x1, logits, w1, w3, w2, ple_args, *, tm_route, tm_rows, tf, tm_move):
    t, d = x1.shape
    topk = 2
    meta, tab, cnt = route(logits, tm=tm_route)
    counts = cnt[0, :N_EXPERTS].astype(jnp.int32)
    padded = ((counts + tm_rows - 1) // tm_rows) * tm_rows
    pad_end = jnp.cumsum(padded).astype(jnp.int32)
    pad_start = pad_end - padded
    eidx = tab[META_E0:META_E1 + 1].astype(jnp.int32)
    rank = tab[META_R0:META_R1 + 1].astype(jnp.int32)
    dest = rank
    for e in range(N_EXPERTS):
        dest = dest + jnp.where(eidx == e, pad_start[e], 0)
    dest = dest.reshape(topk * t)
    n_tiles = -(-(t * topk) // tm_rows) + N_EXPERTS
    tile_start = jnp.arange(n_tiles, dtype=jnp.int32) * tm_rows
    tile_expert = jnp.minimum(jnp.sum(tile_start[:, None] >= pad_end[None, :], axis=1),
                              N_EXPERTS - 1).astype(jnp.int32)
    n_active = pad_end[N_EXPERTS - 1:] // tm_rows
    xs = dispatch(dest, pad_end, n_active, hf, n_rows=n_tiles * tm_rows, tm=tm_move, topk=topk, tm_rows=tm_rows)
    y = experts(tile_expert, n_active, xs, w1, w3, w2, tm=tm_rows, tf=tf, fc=256)
    return combine(dest, x1, meta, y, *ple_args, tm=tm_move, topk=topk)


def _tile2(g):
    return jnp.concatenate([g, g]).reshape(1, 2 * g.shape[0])


def kernel(x, p, g_mix, w_in, g_q, g_k, conv_w, conv_b, b_i, b_f, g_h, w_oa, w_ob, w_out, g_ffn, w_d1, w_d3,
           w_d2, w_router, w_e1, w_e3, w_e2, g_ple, w_ple_gate, w_ple_proj):
    batch, seq, d = x.shape
    depth = w_in.shape[0]
    t = batch * seq
    nh = MLSTM_HEADS
    x2 = x.reshape(t, d)
    c_q, c_k, c_v = 0, 512, 1024
    c_qk, c_vm, c_om, c_i, c_f, c_ga, c_gb, c_end = 1536, 2560, 3072, 3584, 3588, 3592, 4616, 5640

    w_in_t = jnp.swapaxes(w_in, 1, 2).astype(BF16)
    for l in range(depth):
        w_bf = w_in_t[l]
        w_gates = w_bf[c_ga:c_end]
        proj, gif = in_proj(x2, g_mix[l].reshape(1, d), w_gates, w_bf, nb_cols=c_i, if_col=c_i,
                            tm=1024, tn=512)

        ya = moba(proj, _tile2(g_q[l]), _tile2(g_k[l]), batch=batch, seq=seq)

        bias = jnp.concatenate([b_i[l], b_f[l]])
        bias_row = jnp.pad(bias, (0, LANES - 2 * nh)).reshape(1, LANES)
        bias_col = bias.reshape(2 * nh, 1)
        gates_row = gif[:, :2 * nh].reshape(batch, seq, 2 * nh).transpose(0, 2, 1)
        yb = mlstm(proj, gif, gates_row, bias_row, bias_col, conv_w[l], conv_b[l].reshape(1, -1),
                   g_h[l].reshape(1, -1), batch=batch, seq=seq, chunk=256)

        j = l // 2
        moe = l % 2 == 1
        wr = jnp.pad(w_router[j], ((0, 0), (0, LANES - N_EXPERTS))) if moe else None
        outs = merge(ya, yb, proj, x2, w_oa[l].astype(BF16), w_ob[l].astype(BF16), w_out[l].astype(BF16),
                     g_ffn[l].reshape(1, d), wr, tm=1024)
        ple_args = (p.reshape(depth * t, -1), l * t, g_ple[l].reshape(1, d), w_ple_gate[l].astype(BF16),
                    w_ple_proj[l].astype(BF16))
        if moe:
            x1, hf, logits = outs
            x2 = moe_ffn(hf, x1, logits, w_e1[j], w_e3[j].astype(BF16), w_e2[j],
                         ple_args, tm_route=512, tm_rows=512, tf=1792, tm_move=512)
        else:
            x1, hf = outs
            x2 = dense_ffn(hf, x1, w_d1[j].astype(BF16), w_d3[j].astype(BF16), w_d2[j].astype(BF16),
                           *ple_args, tm=1024, fc=256)
    return x2.reshape(batch, seq, d)
```

```python
import functools

import jax
import jax.numpy as jnp
from jax import lax
from jax.experimental import pallas as pl
from jax.experimental.pallas import tpu as pltpu

F32 = jnp.float32
BF16 = jnp.bfloat16

RMS_EPS = 1e-6
LANES = 128
SUBLANES = 8

MOBA_HEADS = 8
MOBA_HEAD_DIM = 64
MOBA_BLOCK = 256
MOBA_TOPK = 3
MLSTM_HEADS = 4
MLSTM_DIM = 128
CONV_WIDTH = 4
N_EXPERTS = 8

COL_GA, COL_GB = 0, 8
COL_QA, COL_KA, COL_VA = 16, 20, 24
COL_QM, COL_KM, COL_VM, COL_OM = 28, 32, 36, 40
N_PROJ = 44 * LANES

VMEM_LIMIT = 56 * 1024 * 1024
ROW_DMA_UNROLL = 16


def _params(*sem):
    return pltpu.CompilerParams(dimension_semantics=sem, vmem_limit_bytes=VMEM_LIMIT)


def _sigmoid(x):
    return 1.0 / (1.0 + jnp.exp(-x))


def _rms(x, g):
    return x * lax.rsqrt(jnp.mean(x * x, axis=-1, keepdims=True) + RMS_EPS) * g


def _split_bf16(x):
    hi = x.astype(BF16)
    return hi, (x - hi.astype(F32)).astype(BF16)


def _store_row_tiles(ref, x):
    ref[...] = x.reshape(x.shape[0] * SUBLANES, LANES)


def _load_row_tiles(ref, rows):
    return ref[...].reshape(rows, SUBLANES * LANES)


def _nt_dot(a, b, **kw):
    return lax.dot_general(a, b, (((1,), (1,)), ((), ())), preferred_element_type=F32, **kw)


def _in_proj_body(x_ref, g_ref, wa_ref, wb_ref, o_ref, oif_ref, h_ref, *, nb_cols, if_col, tn):
    h_ref[...] = _rms(x_ref[...], g_ref[...]).astype(BF16)
    oif_ref[...] = _nt_dot(h_ref[...], wb_ref[if_col:if_col + LANES, :])
    na = wa_ref.shape[0]
    for c0 in range(0, na + nb_cols, tn):
        w = wa_ref[c0:c0 + tn, :] if c0 < na else wb_ref[c0 - na:c0 - na + tn, :]
        o_ref[:, c0:c0 + tn] = _nt_dot(h_ref[...], w).astype(o_ref.dtype)


def in_proj(x2, g, wa, w_full, layer, *, nb_cols, if_col, tm, tn):
    t, d = x2.shape
    n = wa.shape[0] + nb_cols
    resident = dict(pipeline_mode=pl.Buffered(1))
    return pl.pallas_call(
        functools.partial(_in_proj_body, nb_cols=nb_cols, if_col=if_col, tn=tn),
        grid=(t // tm,),
        in_specs=[
            pl.BlockSpec((tm, d), lambda m: (m, 0)),
            pl.BlockSpec((1, d), lambda m: (0, 0)),
            pl.BlockSpec(wa.shape, lambda m: (0, 0), **resident),
            pl.BlockSpec((None,) + w_full.shape[1:], lambda m: (layer, 0, 0), **resident),
        ],
        out_specs=[
            pl.BlockSpec((tm, n), lambda m: (m, 0)),
            pl.BlockSpec((tm, LANES), lambda m: (m, 0)),
        ],
        out_shape=[jax.ShapeDtypeStruct((t, n), BF16), jax.ShapeDtypeStruct((t, LANES), F32)],
        scratch_shapes=[pltpu.VMEM((tm, d), BF16)],
        compiler_params=_params("arbitrary"),
        name="in_proj",
    )(x2, g, wa, w_full)


MASK_BIAS = -1e30
LOG2_E = 1.4426950408889634


def _moba_body(q_ref, k_ref, v_ref, gq_ref, gk_ref, o_ref,
               kn_ref, vt_ref, kmean_ref, qaug_ref, s_ref, m_ref, alpha_ref, acc_ref,
               *, nb, blk, dh, topk, nheads):
    i = pl.program_id(1)
    pair = 2 * blk
    lane = lax.broadcasted_iota(jnp.int32, (1, LANES), 1)
    head0 = lane < dh

    same_head = (lax.broadcasted_iota(jnp.int32, (LANES, LANES), 0) // dh
                 == lax.broadcasted_iota(jnp.int32, (LANES, LANES), 1) // dh)
    head_ones = jnp.where(same_head, 1.0, 0.0).astype(BF16)

    def head_rms(x, g, on_mxu):
        x2 = x * x
        if on_mxu:
            hi, lo = _split_bf16(x2)
            ss = (jnp.dot(hi, head_ones, preferred_element_type=F32)
                  + jnp.dot(lo, head_ones, preferred_element_type=F32))
        else:
            s0 = jnp.sum(jnp.where(head0, x2, 0.0), axis=-1, keepdims=True)
            s1 = jnp.sum(jnp.where(head0, 0.0, x2), axis=-1, keepdims=True)
            ss = jnp.where(head0, s0, s1)
        return x * lax.rsqrt(ss * (1.0 / dh) + RMS_EPS) * g

    @pl.when(i == 0)
    def _():
        def prep(j, c):
            r0 = pl.multiple_of(j * blk, blk)
            onehot = jnp.where(lane == dh + j, 1.0, 0.0)
            for p in range(nheads // 2):
                cols = slice(p * LANES, (p + 1) * LANES)
                kn = head_rms(k_ref[pl.ds(r0, blk), cols].astype(F32), gk_ref[...], True)
                for hh, kh in ((0, kn), (1, pltpu.roll(kn, dh, axis=1))):
                    h = 2 * p + hh
                    kmean_ref[h, pl.ds(j, 1), :] = jnp.mean(jnp.where(head0, kh, 0.0), axis=0, keepdims=True)
                    kn_ref[h, pl.ds(r0, blk), :] = jnp.where(head0, kh, onehot).astype(BF16)
                v_t = v_ref[pl.ds(r0, blk), cols].astype(F32).T.astype(BF16)
                for hh in range(2):
                    vt_ref[2 * p + hh, 0:dh, pl.ds(r0, blk)] = v_t[hh * dh:(hh + 1) * dh, :]
                    vt_ref[2 * p + hh, dh:, pl.ds(r0, blk)] = jnp.ones((vt_ref.shape[1] - dh, blk), BF16)
            return c

        lax.fori_loop(0, nb, prep, 0)

    jidx = lax.broadcasted_iota(jnp.int32, (nb, blk), 0)
    key_i = lax.broadcasted_iota(jnp.int32, (blk, blk), 0)
    qry_i = lax.broadcasted_iota(jnp.int32, (blk, blk), 1)
    causal = key_i <= qry_i
    r_own = pl.multiple_of(i * blk, blk)
    qk_scale = dh ** -0.5 * LOG2_E
    for p in range(nheads // 2):
        cols = slice(p * LANES, (p + 1) * LANES)
        qn_t = head_rms(q_ref[:, cols].astype(F32), gq_ref[...], False).T
        for hh in range(2):
            h = 2 * p + hh
            q_t = qn_t[hh * dh:(hh + 1) * dh, :]
            gate = jnp.dot(kmean_ref[h].astype(BF16),
                           jnp.concatenate([q_t, jnp.zeros((LANES - dh, blk), F32)], axis=0).astype(BF16),
                           preferred_element_type=F32)
            rank = jnp.zeros((nb, blk), F32)
            for jp in range(nb):
                row = gate[jp:jp + 1, :]
                beats = (row > gate) | ((row == gate) & (jidx > jp))
                rank = rank + jnp.where(beats, jnp.where(jp < i, 1.0, 0.0), 0.0)
            sel = (rank < topk) & (jidx < i)
            q_s = q_t * qk_scale
            pad = jnp.zeros((LANES - dh - nb, blk), F32)
            qaug_ref[h] = jnp.concatenate([q_s, jnp.where(sel, 0.0, MASK_BIAS), pad], axis=0).astype(BF16)
            qaug_own = jnp.concatenate([q_s, jnp.where(jidx == i, 0.0, MASK_BIAS), pad], axis=0).astype(BF16)
            st = jnp.dot(kn_ref[h, pl.ds(r_own, blk), :], qaug_own, preferred_element_type=F32)
            st = jnp.where(causal, st, -jnp.inf)
            s_ref[h, 0:blk, :] = st
            m_ref[h] = jnp.max(st, axis=0, keepdims=True)

    def finish_own(h):
        pr = jnp.exp2(s_ref[h, 0:blk, :] - m_ref[h]).astype(BF16)
        acc_ref[h] = jnp.dot(vt_ref[h, :, pl.ds(r_own, blk)], pr, preferred_element_type=F32)

    def score_pair(u, h):
        r0 = pl.multiple_of(u * pair, pair)
        st = jnp.dot(kn_ref[h, pl.ds(r0, pair), :], qaug_ref[h], preferred_element_type=F32)
        m_old = m_ref[h]
        m_new = jnp.maximum(m_old, jnp.max(st, axis=0, keepdims=True))
        s_ref[h] = st
        alpha_ref[h] = jnp.exp2(m_old - m_new)
        m_ref[h] = m_new

    def finish_pair(u, h):
        r0 = pl.multiple_of(u * pair, pair)
        pr = jnp.exp2(s_ref[h] - m_ref[h]).astype(BF16)
        acc_ref[h] = alpha_ref[h] * acc_ref[h] + jnp.dot(vt_ref[h, :, pl.ds(r0, pair)], pr,
                                                         preferred_element_type=F32)

    n_pairs = jnp.maximum((i + 1) // 2, 1)
    for h in range(nheads):
        finish_own(h)
        score_pair(0, h)

    def body(u, c):
        for h in range(nheads):
            finish_pair(u - 1, h)
            score_pair(u, h)
        return c

    lax.fori_loop(1, n_pairs, body, 0)
    for h in range(nheads):
        finish_pair(n_pairs - 1, h)

    for p in range(nheads // 2):
        a0 = acc_ref[2 * p]
        a1 = acc_ref[2 * p + 1]
        ot = jnp.concatenate([a0[0:dh] / a0[dh:dh + 1], a1[0:dh] / a1[dh:dh + 1]], axis=0)
        o_ref[:, p * LANES:(p + 1) * LANES] = ot.T.astype(o_ref.dtype)


def moba(proj, gq2, gk2, *, batch, seq):
    nb = seq // MOBA_BLOCK
    blk = MOBA_BLOCK
    dh = MOBA_HEAD_DIM
    nheads = MOBA_HEADS
    width = nheads * dh
    wb = width // LANES
    assert dh + nb <= LANES and 2 * dh == LANES and nb % 2 == 0
    v_rows = dh + 2 * SUBLANES
    body = functools.partial(_moba_body, nb=nb, blk=blk, dh=dh, topk=MOBA_TOPK, nheads=nheads)
    return pl.pallas_call(
        body,
        grid=(batch, nb),
        in_specs=[
            pl.BlockSpec((blk, width), lambda b, i: (b * nb + i, COL_QA // wb)),
            pl.BlockSpec((seq, width), lambda b, i: (b, COL_KA // wb)),
            pl.BlockSpec((seq, width), lambda b, i: (b, COL_VA // wb)),
            pl.BlockSpec((1, LANES), lambda b, i: (0, 0)),
            pl.BlockSpec((1, LANES), lambda b, i: (0, 0)),
        ],
        out_specs=pl.BlockSpec((blk, width), lambda b, i: (b * nb + i, 0)),
        out_shape=jax.ShapeDtypeStruct((batch * seq, width), BF16),
        scratch_shapes=[
            pltpu.VMEM((nheads, seq, LANES), BF16),
            pltpu.VMEM((nheads, v_rows, seq), BF16),
            pltpu.VMEM((nheads, nb, LANES), F32),
            pltpu.VMEM((nheads, LANES, blk), BF16),
            pltpu.VMEM((nheads, 2 * blk, blk), F32),
            pltpu.VMEM((nheads, 1, blk), F32),
            pltpu.VMEM((nheads, 1, blk), F32),
            pltpu.VMEM((nheads, v_rows, blk), F32),
        ],
        compiler_params=_params("arbitrary", "arbitrary"),
        name="moba",
    )(proj, proj, proj, gq2, gk2)


def _log_sigmoid(x):
    return jnp.minimum(x, 0.0) - jnp.log(1.0 + jnp.exp(-jnp.abs(x)))


def _dot_tri(tri, x, tri_left):
    out = None
    for _ in range(3):
        piece = x.astype(BF16)
        x = x - piece.astype(F32)
        term = (jnp.dot(tri, piece, preferred_element_type=F32) if tri_left
                else jnp.dot(piece, tri, preferred_element_type=F32))
        out = term if out is None else out + term
    return out


def _mlstm_body(qr_ref, kr_ref, v_ref, og_ref, gcol_ref, grow_ref, brow_ref, bcol_ref,
                cwq_ref, cwk_ref, cbq_ref, cbk_ref, gh_ref, o_ref,
                qx_ref, kx_ref, c_ref, m_ref, *, chunk, dk, nh):
    L = chunk
    width = nh * dk

    @pl.when(pl.program_id(1) == 0)
    def _():
        qx_ref[0:SUBLANES, :] = jnp.zeros((SUBLANES, width), F32)
        kx_ref[0:SUBLANES, :] = jnp.zeros((SUBLANES, width), F32)
        c_ref[...] = jnp.zeros_like(c_ref)
        m_ref[...] = jnp.zeros_like(m_ref)

    qx_ref[SUBLANES:SUBLANES + L, :] = qr_ref[...].astype(F32)
    kx_ref[SUBLANES:SUBLANES + L, :] = kr_ref[...].astype(F32)

    def conv_silu(x_ref, w_ref, b_ref):
        acc = b_ref[...] + w_ref[0:1, :] * x_ref[pl.ds(SUBLANES - CONV_WIDTH + 1, L), :]
        for j in range(1, CONV_WIDTH):
            acc = acc + w_ref[j:j + 1, :] * x_ref[pl.ds(SUBLANES - CONV_WIDTH + 1 + j, L), :]
        return acc * _sigmoid(acc)

    q_all = conv_silu(qx_ref, cwq_ref, cbq_ref)
    k_all = conv_silu(kx_ref, cwk_ref, cbk_ref) * (dk ** -0.5)
    qx_ref[0:SUBLANES, :] = qx_ref[L:L + SUBLANES, :]
    kx_ref[0:SUBLANES, :] = kx_ref[L:L + SUBLANES, :]

    pre_col = gcol_ref[...] + brow_ref[...]
    pre_row = grow_ref[0] + bcol_ref[...]
    t_i = lax.broadcasted_iota(jnp.int32, (L, L), 0)
    s_i = lax.broadcasted_iota(jnp.int32, (L, L), 1)
    tril = s_i <= t_i
    bcum_cols = _dot_tri(jnp.where(tril, 1.0, 0.0).astype(BF16), _log_sigmoid(pre_col), True)
    bcum_rows = _dot_tri(jnp.where(t_i <= s_i, 1.0, 0.0).astype(BF16), _log_sigmoid(pre_row), False)
    ones = jnp.ones((L, dk), BF16)

    for h in range(nh):
        cols = slice(h * dk, (h + 1) * dk)
        q = q_all[:, cols]
        k = k_all[:, cols]
        i_col = pre_col[:, h:h + 1]
        i_row = pre_row[h:h + 1, :]
        bcum_col = bcum_cols[:, nh + h:nh + h + 1]
        bcum_row = bcum_rows[nh + h:nh + h + 1, :]

        m_prev = m_ref[h, 0:1, 0:1]
        a_col = bcum_col + m_prev
        dmat = jnp.where(tril, bcum_col - bcum_row + i_row, -jnp.inf)
        m_t = jnp.maximum(a_col, jnp.max(dmat, axis=-1, keepdims=True))
        dw = jnp.exp(dmat - m_t)
        aw = jnp.exp(a_col - m_t)

        qb = q.astype(BF16)
        kb = k.astype(BF16)
        v_aug = jnp.concatenate([v_ref[:, cols], ones], axis=-1)
        sqk = _nt_dot(qb, kb) * dw
        num_aug = (aw * jnp.dot(qb, c_ref[h].astype(BF16), preferred_element_type=F32)
                   + jnp.dot(sqk.astype(BF16), v_aug, preferred_element_type=F32))
        den = num_aug[:, dk:dk + 1]
        hc = num_aug[:, 0:dk] / jnp.maximum(jnp.abs(den), jnp.exp(-m_t))

        b_last = bcum_col[L - 1:L, :]
        g_col = b_last - bcum_col + i_col
        m_new = jnp.maximum(b_last + m_prev, jnp.max(g_col, axis=0, keepdims=True))
        w_c = jnp.exp(b_last + m_prev - m_new)
        kw_t = (k * jnp.exp(g_col - m_new)).T.astype(BF16)
        c_ref[h] = w_c * c_ref[h] + jnp.dot(kw_t, v_aug, preferred_element_type=F32)
        m_ref[h] = jnp.broadcast_to(m_new, (1, LANES))

        o_ref[:, cols] = (_rms(hc, gh_ref[...]) * _sigmoid(og_ref[:, cols].astype(F32))).astype(o_ref.dtype)


def mlstm(proj, gates_col, gates_row, bias_row, bias_col, conv_w, conv_b, gh, *, batch, seq, chunk):
    nh = MLSTM_HEADS
    dk = MLSTM_DIM
    width = nh * dk
    wb = width // LANES
    nc = seq // chunk
    body = functools.partial(_mlstm_body, chunk=chunk, dk=dk, nh=nh)

    def rows(col0):
        return pl.BlockSpec((chunk, width), lambda b, c: (b * nc + c, col0 // wb))

    return pl.pallas_call(
        body,
        grid=(batch, nc),
        in_specs=[
            rows(COL_QM), rows(COL_KM), rows(COL_VM), rows(COL_OM),
            pl.BlockSpec((chunk, LANES), lambda b, c: (b * nc + c, 0)),
            pl.BlockSpec((1, SUBLANES, chunk), lambda b, c: (b, 0, c)),
            pl.BlockSpec((1, LANES), lambda b, c: (0, 0)),
            pl.BlockSpec((SUBLANES, 1), lambda b, c: (0, 0)),
            pl.BlockSpec((CONV_WIDTH, width), lambda b, c: (0, 0)),
            pl.BlockSpec((CONV_WIDTH, width), lambda b, c: (0, 1)),
            pl.BlockSpec((1, width), lambda b, c: (0, 0)),
            pl.BlockSpec((1, width), lambda b, c: (0, 1)),
            pl.BlockSpec((1, LANES), lambda b, c: (0, 0)),
        ],
        out_specs=pl.BlockSpec((chunk, width), lambda b, c: (b * nc + c, 0)),
        out_shape=jax.ShapeDtypeStruct((batch * seq, width), BF16),
        scratch_shapes=[
            pltpu.VMEM((chunk + 2 * SUBLANES, width), F32),
            pltpu.VMEM((chunk + 2 * SUBLANES, width), F32),
            pltpu.VMEM((nh, dk, 2 * dk), F32),
            pltpu.VMEM((nh, 1, LANES), F32),
        ],
        compiler_params=_params("arbitrary", "arbitrary"),
        name="mlstm",
    )(proj, proj, proj, proj, gates_col, gates_row, bias_row, bias_col,
      conv_w, conv_w, conv_b, conv_b, gh)


def _merge_body(ya_ref, yb_ref, ga_ref, gb_ref, x_ref, woa_ref, wob_ref, wout_ref, gffn_ref, *rest, moe):
    a = jnp.dot(ya_ref[...], woa_ref[...], preferred_element_type=F32)
    b = jnp.dot(yb_ref[...], wob_ref[...], preferred_element_type=F32)
    mixed = _sigmoid(ga_ref[...].astype(F32)) * a + _sigmoid(gb_ref[...].astype(F32)) * b
    x1 = x_ref[...] + jnp.dot(mixed.astype(BF16), wout_ref[...], preferred_element_type=F32)
    hf = _rms(x1, gffn_ref[...])
    if moe:
        wr_ref, x1_ref, hf_ref, lg_ref = rest
        _store_row_tiles(hf_ref, hf)
        lg_ref[...] = jnp.dot(hf.astype(BF16), wr_ref[...].astype(BF16), preferred_element_type=F32)
    else:
        x1_ref, hf_ref = rest
        hf_ref[...] = hf.astype(BF16)
    x1_ref[...] = x1


def merge(ya, yb, proj, x2, woa, wob, wout, gffn, wr, *, tm):
    t, d = x2.shape
    moe = wr is not None
    full = lambda m: (0, 0)
    in_specs = [
        pl.BlockSpec((tm, ya.shape[1]), lambda m: (m, 0)),
        pl.BlockSpec((tm, yb.shape[1]), lambda m: (m, 0)),
        pl.BlockSpec((tm, d), lambda m: (m, COL_GA * LANES // d)),
        pl.BlockSpec((tm, d), lambda m: (m, COL_GB * LANES // d)),
        pl.BlockSpec((tm, d), lambda m: (m, 0)),
        pl.BlockSpec(woa.shape, full), pl.BlockSpec(wob.shape, full), pl.BlockSpec(wout.shape, full),
        pl.BlockSpec((1, d), full),
    ]
    args = [ya, yb, proj, proj, x2, woa, wob, wout, gffn]
    out_specs = [pl.BlockSpec((tm, d), lambda m: (m, 0)), pl.BlockSpec((tm, d), lambda m: (m, 0))]
    out_shape = [jax.ShapeDtypeStruct((t, d), F32), jax.ShapeDtypeStruct((t, d), BF16)]
    if moe:
        out_specs[1] = pl.BlockSpec((tm * SUBLANES, LANES), lambda m: (m, 0))
        out_shape[1] = jax.ShapeDtypeStruct((t * SUBLANES, LANES), F32)
    if moe:
        in_specs.append(pl.BlockSpec(wr.shape, full))
        args.append(wr)
        out_specs.append(pl.BlockSpec((tm, LANES), lambda m: (m, 0)))
        out_shape.append(jax.ShapeDtypeStruct((t, LANES), F32))
    return pl.pallas_call(
        functools.partial(_merge_body, moe=moe),
        grid=(t // tm,),
        in_specs=in_specs, out_specs=out_specs, out_shape=out_shape,
        compiler_params=_params("arbitrary"),
        name="merge_moe" if moe else "merge",
    )(*args)


def _swiglu_gate(x, w1_ref, w3_ref, g_ref, fc):
    dff = g_ref.shape[1]
    for f0 in range(0, dff, fc):
        a = jnp.dot(x, w1_ref[:, f0:f0 + fc].astype(BF16), preferred_element_type=F32)
        b = jnp.dot(x, w3_ref[:, f0:f0 + fc].astype(BF16), preferred_element_type=F32)
        g_ref[:, f0:f0 + fc] = (a * _sigmoid(a) * b).astype(BF16)


def _swiglu(x, w1_ref, w3_ref, w2_ref, g_ref, fc):
    _swiglu_gate(x, w1_ref, w3_ref, g_ref, fc)
    return jnp.dot(g_ref[...], w2_ref[...].astype(BF16), preferred_element_type=F32)


def _ple(x, p_ref, g_ref, wg_ref, wp_ref):
    gate = _sigmoid(jnp.dot(_rms(x, g_ref[...]).astype(BF16), wg_ref[...], preferred_element_type=F32))
    emb = jnp.dot(p_ref[...].astype(BF16), wp_ref[...], preferred_element_type=F32)
    return x + gate * emb


def _ffn_body(hf_ref, x1_ref, w1_ref, w3_ref, w2_ref, p_ref, g_ref, wg_ref, wp_ref, o_ref, act_ref, *, fc):
    x2 = x1_ref[...] + _swiglu(hf_ref[...], w1_ref, w3_ref, w2_ref, act_ref, fc)
    o_ref[...] = _ple(x2, p_ref, g_ref, wg_ref, wp_ref)


def dense_ffn(hf, x1, w1, w3, w2, p2, p_row0, g, wg, wp, *, tm, fc):
    t, d = x1.shape
    dff = w1.shape[1]
    p_blk0 = p_row0 // tm
    resident = dict(pipeline_mode=pl.Buffered(1))
    full = lambda m: (0, 0)
    return pl.pallas_call(
        functools.partial(_ffn_body, fc=fc),
        grid=(t // tm,),
        in_specs=[
            pl.BlockSpec((tm, d), lambda m: (m, 0)),
            pl.BlockSpec((tm, d), lambda m: (m, 0)),
            pl.BlockSpec((d, dff), full, **resident),
            pl.BlockSpec((d, dff), full, **resident),
            pl.BlockSpec((dff, d), full, **resident),
            pl.BlockSpec((tm, p2.shape[1]), lambda m: (p_blk0 + m, 0)),
            pl.BlockSpec((1, d), full),
            pl.BlockSpec(wg.shape, full, **resident),
            pl.BlockSpec(wp.shape, full, **resident),
        ],
        out_specs=pl.BlockSpec((tm, d), lambda m: (m, 0)),
        out_shape=jax.ShapeDtypeStruct((t, d), F32),
        scratch_shapes=[pltpu.VMEM((tm, dff), BF16)],
        compiler_params=_params("arbitrary"),
        name="dense_ffn",
    )(hf, x1, w1, w3, w2, p2, g, wg, wp)


META_E0, META_E1, META_G0, META_G1, META_R0, META_R1 = 0, 1, 2, 3, 4, 5


def _route_body(lg_ref, meta_ref, tab_ref, cnt_ref, carry_ref, *, tm, ne):
    @pl.when(pl.program_id(0) == 0)
    def _():
        carry_ref[...] = jnp.zeros_like(carry_ref)

    lane = lax.broadcasted_iota(jnp.int32, (tm, LANES), 1)
    lanef = lane.astype(F32)
    lg = jnp.where(lane < ne, lg_ref[...], -jnp.inf)
    m1 = jnp.max(lg, axis=-1, keepdims=True)
    e1 = jnp.min(jnp.where(lg == m1, lanef, float(LANES)), axis=-1, keepdims=True)
    lg2 = jnp.where(lanef == e1, -jnp.inf, lg)
    m2 = jnp.max(lg2, axis=-1, keepdims=True)
    e2 = jnp.min(jnp.where(lg2 == m2, lanef, float(LANES)), axis=-1, keepdims=True)
    ex = jnp.exp(m2 - m1)
    g1 = 1.0 / (1.0 + ex)
    g2 = ex / (1.0 + ex)
    onehot = jnp.where((lanef == e1) | (lanef == e2), 1.0, 0.0)
    row = lax.broadcasted_iota(jnp.int32, (tm, tm), 0)
    col = lax.broadcasted_iota(jnp.int32, (tm, tm), 1)
    before = jnp.dot(jnp.where(col < row, 1.0, 0.0).astype(BF16), onehot.astype(BF16),
                     preferred_element_type=F32) + carry_ref[...]
    r1 = jnp.sum(jnp.where(lanef == e1, before, 0.0), axis=-1, keepdims=True)
    r2 = jnp.sum(jnp.where(lanef == e2, before, 0.0), axis=-1, keepdims=True)
    carry_ref[...] += jnp.sum(onehot, axis=0, keepdims=True)
    meta = jnp.zeros((tm, LANES), F32)
    for pos, val in ((META_E0, e1), (META_E1, e2), (META_G0, g1), (META_G1, g2), (META_R0, r1), (META_R1, r2)):
        meta = jnp.where(lane == pos, val, meta)
    meta_ref[...] = meta
    tab_ref[...] = meta.T[0:SUBLANES, :]
    cnt_ref[...] = jnp.broadcast_to(carry_ref[...], cnt_ref.shape)


def route(logits, *, tm):
    t = logits.shape[0]
    return pl.pallas_call(
        functools.partial(_route_body, tm=tm, ne=N_EXPERTS),
        grid=(t // tm,),
        in_specs=[pl.BlockSpec((tm, LANES), lambda m: (m, 0))],
        out_specs=[pl.BlockSpec((tm, LANES), lambda m: (m, 0)),
                   pl.BlockSpec((SUBLANES, tm), lambda m: (0, m)),
                   pl.BlockSpec((SUBLANES, LANES), lambda m: (0, 0))],
        out_shape=[jax.ShapeDtypeStruct((t, LANES), F32), jax.ShapeDtypeStruct((SUBLANES, t), F32),
                   jax.ShapeDtypeStruct((SUBLANES, LANES), F32)],
        scratch_shapes=[pltpu.VMEM((1, LANES), F32)],
        compiler_params=_params("arbitrary"),
        name="route",
    )(logits)


def _dispatch_body(dest_ref, pe_ref, na_ref, hf_ref, xs_ref, zero_ref, sem, zsem, *, tm, topk, tm_rows, n_tok):
    base = pl.program_id(0) * tm

    @pl.when(pl.program_id(0) == 0)
    def _():
        zero_ref[...] = jnp.zeros_like(zero_ref)
        n_tiles = xs_ref.shape[0] // (tm_rows * SUBLANES)
        fills = []
        for e in range(N_EXPERTS):
            end = pe_ref[e]
            nonempty = end > (pe_ref[e - 1] if e else 0)
            fills.append((nonempty, pl.multiple_of(jnp.maximum(end - tm_rows, 0), tm_rows)))
        for tile in range(n_tiles):
            fills.append((tile >= na_ref[0], tile * tm_rows))
        for phase in ("start", "wait"):
            for cond, row0 in fills:
                @pl.when(cond)
                def _(row0=row0, phase=phase):
                    cp = pltpu.make_async_copy(
                        zero_ref, xs_ref.at[pl.ds(row0 * SUBLANES, tm_rows * SUBLANES)], zsem)
                    cp.start() if phase == "start" else cp.wait()

    def issue(r, c):
        for k in range(topk):
            d = dest_ref[k * n_tok + base + r]
            pltpu.make_async_copy(_row_tile(hf_ref, r), _row_tile(xs_ref, d), sem).start(priority=k % 2)
        return c

    lax.fori_loop(0, tm, issue, 0, unroll=ROW_DMA_UNROLL)
    for k in range(topk):
        pltpu.make_async_copy(hf_ref, xs_ref.at[pl.ds(0, tm * SUBLANES)], sem).wait()


def _row_tile(ref, r):
    return ref.at[pl.ds(pl.multiple_of(r * SUBLANES, SUBLANES), SUBLANES)]


def dispatch(dest, pad_end, n_active, hf, *, n_rows, tm, topk, tm_rows):
    t = hf.shape[0] // SUBLANES
    return pl.pallas_call(
        functools.partial(_dispatch_body, tm=tm, topk=topk, tm_rows=tm_rows, n_tok=t),
        grid_spec=pltpu.PrefetchScalarGridSpec(
            num_scalar_prefetch=3,
            grid=(t // tm,),
            in_specs=[pl.BlockSpec((tm * SUBLANES, LANES), lambda m, *_: (m, 0))],
            out_specs=pl.BlockSpec(memory_space=pl.ANY),
            scratch_shapes=[pltpu.VMEM((tm_rows * SUBLANES, LANES), F32),
                            pltpu.SemaphoreType.DMA, pltpu.SemaphoreType.DMA],
        ),
        out_shape=jax.ShapeDtypeStruct((n_rows * SUBLANES, LANES), F32),
        compiler_params=_params("arbitrary"),
        name="moe_dispatch",
    )(dest, pad_end, n_active, hf)


def _experts_body(te_ref, na_ref, xs_ref, w1_ref, w3_ref, w2_ref, y_ref, xb_ref, g_ref, acc_ref, *, fc):
    del te_ref
    i = pl.program_id(0)
    f = pl.program_id(1)
    last = pl.num_programs(1) - 1
    tm = xb_ref.shape[0]

    @pl.when(i >= na_ref[0])
    def _():
        y_ref[...] = jnp.zeros_like(y_ref)

    @pl.when(i < na_ref[0])
    def _():
        @pl.when(f == 0)
        def _():
            xb_ref[...] = _load_row_tiles(xs_ref, tm).astype(BF16)

        y = _swiglu(xb_ref[...], w1_ref.at[0], w3_ref.at[0], w2_ref.at[0], g_ref, fc)

        @pl.when(f == 0)
        def _():
            acc_ref[...] = y

        @pl.when((f > 0) & (f < last))
        def _():
            acc_ref[...] += y

        @pl.when(f == last)
        def _():
            _store_row_tiles(y_ref, acc_ref[...] + y)


def experts(tile_expert, n_active, xs, w1, w3, w2, *, tm, tf, fc):
    n_rows = xs.shape[0] // SUBLANES
    d = w1.shape[1]
    dff = w1.shape[2]
    assert dff // tf >= 2
    row_tile = lambda i, f, te, na: (jnp.minimum(i, na[0] - 1), 0)
    ftile = lambda i, f, na: jnp.where(i < na[0], f, dff // tf - 1)
    return pl.pallas_call(
        functools.partial(_experts_body, fc=fc),
        grid_spec=pltpu.PrefetchScalarGridSpec(
            num_scalar_prefetch=2,
            grid=(n_rows // tm, dff // tf),
            in_specs=[
                pl.BlockSpec((tm * SUBLANES, LANES), row_tile),
                pl.BlockSpec((1, d, tf), lambda i, f, te, na: (te[i], 0, ftile(i, f, na))),
                pl.BlockSpec((1, d, tf), lambda i, f, te, na: (te[i], 0, ftile(i, f, na))),
                pl.BlockSpec((1, tf, d), lambda i, f, te, na: (te[i], ftile(i, f, na), 0)),
            ],
            out_specs=pl.BlockSpec((tm * SUBLANES, LANES), lambda i, f, te, na: (i, 0)),
            scratch_shapes=[pltpu.VMEM((tm, d), BF16), pltpu.VMEM((tm, tf), BF16), pltpu.VMEM((tm, d), F32)],
        ),
        out_shape=jax.ShapeDtypeStruct((n_rows * SUBLANES, LANES), F32),
        compiler_params=_params("arbitrary", "arbitrary"),
        name="moe_experts",
    )(tile_expert, n_active, xs, w1, w3, w2)


def _combine_body(dest_ref, x1_ref, meta_ref, p_ref, g_ref, wg_ref, wp_ref, y_ref, o_ref, buf_ref, sem,
                  *, tm, topk):
    m = pl.program_id(0)
    n_steps = pl.num_programs(0)
    n_tok = n_steps * tm

    def start_gather(tile, slot):
        def issue(r, c):
            for k in range(topk):
                d = dest_ref[k * n_tok + tile * tm + r]
                pltpu.make_async_copy(_row_tile(y_ref, d), _row_tile(buf_ref.at[slot, k], r),
                                      sem.at[slot]).start(priority=k % 2)
            return c

        lax.fori_loop(0, tm, issue, 0, unroll=ROW_DMA_UNROLL)

    @pl.when(m == 0)
    def _():
        start_gather(0, 0)

    @pl.when(m + 1 < n_steps)
    def _():
        start_gather(m + 1, (m + 1) % 2)

    slot = m % 2
    for k in range(topk):
        pltpu.make_async_copy(y_ref.at[pl.ds(0, tm * SUBLANES)], buf_ref.at[slot, k], sem.at[slot]).wait()
    meta = meta_ref[...]
    g0 = meta[:, META_G0:META_G0 + 1]
    g1 = meta[:, META_G1:META_G1 + 1]
    x2 = x1_ref[...] + (g0 * _load_row_tiles(buf_ref.at[slot, 0], tm)
                        + g1 * _load_row_tiles(buf_ref.at[slot, 1], tm))
    o_ref[...] = _ple(x2, p_ref, g_ref, wg_ref, wp_ref)


def combine(dest, x1, meta, y, p2, p_row0, g, wg, wp, *, tm, topk):
    t, d = x1.shape
    full = lambda m, dest: (0, 0)
    p_blk0 = p_row0 // tm
    return pl.pallas_call(
        functools.partial(_combine_body, tm=tm, topk=topk),
        grid_spec=pltpu.PrefetchScalarGridSpec(
            num_scalar_prefetch=1,
            grid=(t // tm,),
            in_specs=[pl.BlockSpec((tm, d), lambda m, dest: (m, 0)),
                      pl.BlockSpec((tm, LANES), lambda m, dest: (m, 0)),
                      pl.BlockSpec((tm, p2.shape[1]), lambda m, dest: (p_blk0 + m, 0)),
                      pl.BlockSpec((1, d), full), pl.BlockSpec(wg.shape, full), pl.BlockSpec(wp.shape, full),
                      pl.BlockSpec(memory_space=pl.ANY)],
            out_specs=pl.BlockSpec((tm, d), lambda m, dest: (m, 0)),
            scratch_shapes=[pltpu.VMEM((2, topk, tm * SUBLANES, LANES), F32), pltpu.SemaphoreType.DMA((2,))],
        ),
        out_shape=jax.ShapeDtypeStruct((t, d), F32),
        compiler_params=_params("arbitrary"),
        name="moe_combine",
    )(dest, x1, meta, p2, g, wg, wp, y)


def moe_ffn(hf, x1, logits, w1, w3, w2, ple_args, *, tm_route, tm_rows, tf, tm_move):
    t, d = x1.shape
    topk = 2
    meta, tab, cnt = route(logits, tm=tm_route)
    counts = cnt[0, :N_EXPERTS].astype(jnp.int32)
    padded = ((counts + tm_rows - 1) // tm_rows) * tm_rows
    pad_end = jnp.cumsum(padded).astype(jnp.int32)
    pad_start = pad_end - padded
    eidx = tab[META_E0:META_E1 + 1].astype(jnp.int32)
    rank = tab[META_R0:META_R1 + 1].astype(jnp.int32)
    dest = rank
    for e in range(N_EXPERTS):
        dest = dest + jnp.where(eidx == e, pad_start[e], 0)
    dest = dest.reshape(topk * t)
    n_tiles = -(-(t * topk) // tm_rows) + N_EXPERTS
    tile_start = jnp.arange(n_tiles, dtype=jnp.int32) * tm_rows
    tile_expert = jnp.minimum(jnp.sum(tile_start[:, None] >= pad_end[None, :], axis=1),
                              N_EXPERTS - 1).astype(jnp.int32)
    n_active = pad_end[N_EXPERTS - 1:] // tm_rows
    xs = dispatch(dest, pad_end, n_active, hf, n_rows=n_tiles * tm_rows, tm=tm_move, topk=topk, tm_rows=tm_rows)
    y = experts(tile_expert, n_active, xs, w1, w3, w2, tm=tm_rows, tf=tf, fc=256)
    return combine(dest, x1, meta, y, *ple_args, tm=tm_move, topk=topk)


def _tile2(g):
    return jnp.concatenate([g, g]).reshape(1, 2 * g.shape[0])


def kernel(x, p, g_mix, w_in, g_q, g_k, conv_w, conv_b, b_i, b_f, g_h, w_oa, w_ob, w_out, g_ffn, w_d1, w_d3,
           w_d2, w_router, w_e1, w_e3, w_e2, g_ple, w_ple_gate, w_ple_proj):
    batch, seq, d = x.shape
    depth = w_in.shape[0]
    t = batch * seq
    nh = MLSTM_HEADS
    x2 = x.reshape(t, d)
    c_q, c_k, c_v = 0, 512, 1024
    c_qk, c_vm, c_om, c_i, c_f, c_ga, c_gb, c_end = 1536, 2560, 3072, 3584, 3588, 3592, 4616, 5640

    w_in_t = jnp.swapaxes(w_in, 1, 2).astype(BF16)
    for l in range(depth):
        w_gates = w_in_t[l, c_ga:c_end]
        proj, gif = in_proj(x2, g_mix[l].reshape(1, d), w_gates, w_in_t, l, nb_cols=c_i, if_col=c_i,
                            tm=1024, tn=512)

        ya = moba(proj, _tile2(g_q[l]), _tile2(g_k[l]), batch=batch, seq=seq)

        bias = jnp.concatenate([b_i[l], b_f[l]])
        bias_row = jnp.pad(bias, (0, LANES - 2 * nh)).reshape(1, LANES)
        bias_col = bias.reshape(2 * nh, 1)
        gates_row = gif[:, :2 * nh].reshape(batch, seq, 2 * nh).transpose(0, 2, 1)
        yb = mlstm(proj, gif, gates_row, bias_row, bias_col, conv_w[l], conv_b[l].reshape(1, -1),
                   g_h[l].reshape(1, -1), batch=batch, seq=seq, chunk=256)

        j = l // 2
        moe = l % 2 == 1
        wr = jnp.pad(w_router[j], ((0, 0), (0, LANES - N_EXPERTS))) if moe else None
        outs = merge(ya, yb, proj, x2, w_oa[l].astype(BF16), w_ob[l].astype(BF16), w_out[l].astype(BF16),
                     g_ffn[l].reshape(1, d), wr, tm=1024)
        ple_args = (p.reshape(depth * t, -1), l * t, g_ple[l].reshape(1, d), w_ple_gate[l].astype(BF16),
                    w_ple_proj[l].astype(BF16))
        if moe:
            x1, hf, logits = outs
            x2 = moe_ffn(hf, x1, logits, w_e1[j], w_e3[j].astype(BF16), w_e2[j],
                         ple_args, tm_route=512, tm_rows=512, tf=1792, tm_move=512)
        else:
            x1, hf = outs
            x2 = dense_ffn(hf, x1, w_d1[j].astype(BF16), w_d3[j].astype(BF16), w_d2[j].astype(BF16),
                           *ple_args, tm=1024, fc=256)
    return x2.reshape(batch, seq, d)
```

```python
import functools

import jax
import jax.numpy as jnp
from jax import lax
from jax.experimental import pallas as pl
from jax.experimental.pallas import tpu as pltpu

F32 = jnp.float32
BF16 = jnp.bfloat16

RMS_EPS = 1e-6
LANES = 128
SUBLANES = 8

MOBA_HEADS = 8
MOBA_HEAD_DIM = 64
MOBA_BLOCK = 256
MOBA_TOPK = 3
MLSTM_HEADS = 4
MLSTM_DIM = 128
CONV_WIDTH = 4
N_EXPERTS = 8

COL_GA, COL_GB = 0, 8
COL_QA, COL_KA, COL_VA = 16, 20, 24
COL_QM, COL_KM, COL_VM, COL_OM = 28, 32, 36, 40
N_PROJ = 44 * LANES

VMEM_LIMIT = 56 * 1024 * 1024
ROW_DMA_UNROLL = 16


def _params(*sem):
    return pltpu.CompilerParams(dimension_semantics=sem, vmem_limit_bytes=VMEM_LIMIT)


def _sigmoid(x):
    return 1.0 / (1.0 + jnp.exp(-x))


def _rms(x, g):
    return x * lax.rsqrt(jnp.mean(x * x, axis=-1, keepdims=True) + RMS_EPS) * g


def _split_bf16(x):
    hi = x.astype(BF16)
    return hi, (x - hi.astype(F32)).astype(BF16)


def _store_row_tiles(ref, x):
    ref[...] = x.reshape(x.shape[0] * SUBLANES, LANES)


def _load_row_tiles(ref, rows):
    return ref[...].reshape(rows, SUBLANES * LANES)


def _nt_dot(a, b, **kw):
    return lax.dot_general(a, b, (((1,), (1,)), ((), ())), preferred_element_type=F32, **kw)


def _in_proj_body(x_ref, g_ref, wa_ref, wb_ref, o_ref, oif_ref, h_ref, *, nb_cols, if_col, tn):
    h_ref[...] = _rms(x_ref[...], g_ref[...]).astype(BF16)
    oif_ref[...] = _nt_dot(h_ref[...], wb_ref[if_col:if_col + LANES, :])
    na = wa_ref.shape[0]
    for c0 in range(0, na + nb_cols, tn):
        w = wa_ref[c0:c0 + tn, :] if c0 < na else wb_ref[c0 - na:c0 - na + tn, :]
        o_ref[:, c0:c0 + tn] = _nt_dot(h_ref[...], w).astype(o_ref.dtype)


def in_proj(x2, g, wa, w_full, layer, *, nb_cols, if_col, tm, tn):
    t, d = x2.shape
    n = wa.shape[0] + nb_cols
    resident = dict(pipeline_mode=pl.Buffered(1))
    return pl.pallas_call(
        functools.partial(_in_proj_body, nb_cols=nb_cols, if_col=if_col, tn=tn),
        grid=(t // tm,),
        in_specs=[
            pl.BlockSpec((tm, d), lambda m: (m, 0)),
            pl.BlockSpec((1, d), lambda m: (0, 0)),
            pl.BlockSpec(wa.shape, lambda m: (0, 0), **resident),
            pl.BlockSpec((None,) + w_full.shape[1:], lambda m: (layer, 0, 0), **resident),
        ],
        out_specs=[
            pl.BlockSpec((tm, n), lambda m: (m, 0)),
            pl.BlockSpec((tm, LANES), lambda m: (m, 0)),
        ],
        out_shape=[jax.ShapeDtypeStruct((t, n), BF16), jax.ShapeDtypeStruct((t, LANES), F32)],
        scratch_shapes=[pltpu.VMEM((tm, d), BF16)],
        compiler_params=_params("arbitrary"),
        name="in_proj",
    )(x2, g, wa, w_full)


MASK_BIAS = -1e30
LOG2_E = 1.4426950408889634


def _moba_body(q_ref, k_ref, v_ref, gq_ref, gk_ref, o_ref,
               kn_ref, vt_ref, kmean_ref, qaug_ref, s_ref, m_ref, alpha_ref, acc_ref,
               *, nb, blk, dh, topk, nheads):
    i = pl.program_id(1)
    pair = 2 * blk
    lane = lax.broadcasted_iota(jnp.int32, (1, LANES), 1)
    head0 = lane < dh

    same_head = (lax.broadcasted_iota(jnp.int32, (LANES, LANES), 0) // dh
                 == lax.broadcasted_iota(jnp.int32, (LANES, LANES), 1) // dh)
    head_ones = jnp.where(same_head, 1.0, 0.0).astype(BF16)

    def head_rms(x, g, on_mxu):
        x2 = x * x
        if on_mxu:
            hi, lo = _split_bf16(x2)
            ss = (jnp.dot(hi, head_ones, preferred_element_type=F32)
                  + jnp.dot(lo, head_ones, preferred_element_type=F32))
        else:
            s0 = jnp.sum(jnp.where(head0, x2, 0.0), axis=-1, keepdims=True)
            s1 = jnp.sum(jnp.where(head0, 0.0, x2), axis=-1, keepdims=True)
            ss = jnp.where(head0, s0, s1)
        return x * lax.rsqrt(ss * (1.0 / dh) + RMS_EPS) * g

    @pl.when(i == 0)
    def _():
        def prep(j, c):
            r0 = pl.multiple_of(j * blk, blk)
            onehot = jnp.where(lane == dh + j, 1.0, 0.0)
            for p in range(nheads // 2):
                cols = slice(p * LANES, (p + 1) * LANES)
                kn = head_rms(k_ref[pl.ds(r0, blk), cols].astype(F32), gk_ref[...], True)
                for hh, kh in ((0, kn), (1, pltpu.roll(kn, dh, axis=1))):
                    h = 2 * p + hh
                    kmean_ref[h, pl.ds(j, 1), :] = jnp.mean(jnp.where(head0, kh, 0.0), axis=0, keepdims=True)
                    kn_ref[h, pl.ds(r0, blk), :] = jnp.where(head0, kh, onehot).astype(BF16)
                v_t = v_ref[pl.ds(r0, blk), cols].astype(F32).T.astype(BF16)
                for hh in range(2):
                    vt_ref[2 * p + hh, 0:dh, pl.ds(r0, blk)] = v_t[hh * dh:(hh + 1) * dh, :]
                    vt_ref[2 * p + hh, dh:, pl.ds(r0, blk)] = jnp.ones((vt_ref.shape[1] - dh, blk), BF16)
            return c

        lax.fori_loop(0, nb, prep, 0)

    jidx = lax.broadcasted_iota(jnp.int32, (nb, blk), 0)
    key_i = lax.broadcasted_iota(jnp.int32, (blk, blk), 0)
    qry_i = lax.broadcasted_iota(jnp.int32, (blk, blk), 1)
    causal = key_i <= qry_i
    r_own = pl.multiple_of(i * blk, blk)
    qk_scale = dh ** -0.5 * LOG2_E
    for p in range(nheads // 2):
        cols = slice(p * LANES, (p + 1) * LANES)
        qn_t = head_rms(q_ref[:, cols].astype(F32), gq_ref[...], False).T
        for hh in range(2):
            h = 2 * p + hh
            q_t = qn_t[hh * dh:(hh + 1) * dh, :]
            gate = jnp.dot(kmean_ref[h].astype(BF16),
                           jnp.concatenate([q_t, jnp.zeros((LANES - dh, blk), F32)], axis=0).astype(BF16),
                           preferred_element_type=F32)
            rank = jnp.zeros((nb, blk), F32)
            for jp in range(nb):
                row = gate[jp:jp + 1, :]
                beats = (row > gate) | ((row == gate) & (jidx > jp))
                rank = rank + jnp.where(beats, jnp.where(jp < i, 1.0, 0.0), 0.0)
            sel = (rank < topk) & (jidx < i)
            q_s = q_t * qk_scale
            pad = jnp.zeros((LANES - dh - nb, blk), F32)
            qaug_ref[h] = jnp.concatenate([q_s, jnp.where(sel, 0.0, MASK_BIAS), pad], axis=0).astype(BF16)
            qaug_own = jnp.concatenate([q_s, jnp.where(jidx == i, 0.0, MASK_BIAS), pad], axis=0).astype(BF16)
            st = jnp.dot(kn_ref[h, pl.ds(r_own, blk), :], qaug_own, preferred_element_type=F32)
            st = jnp.where(causal, st, -jnp.inf)
            s_ref[h, 0:blk, :] = st
            m_ref[h] = jnp.max(st, axis=0, keepdims=True)

    def finish_own(h):
        pr = jnp.exp2(s_ref[h, 0:blk, :] - m_ref[h]).astype(BF16)
        acc_ref[h] = jnp.dot(vt_ref[h, :, pl.ds(r_own, blk)], pr, preferred_element_type=F32)

    def score_pair(u, h):
        r0 = pl.multiple_of(u * pair, pair)
        st = jnp.dot(kn_ref[h, pl.ds(r0, pair), :], qaug_ref[h], preferred_element_type=F32)
        m_old = m_ref[h]
        m_new = jnp.maximum(m_old, jnp.max(st, axis=0, keepdims=True))
        s_ref[h] = st
        alpha_ref[h] = jnp.exp2(m_old - m_new)
        m_ref[h] = m_new

    def finish_pair(u, h):
        r0 = pl.multiple_of(u * pair, pair)
        pr = jnp.exp2(s_ref[h] - m_ref[h]).astype(BF16)
        acc_ref[h] = alpha_ref[h] * acc_ref[h] + jnp.dot(vt_ref[h, :, pl.ds(r0, pair)], pr,
                                                         preferred_element_type=F32)

    n_pairs = jnp.maximum((i + 1) // 2, 1)
    for h in range(nheads):
        finish_own(h)
        score_pair(0, h)

    def body(u, c):
        for h in range(nheads):
            finish_pair(u - 1, h)
            score_pair(u, h)
        return c

    lax.fori_loop(1, n_pairs, body, 0)
    for h in range(nheads):
        finish_pair(n_pairs - 1, h)

    for p in range(nheads // 2):
        a0 = acc_ref[2 * p]
        a1 = acc_ref[2 * p + 1]
        ot = jnp.concatenate([a0[0:dh] / a0[dh:dh + 1], a1[0:dh] / a1[dh:dh + 1]], axis=0)
        o_ref[:, p * LANES:(p + 1) * LANES] = ot.T.astype(o_ref.dtype)


def moba(proj, gq2, gk2, *, batch, seq):
    nb = seq // MOBA_BLOCK
    blk = MOBA_BLOCK
    dh = MOBA_HEAD_DIM
    nheads = MOBA_HEADS
    width = nheads * dh
    wb = width // LANES
    assert dh + nb <= LANES and 2 * dh == LANES and nb % 2 == 0
    v_rows = dh + 2 * SUBLANES
    body = functools.partial(_moba_body, nb=nb, blk=blk, dh=dh, topk=MOBA_TOPK, nheads=nheads)
    return pl.pallas_call(
        body,
        grid=(batch, nb),
        in_specs=[
            pl.BlockSpec((blk, width), lambda b, i: (b * nb + i, COL_QA // wb)),
            pl.BlockSpec((seq, width), lambda b, i: (b, COL_KA // wb)),
            pl.BlockSpec((seq, width), lambda b, i: (b, COL_VA // wb)),
            pl.BlockSpec((1, LANES), lambda b, i: (0, 0)),
            pl.BlockSpec((1, LANES), lambda b, i: (0, 0)),
        ],
        out_specs=pl.BlockSpec((blk, width), lambda b, i: (b * nb + i, 0)),
        out_shape=jax.ShapeDtypeStruct((batch * seq, width), BF16),
        scratch_shapes=[
            pltpu.VMEM((nheads, seq, LANES), BF16),
            pltpu.VMEM((nheads, v_rows, seq), BF16),
            pltpu.VMEM((nheads, nb, LANES), F32),
            pltpu.VMEM((nheads, LANES, blk), BF16),
            pltpu.VMEM((nheads, 2 * blk, blk), F32),
            pltpu.VMEM((nheads, 1, blk), F32),
            pltpu.VMEM((nheads, 1, blk), F32),
            pltpu.VMEM((nheads, v_rows, blk), F32),
        ],
        compiler_params=_params("arbitrary", "arbitrary"),
        name="moba",
    )(proj, proj, proj, gq2, gk2)


def _log_sigmoid(x):
    return jnp.minimum(x, 0.0) - jnp.log(1.0 + jnp.exp(-jnp.abs(x)))


def _dot_tri(tri, x, tri_left):
    out = None
    for _ in range(3):
        piece = x.astype(BF16)
        x = x - piece.astype(F32)
        term = (jnp.dot(tri, piece, preferred_element_type=F32) if tri_left
                else jnp.dot(piece, tri, preferred_element_type=F32))
        out = term if out is None else out + term
    return out


def _mlstm_body(qr_ref, kr_ref, v_ref, og_ref, gcol_ref, grow_ref, brow_ref, bcol_ref,
                cwq_ref, cwk_ref, cbq_ref, cbk_ref, gh_ref, o_ref,
                qx_ref, kx_ref, c_ref, m_ref, *, chunk, dk, nh):
    L = chunk
    width = nh * dk

    @pl.when(pl.program_id(1) == 0)
    def _():
        qx_ref[0:SUBLANES, :] = jnp.zeros((SUBLANES, width), F32)
        kx_ref[0:SUBLANES, :] = jnp.zeros((SUBLANES, width), F32)
        c_ref[...] = jnp.zeros_like(c_ref)
        m_ref[...] = jnp.zeros_like(m_ref)

    qx_ref[SUBLANES:SUBLANES + L, :] = qr_ref[...].astype(F32)
    kx_ref[SUBLANES:SUBLANES + L, :] = kr_ref[...].astype(F32)

    def conv_silu(x_ref, w_ref, b_ref):
        acc = b_ref[...] + w_ref[0:1, :] * x_ref[pl.ds(SUBLANES - CONV_WIDTH + 1, L), :]
        for j in range(1, CONV_WIDTH):
            acc = acc + w_ref[j:j + 1, :] * x_ref[pl.ds(SUBLANES - CONV_WIDTH + 1 + j, L), :]
        return acc * _sigmoid(acc)

    q_all = conv_silu(qx_ref, cwq_ref, cbq_ref)
    k_all = conv_silu(kx_ref, cwk_ref, cbk_ref) * (dk ** -0.5)
    qx_ref[0:SUBLANES, :] = qx_ref[L:L + SUBLANES, :]
    kx_ref[0:SUBLANES, :] = kx_ref[L:L + SUBLANES, :]

    pre_col = gcol_ref[...] + brow_ref[...]
    pre_row = grow_ref[0] + bcol_ref[...]
    t_i = lax.broadcasted_iota(jnp.int32, (L, L), 0)
    s_i = lax.broadcasted_iota(jnp.int32, (L, L), 1)
    tril = s_i <= t_i
    bcum_cols = _dot_tri(jnp.where(tril, 1.0, 0.0).astype(BF16), _log_sigmoid(pre_col), True)
    bcum_rows = _dot_tri(jnp.where(t_i <= s_i, 1.0, 0.0).astype(BF16), _log_sigmoid(pre_row), False)
    ones = jnp.ones((L, dk), BF16)

    for h in range(nh):
        cols = slice(h * dk, (h + 1) * dk)
        q = q_all[:, cols]
        k = k_all[:, cols]
        i_col = pre_col[:, h:h + 1]
        i_row = pre_row[h:h + 1, :]
        bcum_col = bcum_cols[:, nh + h:nh + h + 1]
        bcum_row = bcum_rows[nh + h:nh + h + 1, :]

        m_prev = m_ref[h, 0:1, 0:1]
        a_col = bcum_col + m_prev
        dmat = jnp.where(tril, bcum_col - bcum_row + i_row, -jnp.inf)
        m_t = jnp.maximum(a_col, jnp.max(dmat, axis=-1, keepdims=True))
        dw = jnp.exp(dmat - m_t)
        aw = jnp.exp(a_col - m_t)

        qb = q.astype(BF16)
        kb = k.astype(BF16)
        v_aug = jnp.concatenate([v_ref[:, cols], ones], axis=-1)
        sqk = _nt_dot(qb, kb) * dw
        num_aug = (aw * jnp.dot(qb, c_ref[h].astype(BF16), preferred_element_type=F32)
                   + jnp.dot(sqk.astype(BF16), v_aug, preferred_element_type=F32))
        den = num_aug[:, dk:dk + 1]
        hc = num_aug[:, 0:dk] / jnp.maximum(jnp.abs(den), jnp.exp(-m_t))

        b_last = bcum_col[L - 1:L, :]
        g_col = b_last - bcum_col + i_col
        m_new = jnp.maximum(b_last + m_prev, jnp.max(g_col, axis=0, keepdims=True))
        w_c = jnp.exp(b_last + m_prev - m_new)
        kw_t = (k * jnp.exp(g_col - m_new)).T.astype(BF16)
        c_ref[h] = w_c * c_ref[h] + jnp.dot(kw_t, v_aug, preferred_element_type=F32)
        m_ref[h] = jnp.broadcast_to(m_new, (1, LANES))

        o_ref[:, cols] = (_rms(hc, gh_ref[...]) * _sigmoid(og_ref[:, cols].astype(F32))).astype(o_ref.dtype)


def mlstm(proj, gates_col, gates_row, bias_row, bias_col, conv_w, conv_b, gh, *, batch, seq, chunk):
    nh = MLSTM_HEADS
    dk = MLSTM_DIM
    width = nh * dk
    wb = width // LANES
    nc = seq // chunk
    body = functools.partial(_mlstm_body, chunk=chunk, dk=dk, nh=nh)

    def rows(col0):
        return pl.BlockSpec((chunk, width), lambda b, c: (b * nc + c, col0 // wb))

    return pl.pallas_call(
        body,
        grid=(batch, nc),
        in_specs=[
            rows(COL_QM), rows(COL_KM), rows(COL_VM), rows(COL_OM),
            pl.BlockSpec((chunk, LANES), lambda b, c: (b * nc + c, 0)),
            pl.BlockSpec((1, SUBLANES, chunk), lambda b, c: (b, 0, c)),
            pl.BlockSpec((1, LANES), lambda b, c: (0, 0)),
            pl.BlockSpec((SUBLANES, 1), lambda b, c: (0, 0)),
            pl.BlockSpec((CONV_WIDTH, width), lambda b, c: (0, 0)),
            pl.BlockSpec((CONV_WIDTH, width), lambda b, c: (0, 1)),
            pl.BlockSpec((1, width), lambda b, c: (0, 0)),
            pl.BlockSpec((1, width), lambda b, c: (0, 1)),
            pl.BlockSpec((1, LANES), lambda b, c: (0, 0)),
        ],
        out_specs=pl.BlockSpec((chunk, width), lambda b, c: (b * nc + c, 0)),
        out_shape=jax.ShapeDtypeStruct((batch * seq, width), BF16),
        scratch_shapes=[
            pltpu.VMEM((chunk + 2 * SUBLANES, width), F32),
            pltpu.VMEM((chunk + 2 * SUBLANES, width), F32),
            pltpu.VMEM((nh, dk, 2 * dk), F32),
            pltpu.VMEM((nh, 1, LANES), F32),
        ],
        compiler_params=_params("arbitrary", "arbitrary"),
        name="mlstm",
    )(proj, proj, proj, proj, gates_col, gates_row, bias_row, bias_col,
      conv_w, conv_w, conv_b, conv_b, gh)


def _merge_body(ya_ref, yb_ref, ga_ref, gb_ref, x_ref, woa_ref, wob_ref, wout_ref, gffn_ref, *rest, moe):
    a = jnp.dot(ya_ref[...], woa_ref[...], preferred_element_type=F32)
    b = jnp.dot(yb_ref[...], wob_ref[...], preferred_element_type=F32)
    mixed = _sigmoid(ga_ref[...].astype(F32)) * a + _sigmoid(gb_ref[...].astype(F32)) * b
    x1 = x_ref[...] + jnp.dot(mixed.astype(BF16), wout_ref[...], preferred_element_type=F32)
    hf = _rms(x1, gffn_ref[...])
    if moe:
        wr_ref, x1_ref, hf_ref, lg_ref = rest
        _store_row_tiles(hf_ref, hf)
        lg_ref[...] = jnp.dot(hf.astype(BF16), wr_ref[...].astype(BF16), preferred_element_type=F32)
    else:
        x1_ref, hf_ref = rest
        hf_ref[...] = hf.astype(BF16)
    x1_ref[...] = x1


def merge(ya, yb, proj, x2, woa, wob, wout, gffn, wr, *, tm):
    t, d = x2.shape
    moe = wr is not None
    full = lambda m: (0, 0)
    in_specs = [
        pl.BlockSpec((tm, ya.shape[1]), lambda m: (m, 0)),
        pl.BlockSpec((tm, yb.shape[1]), lambda m: (m, 0)),
        pl.BlockSpec((tm, d), lambda m: (m, COL_GA * LANES // d)),
        pl.BlockSpec((tm, d), lambda m: (m, COL_GB * LANES // d)),
        pl.BlockSpec((tm, d), lambda m: (m, 0)),
        pl.BlockSpec(woa.shape, full), pl.BlockSpec(wob.shape, full), pl.BlockSpec(wout.shape, full),
        pl.BlockSpec((1, d), full),
    ]
    args = [ya, yb, proj, proj, x2, woa, wob, wout, gffn]
    out_specs = [pl.BlockSpec((tm, d), lambda m: (m, 0)), pl.BlockSpec((tm, d), lambda m: (m, 0))]
    out_shape = [jax.ShapeDtypeStruct((t, d), F32), jax.ShapeDtypeStruct((t, d), BF16)]
    if moe:
        out_specs[1] = pl.BlockSpec((tm * SUBLANES, LANES), lambda m: (m, 0))
        out_shape[1] = jax.ShapeDtypeStruct((t * SUBLANES, LANES), F32)
    if moe:
        in_specs.append(pl.BlockSpec(wr.shape, full))
        args.append(wr)
        out_specs.append(pl.BlockSpec((tm, LANES), lambda m: (m, 0)))
        out_shape.append(jax.ShapeDtypeStruct((t, LANES), F32))
    return pl.pallas_call(
        functools.partial(_merge_body, moe=moe),
        grid=(t // tm,),
        in_specs=in_specs, out_specs=out_specs, out_shape=out_shape,
        compiler_params=_params("arbitrary"),
        name="merge_moe" if moe else "merge",
    )(*args)


def _swiglu_gate(x, w1_ref, w3_ref, g_ref, fc):
    dff = g_ref.shape[1]
    for f0 in range(0, dff, fc):
        a = jnp.dot(x, w1_ref[:, f0:f0 + fc].astype(BF16), preferred_element_type=F32)
        b = jnp.dot(x, w3_ref[:, f0:f0 + fc].astype(BF16), preferred_element_type=F32)
        g_ref[:, f0:f0 + fc] = (a * _sigmoid(a) * b).astype(BF16)


def _swiglu(x, w1_ref, w3_ref, w2_ref, g_ref, fc):
    _swiglu_gate(x, w1_ref, w3_ref, g_ref, fc)
    return jnp.dot(g_ref[...], w2_ref[...].astype(BF16), preferred_element_type=F32)


def _ple(x, p_ref, g_ref, wg_ref, wp_ref):
    gate = _sigmoid(jnp.dot(_rms(x, g_ref[...]).astype(BF16), wg_ref[...], preferred_element_type=F32))
    emb = jnp.dot(p_ref[...].astype(BF16), wp_ref[...], preferred_element_type=F32)
    return x + gate * emb


def _ffn_body(hf_ref, x1_ref, w1_ref, w3_ref, w2_ref, p_ref, g_ref, wg_ref, wp_ref, o_ref, act_ref, *, fc):
    x2 = x1_ref[...] + _swiglu(hf_ref[...], w1_ref, w3_ref, w2_ref, act_ref, fc)
    o_ref[...] = _ple(x2, p_ref, g_ref, wg_ref, wp_ref)


def dense_ffn(hf, x1, w1, w3, w2, p2, p_row0, g, wg, wp, *, tm, fc):
    t, d = x1.shape
    dff = w1.shape[1]
    p_blk0 = p_row0 // tm
    resident = dict(pipeline_mode=pl.Buffered(1))
    full = lambda m: (0, 0)
    return pl.pallas_call(
        functools.partial(_ffn_body, fc=fc),
        grid=(t // tm,),
        in_specs=[
            pl.BlockSpec((tm, d), lambda m: (m, 0)),
            pl.BlockSpec((tm, d), lambda m: (m, 0)),
            pl.BlockSpec((d, dff), full, **resident),
            pl.BlockSpec((d, dff), full, **resident),
            pl.BlockSpec((dff, d), full, **resident),
            pl.BlockSpec((tm, p2.shape[1]), lambda m: (p_blk0 + m, 0)),
            pl.BlockSpec((1, d), full),
            pl.BlockSpec(wg.shape, full, **resident),
            pl.BlockSpec(wp.shape, full, **resident),
        ],
        out_specs=pl.BlockSpec((tm, d), lambda m: (m, 0)),
        out_shape=jax.ShapeDtypeStruct((t, d), F32),
        scratch_shapes=[pltpu.VMEM((tm, dff), BF16)],
        compiler_params=_params("arbitrary"),
        name="dense_ffn",
    )(hf, x1, w1, w3, w2, p2, g, wg, wp)


META_E0, META_E1, META_G0, META_G1, META_R0, META_R1 = 0, 1, 2, 3, 4, 5


def _route_body(lg_ref, meta_ref, tab_ref, cnt_ref, carry_ref, *, tm, ne):
    @pl.when(pl.program_id(0) == 0)
    def _():
        carry_ref[...] = jnp.zeros_like(carry_ref)

    lane = lax.broadcasted_iota(jnp.int32, (tm, LANES), 1)
    lanef = lane.astype(F32)
    lg = jnp.where(lane < ne, lg_ref[...], -jnp.inf)
    m1 = jnp.max(lg, axis=-1, keepdims=True)
    e1 = jnp.min(jnp.where(lg == m1, lanef, float(LANES)), axis=-1, keepdims=True)
    lg2 = jnp.where(lanef == e1, -jnp.inf, lg)
    m2 = jnp.max(lg2, axis=-1, keepdims=True)
    e2 = jnp.min(jnp.where(lg2 == m2, lanef, float(LANES)), axis=-1, keepdims=True)
    ex = jnp.exp(m2 - m1)
    g1 = 1.0 / (1.0 + ex)
    g2 = ex / (1.0 + ex)
    onehot = jnp.where((lanef == e1) | (lanef == e2), 1.0, 0.0)
    row = lax.broadcasted_iota(jnp.int32, (tm, tm), 0)
    col = lax.broadcasted_iota(jnp.int32, (tm, tm), 1)
    before = jnp.dot(jnp.where(col < row, 1.0, 0.0).astype(BF16), onehot.astype(BF16),
                     preferred_element_type=F32) + carry_ref[...]
    r1 = jnp.sum(jnp.where(lanef == e1, before, 0.0), axis=-1, keepdims=True)
    r2 = jnp.sum(jnp.where(lanef == e2, before, 0.0), axis=-1, keepdims=True)
    carry_ref[...] += jnp.sum(onehot, axis=0, keepdims=True)
    meta = jnp.zeros((tm, LANES), F32)
    for pos, val in ((META_E0, e1), (META_E1, e2), (META_G0, g1), (META_G1, g2), (META_R0, r1), (META_R1, r2)):
        meta = jnp.where(lane == pos, val, meta)
    meta_ref[...] = meta
    tab_ref[...] = meta.T[0:SUBLANES, :]
    cnt_ref[...] = jnp.broadcast_to(carry_ref[...], cnt_ref.shape)


def route(logits, *, tm):
    t = logits.shape[0]
    return pl.pallas_call(
        functools.partial(_route_body, tm=tm, ne=N_EXPERTS),
        grid=(t // tm,),
        in_specs=[pl.BlockSpec((tm, LANES), lambda m: (m, 0))],
        out_specs=[pl.BlockSpec((tm, LANES), lambda m: (m, 0)),
                   pl.BlockSpec((SUBLANES, tm), lambda m: (0, m)),
                   pl.BlockSpec((SUBLANES, LANES), lambda m: (0, 0))],
        out_shape=[jax.ShapeDtypeStruct((t, LANES), F32), jax.ShapeDtypeStruct((SUBLANES, t), F32),
                   jax.ShapeDtypeStruct((SUBLANES, LANES), F32)],
        scratch_shapes=[pltpu.VMEM((1, LANES), F32)],
        compiler_params=_params("arbitrary"),
        name="route",
    )(logits)


def _dispatch_body(dest_ref, pe_ref, na_ref, hf_ref, xs_ref, zero_ref, sem, zsem, *, tm, topk, tm_rows, n_tok):
    base = pl.program_id(0) * tm

    @pl.when(pl.program_id(0) == 0)
    def _():
        zero_ref[...] = jnp.zeros_like(zero_ref)
        n_tiles = xs_ref.shape[0] // (tm_rows * SUBLANES)
        fills = []
        for e in range(N_EXPERTS):
            end = pe_ref[e]
            nonempty = end > (pe_ref[e - 1] if e else 0)
            fills.append((nonempty, pl.multiple_of(jnp.maximum(end - tm_rows, 0), tm_rows)))
        for tile in range(n_tiles):
            fills.append((tile >= na_ref[0], tile * tm_rows))
        for phase in ("start", "wait"):
            for cond, row0 in fills:
                @pl.when(cond)
                def _(row0=row0, phase=phase):
                    cp = pltpu.make_async_copy(
                        zero_ref, xs_ref.at[pl.ds(row0 * SUBLANES, tm_rows * SUBLANES)], zsem)
                    cp.start() if phase == "start" else cp.wait()

    def issue(r, c):
        for k in range(topk):
            d = dest_ref[k * n_tok + base + r]
            pltpu.make_async_copy(_row_tile(hf_ref, r), _row_tile(xs_ref, d), sem).start(priority=k % 2)
        return c

    lax.fori_loop(0, tm, issue, 0, unroll=ROW_DMA_UNROLL)
    for k in range(topk):
        pltpu.make_async_copy(hf_ref, xs_ref.at[pl.ds(0, tm * SUBLANES)], sem).wait()


def _row_tile(ref, r):
    return ref.at[pl.ds(pl.multiple_of(r * SUBLANES, SUBLANES), SUBLANES)]


def dispatch(dest, pad_end, n_active, hf, *, n_rows, tm, topk, tm_rows):
    t = hf.shape[0] // SUBLANES
    return pl.pallas_call(
        functools.partial(_dispatch_body, tm=tm, topk=topk, tm_rows=tm_rows, n_tok=t),
        grid_spec=pltpu.PrefetchScalarGridSpec(
            num_scalar_prefetch=3,
            grid=(t // tm,),
            in_specs=[pl.BlockSpec((tm * SUBLANES, LANES), lambda m, *_: (m, 0))],
            out_specs=pl.BlockSpec(memory_space=pl.ANY),
            scratch_shapes=[pltpu.VMEM((tm_rows * SUBLANES, LANES), F32),
                            pltpu.SemaphoreType.DMA, pltpu.SemaphoreType.DMA],
        ),
        out_shape=jax.ShapeDtypeStruct((n_rows * SUBLANES, LANES), F32),
        compiler_params=_params("arbitrary"),
        name="moe_dispatch",
    )(dest, pad_end, n_active, hf)


def _experts_body(te_ref, na_ref, xs_ref, w1_hbm, w3_hbm, w2_hbm, y_ref,
                  xb_ref, g_ref, w1_buf, w3_buf, w2_buf, wsem, *, fc):
    i = pl.program_id(0)
    n_tiles = pl.num_programs(0)
    na = na_ref[0]
    tm = xb_ref.shape[0]
    tf = g_ref.shape[2]

    def weight_copies(tile, half):
        e = te_ref[tile]
        cols = pl.ds(half * tf, tf)
        return (pltpu.make_async_copy(w1_hbm.at[e, :, cols], w1_buf.at[half], wsem.at[half, 0]),
                pltpu.make_async_copy(w3_hbm.at[e, :, cols], w3_buf.at[half], wsem.at[half, 1]),
                pltpu.make_async_copy(w2_hbm.at[e, cols, :], w2_buf.at[half], wsem.at[half, 2]))

    def start(tile, half):
        for cp in weight_copies(tile, half):
            cp.start()

    def wait(half):
        for cp in weight_copies(0, half):
            cp.wait()

    @pl.when(i == 0)
    def _():
        start(0, 0)

    @pl.when(i >= na)
    def _():
        y_ref[...] = jnp.zeros_like(y_ref)

    @pl.when(i < na)
    def _():
        start(i, 1)
        wait(0)
        xb_ref[...] = _load_row_tiles(xs_ref, tm).astype(BF16)
        xb = xb_ref[...]
        _swiglu_gate(xb, w1_buf.at[0], w3_buf.at[0], g_ref.at[0], fc)
        y = jnp.dot(g_ref[0], w2_buf[0].astype(BF16), preferred_element_type=F32)
        start(jnp.minimum(i + 1, na - 1), 0)
        wait(1)
        _swiglu_gate(xb, w1_buf.at[1], w3_buf.at[1], g_ref.at[1], fc)
        y = y + jnp.dot(g_ref[1], w2_buf[1].astype(BF16), preferred_element_type=F32)
        _store_row_tiles(y_ref, y)

    @pl.when((i == na) | ((i == n_tiles - 1) & (i < na)))
    def _():
        wait(0)


def experts(tile_expert, n_active, xs, w1, w3, w2, *, tm, fc):
    n_rows = xs.shape[0] // SUBLANES
    d = w1.shape[1]
    dff = w1.shape[2]
    tf = dff // 2
    return pl.pallas_call(
        functools.partial(_experts_body, fc=fc),
        grid_spec=pltpu.PrefetchScalarGridSpec(
            num_scalar_prefetch=2,
            grid=(n_rows // tm,),
            in_specs=[
                pl.BlockSpec((tm * SUBLANES, LANES), lambda i, te, na: (jnp.minimum(i, na[0] - 1), 0)),
                pl.BlockSpec(memory_space=pl.ANY),
                pl.BlockSpec(memory_space=pl.ANY),
                pl.BlockSpec(memory_space=pl.ANY),
            ],
            out_specs=pl.BlockSpec((tm * SUBLANES, LANES), lambda i, te, na: (i, 0)),
            scratch_shapes=[
                pltpu.VMEM((tm, d), BF16),
                pltpu.VMEM((2, tm, tf), BF16),
                pltpu.VMEM((2, d, tf), w1.dtype),
                pltpu.VMEM((2, d, tf), w3.dtype),
                pltpu.VMEM((2, tf, d), w2.dtype),
                pltpu.SemaphoreType.DMA((2, 3)),
            ],
        ),
        out_shape=jax.ShapeDtypeStruct((n_rows * SUBLANES, LANES), F32),
        compiler_params=_params("arbitrary"),
        name="moe_experts",
    )(tile_expert, n_active, xs, w1, w3, w2)


def _combine_body(dest_ref, x1_ref, meta_ref, p_ref, g_ref, wg_ref, wp_ref, y_ref, o_ref, buf_ref, sem,
                  *, tm, topk):
    m = pl.program_id(0)
    n_steps = pl.num_programs(0)
    n_tok = n_steps * tm

    def start_gather(tile, slot):
        def issue(r, c):
            for k in range(topk):
                d = dest_ref[k * n_tok + tile * tm + r]
                pltpu.make_async_copy(_row_tile(y_ref, d), _row_tile(buf_ref.at[slot, k], r),
                                      sem.at[slot]).start(priority=k % 2)
            return c

        lax.fori_loop(0, tm, issue, 0, unroll=ROW_DMA_UNROLL)

    @pl.when(m == 0)
    def _():
        start_gather(0, 0)

    @pl.when(m + 1 < n_steps)
    def _():
        start_gather(m + 1, (m + 1) % 2)

    slot = m % 2
    for k in range(topk):
        pltpu.make_async_copy(y_ref.at[pl.ds(0, tm * SUBLANES)], buf_ref.at[slot, k], sem.at[slot]).wait()
    meta = meta_ref[...]
    g0 = meta[:, META_G0:META_G0 + 1]
    g1 = meta[:, META_G1:META_G1 + 1]
    x2 = x1_ref[...] + (g0 * _load_row_tiles(buf_ref.at[slot, 0], tm)
                        + g1 * _load_row_tiles(buf_ref.at[slot, 1], tm))
    o_ref[...] = _ple(x2, p_ref, g_ref, wg_ref, wp_ref)


def combine(dest, x1, meta, y, p2, p_row0, g, wg, wp, *, tm, topk):
    t, d = x1.shape
    full = lambda m, dest: (0, 0)
    p_blk0 = p_row0 // tm
    return pl.pallas_call(
        functools.partial(_combine_body, tm=tm, topk=topk),
        grid_spec=pltpu.PrefetchScalarGridSpec(
            num_scalar_prefetch=1,
            grid=(t // tm,),
            in_specs=[pl.BlockSpec((tm, d), lambda m, dest: (m, 0)),
                      pl.BlockSpec((tm, LANES), lambda m, dest: (m, 0)),
                      pl.BlockSpec((tm, p2.shape[1]), lambda m, dest: (p_blk0 + m, 0)),
                      pl.BlockSpec((1, d), full), pl.BlockSpec(wg.shape, full), pl.BlockSpec(wp.shape, full),
                      pl.BlockSpec(memory_space=pl.ANY)],
            out_specs=pl.BlockSpec((tm, d), lambda m, dest: (m, 0)),
            scratch_shapes=[pltpu.VMEM((2, topk, tm * SUBLANES, LANES), F32), pltpu.SemaphoreType.DMA((2,))],
        ),
        out_shape=jax.ShapeDtypeStruct((t, d), F32),
        compiler_params=_params("arbitrary"),
        name="moe_combine",
    )(dest, x1, meta, p2, g, wg, wp, y)


def moe_ffn(hf, x1, logits, w1, w3, w2, ple_args, *, tm_route, tm_rows, tm_move):
    t, d = x1.shape
    topk = 2
    meta, tab, cnt = route(logits, tm=tm_route)
    counts = cnt[0, :N_EXPERTS].astype(jnp.int32)
    padded = ((counts + tm_rows - 1) // tm_rows) * tm_rows
    pad_end = jnp.cumsum(padded).astype(jnp.int32)
    pad_start = pad_end - padded
    eidx = tab[META_E0:META_E1 + 1].astype(jnp.int32)
    rank = tab[META_R0:META_R1 + 1].astype(jnp.int32)
    dest = rank
    for e in range(N_EXPERTS):
        dest = dest + jnp.where(eidx == e, pad_start[e], 0)
    dest = dest.reshape(topk * t)
    n_tiles = -(-(t * topk) // tm_rows) + N_EXPERTS
    tile_start = jnp.arange(n_tiles, dtype=jnp.int32) * tm_rows
    tile_expert = jnp.minimum(jnp.sum(tile_start[:, None] >= pad_end[None, :], axis=1),
                              N_EXPERTS - 1).astype(jnp.int32)
    n_active = pad_end[N_EXPERTS - 1:] // tm_rows
    xs = dispatch(dest, pad_end, n_active, hf, n_rows=n_tiles * tm_rows, tm=tm_move, topk=topk, tm_rows=tm_rows)
    y = experts(tile_expert, n_active, xs, w1, w3, w2, tm=tm_rows, fc=256)
    return combine(dest, x1, meta, y, *ple_args, tm=tm_move, topk=topk)


def _tile2(g):
    return jnp.concatenate([g, g]).reshape(1, 2 * g.shape[0])


def kernel(x, p, g_mix, w_in, g_q, g_k, conv_w, conv_b, b_i, b_f, g_h, w_oa, w_ob, w_out, g_ffn, w_d1, w_d3,
           w_d2, w_router, w_e1, w_e3, w_e2, g_ple, w_ple_gate, w_ple_proj):
    batch, seq, d = x.shape
    depth = w_in.shape[0]
    t = batch * seq
    nh = MLSTM_HEADS
    x2 = x.reshape(t, d)
    c_q, c_k, c_v = 0, 512, 1024
    c_qk, c_vm, c_om, c_i, c_f, c_ga, c_gb, c_end = 1536, 2560, 3072, 3584, 3588, 3592, 4616, 5640

    w_in_t = jnp.swapaxes(w_in, 1, 2).astype(BF16)
    for l in range(depth):
        w_gates = w_in_t[l, c_ga:c_end]
        proj, gif = in_proj(x2, g_mix[l].reshape(1, d), w_gates, w_in_t, l, nb_cols=c_i, if_col=c_i,
                            tm=1024, tn=512)

        ya = moba(proj, _tile2(g_q[l]), _tile2(g_k[l]), batch=batch, seq=seq)

        bias = jnp.concatenate([b_i[l], b_f[l]])
        bias_row = jnp.pad(bias, (0, LANES - 2 * nh)).reshape(1, LANES)
        bias_col = bias.reshape(2 * nh, 1)
        gates_row = gif[:, :2 * nh].reshape(batch, seq, 2 * nh).transpose(0, 2, 1)
        yb = mlstm(proj, gif, gates_row, bias_row, bias_col, conv_w[l], conv_b[l].reshape(1, -1),
                   g_h[l].reshape(1, -1), batch=batch, seq=seq, chunk=256)

        j = l // 2
        moe = l % 2 == 1
        wr = jnp.pad(w_router[j], ((0, 0), (0, LANES - N_EXPERTS))) if moe else None
        outs = merge(ya, yb, proj, x2, w_oa[l].astype(BF16), w_ob[l].astype(BF16), w_out[l].astype(BF16),
                     g_ffn[l].reshape(1, d), wr, tm=1024)
        ple_args = (p.reshape(depth * t, -1), l * t, g_ple[l].reshape(1, d), w_ple_gate[l].astype(BF16),
                    w_ple_proj[l].astype(BF16))
        if moe:
            x1, hf, logits = outs
            x2 = moe_ffn(hf, x1, logits, w_e1[j], w_e3[j].astype(BF16), w_e2[j],
                         ple_args, tm_route=512, tm_rows=512, tm_move=512)
        else:
            x1, hf = outs
            x2 = dense_ffn(hf, x1, w_d1[j].astype(BF16), w_d3[j].astype(BF16), w_d2[j].astype(BF16),
                           *ple_args, tm=1024, fc=256)
    return x2.reshape(batch, seq, d)
```

```python
import functools

import jax
import jax.numpy as jnp
from jax import lax
from jax.experimental import pallas as pl
from jax.experimental.pallas import tpu as pltpu

F32 = jnp.float32
BF16 = jnp.bfloat16

RMS_EPS = 1e-6
LANES = 128
SUBLANES = 8

MOBA_HEADS = 8
MOBA_HEAD_DIM = 64
MOBA_BLOCK = 256
MOBA_TOPK = 3
MLSTM_HEADS = 4
MLSTM_DIM = 128
CONV_WIDTH = 4
N_EXPERTS = 8

COL_GA, COL_GB = 0, 8
COL_QA, COL_KA, COL_VA = 16, 20, 24
COL_QM, COL_KM, COL_VM, COL_OM = 28, 32, 36, 40
N_PROJ = 44 * LANES

VMEM_LIMIT = 56 * 1024 * 1024
ROW_DMA_UNROLL = 16


def _params(*sem):
    return pltpu.CompilerParams(dimension_semantics=sem, vmem_limit_bytes=VMEM_LIMIT)


def _sigmoid(x):
    return 1.0 / (1.0 + jnp.exp(-x))


def _rms(x, g):
    return x * lax.rsqrt(jnp.mean(x * x, axis=-1, keepdims=True) + RMS_EPS) * g


def _split_bf16(x):
    hi = x.astype(BF16)
    return hi, (x - hi.astype(F32)).astype(BF16)


def _store_row_tiles(ref, x):
    ref[...] = x.reshape(x.shape[0] * SUBLANES, LANES)


def _load_row_tiles(ref, rows):
    return ref[...].reshape(rows, SUBLANES * LANES)


def _nt_dot(a, b, **kw):
    return lax.dot_general(a, b, (((1,), (1,)), ((), ())), preferred_element_type=F32, **kw)


def _in_proj_body(x_ref, g_ref, wa_ref, wb_ref, o_ref, oif_ref, h_ref, *, nb_cols, if_col, tn):
    h_ref[...] = _rms(x_ref[...], g_ref[...]).astype(BF16)
    oif_ref[...] = _nt_dot(h_ref[...], wb_ref[if_col:if_col + LANES, :])
    na = wa_ref.shape[0]
    for c0 in range(0, na + nb_cols, tn):
        w = wa_ref[c0:c0 + tn, :] if c0 < na else wb_ref[c0 - na:c0 - na + tn, :]
        o_ref[:, c0:c0 + tn] = _nt_dot(h_ref[...], w).astype(o_ref.dtype)


def in_proj(x2, g, wa, w_full, layer, *, nb_cols, if_col, tm, tn):
    t, d = x2.shape
    n = wa.shape[0] + nb_cols
    resident = dict(pipeline_mode=pl.Buffered(1))
    return pl.pallas_call(
        functools.partial(_in_proj_body, nb_cols=nb_cols, if_col=if_col, tn=tn),
        grid=(t // tm,),
        in_specs=[
            pl.BlockSpec((tm, d), lambda m: (m, 0)),
            pl.BlockSpec((1, d), lambda m: (0, 0)),
            pl.BlockSpec(wa.shape, lambda m: (0, 0), **resident),
            pl.BlockSpec((None,) + w_full.shape[1:], lambda m: (layer, 0, 0), **resident),
        ],
        out_specs=[
            pl.BlockSpec((tm, n), lambda m: (m, 0)),
            pl.BlockSpec((tm, LANES), lambda m: (m, 0)),
        ],
        out_shape=[jax.ShapeDtypeStruct((t, n), BF16), jax.ShapeDtypeStruct((t, LANES), F32)],
        scratch_shapes=[pltpu.VMEM((tm, d), BF16)],
        compiler_params=_params("arbitrary"),
        name="in_proj",
    )(x2, g, wa, w_full)


MASK_BIAS = -1e30
LOG2_E = 1.4426950408889634


def _moba_body(q_ref, k_ref, v_ref, gq_ref, gk_ref, o_ref,
               kn_ref, vt_ref, kmean_ref, qaug_ref, s_ref, m_ref, alpha_ref, acc_ref,
               *, nb, blk, dh, topk, nheads):
    i = pl.program_id(1)
    pair = 2 * blk
    lane = lax.broadcasted_iota(jnp.int32, (1, LANES), 1)
    head0 = lane < dh

    same_head = (lax.broadcasted_iota(jnp.int32, (LANES, LANES), 0) // dh
                 == lax.broadcasted_iota(jnp.int32, (LANES, LANES), 1) // dh)
    head_ones = jnp.where(same_head, 1.0, 0.0).astype(BF16)

    def head_rms(x, g, on_mxu):
        x2 = x * x
        if on_mxu:
            hi, lo = _split_bf16(x2)
            ss = (jnp.dot(hi, head_ones, preferred_element_type=F32)
                  + jnp.dot(lo, head_ones, preferred_element_type=F32))
        else:
            s0 = jnp.sum(jnp.where(head0, x2, 0.0), axis=-1, keepdims=True)
            s1 = jnp.sum(jnp.where(head0, 0.0, x2), axis=-1, keepdims=True)
            ss = jnp.where(head0, s0, s1)
        return x * lax.rsqrt(ss * (1.0 / dh) + RMS_EPS) * g

    @pl.when(i == 0)
    def _():
        def prep(j, c):
            r0 = pl.multiple_of(j * blk, blk)
            onehot = jnp.where(lane == dh + j, 1.0, 0.0)
            for p in range(nheads // 2):
                cols = slice(p * LANES, (p + 1) * LANES)
                kn = head_rms(k_ref[pl.ds(r0, blk), cols].astype(F32), gk_ref[...], True)
                for hh, kh in ((0, kn), (1, pltpu.roll(kn, dh, axis=1))):
                    h = 2 * p + hh
                    kmean_ref[h, pl.ds(j, 1), :] = jnp.mean(jnp.where(head0, kh, 0.0), axis=0, keepdims=True)
                    kn_ref[h, pl.ds(r0, blk), :] = jnp.where(head0, kh, onehot).astype(BF16)
                v_t = v_ref[pl.ds(r0, blk), cols].astype(F32).T.astype(BF16)
                for hh in range(2):
                    vt_ref[2 * p + hh, 0:dh, pl.ds(r0, blk)] = v_t[hh * dh:(hh + 1) * dh, :]
                    vt_ref[2 * p + hh, dh:, pl.ds(r0, blk)] = jnp.ones((vt_ref.shape[1] - dh, blk), BF16)
            return c

        lax.fori_loop(0, nb, prep, 0)

    jidx = lax.broadcasted_iota(jnp.int32, (nb, blk), 0)
    key_i = lax.broadcasted_iota(jnp.int32, (blk, blk), 0)
    qry_i = lax.broadcasted_iota(jnp.int32, (blk, blk), 1)
    causal = key_i <= qry_i
    r_own = pl.multiple_of(i * blk, blk)
    qk_scale = dh ** -0.5 * LOG2_E
    for p in range(nheads // 2):
        cols = slice(p * LANES, (p + 1) * LANES)
        qn_t = head_rms(q_ref[:, cols].astype(F32), gq_ref[...], False).T
        for hh in range(2):
            h = 2 * p + hh
            q_t = qn_t[hh * dh:(hh + 1) * dh, :]
            gate = jnp.dot(kmean_ref[h].astype(BF16),
                           jnp.concatenate([q_t, jnp.zeros((LANES - dh, blk), F32)], axis=0).astype(BF16),
                           preferred_element_type=F32)
            rank = jnp.zeros((nb, blk), F32)
            for jp in range(nb):
                row = gate[jp:jp + 1, :]
                beats = (row > gate) | ((row == gate) & (jidx > jp))
                rank = rank + jnp.where(beats, jnp.where(jp < i, 1.0, 0.0), 0.0)
            sel = (rank < topk) & (jidx < i)
            q_s = q_t * qk_scale
            pad = jnp.zeros((LANES - dh - nb, blk), F32)
            qaug_ref[h] = jnp.concatenate([q_s, jnp.where(sel, 0.0, MASK_BIAS), pad], axis=0).astype(BF16)
            qaug_own = jnp.concatenate([q_s, jnp.where(jidx == i, 0.0, MASK_BIAS), pad], axis=0).astype(BF16)
            st = jnp.dot(kn_ref[h, pl.ds(r_own, blk), :], qaug_own, preferred_element_type=F32)
            st = jnp.where(causal, st, -jnp.inf)
            s_ref[h, 0:blk, :] = st
            m_ref[h] = jnp.max(st, axis=0, keepdims=True)

    def finish_own(h):
        pr = jnp.exp2(s_ref[h, 0:blk, :] - m_ref[h]).astype(BF16)
        acc_ref[h] = jnp.dot(vt_ref[h, :, pl.ds(r_own, blk)], pr, preferred_element_type=F32)

    def score_pair(u, h):
        r0 = pl.multiple_of(u * pair, pair)
        st = jnp.dot(kn_ref[h, pl.ds(r0, pair), :], qaug_ref[h], preferred_element_type=F32)
        m_old = m_ref[h]
        m_new = jnp.maximum(m_old, jnp.max(st, axis=0, keepdims=True))
        s_ref[h] = st
        alpha_ref[h] = jnp.exp2(m_old - m_new)
        m_ref[h] = m_new

    def finish_pair(u, h):
        r0 = pl.multiple_of(u * pair, pair)
        pr = jnp.exp2(s_ref[h] - m_ref[h]).astype(BF16)
        acc_ref[h] = alpha_ref[h] * acc_ref[h] + jnp.dot(vt_ref[h, :, pl.ds(r0, pair)], pr,
                                                         preferred_element_type=F32)

    n_pairs = jnp.maximum((i + 1) // 2, 1)
    for h in range(nheads):
        finish_own(h)
        score_pair(0, h)

    def body(u, c):
        for h in range(nheads):
            finish_pair(u - 1, h)
            score_pair(u, h)
        return c

    lax.fori_loop(1, n_pairs, body, 0)
    for h in range(nheads):
        finish_pair(n_pairs - 1, h)

    for p in range(nheads // 2):
        a0 = acc_ref[2 * p]
        a1 = acc_ref[2 * p + 1]
        ot = jnp.concatenate([a0[0:dh] / a0[dh:dh + 1], a1[0:dh] / a1[dh:dh + 1]], axis=0)
        o_ref[:, p * LANES:(p + 1) * LANES] = ot.T.astype(o_ref.dtype)


def moba(proj, gq2, gk2, *, batch, seq):
    nb = seq // MOBA_BLOCK
    blk = MOBA_BLOCK
    dh = MOBA_HEAD_DIM
    nheads = MOBA_HEADS
    width = nheads * dh
    wb = width // LANES
    assert dh + nb <= LANES and 2 * dh == LANES and nb % 2 == 0
    v_rows = dh + 2 * SUBLANES
    body = functools.partial(_moba_body, nb=nb, blk=blk, dh=dh, topk=MOBA_TOPK, nheads=nheads)
    return pl.pallas_call(
        body,
        grid=(batch, nb),
        in_specs=[
            pl.BlockSpec((blk, width), lambda b, i: (b * nb + i, COL_QA // wb)),
            pl.BlockSpec((seq, width), lambda b, i: (b, COL_KA // wb)),
            pl.BlockSpec((seq, width), lambda b, i: (b, COL_VA // wb)),
            pl.BlockSpec((1, LANES), lambda b, i: (0, 0)),
            pl.BlockSpec((1, LANES), lambda b, i: (0, 0)),
        ],
        out_specs=pl.BlockSpec((blk, width), lambda b, i: (b * nb + i, 0)),
        out_shape=jax.ShapeDtypeStruct((batch * seq, width), BF16),
        scratch_shapes=[
            pltpu.VMEM((nheads, seq, LANES), BF16),
            pltpu.VMEM((nheads, v_rows, seq), BF16),
            pltpu.VMEM((nheads, nb, LANES), F32),
            pltpu.VMEM((nheads, LANES, blk), BF16),
            pltpu.VMEM((nheads, 2 * blk, blk), F32),
            pltpu.VMEM((nheads, 1, blk), F32),
            pltpu.VMEM((nheads, 1, blk), F32),
            pltpu.VMEM((nheads, v_rows, blk), F32),
        ],
        compiler_params=_params("arbitrary", "arbitrary"),
        name="moba",
    )(proj, proj, proj, gq2, gk2)


def _log_sigmoid(x):
    return jnp.minimum(x, 0.0) - jnp.log(1.0 + jnp.exp(-jnp.abs(x)))


def _dot_tri(tri, x, tri_left):
    out = None
    for _ in range(3):
        piece = x.astype(BF16)
        x = x - piece.astype(F32)
        term = (jnp.dot(tri, piece, preferred_element_type=F32) if tri_left
                else jnp.dot(piece, tri, preferred_element_type=F32))
        out = term if out is None else out + term
    return out


def _mlstm_body(qr_ref, kr_ref, v_ref, og_ref, gcol_ref, grow_ref, brow_ref, bcol_ref,
                cwq_ref, cwk_ref, cbq_ref, cbk_ref, gh_ref, o_ref,
                qx_ref, kx_ref, c_ref, m_ref, *, chunk, dk, nh):
    L = chunk
    width = nh * dk

    @pl.when(pl.program_id(1) == 0)
    def _():
        qx_ref[0:SUBLANES, :] = jnp.zeros((SUBLANES, width), F32)
        kx_ref[0:SUBLANES, :] = jnp.zeros((SUBLANES, width), F32)
        c_ref[...] = jnp.zeros_like(c_ref)
        m_ref[...] = jnp.zeros_like(m_ref)

    qx_ref[SUBLANES:SUBLANES + L, :] = qr_ref[...].astype(F32)
    kx_ref[SUBLANES:SUBLANES + L, :] = kr_ref[...].astype(F32)

    def conv_silu(x_ref, w_ref, b_ref):
        acc = b_ref[...] + w_ref[0:1, :] * x_ref[pl.ds(SUBLANES - CONV_WIDTH + 1, L), :]
        for j in range(1, CONV_WIDTH):
            acc = acc + w_ref[j:j + 1, :] * x_ref[pl.ds(SUBLANES - CONV_WIDTH + 1 + j, L), :]
        return acc * _sigmoid(acc)

    q_all = conv_silu(qx_ref, cwq_ref, cbq_ref)
    k_all = conv_silu(kx_ref, cwk_ref, cbk_ref) * (dk ** -0.5)
    qx_ref[0:SUBLANES, :] = qx_ref[L:L + SUBLANES, :]
    kx_ref[0:SUBLANES, :] = kx_ref[L:L + SUBLANES, :]

    pre_col = gcol_ref[...] + brow_ref[...]
    pre_row = grow_ref[0] + bcol_ref[...]
    t_i = lax.broadcasted_iota(jnp.int32, (L, L), 0)
    s_i = lax.broadcasted_iota(jnp.int32, (L, L), 1)
    tril = s_i <= t_i
    bcum_cols = _dot_tri(jnp.where(tril, 1.0, 0.0).astype(BF16), _log_sigmoid(pre_col), True)
    bcum_rows = _dot_tri(jnp.where(t_i <= s_i, 1.0, 0.0).astype(BF16), _log_sigmoid(pre_row), False)
    src_before_out = t_i <= s_i
    v_t = v_ref[...].astype(F32).T
    ones_t = jnp.ones((dk, L), F32)

    for h in range(nh):
        cols = slice(h * dk, (h + 1) * dk)
        qb = q_all[:, cols].astype(BF16)
        kb = k_all[:, cols].astype(BF16)
        i_col = pre_col[:, h:h + 1]
        i_row = pre_row[h:h + 1, :]
        bcum_col = bcum_cols[:, nh + h:nh + h + 1]
        bcum_row = bcum_rows[nh + h:nh + h + 1, :]

        m_prev = m_ref[h, 0:1, 0:1]
        a_row = bcum_row + m_prev
        dmat_t = jnp.where(src_before_out, bcum_row + (i_col - bcum_col), -jnp.inf)
        m_t = jnp.maximum(a_row, jnp.max(dmat_t, axis=0, keepdims=True))
        sqk_t = (_nt_dot(kb, qb) * jnp.exp(dmat_t - m_t)).astype(BF16)

        v_aug_t = jnp.concatenate([v_t[cols, :], ones_t], axis=0)
        state_t = c_ref[h]
        num_aug_t = (jnp.exp(a_row - m_t) * _nt_dot(state_t.astype(BF16), qb)
                     + jnp.dot(v_aug_t.astype(BF16), sqk_t, preferred_element_type=F32))
        den = num_aug_t[dk:dk + 1, :]
        hc_t = num_aug_t[0:dk, :] / jnp.maximum(jnp.abs(den), jnp.exp(-m_t))
        hn_t = hc_t * lax.rsqrt(jnp.mean(hc_t * hc_t, axis=0, keepdims=True) + RMS_EPS)
        o_ref[:, cols] = (hn_t.T * gh_ref[...] * _sigmoid(og_ref[:, cols].astype(F32))).astype(o_ref.dtype)

        b_last = bcum_row[:, L - 1:L]
        g_row = b_last - bcum_row + i_row
        m_new = jnp.maximum(b_last + m_prev, jnp.max(g_row, axis=-1, keepdims=True))
        w_c = jnp.exp(b_last + m_prev - m_new)
        vw_t = (v_aug_t * jnp.exp(g_row - m_new)).astype(BF16)
        c_ref[h] = w_c * state_t + jnp.dot(vw_t, kb, preferred_element_type=F32)
        m_ref[h] = jnp.broadcast_to(m_new, (1, LANES))


def mlstm(proj, gates_col, gates_row, bias_row, bias_col, conv_w, conv_b, gh, *, batch, seq, chunk):
    nh = MLSTM_HEADS
    dk = MLSTM_DIM
    width = nh * dk
    wb = width // LANES
    nc = seq // chunk
    body = functools.partial(_mlstm_body, chunk=chunk, dk=dk, nh=nh)

    def rows(col0):
        return pl.BlockSpec((chunk, width), lambda b, c: (b * nc + c, col0 // wb))

    return pl.pallas_call(
        body,
        grid=(batch, nc),
        in_specs=[
            rows(COL_QM), rows(COL_KM), rows(COL_VM), rows(COL_OM),
            pl.BlockSpec((chunk, LANES), lambda b, c: (b * nc + c, 0)),
            pl.BlockSpec((1, SUBLANES, chunk), lambda b, c: (b, 0, c)),
            pl.BlockSpec((1, LANES), lambda b, c: (0, 0)),
            pl.BlockSpec((SUBLANES, 1), lambda b, c: (0, 0)),
            pl.BlockSpec((CONV_WIDTH, width), lambda b, c: (0, 0)),
            pl.BlockSpec((CONV_WIDTH, width), lambda b, c: (0, 1)),
            pl.BlockSpec((1, width), lambda b, c: (0, 0)),
            pl.BlockSpec((1, width), lambda b, c: (0, 1)),
            pl.BlockSpec((1, LANES), lambda b, c: (0, 0)),
        ],
        out_specs=pl.BlockSpec((chunk, width), lambda b, c: (b * nc + c, 0)),
        out_shape=jax.ShapeDtypeStruct((batch * seq, width), BF16),
        scratch_shapes=[
            pltpu.VMEM((chunk + 2 * SUBLANES, width), F32),
            pltpu.VMEM((chunk + 2 * SUBLANES, width), F32),
            pltpu.VMEM((nh, 2 * dk, dk), F32),
            pltpu.VMEM((nh, 1, LANES), F32),
        ],
        compiler_params=_params("arbitrary", "arbitrary"),
        name="mlstm",
    )(proj, proj, proj, proj, gates_col, gates_row, bias_row, bias_col,
      conv_w, conv_w, conv_b, conv_b, gh)


def _merge_body(ya_ref, yb_ref, ga_ref, gb_ref, x_ref, woa_ref, wob_ref, wout_ref, gffn_ref, *rest, moe):
    a = jnp.dot(ya_ref[...], woa_ref[...], preferred_element_type=F32)
    b = jnp.dot(yb_ref[...], wob_ref[...], preferred_element_type=F32)
    mixed = _sigmoid(ga_ref[...].astype(F32)) * a + _sigmoid(gb_ref[...].astype(F32)) * b
    x1 = x_ref[...] + jnp.dot(mixed.astype(BF16), wout_ref[...], preferred_element_type=F32)
    hf = _rms(x1, gffn_ref[...])
    if moe:
        wr_ref, x1_ref, hf_ref, lg_ref = rest
        _store_row_tiles(hf_ref, hf)
        lg_ref[...] = jnp.dot(hf.astype(BF16), wr_ref[...].astype(BF16), preferred_element_type=F32)
    else:
        x1_ref, hf_ref = rest
        hf_ref[...] = hf.astype(BF16)
    x1_ref[...] = x1


def merge(ya, yb, proj, x2, woa, wob, wout, gffn, wr, *, tm):
    t, d = x2.shape
    moe = wr is not None
    full = lambda m: (0, 0)
    in_specs = [
        pl.BlockSpec((tm, ya.shape[1]), lambda m: (m, 0)),
        pl.BlockSpec((tm, yb.shape[1]), lambda m: (m, 0)),
        pl.BlockSpec((tm, d), lambda m: (m, COL_GA * LANES // d)),
        pl.BlockSpec((tm, d), lambda m: (m, COL_GB * LANES // d)),
        pl.BlockSpec((tm, d), lambda m: (m, 0)),
        pl.BlockSpec(woa.shape, full), pl.BlockSpec(wob.shape, full), pl.BlockSpec(wout.shape, full),
        pl.BlockSpec((1, d), full),
    ]
    args = [ya, yb, proj, proj, x2, woa, wob, wout, gffn]
    out_specs = [pl.BlockSpec((tm, d), lambda m: (m, 0)), pl.BlockSpec((tm, d), lambda m: (m, 0))]
    out_shape = [jax.ShapeDtypeStruct((t, d), F32), jax.ShapeDtypeStruct((t, d), BF16)]
    if moe:
        out_specs[1] = pl.BlockSpec((tm * SUBLANES, LANES), lambda m: (m, 0))
        out_shape[1] = jax.ShapeDtypeStruct((t * SUBLANES, LANES), F32)
    if moe:
        in_specs.append(pl.BlockSpec(wr.shape, full))
        args.append(wr)
        out_specs.append(pl.BlockSpec((tm, LANES), lambda m: (m, 0)))
        out_shape.append(jax.ShapeDtypeStruct((t, LANES), F32))
    return pl.pallas_call(
        functools.partial(_merge_body, moe=moe),
        grid=(t // tm,),
        in_specs=in_specs, out_specs=out_specs, out_shape=out_shape,
        compiler_params=_params("arbitrary"),
        name="merge_moe" if moe else "merge",
    )(*args)


def _swiglu_gate(x, w1_ref, w3_ref, g_ref, fc):
    dff = g_ref.shape[1]
    for f0 in range(0, dff, fc):
        a = jnp.dot(x, w1_ref[:, f0:f0 + fc].astype(BF16), preferred_element_type=F32)
        b = jnp.dot(x, w3_ref[:, f0:f0 + fc].astype(BF16), preferred_element_type=F32)
        g_ref[:, f0:f0 + fc] = (a * _sigmoid(a) * b).astype(BF16)


def _swiglu(x, w1_ref, w3_ref, w2_ref, g_ref, fc):
    _swiglu_gate(x, w1_ref, w3_ref, g_ref, fc)
    return jnp.dot(g_ref[...], w2_ref[...].astype(BF16), preferred_element_type=F32)


def _ple(x, p_ref, g_ref, wg_ref, wp_ref):
    gate = _sigmoid(jnp.dot(_rms(x, g_ref[...]).astype(BF16), wg_ref[...], preferred_element_type=F32))
    emb = jnp.dot(p_ref[...].astype(BF16), wp_ref[...], preferred_element_type=F32)
    return x + gate * emb


def _ffn_body(hf_ref, x1_ref, w1_ref, w3_ref, w2_ref, p_ref, g_ref, wg_ref, wp_ref, o_ref, act_ref, *, fc):
    x2 = x1_ref[...] + _swiglu(hf_ref[...], w1_ref, w3_ref, w2_ref, act_ref, fc)
    o_ref[...] = _ple(x2, p_ref, g_ref, wg_ref, wp_ref)


def dense_ffn(hf, x1, w1, w3, w2, p2, p_row0, g, wg, wp, *, tm, fc):
    t, d = x1.shape
    dff = w1.shape[1]
    p_blk0 = p_row0 // tm
    resident = dict(pipeline_mode=pl.Buffered(1))
    full = lambda m: (0, 0)
    return pl.pallas_call(
        functools.partial(_ffn_body, fc=fc),
        grid=(t // tm,),
        in_specs=[
            pl.BlockSpec((tm, d), lambda m: (m, 0)),
            pl.BlockSpec((tm, d), lambda m: (m, 0)),
            pl.BlockSpec((d, dff), full, **resident),
            pl.BlockSpec((d, dff), full, **resident),
            pl.BlockSpec((dff, d), full, **resident),
            pl.BlockSpec((tm, p2.shape[1]), lambda m: (p_blk0 + m, 0)),
            pl.BlockSpec((1, d), full),
            pl.BlockSpec(wg.shape, full, **resident),
            pl.BlockSpec(wp.shape, full, **resident),
        ],
        out_specs=pl.BlockSpec((tm, d), lambda m: (m, 0)),
        out_shape=jax.ShapeDtypeStruct((t, d), F32),
        scratch_shapes=[pltpu.VMEM((tm, dff), BF16)],
        compiler_params=_params("arbitrary"),
        name="dense_ffn",
    )(hf, x1, w1, w3, w2, p2, g, wg, wp)


META_E0, META_E1, META_G0, META_G1, META_R0, META_R1 = 0, 1, 2, 3, 4, 5


def _route_body(lg_ref, meta_ref, tab_ref, cnt_ref, carry_ref, *, tm, ne):
    @pl.when(pl.program_id(0) == 0)
    def _():
        carry_ref[...] = jnp.zeros_like(carry_ref)

    lane = lax.broadcasted_iota(jnp.int32, (tm, LANES), 1)
    lanef = lane.astype(F32)
    lg = jnp.where(lane < ne, lg_ref[...], -jnp.inf)
    m1 = jnp.max(lg, axis=-1, keepdims=True)
    e1 = jnp.min(jnp.where(lg == m1, lanef, float(LANES)), axis=-1, keepdims=True)
    lg2 = jnp.where(lanef == e1, -jnp.inf, lg)
    m2 = jnp.max(lg2, axis=-1, keepdims=True)
    e2 = jnp.min(jnp.where(lg2 == m2, lanef, float(LANES)), axis=-1, keepdims=True)
    ex = jnp.exp(m2 - m1)
    g1 = 1.0 / (1.0 + ex)
    g2 = ex / (1.0 + ex)
    onehot = jnp.where((lanef == e1) | (lanef == e2), 1.0, 0.0)
    row = lax.broadcasted_iota(jnp.int32, (tm, tm), 0)
    col = lax.broadcasted_iota(jnp.int32, (tm, tm), 1)
    before = jnp.dot(jnp.where(col < row, 1.0, 0.0).astype(BF16), onehot.astype(BF16),
                     preferred_element_type=F32) + carry_ref[...]
    r1 = jnp.sum(jnp.where(lanef == e1, before, 0.0), axis=-1, keepdims=True)
    r2 = jnp.sum(jnp.where(lanef == e2, before, 0.0), axis=-1, keepdims=True)
    carry_ref[...] += jnp.sum(onehot, axis=0, keepdims=True)
    meta = jnp.zeros((tm, LANES), F32)
    for pos, val in ((META_E0, e1), (META_E1, e2), (META_G0, g1), (META_G1, g2), (META_R0, r1), (META_R1, r2)):
        meta = jnp.where(lane == pos, val, meta)
    meta_ref[...] = meta
    tab_ref[...] = meta.T[0:SUBLANES, :]
    cnt_ref[...] = jnp.broadcast_to(carry_ref[...], cnt_ref.shape)


def route(logits, *, tm):
    t = logits.shape[0]
    return pl.pallas_call(
        functools.partial(_route_body, tm=tm, ne=N_EXPERTS),
        grid=(t // tm,),
        in_specs=[pl.BlockSpec((tm, LANES), lambda m: (m, 0))],
        out_specs=[pl.BlockSpec((tm, LANES), lambda m: (m, 0)),
                   pl.BlockSpec((SUBLANES, tm), lambda m: (0, m)),
                   pl.BlockSpec((SUBLANES, LANES), lambda m: (0, 0))],
        out_shape=[jax.ShapeDtypeStruct((t, LANES), F32), jax.ShapeDtypeStruct((SUBLANES, t), F32),
                   jax.ShapeDtypeStruct((SUBLANES, LANES), F32)],
        scratch_shapes=[pltpu.VMEM((1, LANES), F32)],
        compiler_params=_params("arbitrary"),
        name="route",
    )(logits)


def _dispatch_body(dest_ref, pe_ref, na_ref, hf_ref, xs_ref, zero_ref, sem, zsem, *, tm, topk, tm_rows, n_tok):
    base = pl.program_id(0) * tm

    @pl.when(pl.program_id(0) == 0)
    def _():
        zero_ref[...] = jnp.zeros_like(zero_ref)
        n_tiles = xs_ref.shape[0] // (tm_rows * SUBLANES)
        fills = []
        for e in range(N_EXPERTS):
            end = pe_ref[e]
            nonempty = end > (pe_ref[e - 1] if e else 0)
            fills.append((nonempty, pl.multiple_of(jnp.maximum(end - tm_rows, 0), tm_rows)))
        for tile in range(n_tiles):
            fills.append((tile >= na_ref[0], tile * tm_rows))
        for phase in ("start", "wait"):
            for cond, row0 in fills:
                @pl.when(cond)
                def _(row0=row0, phase=phase):
                    cp = pltpu.make_async_copy(
                        zero_ref, xs_ref.at[pl.ds(row0 * SUBLANES, tm_rows * SUBLANES)], zsem)
                    cp.start() if phase == "start" else cp.wait()

    def issue(r, c):
        for k in range(topk):
            d = dest_ref[k * n_tok + base + r]
            pltpu.make_async_copy(_row_tile(hf_ref, r), _row_tile(xs_ref, d), sem).start(priority=k % 2)
        return c

    lax.fori_loop(0, tm, issue, 0, unroll=ROW_DMA_UNROLL)
    for k in range(topk):
        pltpu.make_async_copy(hf_ref, xs_ref.at[pl.ds(0, tm * SUBLANES)], sem).wait()


def _row_tile(ref, r):
    return ref.at[pl.ds(pl.multiple_of(r * SUBLANES, SUBLANES), SUBLANES)]


def dispatch(dest, pad_end, n_active, hf, *, n_rows, tm, topk, tm_rows):
    t = hf.shape[0] // SUBLANES
    return pl.pallas_call(
        functools.partial(_dispatch_body, tm=tm, topk=topk, tm_rows=tm_rows, n_tok=t),
        grid_spec=pltpu.PrefetchScalarGridSpec(
            num_scalar_prefetch=3,
            grid=(t // tm,),
            in_specs=[pl.BlockSpec((tm * SUBLANES, LANES), lambda m, *_: (m, 0))],
            out_specs=pl.BlockSpec(memory_space=pl.ANY),
            scratch_shapes=[pltpu.VMEM((tm_rows * SUBLANES, LANES), F32),
                            pltpu.SemaphoreType.DMA, pltpu.SemaphoreType.DMA],
        ),
        out_shape=jax.ShapeDtypeStruct((n_rows * SUBLANES, LANES), F32),
        compiler_params=_params("arbitrary"),
        name="moe_dispatch",
    )(dest, pad_end, n_active, hf)


def _experts_body(te_ref, na_ref, xs_ref, w1_ref, w3_ref, w2_ref, y_ref, xb_ref, g_ref, acc_ref, *, fc):
    del te_ref
    i = pl.program_id(0)
    f = pl.program_id(1)
    last = pl.num_programs(1) - 1
    tm = xb_ref.shape[0]

    @pl.when(i >= na_ref[0])
    def _():
        y_ref[...] = jnp.zeros_like(y_ref)

    @pl.when(i < na_ref[0])
    def _():
        @pl.when(f == 0)
        def _():
            xb_ref[...] = _load_row_tiles(xs_ref, tm).astype(BF16)

        y = _swiglu(xb_ref[...], w1_ref.at[0], w3_ref.at[0], w2_ref.at[0], g_ref, fc)

        @pl.when(f == 0)
        def _():
            acc_ref[...] = y

        @pl.when((f > 0) & (f < last))
        def _():
            acc_ref[...] += y

        @pl.when(f == last)
        def _():
            _store_row_tiles(y_ref, acc_ref[...] + y)


def experts(tile_expert, n_active, xs, w1, w3, w2, *, tm, tf, fc):
    n_rows = xs.shape[0] // SUBLANES
    d = w1.shape[1]
    dff = w1.shape[2]
    assert dff // tf >= 2
    row_tile = lambda i, f, te, na: (jnp.minimum(i, na[0] - 1), 0)
    ftile = lambda i, f, na: jnp.where(i < na[0], f, dff // tf - 1)
    return pl.pallas_call(
        functools.partial(_experts_body, fc=fc),
        grid_spec=pltpu.PrefetchScalarGridSpec(
            num_scalar_prefetch=2,
            grid=(n_rows // tm, dff // tf),
            in_specs=[
                pl.BlockSpec((tm * SUBLANES, LANES), row_tile),
                pl.BlockSpec((1, d, tf), lambda i, f, te, na: (te[i], 0, ftile(i, f, na))),
                pl.BlockSpec((1, d, tf), lambda i, f, te, na: (te[i], 0, ftile(i, f, na))),
                pl.BlockSpec((1, tf, d), lambda i, f, te, na: (te[i], ftile(i, f, na), 0)),
            ],
            out_specs=pl.BlockSpec((tm * SUBLANES, LANES), lambda i, f, te, na: (i, 0)),
            scratch_shapes=[pltpu.VMEM((tm, d), BF16), pltpu.VMEM((tm, tf), BF16), pltpu.VMEM((tm, d), F32)],
        ),
        out_shape=jax.ShapeDtypeStruct((n_rows * SUBLANES, LANES), F32),
        compiler_params=_params("arbitrary", "arbitrary"),
        name="moe_experts",
    )(tile_expert, n_active, xs, w1, w3, w2)


def _combine_body(dest_ref, x1_ref, meta_ref, p_ref, g_ref, wg_ref, wp_ref, y_ref, o_ref, buf_ref, sem,
                  *, tm, topk):
    m = pl.program_id(0)
    n_steps = pl.num_programs(0)
    n_tok = n_steps * tm

    def start_gather(tile, slot):
        def issue(r, c):
            for k in range(topk):
                d = dest_ref[k * n_tok + tile * tm + r]
                pltpu.make_async_copy(_row_tile(y_ref, d), _row_tile(buf_ref.at[slot, k], r),
                                      sem.at[slot]).start(priority=k % 2)
            return c

        lax.fori_loop(0, tm, issue, 0, unroll=ROW_DMA_UNROLL)

    @pl.when(m == 0)
    def _():
        start_gather(0, 0)

    @pl.when(m + 1 < n_steps)
    def _():
        start_gather(m + 1, (m + 1) % 2)

    slot = m % 2
    for k in range(topk):
        pltpu.make_async_copy(y_ref.at[pl.ds(0, tm * SUBLANES)], buf_ref.at[slot, k], sem.at[slot]).wait()
    meta = meta_ref[...]
    g0 = meta[:, META_G0:META_G0 + 1]
    g1 = meta[:, META_G1:META_G1 + 1]
    x2 = x1_ref[...] + (g0 * _load_row_tiles(buf_ref.at[slot, 0], tm)
                        + g1 * _load_row_tiles(buf_ref.at[slot, 1], tm))
    o_ref[...] = _ple(x2, p_ref, g_ref, wg_ref, wp_ref)


def combine(dest, x1, meta, y, p2, p_row0, g, wg, wp, *, tm, topk):
    t, d = x1.shape
    full = lambda m, dest: (0, 0)
    p_blk0 = p_row0 // tm
    return pl.pallas_call(
        functools.partial(_combine_body, tm=tm, topk=topk),
        grid_spec=pltpu.PrefetchScalarGridSpec(
            num_scalar_prefetch=1,
            grid=(t // tm,),
            in_specs=[pl.BlockSpec((tm, d), lambda m, dest: (m, 0)),
                      pl.BlockSpec((tm, LANES), lambda m, dest: (m, 0)),
                      pl.BlockSpec((tm, p2.shape[1]), lambda m, dest: (p_blk0 + m, 0)),
                      pl.BlockSpec((1, d), full), pl.BlockSpec(wg.shape, full), pl.BlockSpec(wp.shape, full),
                      pl.BlockSpec(memory_space=pl.ANY)],
            out_specs=pl.BlockSpec((tm, d), lambda m, dest: (m, 0)),
            scratch_shapes=[pltpu.VMEM((2, topk, tm * SUBLANES, LANES), F32), pltpu.SemaphoreType.DMA((2,))],
        ),
        out_shape=jax.ShapeDtypeStruct((t, d), F32),
        compiler_params=_params("arbitrary"),
        name="moe_combine",
    )(dest, x1, meta, p2, g, wg, wp, y)


def moe_ffn(hf, x1, logits, w1, w3, w2, ple_args, *, tm_route, tm_rows, tf, tm_move):
    t, d = x1.shape
    topk = 2
    meta, tab, cnt = route(logits, tm=tm_route)
    counts = cnt[0, :N_EXPERTS].astype(jnp.int32)
    padded = ((counts + tm_rows - 1) // tm_rows) * tm_rows
    pad_end = jnp.cumsum(padded).astype(jnp.int32)
    pad_start = pad_end - padded
    eidx = tab[META_E0:META_E1 + 1].astype(jnp.int32)
    rank = tab[META_R0:META_R1 + 1].astype(jnp.int32)
    dest = rank
    for e in range(N_EXPERTS):
        dest = dest + jnp.where(eidx == e, pad_start[e], 0)
    dest = dest.reshape(topk * t)
    n_tiles = -(-(t * topk) // tm_rows) + N_EXPERTS
    tile_start = jnp.arange(n_tiles, dtype=jnp.int32) * tm_rows
    tile_expert = jnp.minimum(jnp.sum(tile_start[:, None] >= pad_end[None, :], axis=1),
                              N_EXPERTS - 1).astype(jnp.int32)
    n_active = pad_end[N_EXPERTS - 1:] // tm_rows
    xs = dispatch(dest, pad_end, n_active, hf, n_rows=n_tiles * tm_rows, tm=tm_move, topk=topk, tm_rows=tm_rows)
    y = experts(tile_expert, n_active, xs, w1, w3, w2, tm=tm_rows, tf=tf, fc=256)
    return combine(dest, x1, meta, y, *ple_args, tm=tm_move, topk=topk)


def _tile2(g):
    return jnp.concatenate([g, g]).reshape(1, 2 * g.shape[0])


def kernel(x, p, g_mix, w_in, g_q, g_k, conv_w, conv_b, b_i, b_f, g_h, w_oa, w_ob, w_out, g_ffn, w_d1, w_d3,
           w_d2, w_router, w_e1, w_e3, w_e2, g_ple, w_ple_gate, w_ple_proj):
    batch, seq, d = x.shape
    depth = w_in.shape[0]
    t = batch * seq
    nh = MLSTM_HEADS
    x2 = x.reshape(t, d)
    c_q, c_k, c_v = 0, 512, 1024
    c_qk, c_vm, c_om, c_i, c_f, c_ga, c_gb, c_end = 1536, 2560, 3072, 3584, 3588, 3592, 4616, 5640

    w_in_t = jnp.swapaxes(w_in, 1, 2).astype(BF16)
    for l in range(depth):
        w_gates = w_in_t[l, c_ga:c_end]
        proj, gif = in_proj(x2, g_mix[l].reshape(1, d), w_gates, w_in_t, l, nb_cols=c_i, if_col=c_i,
                            tm=1024, tn=512)

        ya = moba(proj, _tile2(g_q[l]), _tile2(g_k[l]), batch=batch, seq=seq)

        bias = jnp.concatenate([b_i[l], b_f[l]])
        bias_row = jnp.pad(bias, (0, LANES - 2 * nh)).reshape(1, LANES)
        bias_col = bias.reshape(2 * nh, 1)
        gates_row = gif[:, :2 * nh].reshape(batch, seq, 2 * nh).transpose(0, 2, 1)
        yb = mlstm(proj, gif, gates_row, bias_row, bias_col, conv_w[l], conv_b[l].reshape(1, -1),
                   g_h[l].reshape(1, -1), batch=batch, seq=seq, chunk=256)

        j = l // 2
        moe = l % 2 == 1
        wr = jnp.pad(w_router[j], ((0, 0), (0, LANES - N_EXPERTS))) if moe else None
        outs = merge(ya, yb, proj, x2, w_oa[l].astype(BF16), w_ob[l].astype(BF16), w_out[l].astype(BF16),
                     g_ffn[l].reshape(1, d), wr, tm=1024)
        ple_args = (p.reshape(depth * t, -1), l * t, g_ple[l].reshape(1, d), w_ple_gate[l].astype(BF16),
                    w_ple_proj[l].astype(BF16))
        if moe:
            x1, hf, logits = outs
            x2 = moe_ffn(hf, x1, logits, w_e1[j], w_e3[j].astype(BF16), w_e2[j],
                         ple_args, tm_route=512, tm_rows=512, tf=1792, tm_move=512)
        else:
            x1, hf = outs
            x2 = dense_ffn(hf, x1, w_d1[j].astype(BF16), w_d3[j].astype(BF16), w_d2[j].astype(BF16),
                           *ple_args, tm=1024, fc=256)
    return x2.reshape(batch, seq, d)
```

```python
import functools

import jax
import jax.numpy as jnp
from jax import lax
from jax.experimental import pallas as pl
from jax.experimental.pallas import tpu as pltpu

F32 = jnp.float32
BF16 = jnp.bfloat16

RMS_EPS = 1e-6
LANES = 128
SUBLANES = 8

MOBA_HEADS = 8
MOBA_HEAD_DIM = 64
MOBA_BLOCK = 256
MOBA_TOPK = 3
MLSTM_HEADS = 4
MLSTM_DIM = 128
CONV_WIDTH = 4
N_EXPERTS = 8

COL_GA, COL_GB = 0, 8
COL_QA, COL_KA, COL_VA = 16, 20, 24
COL_QM, COL_KM, COL_VM, COL_OM = 28, 32, 36, 40
N_PROJ = 44 * LANES

VMEM_LIMIT = 56 * 1024 * 1024
ROW_DMA_UNROLL = 16


def _params(*sem):
    return pltpu.CompilerParams(dimension_semantics=sem, vmem_limit_bytes=VMEM_LIMIT)


def _sigmoid(x):
    return 1.0 / (1.0 + jnp.exp(-x))


def _rms(x, g):
    return x * lax.rsqrt(jnp.mean(x * x, axis=-1, keepdims=True) + RMS_EPS) * g


def _split_bf16(x):
    hi = x.astype(BF16)
    return hi, (x - hi.astype(F32)).astype(BF16)


def _store_row_tiles(ref, x):
    ref[...] = x.reshape(x.shape[0] * SUBLANES, LANES)


def _load_row_tiles(ref, rows):
    return ref[...].reshape(rows, SUBLANES * LANES)


def _nt_dot(a, b, **kw):
    return lax.dot_general(a, b, (((1,), (1,)), ((), ())), preferred_element_type=F32, **kw)


def _in_proj_body(x_ref, g_ref, wa_ref, wb_ref, o_ref, oif_ref, h_ref, *, nb_cols, if_col, tn):
    h_ref[...] = _rms(x_ref[...], g_ref[...]).astype(BF16)
    oif_ref[...] = _nt_dot(h_ref[...], wb_ref[if_col:if_col + LANES, :])
    na = wa_ref.shape[0]
    for c0 in range(0, na + nb_cols, tn):
        w = wa_ref[c0:c0 + tn, :] if c0 < na else wb_ref[c0 - na:c0 - na + tn, :]
        o_ref[:, c0:c0 + tn] = _nt_dot(h_ref[...], w).astype(o_ref.dtype)


def in_proj(x2, g, wa, w_full, layer, *, nb_cols, if_col, tm, tn):
    t, d = x2.shape
    n = wa.shape[0] + nb_cols
    resident = dict(pipeline_mode=pl.Buffered(1))
    return pl.pallas_call(
        functools.partial(_in_proj_body, nb_cols=nb_cols, if_col=if_col, tn=tn),
        grid=(t // tm,),
        in_specs=[
            pl.BlockSpec((tm, d), lambda m: (m, 0)),
            pl.BlockSpec((1, d), lambda m: (0, 0)),
            pl.BlockSpec(wa.shape, lambda m: (0, 0), **resident),
            pl.BlockSpec((None,) + w_full.shape[1:], lambda m: (layer, 0, 0), **resident),
        ],
        out_specs=[
            pl.BlockSpec((tm, n), lambda m: (m, 0)),
            pl.BlockSpec((tm, LANES), lambda m: (m, 0)),
        ],
        out_shape=[jax.ShapeDtypeStruct((t, n), BF16), jax.ShapeDtypeStruct((t, LANES), F32)],
        scratch_shapes=[pltpu.VMEM((tm, d), BF16)],
        compiler_params=_params("arbitrary"),
        name="in_proj",
    )(x2, g, wa, w_full)


MASK_BIAS = -1e30
LOG2_E = 1.4426950408889634


def _moba_body(q_ref, k_ref, v_ref, gq_ref, gk_ref, o_ref,
               kn_ref, vt_ref, kmean_ref, qaug_ref, s_ref, m_ref, alpha_ref, acc_ref,
               *, nb, blk, dh, topk, nheads):
    i = pl.program_id(1)
    pair = 2 * blk
    lane = lax.broadcasted_iota(jnp.int32, (1, LANES), 1)
    head0 = lane < dh

    same_head = (lax.broadcasted_iota(jnp.int32, (LANES, LANES), 0) // dh
                 == lax.broadcasted_iota(jnp.int32, (LANES, LANES), 1) // dh)
    head_ones = jnp.where(same_head, 1.0, 0.0).astype(BF16)

    def head_rms(x, g):
        hi, lo = _split_bf16(x * x)
        ss = (jnp.dot(hi, head_ones, preferred_element_type=F32)
              + jnp.dot(lo, head_ones, preferred_element_type=F32))
        return x * lax.rsqrt(ss * (1.0 / dh) + RMS_EPS) * g

    @pl.when(i == 0)
    def _():
        def prep(j, c):
            r0 = pl.multiple_of(j * blk, blk)
            onehot = jnp.where(lane == dh + j, 1.0, 0.0)
            for p in range(nheads // 2):
                cols = slice(p * LANES, (p + 1) * LANES)
                kn = head_rms(k_ref[pl.ds(r0, blk), cols].astype(F32), gk_ref[...])
                for hh, kh in ((0, kn), (1, pltpu.roll(kn, dh, axis=1))):
                    h = 2 * p + hh
                    kmean_ref[h, pl.ds(j, 1), :] = jnp.mean(jnp.where(head0, kh, 0.0), axis=0, keepdims=True)
                    kn_ref[h, pl.ds(r0, blk), :] = jnp.where(head0, kh, onehot).astype(BF16)
                v_t = v_ref[pl.ds(r0, blk), cols].astype(F32).T.astype(BF16)
                for hh in range(2):
                    vt_ref[2 * p + hh, 0:dh, pl.ds(r0, blk)] = v_t[hh * dh:(hh + 1) * dh, :]
                    vt_ref[2 * p + hh, dh:, pl.ds(r0, blk)] = jnp.ones((vt_ref.shape[1] - dh, blk), BF16)
            return c

        lax.fori_loop(0, nb, prep, 0)

    jidx = lax.broadcasted_iota(jnp.int32, (nb, blk), 0)
    key_i = lax.broadcasted_iota(jnp.int32, (blk, blk), 0)
    qry_i = lax.broadcasted_iota(jnp.int32, (blk, blk), 1)
    causal = key_i <= qry_i
    r_own = pl.multiple_of(i * blk, blk)
    qk_scale = dh ** -0.5 * LOG2_E
    for p in range(nheads // 2):
        cols = slice(p * LANES, (p + 1) * LANES)
        qr_t = q_ref[:, cols].astype(F32).T
        for hh in range(2):
            h = 2 * p + hh
            q_raw = qr_t[hh * dh:(hh + 1) * dh, :]
            q_t = (q_raw * lax.rsqrt(jnp.mean(q_raw * q_raw, axis=0, keepdims=True) + RMS_EPS)
                   * gq_ref[hh * dh:(hh + 1) * dh, :])
            gate = jnp.dot(kmean_ref[h].astype(BF16),
                           jnp.concatenate([q_t, jnp.zeros((LANES - dh, blk), F32)], axis=0).astype(BF16),
                           preferred_element_type=F32)
            rank = jnp.zeros((nb, blk), F32)
            for jp in range(nb):
                row = gate[jp:jp + 1, :]
                beats = (row > gate) | ((row == gate) & (jidx > jp))
                rank = rank + jnp.where(beats, jnp.where(jp < i, 1.0, 0.0), 0.0)
            sel = (rank < topk) & (jidx < i)
            q_s = q_t * qk_scale
            pad = jnp.zeros((LANES - dh - nb, blk), F32)
            qaug_ref[h] = jnp.concatenate([q_s, jnp.where(sel, 0.0, MASK_BIAS), pad], axis=0).astype(BF16)
            qaug_own = jnp.concatenate([q_s, jnp.where(jidx == i, 0.0, MASK_BIAS), pad], axis=0).astype(BF16)
            st = jnp.dot(kn_ref[h, pl.ds(r_own, blk), :], qaug_own, preferred_element_type=F32)
            st = jnp.where(causal, st, -jnp.inf)
            s_ref[h, 0:blk, :] = st
            m_ref[h] = jnp.max(st, axis=0, keepdims=True)

    def finish_own(h):
        pr = jnp.exp2(s_ref[h, 0:blk, :] - m_ref[h]).astype(BF16)
        acc_ref[h] = jnp.dot(vt_ref[h, :, pl.ds(r_own, blk)], pr, preferred_element_type=F32)

    def score_pair(u, h):
        r0 = pl.multiple_of(u * pair, pair)
        st = jnp.dot(kn_ref[h, pl.ds(r0, pair), :], qaug_ref[h], preferred_element_type=F32)
        m_old = m_ref[h]
        m_new = jnp.maximum(m_old, jnp.max(st, axis=0, keepdims=True))
        s_ref[h] = st
        alpha_ref[h] = jnp.exp2(m_old - m_new)
        m_ref[h] = m_new

    def finish_pair(u, h):
        r0 = pl.multiple_of(u * pair, pair)
        pr = jnp.exp2(s_ref[h] - m_ref[h]).astype(BF16)
        acc_ref[h] = alpha_ref[h] * acc_ref[h] + jnp.dot(vt_ref[h, :, pl.ds(r0, pair)], pr,
                                                         preferred_element_type=F32)

    n_pairs = jnp.maximum((i + 1) // 2, 1)
    for h in range(nheads):
        finish_own(h)
        score_pair(0, h)

    def body(u, c):
        for h in range(nheads):
            finish_pair(u - 1, h)
            score_pair(u, h)
        return c

    lax.fori_loop(1, n_pairs, body, 0)
    for h in range(nheads):
        finish_pair(n_pairs - 1, h)

    for p in range(nheads // 2):
        a0 = acc_ref[2 * p]
        a1 = acc_ref[2 * p + 1]
        ot = jnp.concatenate([a0[0:dh] / a0[dh:dh + 1], a1[0:dh] / a1[dh:dh + 1]], axis=0)
        o_ref[:, p * LANES:(p + 1) * LANES] = ot.T.astype(o_ref.dtype)


def moba(proj, gq2, gk2, *, batch, seq):
    nb = seq // MOBA_BLOCK
    blk = MOBA_BLOCK
    gq_t = jnp.broadcast_to(gq2.reshape(LANES, 1), (LANES, blk))
    dh = MOBA_HEAD_DIM
    nheads = MOBA_HEADS
    width = nheads * dh
    wb = width // LANES
    assert dh + nb <= LANES and 2 * dh == LANES and nb % 2 == 0
    v_rows = dh + 2 * SUBLANES
    body = functools.partial(_moba_body, nb=nb, blk=blk, dh=dh, topk=MOBA_TOPK, nheads=nheads)
    return pl.pallas_call(
        body,
        grid=(batch, nb),
        in_specs=[
            pl.BlockSpec((blk, width), lambda b, i: (b * nb + i, COL_QA // wb)),
            pl.BlockSpec((seq, width), lambda b, i: (b, COL_KA // wb)),
            pl.BlockSpec((seq, width), lambda b, i: (b, COL_VA // wb)),
            pl.BlockSpec((LANES, blk), lambda b, i: (0, 0)),
            pl.BlockSpec((1, LANES), lambda b, i: (0, 0)),
        ],
        out_specs=pl.BlockSpec((blk, width), lambda b, i: (b * nb + i, 0)),
        out_shape=jax.ShapeDtypeStruct((batch * seq, width), BF16),
        scratch_shapes=[
            pltpu.VMEM((nheads, seq, LANES), BF16),
            pltpu.VMEM((nheads, v_rows, seq), BF16),
            pltpu.VMEM((nheads, nb, LANES), F32),
            pltpu.VMEM((nheads, LANES, blk), BF16),
            pltpu.VMEM((nheads, 2 * blk, blk), F32),
            pltpu.VMEM((nheads, 1, blk), F32),
            pltpu.VMEM((nheads, 1, blk), F32),
            pltpu.VMEM((nheads, v_rows, blk), F32),
        ],
        compiler_params=_params("arbitrary", "arbitrary"),
        name="moba",
    )(proj, proj, proj, gq_t, gk2)


def _log_sigmoid(x):
    return jnp.minimum(x, 0.0) - jnp.log(1.0 + jnp.exp(-jnp.abs(x)))


def _dot_tri(tri, x, tri_left):
    out = None
    for _ in range(3):
        piece = x.astype(BF16)
        x = x - piece.astype(F32)
        term = (jnp.dot(tri, piece, preferred_element_type=F32) if tri_left
                else jnp.dot(piece, tri, preferred_element_type=F32))
        out = term if out is None else out + term
    return out


def _mlstm_body(qr_ref, kr_ref, v_ref, og_ref, gcol_ref, grow_ref, brow_ref, bcol_ref,
                cwq_ref, cwk_ref, cbq_ref, cbk_ref, gh_ref, o_ref,
                qx_ref, kx_ref, c_ref, m_ref, *, chunk, dk, nh):
    L = chunk
    width = nh * dk

    @pl.when(pl.program_id(1) == 0)
    def _():
        qx_ref[0:SUBLANES, :] = jnp.zeros((SUBLANES, width), F32)
        kx_ref[0:SUBLANES, :] = jnp.zeros((SUBLANES, width), F32)
        c_ref[...] = jnp.zeros_like(c_ref)
        m_ref[...] = jnp.zeros_like(m_ref)

    qx_ref[SUBLANES:SUBLANES + L, :] = qr_ref[...].astype(F32)
    kx_ref[SUBLANES:SUBLANES + L, :] = kr_ref[...].astype(F32)

    def conv_silu(x_ref, w_ref, b_ref):
        acc = b_ref[...] + w_ref[0:1, :] * x_ref[pl.ds(SUBLANES - CONV_WIDTH + 1, L), :]
        for j in range(1, CONV_WIDTH):
            acc = acc + w_ref[j:j + 1, :] * x_ref[pl.ds(SUBLANES - CONV_WIDTH + 1 + j, L), :]
        return acc * _sigmoid(acc)

    q_all = conv_silu(qx_ref, cwq_ref, cbq_ref)
    k_all = conv_silu(kx_ref, cwk_ref, cbk_ref) * (dk ** -0.5)
    qx_ref[0:SUBLANES, :] = qx_ref[L:L + SUBLANES, :]
    kx_ref[0:SUBLANES, :] = kx_ref[L:L + SUBLANES, :]

    pre_col = gcol_ref[...] + brow_ref[...]
    pre_row = grow_ref[0] + bcol_ref[...]
    t_i = lax.broadcasted_iota(jnp.int32, (L, L), 0)
    s_i = lax.broadcasted_iota(jnp.int32, (L, L), 1)
    tril = s_i <= t_i
    bcum_cols = _dot_tri(jnp.where(tril, 1.0, 0.0).astype(BF16), _log_sigmoid(pre_col), True)
    bcum_rows = _dot_tri(jnp.where(t_i <= s_i, 1.0, 0.0).astype(BF16), _log_sigmoid(pre_row), False)
    src_before_out = t_i <= s_i
    v_t = v_ref[...].astype(F32).T
    ones_t = jnp.ones((dk, L), F32)

    for h in range(nh):
        cols = slice(h * dk, (h + 1) * dk)
        qb = q_all[:, cols].astype(BF16)
        kb = k_all[:, cols].astype(BF16)
        i_col = pre_col[:, h:h + 1]
        i_row = pre_row[h:h + 1, :]
        bcum_col = bcum_cols[:, nh + h:nh + h + 1]
        bcum_row = bcum_rows[nh + h:nh + h + 1, :]

        m_prev = m_ref[h, 0:1, 0:1]
        a_row = bcum_row + m_prev
        dmat_t = jnp.where(src_before_out, bcum_row + (i_col - bcum_col), -jnp.inf)
        m_t = jnp.maximum(a_row, jnp.max(dmat_t, axis=0, keepdims=True))
        sqk_t = (_nt_dot(kb, qb) * jnp.exp(dmat_t - m_t)).astype(BF16)

        v_aug_t = jnp.concatenate([v_t[cols, :], ones_t], axis=0)
        state_t = c_ref[h]
        num_aug_t = (jnp.exp(a_row - m_t) * _nt_dot(state_t.astype(BF16), qb)
                     + jnp.dot(v_aug_t.astype(BF16), sqk_t, preferred_element_type=F32))
        den = num_aug_t[dk:dk + 1, :]
        hc_t = num_aug_t[0:dk, :] / jnp.maximum(jnp.abs(den), jnp.exp(-m_t))
        hn_t = hc_t * lax.rsqrt(jnp.mean(hc_t * hc_t, axis=0, keepdims=True) + RMS_EPS)
        o_ref[:, cols] = (hn_t.T * gh_ref[...] * _sigmoid(og_ref[:, cols].astype(F32))).astype(o_ref.dtype)

        b_last = bcum_row[:, L - 1:L]
        g_row = b_last - bcum_row + i_row
        m_new = jnp.maximum(b_last + m_prev, jnp.max(g_row, axis=-1, keepdims=True))
        w_c = jnp.exp(b_last + m_prev - m_new)
        vw_t = (v_aug_t * jnp.exp(g_row - m_new)).astype(BF16)
        c_ref[h] = w_c * state_t + jnp.dot(vw_t, kb, preferred_element_type=F32)
        m_ref[h] = jnp.broadcast_to(m_new, (1, LANES))


def mlstm(proj, gates_col, gates_row, bias_row, bias_col, conv_w, conv_b, gh, *, batch, seq, chunk):
    nh = MLSTM_HEADS
    dk = MLSTM_DIM
    width = nh * dk
    wb = width // LANES
    nc = seq // chunk
    body = functools.partial(_mlstm_body, chunk=chunk, dk=dk, nh=nh)

    def rows(col0):
        return pl.BlockSpec((chunk, width), lambda b, c: (b * nc + c, col0 // wb))

    return pl.pallas_call(
        body,
        grid=(batch, nc),
        in_specs=[
            rows(COL_QM), rows(COL_KM), rows(COL_VM), rows(COL_OM),
            pl.BlockSpec((chunk, LANES), lambda b, c: (b * nc + c, 0)),
            pl.BlockSpec((1, SUBLANES, chunk), lambda b, c: (b, 0, c)),
            pl.BlockSpec((1, LANES), lambda b, c: (0, 0)),
            pl.BlockSpec((SUBLANES, 1), lambda b, c: (0, 0)),
            pl.BlockSpec((CONV_WIDTH, width), lambda b, c: (0, 0)),
            pl.BlockSpec((CONV_WIDTH, width), lambda b, c: (0, 1)),
            pl.BlockSpec((1, width), lambda b, c: (0, 0)),
            pl.BlockSpec((1, width), lambda b, c: (0, 1)),
            pl.BlockSpec((1, LANES), lambda b, c: (0, 0)),
        ],
        out_specs=pl.BlockSpec((chunk, width), lambda b, c: (b * nc + c, 0)),
        out_shape=jax.ShapeDtypeStruct((batch * seq, width), BF16),
        scratch_shapes=[
            pltpu.VMEM((chunk + 2 * SUBLANES, width), F32),
            pltpu.VMEM((chunk + 2 * SUBLANES, width), F32),
            pltpu.VMEM((nh, 2 * dk, dk), F32),
            pltpu.VMEM((nh, 1, LANES), F32),
        ],
        compiler_params=_params("arbitrary", "arbitrary"),
        name="mlstm",
    )(proj, proj, proj, proj, gates_col, gates_row, bias_row, bias_col,
      conv_w, conv_w, conv_b, conv_b, gh)


def _merge_body(ya_ref, yb_ref, ga_ref, gb_ref, x_ref, woa_ref, wob_ref, wout_ref, gffn_ref, *rest, moe):
    a = jnp.dot(ya_ref[...], woa_ref[...], preferred_element_type=F32)
    b = jnp.dot(yb_ref[...], wob_ref[...], preferred_element_type=F32)
    mixed = _sigmoid(ga_ref[...].astype(F32)) * a + _sigmoid(gb_ref[...].astype(F32)) * b
    x1 = x_ref[...] + jnp.dot(mixed.astype(BF16), wout_ref[...], preferred_element_type=F32)
    hf = _rms(x1, gffn_ref[...])
    if moe:
        wr_ref, x1_ref, hf_ref, lg_ref = rest
        _store_row_tiles(hf_ref, hf)
        lg_ref[...] = jnp.dot(hf.astype(BF16), wr_ref[...].astype(BF16), preferred_element_type=F32)
    else:
        x1_ref, hf_ref = rest
        hf_ref[...] = hf.astype(BF16)
    x1_ref[...] = x1


def merge(ya, yb, proj, x2, woa, wob, wout, gffn, wr, *, tm):
    t, d = x2.shape
    moe = wr is not None
    full = lambda m: (0, 0)
    in_specs = [
        pl.BlockSpec((tm, ya.shape[1]), lambda m: (m, 0)),
        pl.BlockSpec((tm, yb.shape[1]), lambda m: (m, 0)),
        pl.BlockSpec((tm, d), lambda m: (m, COL_GA * LANES // d)),
        pl.BlockSpec((tm, d), lambda m: (m, COL_GB * LANES // d)),
        pl.BlockSpec((tm, d), lambda m: (m, 0)),
        pl.BlockSpec(woa.shape, full), pl.BlockSpec(wob.shape, full), pl.BlockSpec(wout.shape, full),
        pl.BlockSpec((1, d), full),
    ]
    args = [ya, yb, proj, proj, x2, woa, wob, wout, gffn]
    out_specs = [pl.BlockSpec((tm, d), lambda m: (m, 0)), pl.BlockSpec((tm, d), lambda m: (m, 0))]
    out_shape = [jax.ShapeDtypeStruct((t, d), F32), jax.ShapeDtypeStruct((t, d), BF16)]
    if moe:
        out_specs[1] = pl.BlockSpec((tm * SUBLANES, LANES), lambda m: (m, 0))
        out_shape[1] = jax.ShapeDtypeStruct((t * SUBLANES, LANES), F32)
    if moe:
        in_specs.append(pl.BlockSpec(wr.shape, full))
        args.append(wr)
        out_specs.append(pl.BlockSpec((tm, LANES), lambda m: (m, 0)))
        out_shape.append(jax.ShapeDtypeStruct((t, LANES), F32))
    return pl.pallas_call(
        functools.partial(_merge_body, moe=moe),
        grid=(t // tm,),
        in_specs=in_specs, out_specs=out_specs, out_shape=out_shape,
        compiler_params=_params("arbitrary"),
        name="merge_moe" if moe else "merge",
    )(*args)


def _swiglu_gate(x, w1_ref, w3_ref, g_ref, fc):
    dff = g_ref.shape[1]
    for f0 in range(0, dff, fc):
        a = jnp.dot(x, w1_ref[:, f0:f0 + fc].astype(BF16), preferred_element_type=F32)
        b = jnp.dot(x, w3_ref[:, f0:f0 + fc].astype(BF16), preferred_element_type=F32)
        g_ref[:, f0:f0 + fc] = (a * _sigmoid(a) * b).astype(BF16)


def _swiglu(x, w1_ref, w3_ref, w2_ref, g_ref, fc):
    _swiglu_gate(x, w1_ref, w3_ref, g_ref, fc)
    return jnp.dot(g_ref[...], w2_ref[...].astype(BF16), preferred_element_type=F32)


def _ple(x, p_ref, g_ref, wg_ref, wp_ref):
    gate = _sigmoid(jnp.dot(_rms(x, g_ref[...]).astype(BF16), wg_ref[...], preferred_element_type=F32))
    emb = jnp.dot(p_ref[...].astype(BF16), wp_ref[...], preferred_element_type=F32)
    return x + gate * emb


def _ffn_body(hf_ref, x1_ref, w1_ref, w3_ref, w2_ref, p_ref, g_ref, wg_ref, wp_ref, o_ref, act_ref, *, fc):
    x2 = x1_ref[...] + _swiglu(hf_ref[...], w1_ref, w3_ref, w2_ref, act_ref, fc)
    o_ref[...] = _ple(x2, p_ref, g_ref, wg_ref, wp_ref)


def dense_ffn(hf, x1, w1, w3, w2, p2, p_row0, g, wg, wp, *, tm, fc):
    t, d = x1.shape
    dff = w1.shape[1]
    p_blk0 = p_row0 // tm
    resident = dict(pipeline_mode=pl.Buffered(1))
    full = lambda m: (0, 0)
    return pl.pallas_call(
        functools.partial(_ffn_body, fc=fc),
        grid=(t // tm,),
        in_specs=[
            pl.BlockSpec((tm, d), lambda m: (m, 0)),
            pl.BlockSpec((tm, d), lambda m: (m, 0)),
            pl.BlockSpec((d, dff), full, **resident),
            pl.BlockSpec((d, dff), full, **resident),
            pl.BlockSpec((dff, d), full, **resident),
            pl.BlockSpec((tm, p2.shape[1]), lambda m: (p_blk0 + m, 0)),
            pl.BlockSpec((1, d), full),
            pl.BlockSpec(wg.shape, full, **resident),
            pl.BlockSpec(wp.shape, full, **resident),
        ],
        out_specs=pl.BlockSpec((tm, d), lambda m: (m, 0)),
        out_shape=jax.ShapeDtypeStruct((t, d), F32),
        scratch_shapes=[pltpu.VMEM((tm, dff), BF16)],
        compiler_params=_params("arbitrary"),
        name="dense_ffn",
    )(hf, x1, w1, w3, w2, p2, g, wg, wp)


META_E0, META_E1, META_G0, META_G1, META_R0, META_R1 = 0, 1, 2, 3, 4, 5


def _route_body(lg_ref, meta_ref, tab_ref, cnt_ref, carry_ref, *, tm, ne):
    @pl.when(pl.program_id(0) == 0)
    def _():
        carry_ref[...] = jnp.zeros_like(carry_ref)

    lane = lax.broadcasted_iota(jnp.int32, (tm, LANES), 1)
    lanef = lane.astype(F32)
    lg = jnp.where(lane < ne, lg_ref[...], -jnp.inf)
    m1 = jnp.max(lg, axis=-1, keepdims=True)
    e1 = jnp.min(jnp.where(lg == m1, lanef, float(LANES)), axis=-1, keepdims=True)
    lg2 = jnp.where(lanef == e1, -jnp.inf, lg)
    m2 = jnp.max(lg2, axis=-1, keepdims=True)
    e2 = jnp.min(jnp.where(lg2 == m2, lanef, float(LANES)), axis=-1, keepdims=True)
    ex = jnp.exp(m2 - m1)
    g1 = 1.0 / (1.0 + ex)
    g2 = ex / (1.0 + ex)
    onehot = jnp.where((lanef == e1) | (lanef == e2), 1.0, 0.0)
    row = lax.broadcasted_iota(jnp.int32, (tm, tm), 0)
    col = lax.broadcasted_iota(jnp.int32, (tm, tm), 1)
    before = jnp.dot(jnp.where(col < row, 1.0, 0.0).astype(BF16), onehot.astype(BF16),
                     preferred_element_type=F32) + carry_ref[...]
    r1 = jnp.sum(jnp.where(lanef == e1, before, 0.0), axis=-1, keepdims=True)
    r2 = jnp.sum(jnp.where(lanef == e2, before, 0.0), axis=-1, keepdims=True)
    carry_ref[...] += jnp.sum(onehot, axis=0, keepdims=True)
    meta = jnp.zeros((tm, LANES), F32)
    for pos, val in ((META_E0, e1), (META_E1, e2), (META_G0, g1), (META_G1, g2), (META_R0, r1), (META_R1, r2)):
        meta = jnp.where(lane == pos, val, meta)
    meta_ref[...] = meta
    tab_ref[...] = meta.T[0:SUBLANES, :]
    cnt_ref[...] = jnp.broadcast_to(carry_ref[...], cnt_ref.shape)


def route(logits, *, tm):
    t = logits.shape[0]
    return pl.pallas_call(
        functools.partial(_route_body, tm=tm, ne=N_EXPERTS),
        grid=(t // tm,),
        in_specs=[pl.BlockSpec((tm, LANES), lambda m: (m, 0))],
        out_specs=[pl.BlockSpec((tm, LANES), lambda m: (m, 0)),
                   pl.BlockSpec((SUBLANES, tm), lambda m: (0, m)),
                   pl.BlockSpec((SUBLANES, LANES), lambda m: (0, 0))],
        out_shape=[jax.ShapeDtypeStruct((t, LANES), F32), jax.ShapeDtypeStruct((SUBLANES, t), F32),
                   jax.ShapeDtypeStruct((SUBLANES, LANES), F32)],
        scratch_shapes=[pltpu.VMEM((1, LANES), F32)],
        compiler_params=_params("arbitrary"),
        name="route",
    )(logits)


def _dispatch_body(dest_ref, pe_ref, na_ref, hf_ref, xs_ref, zero_ref, sem, zsem, *, tm, topk, tm_rows, n_tok):
    base = pl.program_id(0) * tm

    @pl.when(pl.program_id(0) == 0)
    def _():
        zero_ref[...] = jnp.zeros_like(zero_ref)
        n_tiles = xs_ref.shape[0] // (tm_rows * SUBLANES)
        fills = []
        for e in range(N_EXPERTS):
            end = pe_ref[e]
            nonempty = end > (pe_ref[e - 1] if e else 0)
            fills.append((nonempty, pl.multiple_of(jnp.maximum(end - tm_rows, 0), tm_rows)))
        for tile in range(n_tiles):
            fills.append((tile >= na_ref[0], tile * tm_rows))
        for phase in ("start", "wait"):
            for cond, row0 in fills:
                @pl.when(cond)
                def _(row0=row0, phase=phase):
                    cp = pltpu.make_async_copy(
                        zero_ref, xs_ref.at[pl.ds(row0 * SUBLANES, tm_rows * SUBLANES)], zsem)
                    cp.start() if phase == "start" else cp.wait()

    def issue(r, c):
        for k in range(topk):
            d = dest_ref[k * n_tok + base + r]
            pltpu.make_async_copy(_row_tile(hf_ref, r), _row_tile(xs_ref, d), sem).start(priority=k % 2)
        return c

    lax.fori_loop(0, tm, issue, 0, unroll=ROW_DMA_UNROLL)
    for k in range(topk):
        pltpu.make_async_copy(hf_ref, xs_ref.at[pl.ds(0, tm * SUBLANES)], sem).wait()


def _row_tile(ref, r):
    return ref.at[pl.ds(pl.multiple_of(r * SUBLANES, SUBLANES), SUBLANES)]


def dispatch(dest, pad_end, n_active, hf, *, n_rows, tm, topk, tm_rows):
    t = hf.shape[0] // SUBLANES
    return pl.pallas_call(
        functools.partial(_dispatch_body, tm=tm, topk=topk, tm_rows=tm_rows, n_tok=t),
        grid_spec=pltpu.PrefetchScalarGridSpec(
            num_scalar_prefetch=3,
            grid=(t // tm,),
            in_specs=[pl.BlockSpec((tm * SUBLANES, LANES), lambda m, *_: (m, 0))],
            out_specs=pl.BlockSpec(memory_space=pl.ANY),
            scratch_shapes=[pltpu.VMEM((tm_rows * SUBLANES, LANES), F32),
                            pltpu.SemaphoreType.DMA, pltpu.SemaphoreType.DMA],
        ),
        out_shape=jax.ShapeDtypeStruct((n_rows * SUBLANES, LANES), F32),
        compiler_params=_params("arbitrary"),
        name="moe_dispatch",
    )(dest, pad_end, n_active, hf)


def _experts_body(te_ref, na_ref, xs_ref, w1_ref, w3_ref, w2_ref, y_ref, xb_ref, g_ref, acc_ref, *, fc):
    del te_ref
    i = pl.program_id(0)
    f = pl.program_id(1)
    last = pl.num_programs(1) - 1
    tm = xb_ref.shape[0]

    @pl.when(i >= na_ref[0])
    def _():
        y_ref[...] = jnp.zeros_like(y_ref)

    @pl.when(i < na_ref[0])
    def _():
        @pl.when(f == 0)
        def _():
            xb_ref[...] = _load_row_tiles(xs_ref, tm).astype(BF16)

        y = _swiglu(xb_ref[...], w1_ref.at[0], w3_ref.at[0], w2_ref.at[0], g_ref, fc)

        @pl.when(f == 0)
        def _():
            acc_ref[...] = y

        @pl.when((f > 0) & (f < last))
        def _():
            acc_ref[...] += y

        @pl.when(f == last)
        def _():
            _store_row_tiles(y_ref, acc_ref[...] + y)


def experts(tile_expert, n_active, xs, w1, w3, w2, *, tm, tf, fc):
    n_rows = xs.shape[0] // SUBLANES
    d = w1.shape[1]
    dff = w1.shape[2]
    assert dff // tf >= 2
    row_tile = lambda i, f, te, na: (jnp.minimum(i, na[0] - 1), 0)
    ftile = lambda i, f, na: jnp.where(i < na[0], f, dff // tf - 1)
    return pl.pallas_call(
        functools.partial(_experts_body, fc=fc),
        grid_spec=pltpu.PrefetchScalarGridSpec(
            num_scalar_prefetch=2,
            grid=(n_rows // tm, dff // tf),
            in_specs=[
                pl.BlockSpec((tm * SUBLANES, LANES), row_tile),
                pl.BlockSpec((1, d, tf), lambda i, f, te, na: (te[i], 0, ftile(i, f, na))),
                pl.BlockSpec((1, d, tf), lambda i, f, te, na: (te[i], 0, ftile(i, f, na))),
                pl.BlockSpec((1, tf, d), lambda i, f, te, na: (te[i], ftile(i, f, na), 0)),
            ],
            out_specs=pl.BlockSpec((tm * SUBLANES, LANES), lambda i, f, te, na: (i, 0)),
            scratch_shapes=[pltpu.VMEM((tm, d), BF16), pltpu.VMEM((tm, tf), BF16), pltpu.VMEM((tm, d), F32)],
        ),
        out_shape=jax.ShapeDtypeStruct((n_rows * SUBLANES, LANES), F32),
        compiler_params=_params("arbitrary", "arbitrary"),
        name="moe_experts",
    )(tile_expert, n_active, xs, w1, w3, w2)


def _combine_body(dest_ref, x1_ref, meta_ref, p_ref, g_ref, wg_ref, wp_ref, y_ref, o_ref, buf_ref, sem,
                  *, tm, topk):
    m = pl.program_id(0)
    n_steps = pl.num_programs(0)
    n_tok = n_steps * tm

    def start_gather(tile, slot):
        def issue(r, c):
            for k in range(topk):
                d = dest_ref[k * n_tok + tile * tm + r]
                pltpu.make_async_copy(_row_tile(y_ref, d), _row_tile(buf_ref.at[slot, k], r),
                                      sem.at[slot]).start(priority=k % 2)
            return c

        lax.fori_loop(0, tm, issue, 0, unroll=ROW_DMA_UNROLL)

    @pl.when(m == 0)
    def _():
        start_gather(0, 0)

    @pl.when(m + 1 < n_steps)
    def _():
        start_gather(m + 1, (m + 1) % 2)

    slot = m % 2
    for k in range(topk):
        pltpu.make_async_copy(y_ref.at[pl.ds(0, tm * SUBLANES)], buf_ref.at[slot, k], sem.at[slot]).wait()
    meta = meta_ref[...]
    g0 = meta[:, META_G0:META_G0 + 1]
    g1 = meta[:, META_G1:META_G1 + 1]
    x2 = x1_ref[...] + (g0 * _load_row_tiles(buf_ref.at[slot, 0], tm)
                        + g1 * _load_row_tiles(buf_ref.at[slot, 1], tm))
    o_ref[...] = _ple(x2, p_ref, g_ref, wg_ref, wp_ref)


def combine(dest, x1, meta, y, p2, p_row0, g, wg, wp, *, tm, topk):
    t, d = x1.shape
    full = lambda m, dest: (0, 0)
    p_blk0 = p_row0 // tm
    return pl.pallas_call(
        functools.partial(_combine_body, tm=tm, topk=topk),
        grid_spec=pltpu.PrefetchScalarGridSpec(
            num_scalar_prefetch=1,
            grid=(t // tm,),
            in_specs=[pl.BlockSpec((tm, d), lambda m, dest: (m, 0)),
                      pl.BlockSpec((tm, LANES), lambda m, dest: (m, 0)),
                      pl.BlockSpec((tm, p2.shape[1]), lambda m, dest: (p_blk0 + m, 0)),
                      pl.BlockSpec((1, d), full), pl.BlockSpec(wg.shape, full), pl.BlockSpec(wp.shape, full),
                      pl.BlockSpec(memory_space=pl.ANY)],
            out_specs=pl.BlockSpec((tm, d), lambda m, dest: (m, 0)),
            scratch_shapes=[pltpu.VMEM((2, topk, tm * SUBLANES, LANES), F32), pltpu.SemaphoreType.DMA((2,))],
        ),
        out_shape=jax.ShapeDtypeStruct((t, d), F32),
        compiler_params=_params("arbitrary"),
        name="moe_combine",
    )(dest, x1, meta, p2, g, wg, wp, y)


def moe_ffn(hf, x1, logits, w1, w3, w2, ple_args, *, tm_route, tm_rows, tf, tm_move):
    t, d = x1.shape
    topk = 2
    meta, tab, cnt = route(logits, tm=tm_route)
    counts = cnt[0, :N_EXPERTS].astype(jnp.int32)
    padded = ((counts + tm_rows - 1) // tm_rows) * tm_rows
    pad_end = jnp.cumsum(padded).astype(jnp.int32)
    pad_start = pad_end - padded
    eidx = tab[META_E0:META_E1 + 1].astype(jnp.int32)
    rank = tab[META_R0:META_R1 + 1].astype(jnp.int32)
    dest = rank
    for e in range(N_EXPERTS):
        dest = dest + jnp.where(eidx == e, pad_start[e], 0)
    dest = dest.reshape(topk * t)
    n_tiles = -(-(t * topk) // tm_rows) + N_EXPERTS
    tile_start = jnp.arange(n_tiles, dtype=jnp.int32) * tm_rows
    tile_expert = jnp.minimum(jnp.sum(tile_start[:, None] >= pad_end[None, :], axis=1),
                              N_EXPERTS - 1).astype(jnp.int32)
    n_active = pad_end[N_EXPERTS - 1:] // tm_rows
    xs = dispatch(dest, pad_end, n_active, hf, n_rows=n_tiles * tm_rows, tm=tm_move, topk=topk, tm_rows=tm_rows)
    y = experts(tile_expert, n_active, xs, w1, w3, w2, tm=tm_rows, tf=tf, fc=256)
    return combine(dest, x1, meta, y, *ple_args, tm=tm_move, topk=topk)


def _tile2(g):
    return jnp.concatenate([g, g]).reshape(1, 2 * g.shape[0])


def kernel(x, p, g_mix, w_in, g_q, g_k, conv_w, conv_b, b_i, b_f, g_h, w_oa, w_ob, w_out, g_ffn, w_d1, w_d3,
           w_d2, w_router, w_e1, w_e3, w_e2, g_ple, w_ple_gate, w_ple_proj):
    batch, seq, d = x.shape
    depth = w_in.shape[0]
    t = batch * seq
    nh = MLSTM_HEADS
    x2 = x.reshape(t, d)
    c_q, c_k, c_v = 0, 512, 1024
    c_qk, c_vm, c_om, c_i, c_f, c_ga, c_gb, c_end = 1536, 2560, 3072, 3584, 3588, 3592, 4616, 5640

    w_in_t = jnp.swapaxes(w_in, 1, 2).astype(BF16)
    for l in range(depth):
        w_gates = w_in_t[l, c_ga:c_end]
        proj, gif = in_proj(x2, g_mix[l].reshape(1, d), w_gates, w_in_t, l, nb_cols=c_i, if_col=c_i,
                            tm=1024, tn=512)

        ya = moba(proj, _tile2(g_q[l]), _tile2(g_k[l]), batch=batch, seq=seq)

        bias = jnp.concatenate([b_i[l], b_f[l]])
        bias_row = jnp.pad(bias, (0, LANES - 2 * nh)).reshape(1, LANES)
        bias_col = bias.reshape(2 * nh, 1)
        gates_row = gif[:, :2 * nh].reshape(batch, seq, 2 * nh).transpose(0, 2, 1)
        yb = mlstm(proj, gif, gates_row, bias_row, bias_col, conv_w[l], conv_b[l].reshape(1, -1),
                   g_h[l].reshape(1, -1), batch=batch, seq=seq, chunk=256)

        j = l // 2
        moe = l % 2 == 1
        wr = jnp.pad(w_router[j], ((0, 0), (0, LANES - N_EXPERTS))) if moe else None
        outs = merge(ya, yb, proj, x2, w_oa[l].astype(BF16), w_ob[l].astype(BF16), w_out[l].astype(BF16),
                     g_ffn[l].reshape(1, d), wr, tm=1024)
        ple_args = (p.reshape(depth * t, -1), l * t, g_ple[l].reshape(1, d), w_ple_gate[l].astype(BF16),
                    w_ple_proj[l].astype(BF16))
        if moe:
            x1, hf, logits = outs
            x2 = moe_ffn(hf, x1, logits, w_e1[j], w_e3[j].astype(BF16), w_e2[j],
                         ple_args, tm_route=512, tm_rows=512, tf=1792, tm_move=512)
        else:
            x1, hf = outs
            x2 = dense_ffn(hf, x1, w_d1[j].astype(BF16), w_d3[j].astype(BF16), w_d2[j].astype(BF16),
                           *ple_args, tm=1024, fc=256)
    return x2.reshape(batch, seq, d)
```

```python
import functools

import jax
import jax.numpy as jnp
from jax import lax
from jax.experimental import pallas as pl
from jax.experimental.pallas import tpu as pltpu

F32 = jnp.float32
BF16 = jnp.bfloat16

RMS_EPS = 1e-6
LANES = 128
SUBLANES = 8

MOBA_HEADS = 8
MOBA_HEAD_DIM = 64
MOBA_BLOCK = 256
MOBA_TOPK = 3
MLSTM_HEADS = 4
MLSTM_DIM = 128
CONV_WIDTH = 4
N_EXPERTS = 8

COL_GA, COL_GB = 0, 8
COL_QA, COL_KA, COL_VA = 16, 20, 24
COL_QM, COL_KM, COL_VM, COL_OM = 28, 32, 36, 40
N_PROJ = 44 * LANES

VMEM_LIMIT = 56 * 1024 * 1024
EXPERTS_VMEM_LIMIT = 60 * 1024 * 1024
ROW_DMA_UNROLL = 16


def _params(*sem, vmem_limit=VMEM_LIMIT):
    return pltpu.CompilerParams(dimension_semantics=sem, vmem_limit_bytes=vmem_limit)


def _sigmoid(x):
    return 1.0 / (1.0 + jnp.exp(-x))


def _rms(x, g):
    return x * lax.rsqrt(jnp.mean(x * x, axis=-1, keepdims=True) + RMS_EPS) * g


def _split_bf16(x):
    hi = x.astype(BF16)
    return hi, (x - hi.astype(F32)).astype(BF16)


def _store_row_tiles(ref, x):
    ref[...] = x.reshape(x.shape[0] * SUBLANES, LANES)


def _load_row_tiles(ref, rows):
    return ref[...].reshape(rows, SUBLANES * LANES)


def _nt_dot(a, b, **kw):
    return lax.dot_general(a, b, (((1,), (1,)), ((), ())), preferred_element_type=F32, **kw)


def _in_proj_body(x_ref, g_ref, wa_ref, wb_ref, o_ref, oif_ref, h_ref, *, nb_cols, if_col, tn):
    h_ref[...] = _rms(x_ref[...], g_ref[...]).astype(BF16)
    oif_ref[...] = _nt_dot(h_ref[...], wb_ref[if_col:if_col + LANES, :])
    na = wa_ref.shape[0]
    for c0 in range(0, na + nb_cols, tn):
        w = wa_ref[c0:c0 + tn, :] if c0 < na else wb_ref[c0 - na:c0 - na + tn, :]
        o_ref[:, c0:c0 + tn] = _nt_dot(h_ref[...], w).astype(o_ref.dtype)


def in_proj(x2, g, wa, w_full, layer, *, nb_cols, if_col, tm, tn):
    t, d = x2.shape
    n = wa.shape[0] + nb_cols
    resident = dict(pipeline_mode=pl.Buffered(1))
    return pl.pallas_call(
        functools.partial(_in_proj_body, nb_cols=nb_cols, if_col=if_col, tn=tn),
        grid=(t // tm,),
        in_specs=[
            pl.BlockSpec((tm, d), lambda m: (m, 0)),
            pl.BlockSpec((1, d), lambda m: (0, 0)),
            pl.BlockSpec(wa.shape, lambda m: (0, 0), **resident),
            pl.BlockSpec((None,) + w_full.shape[1:], lambda m: (layer, 0, 0), **resident),
        ],
        out_specs=[
            pl.BlockSpec((tm, n), lambda m: (m, 0)),
            pl.BlockSpec((tm, LANES), lambda m: (m, 0)),
        ],
        out_shape=[jax.ShapeDtypeStruct((t, n), BF16), jax.ShapeDtypeStruct((t, LANES), F32)],
        scratch_shapes=[pltpu.VMEM((tm, d), BF16)],
        compiler_params=_params("arbitrary"),
        name="in_proj",
    )(x2, g, wa, w_full)


MASK_BIAS = -1e30
LOG2_E = 1.4426950408889634


def _moba_body(q_ref, k_ref, v_ref, gq_ref, gk_ref, o_ref,
               kn_ref, vt_ref, kmean_ref, qaug_ref, s_ref, m_ref, alpha_ref, acc_ref,
               *, nb, blk, dh, topk, nheads):
    i = pl.program_id(1)
    pair = 2 * blk
    lane = lax.broadcasted_iota(jnp.int32, (1, LANES), 1)
    head0 = lane < dh

    same_head = (lax.broadcasted_iota(jnp.int32, (LANES, LANES), 0) // dh
                 == lax.broadcasted_iota(jnp.int32, (LANES, LANES), 1) // dh)
    head_ones = jnp.where(same_head, 1.0, 0.0).astype(BF16)

    def head_rms(x, g):
        hi, lo = _split_bf16(x * x)
        ss = (jnp.dot(hi, head_ones, preferred_element_type=F32)
              + jnp.dot(lo, head_ones, preferred_element_type=F32))
        return x * lax.rsqrt(ss * (1.0 / dh) + RMS_EPS) * g

    @pl.when(i == 0)
    def _():
        def prep(j, c):
            r0 = pl.multiple_of(j * blk, blk)
            onehot = jnp.where(lane == dh + j, 1.0, 0.0)
            for p in range(nheads // 2):
                cols = slice(p * LANES, (p + 1) * LANES)
                kn = head_rms(k_ref[pl.ds(r0, blk), cols].astype(F32), gk_ref[...])
                for hh, kh in ((0, kn), (1, pltpu.roll(kn, dh, axis=1))):
                    h = 2 * p + hh
                    kmean_ref[h, pl.ds(j, 1), :] = jnp.mean(jnp.where(head0, kh, 0.0), axis=0, keepdims=True)
                    kn_ref[h, pl.ds(r0, blk), :] = jnp.where(head0, kh, onehot).astype(BF16)
                v_t = v_ref[pl.ds(r0, blk), cols].astype(F32).T.astype(BF16)
                for hh in range(2):
                    vt_ref[2 * p + hh, 0:dh, pl.ds(r0, blk)] = v_t[hh * dh:(hh + 1) * dh, :]
                    vt_ref[2 * p + hh, dh:, pl.ds(r0, blk)] = jnp.ones((vt_ref.shape[1] - dh, blk), BF16)
            return c

        lax.fori_loop(0, nb, prep, 0)

    jidx = lax.broadcasted_iota(jnp.int32, (nb, blk), 0)
    key_i = lax.broadcasted_iota(jnp.int32, (blk, blk), 0)
    qry_i = lax.broadcasted_iota(jnp.int32, (blk, blk), 1)
    causal = key_i <= qry_i
    r_own = pl.multiple_of(i * blk, blk)
    qk_scale = dh ** -0.5 * LOG2_E
    for p in range(nheads // 2):
        cols = slice(p * LANES, (p + 1) * LANES)
        qr_t = q_ref[:, cols].astype(F32).T
        for hh in range(2):
            h = 2 * p + hh
            q_raw = qr_t[hh * dh:(hh + 1) * dh, :]
            q_t = (q_raw * lax.rsqrt(jnp.mean(q_raw * q_raw, axis=0, keepdims=True) + RMS_EPS)
                   * gq_ref[hh * dh:(hh + 1) * dh, :])
            gate = jnp.dot(kmean_ref[h].astype(BF16),
                           jnp.concatenate([q_t, jnp.zeros((LANES - dh, blk), F32)], axis=0).astype(BF16),
                           preferred_element_type=F32)
            rank = jnp.zeros((nb, blk), F32)
            for jp in range(nb):
                row = gate[jp:jp + 1, :]
                beats = (row > gate) | ((row == gate) & (jidx > jp))
                rank = rank + jnp.where(beats, jnp.where(jp < i, 1.0, 0.0), 0.0)
            sel = (rank < topk) & (jidx < i)
            q_s = q_t * qk_scale
            pad = jnp.zeros((LANES - dh - nb, blk), F32)
            qaug_ref[h] = jnp.concatenate([q_s, jnp.where(sel, 0.0, MASK_BIAS), pad], axis=0).astype(BF16)
            qaug_own = jnp.concatenate([q_s, jnp.where(jidx == i, 0.0, MASK_BIAS), pad], axis=0).astype(BF16)
            st = jnp.dot(kn_ref[h, pl.ds(r_own, blk), :], qaug_own, preferred_element_type=F32)
            st = jnp.where(causal, st, -jnp.inf)
            s_ref[h, 0:blk, :] = st
            m_ref[h] = jnp.max(st, axis=0, keepdims=True)

    def finish_own(h):
        pr = jnp.exp2(s_ref[h, 0:blk, :] - m_ref[h]).astype(BF16)
        acc_ref[h] = jnp.dot(vt_ref[h, :, pl.ds(r_own, blk)], pr, preferred_element_type=F32)

    def score_pair(u, h):
        r0 = pl.multiple_of(u * pair, pair)
        st = jnp.dot(kn_ref[h, pl.ds(r0, pair), :], qaug_ref[h], preferred_element_type=F32)
        m_old = m_ref[h]
        m_new = jnp.maximum(m_old, jnp.max(st, axis=0, keepdims=True))
        s_ref[h] = st
        alpha_ref[h] = jnp.exp2(m_old - m_new)
        m_ref[h] = m_new

    def finish_pair(u, h):
        r0 = pl.multiple_of(u * pair, pair)
        pr = jnp.exp2(s_ref[h] - m_ref[h]).astype(BF16)
        acc_ref[h] = alpha_ref[h] * acc_ref[h] + jnp.dot(vt_ref[h, :, pl.ds(r0, pair)], pr,
                                                         preferred_element_type=F32)

    n_pairs = jnp.maximum((i + 1) // 2, 1)
    for h in range(nheads):
        finish_own(h)
        score_pair(0, h)

    def body(u, c):
        for h in range(nheads):
            finish_pair(u - 1, h)
            score_pair(u, h)
        return c

    lax.fori_loop(1, n_pairs, body, 0)
    for h in range(nheads):
        finish_pair(n_pairs - 1, h)

    for p in range(nheads // 2):
        a0 = acc_ref[2 * p]
        a1 = acc_ref[2 * p + 1]
        ot = jnp.concatenate([a0[0:dh] / a0[dh:dh + 1], a1[0:dh] / a1[dh:dh + 1]], axis=0)
        o_ref[:, p * LANES:(p + 1) * LANES] = ot.T.astype(o_ref.dtype)


def moba(proj, gq2, gk2, *, batch, seq):
    nb = seq // MOBA_BLOCK
    blk = MOBA_BLOCK
    gq_t = jnp.broadcast_to(gq2.reshape(LANES, 1), (LANES, blk))
    dh = MOBA_HEAD_DIM
    nheads = MOBA_HEADS
    width = nheads * dh
    wb = width // LANES
    assert dh + nb <= LANES and 2 * dh == LANES and nb % 2 == 0
    v_rows = dh + 2 * SUBLANES
    body = functools.partial(_moba_body, nb=nb, blk=blk, dh=dh, topk=MOBA_TOPK, nheads=nheads)
    return pl.pallas_call(
        body,
        grid=(batch, nb),
        in_specs=[
            pl.BlockSpec((blk, width), lambda b, i: (b * nb + i, COL_QA // wb)),
            pl.BlockSpec((seq, width), lambda b, i: (b, COL_KA // wb)),
            pl.BlockSpec((seq, width), lambda b, i: (b, COL_VA // wb)),
            pl.BlockSpec((LANES, blk), lambda b, i: (0, 0)),
            pl.BlockSpec((1, LANES), lambda b, i: (0, 0)),
        ],
        out_specs=pl.BlockSpec((blk, width), lambda b, i: (b * nb + i, 0)),
        out_shape=jax.ShapeDtypeStruct((batch * seq, width), BF16),
        scratch_shapes=[
            pltpu.VMEM((nheads, seq, LANES), BF16),
            pltpu.VMEM((nheads, v_rows, seq), BF16),
            pltpu.VMEM((nheads, nb, LANES), F32),
            pltpu.VMEM((nheads, LANES, blk), BF16),
            pltpu.VMEM((nheads, 2 * blk, blk), F32),
            pltpu.VMEM((nheads, 1, blk), F32),
            pltpu.VMEM((nheads, 1, blk), F32),
            pltpu.VMEM((nheads, v_rows, blk), F32),
        ],
        compiler_params=_params("arbitrary", "arbitrary"),
        name="moba",
    )(proj, proj, proj, gq_t, gk2)


def _log_sigmoid(x):
    return jnp.minimum(x, 0.0) - jnp.log(1.0 + jnp.exp(-jnp.abs(x)))


def _dot_tri(tri, x, tri_left):
    out = None
    for _ in range(3):
        piece = x.astype(BF16)
        x = x - piece.astype(F32)
        term = (jnp.dot(tri, piece, preferred_element_type=F32) if tri_left
                else jnp.dot(piece, tri, preferred_element_type=F32))
        out = term if out is None else out + term
    return out


def _mlstm_body(qr_ref, kr_ref, v_ref, og_ref, gcol_ref, grow_ref, brow_ref, bcol_ref,
                cwq_ref, cwk_ref, cbq_ref, cbk_ref, gh_ref, o_ref,
                qx_ref, kx_ref, c_ref, m_ref, *, chunk, dk, nh):
    L = chunk
    width = nh * dk

    @pl.when(pl.program_id(1) == 0)
    def _():
        qx_ref[0:SUBLANES, :] = jnp.zeros((SUBLANES, width), F32)
        kx_ref[0:SUBLANES, :] = jnp.zeros((SUBLANES, width), F32)
        c_ref[...] = jnp.zeros_like(c_ref)
        m_ref[...] = jnp.zeros_like(m_ref)

    qx_ref[SUBLANES:SUBLANES + L, :] = qr_ref[...].astype(F32)
    kx_ref[SUBLANES:SUBLANES + L, :] = kr_ref[...].astype(F32)

    def conv_silu(x_ref, w_ref, b_ref):
        acc = b_ref[...] + w_ref[0:1, :] * x_ref[pl.ds(SUBLANES - CONV_WIDTH + 1, L), :]
        for j in range(1, CONV_WIDTH):
            acc = acc + w_ref[j:j + 1, :] * x_ref[pl.ds(SUBLANES - CONV_WIDTH + 1 + j, L), :]
        return acc * _sigmoid(acc)

    q_all = conv_silu(qx_ref, cwq_ref, cbq_ref)
    k_all = conv_silu(kx_ref, cwk_ref, cbk_ref) * (dk ** -0.5)
    qx_ref[0:SUBLANES, :] = qx_ref[L:L + SUBLANES, :]
    kx_ref[0:SUBLANES, :] = kx_ref[L:L + SUBLANES, :]

    pre_col = gcol_ref[...] + brow_ref[...]
    pre_row = grow_ref[0] + bcol_ref[...]
    t_i = lax.broadcasted_iota(jnp.int32, (L, L), 0)
    s_i = lax.broadcasted_iota(jnp.int32, (L, L), 1)
    tril = s_i <= t_i
    bcum_cols = _dot_tri(jnp.where(tril, 1.0, 0.0).astype(BF16), _log_sigmoid(pre_col), True)
    bcum_rows = _dot_tri(jnp.where(t_i <= s_i, 1.0, 0.0).astype(BF16), _log_sigmoid(pre_row), False)
    src_before_out = t_i <= s_i
    v_t = v_ref[...].astype(F32).T
    ones_t = jnp.ones((dk, L), F32)

    for h in range(nh):
        cols = slice(h * dk, (h + 1) * dk)
        qb = q_all[:, cols].astype(BF16)
        kb = k_all[:, cols].astype(BF16)
        i_col = pre_col[:, h:h + 1]
        i_row = pre_row[h:h + 1, :]
        bcum_col = bcum_cols[:, nh + h:nh + h + 1]
        bcum_row = bcum_rows[nh + h:nh + h + 1, :]

        m_prev = m_ref[h, 0:1, 0:1]
        a_row = bcum_row + m_prev
        dmat_t = jnp.where(src_before_out, bcum_row + (i_col - bcum_col), -jnp.inf)
        m_t = jnp.maximum(a_row, jnp.max(dmat_t, axis=0, keepdims=True))
        sqk_t = (_nt_dot(kb, qb) * jnp.exp(dmat_t - m_t)).astype(BF16)

        v_aug_t = jnp.concatenate([v_t[cols, :], ones_t], axis=0)
        state_t = c_ref[h]
        num_aug_t = (jnp.exp(a_row - m_t) * _nt_dot(state_t.astype(BF16), qb)
                     + jnp.dot(v_aug_t.astype(BF16), sqk_t, preferred_element_type=F32))
        den = num_aug_t[dk:dk + 1, :]
        hc_t = num_aug_t[0:dk, :] / jnp.maximum(jnp.abs(den), jnp.exp(-m_t))
        hn_t = hc_t * lax.rsqrt(jnp.mean(hc_t * hc_t, axis=0, keepdims=True) + RMS_EPS)
        o_ref[:, cols] = (hn_t.T * gh_ref[...] * _sigmoid(og_ref[:, cols].astype(F32))).astype(o_ref.dtype)

        b_last = bcum_row[:, L - 1:L]
        g_row = b_last - bcum_row + i_row
        m_new = jnp.maximum(b_last + m_prev, jnp.max(g_row, axis=-1, keepdims=True))
        w_c = jnp.exp(b_last + m_prev - m_new)
        vw_t = (v_aug_t * jnp.exp(g_row - m_new)).astype(BF16)
        c_ref[h] = w_c * state_t + jnp.dot(vw_t, kb, preferred_element_type=F32)
        m_ref[h] = jnp.broadcast_to(m_new, (1, LANES))


def mlstm(proj, gates_col, gates_row, bias_row, bias_col, conv_w, conv_b, gh, *, batch, seq, chunk):
    nh = MLSTM_HEADS
    dk = MLSTM_DIM
    width = nh * dk
    wb = width // LANES
    nc = seq // chunk
    body = functools.partial(_mlstm_body, chunk=chunk, dk=dk, nh=nh)

    def rows(col0):
        return pl.BlockSpec((chunk, width), lambda b, c: (b * nc + c, col0 // wb))

    return pl.pallas_call(
        body,
        grid=(batch, nc),
        in_specs=[
            rows(COL_QM), rows(COL_KM), rows(COL_VM), rows(COL_OM),
            pl.BlockSpec((chunk, LANES), lambda b, c: (b * nc + c, 0)),
            pl.BlockSpec((1, SUBLANES, chunk), lambda b, c: (b, 0, c)),
            pl.BlockSpec((1, LANES), lambda b, c: (0, 0)),
            pl.BlockSpec((SUBLANES, 1), lambda b, c: (0, 0)),
            pl.BlockSpec((CONV_WIDTH, width), lambda b, c: (0, 0)),
            pl.BlockSpec((CONV_WIDTH, width), lambda b, c: (0, 1)),
            pl.BlockSpec((1, width), lambda b, c: (0, 0)),
            pl.BlockSpec((1, width), lambda b, c: (0, 1)),
            pl.BlockSpec((1, LANES), lambda b, c: (0, 0)),
        ],
        out_specs=pl.BlockSpec((chunk, width), lambda b, c: (b * nc + c, 0)),
        out_shape=jax.ShapeDtypeStruct((batch * seq, width), BF16),
        scratch_shapes=[
            pltpu.VMEM((chunk + 2 * SUBLANES, width), F32),
            pltpu.VMEM((chunk + 2 * SUBLANES, width), F32),
            pltpu.VMEM((nh, 2 * dk, dk), F32),
            pltpu.VMEM((nh, 1, LANES), F32),
        ],
        compiler_params=_params("arbitrary", "arbitrary"),
        name="mlstm",
    )(proj, proj, proj, proj, gates_col, gates_row, bias_row, bias_col,
      conv_w, conv_w, conv_b, conv_b, gh)


def _merge_body(ya_ref, yb_ref, ga_ref, gb_ref, x_ref, woa_ref, wob_ref, wout_ref, gffn_ref, *rest, moe):
    a = jnp.dot(ya_ref[...], woa_ref[...], preferred_element_type=F32)
    b = jnp.dot(yb_ref[...], wob_ref[...], preferred_element_type=F32)
    mixed = _sigmoid(ga_ref[...].astype(F32)) * a + _sigmoid(gb_ref[...].astype(F32)) * b
    x1 = x_ref[...] + jnp.dot(mixed.astype(BF16), wout_ref[...], preferred_element_type=F32)
    hf = _rms(x1, gffn_ref[...])
    if moe:
        wr_ref, x1_ref, hf_ref, lg_ref = rest
        _store_row_tiles(hf_ref, hf)
        lg_ref[...] = jnp.dot(hf.astype(BF16), wr_ref[...].astype(BF16), preferred_element_type=F32)
    else:
        x1_ref, hf_ref = rest
        hf_ref[...] = hf.astype(BF16)
    x1_ref[...] = x1


def merge(ya, yb, proj, x2, woa, wob, wout, gffn, wr, *, tm):
    t, d = x2.shape
    moe = wr is not None
    full = lambda m: (0, 0)
    in_specs = [
        pl.BlockSpec((tm, ya.shape[1]), lambda m: (m, 0)),
        pl.BlockSpec((tm, yb.shape[1]), lambda m: (m, 0)),
        pl.BlockSpec((tm, d), lambda m: (m, COL_GA * LANES // d)),
        pl.BlockSpec((tm, d), lambda m: (m, COL_GB * LANES // d)),
        pl.BlockSpec((tm, d), lambda m: (m, 0)),
        pl.BlockSpec(woa.shape, full), pl.BlockSpec(wob.shape, full), pl.BlockSpec(wout.shape, full),
        pl.BlockSpec((1, d), full),
    ]
    args = [ya, yb, proj, proj, x2, woa, wob, wout, gffn]
    out_specs = [pl.BlockSpec((tm, d), lambda m: (m, 0)), pl.BlockSpec((tm, d), lambda m: (m, 0))]
    out_shape = [jax.ShapeDtypeStruct((t, d), F32), jax.ShapeDtypeStruct((t, d), BF16)]
    if moe:
        out_specs[1] = pl.BlockSpec((tm * SUBLANES, LANES), lambda m: (m, 0))
        out_shape[1] = jax.ShapeDtypeStruct((t * SUBLANES, LANES), F32)
    if moe:
        in_specs.append(pl.BlockSpec(wr.shape, full))
        args.append(wr)
        out_specs.append(pl.BlockSpec((tm, LANES), lambda m: (m, 0)))
        out_shape.append(jax.ShapeDtypeStruct((t, LANES), F32))
    return pl.pallas_call(
        functools.partial(_merge_body, moe=moe),
        grid=(t // tm,),
        in_specs=in_specs, out_specs=out_specs, out_shape=out_shape,
        compiler_params=_params("arbitrary"),
        name="merge_moe" if moe else "merge",
    )(*args)


def _swiglu_gate(x, w1_ref, w3_ref, g_ref, fc):
    dff = g_ref.shape[1]
    for f0 in range(0, dff, fc):
        a = jnp.dot(x, w1_ref[:, f0:f0 + fc].astype(BF16), preferred_element_type=F32)
        b = jnp.dot(x, w3_ref[:, f0:f0 + fc].astype(BF16), preferred_element_type=F32)
        g_ref[:, f0:f0 + fc] = (a * _sigmoid(a) * b).astype(BF16)


def _swiglu(x, w1_ref, w3_ref, w2_ref, g_ref, fc):
    _swiglu_gate(x, w1_ref, w3_ref, g_ref, fc)
    return jnp.dot(g_ref[...], w2_ref[...].astype(BF16), preferred_element_type=F32)


def _ple(x, p_ref, g_ref, wg_ref, wp_ref):
    gate = _sigmoid(jnp.dot(_rms(x, g_ref[...]).astype(BF16), wg_ref[...], preferred_element_type=F32))
    emb = jnp.dot(p_ref[...].astype(BF16), wp_ref[...], preferred_element_type=F32)
    return x + gate * emb


def _ffn_body(hf_ref, x1_ref, w1_ref, w3_ref, w2_ref, p_ref, g_ref, wg_ref, wp_ref, o_ref, act_ref, *, fc):
    x2 = x1_ref[...] + _swiglu(hf_ref[...], w1_ref, w3_ref, w2_ref, act_ref, fc)
    o_ref[...] = _ple(x2, p_ref, g_ref, wg_ref, wp_ref)


def dense_ffn(hf, x1, w1, w3, w2, p2, p_row0, g, wg, wp, *, tm, fc):
    t, d = x1.shape
    dff = w1.shape[1]
    p_blk0 = p_row0 // tm
    resident = dict(pipeline_mode=pl.Buffered(1))
    full = lambda m: (0, 0)
    return pl.pallas_call(
        functools.partial(_ffn_body, fc=fc),
        grid=(t // tm,),
        in_specs=[
            pl.BlockSpec((tm, d), lambda m: (m, 0)),
            pl.BlockSpec((tm, d), lambda m: (m, 0)),
            pl.BlockSpec((d, dff), full, **resident),
            pl.BlockSpec((d, dff), full, **resident),
            pl.BlockSpec((dff, d), full, **resident),
            pl.BlockSpec((tm, p2.shape[1]), lambda m: (p_blk0 + m, 0)),
            pl.BlockSpec((1, d), full),
            pl.BlockSpec(wg.shape, full, **resident),
            pl.BlockSpec(wp.shape, full, **resident),
        ],
        out_specs=pl.BlockSpec((tm, d), lambda m: (m, 0)),
        out_shape=jax.ShapeDtypeStruct((t, d), F32),
        scratch_shapes=[pltpu.VMEM((tm, dff), BF16)],
        compiler_params=_params("arbitrary"),
        name="dense_ffn",
    )(hf, x1, w1, w3, w2, p2, g, wg, wp)


META_E0, META_E1, META_G0, META_G1, META_R0, META_R1 = 0, 1, 2, 3, 4, 5


def _route_body(lg_ref, meta_ref, tab_ref, cnt_ref, carry_ref, *, tm, ne):
    @pl.when(pl.program_id(0) == 0)
    def _():
        carry_ref[...] = jnp.zeros_like(carry_ref)

    lane = lax.broadcasted_iota(jnp.int32, (tm, LANES), 1)
    lanef = lane.astype(F32)
    lg = jnp.where(lane < ne, lg_ref[...], -jnp.inf)
    m1 = jnp.max(lg, axis=-1, keepdims=True)
    e1 = jnp.min(jnp.where(lg == m1, lanef, float(LANES)), axis=-1, keepdims=True)
    lg2 = jnp.where(lanef == e1, -jnp.inf, lg)
    m2 = jnp.max(lg2, axis=-1, keepdims=True)
    e2 = jnp.min(jnp.where(lg2 == m2, lanef, float(LANES)), axis=-1, keepdims=True)
    ex = jnp.exp(m2 - m1)
    g1 = 1.0 / (1.0 + ex)
    g2 = ex / (1.0 + ex)
    onehot = jnp.where((lanef == e1) | (lanef == e2), 1.0, 0.0)
    row = lax.broadcasted_iota(jnp.int32, (tm, tm), 0)
    col = lax.broadcasted_iota(jnp.int32, (tm, tm), 1)
    before = jnp.dot(jnp.where(col < row, 1.0, 0.0).astype(BF16), onehot.astype(BF16),
                     preferred_element_type=F32) + carry_ref[...]
    r1 = jnp.sum(jnp.where(lanef == e1, before, 0.0), axis=-1, keepdims=True)
    r2 = jnp.sum(jnp.where(lanef == e2, before, 0.0), axis=-1, keepdims=True)
    carry_ref[...] += jnp.sum(onehot, axis=0, keepdims=True)
    meta = jnp.zeros((tm, LANES), F32)
    for pos, val in ((META_E0, e1), (META_E1, e2), (META_G0, g1), (META_G1, g2), (META_R0, r1), (META_R1, r2)):
        meta = jnp.where(lane == pos, val, meta)
    meta_ref[...] = meta
    tab_ref[...] = meta.T[0:SUBLANES, :]
    cnt_ref[...] = jnp.broadcast_to(carry_ref[...], cnt_ref.shape)


def route(logits, *, tm):
    t = logits.shape[0]
    return pl.pallas_call(
        functools.partial(_route_body, tm=tm, ne=N_EXPERTS),
        grid=(t // tm,),
        in_specs=[pl.BlockSpec((tm, LANES), lambda m: (m, 0))],
        out_specs=[pl.BlockSpec((tm, LANES), lambda m: (m, 0)),
                   pl.BlockSpec((SUBLANES, tm), lambda m: (0, m)),
                   pl.BlockSpec((SUBLANES, LANES), lambda m: (0, 0))],
        out_shape=[jax.ShapeDtypeStruct((t, LANES), F32), jax.ShapeDtypeStruct((SUBLANES, t), F32),
                   jax.ShapeDtypeStruct((SUBLANES, LANES), F32)],
        scratch_shapes=[pltpu.VMEM((1, LANES), F32)],
        compiler_params=_params("arbitrary"),
        name="route",
    )(logits)


def _dispatch_body(dest_ref, pe_ref, na_ref, hf_ref, xs_ref, zero_ref, sem, zsem, *, tm, topk, tm_rows, n_tok):
    base = pl.program_id(0) * tm

    @pl.when(pl.program_id(0) == 0)
    def _():
        zero_ref[...] = jnp.zeros_like(zero_ref)
        n_tiles = xs_ref.shape[0] // (tm_rows * SUBLANES)
        fills = []
        for e in range(N_EXPERTS):
            end = pe_ref[e]
            nonempty = end > (pe_ref[e - 1] if e else 0)
            fills.append((nonempty, pl.multiple_of(jnp.maximum(end - tm_rows, 0), tm_rows)))
        for tile in range(n_tiles):
            fills.append((tile >= na_ref[0], tile * tm_rows))
        for phase in ("start", "wait"):
            for cond, row0 in fills:
                @pl.when(cond)
                def _(row0=row0, phase=phase):
                    cp = pltpu.make_async_copy(
                        zero_ref, xs_ref.at[pl.ds(row0 * SUBLANES, tm_rows * SUBLANES)], zsem)
                    cp.start() if phase == "start" else cp.wait()

    def issue(r, c):
        for k in range(topk):
            d = dest_ref[k * n_tok + base + r]
            pltpu.make_async_copy(_row_tile(hf_ref, r), _row_tile(xs_ref, d), sem).start(priority=k % 2)
        return c

    lax.fori_loop(0, tm, issue, 0, unroll=ROW_DMA_UNROLL)
    for k in range(topk):
        pltpu.make_async_copy(hf_ref, xs_ref.at[pl.ds(0, tm * SUBLANES)], sem).wait()


def _row_tile(ref, r):
    return ref.at[pl.ds(pl.multiple_of(r * SUBLANES, SUBLANES), SUBLANES)]


def dispatch(dest, pad_end, n_active, hf, *, n_rows, tm, topk, tm_rows):
    t = hf.shape[0] // SUBLANES
    return pl.pallas_call(
        functools.partial(_dispatch_body, tm=tm, topk=topk, tm_rows=tm_rows, n_tok=t),
        grid_spec=pltpu.PrefetchScalarGridSpec(
            num_scalar_prefetch=3,
            grid=(t // tm,),
            in_specs=[pl.BlockSpec((tm * SUBLANES, LANES), lambda m, *_: (m, 0))],
            out_specs=pl.BlockSpec(memory_space=pl.ANY),
            scratch_shapes=[pltpu.VMEM((tm_rows * SUBLANES, LANES), F32),
                            pltpu.SemaphoreType.DMA, pltpu.SemaphoreType.DMA],
        ),
        out_shape=jax.ShapeDtypeStruct((n_rows * SUBLANES, LANES), F32),
        compiler_params=_params("arbitrary"),
        name="moe_dispatch",
    )(dest, pad_end, n_active, hf)


def _experts_body(te_ref, na_ref, xs_ref, w1_hbm, w3_hbm, w2_hbm, y_ref,
                  xb_ref, g_ref, w1_buf, w3_buf, w2_buf, wsem, *, fc):
    i = pl.program_id(0)
    na = na_ref[0]
    tm = xb_ref.shape[0]
    tf = g_ref.shape[2]

    def weight_copies(tile, half):
        e = te_ref[tile]
        cols = pl.ds(half * tf, tf)
        return (pltpu.make_async_copy(w1_hbm.at[e, :, cols], w1_buf.at[half], wsem.at[half, 0]),
                pltpu.make_async_copy(w3_hbm.at[e, :, cols], w3_buf.at[half], wsem.at[half, 1]),
                pltpu.make_async_copy(w2_hbm.at[e, cols, :], w2_buf.at[half], wsem.at[half, 2]))

    def start(tile, half):
        for cp in weight_copies(tile, half):
            cp.start()

    def wait(half):
        for cp in weight_copies(0, half):
            cp.wait()

    @pl.when(i >= na)
    def _():
        y_ref[...] = jnp.zeros_like(y_ref)

    @pl.when(i < na)
    def _():
        first_of_expert = (i == 0) | (te_ref[i] != te_ref[jnp.maximum(i - 1, 0)])
        nxt = jnp.minimum(i + 1, na - 1)
        next_is_new_expert = (i + 1 < na) & (te_ref[nxt] != te_ref[i])

        @pl.when(i == 0)
        def _():
            start(0, 0)

        @pl.when(first_of_expert)
        def _():
            start(i, 1)
            wait(0)

        xb_ref[...] = _load_row_tiles(xs_ref, tm).astype(BF16)
        xb = xb_ref[...]
        _swiglu_gate(xb, w1_buf.at[0], w3_buf.at[0], g_ref.at[0], fc)
        y = jnp.dot(g_ref[0], w2_buf[0].astype(BF16), preferred_element_type=F32)
        acc = y.astype(F32)

        @pl.when(next_is_new_expert)
        def _():
            start(nxt, 0)

        @pl.when(first_of_expert)
        def _():
            wait(1)

        _swiglu_gate(xb, w1_buf.at[1], w3_buf.at[1], g_ref.at[1], fc)
        y = acc + jnp.dot(g_ref[1], w2_buf[1].astype(BF16), preferred_element_type=F32)
        _store_row_tiles(y_ref, y)


def experts(tile_expert, n_active, xs, w1, w3, w2, *, tm, fc):
    n_rows = xs.shape[0] // SUBLANES
    d = w1.shape[1]
    dff = w1.shape[2]
    tf = dff // 2
    return pl.pallas_call(
        functools.partial(_experts_body, fc=fc),
        grid_spec=pltpu.PrefetchScalarGridSpec(
            num_scalar_prefetch=2,
            grid=(n_rows // tm,),
            in_specs=[
                pl.BlockSpec((tm * SUBLANES, LANES), lambda i, te, na: (jnp.minimum(i, na[0] - 1), 0)),
                pl.BlockSpec(memory_space=pl.ANY),
                pl.BlockSpec(memory_space=pl.ANY),
                pl.BlockSpec(memory_space=pl.ANY),
            ],
            out_specs=pl.BlockSpec((tm * SUBLANES, LANES), lambda i, te, na: (i, 0)),
            scratch_shapes=[
                pltpu.VMEM((tm, d), BF16),
                pltpu.VMEM((2, tm, tf), BF16),
                pltpu.VMEM((2, d, tf), w1.dtype),
                pltpu.VMEM((2, d, tf), w3.dtype),
                pltpu.VMEM((2, tf, d), w2.dtype),
                pltpu.SemaphoreType.DMA((2, 3)),
            ],
        ),
        out_shape=jax.ShapeDtypeStruct((n_rows * SUBLANES, LANES), F32),
        compiler_params=_params("arbitrary", vmem_limit=EXPERTS_VMEM_LIMIT),
        name="moe_experts",
    )(tile_expert, n_active, xs, w1, w3, w2)


def _combine_body(dest_ref, x1_ref, meta_ref, p_ref, g_ref, wg_ref, wp_ref, y_ref, o_ref, buf_ref, sem,
                  *, tm, topk):
    m = pl.program_id(0)
    n_steps = pl.num_programs(0)
    n_tok = n_steps * tm

    def start_gather(tile, slot):
        def issue(r, c):
            for k in range(topk):
                d = dest_ref[k * n_tok + tile * tm + r]
                pltpu.make_async_copy(_row_tile(y_ref, d), _row_tile(buf_ref.at[slot, k], r),
                                      sem.at[slot]).start(priority=k % 2)
            return c

        lax.fori_loop(0, tm, issue, 0, unroll=ROW_DMA_UNROLL)

    @pl.when(m == 0)
    def _():
        start_gather(0, 0)

    @pl.when(m + 1 < n_steps)
    def _():
        start_gather(m + 1, (m + 1) % 2)

    slot = m % 2
    for k in range(topk):
        pltpu.make_async_copy(y_ref.at[pl.ds(0, tm * SUBLANES)], buf_ref.at[slot, k], sem.at[slot]).wait()
    meta = meta_ref[...]
    g0 = meta[:, META_G0:META_G0 + 1]
    g1 = meta[:, META_G1:META_G1 + 1]
    x2 = x1_ref[...] + (g0 * _load_row_tiles(buf_ref.at[slot, 0], tm)
                        + g1 * _load_row_tiles(buf_ref.at[slot, 1], tm))
    o_ref[...] = _ple(x2, p_ref, g_ref, wg_ref, wp_ref)


def combine(dest, x1, meta, y, p2, p_row0, g, wg, wp, *, tm, topk):
    t, d = x1.shape
    full = lambda m, dest: (0, 0)
    p_blk0 = p_row0 // tm
    return pl.pallas_call(
        functools.partial(_combine_body, tm=tm, topk=topk),
        grid_spec=pltpu.PrefetchScalarGridSpec(
            num_scalar_prefetch=1,
            grid=(t // tm,),
            in_specs=[pl.BlockSpec((tm, d), lambda m, dest: (m, 0)),
                      pl.BlockSpec((tm, LANES), lambda m, dest: (m, 0)),
                      pl.BlockSpec((tm, p2.shape[1]), lambda m, dest: (p_blk0 + m, 0)),
                      pl.BlockSpec((1, d), full), pl.BlockSpec(wg.shape, full), pl.BlockSpec(wp.shape, full),
                      pl.BlockSpec(memory_space=pl.ANY)],
            out_specs=pl.BlockSpec((tm, d), lambda m, dest: (m, 0)),
            scratch_shapes=[pltpu.VMEM((2, topk, tm * SUBLANES, LANES), F32), pltpu.SemaphoreType.DMA((2,))],
        ),
        out_shape=jax.ShapeDtypeStruct((t, d), F32),
        compiler_params=_params("arbitrary"),
        name="moe_combine",
    )(dest, x1, meta, p2, g, wg, wp, y)


def moe_ffn(hf, x1, logits, w1, w3, w2, ple_args, *, tm_route, tm_rows, tm_move):
    t, d = x1.shape
    topk = 2
    meta, tab, cnt = route(logits, tm=tm_route)
    counts = cnt[0, :N_EXPERTS].astype(jnp.int32)
    padded = ((counts + tm_rows - 1) // tm_rows) * tm_rows
    pad_end = jnp.cumsum(padded).astype(jnp.int32)
    pad_start = pad_end - padded
    eidx = tab[META_E0:META_E1 + 1].astype(jnp.int32)
    rank = tab[META_R0:META_R1 + 1].astype(jnp.int32)
    dest = rank
    for e in range(N_EXPERTS):
        dest = dest + jnp.where(eidx == e, pad_start[e], 0)
    dest = dest.reshape(topk * t)
    n_tiles = -(-(t * topk) // tm_rows) + N_EXPERTS
    tile_start = jnp.arange(n_tiles, dtype=jnp.int32) * tm_rows
    tile_expert = jnp.minimum(jnp.sum(tile_start[:, None] >= pad_end[None, :], axis=1),
                              N_EXPERTS - 1).astype(jnp.int32)
    n_active = pad_end[N_EXPERTS - 1:] // tm_rows
    xs = dispatch(dest, pad_end, n_active, hf, n_rows=n_tiles * tm_rows, tm=tm_move, topk=topk, tm_rows=tm_rows)
    y = experts(tile_expert, n_active, xs, w1, w3, w2, tm=tm_rows, fc=256)
    return combine(dest, x1, meta, y, *ple_args, tm=tm_move, topk=topk)


def _tile2(g):
    return jnp.concatenate([g, g]).reshape(1, 2 * g.shape[0])


def kernel(x, p, g_mix, w_in, g_q, g_k, conv_w, conv_b, b_i, b_f, g_h, w_oa, w_ob, w_out, g_ffn, w_d1, w_d3,
           w_d2, w_router, w_e1, w_e3, w_e2, g_ple, w_ple_gate, w_ple_proj):
    batch, seq, d = x.shape
    depth = w_in.shape[0]
    t = batch * seq
    nh = MLSTM_HEADS
    x2 = x.reshape(t, d)
    c_q, c_k, c_v = 0, 512, 1024
    c_qk, c_vm, c_om, c_i, c_f, c_ga, c_gb, c_end = 1536, 2560, 3072, 3584, 3588, 3592, 4616, 5640

    w_in_t = jnp.swapaxes(w_in, 1, 2).astype(BF16)
    for l in range(depth):
        w_gates = w_in_t[l, c_ga:c_end]
        proj, gif = in_proj(x2, g_mix[l].reshape(1, d), w_gates, w_in_t, l, nb_cols=c_i, if_col=c_i,
                            tm=1024, tn=512)

        ya = moba(proj, _tile2(g_q[l]), _tile2(g_k[l]), batch=batch, seq=seq)

        bias = jnp.concatenate([b_i[l], b_f[l]])
        bias_row = jnp.pad(bias, (0, LANES - 2 * nh)).reshape(1, LANES)
        bias_col = bias.reshape(2 * nh, 1)
        gates_row = gif[:, :2 * nh].reshape(batch, seq, 2 * nh).transpose(0, 2, 1)
        yb = mlstm(proj, gif, gates_row, bias_row, bias_col, conv_w[l], conv_b[l].reshape(1, -1),
                   g_h[l].reshape(1, -1), batch=batch, seq=seq, chunk=256)

        j = l // 2
        moe = l % 2 == 1
        wr = jnp.pad(w_router[j], ((0, 0), (0, LANES - N_EXPERTS))) if moe else None
        outs = merge(ya, yb, proj, x2, w_oa[l].astype(BF16), w_ob[l].astype(BF16), w_out[l].astype(BF16),
                     g_ffn[l].reshape(1, d), wr, tm=1024)
        ple_args = (p.reshape(depth * t, -1), l * t, g_ple[l].reshape(1, d), w_ple_gate[l].astype(BF16),
                    w_ple_proj[l].astype(BF16))
        if moe:
            x1, hf, logits = outs
            x2 = moe_ffn(hf, x1, logits, w_e1[j], w_e3[j].astype(BF16), w_e2[j],
                         ple_args, tm_route=512, tm_rows=512, tm_move=512)
        else:
            x1, hf = outs
            x2 = dense_ffn(hf, x1, w_d1[j].astype(BF16), w_d3[j].astype(BF16), w_d2[j].astype(BF16),
                           *ple_args, tm=1024, fc=256)
    return x2.reshape(batch, seq, d)
```

```python
import functools

import jax
import jax.numpy as jnp
from jax import lax
from jax.experimental import pallas as pl
from jax.experimental.pallas import tpu as pltpu

F32 = jnp.float32
BF16 = jnp.bfloat16

RMS_EPS = 1e-6
LANES = 128
SUBLANES = 8

MOBA_HEADS = 8
MOBA_HEAD_DIM = 64
MOBA_BLOCK = 256
MOBA_TOPK = 3
MLSTM_HEADS = 4
MLSTM_DIM = 128
CONV_WIDTH = 4
N_EXPERTS = 8

COL_GA, COL_GB = 0, 8
COL_QA, COL_KA, COL_VA = 16, 20, 24
COL_QM, COL_KM, COL_VM, COL_OM = 28, 32, 36, 40
N_PROJ = 44 * LANES

VMEM_LIMIT = 56 * 1024 * 1024
EXPERTS_VMEM_LIMIT = 60 * 1024 * 1024
ROW_DMA_UNROLL = 16


def _params(*sem, vmem_limit=VMEM_LIMIT):
    return pltpu.CompilerParams(dimension_semantics=sem, vmem_limit_bytes=vmem_limit)


def _sigmoid(x):
    return 1.0 / (1.0 + jnp.exp(-x))


def _rms(x, g):
    return x * lax.rsqrt(jnp.mean(x * x, axis=-1, keepdims=True) + RMS_EPS) * g


def _split_bf16(x):
    hi = x.astype(BF16)
    return hi, (x - hi.astype(F32)).astype(BF16)


def _store_row_tiles(ref, x):
    ref[...] = x.reshape(x.shape[0] * SUBLANES, LANES)


def _load_row_tiles(ref, rows):
    return ref[...].reshape(rows, SUBLANES * LANES)


def _nt_dot(a, b, **kw):
    return lax.dot_general(a, b, (((1,), (1,)), ((), ())), preferred_element_type=F32, **kw)


def _in_proj_body(x_ref, g_ref, wa_ref, wb_ref, o_ref, oif_ref, oif_t_ref, h_ref, *, nb_cols, if_col, tn):
    h_ref[...] = _rms(x_ref[...], g_ref[...]).astype(BF16)
    gates = _nt_dot(h_ref[...], wb_ref[if_col:if_col + LANES, :])
    oif_ref[...] = gates
    oif_t_ref[...] = gates.T[0:SUBLANES, :]
    na = wa_ref.shape[0]
    for c0 in range(0, na + nb_cols, tn):
        w = wa_ref[c0:c0 + tn, :] if c0 < na else wb_ref[c0 - na:c0 - na + tn, :]
        o_ref[:, c0:c0 + tn] = _nt_dot(h_ref[...], w).astype(o_ref.dtype)


def in_proj(x2, g, wa, w_full, layer, *, nb_cols, if_col, tm, tn):
    t, d = x2.shape
    n = wa.shape[0] + nb_cols
    resident = dict(pipeline_mode=pl.Buffered(1))
    return pl.pallas_call(
        functools.partial(_in_proj_body, nb_cols=nb_cols, if_col=if_col, tn=tn),
        grid=(t // tm,),
        in_specs=[
            pl.BlockSpec((tm, d), lambda m: (m, 0)),
            pl.BlockSpec((1, d), lambda m: (0, 0)),
            pl.BlockSpec(wa.shape, lambda m: (0, 0), **resident),
            pl.BlockSpec((None,) + w_full.shape[1:], lambda m: (layer, 0, 0), **resident),
        ],
        out_specs=[
            pl.BlockSpec((tm, n), lambda m: (m, 0)),
            pl.BlockSpec((tm, LANES), lambda m: (m, 0)),
            pl.BlockSpec((SUBLANES, tm), lambda m: (0, m)),
        ],
        out_shape=[jax.ShapeDtypeStruct((t, n), BF16), jax.ShapeDtypeStruct((t, LANES), F32),
                   jax.ShapeDtypeStruct((SUBLANES, t), F32)],
        scratch_shapes=[pltpu.VMEM((tm, d), BF16)],
        compiler_params=_params("arbitrary"),
        name="in_proj",
    )(x2, g, wa, w_full)


MASK_BIAS = -1e30
LOG2_E = 1.4426950408889634


def _moba_body(q_ref, k_ref, v_ref, gq_ref, gk_ref, o_ref,
               kn_ref, vt_ref, kmean_ref, qaug_ref, s_ref, m_ref, alpha_ref, acc_ref,
               *, nb, blk, dh, topk, nheads):
    i = pl.program_id(1)
    pair = 2 * blk
    lane = lax.broadcasted_iota(jnp.int32, (1, LANES), 1)
    head0 = lane < dh

    same_head = (lax.broadcasted_iota(jnp.int32, (LANES, LANES), 0) // dh
                 == lax.broadcasted_iota(jnp.int32, (LANES, LANES), 1) // dh)
    head_ones = jnp.where(same_head, 1.0, 0.0).astype(BF16)

    def head_rms(x, g):
        hi, lo = _split_bf16(x * x)
        ss = (jnp.dot(hi, head_ones, preferred_element_type=F32)
              + jnp.dot(lo, head_ones, preferred_element_type=F32))
        return x * lax.rsqrt(ss * (1.0 / dh) + RMS_EPS) * g

    @pl.when(i == 0)
    def _():
        def prep(j, c):
            r0 = pl.multiple_of(j * blk, blk)
            onehot = jnp.where(lane == dh + j, 1.0, 0.0)
            for p in range(nheads // 2):
                cols = slice(p * LANES, (p + 1) * LANES)
                kn = head_rms(k_ref[pl.ds(r0, blk), cols].astype(F32), gk_ref[...])
                for hh, kh in ((0, kn), (1, pltpu.roll(kn, dh, axis=1))):
                    h = 2 * p + hh
                    kmean_ref[h, pl.ds(j, 1), :] = jnp.mean(jnp.where(head0, kh, 0.0), axis=0, keepdims=True)
                    kn_ref[h, pl.ds(r0, blk), :] = jnp.where(head0, kh, onehot).astype(BF16)
                v_t = v_ref[pl.ds(r0, blk), cols].astype(F32).T.astype(BF16)
                for hh in range(2):
                    vt_ref[2 * p + hh, 0:dh, pl.ds(r0, blk)] = v_t[hh * dh:(hh + 1) * dh, :]
                    vt_ref[2 * p + hh, dh:, pl.ds(r0, blk)] = jnp.ones((vt_ref.shape[1] - dh, blk), BF16)
            return c

        lax.fori_loop(0, nb, prep, 0)

    jidx = lax.broadcasted_iota(jnp.int32, (nb, blk), 0)
    key_i = lax.broadcasted_iota(jnp.int32, (blk, blk), 0)
    qry_i = lax.broadcasted_iota(jnp.int32, (blk, blk), 1)
    causal = key_i <= qry_i
    r_own = pl.multiple_of(i * blk, blk)
    qk_scale = dh ** -0.5 * LOG2_E
    for p in range(nheads // 2):
        cols = slice(p * LANES, (p + 1) * LANES)
        qr_t = q_ref[:, cols].astype(F32).T
        for hh in range(2):
            h = 2 * p + hh
            q_raw = qr_t[hh * dh:(hh + 1) * dh, :]
            q_t = (q_raw * lax.rsqrt(jnp.mean(q_raw * q_raw, axis=0, keepdims=True) + RMS_EPS)
                   * gq_ref[hh * dh:(hh + 1) * dh, :])
            gate = jnp.dot(kmean_ref[h].astype(BF16),
                           jnp.concatenate([q_t, jnp.zeros((LANES - dh, blk), F32)], axis=0).astype(BF16),
                           preferred_element_type=F32)
            rank = jnp.zeros((nb, blk), F32)
            for jp in range(nb):
                row = gate[jp:jp + 1, :]
                beats = (row > gate) | ((row == gate) & (jidx > jp))
                rank = rank + jnp.where(beats, jnp.where(jp < i, 1.0, 0.0), 0.0)
            sel = (rank < topk) & (jidx < i)
            q_s = q_t * qk_scale
            pad = jnp.zeros((LANES - dh - nb, blk), F32)
            qaug_ref[h] = jnp.concatenate([q_s, jnp.where(sel, 0.0, MASK_BIAS), pad], axis=0).astype(BF16)
            qaug_own = jnp.concatenate([q_s, jnp.where(jidx == i, 0.0, MASK_BIAS), pad], axis=0).astype(BF16)
            st = jnp.dot(kn_ref[h, pl.ds(r_own, blk), :], qaug_own, preferred_element_type=F32)
            st = jnp.where(causal, st, -jnp.inf)
            s_ref[h, 0:blk, :] = st
            m_ref[h] = jnp.max(st, axis=0, keepdims=True)

    def finish_own(h):
        pr = jnp.exp2(s_ref[h, 0:blk, :] - m_ref[h]).astype(BF16)
        acc_ref[h] = jnp.dot(vt_ref[h, :, pl.ds(r_own, blk)], pr, preferred_element_type=F32)

    def score_pair(u, h):
        r0 = pl.multiple_of(u * pair, pair)
        st = jnp.dot(kn_ref[h, pl.ds(r0, pair), :], qaug_ref[h], preferred_element_type=F32)
        m_old = m_ref[h]
        m_new = jnp.maximum(m_old, jnp.max(st, axis=0, keepdims=True))
        s_ref[h] = st
        alpha_ref[h] = jnp.exp2(m_old - m_new)
        m_ref[h] = m_new

    def finish_pair(u, h):
        r0 = pl.multiple_of(u * pair, pair)
        pr = jnp.exp2(s_ref[h] - m_ref[h]).astype(BF16)
        acc_ref[h] = alpha_ref[h] * acc_ref[h] + jnp.dot(vt_ref[h, :, pl.ds(r0, pair)], pr,
                                                         preferred_element_type=F32)

    n_pairs = jnp.maximum((i + 1) // 2, 1)
    for h in range(nheads):
        finish_own(h)
        score_pair(0, h)

    def body(u, c):
        for h in range(nheads):
            finish_pair(u - 1, h)
            score_pair(u, h)
        return c

    lax.fori_loop(1, n_pairs, body, 0)
    for h in range(nheads):
        finish_pair(n_pairs - 1, h)

    for p in range(nheads // 2):
        a0 = acc_ref[2 * p]
        a1 = acc_ref[2 * p + 1]
        ot = jnp.concatenate([a0[0:dh] / a0[dh:dh + 1], a1[0:dh] / a1[dh:dh + 1]], axis=0)
        o_ref[:, p * LANES:(p + 1) * LANES] = ot.T.astype(o_ref.dtype)


def moba(proj, gq2, gk2, *, batch, seq):
    nb = seq // MOBA_BLOCK
    blk = MOBA_BLOCK
    gq_t = jnp.broadcast_to(gq2.reshape(LANES, 1), (LANES, blk))
    dh = MOBA_HEAD_DIM
    nheads = MOBA_HEADS
    width = nheads * dh
    wb = width // LANES
    assert dh + nb <= LANES and 2 * dh == LANES and nb % 2 == 0
    v_rows = dh + 2 * SUBLANES
    body = functools.partial(_moba_body, nb=nb, blk=blk, dh=dh, topk=MOBA_TOPK, nheads=nheads)
    return pl.pallas_call(
        body,
        grid=(batch, nb),
        in_specs=[
            pl.BlockSpec((blk, width), lambda b, i: (b * nb + i, COL_QA // wb)),
            pl.BlockSpec((seq, width), lambda b, i: (b, COL_KA // wb)),
            pl.BlockSpec((seq, width), lambda b, i: (b, COL_VA // wb)),
            pl.BlockSpec((LANES, blk), lambda b, i: (0, 0)),
            pl.BlockSpec((1, LANES), lambda b, i: (0, 0)),
        ],
        out_specs=pl.BlockSpec((blk, width), lambda b, i: (b * nb + i, 0)),
        out_shape=jax.ShapeDtypeStruct((batch * seq, width), BF16),
        scratch_shapes=[
            pltpu.VMEM((nheads, seq, LANES), BF16),
            pltpu.VMEM((nheads, v_rows, seq), BF16),
            pltpu.VMEM((nheads, nb, LANES), F32),
            pltpu.VMEM((nheads, LANES, blk), BF16),
            pltpu.VMEM((nheads, 2 * blk, blk), F32),
            pltpu.VMEM((nheads, 1, blk), F32),
            pltpu.VMEM((nheads, 1, blk), F32),
            pltpu.VMEM((nheads, v_rows, blk), F32),
        ],
        compiler_params=_params("arbitrary", "arbitrary"),
        name="moba",
    )(proj, proj, proj, gq_t, gk2)


def _log_sigmoid(x):
    return jnp.minimum(x, 0.0) - jnp.log(1.0 + jnp.exp(-jnp.abs(x)))


def _dot_tri(tri, x, tri_left):
    out = None
    for _ in range(3):
        piece = x.astype(BF16)
        x = x - piece.astype(F32)
        term = (jnp.dot(tri, piece, preferred_element_type=F32) if tri_left
                else jnp.dot(piece, tri, preferred_element_type=F32))
        out = term if out is None else out + term
    return out


def _mlstm_body(qr_ref, kr_ref, v_ref, og_ref, gcol_ref, grow_ref, brow_ref, bcol_ref,
                cwq_ref, cwk_ref, cbq_ref, cbk_ref, gh_ref, o_ref,
                qx_ref, kx_ref, c_ref, m_ref, *, chunk, dk, nh):
    L = chunk
    width = nh * dk

    @pl.when(pl.program_id(1) == 0)
    def _():
        qx_ref[0:SUBLANES, :] = jnp.zeros((SUBLANES, width), F32)
        kx_ref[0:SUBLANES, :] = jnp.zeros((SUBLANES, width), F32)
        c_ref[...] = jnp.zeros_like(c_ref)
        m_ref[...] = jnp.zeros_like(m_ref)

    qx_ref[SUBLANES:SUBLANES + L, :] = qr_ref[...].astype(F32)
    kx_ref[SUBLANES:SUBLANES + L, :] = kr_ref[...].astype(F32)

    def conv_silu(x_ref, w_ref, b_ref):
        acc = b_ref[...] + w_ref[0:1, :] * x_ref[pl.ds(SUBLANES - CONV_WIDTH + 1, L), :]
        for j in range(1, CONV_WIDTH):
            acc = acc + w_ref[j:j + 1, :] * x_ref[pl.ds(SUBLANES - CONV_WIDTH + 1 + j, L), :]
        return acc * _sigmoid(acc)

    q_all = conv_silu(qx_ref, cwq_ref, cbq_ref)
    k_all = conv_silu(kx_ref, cwk_ref, cbk_ref) * (dk ** -0.5)
    qx_ref[0:SUBLANES, :] = qx_ref[L:L + SUBLANES, :]
    kx_ref[0:SUBLANES, :] = kx_ref[L:L + SUBLANES, :]

    pre_col = gcol_ref[...] + brow_ref[...]
    pre_row = grow_ref[...] + bcol_ref[...]
    t_i = lax.broadcasted_iota(jnp.int32, (L, L), 0)
    s_i = lax.broadcasted_iota(jnp.int32, (L, L), 1)
    tril = s_i <= t_i
    bcum_cols = _dot_tri(jnp.where(tril, 1.0, 0.0).astype(BF16), _log_sigmoid(pre_col), True)
    bcum_rows = _dot_tri(jnp.where(t_i <= s_i, 1.0, 0.0).astype(BF16), _log_sigmoid(pre_row), False)
    src_before_out = t_i <= s_i
    v_t = v_ref[...].astype(F32).T
    ones_t = jnp.ones((dk, L), F32)

    for h in range(nh):
        cols = slice(h * dk, (h + 1) * dk)
        qb = q_all[:, cols].astype(BF16)
        kb = k_all[:, cols].astype(BF16)
        i_col = pre_col[:, h:h + 1]
        i_row = pre_row[h:h + 1, :]
        bcum_col = bcum_cols[:, nh + h:nh + h + 1]
        bcum_row = bcum_rows[nh + h:nh + h + 1, :]

        m_prev = m_ref[h, 0:1, 0:1]
        a_row = bcum_row + m_prev
        dmat_t = jnp.where(src_before_out, bcum_row + (i_col - bcum_col), -jnp.inf)
        m_t = jnp.maximum(a_row, jnp.max(dmat_t, axis=0, keepdims=True))
        sqk_t = (_nt_dot(kb, qb) * jnp.exp(dmat_t - m_t)).astype(BF16)

        v_aug_t = jnp.concatenate([v_t[cols, :], ones_t], axis=0)
        state_t = c_ref[h]
        num_aug_t = (jnp.exp(a_row - m_t) * _nt_dot(state_t.astype(BF16), qb)
                     + jnp.dot(v_aug_t.astype(BF16), sqk_t, preferred_element_type=F32))
        den = num_aug_t[dk:dk + 1, :]
        hc_t = num_aug_t[0:dk, :] / jnp.maximum(jnp.abs(den), jnp.exp(-m_t))
        hn_t = hc_t * lax.rsqrt(jnp.mean(hc_t * hc_t, axis=0, keepdims=True) + RMS_EPS)
        o_ref[:, cols] = (hn_t.T * gh_ref[...] * _sigmoid(og_ref[:, cols].astype(F32))).astype(o_ref.dtype)

        b_last = bcum_row[:, L - 1:L]
        g_row = b_last - bcum_row + i_row
        m_new = jnp.maximum(b_last + m_prev, jnp.max(g_row, axis=-1, keepdims=True))
        w_c = jnp.exp(b_last + m_prev - m_new)
        vw_t = (v_aug_t * jnp.exp(g_row - m_new)).astype(BF16)
        c_ref[h] = w_c * state_t + jnp.dot(vw_t, kb, preferred_element_type=F32)
        m_ref[h] = jnp.broadcast_to(m_new, (1, LANES))


def mlstm(proj, gates_col, gates_row, bias_row, bias_col, conv_w, conv_b, gh, *, batch, seq, chunk):
    nh = MLSTM_HEADS
    assert 2 * nh == SUBLANES
    dk = MLSTM_DIM
    width = nh * dk
    wb = width // LANES
    nc = seq // chunk
    body = functools.partial(_mlstm_body, chunk=chunk, dk=dk, nh=nh)

    def rows(col0):
        return pl.BlockSpec((chunk, width), lambda b, c: (b * nc + c, col0 // wb))

    return pl.pallas_call(
        body,
        grid=(batch, nc),
        in_specs=[
            rows(COL_QM), rows(COL_KM), rows(COL_VM), rows(COL_OM),
            pl.BlockSpec((chunk, LANES), lambda b, c: (b * nc + c, 0)),
            pl.BlockSpec((SUBLANES, chunk), lambda b, c: (0, b * nc + c)),
            pl.BlockSpec((1, LANES), lambda b, c: (0, 0)),
            pl.BlockSpec((SUBLANES, 1), lambda b, c: (0, 0)),
            pl.BlockSpec((CONV_WIDTH, width), lambda b, c: (0, 0)),
            pl.BlockSpec((CONV_WIDTH, width), lambda b, c: (0, 1)),
            pl.BlockSpec((1, width), lambda b, c: (0, 0)),
            pl.BlockSpec((1, width), lambda b, c: (0, 1)),
            pl.BlockSpec((1, LANES), lambda b, c: (0, 0)),
        ],
        out_specs=pl.BlockSpec((chunk, width), lambda b, c: (b * nc + c, 0)),
        out_shape=jax.ShapeDtypeStruct((batch * seq, width), BF16),
        scratch_shapes=[
            pltpu.VMEM((chunk + 2 * SUBLANES, width), F32),
            pltpu.VMEM((chunk + 2 * SUBLANES, width), F32),
            pltpu.VMEM((nh, 2 * dk, dk), F32),
            pltpu.VMEM((nh, 1, LANES), F32),
        ],
        compiler_params=_params("arbitrary", "arbitrary"),
        name="mlstm",
    )(proj, proj, proj, proj, gates_col, gates_row, bias_row, bias_col,
      conv_w, conv_w, conv_b, conv_b, gh)


def _merge_body(ya_ref, yb_ref, ga_ref, gb_ref, x_ref, woa_ref, wob_ref, wout_ref, gffn_ref, *rest, moe):
    a = jnp.dot(ya_ref[...], woa_ref[...], preferred_element_type=F32)
    b = jnp.dot(yb_ref[...], wob_ref[...], preferred_element_type=F32)
    mixed = _sigmoid(ga_ref[...].astype(F32)) * a + _sigmoid(gb_ref[...].astype(F32)) * b
    x1 = x_ref[...] + jnp.dot(mixed.astype(BF16), wout_ref[...], preferred_element_type=F32)
    hf = _rms(x1, gffn_ref[...])
    if moe:
        wr_ref, x1_ref, hf_ref, lg_ref = rest
        _store_row_tiles(hf_ref, hf)
        lg_ref[...] = jnp.dot(hf.astype(BF16), wr_ref[...].astype(BF16), preferred_element_type=F32)
    else:
        x1_ref, hf_ref = rest
        hf_ref[...] = hf.astype(BF16)
    x1_ref[...] = x1


def merge(ya, yb, proj, x2, woa, wob, wout, gffn, wr, *, tm):
    t, d = x2.shape
    moe = wr is not None
    full = lambda m: (0, 0)
    in_specs = [
        pl.BlockSpec((tm, ya.shape[1]), lambda m: (m, 0)),
        pl.BlockSpec((tm, yb.shape[1]), lambda m: (m, 0)),
        pl.BlockSpec((tm, d), lambda m: (m, COL_GA * LANES // d)),
        pl.BlockSpec((tm, d), lambda m: (m, COL_GB * LANES // d)),
        pl.BlockSpec((tm, d), lambda m: (m, 0)),
        pl.BlockSpec(woa.shape, full), pl.BlockSpec(wob.shape, full), pl.BlockSpec(wout.shape, full),
        pl.BlockSpec((1, d), full),
    ]
    args = [ya, yb, proj, proj, x2, woa, wob, wout, gffn]
    out_specs = [pl.BlockSpec((tm, d), lambda m: (m, 0)), pl.BlockSpec((tm, d), lambda m: (m, 0))]
    out_shape = [jax.ShapeDtypeStruct((t, d), F32), jax.ShapeDtypeStruct((t, d), BF16)]
    if moe:
        out_specs[1] = pl.BlockSpec((tm * SUBLANES, LANES), lambda m: (m, 0))
        out_shape[1] = jax.ShapeDtypeStruct((t * SUBLANES, LANES), F32)
    if moe:
        in_specs.append(pl.BlockSpec(wr.shape, full))
        args.append(wr)
        out_specs.append(pl.BlockSpec((tm, LANES), lambda m: (m, 0)))
        out_shape.append(jax.ShapeDtypeStruct((t, LANES), F32))
    return pl.pallas_call(
        functools.partial(_merge_body, moe=moe),
        grid=(t // tm,),
        in_specs=in_specs, out_specs=out_specs, out_shape=out_shape,
        compiler_params=_params("arbitrary"),
        name="merge_moe" if moe else "merge",
    )(*args)


def _swiglu_gate(x, w1_ref, w3_ref, g_ref, fc):
    dff = g_ref.shape[1]
    for f0 in range(0, dff, fc):
        a = jnp.dot(x, w1_ref[:, f0:f0 + fc].astype(BF16), preferred_element_type=F32)
        b = jnp.dot(x, w3_ref[:, f0:f0 + fc].astype(BF16), preferred_element_type=F32)
        g_ref[:, f0:f0 + fc] = (a * _sigmoid(a) * b).astype(BF16)


def _swiglu(x, w1_ref, w3_ref, w2_ref, g_ref, fc):
    _swiglu_gate(x, w1_ref, w3_ref, g_ref, fc)
    return jnp.dot(g_ref[...], w2_ref[...].astype(BF16), preferred_element_type=F32)


def _ple(x, p_ref, g_ref, wg_ref, wp_ref):
    gate = _sigmoid(jnp.dot(_rms(x, g_ref[...]).astype(BF16), wg_ref[...], preferred_element_type=F32))
    emb = jnp.dot(p_ref[...].astype(BF16), wp_ref[...], preferred_element_type=F32)
    return x + gate * emb


def _ffn_body(hf_ref, x1_ref, w1_ref, w3_ref, w2_ref, p_ref, g_ref, wg_ref, wp_ref, o_ref, act_ref, *, fc):
    x2 = x1_ref[...] + _swiglu(hf_ref[...], w1_ref, w3_ref, w2_ref, act_ref, fc)
    o_ref[...] = _ple(x2, p_ref, g_ref, wg_ref, wp_ref)


def dense_ffn(hf, x1, w1, w3, w2, p2, p_row0, g, wg, wp, *, tm, fc):
    t, d = x1.shape
    dff = w1.shape[1]
    p_blk0 = p_row0 // tm
    resident = dict(pipeline_mode=pl.Buffered(1))
    full = lambda m: (0, 0)
    return pl.pallas_call(
        functools.partial(_ffn_body, fc=fc),
        grid=(t // tm,),
        in_specs=[
            pl.BlockSpec((tm, d), lambda m: (m, 0)),
            pl.BlockSpec((tm, d), lambda m: (m, 0)),
            pl.BlockSpec((d, dff), full, **resident),
            pl.BlockSpec((d, dff), full, **resident),
            pl.BlockSpec((dff, d), full, **resident),
            pl.BlockSpec((tm, p2.shape[1]), lambda m: (p_blk0 + m, 0)),
            pl.BlockSpec((1, d), full),
            pl.BlockSpec(wg.shape, full, **resident),
            pl.BlockSpec(wp.shape, full, **resident),
        ],
        out_specs=pl.BlockSpec((tm, d), lambda m: (m, 0)),
        out_shape=jax.ShapeDtypeStruct((t, d), F32),
        scratch_shapes=[pltpu.VMEM((tm, dff), BF16)],
        compiler_params=_params("arbitrary"),
        name="dense_ffn",
    )(hf, x1, w1, w3, w2, p2, g, wg, wp)


META_E0, META_E1, META_G0, META_G1, META_R0, META_R1 = 0, 1, 2, 3, 4, 5


def _route_body(lg_ref, meta_ref, tab_ref, cnt_ref, carry_ref, *, tm, ne):
    @pl.when(pl.program_id(0) == 0)
    def _():
        carry_ref[...] = jnp.zeros_like(carry_ref)

    lane = lax.broadcasted_iota(jnp.int32, (tm, LANES), 1)
    lanef = lane.astype(F32)
    lg = jnp.where(lane < ne, lg_ref[...], -jnp.inf)
    m1 = jnp.max(lg, axis=-1, keepdims=True)
    e1 = jnp.min(jnp.where(lg == m1, lanef, float(LANES)), axis=-1, keepdims=True)
    lg2 = jnp.where(lanef == e1, -jnp.inf, lg)
    m2 = jnp.max(lg2, axis=-1, keepdims=True)
    e2 = jnp.min(jnp.where(lg2 == m2, lanef, float(LANES)), axis=-1, keepdims=True)
    ex = jnp.exp(m2 - m1)
    g1 = 1.0 / (1.0 + ex)
    g2 = ex / (1.0 + ex)
    onehot = jnp.where((lanef == e1) | (lanef == e2), 1.0, 0.0)
    row = lax.broadcasted_iota(jnp.int32, (tm, tm), 0)
    col = lax.broadcasted_iota(jnp.int32, (tm, tm), 1)
    before = jnp.dot(jnp.where(col < row, 1.0, 0.0).astype(BF16), onehot.astype(BF16),
                     preferred_element_type=F32) + carry_ref[...]
    r1 = jnp.sum(jnp.where(lanef == e1, before, 0.0), axis=-1, keepdims=True)
    r2 = jnp.sum(jnp.where(lanef == e2, before, 0.0), axis=-1, keepdims=True)
    carry_ref[...] += jnp.sum(onehot, axis=0, keepdims=True)
    meta = jnp.zeros((tm, LANES), F32)
    for pos, val in ((META_E0, e1), (META_E1, e2), (META_G0, g1), (META_G1, g2), (META_R0, r1), (META_R1, r2)):
        meta = jnp.where(lane == pos, val, meta)
    meta_ref[...] = meta
    tab_ref[...] = meta.T[0:SUBLANES, :]
    cnt_ref[...] = jnp.broadcast_to(carry_ref[...], cnt_ref.shape)


def route(logits, *, tm):
    t = logits.shape[0]
    return pl.pallas_call(
        functools.partial(_route_body, tm=tm, ne=N_EXPERTS),
        grid=(t // tm,),
        in_specs=[pl.BlockSpec((tm, LANES), lambda m: (m, 0))],
        out_specs=[pl.BlockSpec((tm, LANES), lambda m: (m, 0)),
                   pl.BlockSpec((SUBLANES, tm), lambda m: (0, m)),
                   pl.BlockSpec((SUBLANES, LANES), lambda m: (0, 0))],
        out_shape=[jax.ShapeDtypeStruct((t, LANES), F32), jax.ShapeDtypeStruct((SUBLANES, t), F32),
                   jax.ShapeDtypeStruct((SUBLANES, LANES), F32)],
        scratch_shapes=[pltpu.VMEM((1, LANES), F32)],
        compiler_params=_params("arbitrary"),
        name="route",
    )(logits)


def _dispatch_body(dest_ref, pe_ref, na_ref, hf_ref, wsrc_ref, xs_ref, wdst_ref, zero_ref, sem, zsem,
                   *, tm, topk, tm_rows, n_tok):
    wdst_ref[...] = wsrc_ref[...].astype(wdst_ref.dtype)

    base = pl.program_id(0) * tm

    @pl.when(pl.program_id(0) == 0)
    def _():
        zero_ref[...] = jnp.zeros_like(zero_ref)
        n_tiles = xs_ref.shape[0] // (tm_rows * SUBLANES)
        fills = []
        for e in range(N_EXPERTS):
            end = pe_ref[e]
            nonempty = end > (pe_ref[e - 1] if e else 0)
            fills.append((nonempty, pl.multiple_of(jnp.maximum(end - tm_rows, 0), tm_rows)))
        for tile in range(n_tiles):
            fills.append((tile >= na_ref[0], tile * tm_rows))
        for phase in ("start", "wait"):
            for cond, row0 in fills:
                @pl.when(cond)
                def _(row0=row0, phase=phase):
                    cp = pltpu.make_async_copy(
                        zero_ref, xs_ref.at[pl.ds(row0 * SUBLANES, tm_rows * SUBLANES)], zsem)
                    cp.start() if phase == "start" else cp.wait()

    def issue(r, c):
        for k in range(topk):
            d = dest_ref[k * n_tok + base + r]
            pltpu.make_async_copy(_row_tile(hf_ref, r), _row_tile(xs_ref, d), sem).start(priority=k % 2)
        return c

    lax.fori_loop(0, tm, issue, 0, unroll=ROW_DMA_UNROLL)
    for k in range(topk):
        pltpu.make_async_copy(hf_ref, xs_ref.at[pl.ds(0, tm * SUBLANES)], sem).wait()


def _row_tile(ref, r):
    return ref.at[pl.ds(pl.multiple_of(r * SUBLANES, SUBLANES), SUBLANES)]


def dispatch(dest, pad_end, n_active, hf, w_f32, *, n_rows, tm, topk, tm_rows):
    t = hf.shape[0] // SUBLANES
    n_steps = t // tm
    w2d = w_f32.reshape(-1, w_f32.shape[-1])
    slab = w2d.shape[0] // n_steps
    assert slab * n_steps == w2d.shape[0] and slab % (2 * SUBLANES) == 0
    xs, w_bf = pl.pallas_call(
        functools.partial(_dispatch_body, tm=tm, topk=topk, tm_rows=tm_rows, n_tok=t),
        grid_spec=pltpu.PrefetchScalarGridSpec(
            num_scalar_prefetch=3,
            grid=(n_steps,),
            in_specs=[pl.BlockSpec((tm * SUBLANES, LANES), lambda m, *_: (m, 0)),
                      pl.BlockSpec((slab, w2d.shape[1]), lambda m, *_: (m, 0))],
            out_specs=[pl.BlockSpec(memory_space=pl.ANY),
                       pl.BlockSpec((slab, w2d.shape[1]), lambda m, *_: (m, 0))],
            scratch_shapes=[pltpu.VMEM((tm_rows * SUBLANES, LANES), F32),
                            pltpu.SemaphoreType.DMA, pltpu.SemaphoreType.DMA],
        ),
        out_shape=[jax.ShapeDtypeStruct((n_rows * SUBLANES, LANES), F32),
                   jax.ShapeDtypeStruct(w2d.shape, BF16)],
        compiler_params=_params("arbitrary"),
        name="moe_dispatch",
    )(dest, pad_end, n_active, hf, w2d)
    return xs, w_bf.reshape(w_f32.shape)


def _experts_body(te_ref, na_ref, xs_ref, w1_hbm, w3_hbm, w2_hbm, y_ref,
                  xb_ref, g_ref, w1_buf, w3_buf, w2_buf, wsem, *, fc):
    i = pl.program_id(0)
    na = na_ref[0]
    tm = xb_ref.shape[0]
    tf = g_ref.shape[2]

    def weight_copies(tile, half):
        e = te_ref[tile]
        cols = pl.ds(half * tf, tf)
        return (pltpu.make_async_copy(w1_hbm.at[e, :, cols], w1_buf.at[half], wsem.at[half, 0]),
                pltpu.make_async_copy(w3_hbm.at[e, :, cols], w3_buf.at[half], wsem.at[half, 1]),
                pltpu.make_async_copy(w2_hbm.at[e, cols, :], w2_buf.at[half], wsem.at[half, 2]))

    def start(tile, half):
        for cp in weight_copies(tile, half):
            cp.start()

    def wait(half):
        for cp in weight_copies(0, half):
            cp.wait()

    @pl.when(i >= na)
    def _():
        y_ref[...] = jnp.zeros_like(y_ref)

    @pl.when(i < na)
    def _():
        first_of_expert = (i == 0) | (te_ref[i] != te_ref[jnp.maximum(i - 1, 0)])
        nxt = jnp.minimum(i + 1, na - 1)
        next_is_new_expert = (i + 1 < na) & (te_ref[nxt] != te_ref[i])

        @pl.when(i == 0)
        def _():
            start(0, 0)

        @pl.when(first_of_expert)
        def _():
            start(i, 1)
            wait(0)

        xb_ref[...] = _load_row_tiles(xs_ref, tm).astype(BF16)
        xb = xb_ref[...]
        _swiglu_gate(xb, w1_buf.at[0], w3_buf.at[0], g_ref.at[0], fc)
        acc = jnp.dot(g_ref[0], w2_buf[0].astype(BF16), preferred_element_type=F32)

        @pl.when(next_is_new_expert)
        def _():
            start(nxt, 0)

        @pl.when(first_of_expert)
        def _():
            wait(1)

        _swiglu_gate(xb, w1_buf.at[1], w3_buf.at[1], g_ref.at[1], fc)
        y = acc + jnp.dot(g_ref[1], w2_buf[1].astype(BF16), preferred_element_type=F32)
        _store_row_tiles(y_ref, y)


def experts(tile_expert, n_active, xs, w1, w3, w2, *, tm, fc):
    n_rows = xs.shape[0] // SUBLANES
    d = w1.shape[1]
    dff = w1.shape[2]
    tf = dff // 2
    return pl.pallas_call(
        functools.partial(_experts_body, fc=fc),
        grid_spec=pltpu.PrefetchScalarGridSpec(
            num_scalar_prefetch=2,
            grid=(n_rows // tm,),
            in_specs=[
                pl.BlockSpec((tm * SUBLANES, LANES), lambda i, te, na: (jnp.minimum(i, na[0] - 1), 0)),
                pl.BlockSpec(memory_space=pl.ANY),
                pl.BlockSpec(memory_space=pl.ANY),
                pl.BlockSpec(memory_space=pl.ANY),
            ],
            out_specs=pl.BlockSpec((tm * SUBLANES, LANES), lambda i, te, na: (i, 0)),
            scratch_shapes=[
                pltpu.VMEM((tm, d), BF16),
                pltpu.VMEM((2, tm, tf), BF16),
                pltpu.VMEM((2, d, tf), w1.dtype),
                pltpu.VMEM((2, d, tf), w3.dtype),
                pltpu.VMEM((2, tf, d), w2.dtype),
                pltpu.SemaphoreType.DMA((2, 3)),
            ],
        ),
        out_shape=jax.ShapeDtypeStruct((n_rows * SUBLANES, LANES), F32),
        compiler_params=_params("arbitrary", vmem_limit=EXPERTS_VMEM_LIMIT),
        name="moe_experts",
    )(tile_expert, n_active, xs, w1, w3, w2)


def _combine_body(dest_ref, x1_ref, meta_ref, p_ref, g_ref, wg_ref, wp_ref, y_ref, o_ref, buf_ref, sem,
                  *, tm, topk):
    m = pl.program_id(0)
    n_steps = pl.num_programs(0)
    n_tok = n_steps * tm

    def start_gather(tile, slot):
        def issue(r, c):
            for k in range(topk):
                d = dest_ref[k * n_tok + tile * tm + r]
                pltpu.make_async_copy(_row_tile(y_ref, d), _row_tile(buf_ref.at[slot, k], r),
                                      sem.at[slot]).start(priority=k % 2)
            return c

        lax.fori_loop(0, tm, issue, 0, unroll=ROW_DMA_UNROLL)

    @pl.when(m == 0)
    def _():
        start_gather(0, 0)

    @pl.when(m + 1 < n_steps)
    def _():
        start_gather(m + 1, (m + 1) % 2)

    slot = m % 2
    for k in range(topk):
        pltpu.make_async_copy(y_ref.at[pl.ds(0, tm * SUBLANES)], buf_ref.at[slot, k], sem.at[slot]).wait()
    meta = meta_ref[...]
    g0 = meta[:, META_G0:META_G0 + 1]
    g1 = meta[:, META_G1:META_G1 + 1]
    x2 = x1_ref[...] + (g0 * _load_row_tiles(buf_ref.at[slot, 0], tm)
                        + g1 * _load_row_tiles(buf_ref.at[slot, 1], tm))
    o_ref[...] = _ple(x2, p_ref, g_ref, wg_ref, wp_ref)


def combine(dest, x1, meta, y, p2, p_row0, g, wg, wp, *, tm, topk):
    t, d = x1.shape
    full = lambda m, dest: (0, 0)
    p_blk0 = p_row0 // tm
    return pl.pallas_call(
        functools.partial(_combine_body, tm=tm, topk=topk),
        grid_spec=pltpu.PrefetchScalarGridSpec(
            num_scalar_prefetch=1,
            grid=(t // tm,),
            in_specs=[pl.BlockSpec((tm, d), lambda m, dest: (m, 0)),
                      pl.BlockSpec((tm, LANES), lambda m, dest: (m, 0)),
                      pl.BlockSpec((tm, p2.shape[1]), lambda m, dest: (p_blk0 + m, 0)),
                      pl.BlockSpec((1, d), full), pl.BlockSpec(wg.shape, full), pl.BlockSpec(wp.shape, full),
                      pl.BlockSpec(memory_space=pl.ANY)],
            out_specs=pl.BlockSpec((tm, d), lambda m, dest: (m, 0)),
            scratch_shapes=[pltpu.VMEM((2, topk, tm * SUBLANES, LANES), F32), pltpu.SemaphoreType.DMA((2,))],
        ),
        out_shape=jax.ShapeDtypeStruct((t, d), F32),
        compiler_params=_params("arbitrary"),
        name="moe_combine",
    )(dest, x1, meta, p2, g, wg, wp, y)


def moe_ffn(hf, x1, logits, w1, w3, w2, ple_args, *, tm_route, tm_rows, tm_dispatch, tm_combine):
    t, d = x1.shape
    topk = 2
    meta, tab, cnt = route(logits, tm=tm_route)
    counts = cnt[0, :N_EXPERTS].astype(jnp.int32)
    padded = ((counts + tm_rows - 1) // tm_rows) * tm_rows
    pad_end = jnp.cumsum(padded).astype(jnp.int32)
    pad_start = pad_end - padded
    eidx = tab[META_E0:META_E1 + 1].astype(jnp.int32)
    rank = tab[META_R0:META_R1 + 1].astype(jnp.int32)
    dest = rank
    for e in range(N_EXPERTS):
        dest = dest + jnp.where(eidx == e, pad_start[e], 0)
    dest = dest.reshape(topk * t)
    n_tiles = -(-(t * topk) // tm_rows) + N_EXPERTS
    tile_start = jnp.arange(n_tiles, dtype=jnp.int32) * tm_rows
    tile_expert = jnp.minimum(jnp.sum(tile_start[:, None] >= pad_end[None, :], axis=1),
                              N_EXPERTS - 1).astype(jnp.int32)
    n_active = pad_end[N_EXPERTS - 1:] // tm_rows
    xs, w3_bf = dispatch(dest, pad_end, n_active, hf, w3, n_rows=n_tiles * tm_rows, tm=tm_dispatch, topk=topk,
                         tm_rows=tm_rows)
    y = experts(tile_expert, n_active, xs, w1, w3_bf, w2, tm=tm_rows, fc=256)
    return combine(dest, x1, meta, y, *ple_args, tm=tm_combine, topk=topk)


def _tile2(g):
    return jnp.concatenate([g, g]).reshape(1, 2 * g.shape[0])


def kernel(x, p, g_mix, w_in, g_q, g_k, conv_w, conv_b, b_i, b_f, g_h, w_oa, w_ob, w_out, g_ffn, w_d1, w_d3,
           w_d2, w_router, w_e1, w_e3, w_e2, g_ple, w_ple_gate, w_ple_proj):
    batch, seq, d = x.shape
    depth = w_in.shape[0]
    t = batch * seq
    nh = MLSTM_HEADS
    x2 = x.reshape(t, d)
    c_q, c_k, c_v = 0, 512, 1024
    c_qk, c_vm, c_om, c_i, c_f, c_ga, c_gb, c_end = 1536, 2560, 3072, 3584, 3588, 3592, 4616, 5640

    w_in_t = jnp.swapaxes(w_in, 1, 2).astype(BF16)
    for l in range(depth):
        w_gates = w_in_t[l, c_ga:c_end]
        proj, gif, gif_t = in_proj(x2, g_mix[l].reshape(1, d), w_gates, w_in_t, l, nb_cols=c_i, if_col=c_i,
                                   tm=1024, tn=512)

        ya = moba(proj, _tile2(g_q[l]), _tile2(g_k[l]), batch=batch, seq=seq)

        bias = jnp.concatenate([b_i[l], b_f[l]])
        bias_row = jnp.pad(bias, (0, LANES - 2 * nh)).reshape(1, LANES)
        bias_col = bias.reshape(2 * nh, 1)
        yb = mlstm(proj, gif, gif_t, bias_row, bias_col, conv_w[l], conv_b[l].reshape(1, -1),
                   g_h[l].reshape(1, -1), batch=batch, seq=seq, chunk=256)

        j = l // 2
        moe = l % 2 == 1
        wr = jnp.pad(w_router[j], ((0, 0), (0, LANES - N_EXPERTS))) if moe else None
        outs = merge(ya, yb, proj, x2, w_oa[l].astype(BF16), w_ob[l].astype(BF16), w_out[l].astype(BF16),
                     g_ffn[l].reshape(1, d), wr, tm=1024)
        ple_args = (p.reshape(depth * t, -1), l * t, g_ple[l].reshape(1, d), w_ple_gate[l].astype(BF16),
                    w_ple_proj[l].astype(BF16))
        if moe:
            x1, hf, logits = outs
            x2 = moe_ffn(hf, x1, logits, w_e1[j], w_e3[j], w_e2[j],
                         ple_args, tm_route=512, tm_rows=512, tm_dispatch=1024, tm_combine=512)
        else:
            x1, hf = outs
            x2 = dense_ffn(hf, x1, w_d1[j].astype(BF16), w_d3[j].astype(BF16), w_d2[j].astype(BF16),
                           *ple_args, tm=1024, fc=256)
    return x2.reshape(batch, seq, d)
```

```python
import functools

import jax
import jax.numpy as jnp
from jax import lax
from jax.experimental import pallas as pl
from jax.experimental.pallas import tpu as pltpu

F32 = jnp.float32
BF16 = jnp.bfloat16

RMS_EPS = 1e-6
LANES = 128
SUBLANES = 8

MOBA_HEADS = 8
MOBA_HEAD_DIM = 64
MOBA_BLOCK = 256
MOBA_TOPK = 3
MLSTM_HEADS = 4
MLSTM_DIM = 128
CONV_WIDTH = 4
N_EXPERTS = 8

COL_GA, COL_GB = 0, 8
COL_QA, COL_KA, COL_VA = 16, 20, 24
COL_QM, COL_KM, COL_VM, COL_OM = 28, 32, 36, 40

VMEM_LIMIT = 56 * 1024 * 1024
EXPERTS_VMEM_LIMIT = 60 * 1024 * 1024
ROW_DMA_UNROLL = 16

TM_IN_PROJ, TN_IN_PROJ = 1024, 512
MLSTM_CHUNK = 256
TM_MERGE = 1024
TM_DENSE = 1024
FC_SWIGLU = 256
TM_ROUTE = 512
TM_EXPERT_ROWS = 512
TM_DISPATCH, TM_COMBINE = 1024, 512


def _params(*sem, vmem_limit=VMEM_LIMIT):
    return pltpu.CompilerParams(dimension_semantics=sem, vmem_limit_bytes=vmem_limit)


def _sigmoid(x):
    return 1.0 / (1.0 + jnp.exp(-x))


def _rms(x, g):
    return x * lax.rsqrt(jnp.mean(x * x, axis=-1, keepdims=True) + RMS_EPS) * g


def _split_bf16(x):
    hi = x.astype(BF16)
    return hi, (x - hi.astype(F32)).astype(BF16)


def _store_row_tiles(ref, x):
    ref[...] = x.reshape(x.shape[0] * SUBLANES, LANES)


def _load_row_tiles(ref, rows):
    return ref[...].reshape(rows, SUBLANES * LANES)


def _nt_dot(a, b, **kw):
    return lax.dot_general(a, b, (((1,), (1,)), ((), ())), preferred_element_type=F32, **kw)


def _in_proj_body(x_ref, g_ref, wa_ref, wb_ref, o_ref, oif_ref, oif_t_ref, h_ref, *, nb_cols, if_col, tn):
    h_ref[...] = _rms(x_ref[...], g_ref[...]).astype(BF16)
    gates = _nt_dot(h_ref[...], wb_ref[if_col:if_col + LANES, :])
    oif_ref[...] = gates
    oif_t_ref[...] = gates.T[0:SUBLANES, :]
    na = wa_ref.shape[0]
    for c0 in range(0, na + nb_cols, tn):
        w = wa_ref[c0:c0 + tn, :] if c0 < na else wb_ref[c0 - na:c0 - na + tn, :]
        o_ref[:, c0:c0 + tn] = _nt_dot(h_ref[...], w).astype(o_ref.dtype)


def in_proj(x2, g, wa, w_full, layer, *, nb_cols, if_col, tm, tn):
    t, d = x2.shape
    n = wa.shape[0] + nb_cols
    resident = dict(pipeline_mode=pl.Buffered(1))
    return pl.pallas_call(
        functools.partial(_in_proj_body, nb_cols=nb_cols, if_col=if_col, tn=tn),
        grid=(t // tm,),
        in_specs=[
            pl.BlockSpec((tm, d), lambda m: (m, 0)),
            pl.BlockSpec((1, d), lambda m: (0, 0)),
            pl.BlockSpec(wa.shape, lambda m: (0, 0), **resident),
            pl.BlockSpec((None,) + w_full.shape[1:], lambda m: (layer, 0, 0), **resident),
        ],
        out_specs=[
            pl.BlockSpec((tm, n), lambda m: (m, 0)),
            pl.BlockSpec((tm, LANES), lambda m: (m, 0)),
            pl.BlockSpec((SUBLANES, tm), lambda m: (0, m)),
        ],
        out_shape=[jax.ShapeDtypeStruct((t, n), BF16), jax.ShapeDtypeStruct((t, LANES), F32),
                   jax.ShapeDtypeStruct((SUBLANES, t), F32)],
        scratch_shapes=[pltpu.VMEM((tm, d), BF16)],
        compiler_params=_params("arbitrary"),
        name="in_proj",
    )(x2, g, wa, w_full)


MASK_BIAS = -1e30
LOG2_E = 1.4426950408889634


def _moba_body(q_ref, k_ref, v_ref, gq_ref, gk_ref, o_ref,
               kn_ref, vt_ref, kmean_ref, qaug_ref, s_ref, m_ref, alpha_ref, acc_ref,
               *, nb, blk, dh, topk, nheads):
    i = pl.program_id(1)
    pair = 2 * blk
    lane = lax.broadcasted_iota(jnp.int32, (1, LANES), 1)
    head0 = lane < dh

    same_head = (lax.broadcasted_iota(jnp.int32, (LANES, LANES), 0) // dh
                 == lax.broadcasted_iota(jnp.int32, (LANES, LANES), 1) // dh)
    head_ones = jnp.where(same_head, 1.0, 0.0).astype(BF16)

    def head_rms(x, g):
        hi, lo = _split_bf16(x * x)
        ss = (jnp.dot(hi, head_ones, preferred_element_type=F32)
              + jnp.dot(lo, head_ones, preferred_element_type=F32))
        return x * lax.rsqrt(ss * (1.0 / dh) + RMS_EPS) * g

    @pl.when(i == 0)
    def _():
        def prep(j, c):
            r0 = pl.multiple_of(j * blk, blk)
            onehot = jnp.where(lane == dh + j, 1.0, 0.0)
            for p in range(nheads // 2):
                cols = slice(p * LANES, (p + 1) * LANES)
                kn = head_rms(k_ref[pl.ds(r0, blk), cols].astype(F32), gk_ref[...])
                for hh, kh in ((0, kn), (1, pltpu.roll(kn, dh, axis=1))):
                    h = 2 * p + hh
                    kmean_ref[h, pl.ds(j, 1), :] = jnp.mean(jnp.where(head0, kh, 0.0), axis=0, keepdims=True)
                    kn_ref[h, pl.ds(r0, blk), :] = jnp.where(head0, kh, onehot).astype(BF16)
                v_t = v_ref[pl.ds(r0, blk), cols].astype(F32).T.astype(BF16)
                for hh in range(2):
                    vt_ref[2 * p + hh, 0:dh, pl.ds(r0, blk)] = v_t[hh * dh:(hh + 1) * dh, :]
                    vt_ref[2 * p + hh, dh:, pl.ds(r0, blk)] = jnp.ones((vt_ref.shape[1] - dh, blk), BF16)
            return c

        lax.fori_loop(0, nb, prep, 0)

    jidx = lax.broadcasted_iota(jnp.int32, (nb, blk), 0)
    key_i = lax.broadcasted_iota(jnp.int32, (blk, blk), 0)
    qry_i = lax.broadcasted_iota(jnp.int32, (blk, blk), 1)
    causal = key_i <= qry_i
    r_own = pl.multiple_of(i * blk, blk)
    qk_scale = dh ** -0.5 * LOG2_E
    for p in range(nheads // 2):
        cols = slice(p * LANES, (p + 1) * LANES)
        qr_t = q_ref[:, cols].astype(F32).T
        for hh in range(2):
            h = 2 * p + hh
            q_raw = qr_t[hh * dh:(hh + 1) * dh, :]
            q_t = (q_raw * lax.rsqrt(jnp.mean(q_raw * q_raw, axis=0, keepdims=True) + RMS_EPS)
                   * gq_ref[hh * dh:(hh + 1) * dh, :])
            gate = jnp.dot(kmean_ref[h].astype(BF16),
                           jnp.concatenate([q_t, jnp.zeros((LANES - dh, blk), F32)], axis=0).astype(BF16),
                           preferred_element_type=F32)
            rank = jnp.zeros((nb, blk), F32)
            for jp in range(nb):
                row = gate[jp:jp + 1, :]
                beats = (row > gate) | ((row == gate) & (jidx > jp))
                rank = rank + jnp.where(beats, jnp.where(jp < i, 1.0, 0.0), 0.0)
            sel = (rank < topk) & (jidx < i)
            q_s = q_t * qk_scale
            pad = jnp.zeros((LANES - dh - nb, blk), F32)
            qaug_ref[h] = jnp.concatenate([q_s, jnp.where(sel, 0.0, MASK_BIAS), pad], axis=0).astype(BF16)
            qaug_own = jnp.concatenate([q_s, jnp.where(jidx == i, 0.0, MASK_BIAS), pad], axis=0).astype(BF16)
            st = jnp.dot(kn_ref[h, pl.ds(r_own, blk), :], qaug_own, preferred_element_type=F32)
            st = jnp.where(causal, st, -jnp.inf)
            s_ref[h, 0:blk, :] = st
            m_ref[h] = jnp.max(st, axis=0, keepdims=True)

    def finish_own(h):
        pr = jnp.exp2(s_ref[h, 0:blk, :] - m_ref[h]).astype(BF16)
        acc_ref[h] = jnp.dot(vt_ref[h, :, pl.ds(r_own, blk)], pr, preferred_element_type=F32)

    def score_pair(u, h):
        r0 = pl.multiple_of(u * pair, pair)
        st = jnp.dot(kn_ref[h, pl.ds(r0, pair), :], qaug_ref[h], preferred_element_type=F32)
        m_old = m_ref[h]
        m_new = jnp.maximum(m_old, jnp.max(st, axis=0, keepdims=True))
        s_ref[h] = st
        alpha_ref[h] = jnp.exp2(m_old - m_new)
        m_ref[h] = m_new

    def finish_pair(u, h):
        r0 = pl.multiple_of(u * pair, pair)
        pr = jnp.exp2(s_ref[h] - m_ref[h]).astype(BF16)
        acc_ref[h] = alpha_ref[h] * acc_ref[h] + jnp.dot(vt_ref[h, :, pl.ds(r0, pair)], pr,
                                                         preferred_element_type=F32)

    n_pairs = jnp.maximum((i + 1) // 2, 1)
    for h in range(nheads):
        finish_own(h)
        score_pair(0, h)

    def body(u, c):
        for h in range(nheads):
            finish_pair(u - 1, h)
            score_pair(u, h)
        return c

    lax.fori_loop(1, n_pairs, body, 0)
    for h in range(nheads):
        finish_pair(n_pairs - 1, h)

    for p in range(nheads // 2):
        a0 = acc_ref[2 * p]
        a1 = acc_ref[2 * p + 1]
        ot = jnp.concatenate([a0[0:dh] / a0[dh:dh + 1], a1[0:dh] / a1[dh:dh + 1]], axis=0)
        o_ref[:, p * LANES:(p + 1) * LANES] = ot.T.astype(o_ref.dtype)


def moba(proj, gq2, gk2, *, batch, seq):
    nb = seq // MOBA_BLOCK
    blk = MOBA_BLOCK
    gq_t = jnp.broadcast_to(gq2.reshape(LANES, 1), (LANES, blk))
    dh = MOBA_HEAD_DIM
    nheads = MOBA_HEADS
    width = nheads * dh
    wb = width // LANES
    assert dh + nb <= LANES and 2 * dh == LANES and nb % 2 == 0
    v_rows = dh + 2 * SUBLANES
    body = functools.partial(_moba_body, nb=nb, blk=blk, dh=dh, topk=MOBA_TOPK, nheads=nheads)
    return pl.pallas_call(
        body,
        grid=(batch, nb),
        in_specs=[
            pl.BlockSpec((blk, width), lambda b, i: (b * nb + i, COL_QA // wb)),
            pl.BlockSpec((seq, width), lambda b, i: (b, COL_KA // wb)),
            pl.BlockSpec((seq, width), lambda b, i: (b, COL_VA // wb)),
            pl.BlockSpec((LANES, blk), lambda b, i: (0, 0)),
            pl.BlockSpec((1, LANES), lambda b, i: (0, 0)),
        ],
        out_specs=pl.BlockSpec((blk, width), lambda b, i: (b * nb + i, 0)),
        out_shape=jax.ShapeDtypeStruct((batch * seq, width), BF16),
        scratch_shapes=[
            pltpu.VMEM((nheads, seq, LANES), BF16),
            pltpu.VMEM((nheads, v_rows, seq), BF16),
            pltpu.VMEM((nheads, nb, LANES), F32),
            pltpu.VMEM((nheads, LANES, blk), BF16),
            pltpu.VMEM((nheads, 2 * blk, blk), F32),
            pltpu.VMEM((nheads, 1, blk), F32),
            pltpu.VMEM((nheads, 1, blk), F32),
            pltpu.VMEM((nheads, v_rows, blk), F32),
        ],
        compiler_params=_params("arbitrary", "arbitrary"),
        name="moba",
    )(proj, proj, proj, gq_t, gk2)


def _log_sigmoid(x):
    return jnp.minimum(x, 0.0) - jnp.log(1.0 + jnp.exp(-jnp.abs(x)))


def _dot_tri(tri, x, tri_left):
    out = None
    for _ in range(3):
        piece = x.astype(BF16)
        x = x - piece.astype(F32)
        term = (jnp.dot(tri, piece, preferred_element_type=F32) if tri_left
                else jnp.dot(piece, tri, preferred_element_type=F32))
        out = term if out is None else out + term
    return out


def _mlstm_body(qr_ref, kr_ref, v_ref, og_ref, gcol_ref, grow_ref, brow_ref, bcol_ref,
                cwq_ref, cwk_ref, cbq_ref, cbk_ref, gh_ref, o_ref,
                qx_ref, kx_ref, c_ref, m_ref, *, chunk, dk, nh):
    L = chunk
    width = nh * dk

    @pl.when(pl.program_id(1) == 0)
    def _():
        qx_ref[0:SUBLANES, :] = jnp.zeros((SUBLANES, width), F32)
        kx_ref[0:SUBLANES, :] = jnp.zeros((SUBLANES, width), F32)
        c_ref[...] = jnp.zeros_like(c_ref)
        m_ref[...] = jnp.zeros_like(m_ref)

    qx_ref[SUBLANES:SUBLANES + L, :] = qr_ref[...].astype(F32)
    kx_ref[SUBLANES:SUBLANES + L, :] = kr_ref[...].astype(F32)

    def conv_silu(x_ref, w_ref, b_ref):
        acc = b_ref[...] + w_ref[0:1, :] * x_ref[pl.ds(SUBLANES - CONV_WIDTH + 1, L), :]
        for j in range(1, CONV_WIDTH):
            acc = acc + w_ref[j:j + 1, :] * x_ref[pl.ds(SUBLANES - CONV_WIDTH + 1 + j, L), :]
        return acc * _sigmoid(acc)

    q_all = conv_silu(qx_ref, cwq_ref, cbq_ref)
    k_all = conv_silu(kx_ref, cwk_ref, cbk_ref) * (dk ** -0.5)
    qx_ref[0:SUBLANES, :] = qx_ref[L:L + SUBLANES, :]
    kx_ref[0:SUBLANES, :] = kx_ref[L:L + SUBLANES, :]

    pre_col = gcol_ref[...] + brow_ref[...]
    pre_row = grow_ref[...] + bcol_ref[...]
    t_i = lax.broadcasted_iota(jnp.int32, (L, L), 0)
    s_i = lax.broadcasted_iota(jnp.int32, (L, L), 1)
    tril = s_i <= t_i
    bcum_cols = _dot_tri(jnp.where(tril, 1.0, 0.0).astype(BF16), _log_sigmoid(pre_col), True)
    bcum_rows = _dot_tri(jnp.where(t_i <= s_i, 1.0, 0.0).astype(BF16), _log_sigmoid(pre_row), False)
    src_before_out = t_i <= s_i
    v_t = v_ref[...].astype(F32).T
    ones_t = jnp.ones((dk, L), F32)

    for h in range(nh):
        cols = slice(h * dk, (h + 1) * dk)
        qb = q_all[:, cols].astype(BF16)
        kb = k_all[:, cols].astype(BF16)
        i_col = pre_col[:, h:h + 1]
        i_row = pre_row[h:h + 1, :]
        bcum_col = bcum_cols[:, nh + h:nh + h + 1]
        bcum_row = bcum_rows[nh + h:nh + h + 1, :]

        m_prev = m_ref[h, 0:1, 0:1]
        a_row = bcum_row + m_prev
        dmat_t = jnp.where(src_before_out, bcum_row + (i_col - bcum_col), -jnp.inf)
        m_t = jnp.maximum(a_row, jnp.max(dmat_t, axis=0, keepdims=True))
        sqk_t = (_nt_dot(kb, qb) * jnp.exp(dmat_t - m_t)).astype(BF16)

        v_aug_t = jnp.concatenate([v_t[cols, :], ones_t], axis=0)
        state_t = c_ref[h]
        num_aug_t = (jnp.exp(a_row - m_t) * _nt_dot(state_t.astype(BF16), qb)
                     + jnp.dot(v_aug_t.astype(BF16), sqk_t, preferred_element_type=F32))
        den = num_aug_t[dk:dk + 1, :]
        hc_t = num_aug_t[0:dk, :] / jnp.maximum(jnp.abs(den), jnp.exp(-m_t))
        hn_t = hc_t * lax.rsqrt(jnp.mean(hc_t * hc_t, axis=0, keepdims=True) + RMS_EPS)
        o_ref[:, cols] = (hn_t.T * gh_ref[...] * _sigmoid(og_ref[:, cols].astype(F32))).astype(o_ref.dtype)

        b_last = bcum_row[:, L - 1:L]
        g_row = b_last - bcum_row + i_row
        m_new = jnp.maximum(b_last + m_prev, jnp.max(g_row, axis=-1, keepdims=True))
        w_c = jnp.exp(b_last + m_prev - m_new)
        vw_t = (v_aug_t * jnp.exp(g_row - m_new)).astype(BF16)
        c_ref[h] = w_c * state_t + jnp.dot(vw_t, kb, preferred_element_type=F32)
        m_ref[h] = jnp.broadcast_to(m_new, (1, LANES))


def mlstm(proj, gates_col, gates_row, bias_row, bias_col, conv_w, conv_b, gh, *, batch, seq, chunk):
    nh = MLSTM_HEADS
    assert 2 * nh == SUBLANES
    dk = MLSTM_DIM
    width = nh * dk
    wb = width // LANES
    nc = seq // chunk
    body = functools.partial(_mlstm_body, chunk=chunk, dk=dk, nh=nh)

    def rows(col0):
        return pl.BlockSpec((chunk, width), lambda b, c: (b * nc + c, col0 // wb))

    return pl.pallas_call(
        body,
        grid=(batch, nc),
        in_specs=[
            rows(COL_QM), rows(COL_KM), rows(COL_VM), rows(COL_OM),
            pl.BlockSpec((chunk, LANES), lambda b, c: (b * nc + c, 0)),
            pl.BlockSpec((SUBLANES, chunk), lambda b, c: (0, b * nc + c)),
            pl.BlockSpec((1, LANES), lambda b, c: (0, 0)),
            pl.BlockSpec((SUBLANES, 1), lambda b, c: (0, 0)),
            pl.BlockSpec((CONV_WIDTH, width), lambda b, c: (0, 0)),
            pl.BlockSpec((CONV_WIDTH, width), lambda b, c: (0, 1)),
            pl.BlockSpec((1, width), lambda b, c: (0, 0)),
            pl.BlockSpec((1, width), lambda b, c: (0, 1)),
            pl.BlockSpec((1, LANES), lambda b, c: (0, 0)),
        ],
        out_specs=pl.BlockSpec((chunk, width), lambda b, c: (b * nc + c, 0)),
        out_shape=jax.ShapeDtypeStruct((batch * seq, width), BF16),
        scratch_shapes=[
            pltpu.VMEM((chunk + 2 * SUBLANES, width), F32),
            pltpu.VMEM((chunk + 2 * SUBLANES, width), F32),
            pltpu.VMEM((nh, 2 * dk, dk), F32),
            pltpu.VMEM((nh, 1, LANES), F32),
        ],
        compiler_params=_params("arbitrary", "arbitrary"),
        name="mlstm",
    )(proj, proj, proj, proj, gates_col, gates_row, bias_row, bias_col,
      conv_w, conv_w, conv_b, conv_b, gh)


def _merge_body(ya_ref, yb_ref, ga_ref, gb_ref, x_ref, woa_ref, wob_ref, wout_ref, gffn_ref, *rest, moe):
    a = jnp.dot(ya_ref[...], woa_ref[...], preferred_element_type=F32)
    b = jnp.dot(yb_ref[...], wob_ref[...], preferred_element_type=F32)
    mixed = _sigmoid(ga_ref[...].astype(F32)) * a + _sigmoid(gb_ref[...].astype(F32)) * b
    x1 = x_ref[...] + jnp.dot(mixed.astype(BF16), wout_ref[...], preferred_element_type=F32)
    hf = _rms(x1, gffn_ref[...])
    if moe:
        wr_ref, x1_ref, hf_ref, lg_ref = rest
        _store_row_tiles(hf_ref, hf)
        lg_ref[...] = jnp.dot(hf.astype(BF16), wr_ref[...].astype(BF16), preferred_element_type=F32)
    else:
        x1_ref, hf_ref = rest
        hf_ref[...] = hf.astype(BF16)
    x1_ref[...] = x1


def merge(ya, yb, proj, x2, woa, wob, wout, gffn, wr, *, tm):
    t, d = x2.shape
    moe = wr is not None
    full = lambda m: (0, 0)
    in_specs = [
        pl.BlockSpec((tm, ya.shape[1]), lambda m: (m, 0)),
        pl.BlockSpec((tm, yb.shape[1]), lambda m: (m, 0)),
        pl.BlockSpec((tm, d), lambda m: (m, COL_GA * LANES // d)),
        pl.BlockSpec((tm, d), lambda m: (m, COL_GB * LANES // d)),
        pl.BlockSpec((tm, d), lambda m: (m, 0)),
        pl.BlockSpec(woa.shape, full), pl.BlockSpec(wob.shape, full), pl.BlockSpec(wout.shape, full),
        pl.BlockSpec((1, d), full),
    ]
    args = [ya, yb, proj, proj, x2, woa, wob, wout, gffn]
    out_specs = [pl.BlockSpec((tm, d), lambda m: (m, 0)), pl.BlockSpec((tm, d), lambda m: (m, 0))]
    out_shape = [jax.ShapeDtypeStruct((t, d), F32), jax.ShapeDtypeStruct((t, d), BF16)]
    if moe:
        out_specs[1] = pl.BlockSpec((tm * SUBLANES, LANES), lambda m: (m, 0))
        out_shape[1] = jax.ShapeDtypeStruct((t * SUBLANES, LANES), F32)
    if moe:
        in_specs.append(pl.BlockSpec(wr.shape, full))
        args.append(wr)
        out_specs.append(pl.BlockSpec((tm, LANES), lambda m: (m, 0)))
        out_shape.append(jax.ShapeDtypeStruct((t, LANES), F32))
    return pl.pallas_call(
        functools.partial(_merge_body, moe=moe),
        grid=(t // tm,),
        in_specs=in_specs, out_specs=out_specs, out_shape=out_shape,
        compiler_params=_params("arbitrary"),
        name="merge_moe" if moe else "merge",
    )(*args)


def _swiglu_gate(x, w1_ref, w3_ref, g_ref, fc):
    dff = g_ref.shape[1]
    for f0 in range(0, dff, fc):
        a = jnp.dot(x, w1_ref[:, f0:f0 + fc].astype(BF16), preferred_element_type=F32)
        b = jnp.dot(x, w3_ref[:, f0:f0 + fc].astype(BF16), preferred_element_type=F32)
        g_ref[:, f0:f0 + fc] = (a * _sigmoid(a) * b).astype(BF16)


def _swiglu(x, w1_ref, w3_ref, w2_ref, g_ref, fc):
    _swiglu_gate(x, w1_ref, w3_ref, g_ref, fc)
    return jnp.dot(g_ref[...], w2_ref[...].astype(BF16), preferred_element_type=F32)


def _ple(x, p_ref, g_ref, wg_ref, wp_ref):
    gate = _sigmoid(jnp.dot(_rms(x, g_ref[...]).astype(BF16), wg_ref[...], preferred_element_type=F32))
    emb = jnp.dot(p_ref[...].astype(BF16), wp_ref[...], preferred_element_type=F32)
    return x + gate * emb


def _ffn_body(hf_ref, x1_ref, w1_ref, w3_ref, w2_ref, p_ref, g_ref, wg_ref, wp_ref, o_ref, act_ref, *, fc):
    x2 = x1_ref[...] + _swiglu(hf_ref[...], w1_ref, w3_ref, w2_ref, act_ref, fc)
    o_ref[...] = _ple(x2, p_ref, g_ref, wg_ref, wp_ref)


def dense_ffn(hf, x1, w1, w3, w2, p2, p_row0, g, wg, wp, *, tm, fc):
    t, d = x1.shape
    dff = w1.shape[1]
    p_blk0 = p_row0 // tm
    resident = dict(pipeline_mode=pl.Buffered(1))
    full = lambda m: (0, 0)
    return pl.pallas_call(
        functools.partial(_ffn_body, fc=fc),
        grid=(t // tm,),
        in_specs=[
            pl.BlockSpec((tm, d), lambda m: (m, 0)),
            pl.BlockSpec((tm, d), lambda m: (m, 0)),
            pl.BlockSpec((d, dff), full, **resident),
            pl.BlockSpec((d, dff), full, **resident),
            pl.BlockSpec((dff, d), full, **resident),
            pl.BlockSpec((tm, p2.shape[1]), lambda m: (p_blk0 + m, 0)),
            pl.BlockSpec((1, d), full),
            pl.BlockSpec(wg.shape, full, **resident),
            pl.BlockSpec(wp.shape, full, **resident),
        ],
        out_specs=pl.BlockSpec((tm, d), lambda m: (m, 0)),
        out_shape=jax.ShapeDtypeStruct((t, d), F32),
        scratch_shapes=[pltpu.VMEM((tm, dff), BF16)],
        compiler_params=_params("arbitrary"),
        name="dense_ffn",
    )(hf, x1, w1, w3, w2, p2, g, wg, wp)


META_E0, META_E1, META_G0, META_G1, META_R0, META_R1 = 0, 1, 2, 3, 4, 5


def _route_body(lg_ref, meta_ref, tab_ref, cnt_ref, carry_ref, *, tm, ne):
    @pl.when(pl.program_id(0) == 0)
    def _():
        carry_ref[...] = jnp.zeros_like(carry_ref)

    lane = lax.broadcasted_iota(jnp.int32, (tm, LANES), 1)
    lanef = lane.astype(F32)
    lg = jnp.where(lane < ne, lg_ref[...], -jnp.inf)
    m1 = jnp.max(lg, axis=-1, keepdims=True)
    e1 = jnp.min(jnp.where(lg == m1, lanef, float(LANES)), axis=-1, keepdims=True)
    lg2 = jnp.where(lanef == e1, -jnp.inf, lg)
    m2 = jnp.max(lg2, axis=-1, keepdims=True)
    e2 = jnp.min(jnp.where(lg2 == m2, lanef, float(LANES)), axis=-1, keepdims=True)
    ex = jnp.exp(m2 - m1)
    g1 = 1.0 / (1.0 + ex)
    g2 = ex / (1.0 + ex)
    onehot = jnp.where((lanef == e1) | (lanef == e2), 1.0, 0.0)
    row = lax.broadcasted_iota(jnp.int32, (tm, tm), 0)
    col = lax.broadcasted_iota(jnp.int32, (tm, tm), 1)
    before = jnp.dot(jnp.where(col < row, 1.0, 0.0).astype(BF16), onehot.astype(BF16),
                     preferred_element_type=F32) + carry_ref[...]
    r1 = jnp.sum(jnp.where(lanef == e1, before, 0.0), axis=-1, keepdims=True)
    r2 = jnp.sum(jnp.where(lanef == e2, before, 0.0), axis=-1, keepdims=True)
    carry_ref[...] += jnp.sum(onehot, axis=0, keepdims=True)
    meta = jnp.zeros((tm, LANES), F32)
    for pos, val in ((META_E0, e1), (META_E1, e2), (META_G0, g1), (META_G1, g2), (META_R0, r1), (META_R1, r2)):
        meta = jnp.where(lane == pos, val, meta)
    meta_ref[...] = meta
    tab_ref[...] = meta.T[0:SUBLANES, :]
    cnt_ref[...] = jnp.broadcast_to(carry_ref[...], cnt_ref.shape)


def route(logits, *, tm):
    t = logits.shape[0]
    return pl.pallas_call(
        functools.partial(_route_body, tm=tm, ne=N_EXPERTS),
        grid=(t // tm,),
        in_specs=[pl.BlockSpec((tm, LANES), lambda m: (m, 0))],
        out_specs=[pl.BlockSpec((tm, LANES), lambda m: (m, 0)),
                   pl.BlockSpec((SUBLANES, tm), lambda m: (0, m)),
                   pl.BlockSpec((SUBLANES, LANES), lambda m: (0, 0))],
        out_shape=[jax.ShapeDtypeStruct((t, LANES), F32), jax.ShapeDtypeStruct((SUBLANES, t), F32),
                   jax.ShapeDtypeStruct((SUBLANES, LANES), F32)],
        scratch_shapes=[pltpu.VMEM((1, LANES), F32)],
        compiler_params=_params("arbitrary"),
        name="route",
    )(logits)


def _dispatch_body(dest_ref, pe_ref, na_ref, hf_ref, wsrc_ref, xs_ref, wdst_ref, zero_ref, sem, zsem,
                   *, tm, topk, tm_rows, n_tok):
    wdst_ref[...] = wsrc_ref[...].astype(wdst_ref.dtype)

    base = pl.program_id(0) * tm

    @pl.when(pl.program_id(0) == 0)
    def _():
        zero_ref[...] = jnp.zeros_like(zero_ref)
        n_tiles = xs_ref.shape[0] // (tm_rows * SUBLANES)
        fills = []
        for e in range(N_EXPERTS):
            end = pe_ref[e]
            nonempty = end > (pe_ref[e - 1] if e else 0)
            fills.append((nonempty, pl.multiple_of(jnp.maximum(end - tm_rows, 0), tm_rows)))
        for tile in range(n_tiles):
            fills.append((tile >= na_ref[0], tile * tm_rows))
        for phase in ("start", "wait"):
            for cond, row0 in fills:
                @pl.when(cond)
                def _(row0=row0, phase=phase):
                    cp = pltpu.make_async_copy(
                        zero_ref, xs_ref.at[pl.ds(row0 * SUBLANES, tm_rows * SUBLANES)], zsem)
                    cp.start() if phase == "start" else cp.wait()

    def issue(r, c):
        for k in range(topk):
            d = dest_ref[k * n_tok + base + r]
            pltpu.make_async_copy(_row_tile(hf_ref, r), _row_tile(xs_ref, d), sem).start(priority=k % 2)
        return c

    lax.fori_loop(0, tm, issue, 0, unroll=ROW_DMA_UNROLL)
    for k in range(topk):
        pltpu.make_async_copy(hf_ref, xs_ref.at[pl.ds(0, tm * SUBLANES)], sem).wait()


def _row_tile(ref, r):
    return ref.at[pl.ds(pl.multiple_of(r * SUBLANES, SUBLANES), SUBLANES)]


def dispatch(dest, pad_end, n_active, hf, w_f32, *, n_rows, tm, topk, tm_rows):
    t = hf.shape[0] // SUBLANES
    n_steps = t // tm
    w2d = w_f32.reshape(-1, w_f32.shape[-1])
    slab = w2d.shape[0] // n_steps
    assert slab * n_steps == w2d.shape[0] and slab % (2 * SUBLANES) == 0
    xs, w_bf = pl.pallas_call(
        functools.partial(_dispatch_body, tm=tm, topk=topk, tm_rows=tm_rows, n_tok=t),
        grid_spec=pltpu.PrefetchScalarGridSpec(
            num_scalar_prefetch=3,
            grid=(n_steps,),
            in_specs=[pl.BlockSpec((tm * SUBLANES, LANES), lambda m, *_: (m, 0)),
                      pl.BlockSpec((slab, w2d.shape[1]), lambda m, *_: (m, 0))],
            out_specs=[pl.BlockSpec(memory_space=pl.ANY),
                       pl.BlockSpec((slab, w2d.shape[1]), lambda m, *_: (m, 0))],
            scratch_shapes=[pltpu.VMEM((tm_rows * SUBLANES, LANES), F32),
                            pltpu.SemaphoreType.DMA, pltpu.SemaphoreType.DMA],
        ),
        out_shape=[jax.ShapeDtypeStruct((n_rows * SUBLANES, LANES), F32),
                   jax.ShapeDtypeStruct(w2d.shape, BF16)],
        compiler_params=_params("arbitrary"),
        name="moe_dispatch",
    )(dest, pad_end, n_active, hf, w2d)
    return xs, w_bf.reshape(w_f32.shape)


def _experts_body(te_ref, na_ref, xs_ref, w1_hbm, w3_hbm, w2_hbm, y_ref,
                  xb_ref, g_ref, w1_buf, w3_buf, w2_buf, wsem, *, fc):
    i = pl.program_id(0)
    na = na_ref[0]
    tm = xb_ref.shape[0]
    tf = g_ref.shape[2]

    def weight_copies(tile, half):
        e = te_ref[tile]
        cols = pl.ds(half * tf, tf)
        return (pltpu.make_async_copy(w1_hbm.at[e, :, cols], w1_buf.at[half], wsem.at[half, 0]),
                pltpu.make_async_copy(w3_hbm.at[e, :, cols], w3_buf.at[half], wsem.at[half, 1]),
                pltpu.make_async_copy(w2_hbm.at[e, cols, :], w2_buf.at[half], wsem.at[half, 2]))

    def start(tile, half):
        for cp in weight_copies(tile, half):
            cp.start()

    def wait(half):
        for cp in weight_copies(0, half):
            cp.wait()

    @pl.when(i >= na)
    def _():
        y_ref[...] = jnp.zeros_like(y_ref)

    @pl.when(i < na)
    def _():
        first_of_expert = (i == 0) | (te_ref[i] != te_ref[jnp.maximum(i - 1, 0)])
        nxt = jnp.minimum(i + 1, na - 1)
        next_is_new_expert = (i + 1 < na) & (te_ref[nxt] != te_ref[i])

        @pl.when(i == 0)
        def _():
            start(0, 0)

        @pl.when(first_of_expert)
        def _():
            start(i, 1)
            wait(0)

        xb_ref[...] = _load_row_tiles(xs_ref, tm).astype(BF16)
        xb = xb_ref[...]
        _swiglu_gate(xb, w1_buf.at[0], w3_buf.at[0], g_ref.at[0], fc)
        acc = jnp.dot(g_ref[0], w2_buf[0].astype(BF16), preferred_element_type=F32)

        @pl.when(next_is_new_expert)
        def _():
            start(nxt, 0)

        @pl.when(first_of_expert)
        def _():
            wait(1)

        _swiglu_gate(xb, w1_buf.at[1], w3_buf.at[1], g_ref.at[1], fc)
        y = acc + jnp.dot(g_ref[1], w2_buf[1].astype(BF16), preferred_element_type=F32)
        _store_row_tiles(y_ref, y)


def experts(tile_expert, n_active, xs, w1, w3, w2, *, tm, fc):
    n_rows = xs.shape[0] // SUBLANES
    d = w1.shape[1]
    dff = w1.shape[2]
    tf = dff // 2
    return pl.pallas_call(
        functools.partial(_experts_body, fc=fc),
        grid_spec=pltpu.PrefetchScalarGridSpec(
            num_scalar_prefetch=2,
            grid=(n_rows // tm,),
            in_specs=[
                pl.BlockSpec((tm * SUBLANES, LANES), lambda i, te, na: (jnp.minimum(i, na[0] - 1), 0)),
                pl.BlockSpec(memory_space=pl.ANY),
                pl.BlockSpec(memory_space=pl.ANY),
                pl.BlockSpec(memory_space=pl.ANY),
            ],
            out_specs=pl.BlockSpec((tm * SUBLANES, LANES), lambda i, te, na: (i, 0)),
            scratch_shapes=[
                pltpu.VMEM((tm, d), BF16),
                pltpu.VMEM((2, tm, tf), BF16),
                pltpu.VMEM((2, d, tf), w1.dtype),
                pltpu.VMEM((2, d, tf), w3.dtype),
                pltpu.VMEM((2, tf, d), w2.dtype),
                pltpu.SemaphoreType.DMA((2, 3)),
            ],
        ),
        out_shape=jax.ShapeDtypeStruct((n_rows * SUBLANES, LANES), F32),
        compiler_params=_params("arbitrary", vmem_limit=EXPERTS_VMEM_LIMIT),
        name="moe_experts",
    )(tile_expert, n_active, xs, w1, w3, w2)


def _combine_body(dest_ref, x1_ref, meta_ref, p_ref, g_ref, wg_ref, wp_ref, y_ref, o_ref, buf_ref, sem,
                  *, tm, topk):
    m = pl.program_id(0)
    n_steps = pl.num_programs(0)
    n_tok = n_steps * tm

    def start_gather(tile, slot):
        def issue(r, c):
            for k in range(topk):
                d = dest_ref[k * n_tok + tile * tm + r]
                pltpu.make_async_copy(_row_tile(y_ref, d), _row_tile(buf_ref.at[slot, k], r),
                                      sem.at[slot]).start(priority=k % 2)
            return c

        lax.fori_loop(0, tm, issue, 0, unroll=ROW_DMA_UNROLL)

    @pl.when(m == 0)
    def _():
        start_gather(0, 0)

    @pl.when(m + 1 < n_steps)
    def _():
        start_gather(m + 1, (m + 1) % 2)

    slot = m % 2
    for k in range(topk):
        pltpu.make_async_copy(y_ref.at[pl.ds(0, tm * SUBLANES)], buf_ref.at[slot, k], sem.at[slot]).wait()
    meta = meta_ref[...]
    g0 = meta[:, META_G0:META_G0 + 1]
    g1 = meta[:, META_G1:META_G1 + 1]
    x2 = x1_ref[...] + (g0 * _load_row_tiles(buf_ref.at[slot, 0], tm)
                        + g1 * _load_row_tiles(buf_ref.at[slot, 1], tm))
    o_ref[...] = _ple(x2, p_ref, g_ref, wg_ref, wp_ref)


def combine(dest, x1, meta, y, p2, p_row0, g, wg, wp, *, tm, topk):
    t, d = x1.shape
    full = lambda m, dest: (0, 0)
    p_blk0 = p_row0 // tm
    return pl.pallas_call(
        functools.partial(_combine_body, tm=tm, topk=topk),
        grid_spec=pltpu.PrefetchScalarGridSpec(
            num_scalar_prefetch=1,
            grid=(t // tm,),
            in_specs=[pl.BlockSpec((tm, d), lambda m, dest: (m, 0)),
                      pl.BlockSpec((tm, LANES), lambda m, dest: (m, 0)),
                      pl.BlockSpec((tm, p2.shape[1]), lambda m, dest: (p_blk0 + m, 0)),
                      pl.BlockSpec((1, d), full), pl.BlockSpec(wg.shape, full), pl.BlockSpec(wp.shape, full),
                      pl.BlockSpec(memory_space=pl.ANY)],
            out_specs=pl.BlockSpec((tm, d), lambda m, dest: (m, 0)),
            scratch_shapes=[pltpu.VMEM((2, topk, tm * SUBLANES, LANES), F32), pltpu.SemaphoreType.DMA((2,))],
        ),
        out_shape=jax.ShapeDtypeStruct((t, d), F32),
        compiler_params=_params("arbitrary"),
        name="moe_combine",
    )(dest, x1, meta, p2, g, wg, wp, y)


def moe_ffn(hf, x1, logits, w1, w3, w2, ple_args, *, tm_route, tm_rows, tm_dispatch, tm_combine):
    t, d = x1.shape
    topk = 2
    meta, tab, cnt = route(logits, tm=tm_route)
    counts = cnt[0, :N_EXPERTS].astype(jnp.int32)
    padded = ((counts + tm_rows - 1) // tm_rows) * tm_rows
    pad_end = jnp.cumsum(padded).astype(jnp.int32)
    pad_start = pad_end - padded
    eidx = tab[META_E0:META_E1 + 1].astype(jnp.int32)
    rank = tab[META_R0:META_R1 + 1].astype(jnp.int32)
    dest = rank
    for e in range(N_EXPERTS):
        dest = dest + jnp.where(eidx == e, pad_start[e], 0)
    dest = dest.reshape(topk * t)
    n_tiles = -(-(t * topk) // tm_rows) + N_EXPERTS
    tile_start = jnp.arange(n_tiles, dtype=jnp.int32) * tm_rows
    tile_expert = jnp.minimum(jnp.sum(tile_start[:, None] >= pad_end[None, :], axis=1),
                              N_EXPERTS - 1).astype(jnp.int32)
    n_active = pad_end[N_EXPERTS - 1:] // tm_rows
    xs, w3_bf = dispatch(dest, pad_end, n_active, hf, w3, n_rows=n_tiles * tm_rows, tm=tm_dispatch, topk=topk,
                         tm_rows=tm_rows)
    y = experts(tile_expert, n_active, xs, w1, w3_bf, w2, tm=tm_rows, fc=FC_SWIGLU)
    return combine(dest, x1, meta, y, *ple_args, tm=tm_combine, topk=topk)


def _tile2(g):
    return jnp.concatenate([g, g]).reshape(1, 2 * g.shape[0])


def kernel(x, p, g_mix, w_in, g_q, g_k, conv_w, conv_b, b_i, b_f, g_h, w_oa, w_ob, w_out, g_ffn, w_d1, w_d3,
           w_d2, w_router, w_e1, w_e3, w_e2, g_ple, w_ple_gate, w_ple_proj):
    batch, seq, d = x.shape
    depth = w_in.shape[0]
    t = batch * seq
    nh = MLSTM_HEADS
    x2 = x.reshape(t, d)
    c_q, c_k, c_v = 0, 512, 1024
    c_qk, c_vm, c_om, c_i, c_f, c_ga, c_gb, c_end = 1536, 2560, 3072, 3584, 3588, 3592, 4616, 5640

    w_in_t = jnp.swapaxes(w_in, 1, 2).astype(BF16)
    for l in range(depth):
        w_gates = w_in_t[l, c_ga:c_end]
        proj, gif, gif_t = in_proj(x2, g_mix[l].reshape(1, d), w_gates, w_in_t, l, nb_cols=c_i, if_col=c_i,
                                   tm=TM_IN_PROJ, tn=TN_IN_PROJ)

        ya = moba(proj, _tile2(g_q[l]), _tile2(g_k[l]), batch=batch, seq=seq)

        bias = jnp.concatenate([b_i[l], b_f[l]])
        bias_row = jnp.pad(bias, (0, LANES - 2 * nh)).reshape(1, LANES)
        bias_col = bias.reshape(2 * nh, 1)
        yb = mlstm(proj, gif, gif_t, bias_row, bias_col, conv_w[l], conv_b[l].reshape(1, -1),
                   g_h[l].reshape(1, -1), batch=batch, seq=seq, chunk=MLSTM_CHUNK)

        j = l // 2
        moe = l % 2 == 1
        wr = jnp.pad(w_router[j], ((0, 0), (0, LANES - N_EXPERTS))) if moe else None
        outs = merge(ya, yb, proj, x2, w_oa[l].astype(BF16), w_ob[l].astype(BF16), w_out[l].astype(BF16),
                     g_ffn[l].reshape(1, d), wr, tm=TM_MERGE)
        ple_args = (p.reshape(depth * t, -1), l * t, g_ple[l].reshape(1, d), w_ple_gate[l].astype(BF16),
                    w_ple_proj[l].astype(BF16))
        if moe:
            x1, hf, logits = outs
            x2 = moe_ffn(hf, x1, logits, w_e1[j], w_e3[j], w_e2[j],
                         ple_args, tm_route=TM_ROUTE, tm_rows=TM_EXPERT_ROWS, tm_dispatch=TM_DISPATCH,
                         tm_combine=TM_COMBINE)
        else:
            x1, hf = outs
            x2 = dense_ffn(hf, x1, w_d1[j].astype(BF16), w_d3[j].astype(BF16), w_d2[j].astype(BF16),
                           *ple_args, tm=TM_DENSE, fc=FC_SWIGLU)
    return x2.reshape(batch, seq, d)
```

```python
import functools

import jax
import jax.numpy as jnp
from jax import lax
from jax.experimental import pallas as pl
from jax.experimental.pallas import tpu as pltpu

F32 = jnp.float32
BF16 = jnp.bfloat16

RMS_EPS = 1e-6
LANES = 128
SUBLANES = 8

MOBA_HEADS = 8
MOBA_HEAD_DIM = 64
MOBA_BLOCK = 256
MOBA_TOPK = 3
MLSTM_HEADS = 4
MLSTM_DIM = 128
CONV_WIDTH = 4
N_EXPERTS = 8

COL_GA, COL_GB = 0, 8
COL_QA, COL_KA, COL_VA = 16, 20, 24
COL_QM, COL_KM, COL_VM, COL_OM = 28, 32, 36, 40

VMEM_LIMIT = 56 * 1024 * 1024
EXPERTS_VMEM_LIMIT = 60 * 1024 * 1024
ROW_DMA_UNROLL = 16

TM_IN_PROJ, TN_IN_PROJ = 1024, 512
MLSTM_CHUNK = 256
TM_MERGE = 1024
TM_DENSE = 1024
FC_SWIGLU = 256
TM_ROUTE = 512
TM_EXPERT_ROWS = 512
TM_DISPATCH, TM_COMBINE = 1024, 512


def _params(*sem, vmem_limit=VMEM_LIMIT):
    return pltpu.CompilerParams(dimension_semantics=sem, vmem_limit_bytes=vmem_limit)


def _sigmoid(x):
    return 1.0 / (1.0 + jnp.exp(-x))


def _rms(x, g):
    return x * lax.rsqrt(jnp.mean(x * x, axis=-1, keepdims=True) + RMS_EPS) * g


def _split_bf16(x):
    hi = x.astype(BF16)
    return hi, (x - hi.astype(F32)).astype(BF16)


def _store_row_tiles(ref, x):
    ref[...] = x.reshape(x.shape[0] * SUBLANES, LANES)


def _load_row_tiles(ref, rows):
    return ref[...].reshape(rows, SUBLANES * LANES)


def _nt_dot(a, b, **kw):
    return lax.dot_general(a, b, (((1,), (1,)), ((), ())), preferred_element_type=F32, **kw)


def _in_proj_body(x_ref, g_ref, wa_ref, wb_ref, o_ref, oif_ref, oif_t_ref, h_ref, *, nb_cols, if_col, tn):
    h_ref[...] = _rms(x_ref[...], g_ref[...]).astype(BF16)
    gates = _nt_dot(h_ref[...], wb_ref[if_col:if_col + LANES, :])
    oif_ref[...] = gates
    oif_t_ref[...] = gates.T[0:SUBLANES, :]
    na = wa_ref.shape[0]
    for c0 in range(0, na + nb_cols, tn):
        w = wa_ref[c0:c0 + tn, :] if c0 < na else wb_ref[c0 - na:c0 - na + tn, :]
        o_ref[:, c0:c0 + tn] = _nt_dot(h_ref[...], w).astype(o_ref.dtype)


def in_proj(x2, g, wa, w_full, layer, *, nb_cols, if_col, tm, tn):
    t, d = x2.shape
    n = wa.shape[0] + nb_cols
    resident = dict(pipeline_mode=pl.Buffered(1))
    return pl.pallas_call(
        functools.partial(_in_proj_body, nb_cols=nb_cols, if_col=if_col, tn=tn),
        grid=(t // tm,),
        in_specs=[
            pl.BlockSpec((tm, d), lambda m: (m, 0)),
            pl.BlockSpec((1, d), lambda m: (0, 0)),
            pl.BlockSpec(wa.shape, lambda m: (0, 0), **resident),
            pl.BlockSpec((None,) + w_full.shape[1:], lambda m: (layer, 0, 0), **resident),
        ],
        out_specs=[
            pl.BlockSpec((tm, n), lambda m: (m, 0)),
            pl.BlockSpec((tm, LANES), lambda m: (m, 0)),
            pl.BlockSpec((SUBLANES, tm), lambda m: (0, m)),
        ],
        out_shape=[jax.ShapeDtypeStruct((t, n), BF16), jax.ShapeDtypeStruct((t, LANES), F32),
                   jax.ShapeDtypeStruct((SUBLANES, t), F32)],
        scratch_shapes=[pltpu.VMEM((tm, d), BF16)],
        compiler_params=_params("arbitrary"),
        name="in_proj",
    )(x2, g, wa, w_full)


MASK_BIAS = -1e30
LOG2_E = 1.4426950408889634


def _moba_body(q_ref, k_ref, v_ref, gq_ref, gk_ref, o_ref,
               kn_ref, vt_ref, kmean_ref, qaug_ref, s_ref, m_ref, alpha_ref, acc_ref,
               *, nb, blk, dh, topk, nheads):
    i = pl.program_id(1)
    pair = 2 * blk
    lane = lax.broadcasted_iota(jnp.int32, (1, LANES), 1)
    head0 = lane < dh

    same_head = (lax.broadcasted_iota(jnp.int32, (LANES, LANES), 0) // dh
                 == lax.broadcasted_iota(jnp.int32, (LANES, LANES), 1) // dh)
    head_ones = jnp.where(same_head, 1.0, 0.0).astype(BF16)

    def head_rms(x, g):
        hi, lo = _split_bf16(x * x)
        ss = (jnp.dot(hi, head_ones, preferred_element_type=F32)
              + jnp.dot(lo, head_ones, preferred_element_type=F32))
        return x * lax.rsqrt(ss * (1.0 / dh) + RMS_EPS) * g

    @pl.when(i == 0)
    def _():
        def prep(j, c):
            r0 = pl.multiple_of(j * blk, blk)
            onehot = jnp.where(lane == dh + j, 1.0, 0.0)
            for p in range(nheads // 2):
                cols = slice(p * LANES, (p + 1) * LANES)
                kn = head_rms(k_ref[pl.ds(r0, blk), cols].astype(F32), gk_ref[...])
                for hh, kh in ((0, kn), (1, pltpu.roll(kn, dh, axis=1))):
                    h = 2 * p + hh
                    kmean_ref[h, pl.ds(j, 1), :] = jnp.mean(jnp.where(head0, kh, 0.0), axis=0, keepdims=True)
                    kn_ref[h, pl.ds(r0, blk), :] = jnp.where(head0, kh, onehot).astype(BF16)
                v_t = v_ref[pl.ds(r0, blk), cols].astype(F32).T.astype(BF16)
                for hh in range(2):
                    vt_ref[2 * p + hh, 0:dh, pl.ds(r0, blk)] = v_t[hh * dh:(hh + 1) * dh, :]
                    vt_ref[2 * p + hh, dh:, pl.ds(r0, blk)] = jnp.ones((vt_ref.shape[1] - dh, blk), BF16)
            return c

        lax.fori_loop(0, nb, prep, 0)

    jidx = lax.broadcasted_iota(jnp.int32, (nb, blk), 0)
    key_i = lax.broadcasted_iota(jnp.int32, (blk, blk), 0)
    qry_i = lax.broadcasted_iota(jnp.int32, (blk, blk), 1)
    causal = key_i <= qry_i
    r_own = pl.multiple_of(i * blk, blk)
    qk_scale = dh ** -0.5 * LOG2_E
    for p in range(nheads // 2):
        cols = slice(p * LANES, (p + 1) * LANES)
        qr_t = q_ref[:, cols].astype(F32).T
        for hh in range(2):
            h = 2 * p + hh
            q_raw = qr_t[hh * dh:(hh + 1) * dh, :]
            q_t = (q_raw * lax.rsqrt(jnp.mean(q_raw * q_raw, axis=0, keepdims=True) + RMS_EPS)
                   * gq_ref[hh * dh:(hh + 1) * dh, :])
            gate = jnp.dot(kmean_ref[h].astype(BF16),
                           jnp.concatenate([q_t, jnp.zeros((LANES - dh, blk), F32)], axis=0).astype(BF16),
                           preferred_element_type=F32)
            rank = jnp.zeros((nb, blk), F32)
            for jp in range(nb):
                row = gate[jp:jp + 1, :]
                beats = (row > gate) | ((row == gate) & (jidx > jp))
                rank = rank + jnp.where(beats, jnp.where(jp < i, 1.0, 0.0), 0.0)
            sel = (rank < topk) & (jidx < i)
            q_s = q_t * qk_scale
            pad = jnp.zeros((LANES - dh - nb, blk), F32)
            qaug_ref[h] = jnp.concatenate([q_s, jnp.where(sel, 0.0, MASK_BIAS), pad], axis=0).astype(BF16)
            qaug_own = jnp.concatenate([q_s, jnp.where(jidx == i, 0.0, MASK_BIAS), pad], axis=0).astype(BF16)
            st = jnp.dot(kn_ref[h, pl.ds(r_own, blk), :], qaug_own, preferred_element_type=F32)
            st = jnp.where(causal, st, -jnp.inf)
            s_ref[h, 0:blk, :] = st
            m_ref[h] = jnp.max(st, axis=0, keepdims=True)

    def finish_own(h):
        pr = jnp.exp2(s_ref[h, 0:blk, :] - m_ref[h]).astype(BF16)
        acc_ref[h] = jnp.dot(vt_ref[h, :, pl.ds(r_own, blk)], pr, preferred_element_type=F32)

    def score_pair(u, h):
        r0 = pl.multiple_of(u * pair, pair)
        st = jnp.dot(kn_ref[h, pl.ds(r0, pair), :], qaug_ref[h], preferred_element_type=F32)
        m_old = m_ref[h]
        m_new = jnp.maximum(m_old, jnp.max(st, axis=0, keepdims=True))
        s_ref[h] = st
        alpha_ref[h] = jnp.exp2(m_old - m_new)
        m_ref[h] = m_new

    def finish_pair(u, h):
        r0 = pl.multiple_of(u * pair, pair)
        pr = jnp.exp2(s_ref[h] - m_ref[h]).astype(BF16)
        acc_ref[h] = alpha_ref[h] * acc_ref[h] + jnp.dot(vt_ref[h, :, pl.ds(r0, pair)], pr,
                                                         preferred_element_type=F32)

    n_pairs = jnp.maximum((i + 1) // 2, 1)
    for h in range(nheads):
        finish_own(h)
        score_pair(0, h)

    def body(u, c):
        for h in range(nheads):
            finish_pair(u - 1, h)
            score_pair(u, h)
        return c

    lax.fori_loop(1, n_pairs, body, 0)
    for h in range(nheads):
        finish_pair(n_pairs - 1, h)

    for p in range(nheads // 2):
        a0 = acc_ref[2 * p]
        a1 = acc_ref[2 * p + 1]
        ot = jnp.concatenate([a0[0:dh] / a0[dh:dh + 1], a1[0:dh] / a1[dh:dh + 1]], axis=0)
        o_ref[:, p * LANES:(p + 1) * LANES] = ot.T.astype(o_ref.dtype)


def moba(proj, gq2, gk2, *, batch, seq):
    nb = seq // MOBA_BLOCK
    blk = MOBA_BLOCK
    gq_t = jnp.broadcast_to(gq2.reshape(LANES, 1), (LANES, blk))
    dh = MOBA_HEAD_DIM
    nheads = MOBA_HEADS
    width = nheads * dh
    wb = width // LANES
    assert dh + nb <= LANES and 2 * dh == LANES and nb % 2 == 0
    v_rows = dh + 2 * SUBLANES
    body = functools.partial(_moba_body, nb=nb, blk=blk, dh=dh, topk=MOBA_TOPK, nheads=nheads)
    return pl.pallas_call(
        body,
        grid=(batch, nb),
        in_specs=[
            pl.BlockSpec((blk, width), lambda b, i: (b * nb + i, COL_QA // wb)),
            pl.BlockSpec((seq, width), lambda b, i: (b, COL_KA // wb)),
            pl.BlockSpec((seq, width), lambda b, i: (b, COL_VA // wb)),
            pl.BlockSpec((LANES, blk), lambda b, i: (0, 0)),
            pl.BlockSpec((1, LANES), lambda b, i: (0, 0)),
        ],
        out_specs=pl.BlockSpec((blk, width), lambda b, i: (b * nb + i, 0)),
        out_shape=jax.ShapeDtypeStruct((batch * seq, width), BF16),
        scratch_shapes=[
            pltpu.VMEM((nheads, seq, LANES), BF16),
            pltpu.VMEM((nheads, v_rows, seq), BF16),
            pltpu.VMEM((nheads, nb, LANES), F32),
            pltpu.VMEM((nheads, LANES, blk), BF16),
            pltpu.VMEM((nheads, 2 * blk, blk), F32),
            pltpu.VMEM((nheads, 1, blk), F32),
            pltpu.VMEM((nheads, 1, blk), F32),
            pltpu.VMEM((nheads, v_rows, blk), F32),
        ],
        compiler_params=_params("arbitrary", "arbitrary"),
        name="moba",
    )(proj, proj, proj, gq_t, gk2)


def _log_sigmoid(x):
    return jnp.minimum(x, 0.0) - jnp.log(1.0 + jnp.exp(-jnp.abs(x)))


def _dot_tri(tri, x, tri_left):
    out = None
    for _ in range(3):
        piece = x.astype(BF16)
        x = x - piece.astype(F32)
        term = (jnp.dot(tri, piece, preferred_element_type=F32) if tri_left
                else jnp.dot(piece, tri, preferred_element_type=F32))
        out = term if out is None else out + term
    return out


def _mlstm_body(qr_ref, kr_ref, v_ref, og_ref, gcol_ref, grow_ref, brow_ref, bcol_ref,
                cwq_ref, cwk_ref, cbq_ref, cbk_ref, gh_ref, o_ref,
                qx_ref, kx_ref, c_ref, m_ref, *, chunk, dk, nh):
    L = chunk
    width = nh * dk

    @pl.when(pl.program_id(1) == 0)
    def _():
        qx_ref[0:SUBLANES, :] = jnp.zeros((SUBLANES, width), F32)
        kx_ref[0:SUBLANES, :] = jnp.zeros((SUBLANES, width), F32)
        c_ref[...] = jnp.zeros_like(c_ref)
        m_ref[...] = jnp.zeros_like(m_ref)

    qx_ref[SUBLANES:SUBLANES + L, :] = qr_ref[...].astype(F32)
    kx_ref[SUBLANES:SUBLANES + L, :] = kr_ref[...].astype(F32)

    def conv_silu(x_ref, w_ref, b_ref):
        acc = b_ref[...] + w_ref[0:1, :] * x_ref[pl.ds(SUBLANES - CONV_WIDTH + 1, L), :]
        for j in range(1, CONV_WIDTH):
            acc = acc + w_ref[j:j + 1, :] * x_ref[pl.ds(SUBLANES - CONV_WIDTH + 1 + j, L), :]
        return acc * _sigmoid(acc)

    q_all = conv_silu(qx_ref, cwq_ref, cbq_ref)
    k_all = conv_silu(kx_ref, cwk_ref, cbk_ref) * (dk ** -0.5)
    qx_ref[0:SUBLANES, :] = qx_ref[L:L + SUBLANES, :]
    kx_ref[0:SUBLANES, :] = kx_ref[L:L + SUBLANES, :]

    pre_col = gcol_ref[...] + brow_ref[...]
    pre_row = grow_ref[...] + bcol_ref[...]
    t_i = lax.broadcasted_iota(jnp.int32, (L, L), 0)
    s_i = lax.broadcasted_iota(jnp.int32, (L, L), 1)
    tril = s_i <= t_i
    bcum_cols = _dot_tri(jnp.where(tril, 1.0, 0.0).astype(BF16), _log_sigmoid(pre_col), True)
    bcum_rows = _dot_tri(jnp.where(t_i <= s_i, 1.0, 0.0).astype(BF16), _log_sigmoid(pre_row), False)
    src_before_out = t_i <= s_i
    v_t = v_ref[...].astype(F32).T
    ones_t = jnp.ones((dk, L), F32)

    for h in range(nh):
        cols = slice(h * dk, (h + 1) * dk)
        qb = q_all[:, cols].astype(BF16)
        kb = k_all[:, cols].astype(BF16)
        i_col = pre_col[:, h:h + 1]
        i_row = pre_row[h:h + 1, :]
        bcum_col = bcum_cols[:, nh + h:nh + h + 1]
        bcum_row = bcum_rows[nh + h:nh + h + 1, :]

        m_prev = m_ref[h, 0:1, 0:1]
        a_row = bcum_row + m_prev
        dmat_t = jnp.where(src_before_out, bcum_row + (i_col - bcum_col), -jnp.inf)
        m_t = jnp.maximum(a_row, jnp.max(dmat_t, axis=0, keepdims=True))
        sqk_t = (_nt_dot(kb, qb) * jnp.exp(dmat_t - m_t)).astype(BF16)

        v_aug_t = jnp.concatenate([v_t[cols, :], ones_t], axis=0)
        state_t = c_ref[h]
        num_aug_t = (jnp.exp(a_row - m_t) * _nt_dot(state_t.astype(BF16), qb)
                     + jnp.dot(v_aug_t.astype(BF16), sqk_t, preferred_element_type=F32))
        den = num_aug_t[dk:dk + 1, :]
        hc_t = num_aug_t[0:dk, :] / jnp.maximum(jnp.abs(den), jnp.exp(-m_t))
        hn_t = hc_t * lax.rsqrt(jnp.mean(hc_t * hc_t, axis=0, keepdims=True) + RMS_EPS)
        o_ref[:, cols] = (hn_t.T * gh_ref[...] * _sigmoid(og_ref[:, cols].astype(F32))).astype(o_ref.dtype)

        b_last = bcum_row[:, L - 1:L]
        g_row = b_last - bcum_row + i_row
        m_new = jnp.maximum(b_last + m_prev, jnp.max(g_row, axis=-1, keepdims=True))
        w_c = jnp.exp(b_last + m_prev - m_new)
        vw_t = (v_aug_t * jnp.exp(g_row - m_new)).astype(BF16)
        c_ref[h] = w_c * state_t + jnp.dot(vw_t, kb, preferred_element_type=F32)
        m_ref[h] = jnp.broadcast_to(m_new, (1, LANES))


def mlstm(proj, gates_col, gates_row, bias_row, bias_col, conv_w, conv_b, gh, *, batch, seq, chunk):
    nh = MLSTM_HEADS
    assert 2 * nh == SUBLANES
    dk = MLSTM_DIM
    width = nh * dk
    wb = width // LANES
    nc = seq // chunk
    body = functools.partial(_mlstm_body, chunk=chunk, dk=dk, nh=nh)

    def rows(col0):
        return pl.BlockSpec((chunk, width), lambda b, c: (b * nc + c, col0 // wb))

    return pl.pallas_call(
        body,
        grid=(batch, nc),
        in_specs=[
            rows(COL_QM), rows(COL_KM), rows(COL_VM), rows(COL_OM),
            pl.BlockSpec((chunk, LANES), lambda b, c: (b * nc + c, 0)),
            pl.BlockSpec((SUBLANES, chunk), lambda b, c: (0, b * nc + c)),
            pl.BlockSpec((1, LANES), lambda b, c: (0, 0)),
            pl.BlockSpec((SUBLANES, 1), lambda b, c: (0, 0)),
            pl.BlockSpec((CONV_WIDTH, width), lambda b, c: (0, 0)),
            pl.BlockSpec((CONV_WIDTH, width), lambda b, c: (0, 1)),
            pl.BlockSpec((1, width), lambda b, c: (0, 0)),
            pl.BlockSpec((1, width), lambda b, c: (0, 1)),
            pl.BlockSpec((1, LANES), lambda b, c: (0, 0)),
        ],
        out_specs=pl.BlockSpec((chunk, width), lambda b, c: (b * nc + c, 0)),
        out_shape=jax.ShapeDtypeStruct((batch * seq, width), BF16),
        scratch_shapes=[
            pltpu.VMEM((chunk + 2 * SUBLANES, width), F32),
            pltpu.VMEM((chunk + 2 * SUBLANES, width), F32),
            pltpu.VMEM((nh, 2 * dk, dk), F32),
            pltpu.VMEM((nh, 1, LANES), F32),
        ],
        compiler_params=_params("arbitrary", "arbitrary"),
        name="mlstm",
    )(proj, proj, proj, proj, gates_col, gates_row, bias_row, bias_col,
      conv_w, conv_w, conv_b, conv_b, gh)


def _merge_body(ya_ref, yb_ref, ga_ref, gb_ref, x_ref, woa_ref, wob_ref, wout_ref, gffn_ref, *rest, moe):
    a = jnp.dot(ya_ref[...], woa_ref[...], preferred_element_type=F32)
    b = jnp.dot(yb_ref[...], wob_ref[...], preferred_element_type=F32)
    mixed = _sigmoid(ga_ref[...].astype(F32)) * a + _sigmoid(gb_ref[...].astype(F32)) * b
    x1 = x_ref[...] + jnp.dot(mixed.astype(BF16), wout_ref[...], preferred_element_type=F32)
    hf = _rms(x1, gffn_ref[...])
    if moe:
        wr_ref, x1_ref, hf_ref, lg_ref = rest
        _store_row_tiles(hf_ref, hf)
        lg_ref[...] = jnp.dot(hf.astype(BF16), wr_ref[...].astype(BF16), preferred_element_type=F32)
    else:
        x1_ref, hf_ref = rest
        hf_ref[...] = hf.astype(BF16)
    x1_ref[...] = x1


def merge(ya, yb, proj, x2, woa, wob, wout, gffn, wr, *, tm):
    t, d = x2.shape
    moe = wr is not None
    full = lambda m: (0, 0)
    in_specs = [
        pl.BlockSpec((tm, ya.shape[1]), lambda m: (m, 0)),
        pl.BlockSpec((tm, yb.shape[1]), lambda m: (m, 0)),
        pl.BlockSpec((tm, d), lambda m: (m, COL_GA * LANES // d)),
        pl.BlockSpec((tm, d), lambda m: (m, COL_GB * LANES // d)),
        pl.BlockSpec((tm, d), lambda m: (m, 0)),
        pl.BlockSpec(woa.shape, full), pl.BlockSpec(wob.shape, full), pl.BlockSpec(wout.shape, full),
        pl.BlockSpec((1, d), full),
    ]
    args = [ya, yb, proj, proj, x2, woa, wob, wout, gffn]
    out_specs = [pl.BlockSpec((tm, d), lambda m: (m, 0)), pl.BlockSpec((tm, d), lambda m: (m, 0))]
    out_shape = [jax.ShapeDtypeStruct((t, d), F32), jax.ShapeDtypeStruct((t, d), BF16)]
    if moe:
        out_specs[1] = pl.BlockSpec((tm * SUBLANES, LANES), lambda m: (m, 0))
        out_shape[1] = jax.ShapeDtypeStruct((t * SUBLANES, LANES), F32)
    if moe:
        in_specs.append(pl.BlockSpec(wr.shape, full))
        args.append(wr)
        out_specs.append(pl.BlockSpec((tm, LANES), lambda m: (m, 0)))
        out_shape.append(jax.ShapeDtypeStruct((t, LANES), F32))
    return pl.pallas_call(
        functools.partial(_merge_body, moe=moe),
        grid=(t // tm,),
        in_specs=in_specs, out_specs=out_specs, out_shape=out_shape,
        compiler_params=_params("arbitrary"),
        name="merge_moe" if moe else "merge",
    )(*args)


def _swiglu_gate(x, w1_ref, w3_ref, g_ref, fc):
    dff = g_ref.shape[1]
    for f0 in range(0, dff, fc):
        a = jnp.dot(x, w1_ref[:, f0:f0 + fc].astype(BF16), preferred_element_type=F32)
        b = jnp.dot(x, w3_ref[:, f0:f0 + fc].astype(BF16), preferred_element_type=F32)
        g_ref[:, f0:f0 + fc] = (a * _sigmoid(a) * b).astype(BF16)


def _swiglu(x, w1_ref, w3_ref, w2_ref, g_ref, fc):
    _swiglu_gate(x, w1_ref, w3_ref, g_ref, fc)
    return jnp.dot(g_ref[...], w2_ref[...].astype(BF16), preferred_element_type=F32)


def _ple(x, p_ref, g_ref, wg_ref, wp_ref):
    gate = _sigmoid(jnp.dot(_rms(x, g_ref[...]).astype(BF16), wg_ref[...], preferred_element_type=F32))
    emb = jnp.dot(p_ref[...].astype(BF16), wp_ref[...], preferred_element_type=F32)
    return x + gate * emb


def _ffn_body(hf_ref, x1_ref, w1_ref, w3_ref, w2_ref, p_ref, g_ref, wg_ref, wp_ref, o_ref, act_ref, *, fc):
    x2 = x1_ref[...] + _swiglu(hf_ref[...], w1_ref, w3_ref, w2_ref, act_ref, fc)
    o_ref[...] = _ple(x2, p_ref, g_ref, wg_ref, wp_ref)


def dense_ffn(hf, x1, w1, w3, w2, p2, p_row0, g, wg, wp, *, tm, fc):
    t, d = x1.shape
    dff = w1.shape[1]
    p_blk0 = p_row0 // tm
    resident = dict(pipeline_mode=pl.Buffered(1))
    full = lambda m: (0, 0)
    return pl.pallas_call(
        functools.partial(_ffn_body, fc=fc),
        grid=(t // tm,),
        in_specs=[
            pl.BlockSpec((tm, d), lambda m: (m, 0)),
            pl.BlockSpec((tm, d), lambda m: (m, 0)),
            pl.BlockSpec((d, dff), full, **resident),
            pl.BlockSpec((d, dff), full, **resident),
            pl.BlockSpec((dff, d), full, **resident),
            pl.BlockSpec((tm, p2.shape[1]), lambda m: (p_blk0 + m, 0)),
            pl.BlockSpec((1, d), full),
            pl.BlockSpec(wg.shape, full, **resident),
            pl.BlockSpec(wp.shape, full, **resident),
        ],
        out_specs=pl.BlockSpec((tm, d), lambda m: (m, 0)),
        out_shape=jax.ShapeDtypeStruct((t, d), F32),
        scratch_shapes=[pltpu.VMEM((tm, dff), BF16)],
        compiler_params=_params("arbitrary"),
        name="dense_ffn",
    )(hf, x1, w1, w3, w2, p2, g, wg, wp)


META_E0, META_E1, META_G0, META_G1, META_R0, META_R1 = 0, 1, 2, 3, 4, 5


def _route_body(lg_ref, meta_ref, tab_ref, cnt_ref, carry_ref, *, tm, ne):
    @pl.when(pl.program_id(0) == 0)
    def _():
        carry_ref[...] = jnp.zeros_like(carry_ref)

    lane = lax.broadcasted_iota(jnp.int32, (tm, LANES), 1)
    lanef = lane.astype(F32)
    lg = jnp.where(lane < ne, lg_ref[...], -jnp.inf)
    m1 = jnp.max(lg, axis=-1, keepdims=True)
    e1 = jnp.min(jnp.where(lg == m1, lanef, float(LANES)), axis=-1, keepdims=True)
    lg2 = jnp.where(lanef == e1, -jnp.inf, lg)
    m2 = jnp.max(lg2, axis=-1, keepdims=True)
    e2 = jnp.min(jnp.where(lg2 == m2, lanef, float(LANES)), axis=-1, keepdims=True)
    ex = jnp.exp(m2 - m1)
    g1 = 1.0 / (1.0 + ex)
    g2 = ex / (1.0 + ex)
    onehot = jnp.where((lanef == e1) | (lanef == e2), 1.0, 0.0)
    row = lax.broadcasted_iota(jnp.int32, (tm, tm), 0)
    col = lax.broadcasted_iota(jnp.int32, (tm, tm), 1)
    before = jnp.dot(jnp.where(col < row, 1.0, 0.0).astype(BF16), onehot.astype(BF16),
                     preferred_element_type=F32) + carry_ref[...]
    r1 = jnp.sum(jnp.where(lanef == e1, before, 0.0), axis=-1, keepdims=True)
    r2 = jnp.sum(jnp.where(lanef == e2, before, 0.0), axis=-1, keepdims=True)
    carry_ref[...] += jnp.sum(onehot, axis=0, keepdims=True)
    meta = jnp.zeros((tm, LANES), F32)
    for pos, val in ((META_E0, e1), (META_E1, e2), (META_G0, g1), (META_G1, g2), (META_R0, r1), (META_R1, r2)):
        meta = jnp.where(lane == pos, val, meta)
    meta_ref[...] = meta
    tab_ref[...] = meta.T[0:SUBLANES, :]
    cnt_ref[...] = jnp.broadcast_to(carry_ref[...], cnt_ref.shape)


def route(logits, *, tm):
    t = logits.shape[0]
    return pl.pallas_call(
        functools.partial(_route_body, tm=tm, ne=N_EXPERTS),
        grid=(t // tm,),
        in_specs=[pl.BlockSpec((tm, LANES), lambda m: (m, 0))],
        out_specs=[pl.BlockSpec((tm, LANES), lambda m: (m, 0)),
                   pl.BlockSpec((SUBLANES, tm), lambda m: (0, m)),
                   pl.BlockSpec((SUBLANES, LANES), lambda m: (0, 0))],
        out_shape=[jax.ShapeDtypeStruct((t, LANES), F32), jax.ShapeDtypeStruct((SUBLANES, t), F32),
                   jax.ShapeDtypeStruct((SUBLANES, LANES), F32)],
        scratch_shapes=[pltpu.VMEM((1, LANES), F32)],
        compiler_params=_params("arbitrary"),
        name="route",
    )(logits)


def _dispatch_body(dest_ref, pe_ref, na_ref, hf_ref, wsrc_ref, xs_ref, wdst_ref, zero_ref, sem, zsem,
                   *, tm, topk, tm_rows, n_tok):
    base = pl.program_id(0) * tm

    @pl.when(pl.program_id(0) == 0)
    def _():
        zero_ref[...] = jnp.zeros_like(zero_ref)
        n_tiles = xs_ref.shape[0] // (tm_rows * SUBLANES)
        fills = []
        for e in range(N_EXPERTS):
            end = pe_ref[e]
            nonempty = end > (pe_ref[e - 1] if e else 0)
            fills.append((nonempty, pl.multiple_of(jnp.maximum(end - tm_rows, 0), tm_rows)))
        for tile in range(n_tiles):
            fills.append((tile >= na_ref[0], tile * tm_rows))
        for phase in ("start", "wait"):
            for cond, row0 in fills:
                @pl.when(cond)
                def _(row0=row0, phase=phase):
                    cp = pltpu.make_async_copy(
                        zero_ref, xs_ref.at[pl.ds(row0 * SUBLANES, tm_rows * SUBLANES)], zsem)
                    cp.start() if phase == "start" else cp.wait()

    def issue(r, c):
        for k in range(topk):
            d = dest_ref[k * n_tok + base + r]
            pltpu.make_async_copy(_row_tile(hf_ref, r), _row_tile(xs_ref, d), sem).start(priority=k % 2)
        return c

    lax.fori_loop(0, tm, issue, 0, unroll=ROW_DMA_UNROLL)
    wdst_ref[...] = wsrc_ref[...].astype(wdst_ref.dtype)
    for k in range(topk):
        pltpu.make_async_copy(hf_ref, xs_ref.at[pl.ds(0, tm * SUBLANES)], sem).wait()


def _row_tile(ref, r):
    return ref.at[pl.ds(pl.multiple_of(r * SUBLANES, SUBLANES), SUBLANES)]


def dispatch(dest, pad_end, n_active, hf, w_f32, *, n_rows, tm, topk, tm_rows):
    t = hf.shape[0] // SUBLANES
    n_steps = t // tm
    w2d = w_f32.reshape(-1, w_f32.shape[-1])
    slab = w2d.shape[0] // n_steps
    assert slab * n_steps == w2d.shape[0] and slab % (2 * SUBLANES) == 0
    xs, w_bf = pl.pallas_call(
        functools.partial(_dispatch_body, tm=tm, topk=topk, tm_rows=tm_rows, n_tok=t),
        grid_spec=pltpu.PrefetchScalarGridSpec(
            num_scalar_prefetch=3,
            grid=(n_steps,),
            in_specs=[pl.BlockSpec((tm * SUBLANES, LANES), lambda m, *_: (m, 0)),
                      pl.BlockSpec((slab, w2d.shape[1]), lambda m, *_: (m, 0))],
            out_specs=[pl.BlockSpec(memory_space=pl.ANY),
                       pl.BlockSpec((slab, w2d.shape[1]), lambda m, *_: (m, 0))],
            scratch_shapes=[pltpu.VMEM((tm_rows * SUBLANES, LANES), F32),
                            pltpu.SemaphoreType.DMA, pltpu.SemaphoreType.DMA],
        ),
        out_shape=[jax.ShapeDtypeStruct((n_rows * SUBLANES, LANES), F32),
                   jax.ShapeDtypeStruct(w2d.shape, BF16)],
        compiler_params=_params("arbitrary"),
        name="moe_dispatch",
    )(dest, pad_end, n_active, hf, w2d)
    return xs, w_bf.reshape(w_f32.shape)


def _experts_body(te_ref, na_ref, xs_ref, w1_hbm, w3_hbm, w2_hbm, y_ref,
                  xb_ref, g_ref, w1_buf, w3_buf, w2_buf, wsem, *, fc):
    i = pl.program_id(0)
    na = na_ref[0]
    tm = xb_ref.shape[0]
    tf = g_ref.shape[2]

    def weight_copies(tile, half):
        e = te_ref[tile]
        cols = pl.ds(half * tf, tf)
        return (pltpu.make_async_copy(w1_hbm.at[e, :, cols], w1_buf.at[half], wsem.at[half, 0]),
                pltpu.make_async_copy(w3_hbm.at[e, :, cols], w3_buf.at[half], wsem.at[half, 1]),
                pltpu.make_async_copy(w2_hbm.at[e, cols, :], w2_buf.at[half], wsem.at[half, 2]))

    def start(tile, half):
        for cp in weight_copies(tile, half):
            cp.start()

    def wait(half):
        for cp in weight_copies(0, half):
            cp.wait()

    @pl.when(i >= na)
    def _():
        y_ref[...] = jnp.zeros_like(y_ref)

    @pl.when(i < na)
    def _():
        first_of_expert = (i == 0) | (te_ref[i] != te_ref[jnp.maximum(i - 1, 0)])
        nxt = jnp.minimum(i + 1, na - 1)
        next_is_new_expert = (i + 1 < na) & (te_ref[nxt] != te_ref[i])

        @pl.when(i == 0)
        def _():
            start(0, 0)

        @pl.when(first_of_expert)
        def _():
            start(i, 1)
            wait(0)

        xb_ref[...] = _load_row_tiles(xs_ref, tm).astype(BF16)
        xb = xb_ref[...]
        _swiglu_gate(xb, w1_buf.at[0], w3_buf.at[0], g_ref.at[0], fc)
        acc = jnp.dot(g_ref[0], w2_buf[0].astype(BF16), preferred_element_type=F32)

        @pl.when(next_is_new_expert)
        def _():
            start(nxt, 0)

        @pl.when(first_of_expert)
        def _():
            wait(1)

        _swiglu_gate(xb, w1_buf.at[1], w3_buf.at[1], g_ref.at[1], fc)
        y = acc + jnp.dot(g_ref[1], w2_buf[1].astype(BF16), preferred_element_type=F32)
        _store_row_tiles(y_ref, y)


def experts(tile_expert, n_active, xs, w1, w3, w2, *, tm, fc):
    n_rows = xs.shape[0] // SUBLANES
    d = w1.shape[1]
    dff = w1.shape[2]
    tf = dff // 2
    return pl.pallas_call(
        functools.partial(_experts_body, fc=fc),
        grid_spec=pltpu.PrefetchScalarGridSpec(
            num_scalar_prefetch=2,
            grid=(n_rows // tm,),
            in_specs=[
                pl.BlockSpec((tm * SUBLANES, LANES), lambda i, te, na: (jnp.minimum(i, na[0] - 1), 0)),
                pl.BlockSpec(memory_space=pl.ANY),
                pl.BlockSpec(memory_space=pl.ANY),
                pl.BlockSpec(memory_space=pl.ANY),
            ],
            out_specs=pl.BlockSpec((tm * SUBLANES, LANES), lambda i, te, na: (i, 0)),
            scratch_shapes=[
                pltpu.VMEM((tm, d), BF16),
                pltpu.VMEM((2, tm, tf), BF16),
                pltpu.VMEM((2, d, tf), w1.dtype),
                pltpu.VMEM((2, d, tf), w3.dtype),
                pltpu.VMEM((2, tf, d), w2.dtype),
                pltpu.SemaphoreType.DMA((2, 3)),
            ],
        ),
        out_shape=jax.ShapeDtypeStruct((n_rows * SUBLANES, LANES), F32),
        compiler_params=_params("arbitrary", vmem_limit=EXPERTS_VMEM_LIMIT),
        name="moe_experts",
    )(tile_expert, n_active, xs, w1, w3, w2)


def _combine_body(dest_ref, x1_ref, meta_ref, p_ref, g_ref, wg_ref, wp_ref, y_ref, o_ref, buf_ref, sem,
                  *, tm, topk):
    m = pl.program_id(0)
    n_steps = pl.num_programs(0)
    n_tok = n_steps * tm

    def start_gather(tile, slot):
        def issue(r, c):
            for k in range(topk):
                d = dest_ref[k * n_tok + tile * tm + r]
                pltpu.make_async_copy(_row_tile(y_ref, d), _row_tile(buf_ref.at[slot, k], r),
                                      sem.at[slot]).start(priority=k % 2)
            return c

        lax.fori_loop(0, tm, issue, 0, unroll=ROW_DMA_UNROLL)

    @pl.when(m == 0)
    def _():
        start_gather(0, 0)

    @pl.when(m + 1 < n_steps)
    def _():
        start_gather(m + 1, (m + 1) % 2)

    slot = m % 2
    for k in range(topk):
        pltpu.make_async_copy(y_ref.at[pl.ds(0, tm * SUBLANES)], buf_ref.at[slot, k], sem.at[slot]).wait()
    meta = meta_ref[...]
    g0 = meta[:, META_G0:META_G0 + 1]
    g1 = meta[:, META_G1:META_G1 + 1]
    x2 = x1_ref[...] + (g0 * _load_row_tiles(buf_ref.at[slot, 0], tm)
                        + g1 * _load_row_tiles(buf_ref.at[slot, 1], tm))
    o_ref[...] = _ple(x2, p_ref, g_ref, wg_ref, wp_ref)


def combine(dest, x1, meta, y, p2, p_row0, g, wg, wp, *, tm, topk):
    t, d = x1.shape
    full = lambda m, dest: (0, 0)
    p_blk0 = p_row0 // tm
    return pl.pallas_call(
        functools.partial(_combine_body, tm=tm, topk=topk),
        grid_spec=pltpu.PrefetchScalarGridSpec(
            num_scalar_prefetch=1,
            grid=(t // tm,),
            in_specs=[pl.BlockSpec((tm, d), lambda m, dest: (m, 0)),
                      pl.BlockSpec((tm, LANES), lambda m, dest: (m, 0)),
                      pl.BlockSpec((tm, p2.shape[1]), lambda m, dest: (p_blk0 + m, 0)),
                      pl.BlockSpec((1, d), full), pl.BlockSpec(wg.shape, full), pl.BlockSpec(wp.shape, full),
                      pl.BlockSpec(memory_space=pl.ANY)],
            out_specs=pl.BlockSpec((tm, d), lambda m, dest: (m, 0)),
            scratch_shapes=[pltpu.VMEM((2, topk, tm * SUBLANES, LANES), F32), pltpu.SemaphoreType.DMA((2,))],
        ),
        out_shape=jax.ShapeDtypeStruct((t, d), F32),
        compiler_params=_params("arbitrary"),
        name="moe_combine",
    )(dest, x1, meta, p2, g, wg, wp, y)


def moe_ffn(hf, x1, logits, w1, w3, w2, ple_args, *, tm_route, tm_rows, tm_dispatch, tm_combine):
    t, d = x1.shape
    topk = 2
    meta, tab, cnt = route(logits, tm=tm_route)
    counts = cnt[0, :N_EXPERTS].astype(jnp.int32)
    padded = ((counts + tm_rows - 1) // tm_rows) * tm_rows
    pad_end = jnp.cumsum(padded).astype(jnp.int32)
    pad_start = pad_end - padded
    eidx = tab[META_E0:META_E1 + 1].astype(jnp.int32)
    rank = tab[META_R0:META_R1 + 1].astype(jnp.int32)
    dest = rank
    for e in range(N_EXPERTS):
        dest = dest + jnp.where(eidx == e, pad_start[e], 0)
    dest = dest.reshape(topk * t)
    n_tiles = -(-(t * topk) // tm_rows) + N_EXPERTS
    tile_start = jnp.arange(n_tiles, dtype=jnp.int32) * tm_rows
    tile_expert = jnp.minimum(jnp.sum(tile_start[:, None] >= pad_end[None, :], axis=1),
                              N_EXPERTS - 1).astype(jnp.int32)
    n_active = pad_end[N_EXPERTS - 1:] // tm_rows
    xs, w3_bf = dispatch(dest, pad_end, n_active, hf, w3, n_rows=n_tiles * tm_rows, tm=tm_dispatch, topk=topk,
                         tm_rows=tm_rows)
    y = experts(tile_expert, n_active, xs, w1, w3_bf, w2, tm=tm_rows, fc=FC_SWIGLU)
    return combine(dest, x1, meta, y, *ple_args, tm=tm_combine, topk=topk)


def _tile2(g):
    return jnp.concatenate([g, g]).reshape(1, 2 * g.shape[0])


def kernel(x, p, g_mix, w_in, g_q, g_k, conv_w, conv_b, b_i, b_f, g_h, w_oa, w_ob, w_out, g_ffn, w_d1, w_d3,
           w_d2, w_router, w_e1, w_e3, w_e2, g_ple, w_ple_gate, w_ple_proj):
    batch, seq, d = x.shape
    depth = w_in.shape[0]
    t = batch * seq
    nh = MLSTM_HEADS
    x2 = x.reshape(t, d)
    c_q, c_k, c_v = 0, 512, 1024
    c_qk, c_vm, c_om, c_i, c_f, c_ga, c_gb, c_end = 1536, 2560, 3072, 3584, 3588, 3592, 4616, 5640

    w_in_t = jnp.swapaxes(w_in, 1, 2).astype(BF16)
    for l in range(depth):
        w_gates = w_in_t[l, c_ga:c_end]
        proj, gif, gif_t = in_proj(x2, g_mix[l].reshape(1, d), w_gates, w_in_t, l, nb_cols=c_i, if_col=c_i,
                                   tm=TM_IN_PROJ, tn=TN_IN_PROJ)

        ya = moba(proj, _tile2(g_q[l]), _tile2(g_k[l]), batch=batch, seq=seq)

        bias = jnp.concatenate([b_i[l], b_f[l]])
        bias_row = jnp.pad(bias, (0, LANES - 2 * nh)).reshape(1, LANES)
        bias_col = bias.reshape(2 * nh, 1)
        yb = mlstm(proj, gif, gif_t, bias_row, bias_col, conv_w[l], conv_b[l].reshape(1, -1),
                   g_h[l].reshape(1, -1), batch=batch, seq=seq, chunk=MLSTM_CHUNK)

        j = l // 2
        moe = l % 2 == 1
        wr = jnp.pad(w_router[j], ((0, 0), (0, LANES - N_EXPERTS))) if moe else None
        outs = merge(ya, yb, proj, x2, w_oa[l].astype(BF16), w_ob[l].astype(BF16), w_out[l].astype(BF16),
                     g_ffn[l].reshape(1, d), wr, tm=TM_MERGE)
        ple_args = (p.reshape(depth * t, -1), l * t, g_ple[l].reshape(1, d), w_ple_gate[l].astype(BF16),
                    w_ple_proj[l].astype(BF16))
        if moe:
            x1, hf, logits = outs
            x2 = moe_ffn(hf, x1, logits, w_e1[j], w_e3[j], w_e2[j],
                         ple_args, tm_route=TM_ROUTE, tm_rows=TM_EXPERT_ROWS, tm_dispatch=TM_DISPATCH,
                         tm_combine=TM_COMBINE)
        else:
            x1, hf = outs
            x2 = dense_ffn(hf, x1, w_d1[j].astype(BF16), w_d3[j].astype(BF16), w_d2[j].astype(BF16),
                           *ple_args, tm=TM_DENSE, fc=FC_SWIGLU)
    return x2.reshape(batch, seq, d)
```

```python
import functools

import jax
import jax.numpy as jnp
from jax import lax
from jax.experimental import pallas as pl
from jax.experimental.pallas import tpu as pltpu

F32 = jnp.float32
BF16 = jnp.bfloat16

RMS_EPS = 1e-6
LANES = 128
SUBLANES = 8

MOBA_HEADS = 8
MOBA_HEAD_DIM = 64
MOBA_BLOCK = 256
MOBA_TOPK = 3
MLSTM_HEADS = 4
MLSTM_DIM = 128
CONV_WIDTH = 4
N_EXPERTS = 8

COL_GA, COL_GB = 0, 8
COL_QA, COL_KA, COL_VA = 16, 20, 24
COL_QM, COL_KM, COL_VM, COL_OM = 28, 32, 36, 40

VMEM_LIMIT = 56 * 1024 * 1024
EXPERTS_VMEM_LIMIT = 60 * 1024 * 1024
ROW_DMA_UNROLL = 16

TM_IN_PROJ, TN_IN_PROJ = 1024, 512
MLSTM_CHUNK = 256
TM_MERGE = 1024
TM_DENSE = 1024
FC_SWIGLU = 256
TM_ROUTE = 1024
TM_EXPERT_ROWS = 512
TM_DISPATCH, TM_COMBINE = 1024, 512


def _params(*sem, vmem_limit=VMEM_LIMIT):
    return pltpu.CompilerParams(dimension_semantics=sem, vmem_limit_bytes=vmem_limit)


def _sigmoid(x):
    return 1.0 / (1.0 + jnp.exp(-x))


def _rms(x, g):
    return x * lax.rsqrt(jnp.mean(x * x, axis=-1, keepdims=True) + RMS_EPS) * g


def _split_bf16(x):
    hi = x.astype(BF16)
    return hi, (x - hi.astype(F32)).astype(BF16)


def _store_row_tiles(ref, x):
    ref[...] = x.reshape(x.shape[0] * SUBLANES, LANES)


def _load_row_tiles(ref, rows):
    return ref[...].reshape(rows, SUBLANES * LANES)


def _nt_dot(a, b, **kw):
    return lax.dot_general(a, b, (((1,), (1,)), ((), ())), preferred_element_type=F32, **kw)


def _in_proj_body(x_ref, g_ref, wa_ref, wb_ref, o_ref, oif_ref, oif_t_ref, h_ref, *, nb_cols, if_col, tn):
    h_ref[...] = _rms(x_ref[...], g_ref[...]).astype(BF16)
    gates = _nt_dot(h_ref[...], wb_ref[if_col:if_col + LANES, :])
    oif_ref[...] = gates
    oif_t_ref[...] = gates.T[0:SUBLANES, :]
    na = wa_ref.shape[0]
    for c0 in range(0, na + nb_cols, tn):
        w = wa_ref[c0:c0 + tn, :] if c0 < na else wb_ref[c0 - na:c0 - na + tn, :]
        o_ref[:, c0:c0 + tn] = _nt_dot(h_ref[...], w).astype(o_ref.dtype)


def in_proj(x2, g, wa, w_full, layer, *, nb_cols, if_col, tm, tn):
    t, d = x2.shape
    n = wa.shape[0] + nb_cols
    resident = dict(pipeline_mode=pl.Buffered(1))
    return pl.pallas_call(
        functools.partial(_in_proj_body, nb_cols=nb_cols, if_col=if_col, tn=tn),
        grid=(t // tm,),
        in_specs=[
            pl.BlockSpec((tm, d), lambda m: (m, 0)),
            pl.BlockSpec((1, d), lambda m: (0, 0)),
            pl.BlockSpec(wa.shape, lambda m: (0, 0), **resident),
            pl.BlockSpec((None,) + w_full.shape[1:], lambda m: (layer, 0, 0), **resident),
        ],
        out_specs=[
            pl.BlockSpec((tm, n), lambda m: (m, 0)),
            pl.BlockSpec((tm, LANES), lambda m: (m, 0)),
            pl.BlockSpec((SUBLANES, tm), lambda m: (0, m)),
        ],
        out_shape=[jax.ShapeDtypeStruct((t, n), BF16), jax.ShapeDtypeStruct((t, LANES), F32),
                   jax.ShapeDtypeStruct((SUBLANES, t), F32)],
        scratch_shapes=[pltpu.VMEM((tm, d), BF16)],
        compiler_params=_params("arbitrary"),
        name="in_proj",
    )(x2, g, wa, w_full)


MASK_BIAS = -1e30
LOG2_E = 1.4426950408889634


def _moba_body(q_ref, k_ref, v_ref, gq_ref, gk_ref, o_ref,
               kn_ref, vt_ref, kmean_ref, qaug_ref, s_ref, m_ref, alpha_ref, acc_ref,
               *, nb, blk, dh, topk, nheads):
    i = pl.program_id(1)
    pair = 2 * blk
    lane = lax.broadcasted_iota(jnp.int32, (1, LANES), 1)
    head0 = lane < dh

    same_head = (lax.broadcasted_iota(jnp.int32, (LANES, LANES), 0) // dh
                 == lax.broadcasted_iota(jnp.int32, (LANES, LANES), 1) // dh)
    head_ones = jnp.where(same_head, 1.0, 0.0).astype(BF16)

    def head_rms(x, g):
        hi, lo = _split_bf16(x * x)
        ss = (jnp.dot(hi, head_ones, preferred_element_type=F32)
              + jnp.dot(lo, head_ones, preferred_element_type=F32))
        return x * lax.rsqrt(ss * (1.0 / dh) + RMS_EPS) * g

    @pl.when(i == 0)
    def _():
        def prep(j, c):
            r0 = pl.multiple_of(j * blk, blk)
            onehot = jnp.where(lane == dh + j, 1.0, 0.0)
            for p in range(nheads // 2):
                cols = slice(p * LANES, (p + 1) * LANES)
                kn = head_rms(k_ref[pl.ds(r0, blk), cols].astype(F32), gk_ref[...])
                for hh, kh in ((0, kn), (1, pltpu.roll(kn, dh, axis=1))):
                    h = 2 * p + hh
                    kmean_ref[h, pl.ds(j, 1), :] = jnp.mean(jnp.where(head0, kh, 0.0), axis=0, keepdims=True)
                    kn_ref[h, pl.ds(r0, blk), :] = jnp.where(head0, kh, onehot).astype(BF16)
                v_t = v_ref[pl.ds(r0, blk), cols].astype(F32).T.astype(BF16)
                for hh in range(2):
                    vt_ref[2 * p + hh, 0:dh, pl.ds(r0, blk)] = v_t[hh * dh:(hh + 1) * dh, :]
                    vt_ref[2 * p + hh, dh:, pl.ds(r0, blk)] = jnp.ones((vt_ref.shape[1] - dh, blk), BF16)
            return c

        lax.fori_loop(0, nb, prep, 0)

    jidx = lax.broadcasted_iota(jnp.int32, (nb, blk), 0)
    key_i = lax.broadcasted_iota(jnp.int32, (blk, blk), 0)
    qry_i = lax.broadcasted_iota(jnp.int32, (blk, blk), 1)
    causal = key_i <= qry_i
    r_own = pl.multiple_of(i * blk, blk)
    qk_scale = dh ** -0.5 * LOG2_E
    for p in range(nheads // 2):
        cols = slice(p * LANES, (p + 1) * LANES)
        qr_t = q_ref[:, cols].astype(F32).T
        for hh in range(2):
            h = 2 * p + hh
            q_raw = qr_t[hh * dh:(hh + 1) * dh, :]
            q_t = (q_raw * lax.rsqrt(jnp.mean(q_raw * q_raw, axis=0, keepdims=True) + RMS_EPS)
                   * gq_ref[hh * dh:(hh + 1) * dh, :])
            gate = jnp.dot(kmean_ref[h].astype(BF16),
                           jnp.concatenate([q_t, jnp.zeros((LANES - dh, blk), F32)], axis=0).astype(BF16),
                           preferred_element_type=F32)
            rank = jnp.zeros((nb, blk), F32)
            for jp in range(nb):
                row = gate[jp:jp + 1, :]
                beats = (row > gate) | ((row == gate) & (jidx > jp))
                rank = rank + jnp.where(beats, jnp.where(jp < i, 1.0, 0.0), 0.0)
            sel = (rank < topk) & (jidx < i)
            q_s = q_t * qk_scale
            pad = jnp.zeros((LANES - dh - nb, blk), F32)
            qaug_ref[h] = jnp.concatenate([q_s, jnp.where(sel, 0.0, MASK_BIAS), pad], axis=0).astype(BF16)
            qaug_own = jnp.concatenate([q_s, jnp.where(jidx == i, 0.0, MASK_BIAS), pad], axis=0).astype(BF16)
            st = jnp.dot(kn_ref[h, pl.ds(r_own, blk), :], qaug_own, preferred_element_type=F32)
            st = jnp.where(causal, st, -jnp.inf)
            s_ref[h, 0:blk, :] = st
            m_ref[h] = jnp.max(st, axis=0, keepdims=True)

    def finish_own(h):
        pr = jnp.exp2(s_ref[h, 0:blk, :] - m_ref[h]).astype(BF16)
        acc_ref[h] = jnp.dot(vt_ref[h, :, pl.ds(r_own, blk)], pr, preferred_element_type=F32)

    def score_pair(u, h):
        r0 = pl.multiple_of(u * pair, pair)
        st = jnp.dot(kn_ref[h, pl.ds(r0, pair), :], qaug_ref[h], preferred_element_type=F32)
        m_old = m_ref[h]
        m_new = jnp.maximum(m_old, jnp.max(st, axis=0, keepdims=True))
        s_ref[h] = st
        alpha_ref[h] = jnp.exp2(m_old - m_new)
        m_ref[h] = m_new

    def finish_pair(u, h):
        r0 = pl.multiple_of(u * pair, pair)
        pr = jnp.exp2(s_ref[h] - m_ref[h]).astype(BF16)
        acc_ref[h] = alpha_ref[h] * acc_ref[h] + jnp.dot(vt_ref[h, :, pl.ds(r0, pair)], pr,
                                                         preferred_element_type=F32)

    n_pairs = jnp.maximum((i + 1) // 2, 1)
    for h in range(nheads):
        finish_own(h)
        score_pair(0, h)

    def body(u, c):
        for h in range(nheads):
            finish_pair(u - 1, h)
            score_pair(u, h)
        return c

    lax.fori_loop(1, n_pairs, body, 0)
    for h in range(nheads):
        finish_pair(n_pairs - 1, h)

    for p in range(nheads // 2):
        a0 = acc_ref[2 * p]
        a1 = acc_ref[2 * p + 1]
        ot = jnp.concatenate([a0[0:dh] / a0[dh:dh + 1], a1[0:dh] / a1[dh:dh + 1]], axis=0)
        o_ref[:, p * LANES:(p + 1) * LANES] = ot.T.astype(o_ref.dtype)


def moba(proj, gq2, gk2, *, batch, seq):
    nb = seq // MOBA_BLOCK
    blk = MOBA_BLOCK
    gq_t = jnp.broadcast_to(gq2.reshape(LANES, 1), (LANES, blk))
    dh = MOBA_HEAD_DIM
    nheads = MOBA_HEADS
    width = nheads * dh
    wb = width // LANES
    assert dh + nb <= LANES and 2 * dh == LANES and nb % 2 == 0
    v_rows = dh + 2 * SUBLANES
    body = functools.partial(_moba_body, nb=nb, blk=blk, dh=dh, topk=MOBA_TOPK, nheads=nheads)
    return pl.pallas_call(
        body,
        grid=(batch, nb),
        in_specs=[
            pl.BlockSpec((blk, width), lambda b, i: (b * nb + i, COL_QA // wb)),
            pl.BlockSpec((seq, width), lambda b, i: (b, COL_KA // wb)),
            pl.BlockSpec((seq, width), lambda b, i: (b, COL_VA // wb)),
            pl.BlockSpec((LANES, blk), lambda b, i: (0, 0)),
            pl.BlockSpec((1, LANES), lambda b, i: (0, 0)),
        ],
        out_specs=pl.BlockSpec((blk, width), lambda b, i: (b * nb + i, 0)),
        out_shape=jax.ShapeDtypeStruct((batch * seq, width), BF16),
        scratch_shapes=[
            pltpu.VMEM((nheads, seq, LANES), BF16),
            pltpu.VMEM((nheads, v_rows, seq), BF16),
            pltpu.VMEM((nheads, nb, LANES), F32),
            pltpu.VMEM((nheads, LANES, blk), BF16),
            pltpu.VMEM((nheads, 2 * blk, blk), F32),
            pltpu.VMEM((nheads, 1, blk), F32),
            pltpu.VMEM((nheads, 1, blk), F32),
            pltpu.VMEM((nheads, v_rows, blk), F32),
        ],
        compiler_params=_params("arbitrary", "arbitrary"),
        name="moba",
    )(proj, proj, proj, gq_t, gk2)


def _log_sigmoid(x):
    return jnp.minimum(x, 0.0) - jnp.log(1.0 + jnp.exp(-jnp.abs(x)))


def _dot_tri(tri, x, tri_left):
    out = None
    for _ in range(3):
        piece = x.astype(BF16)
        x = x - piece.astype(F32)
        term = (jnp.dot(tri, piece, preferred_element_type=F32) if tri_left
                else jnp.dot(piece, tri, preferred_element_type=F32))
        out = term if out is None else out + term
    return out


def _mlstm_body(qr_ref, kr_ref, v_ref, og_ref, gcol_ref, grow_ref, brow_ref, bcol_ref,
                cwq_ref, cwk_ref, cbq_ref, cbk_ref, gh_ref, o_ref,
                qx_ref, kx_ref, c_ref, m_ref, *, chunk, dk, nh):
    L = chunk
    width = nh * dk

    @pl.when(pl.program_id(1) == 0)
    def _():
        qx_ref[0:SUBLANES, :] = jnp.zeros((SUBLANES, width), F32)
        kx_ref[0:SUBLANES, :] = jnp.zeros((SUBLANES, width), F32)
        c_ref[...] = jnp.zeros_like(c_ref)
        m_ref[...] = jnp.zeros_like(m_ref)

    qx_ref[SUBLANES:SUBLANES + L, :] = qr_ref[...].astype(F32)
    kx_ref[SUBLANES:SUBLANES + L, :] = kr_ref[...].astype(F32)

    def conv_silu(x_ref, w_ref, b_ref):
        acc = b_ref[...] + w_ref[0:1, :] * x_ref[pl.ds(SUBLANES - CONV_WIDTH + 1, L), :]
        for j in range(1, CONV_WIDTH):
            acc = acc + w_ref[j:j + 1, :] * x_ref[pl.ds(SUBLANES - CONV_WIDTH + 1 + j, L), :]
        return acc * _sigmoid(acc)

    q_all = conv_silu(qx_ref, cwq_ref, cbq_ref)
    k_all = conv_silu(kx_ref, cwk_ref, cbk_ref) * (dk ** -0.5)
    qx_ref[0:SUBLANES, :] = qx_ref[L:L + SUBLANES, :]
    kx_ref[0:SUBLANES, :] = kx_ref[L:L + SUBLANES, :]

    pre_col = gcol_ref[...] + brow_ref[...]
    pre_row = grow_ref[...] + bcol_ref[...]
    t_i = lax.broadcasted_iota(jnp.int32, (L, L), 0)
    s_i = lax.broadcasted_iota(jnp.int32, (L, L), 1)
    tril = s_i <= t_i
    bcum_cols = _dot_tri(jnp.where(tril, 1.0, 0.0).astype(BF16), _log_sigmoid(pre_col), True)
    bcum_rows = _dot_tri(jnp.where(t_i <= s_i, 1.0, 0.0).astype(BF16), _log_sigmoid(pre_row), False)
    src_before_out = t_i <= s_i
    v_t = v_ref[...].astype(F32).T
    ones_t = jnp.ones((dk, L), F32)

    for h in range(nh):
        cols = slice(h * dk, (h + 1) * dk)
        qb = q_all[:, cols].astype(BF16)
        kb = k_all[:, cols].astype(BF16)
        i_col = pre_col[:, h:h + 1]
        i_row = pre_row[h:h + 1, :]
        bcum_col = bcum_cols[:, nh + h:nh + h + 1]
        bcum_row = bcum_rows[nh + h:nh + h + 1, :]

        m_prev = m_ref[h, 0:1, 0:1]
        a_row = bcum_row + m_prev
        dmat_t = jnp.where(src_before_out, bcum_row + (i_col - bcum_col), -jnp.inf)
        m_t = jnp.maximum(a_row, jnp.max(dmat_t, axis=0, keepdims=True))
        sqk_t = (_nt_dot(kb, qb) * jnp.exp(dmat_t - m_t)).astype(BF16)

        v_aug_t = jnp.concatenate([v_t[cols, :], ones_t], axis=0)
        state_t = c_ref[h]
        num_aug_t = (jnp.exp(a_row - m_t) * _nt_dot(state_t.astype(BF16), qb)
                     + jnp.dot(v_aug_t.astype(BF16), sqk_t, preferred_element_type=F32))
        den = num_aug_t[dk:dk + 1, :]
        hc_t = num_aug_t[0:dk, :] / jnp.maximum(jnp.abs(den), jnp.exp(-m_t))
        hn_t = hc_t * lax.rsqrt(jnp.mean(hc_t * hc_t, axis=0, keepdims=True) + RMS_EPS)
        o_ref[:, cols] = (hn_t.T * gh_ref[...] * _sigmoid(og_ref[:, cols].astype(F32))).astype(o_ref.dtype)

        b_last = bcum_row[:, L - 1:L]
        g_row = b_last - bcum_row + i_row
        m_new = jnp.maximum(b_last + m_prev, jnp.max(g_row, axis=-1, keepdims=True))
        w_c = jnp.exp(b_last + m_prev - m_new)
        vw_t = (v_aug_t * jnp.exp(g_row - m_new)).astype(BF16)
        c_ref[h] = w_c * state_t + jnp.dot(vw_t, kb, preferred_element_type=F32)
        m_ref[h] = jnp.broadcast_to(m_new, (1, LANES))


def mlstm(proj, gates_col, gates_row, bias_row, bias_col, conv_w, conv_b, gh, *, batch, seq, chunk):
    nh = MLSTM_HEADS
    assert 2 * nh == SUBLANES
    dk = MLSTM_DIM
    width = nh * dk
    wb = width // LANES
    nc = seq // chunk
    body = functools.partial(_mlstm_body, chunk=chunk, dk=dk, nh=nh)

    def rows(col0):
        return pl.BlockSpec((chunk, width), lambda b, c: (b * nc + c, col0 // wb))

    return pl.pallas_call(
        body,
        grid=(batch, nc),
        in_specs=[
            rows(COL_QM), rows(COL_KM), rows(COL_VM), rows(COL_OM),
            pl.BlockSpec((chunk, LANES), lambda b, c: (b * nc + c, 0)),
            pl.BlockSpec((SUBLANES, chunk), lambda b, c: (0, b * nc + c)),
            pl.BlockSpec((1, LANES), lambda b, c: (0, 0)),
            pl.BlockSpec((SUBLANES, 1), lambda b, c: (0, 0)),
            pl.BlockSpec((CONV_WIDTH, width), lambda b, c: (0, 0)),
            pl.BlockSpec((CONV_WIDTH, width), lambda b, c: (0, 1)),
            pl.BlockSpec((1, width), lambda b, c: (0, 0)),
            pl.BlockSpec((1, width), lambda b, c: (0, 1)),
            pl.BlockSpec((1, LANES), lambda b, c: (0, 0)),
        ],
        out_specs=pl.BlockSpec((chunk, width), lambda b, c: (b * nc + c, 0)),
        out_shape=jax.ShapeDtypeStruct((batch * seq, width), BF16),
        scratch_shapes=[
            pltpu.VMEM((chunk + 2 * SUBLANES, width), F32),
            pltpu.VMEM((chunk + 2 * SUBLANES, width), F32),
            pltpu.VMEM((nh, 2 * dk, dk), F32),
            pltpu.VMEM((nh, 1, LANES), F32),
        ],
        compiler_params=_params("arbitrary", "arbitrary"),
        name="mlstm",
    )(proj, proj, proj, proj, gates_col, gates_row, bias_row, bias_col,
      conv_w, conv_w, conv_b, conv_b, gh)


def _merge_body(ya_ref, yb_ref, ga_ref, gb_ref, x_ref, woa_ref, wob_ref, wout_ref, gffn_ref, *rest, moe):
    a = jnp.dot(ya_ref[...], woa_ref[...], preferred_element_type=F32)
    b = jnp.dot(yb_ref[...], wob_ref[...], preferred_element_type=F32)
    mixed = _sigmoid(ga_ref[...].astype(F32)) * a + _sigmoid(gb_ref[...].astype(F32)) * b
    x1 = x_ref[...] + jnp.dot(mixed.astype(BF16), wout_ref[...], preferred_element_type=F32)
    hf = _rms(x1, gffn_ref[...])
    if moe:
        wr_ref, x1_ref, hf_ref, lg_ref = rest
        _store_row_tiles(hf_ref, hf)
        lg_ref[...] = jnp.dot(hf.astype(BF16), wr_ref[...].astype(BF16), preferred_element_type=F32)
    else:
        x1_ref, hf_ref = rest
        hf_ref[...] = hf.astype(BF16)
    x1_ref[...] = x1


def merge(ya, yb, proj, x2, woa, wob, wout, gffn, wr, *, tm):
    t, d = x2.shape
    moe = wr is not None
    full = lambda m: (0, 0)
    in_specs = [
        pl.BlockSpec((tm, ya.shape[1]), lambda m: (m, 0)),
        pl.BlockSpec((tm, yb.shape[1]), lambda m: (m, 0)),
        pl.BlockSpec((tm, d), lambda m: (m, COL_GA * LANES // d)),
        pl.BlockSpec((tm, d), lambda m: (m, COL_GB * LANES // d)),
        pl.BlockSpec((tm, d), lambda m: (m, 0)),
        pl.BlockSpec(woa.shape, full), pl.BlockSpec(wob.shape, full), pl.BlockSpec(wout.shape, full),
        pl.BlockSpec((1, d), full),
    ]
    args = [ya, yb, proj, proj, x2, woa, wob, wout, gffn]
    out_specs = [pl.BlockSpec((tm, d), lambda m: (m, 0)), pl.BlockSpec((tm, d), lambda m: (m, 0))]
    out_shape = [jax.ShapeDtypeStruct((t, d), F32), jax.ShapeDtypeStruct((t, d), BF16)]
    if moe:
        out_specs[1] = pl.BlockSpec((tm * SUBLANES, LANES), lambda m: (m, 0))
        out_shape[1] = jax.ShapeDtypeStruct((t * SUBLANES, LANES), F32)
    if moe:
        in_specs.append(pl.BlockSpec(wr.shape, full))
        args.append(wr)
        out_specs.append(pl.BlockSpec((tm, LANES), lambda m: (m, 0)))
        out_shape.append(jax.ShapeDtypeStruct((t, LANES), F32))
    return pl.pallas_call(
        functools.partial(_merge_body, moe=moe),
        grid=(t // tm,),
        in_specs=in_specs, out_specs=out_specs, out_shape=out_shape,
        compiler_params=_params("arbitrary"),
        name="merge_moe" if moe else "merge",
    )(*args)


def _swiglu_gate(x, w1_ref, w3_ref, g_ref, fc):
    dff = g_ref.shape[1]
    for f0 in range(0, dff, fc):
        a = jnp.dot(x, w1_ref[:, f0:f0 + fc].astype(BF16), preferred_element_type=F32)
        b = jnp.dot(x, w3_ref[:, f0:f0 + fc].astype(BF16), preferred_element_type=F32)
        g_ref[:, f0:f0 + fc] = (a * _sigmoid(a) * b).astype(BF16)


def _swiglu(x, w1_ref, w3_ref, w2_ref, g_ref, fc):
    _swiglu_gate(x, w1_ref, w3_ref, g_ref, fc)
    return jnp.dot(g_ref[...], w2_ref[...].astype(BF16), preferred_element_type=F32)


def _ple(x, p_ref, g_ref, wg_ref, wp_ref):
    gate = _sigmoid(jnp.dot(_rms(x, g_ref[...]).astype(BF16), wg_ref[...], preferred_element_type=F32))
    emb = jnp.dot(p_ref[...].astype(BF16), wp_ref[...], preferred_element_type=F32)
    return x + gate * emb


def _ffn_body(hf_ref, x1_ref, w1_ref, w3_ref, w2_ref, p_ref, g_ref, wg_ref, wp_ref, o_ref, act_ref, *, fc):
    x2 = x1_ref[...] + _swiglu(hf_ref[...], w1_ref, w3_ref, w2_ref, act_ref, fc)
    o_ref[...] = _ple(x2, p_ref, g_ref, wg_ref, wp_ref)


def dense_ffn(hf, x1, w1, w3, w2, p2, p_row0, g, wg, wp, *, tm, fc):
    t, d = x1.shape
    dff = w1.shape[1]
    p_blk0 = p_row0 // tm
    resident = dict(pipeline_mode=pl.Buffered(1))
    full = lambda m: (0, 0)
    return pl.pallas_call(
        functools.partial(_ffn_body, fc=fc),
        grid=(t // tm,),
        in_specs=[
            pl.BlockSpec((tm, d), lambda m: (m, 0)),
            pl.BlockSpec((tm, d), lambda m: (m, 0)),
            pl.BlockSpec((d, dff), full, **resident),
            pl.BlockSpec((d, dff), full, **resident),
            pl.BlockSpec((dff, d), full, **resident),
            pl.BlockSpec((tm, p2.shape[1]), lambda m: (p_blk0 + m, 0)),
            pl.BlockSpec((1, d), full),
            pl.BlockSpec(wg.shape, full, **resident),
            pl.BlockSpec(wp.shape, full, **resident),
        ],
        out_specs=pl.BlockSpec((tm, d), lambda m: (m, 0)),
        out_shape=jax.ShapeDtypeStruct((t, d), F32),
        scratch_shapes=[pltpu.VMEM((tm, dff), BF16)],
        compiler_params=_params("arbitrary"),
        name="dense_ffn",
    )(hf, x1, w1, w3, w2, p2, g, wg, wp)


META_E0, META_E1, META_G0, META_G1, META_R0, META_R1 = 0, 1, 2, 3, 4, 5


def _route_body(lg_ref, meta_ref, tab_ref, cnt_ref, carry_ref, *, tm, ne):
    @pl.when(pl.program_id(0) == 0)
    def _():
        carry_ref[...] = jnp.zeros_like(carry_ref)

    lane = lax.broadcasted_iota(jnp.int32, (tm, LANES), 1)
    lanef = lane.astype(F32)
    lg = jnp.where(lane < ne, lg_ref[...], -jnp.inf)
    m1 = jnp.max(lg, axis=-1, keepdims=True)
    e1 = jnp.min(jnp.where(lg == m1, lanef, float(LANES)), axis=-1, keepdims=True)
    lg2 = jnp.where(lanef == e1, -jnp.inf, lg)
    m2 = jnp.max(lg2, axis=-1, keepdims=True)
    e2 = jnp.min(jnp.where(lg2 == m2, lanef, float(LANES)), axis=-1, keepdims=True)
    ex = jnp.exp(m2 - m1)
    g1 = 1.0 / (1.0 + ex)
    g2 = ex / (1.0 + ex)
    onehot = jnp.where((lanef == e1) | (lanef == e2), 1.0, 0.0)
    row = lax.broadcasted_iota(jnp.int32, (tm, tm), 0)
    col = lax.broadcasted_iota(jnp.int32, (tm, tm), 1)
    before = jnp.dot(jnp.where(col < row, 1.0, 0.0).astype(BF16), onehot.astype(BF16),
                     preferred_element_type=F32) + carry_ref[...]
    r1 = jnp.sum(jnp.where(lanef == e1, before, 0.0), axis=-1, keepdims=True)
    r2 = jnp.sum(jnp.where(lanef == e2, before, 0.0), axis=-1, keepdims=True)
    carry_ref[...] += jnp.sum(onehot, axis=0, keepdims=True)
    meta = jnp.zeros((tm, LANES), F32)
    for pos, val in ((META_E0, e1), (META_E1, e2), (META_G0, g1), (META_G1, g2), (META_R0, r1), (META_R1, r2)):
        meta = jnp.where(lane == pos, val, meta)
    meta_ref[...] = meta
    tab_ref[...] = meta.T[0:SUBLANES, :]
    cnt_ref[...] = jnp.broadcast_to(carry_ref[...], cnt_ref.shape)


def route(logits, *, tm):
    t = logits.shape[0]
    return pl.pallas_call(
        functools.partial(_route_body, tm=tm, ne=N_EXPERTS),
        grid=(t // tm,),
        in_specs=[pl.BlockSpec((tm, LANES), lambda m: (m, 0))],
        out_specs=[pl.BlockSpec((tm, LANES), lambda m: (m, 0)),
                   pl.BlockSpec((SUBLANES, tm), lambda m: (0, m)),
                   pl.BlockSpec((SUBLANES, LANES), lambda m: (0, 0))],
        out_shape=[jax.ShapeDtypeStruct((t, LANES), F32), jax.ShapeDtypeStruct((SUBLANES, t), F32),
                   jax.ShapeDtypeStruct((SUBLANES, LANES), F32)],
        scratch_shapes=[pltpu.VMEM((1, LANES), F32)],
        compiler_params=_params("arbitrary"),
        name="route",
    )(logits)


def _dispatch_body(dest_ref, pe_ref, na_ref, hf_ref, wsrc_ref, xs_ref, wdst_ref, zero_ref, sem, zsem,
                   *, tm, topk, tm_rows, n_tok):
    base = pl.program_id(0) * tm

    @pl.when(pl.program_id(0) == 0)
    def _():
        zero_ref[...] = jnp.zeros_like(zero_ref)
        n_tiles = xs_ref.shape[0] // (tm_rows * SUBLANES)
        fills = []
        for e in range(N_EXPERTS):
            end = pe_ref[e]
            nonempty = end > (pe_ref[e - 1] if e else 0)
            fills.append((nonempty, pl.multiple_of(jnp.maximum(end - tm_rows, 0), tm_rows)))
        for tile in range(n_tiles):
            fills.append((tile >= na_ref[0], tile * tm_rows))
        for phase in ("start", "wait"):
            for cond, row0 in fills:
                @pl.when(cond)
                def _(row0=row0, phase=phase):
                    cp = pltpu.make_async_copy(
                        zero_ref, xs_ref.at[pl.ds(row0 * SUBLANES, tm_rows * SUBLANES)], zsem)
                    cp.start() if phase == "start" else cp.wait()

    def issue(r, c):
        for k in range(topk):
            d = dest_ref[k * n_tok + base + r]
            pltpu.make_async_copy(_row_tile(hf_ref, r), _row_tile(xs_ref, d), sem).start(priority=k % 2)
        return c

    lax.fori_loop(0, tm, issue, 0, unroll=ROW_DMA_UNROLL)
    wdst_ref[...] = wsrc_ref[...].astype(wdst_ref.dtype)
    for k in range(topk):
        pltpu.make_async_copy(hf_ref, xs_ref.at[pl.ds(0, tm * SUBLANES)], sem).wait()


def _row_tile(ref, r):
    return ref.at[pl.ds(pl.multiple_of(r * SUBLANES, SUBLANES), SUBLANES)]


def dispatch(dest, pad_end, n_active, hf, w_f32, *, n_rows, tm, topk, tm_rows):
    t = hf.shape[0] // SUBLANES
    n_steps = t // tm
    w2d = w_f32.reshape(-1, w_f32.shape[-1])
    slab = w2d.shape[0] // n_steps
    assert slab * n_steps == w2d.shape[0] and slab % (2 * SUBLANES) == 0
    xs, w_bf = pl.pallas_call(
        functools.partial(_dispatch_body, tm=tm, topk=topk, tm_rows=tm_rows, n_tok=t),
        grid_spec=pltpu.PrefetchScalarGridSpec(
            num_scalar_prefetch=3,
            grid=(n_steps,),
            in_specs=[pl.BlockSpec((tm * SUBLANES, LANES), lambda m, *_: (m, 0)),
                      pl.BlockSpec((slab, w2d.shape[1]), lambda m, *_: (m, 0))],
            out_specs=[pl.BlockSpec(memory_space=pl.ANY),
                       pl.BlockSpec((slab, w2d.shape[1]), lambda m, *_: (m, 0))],
            scratch_shapes=[pltpu.VMEM((tm_rows * SUBLANES, LANES), F32),
                            pltpu.SemaphoreType.DMA, pltpu.SemaphoreType.DMA],
        ),
        out_shape=[jax.ShapeDtypeStruct((n_rows * SUBLANES, LANES), F32),
                   jax.ShapeDtypeStruct(w2d.shape, BF16)],
        compiler_params=_params("arbitrary"),
        name="moe_dispatch",
    )(dest, pad_end, n_active, hf, w2d)
    return xs, w_bf.reshape(w_f32.shape)


def _experts_body(te_ref, na_ref, xs_ref, w1_hbm, w3_hbm, w2_hbm, y_ref,
                  xb_ref, g_ref, w1_buf, w3_buf, w2_buf, wsem, *, fc):
    i = pl.program_id(0)
    na = na_ref[0]
    tm = xb_ref.shape[0]
    tf = g_ref.shape[2]

    def weight_copies(tile, half):
        e = te_ref[tile]
        cols = pl.ds(half * tf, tf)
        return (pltpu.make_async_copy(w1_hbm.at[e, :, cols], w1_buf.at[half], wsem.at[half, 0]),
                pltpu.make_async_copy(w3_hbm.at[e, :, cols], w3_buf.at[half], wsem.at[half, 1]),
                pltpu.make_async_copy(w2_hbm.at[e, cols, :], w2_buf.at[half], wsem.at[half, 2]))

    def start(tile, half):
        for cp in weight_copies(tile, half):
            cp.start()

    def wait(half):
        for cp in weight_copies(0, half):
            cp.wait()

    @pl.when(i >= na)
    def _():
        y_ref[...] = jnp.zeros_like(y_ref)

    @pl.when(i < na)
    def _():
        first_of_expert = (i == 0) | (te_ref[i] != te_ref[jnp.maximum(i - 1, 0)])
        nxt = jnp.minimum(i + 1, na - 1)
        next_is_new_expert = (i + 1 < na) & (te_ref[nxt] != te_ref[i])

        @pl.when(i == 0)
        def _():
            start(0, 0)

        @pl.when(first_of_expert)
        def _():
            start(i, 1)
            wait(0)

        xb_ref[...] = _load_row_tiles(xs_ref, tm).astype(BF16)
        xb = xb_ref[...]
        _swiglu_gate(xb, w1_buf.at[0], w3_buf.at[0], g_ref.at[0], fc)
        acc = jnp.dot(g_ref[0], w2_buf[0].astype(BF16), preferred_element_type=F32)

        @pl.when(next_is_new_expert)
        def _():
            start(nxt, 0)

        @pl.when(first_of_expert)
        def _():
            wait(1)

        _swiglu_gate(xb, w1_buf.at[1], w3_buf.at[1], g_ref.at[1], fc)
        y = acc + jnp.dot(g_ref[1], w2_buf[1].astype(BF16), preferred_element_type=F32)
        _store_row_tiles(y_ref, y)


def experts(tile_expert, n_active, xs, w1, w3, w2, *, tm, fc):
    n_rows = xs.shape[0] // SUBLANES
    d = w1.shape[1]
    dff = w1.shape[2]
    tf = dff // 2
    return pl.pallas_call(
        functools.partial(_experts_body, fc=fc),
        grid_spec=pltpu.PrefetchScalarGridSpec(
            num_scalar_prefetch=2,
            grid=(n_rows // tm,),
            in_specs=[
                pl.BlockSpec((tm * SUBLANES, LANES), lambda i, te, na: (jnp.minimum(i, na[0] - 1), 0)),
                pl.BlockSpec(memory_space=pl.ANY),
                pl.BlockSpec(memory_space=pl.ANY),
                pl.BlockSpec(memory_space=pl.ANY),
            ],
            out_specs=pl.BlockSpec((tm * SUBLANES, LANES), lambda i, te, na: (i, 0)),
            scratch_shapes=[
                pltpu.VMEM((tm, d), BF16),
                pltpu.VMEM((2, tm, tf), BF16),
                pltpu.VMEM((2, d, tf), w1.dtype),
                pltpu.VMEM((2, d, tf), w3.dtype),
                pltpu.VMEM((2, tf, d), w2.dtype),
                pltpu.SemaphoreType.DMA((2, 3)),
            ],
        ),
        out_shape=jax.ShapeDtypeStruct((n_rows * SUBLANES, LANES), F32),
        compiler_params=_params("arbitrary", vmem_limit=EXPERTS_VMEM_LIMIT),
        name="moe_experts",
    )(tile_expert, n_active, xs, w1, w3, w2)


def _combine_body(dest_ref, x1_ref, meta_ref, p_ref, g_ref, wg_ref, wp_ref, y_ref, o_ref, buf_ref, sem,
                  *, tm, topk):
    m = pl.program_id(0)
    n_steps = pl.num_programs(0)
    n_tok = n_steps * tm

    def start_gather(tile, slot):
        def issue(r, c):
            for k in range(topk):
                d = dest_ref[k * n_tok + tile * tm + r]
                pltpu.make_async_copy(_row_tile(y_ref, d), _row_tile(buf_ref.at[slot, k], r),
                                      sem.at[slot]).start(priority=k % 2)
            return c

        lax.fori_loop(0, tm, issue, 0, unroll=ROW_DMA_UNROLL)

    @pl.when(m == 0)
    def _():
        start_gather(0, 0)

    @pl.when(m + 1 < n_steps)
    def _():
        start_gather(m + 1, (m + 1) % 2)

    slot = m % 2
    for k in range(topk):
        pltpu.make_async_copy(y_ref.at[pl.ds(0, tm * SUBLANES)], buf_ref.at[slot, k], sem.at[slot]).wait()
    meta = meta_ref[...]
    g0 = meta[:, META_G0:META_G0 + 1]
    g1 = meta[:, META_G1:META_G1 + 1]
    x2 = x1_ref[...] + (g0 * _load_row_tiles(buf_ref.at[slot, 0], tm)
                        + g1 * _load_row_tiles(buf_ref.at[slot, 1], tm))
    o_ref[...] = _ple(x2, p_ref, g_ref, wg_ref, wp_ref)


def combine(dest, x1, meta, y, p2, p_row0, g, wg, wp, *, tm, topk):
    t, d = x1.shape
    full = lambda m, dest: (0, 0)
    p_blk0 = p_row0 // tm
    return pl.pallas_call(
        functools.partial(_combine_body, tm=tm, topk=topk),
        grid_spec=pltpu.PrefetchScalarGridSpec(
            num_scalar_prefetch=1,
            grid=(t // tm,),
            in_specs=[pl.BlockSpec((tm, d), lambda m, dest: (m, 0)),
                      pl.BlockSpec((tm, LANES), lambda m, dest: (m, 0)),
                      pl.BlockSpec((tm, p2.shape[1]), lambda m, dest: (p_blk0 + m, 0)),
                      pl.BlockSpec((1, d), full), pl.BlockSpec(wg.shape, full), pl.BlockSpec(wp.shape, full),
                      pl.BlockSpec(memory_space=pl.ANY)],
            out_specs=pl.BlockSpec((tm, d), lambda m, dest: (m, 0)),
            scratch_shapes=[pltpu.VMEM((2, topk, tm * SUBLANES, LANES), F32), pltpu.SemaphoreType.DMA((2,))],
        ),
        out_shape=jax.ShapeDtypeStruct((t, d), F32),
        compiler_params=_params("arbitrary"),
        name="moe_combine",
    )(dest, x1, meta, p2, g, wg, wp, y)


def moe_ffn(hf, x1, logits, w1, w3, w2, ple_args, *, tm_route, tm_rows, tm_dispatch, tm_combine):
    t, d = x1.shape
    topk = 2
    meta, tab, cnt = route(logits, tm=tm_route)
    counts = cnt[0, :N_EXPERTS].astype(jnp.int32)
    padded = ((counts + tm_rows - 1) // tm_rows) * tm_rows
    pad_end = jnp.cumsum(padded).astype(jnp.int32)
    pad_start = pad_end - padded
    eidx = tab[META_E0:META_E1 + 1].astype(jnp.int32)
    rank = tab[META_R0:META_R1 + 1].astype(jnp.int32)
    dest = rank
    for e in range(N_EXPERTS):
        dest = dest + jnp.where(eidx == e, pad_start[e], 0)
    dest = dest.reshape(topk * t)
    n_tiles = -(-(t * topk) // tm_rows) + N_EXPERTS
    tile_start = jnp.arange(n_tiles, dtype=jnp.int32) * tm_rows
    tile_expert = jnp.minimum(jnp.sum(tile_start[:, None] >= pad_end[None, :], axis=1),
                              N_EXPERTS - 1).astype(jnp.int32)
    n_active = pad_end[N_EXPERTS - 1:] // tm_rows
    xs, w3_bf = dispatch(dest, pad_end, n_active, hf, w3, n_rows=n_tiles * tm_rows, tm=tm_dispatch, topk=topk,
                         tm_rows=tm_rows)
    y = experts(tile_expert, n_active, xs, w1, w3_bf, w2, tm=tm_rows, fc=FC_SWIGLU)
    return combine(dest, x1, meta, y, *ple_args, tm=tm_combine, topk=topk)


def _tile2(g):
    return jnp.concatenate([g, g]).reshape(1, 2 * g.shape[0])


def kernel(x, p, g_mix, w_in, g_q, g_k, conv_w, conv_b, b_i, b_f, g_h, w_oa, w_ob, w_out, g_ffn, w_d1, w_d3,
           w_d2, w_router, w_e1, w_e3, w_e2, g_ple, w_ple_gate, w_ple_proj):
    batch, seq, d = x.shape
    depth = w_in.shape[0]
    t = batch * seq
    nh = MLSTM_HEADS
    x2 = x.reshape(t, d)
    c_q, c_k, c_v = 0, 512, 1024
    c_qk, c_vm, c_om, c_i, c_f, c_ga, c_gb, c_end = 1536, 2560, 3072, 3584, 3588, 3592, 4616, 5640

    w_in_t = jnp.swapaxes(w_in, 1, 2).astype(BF16)
    for l in range(depth):
        w_gates = w_in_t[l, c_ga:c_end]
        proj, gif, gif_t = in_proj(x2, g_mix[l].reshape(1, d), w_gates, w_in_t, l, nb_cols=c_i, if_col=c_i,
                                   tm=TM_IN_PROJ, tn=TN_IN_PROJ)

        ya = moba(proj, _tile2(g_q[l]), _tile2(g_k[l]), batch=batch, seq=seq)

        bias = jnp.concatenate([b_i[l], b_f[l]])
        bias_row = jnp.pad(bias, (0, LANES - 2 * nh)).reshape(1, LANES)
        bias_col = bias.reshape(2 * nh, 1)
        yb = mlstm(proj, gif, gif_t, bias_row, bias_col, conv_w[l], conv_b[l].reshape(1, -1),
                   g_h[l].reshape(1, -1), batch=batch, seq=seq, chunk=MLSTM_CHUNK)

        j = l // 2
        moe = l % 2 == 1
        wr = jnp.pad(w_router[j], ((0, 0), (0, LANES - N_EXPERTS))) if moe else None
        outs = merge(ya, yb, proj, x2, w_oa[l].astype(BF16), w_ob[l].astype(BF16), w_out[l].astype(BF16),
                     g_ffn[l].reshape(1, d), wr, tm=TM_MERGE)
        ple_args = (p.reshape(depth * t, -1), l * t, g_ple[l].reshape(1, d), w_ple_gate[l].astype(BF16),
                    w_ple_proj[l].astype(BF16))
        if moe:
            x1, hf, logits = outs
            x2 = moe_ffn(hf, x1, logits, w_e1[j], w_e3[j], w_e2[j],
                         ple_args, tm_route=TM_ROUTE, tm_rows=TM_EXPERT_ROWS, tm_dispatch=TM_DISPATCH,
                         tm_combine=TM_COMBINE)
        else:
            x1, hf = outs
            x2 = dense_ffn(hf, x1, w_d1[j].astype(BF16), w_d3[j].astype(BF16), w_d2[j].astype(BF16),
                           *ple_args, tm=TM_DENSE, fc=FC_SWIGLU)
    return x2.reshape(batch, seq, d)
```

```python
import functools

import jax
import jax.numpy as jnp
from jax import lax
from jax.experimental import pallas as pl
from jax.experimental.pallas import tpu as pltpu

F32 = jnp.float32
BF16 = jnp.bfloat16

RMS_EPS = 1e-6
LANES = 128
SUBLANES = 8

MOBA_HEADS = 8
MOBA_HEAD_DIM = 64
MOBA_BLOCK = 256
MOBA_TOPK = 3
MLSTM_HEADS = 4
MLSTM_DIM = 128
CONV_WIDTH = 4
N_EXPERTS = 8

COL_GA, COL_GB = 0, 8
COL_QA, COL_KA, COL_VA = 16, 20, 24
COL_QM, COL_KM, COL_VM, COL_OM = 28, 32, 36, 40

VMEM_LIMIT = 56 * 1024 * 1024
EXPERTS_VMEM_LIMIT = 60 * 1024 * 1024
ROW_DMA_UNROLL = 16

TM_IN_PROJ, TN_IN_PROJ = 1024, 512
MLSTM_CHUNK = 256
TM_MERGE = 1024
TM_DENSE = 1024
FC_SWIGLU = 256
TM_EXPERT_ROWS = 512
TM_DISPATCH, TM_COMBINE = 1024, 512


def _params(*sem, vmem_limit=VMEM_LIMIT):
    return pltpu.CompilerParams(dimension_semantics=sem, vmem_limit_bytes=vmem_limit)


def _sigmoid(x):
    return 1.0 / (1.0 + jnp.exp(-x))


def _rms(x, g):
    return x * lax.rsqrt(jnp.mean(x * x, axis=-1, keepdims=True) + RMS_EPS) * g


def _split_bf16(x):
    hi = x.astype(BF16)
    return hi, (x - hi.astype(F32)).astype(BF16)


def _store_row_tiles(ref, x):
    ref[...] = x.reshape(x.shape[0] * SUBLANES, LANES)


def _load_row_tiles(ref, rows):
    return ref[...].reshape(rows, SUBLANES * LANES)


def _nt_dot(a, b, **kw):
    return lax.dot_general(a, b, (((1,), (1,)), ((), ())), preferred_element_type=F32, **kw)


def _in_proj_body(x_ref, g_ref, wa_ref, wb_ref, o_ref, oif_ref, oif_t_ref, h_ref, *, nb_cols, if_col, tn):
    h_ref[...] = _rms(x_ref[...], g_ref[...]).astype(BF16)
    gates = _nt_dot(h_ref[...], wb_ref[if_col:if_col + LANES, :])
    oif_ref[...] = gates
    oif_t_ref[...] = gates.T[0:SUBLANES, :]
    na = wa_ref.shape[0]
    for c0 in range(0, na + nb_cols, tn):
        w = wa_ref[c0:c0 + tn, :] if c0 < na else wb_ref[c0 - na:c0 - na + tn, :]
        o_ref[:, c0:c0 + tn] = _nt_dot(h_ref[...], w).astype(o_ref.dtype)


def in_proj(x2, g, wa, w_full, layer, *, nb_cols, if_col, tm, tn):
    t, d = x2.shape
    n = wa.shape[0] + nb_cols
    resident = dict(pipeline_mode=pl.Buffered(1))
    return pl.pallas_call(
        functools.partial(_in_proj_body, nb_cols=nb_cols, if_col=if_col, tn=tn),
        grid=(t // tm,),
        in_specs=[
            pl.BlockSpec((tm, d), lambda m: (m, 0)),
            pl.BlockSpec((1, d), lambda m: (0, 0)),
            pl.BlockSpec(wa.shape, lambda m: (0, 0), **resident),
            pl.BlockSpec((None,) + w_full.shape[1:], lambda m: (layer, 0, 0), **resident),
        ],
        out_specs=[
            pl.BlockSpec((tm, n), lambda m: (m, 0)),
            pl.BlockSpec((tm, LANES), lambda m: (m, 0)),
            pl.BlockSpec((SUBLANES, tm), lambda m: (0, m)),
        ],
        out_shape=[jax.ShapeDtypeStruct((t, n), BF16), jax.ShapeDtypeStruct((t, LANES), F32),
                   jax.ShapeDtypeStruct((SUBLANES, t), F32)],
        scratch_shapes=[pltpu.VMEM((tm, d), BF16)],
        compiler_params=_params("arbitrary"),
        name="in_proj",
    )(x2, g, wa, w_full)


MASK_BIAS = -1e30
LOG2_E = 1.4426950408889634


def _moba_body(q_ref, k_ref, v_ref, gq_ref, gk_ref, o_ref,
               kn_ref, vt_ref, kmean_ref, qaug_ref, s_ref, m_ref, alpha_ref, acc_ref,
               *, nb, blk, dh, topk, nheads):
    i = pl.program_id(1)
    pair = 2 * blk
    lane = lax.broadcasted_iota(jnp.int32, (1, LANES), 1)
    head0 = lane < dh

    same_head = (lax.broadcasted_iota(jnp.int32, (LANES, LANES), 0) // dh
                 == lax.broadcasted_iota(jnp.int32, (LANES, LANES), 1) // dh)
    head_ones = jnp.where(same_head, 1.0, 0.0).astype(BF16)

    def head_rms(x, g):
        hi, lo = _split_bf16(x * x)
        ss = (jnp.dot(hi, head_ones, preferred_element_type=F32)
              + jnp.dot(lo, head_ones, preferred_element_type=F32))
        return x * lax.rsqrt(ss * (1.0 / dh) + RMS_EPS) * g

    @pl.when(i == 0)
    def _():
        def prep(j, c):
            r0 = pl.multiple_of(j * blk, blk)
            onehot = jnp.where(lane == dh + j, 1.0, 0.0)
            for p in range(nheads // 2):
                cols = slice(p * LANES, (p + 1) * LANES)
                kn = head_rms(k_ref[pl.ds(r0, blk), cols].astype(F32), gk_ref[...])
                for hh, kh in ((0, kn), (1, pltpu.roll(kn, dh, axis=1))):
                    h = 2 * p + hh
                    kmean_ref[h, pl.ds(j, 1), :] = jnp.mean(jnp.where(head0, kh, 0.0), axis=0, keepdims=True)
                    kn_ref[h, pl.ds(r0, blk), :] = jnp.where(head0, kh, onehot).astype(BF16)
                v_t = v_ref[pl.ds(r0, blk), cols].astype(F32).T.astype(BF16)
                for hh in range(2):
                    vt_ref[2 * p + hh, 0:dh, pl.ds(r0, blk)] = v_t[hh * dh:(hh + 1) * dh, :]
                    vt_ref[2 * p + hh, dh:, pl.ds(r0, blk)] = jnp.ones((vt_ref.shape[1] - dh, blk), BF16)
            return c

        lax.fori_loop(0, nb, prep, 0)

    jidx = lax.broadcasted_iota(jnp.int32, (nb, blk), 0)
    key_i = lax.broadcasted_iota(jnp.int32, (blk, blk), 0)
    qry_i = lax.broadcasted_iota(jnp.int32, (blk, blk), 1)
    causal = key_i <= qry_i
    r_own = pl.multiple_of(i * blk, blk)
    qk_scale = dh ** -0.5 * LOG2_E
    for p in range(nheads // 2):
        cols = slice(p * LANES, (p + 1) * LANES)
        qr_t = q_ref[:, cols].astype(F32).T
        for hh in range(2):
            h = 2 * p + hh
            q_raw = qr_t[hh * dh:(hh + 1) * dh, :]
            q_t = (q_raw * lax.rsqrt(jnp.mean(q_raw * q_raw, axis=0, keepdims=True) + RMS_EPS)
                   * gq_ref[hh * dh:(hh + 1) * dh, :])
            gate = jnp.dot(kmean_ref[h].astype(BF16),
                           jnp.concatenate([q_t, jnp.zeros((LANES - dh, blk), F32)], axis=0).astype(BF16),
                           preferred_element_type=F32)
            rank = jnp.zeros((nb, blk), F32)
            for jp in range(nb):
                row = gate[jp:jp + 1, :]
                beats = (row > gate) | ((row == gate) & (jidx > jp))
                rank = rank + jnp.where(beats, jnp.where(jp < i, 1.0, 0.0), 0.0)
            sel = (rank < topk) & (jidx < i)
            q_s = q_t * qk_scale
            pad = jnp.zeros((LANES - dh - nb, blk), F32)
            qaug_ref[h] = jnp.concatenate([q_s, jnp.where(sel, 0.0, MASK_BIAS), pad], axis=0).astype(BF16)
            qaug_own = jnp.concatenate([q_s, jnp.where(jidx == i, 0.0, MASK_BIAS), pad], axis=0).astype(BF16)
            st = jnp.dot(kn_ref[h, pl.ds(r_own, blk), :], qaug_own, preferred_element_type=F32)
            st = jnp.where(causal, st, -jnp.inf)
            s_ref[h, 0:blk, :] = st
            m_ref[h] = jnp.max(st, axis=0, keepdims=True)

    def finish_own(h):
        pr = jnp.exp2(s_ref[h, 0:blk, :] - m_ref[h]).astype(BF16)
        acc_ref[h] = jnp.dot(vt_ref[h, :, pl.ds(r_own, blk)], pr, preferred_element_type=F32)

    def score_pair(u, h):
        r0 = pl.multiple_of(u * pair, pair)
        st = jnp.dot(kn_ref[h, pl.ds(r0, pair), :], qaug_ref[h], preferred_element_type=F32)
        m_old = m_ref[h]
        m_new = jnp.maximum(m_old, jnp.max(st, axis=0, keepdims=True))
        s_ref[h] = st
        alpha_ref[h] = jnp.exp2(m_old - m_new)
        m_ref[h] = m_new

    def finish_pair(u, h):
        r0 = pl.multiple_of(u * pair, pair)
        pr = jnp.exp2(s_ref[h] - m_ref[h]).astype(BF16)
        acc_ref[h] = alpha_ref[h] * acc_ref[h] + jnp.dot(vt_ref[h, :, pl.ds(r0, pair)], pr,
                                                         preferred_element_type=F32)

    n_pairs = jnp.maximum((i + 1) // 2, 1)
    for h in range(nheads):
        finish_own(h)
        score_pair(0, h)

    def body(u, c):
        for h in range(nheads):
            finish_pair(u - 1, h)
            score_pair(u, h)
        return c

    lax.fori_loop(1, n_pairs, body, 0)
    for h in range(nheads):
        finish_pair(n_pairs - 1, h)

    for p in range(nheads // 2):
        a0 = acc_ref[2 * p]
        a1 = acc_ref[2 * p + 1]
        ot = jnp.concatenate([a0[0:dh] / a0[dh:dh + 1], a1[0:dh] / a1[dh:dh + 1]], axis=0)
        o_ref[:, p * LANES:(p + 1) * LANES] = ot.T.astype(o_ref.dtype)


def moba(proj, gq2, gk2, *, batch, seq):
    nb = seq // MOBA_BLOCK
    blk = MOBA_BLOCK
    gq_t = jnp.broadcast_to(gq2.reshape(LANES, 1), (LANES, blk))
    dh = MOBA_HEAD_DIM
    nheads = MOBA_HEADS
    width = nheads * dh
    wb = width // LANES
    assert dh + nb <= LANES and 2 * dh == LANES and nb % 2 == 0
    v_rows = dh + 2 * SUBLANES
    body = functools.partial(_moba_body, nb=nb, blk=blk, dh=dh, topk=MOBA_TOPK, nheads=nheads)
    return pl.pallas_call(
        body,
        grid=(batch, nb),
        in_specs=[
            pl.BlockSpec((blk, width), lambda b, i: (b * nb + i, COL_QA // wb)),
            pl.BlockSpec((seq, width), lambda b, i: (b, COL_KA // wb)),
            pl.BlockSpec((seq, width), lambda b, i: (b, COL_VA // wb)),
            pl.BlockSpec((LANES, blk), lambda b, i: (0, 0)),
            pl.BlockSpec((1, LANES), lambda b, i: (0, 0)),
        ],
        out_specs=pl.BlockSpec((blk, width), lambda b, i: (b * nb + i, 0)),
        out_shape=jax.ShapeDtypeStruct((batch * seq, width), BF16),
        scratch_shapes=[
            pltpu.VMEM((nheads, seq, LANES), BF16),
            pltpu.VMEM((nheads, v_rows, seq), BF16),
            pltpu.VMEM((nheads, nb, LANES), F32),
            pltpu.VMEM((nheads, LANES, blk), BF16),
            pltpu.VMEM((nheads, 2 * blk, blk), F32),
            pltpu.VMEM((nheads, 1, blk), F32),
            pltpu.VMEM((nheads, 1, blk), F32),
            pltpu.VMEM((nheads, v_rows, blk), F32),
        ],
        compiler_params=_params("arbitrary", "arbitrary"),
        name="moba",
    )(proj, proj, proj, gq_t, gk2)


def _log_sigmoid(x):
    return jnp.minimum(x, 0.0) - jnp.log(1.0 + jnp.exp(-jnp.abs(x)))


def _dot_tri(tri, x, tri_left):
    out = None
    for _ in range(3):
        piece = x.astype(BF16)
        x = x - piece.astype(F32)
        term = (jnp.dot(tri, piece, preferred_element_type=F32) if tri_left
                else jnp.dot(piece, tri, preferred_element_type=F32))
        out = term if out is None else out + term
    return out


def _mlstm_body(qr_ref, kr_ref, v_ref, og_ref, gcol_ref, grow_ref, brow_ref, bcol_ref,
                cwq_ref, cwk_ref, cbq_ref, cbk_ref, gh_ref, o_ref,
                qx_ref, kx_ref, c_ref, m_ref, *, chunk, dk, nh):
    L = chunk
    width = nh * dk

    @pl.when(pl.program_id(1) == 0)
    def _():
        qx_ref[0:SUBLANES, :] = jnp.zeros((SUBLANES, width), F32)
        kx_ref[0:SUBLANES, :] = jnp.zeros((SUBLANES, width), F32)
        c_ref[...] = jnp.zeros_like(c_ref)
        m_ref[...] = jnp.zeros_like(m_ref)

    qx_ref[SUBLANES:SUBLANES + L, :] = qr_ref[...].astype(F32)
    kx_ref[SUBLANES:SUBLANES + L, :] = kr_ref[...].astype(F32)

    def conv_silu(x_ref, w_ref, b_ref):
        acc = b_ref[...] + w_ref[0:1, :] * x_ref[pl.ds(SUBLANES - CONV_WIDTH + 1, L), :]
        for j in range(1, CONV_WIDTH):
            acc = acc + w_ref[j:j + 1, :] * x_ref[pl.ds(SUBLANES - CONV_WIDTH + 1 + j, L), :]
        return acc * _sigmoid(acc)

    q_all = conv_silu(qx_ref, cwq_ref, cbq_ref)
    k_all = conv_silu(kx_ref, cwk_ref, cbk_ref) * (dk ** -0.5)
    qx_ref[0:SUBLANES, :] = qx_ref[L:L + SUBLANES, :]
    kx_ref[0:SUBLANES, :] = kx_ref[L:L + SUBLANES, :]

    pre_col = gcol_ref[...] + brow_ref[...]
    pre_row = grow_ref[...] + bcol_ref[...]
    t_i = lax.broadcasted_iota(jnp.int32, (L, L), 0)
    s_i = lax.broadcasted_iota(jnp.int32, (L, L), 1)
    tril = s_i <= t_i
    bcum_cols = _dot_tri(jnp.where(tril, 1.0, 0.0).astype(BF16), _log_sigmoid(pre_col), True)
    bcum_rows = _dot_tri(jnp.where(t_i <= s_i, 1.0, 0.0).astype(BF16), _log_sigmoid(pre_row), False)
    src_before_out = t_i <= s_i
    v_t = v_ref[...].astype(F32).T
    ones_t = jnp.ones((dk, L), F32)

    for h in range(nh):
        cols = slice(h * dk, (h + 1) * dk)
        qb = q_all[:, cols].astype(BF16)
        kb = k_all[:, cols].astype(BF16)
        i_col = pre_col[:, h:h + 1]
        i_row = pre_row[h:h + 1, :]
        bcum_col = bcum_cols[:, nh + h:nh + h + 1]
        bcum_row = bcum_rows[nh + h:nh + h + 1, :]

        m_prev = m_ref[h, 0:1, 0:1]
        a_row = bcum_row + m_prev
        dmat_t = jnp.where(src_before_out, bcum_row + (i_col - bcum_col), -jnp.inf)
        m_t = jnp.maximum(a_row, jnp.max(dmat_t, axis=0, keepdims=True))
        sqk_t = (_nt_dot(kb, qb) * jnp.exp(dmat_t - m_t)).astype(BF16)

        v_aug_t = jnp.concatenate([v_t[cols, :], ones_t], axis=0)
        state_t = c_ref[h]
        num_aug_t = (jnp.exp(a_row - m_t) * _nt_dot(state_t.astype(BF16), qb)
                     + jnp.dot(v_aug_t.astype(BF16), sqk_t, preferred_element_type=F32))
        den = num_aug_t[dk:dk + 1, :]
        hc_t = num_aug_t[0:dk, :] / jnp.maximum(jnp.abs(den), jnp.exp(-m_t))
        hn_t = hc_t * lax.rsqrt(jnp.mean(hc_t * hc_t, axis=0, keepdims=True) + RMS_EPS)
        o_ref[:, cols] = (hn_t.T * gh_ref[...] * _sigmoid(og_ref[:, cols].astype(F32))).astype(o_ref.dtype)

        b_last = bcum_row[:, L - 1:L]
        g_row = b_last - bcum_row + i_row
        m_new = jnp.maximum(b_last + m_prev, jnp.max(g_row, axis=-1, keepdims=True))
        w_c = jnp.exp(b_last + m_prev - m_new)
        vw_t = (v_aug_t * jnp.exp(g_row - m_new)).astype(BF16)
        c_ref[h] = w_c * state_t + jnp.dot(vw_t, kb, preferred_element_type=F32)
        m_ref[h] = jnp.broadcast_to(m_new, (1, LANES))


def mlstm(proj, gates_col, gates_row, bias_row, bias_col, conv_w, conv_b, gh, *, batch, seq, chunk):
    nh = MLSTM_HEADS
    assert 2 * nh == SUBLANES
    dk = MLSTM_DIM
    width = nh * dk
    wb = width // LANES
    nc = seq // chunk
    body = functools.partial(_mlstm_body, chunk=chunk, dk=dk, nh=nh)

    def rows(col0):
        return pl.BlockSpec((chunk, width), lambda b, c: (b * nc + c, col0 // wb))

    return pl.pallas_call(
        body,
        grid=(batch, nc),
        in_specs=[
            rows(COL_QM), rows(COL_KM), rows(COL_VM), rows(COL_OM),
            pl.BlockSpec((chunk, LANES), lambda b, c: (b * nc + c, 0)),
            pl.BlockSpec((SUBLANES, chunk), lambda b, c: (0, b * nc + c)),
            pl.BlockSpec((1, LANES), lambda b, c: (0, 0)),
            pl.BlockSpec((SUBLANES, 1), lambda b, c: (0, 0)),
            pl.BlockSpec((CONV_WIDTH, width), lambda b, c: (0, 0)),
            pl.BlockSpec((CONV_WIDTH, width), lambda b, c: (0, 1)),
            pl.BlockSpec((1, width), lambda b, c: (0, 0)),
            pl.BlockSpec((1, width), lambda b, c: (0, 1)),
            pl.BlockSpec((1, LANES), lambda b, c: (0, 0)),
        ],
        out_specs=pl.BlockSpec((chunk, width), lambda b, c: (b * nc + c, 0)),
        out_shape=jax.ShapeDtypeStruct((batch * seq, width), BF16),
        scratch_shapes=[
            pltpu.VMEM((chunk + 2 * SUBLANES, width), F32),
            pltpu.VMEM((chunk + 2 * SUBLANES, width), F32),
            pltpu.VMEM((nh, 2 * dk, dk), F32),
            pltpu.VMEM((nh, 1, LANES), F32),
        ],
        compiler_params=_params("arbitrary", "arbitrary"),
        name="mlstm",
    )(proj, proj, proj, proj, gates_col, gates_row, bias_row, bias_col,
      conv_w, conv_w, conv_b, conv_b, gh)


def _merge_body(ya_ref, yb_ref, ga_ref, gb_ref, x_ref, woa_ref, wob_ref, wout_ref, gffn_ref, *rest, moe):
    a = jnp.dot(ya_ref[...], woa_ref[...], preferred_element_type=F32)
    b = jnp.dot(yb_ref[...], wob_ref[...], preferred_element_type=F32)
    mixed = _sigmoid(ga_ref[...].astype(F32)) * a + _sigmoid(gb_ref[...].astype(F32)) * b
    x1 = x_ref[...] + jnp.dot(mixed.astype(BF16), wout_ref[...], preferred_element_type=F32)
    hf = _rms(x1, gffn_ref[...])
    if moe:
        wr_ref, x1_ref, hf_ref, meta_ref, tab_ref, cnt_ref, lg_ref, carry_ref = rest
        _store_row_tiles(hf_ref, hf)
        lg_ref[...] = jnp.dot(hf.astype(BF16), wr_ref[...].astype(BF16), preferred_element_type=F32)
        _route_body(lg_ref, meta_ref, tab_ref, cnt_ref, carry_ref, tm=x1_ref.shape[0], ne=N_EXPERTS)
    else:
        x1_ref, hf_ref = rest
        hf_ref[...] = hf.astype(BF16)
    x1_ref[...] = x1


def merge(ya, yb, proj, x2, woa, wob, wout, gffn, wr, *, tm):
    t, d = x2.shape
    moe = wr is not None
    full = lambda m: (0, 0)
    in_specs = [
        pl.BlockSpec((tm, ya.shape[1]), lambda m: (m, 0)),
        pl.BlockSpec((tm, yb.shape[1]), lambda m: (m, 0)),
        pl.BlockSpec((tm, d), lambda m: (m, COL_GA * LANES // d)),
        pl.BlockSpec((tm, d), lambda m: (m, COL_GB * LANES // d)),
        pl.BlockSpec((tm, d), lambda m: (m, 0)),
        pl.BlockSpec(woa.shape, full), pl.BlockSpec(wob.shape, full), pl.BlockSpec(wout.shape, full),
        pl.BlockSpec((1, d), full),
    ]
    args = [ya, yb, proj, proj, x2, woa, wob, wout, gffn]
    out_specs = [pl.BlockSpec((tm, d), lambda m: (m, 0)), pl.BlockSpec((tm, d), lambda m: (m, 0))]
    out_shape = [jax.ShapeDtypeStruct((t, d), F32), jax.ShapeDtypeStruct((t, d), BF16)]
    if moe:
        out_specs[1] = pl.BlockSpec((tm * SUBLANES, LANES), lambda m: (m, 0))
        out_shape[1] = jax.ShapeDtypeStruct((t * SUBLANES, LANES), F32)
    if moe:
        in_specs.append(pl.BlockSpec(wr.shape, full))
        args.append(wr)
        out_specs += [pl.BlockSpec((tm, LANES), lambda m: (m, 0)),
                      pl.BlockSpec((SUBLANES, tm), lambda m: (0, m)),
                      pl.BlockSpec((SUBLANES, LANES), lambda m: (0, 0))]
        out_shape += [jax.ShapeDtypeStruct((t, LANES), F32), jax.ShapeDtypeStruct((SUBLANES, t), F32),
                      jax.ShapeDtypeStruct((SUBLANES, LANES), F32)]
    return pl.pallas_call(
        functools.partial(_merge_body, moe=moe),
        grid=(t // tm,),
        in_specs=in_specs, out_specs=out_specs, out_shape=out_shape,
        scratch_shapes=[pltpu.VMEM((tm, LANES), F32), pltpu.VMEM((1, LANES), F32)] if moe else [],
        compiler_params=_params("arbitrary"),
        name="merge_moe" if moe else "merge",
    )(*args)


def _swiglu_gate(x, w1_ref, w3_ref, g_ref, fc):
    dff = g_ref.shape[1]
    for f0 in range(0, dff, fc):
        a = jnp.dot(x, w1_ref[:, f0:f0 + fc].astype(BF16), preferred_element_type=F32)
        b = jnp.dot(x, w3_ref[:, f0:f0 + fc].astype(BF16), preferred_element_type=F32)
        g_ref[:, f0:f0 + fc] = (a * _sigmoid(a) * b).astype(BF16)


def _swiglu(x, w1_ref, w3_ref, w2_ref, g_ref, fc):
    _swiglu_gate(x, w1_ref, w3_ref, g_ref, fc)
    return jnp.dot(g_ref[...], w2_ref[...].astype(BF16), preferred_element_type=F32)


def _ple(x, p_ref, g_ref, wg_ref, wp_ref):
    gate = _sigmoid(jnp.dot(_rms(x, g_ref[...]).astype(BF16), wg_ref[...], preferred_element_type=F32))
    emb = jnp.dot(p_ref[...].astype(BF16), wp_ref[...], preferred_element_type=F32)
    return x + gate * emb


def _ffn_body(hf_ref, x1_ref, w1_ref, w3_ref, w2_ref, p_ref, g_ref, wg_ref, wp_ref, o_ref, act_ref, *, fc):
    x2 = x1_ref[...] + _swiglu(hf_ref[...], w1_ref, w3_ref, w2_ref, act_ref, fc)
    o_ref[...] = _ple(x2, p_ref, g_ref, wg_ref, wp_ref)


def dense_ffn(hf, x1, w1, w3, w2, p2, p_row0, g, wg, wp, *, tm, fc):
    t, d = x1.shape
    dff = w1.shape[1]
    p_blk0 = p_row0 // tm
    resident = dict(pipeline_mode=pl.Buffered(1))
    full = lambda m: (0, 0)
    return pl.pallas_call(
        functools.partial(_ffn_body, fc=fc),
        grid=(t // tm,),
        in_specs=[
            pl.BlockSpec((tm, d), lambda m: (m, 0)),
            pl.BlockSpec((tm, d), lambda m: (m, 0)),
            pl.BlockSpec((d, dff), full, **resident),
            pl.BlockSpec((d, dff), full, **resident),
            pl.BlockSpec((dff, d), full, **resident),
            pl.BlockSpec((tm, p2.shape[1]), lambda m: (p_blk0 + m, 0)),
            pl.BlockSpec((1, d), full),
            pl.BlockSpec(wg.shape, full, **resident),
            pl.BlockSpec(wp.shape, full, **resident),
        ],
        out_specs=pl.BlockSpec((tm, d), lambda m: (m, 0)),
        out_shape=jax.ShapeDtypeStruct((t, d), F32),
        scratch_shapes=[pltpu.VMEM((tm, dff), BF16)],
        compiler_params=_params("arbitrary"),
        name="dense_ffn",
    )(hf, x1, w1, w3, w2, p2, g, wg, wp)


META_E0, META_E1, META_G0, META_G1, META_R0, META_R1 = 0, 1, 2, 3, 4, 5


def _route_body(lg_ref, meta_ref, tab_ref, cnt_ref, carry_ref, *, tm, ne):
    @pl.when(pl.program_id(0) == 0)
    def _():
        carry_ref[...] = jnp.zeros_like(carry_ref)

    lane = lax.broadcasted_iota(jnp.int32, (tm, LANES), 1)
    lanef = lane.astype(F32)
    lg = jnp.where(lane < ne, lg_ref[...], -jnp.inf)
    m1 = jnp.max(lg, axis=-1, keepdims=True)
    e1 = jnp.min(jnp.where(lg == m1, lanef, float(LANES)), axis=-1, keepdims=True)
    lg2 = jnp.where(lanef == e1, -jnp.inf, lg)
    m2 = jnp.max(lg2, axis=-1, keepdims=True)
    e2 = jnp.min(jnp.where(lg2 == m2, lanef, float(LANES)), axis=-1, keepdims=True)
    ex = jnp.exp(m2 - m1)
    g1 = 1.0 / (1.0 + ex)
    g2 = ex / (1.0 + ex)
    onehot = jnp.where((lanef == e1) | (lanef == e2), 1.0, 0.0)
    row = lax.broadcasted_iota(jnp.int32, (tm, tm), 0)
    col = lax.broadcasted_iota(jnp.int32, (tm, tm), 1)
    before = jnp.dot(jnp.where(col < row, 1.0, 0.0).astype(BF16), onehot.astype(BF16),
                     preferred_element_type=F32) + carry_ref[...]
    r1 = jnp.sum(jnp.where(lanef == e1, before, 0.0), axis=-1, keepdims=True)
    r2 = jnp.sum(jnp.where(lanef == e2, before, 0.0), axis=-1, keepdims=True)
    carry_ref[...] += jnp.sum(onehot, axis=0, keepdims=True)
    meta = jnp.zeros((tm, LANES), F32)
    for pos, val in ((META_E0, e1), (META_E1, e2), (META_G0, g1), (META_G1, g2), (META_R0, r1), (META_R1, r2)):
        meta = jnp.where(lane == pos, val, meta)
    meta_ref[...] = meta
    tab_ref[...] = meta.T[0:SUBLANES, :]
    cnt_ref[...] = jnp.broadcast_to(carry_ref[...], cnt_ref.shape)


def _dispatch_body(dest_ref, pe_ref, na_ref, hf_ref, wsrc_ref, xs_ref, wdst_ref, zero_ref, sem, zsem,
                   *, tm, topk, tm_rows, n_tok):
    base = pl.program_id(0) * tm

    @pl.when(pl.program_id(0) == 0)
    def _():
        zero_ref[...] = jnp.zeros_like(zero_ref)
        n_tiles = xs_ref.shape[0] // (tm_rows * SUBLANES)
        fills = []
        for e in range(N_EXPERTS):
            end = pe_ref[e]
            nonempty = end > (pe_ref[e - 1] if e else 0)
            fills.append((nonempty, pl.multiple_of(jnp.maximum(end - tm_rows, 0), tm_rows)))
        for tile in range(n_tiles):
            fills.append((tile >= na_ref[0], tile * tm_rows))
        for phase in ("start", "wait"):
            for cond, row0 in fills:
                @pl.when(cond)
                def _(row0=row0, phase=phase):
                    cp = pltpu.make_async_copy(
                        zero_ref, xs_ref.at[pl.ds(row0 * SUBLANES, tm_rows * SUBLANES)], zsem)
                    cp.start() if phase == "start" else cp.wait()

    def issue(r, c):
        for k in range(topk):
            d = dest_ref[k * n_tok + base + r]
            pltpu.make_async_copy(_row_tile(hf_ref, r), _row_tile(xs_ref, d), sem).start(priority=k % 2)
        return c

    lax.fori_loop(0, tm, issue, 0, unroll=ROW_DMA_UNROLL)
    wdst_ref[...] = wsrc_ref[...].astype(wdst_ref.dtype)
    for k in range(topk):
        pltpu.make_async_copy(hf_ref, xs_ref.at[pl.ds(0, tm * SUBLANES)], sem).wait()


def _row_tile(ref, r):
    return ref.at[pl.ds(pl.multiple_of(r * SUBLANES, SUBLANES), SUBLANES)]


def dispatch(dest, pad_end, n_active, hf, w_f32, *, n_rows, tm, topk, tm_rows):
    t = hf.shape[0] // SUBLANES
    n_steps = t // tm
    w2d = w_f32.reshape(-1, w_f32.shape[-1])
    slab = w2d.shape[0] // n_steps
    assert slab * n_steps == w2d.shape[0] and slab % (2 * SUBLANES) == 0
    xs, w_bf = pl.pallas_call(
        functools.partial(_dispatch_body, tm=tm, topk=topk, tm_rows=tm_rows, n_tok=t),
        grid_spec=pltpu.PrefetchScalarGridSpec(
            num_scalar_prefetch=3,
            grid=(n_steps,),
            in_specs=[pl.BlockSpec((tm * SUBLANES, LANES), lambda m, *_: (m, 0)),
                      pl.BlockSpec((slab, w2d.shape[1]), lambda m, *_: (m, 0))],
            out_specs=[pl.BlockSpec(memory_space=pl.ANY),
                       pl.BlockSpec((slab, w2d.shape[1]), lambda m, *_: (m, 0))],
            scratch_shapes=[pltpu.VMEM((tm_rows * SUBLANES, LANES), F32),
                            pltpu.SemaphoreType.DMA, pltpu.SemaphoreType.DMA],
        ),
        out_shape=[jax.ShapeDtypeStruct((n_rows * SUBLANES, LANES), F32),
                   jax.ShapeDtypeStruct(w2d.shape, BF16)],
        compiler_params=_params("arbitrary"),
        name="moe_dispatch",
    )(dest, pad_end, n_active, hf, w2d)
    return xs, w_bf.reshape(w_f32.shape)


def _experts_body(te_ref, na_ref, xs_ref, w1_hbm, w3_hbm, w2_hbm, y_ref,
                  xb_ref, g_ref, w1_buf, w3_buf, w2_buf, wsem, *, fc):
    i = pl.program_id(0)
    na = na_ref[0]
    tm = xb_ref.shape[0]
    tf = g_ref.shape[2]

    def weight_copies(tile, half):
        e = te_ref[tile]
        cols = pl.ds(half * tf, tf)
        return (pltpu.make_async_copy(w1_hbm.at[e, :, cols], w1_buf.at[half], wsem.at[half, 0]),
                pltpu.make_async_copy(w3_hbm.at[e, :, cols], w3_buf.at[half], wsem.at[half, 1]),
                pltpu.make_async_copy(w2_hbm.at[e, cols, :], w2_buf.at[half], wsem.at[half, 2]))

    def start(tile, half):
        for cp in weight_copies(tile, half):
            cp.start()

    def wait(half):
        for cp in weight_copies(0, half):
            cp.wait()

    @pl.when(i >= na)
    def _():
        y_ref[...] = jnp.zeros_like(y_ref)

    @pl.when(i < na)
    def _():
        first_of_expert = (i == 0) | (te_ref[i] != te_ref[jnp.maximum(i - 1, 0)])
        nxt = jnp.minimum(i + 1, na - 1)
        next_is_new_expert = (i + 1 < na) & (te_ref[nxt] != te_ref[i])

        @pl.when(i == 0)
        def _():
            start(0, 0)

        @pl.when(first_of_expert)
        def _():
            start(i, 1)
            wait(0)

        xb_ref[...] = _load_row_tiles(xs_ref, tm).astype(BF16)
        xb = xb_ref[...]
        _swiglu_gate(xb, w1_buf.at[0], w3_buf.at[0], g_ref.at[0], fc)
        acc = jnp.dot(g_ref[0], w2_buf[0].astype(BF16), preferred_element_type=F32)

        @pl.when(next_is_new_expert)
        def _():
            start(nxt, 0)

        @pl.when(first_of_expert)
        def _():
            wait(1)

        _swiglu_gate(xb, w1_buf.at[1], w3_buf.at[1], g_ref.at[1], fc)
        y = acc + jnp.dot(g_ref[1], w2_buf[1].astype(BF16), preferred_element_type=F32)
        _store_row_tiles(y_ref, y)


def experts(tile_expert, n_active, xs, w1, w3, w2, *, tm, fc):
    n_rows = xs.shape[0] // SUBLANES
    d = w1.shape[1]
    dff = w1.shape[2]
    tf = dff // 2
    return pl.pallas_call(
        functools.partial(_experts_body, fc=fc),
        grid_spec=pltpu.PrefetchScalarGridSpec(
            num_scalar_prefetch=2,
            grid=(n_rows // tm,),
            in_specs=[
                pl.BlockSpec((tm * SUBLANES, LANES), lambda i, te, na: (jnp.minimum(i, na[0] - 1), 0)),
                pl.BlockSpec(memory_space=pl.ANY),
                pl.BlockSpec(memory_space=pl.ANY),
                pl.BlockSpec(memory_space=pl.ANY),
            ],
            out_specs=pl.BlockSpec((tm * SUBLANES, LANES), lambda i, te, na: (i, 0)),
            scratch_shapes=[
                pltpu.VMEM((tm, d), BF16),
                pltpu.VMEM((2, tm, tf), BF16),
                pltpu.VMEM((2, d, tf), w1.dtype),
                pltpu.VMEM((2, d, tf), w3.dtype),
                pltpu.VMEM((2, tf, d), w2.dtype),
                pltpu.SemaphoreType.DMA((2, 3)),
            ],
        ),
        out_shape=jax.ShapeDtypeStruct((n_rows * SUBLANES, LANES), F32),
        compiler_params=_params("arbitrary", vmem_limit=EXPERTS_VMEM_LIMIT),
        name="moe_experts",
    )(tile_expert, n_active, xs, w1, w3, w2)


def _combine_body(dest_ref, x1_ref, meta_ref, p_ref, g_ref, wg_ref, wp_ref, y_ref, o_ref, buf_ref, sem,
                  *, tm, topk):
    m = pl.program_id(0)
    n_steps = pl.num_programs(0)
    n_tok = n_steps * tm

    def start_gather(tile, slot):
        def issue(r, c):
            for k in range(topk):
                d = dest_ref[k * n_tok + tile * tm + r]
                pltpu.make_async_copy(_row_tile(y_ref, d), _row_tile(buf_ref.at[slot, k], r),
                                      sem.at[slot]).start(priority=k % 2)
            return c

        lax.fori_loop(0, tm, issue, 0, unroll=ROW_DMA_UNROLL)

    @pl.when(m == 0)
    def _():
        start_gather(0, 0)

    @pl.when(m + 1 < n_steps)
    def _():
        start_gather(m + 1, (m + 1) % 2)

    slot = m % 2
    for k in range(topk):
        pltpu.make_async_copy(y_ref.at[pl.ds(0, tm * SUBLANES)], buf_ref.at[slot, k], sem.at[slot]).wait()
    meta = meta_ref[...]
    g0 = meta[:, META_G0:META_G0 + 1]
    g1 = meta[:, META_G1:META_G1 + 1]
    x2 = x1_ref[...] + (g0 * _load_row_tiles(buf_ref.at[slot, 0], tm)
                        + g1 * _load_row_tiles(buf_ref.at[slot, 1], tm))
    o_ref[...] = _ple(x2, p_ref, g_ref, wg_ref, wp_ref)


def combine(dest, x1, meta, y, p2, p_row0, g, wg, wp, *, tm, topk):
    t, d = x1.shape
    full = lambda m, dest: (0, 0)
    p_blk0 = p_row0 // tm
    return pl.pallas_call(
        functools.partial(_combine_body, tm=tm, topk=topk),
        grid_spec=pltpu.PrefetchScalarGridSpec(
            num_scalar_prefetch=1,
            grid=(t // tm,),
            in_specs=[pl.BlockSpec((tm, d), lambda m, dest: (m, 0)),
                      pl.BlockSpec((tm, LANES), lambda m, dest: (m, 0)),
                      pl.BlockSpec((tm, p2.shape[1]), lambda m, dest: (p_blk0 + m, 0)),
                      pl.BlockSpec((1, d), full), pl.BlockSpec(wg.shape, full), pl.BlockSpec(wp.shape, full),
                      pl.BlockSpec(memory_space=pl.ANY)],
            out_specs=pl.BlockSpec((tm, d), lambda m, dest: (m, 0)),
            scratch_shapes=[pltpu.VMEM((2, topk, tm * SUBLANES, LANES), F32), pltpu.SemaphoreType.DMA((2,))],
        ),
        out_shape=jax.ShapeDtypeStruct((t, d), F32),
        compiler_params=_params("arbitrary"),
        name="moe_combine",
    )(dest, x1, meta, p2, g, wg, wp, y)


def moe_ffn(hf, x1, routing, w1, w3, w2, ple_args, *, tm_rows, tm_dispatch, tm_combine):
    t, d = x1.shape
    topk = 2
    meta, tab, cnt = routing
    counts = cnt[0, :N_EXPERTS].astype(jnp.int32)
    padded = ((counts + tm_rows - 1) // tm_rows) * tm_rows
    pad_end = jnp.cumsum(padded).astype(jnp.int32)
    pad_start = pad_end - padded
    eidx = tab[META_E0:META_E1 + 1].astype(jnp.int32)
    rank = tab[META_R0:META_R1 + 1].astype(jnp.int32)
    dest = rank
    for e in range(N_EXPERTS):
        dest = dest + jnp.where(eidx == e, pad_start[e], 0)
    dest = dest.reshape(topk * t)
    n_tiles = -(-(t * topk) // tm_rows) + N_EXPERTS
    tile_start = jnp.arange(n_tiles, dtype=jnp.int32) * tm_rows
    tile_expert = jnp.minimum(jnp.sum(tile_start[:, None] >= pad_end[None, :], axis=1),
                              N_EXPERTS - 1).astype(jnp.int32)
    n_active = pad_end[N_EXPERTS - 1:] // tm_rows
    xs, w3_bf = dispatch(dest, pad_end, n_active, hf, w3, n_rows=n_tiles * tm_rows, tm=tm_dispatch, topk=topk,
                         tm_rows=tm_rows)
    y = experts(tile_expert, n_active, xs, w1, w3_bf, w2, tm=tm_rows, fc=FC_SWIGLU)
    return combine(dest, x1, meta, y, *ple_args, tm=tm_combine, topk=topk)


def _tile2(g):
    return jnp.concatenate([g, g]).reshape(1, 2 * g.shape[0])


def kernel(x, p, g_mix, w_in, g_q, g_k, conv_w, conv_b, b_i, b_f, g_h, w_oa, w_ob, w_out, g_ffn, w_d1, w_d3,
           w_d2, w_router, w_e1, w_e3, w_e2, g_ple, w_ple_gate, w_ple_proj):
    batch, seq, d = x.shape
    depth = w_in.shape[0]
    t = batch * seq
    nh = MLSTM_HEADS
    x2 = x.reshape(t, d)
    c_q, c_k, c_v = 0, 512, 1024
    c_qk, c_vm, c_om, c_i, c_f, c_ga, c_gb, c_end = 1536, 2560, 3072, 3584, 3588, 3592, 4616, 5640

    w_in_t = jnp.swapaxes(w_in, 1, 2).astype(BF16)
    for l in range(depth):
        w_gates = w_in_t[l, c_ga:c_end]
        proj, gif, gif_t = in_proj(x2, g_mix[l].reshape(1, d), w_gates, w_in_t, l, nb_cols=c_i, if_col=c_i,
                                   tm=TM_IN_PROJ, tn=TN_IN_PROJ)

        ya = moba(proj, _tile2(g_q[l]), _tile2(g_k[l]), batch=batch, seq=seq)

        bias = jnp.concatenate([b_i[l], b_f[l]])
        bias_row = jnp.pad(bias, (0, LANES - 2 * nh)).reshape(1, LANES)
        bias_col = bias.reshape(2 * nh, 1)
        yb = mlstm(proj, gif, gif_t, bias_row, bias_col, conv_w[l], conv_b[l].reshape(1, -1),
                   g_h[l].reshape(1, -1), batch=batch, seq=seq, chunk=MLSTM_CHUNK)

        j = l // 2
        moe = l % 2 == 1
        wr = jnp.pad(w_router[j], ((0, 0), (0, LANES - N_EXPERTS))) if moe else None
        outs = merge(ya, yb, proj, x2, w_oa[l].astype(BF16), w_ob[l].astype(BF16), w_out[l].astype(BF16),
                     g_ffn[l].reshape(1, d), wr, tm=TM_MERGE)
        ple_args = (p.reshape(depth * t, -1), l * t, g_ple[l].reshape(1, d), w_ple_gate[l].astype(BF16),
                    w_ple_proj[l].astype(BF16))
        if moe:
            x1, hf, *routing = outs
            x2 = moe_ffn(hf, x1, routing, w_e1[j], w_e3[j], w_e2[j],
                         ple_args, tm_rows=TM_EXPERT_ROWS, tm_dispatch=TM_DISPATCH, tm_combine=TM_COMBINE)
        else:
            x1, hf = outs
            x2 = dense_ffn(hf, x1, w_d1[j].astype(BF16), w_d3[j].astype(BF16), w_d2[j].astype(BF16),
                           *ple_args, tm=TM_DENSE, fc=FC_SWIGLU)
    return x2.reshape(batch, seq, d)
```
